```python
import math
import jax
import jax.numpy as jnp
from jax import lax
import numpy as np

D_MODEL = 1024
BATCH = 4
SEQ = 8192
DEPTH = 1

GRID_W = 64
CTX_LEN = 256
S5_WIDTH = 256
S5_GROUP = 16
S5_GROUPS = S5_WIDTH // S5_GROUP
S5_STATE = 64
DT_MIN = 1e-3
DT_MAX = 1e-1
HG_HEADS = 6
HG_DK = 128
HG_DV = 128
HG_WIDTH = HG_HEADS * HG_DK
HG_CHUNK = 64
N_BRANCH = 2
IN_WIDTH = S5_WIDTH + 5 * HG_WIDTH + N_BRANCH * D_MODEL
N_EXPERTS = 64
ROUTE_GROUPS = 8
TOPK_GROUPS = 4
TOP_K = 8
EXPERT_HIDDEN = 256
SHARED_HIDDEN = 256
ROUTED_SCALE = 2.5
MOE_BLOCK = 128
EPS = 1e-6

kernel_name = 'hybrid_s5_hgrn2_moe_prefix_block'


def rms_norm(t, g):
    tf = t.astype(jnp.float32)
    y = tf * lax.rsqrt(jnp.mean(tf * tf, axis=-1, keepdims=True) + EPS)
    return (y * g.astype(jnp.float32)).astype(t.dtype)


def to_colmajor(t, rows):
    b, l, f = t.shape
    return t.reshape(b, rows, GRID_W, f).transpose(0, 2, 1, 3).reshape(b, l, f)


def from_colmajor(t, rows):
    b, l, f = t.shape
    return t.reshape(b, GRID_W, rows, f).transpose(0, 2, 1, 3).reshape(b, l, f)


def split_in(z):
    sizes = (S5_WIDTH,) + (HG_WIDTH,) * 5 + (D_MODEL,) * N_BRANCH
    idx = [int(v) for v in np.cumsum(sizes)[:-1]]
    return jnp.split(z, idx, axis=-1)


def s5_discretise(lam_re, lam_im, log_dt, b_re, b_im):
    lam = lax.complex(jnp.minimum(lam_re.astype(jnp.float32), -1e-4), lam_im.astype(jnp.float32))
    dt = jnp.exp(log_dt.astype(jnp.float32))[:, None]
    lam_dt = lam * dt
    b = lax.complex(b_re.astype(jnp.float32), b_im.astype(jnp.float32))
    b_bar = ((jnp.exp(lam_dt) - 1.0) / lam)[..., None] * b
    return lam_dt, b_bar


def s5_prepare(lp):
    lam_f, bbar_f = s5_discretise(lp['s5_lam_re'][0], lp['s5_lam_im'][0], lp['s5_log_dt'][0],
                                  lp['s5_b_re'][0], lp['s5_b_im'][0])
    lam_b, bbar_b = s5_discretise(lp['s5_lam_re'][1], lp['s5_lam_im'][1], lp['s5_log_dt'][1],
                                  lp['s5_b_re'][1], lp['s5_b_im'][1])
    return {'lam_f': lam_f, 'bbar_f': bbar_f, 'lam_b': lam_b, 'bbar_b': bbar_b,
            'c': lax.complex(lp['s5_c_re'].astype(jnp.float32), lp['s5_c_im'].astype(jnp.float32)),
            'd': lp['s5_d'].astype(jnp.float32).reshape(S5_GROUPS, S5_GROUP)}


def s5_drive(u, b_bar):
    return jnp.einsum('blgc,gpc->blgp', u.astype(jnp.complex64), b_bar)


def _linear_recurrence_op(e1, e2):
    a1, b1 = e1
    a2, b2 = e2
    return a2 * a1, a2 * b1 + b2


def s5_scan(bu, lam_dt, h0, reverse):
    l = bu.shape[1]
    a = jnp.broadcast_to(jnp.exp(lam_dt), bu.shape)
    _, states = lax.associative_scan(_linear_recurrence_op, (a, bu), axis=1, reverse=reverse)
    pos = jnp.arange(l, dtype=jnp.float32)
    steps = (l - pos) if reverse else (pos + 1.0)
    return states + jnp.exp(steps[:, None, None] * lam_dt)[None] * h0[:, None]


def s5_final_state(bu, lam_dt, reverse):
    l = bu.shape[1]
    pos = jnp.arange(l, dtype=jnp.float32)
    steps = pos if reverse else (l - 1.0) - pos
    decay = jnp.exp(steps[:, None, None] * lam_dt)
    return jnp.einsum('lgp,blgp->bgp', decay, bu)


def hgrn_forget(zf, lb):
    f = lb + (1.0 - lb) * jax.nn.sigmoid(zf)
    return jnp.log(f), (1.0 - lb) * jax.nn.sigmoid(-zf)


def split_heads(t):
    b, l, _ = t.shape
    return t.reshape(b, l, HG_HEADS, -1).transpose(0, 2, 1, 3)


def merge_heads(t):
    b, h, l, d = t.shape
    return t.transpose(0, 2, 1, 3).reshape(b, l, h * d)


def gla_chunkwise(q, k, v, log_f, s0):
    b, h, l, dk = q.shape
    n = l // HG_CHUNK
    rs = lambda t: t.reshape(b, h, n, HG_CHUNK, t.shape[-1])
    q, k, v, log_f = rs(q), rs(k), rs(v), rs(log_f)
    cum = jnp.cumsum(log_f, axis=3)
    ref = cum[:, :, :, HG_CHUNK // 2:HG_CHUNK // 2 + 1]
    qi = q * jnp.exp(cum - ref)
    ki = k * jnp.exp(ref - cum)
    mask = jnp.tril(jnp.ones((HG_CHUNK, HG_CHUNK), dtype=bool))
    scores = jnp.where(mask, jnp.einsum('bhnti,bhnsi->bhnts', qi, ki), 0.0)
    o_intra = jnp.einsum('bhnts,bhnsv->bhntv', scores, v)
    total = cum[:, :, :, -1]
    q_in = q * jnp.exp(cum)
    kv = jnp.einsum('bhnsi,bhnsv->bhniv', k * jnp.exp(total[:, :, :, None] - cum), v)

    def step(s, inp):
        q_c, dec, kv_c = inp
        o = jnp.einsum('bhti,bhiv->bhtv', q_c, s)
        return dec[..., None] * s + kv_c, o

    _, o_inter = lax.scan(step, s0, (jnp.moveaxis(q_in, 2, 0), jnp.moveaxis(jnp.exp(total), 2, 0),
                                     jnp.moveaxis(kv, 2, 0)))
    o = o_intra + jnp.moveaxis(o_inter, 0, 2)
    return o.reshape(b, h, l, v.shape[-1])


def gla_final_state(k, v, log_f):
    cum = jnp.cumsum(log_f, axis=2)
    return jnp.einsum('bhli,bhlv->bhiv', k * jnp.exp(cum[:, :, -1:] - cum), v)


def context_states(zc, s5, lb):
    u, _, ff, fb, i, _, _, _ = split_in(zc)
    b, l, _ = zc.shape
    uc = u.astype(jnp.float32).reshape(b, l, S5_GROUPS, S5_GROUP)
    h_f = s5_final_state(s5_drive(uc, s5['bbar_f']), s5['lam_f'], reverse=False)
    h_b = s5_final_state(s5_drive(uc, s5['bbar_b']), s5['lam_b'], reverse=True)
    logf_f, k_f = hgrn_forget(ff.astype(jnp.float32), lb)
    logf_b, k_b = hgrn_forget(fb.astype(jnp.float32), lb)
    vh = split_heads(i.astype(jnp.float32))
    s_f = gla_final_state(split_heads(k_f), vh, split_heads(logf_f))
    flip = lambda t: jnp.flip(t, axis=2)
    s_b = gla_final_state(flip(split_heads(k_b)), flip(vh), flip(split_heads(logf_b)))
    return h_f, h_b, s_f, s_b


def token_mixer(z, states, s5, lp, lb, rows):
    h_f, h_b, s_f, s_b = states
    u, q, ff, fb, i, g_out, gate_a, gate_b = split_in(z)
    b, l, _ = z.shape
    uf = u.astype(jnp.float32).reshape(b, l, S5_GROUPS, S5_GROUP)
    xs = (s5_scan(s5_drive(uf, s5['bbar_f']), s5['lam_f'], h_f, reverse=False)
          + s5_scan(s5_drive(uf, s5['bbar_b']), s5['lam_b'], h_b, reverse=True))
    y = jnp.einsum('blgp,gcp->blgc', xs, s5['c']).real + s5['d'] * uf
    y = jax.nn.gelu(y.reshape(b, l, S5_WIDTH)).astype(z.dtype)
    y_a = y * jax.nn.sigmoid(y @ lp['s5_w_glu'])
    q, ff, fb, i = [t.astype(jnp.float32) for t in (q, ff, fb, i)]
    if rows is not None:
        q, ff, fb, i = [to_colmajor(t, rows) for t in (q, ff, fb, i)]
    logf_f, k_f = hgrn_forget(ff, lb)
    logf_b, k_b = hgrn_forget(fb, lb)
    qh, vh = split_heads(q), split_heads(i)
    flip = lambda t: jnp.flip(t, axis=2)
    o = gla_chunkwise(qh, split_heads(k_f), vh, split_heads(logf_f), s_f)
    o = o + flip(gla_chunkwise(flip(qh), flip(split_heads(k_b)), flip(vh), flip(split_heads(logf_b)), s_b))
    o = merge_heads(rms_norm(o, lp['hg_norm_g']))
    if rows is not None:
        o = from_colmajor(o, rows)
    y_b = o.astype(z.dtype) * jax.nn.silu(g_out)
    merged = jax.nn.sigmoid(gate_a) * (y_a @ lp['p_a']) + jax.nn.sigmoid(gate_b) * (y_b @ lp['p_b'])
    return merged @ lp['w_out']


def moe_ffn(h, lp):
    b, l, dm = h.shape
    xt = h.reshape(-1, dm)
    n_tok = xt.shape[0]
    scores = jax.nn.sigmoid((xt @ lp['moe_w_router']).astype(jnp.float32))
    biased = scores + lp['moe_b_router'].astype(jnp.float32)
    per_group = N_EXPERTS // ROUTE_GROUPS
    grp_score = lax.top_k(biased.reshape(n_tok, ROUTE_GROUPS, per_group), 2)[0].sum(-1)
    _, top_groups = lax.top_k(grp_score, TOPK_GROUPS)
    group_ok = jnp.any(top_groups[:, :, None] == jnp.arange(ROUTE_GROUPS)[None, None, :], axis=1)
    expert_ok = jnp.repeat(group_ok, per_group, axis=1)
    _, eidx = lax.top_k(jnp.where(expert_ok, biased, -jnp.inf), TOP_K)
    wts = jnp.take_along_axis(scores, eidx, axis=1)
    wts = wts / jnp.sum(wts, axis=-1, keepdims=True) * ROUTED_SCALE
    n_assign = n_tok * TOP_K
    e_flat = eidx.reshape(-1).astype(jnp.int32)
    tok_flat = jnp.arange(n_assign, dtype=jnp.int32) // TOP_K
    w_flat = wts.reshape(-1)
    order = jnp.argsort(e_flat)
    e_sorted = e_flat[order]
    counts = jnp.bincount(e_flat, length=N_EXPERTS)
    starts = jnp.cumsum(counts) - counts
    padded = (counts + MOE_BLOCK - 1) // MOE_BLOCK * MOE_BLOCK
    pends = jnp.cumsum(padded)
    pstarts = pends - padded
    dest = pstarts[e_sorted] + jnp.arange(n_assign, dtype=jnp.int32) - starts[e_sorted]
    n_blocks = (n_assign + N_EXPERTS * (MOE_BLOCK - 1) + MOE_BLOCK - 1) // MOE_BLOCK
    cap = n_blocks * MOE_BLOCK
    row_tok = jnp.full((cap,), n_tok, jnp.int32).at[dest].set(tok_flat[order])
    row_w = jnp.zeros((cap,), jnp.float32).at[dest].set(w_flat[order])
    block_e = jnp.minimum(jnp.searchsorted(pends, jnp.arange(n_blocks) * MOE_BLOCK, side='right'),
                          N_EXPERTS - 1).astype(jnp.int32)
    x_pad = jnp.concatenate([xt, jnp.zeros((1, dm), xt.dtype)], axis=0)
    w1, w3, w2 = lp['moe_w1'], lp['moe_w3'], lp['moe_w2']

    def expert_block(acc, blk):
        toks, wr, e = blk
        xb = x_pad[toks]
        hid = jax.nn.silu(xb @ w1[e]) * (xb @ w3[e])
        return acc.at[toks].add((hid @ w2[e]) * wr[:, None].astype(xb.dtype)), None

    acc, _ = lax.scan(expert_block, jnp.zeros((n_tok + 1, dm), xt.dtype),
                      (row_tok.reshape(n_blocks, MOE_BLOCK), row_w.reshape(n_blocks, MOE_BLOCK), block_e))
    shared = (jax.nn.silu(xt @ lp['moe_ws1']) * (xt @ lp['moe_ws3'])) @ lp['moe_ws2']
    return (acc[:n_tok] + shared).reshape(b, l, dm)


def layer_forward(x, ctx, c, c_ctx, lp, lb, rows, last):
    mod = (jax.nn.silu(c) @ lp['w_ada'] + lp['b_ada'])[:, None, :]
    mod_c = jax.nn.silu(c_ctx) @ lp['w_ada'] + lp['b_ada']
    sh1, sc1, g1, sh2, sc2, g2 = jnp.split(mod, 6, axis=-1)
    csh1, csc1, cg1, csh2, csc2, cg2 = jnp.split(mod_c, 6, axis=-1)
    s5 = s5_prepare(lp)
    h = rms_norm(x, lp['norm1_g']) * (1.0 + sc1) + sh1
    hc = rms_norm(ctx, lp['norm1_g']) * (1.0 + csc1) + csh1
    z = h @ lp['w_in']
    zc = hc @ lp['w_in']
    states = context_states(zc, s5, lb)
    x = x + g1 * token_mixer(z, states, s5, lp, lb, rows)
    x = x + g2 * moe_ffn(rms_norm(x, lp['norm2_g']) * (1.0 + sc2) + sh2, lp)
    if not last:
        b = ctx.shape[0]
        zero_states = (jnp.zeros((b, S5_GROUPS, S5_STATE), jnp.complex64),
                       jnp.zeros((b, S5_GROUPS, S5_STATE), jnp.complex64),
                       jnp.zeros((b, HG_HEADS, HG_DK, HG_DV), jnp.float32),
                       jnp.zeros((b, HG_HEADS, HG_DK, HG_DV), jnp.float32))
        ctx = ctx + cg1 * token_mixer(zc, zero_states, s5, lp, lb, None)
        ctx = ctx + cg2 * moe_ffn(rms_norm(ctx, lp['norm2_g']) * (1.0 + csc2) + csh2, lp)
    return x, ctx


def setup_inputs(seed: int = 0) -> dict:
    key = jax.random.key(seed)
    ks = iter(jax.random.split(key, 40))

    def nrm(shape, scale):
        return jax.random.normal(next(ks), shape, jnp.float32) * scale

    d = D_MODEL
    g, cc, p = S5_GROUPS, S5_GROUP, S5_STATE
    e, f = N_EXPERTS, EXPERT_HIDDEN
    n_idx = jnp.arange(p, dtype=jnp.float32)
    return {
        'x': nrm((BATCH, SEQ, d), 1.0),
        'c': nrm((BATCH, d), 1.0),
        'ctx': nrm((BATCH, CTX_LEN, d), 1.0),
        'c_ctx': nrm((d,), 1.0),
        'w_ada': nrm((DEPTH, d, 6 * d), 0.5 * d ** -0.5),
        'b_ada': nrm((DEPTH, 6 * d), 0.01),
        'norm1_g': 1.0 + nrm((DEPTH, d), 0.01),
        'norm2_g': 1.0 + nrm((DEPTH, d), 0.01),
        'w_in': nrm((DEPTH, d, IN_WIDTH), d ** -0.5),
        's5_lam_re': -0.5 + nrm((DEPTH, 2, g, p), 0.01),
        's5_lam_im': math.pi * n_idx + nrm((DEPTH, 2, g, p), 0.01),
        's5_log_dt': jax.random.uniform(next(ks), (DEPTH, 2, g), jnp.float32,
                                        math.log(DT_MIN), math.log(DT_MAX)),
        's5_b_re': nrm((DEPTH, 2, g, p, cc), (2 * cc) ** -0.5),
        's5_b_im': nrm((DEPTH, 2, g, p, cc), (2 * cc) ** -0.5),
        's5_c_re': nrm((DEPTH, g, cc, p), (2 * p) ** -0.5),
        's5_c_im': nrm((DEPTH, g, cc, p), (2 * p) ** -0.5),
        's5_d': nrm((DEPTH, S5_WIDTH), 1.0),
        's5_w_glu': nrm((DEPTH, S5_WIDTH, S5_WIDTH), S5_WIDTH ** -0.5),
        'hg_lb_logits': nrm((DEPTH + 1, HG_WIDTH), 0.1),
        'hg_norm_g': 1.0 + nrm((DEPTH, HG_DV), 0.01),
        'p_a': nrm((DEPTH, S5_WIDTH, d), S5_WIDTH ** -0.5),
        'p_b': nrm((DEPTH, HG_WIDTH, d), HG_WIDTH ** -0.5),
        'w_out': nrm((DEPTH, d, d), d ** -0.5),
        'moe_w_router': nrm((DEPTH, d, e), d ** -0.5),
        'moe_b_router': nrm((DEPTH, e), 0.01),
        'moe_w1': nrm((DEPTH, e, d, f), d ** -0.5),
        'moe_w3': nrm((DEPTH, e, d, f), d ** -0.5),
        'moe_w2': nrm((DEPTH, e, f, d), f ** -0.5),
        'moe_ws1': nrm((DEPTH, d, SHARED_HIDDEN), d ** -0.5),
        'moe_ws3': nrm((DEPTH, d, SHARED_HIDDEN), d ** -0.5),
        'moe_ws2': nrm((DEPTH, SHARED_HIDDEN, d), SHARED_HIDDEN ** -0.5),
        'final_norm_g': 1.0 + nrm((d,), 0.01),
    }


def reference(x, c, ctx, c_ctx, w_ada, b_ada, norm1_g, norm2_g, w_in, s5_lam_re, s5_lam_im,
              s5_log_dt, s5_b_re, s5_b_im, s5_c_re, s5_c_im, s5_d, s5_w_glu, hg_lb_logits,
              hg_norm_g, p_a, p_b, w_out, moe_w_router, moe_b_router, moe_w1, moe_w3, moe_w2,
              moe_ws1, moe_ws3, moe_ws2, final_norm_g):
    rows = x.shape[1] // GRID_W
    lb_all = jnp.cumsum(jax.nn.softmax(hg_lb_logits.astype(jnp.float32), axis=0), axis=0)
    for layer in range(DEPTH):
        lp = {'w_ada': w_ada[layer], 'b_ada': b_ada[layer], 'norm1_g': norm1_g[layer],
              'norm2_g': norm2_g[layer], 'w_in': w_in[layer],
              's5_lam_re': s5_lam_re[layer], 's5_lam_im': s5_lam_im[layer], 's5_log_dt': s5_log_dt[layer],
              's5_b_re': s5_b_re[layer], 's5_b_im': s5_b_im[layer], 's5_c_re': s5_c_re[layer],
              's5_c_im': s5_c_im[layer], 's5_d': s5_d[layer], 's5_w_glu': s5_w_glu[layer],
              'hg_norm_g': hg_norm_g[layer], 'p_a': p_a[layer], 'p_b': p_b[layer], 'w_out': w_out[layer],
              'moe_w_router': moe_w_router[layer], 'moe_b_router': moe_b_router[layer],
              'moe_w1': moe_w1[layer], 'moe_w3': moe_w3[layer], 'moe_w2': moe_w2[layer],
              'moe_ws1': moe_ws1[layer], 'moe_ws3': moe_ws3[layer], 'moe_ws2': moe_ws2[layer]}
        x, ctx = layer_forward(x, ctx, c, c_ctx, lp, lb_all[layer], rows, layer == DEPTH - 1)
    return rms_norm(x, final_norm_g)
```

```python
import functools
import math

import jax
import jax.numpy as jnp
from jax import lax
from jax.experimental import pallas as pl
from jax.experimental.pallas import tpu as pltpu

F32 = jnp.float32
BF16 = jnp.bfloat16

GRID_W = 64
S5_WIDTH = 256
S5_GROUP = 16
S5_GROUPS = 16
S5_STATE = 64
HG_HEADS = 6
HG_DK = 128
HG_WIDTH = HG_HEADS * HG_DK
N_EXPERTS = 64
ROUTE_GROUPS = 8
TOPK_GROUPS = 4
TOP_K = 8
ROUTED_SCALE = 2.5
EPS = 1e-6

S5_T = 16
HG_CHUNK = 64
VMEM_LIMIT = 56 * 1024 * 1024

_NT = (((1,), (1,)), ((), ()))
_TN = (((0,), (0,)), ((), ()))


def _params(*sem):
    return pltpu.CompilerParams(dimension_semantics=sem, vmem_limit_bytes=VMEM_LIMIT)


def _dot(a, b):
    return jnp.dot(a, b, preferred_element_type=F32)


def _sigmoid(x):
    return 1.0 / (1.0 + jnp.exp(-x))


def _ada_kernel(c_ref, w_ref, b_ref, o_ref):
    c = c_ref[...]
    s = (c * _sigmoid(c)).astype(BF16)
    o_ref[...] = _dot(s, w_ref[...].astype(BF16)) + b_ref[...]


def _ada(c8, w_ada, b_ada):
    d, n = w_ada.shape
    tn = 1536
    return pl.pallas_call(
        _ada_kernel,
        out_shape=jax.ShapeDtypeStruct((8, n), F32),
        grid=(n // tn,),
        in_specs=[pl.BlockSpec((8, d), lambda j: (0, 0)),
                  pl.BlockSpec((d, tn), lambda j: (0, j)),
                  pl.BlockSpec((1, tn), lambda j: (0, j))],
        out_specs=pl.BlockSpec((8, tn), lambda j: (0, j)),
        compiler_params=_params("arbitrary"),
        name="ada_mod",
    )(c8, w_ada, b_ada.reshape(1, n))


_IN_PIECES = (("u", 0, 256, BF16), ("q", 256, 768, BF16), ("ff", 1024, 768, F32),
              ("fb", 1792, 768, F32), ("i", 2560, 768, BF16), ("go", 3328, 768, BF16),
              ("ga", 4096, 1024, BF16), ("gb", 5120, 1024, BF16))


def _inproj_kernel(x_ref, sc_ref, sh_ref, g_ref, w_ref, *o_refs):
    x = x_ref[...]
    y = x * lax.rsqrt(jnp.mean(x * x, axis=-1, keepdims=True) + EPS) * g_ref[...]
    h = (y * (1.0 + sc_ref[0]) + sh_ref[0]).astype(BF16)
    for (_, a, wd, _), o_ref in zip(_IN_PIECES, o_refs):
        o_ref[...] = _dot(h, w_ref[:, a:a + wd]).astype(o_ref.dtype)


def _inproj(x2d, sc, sh, g, w_bf16, rows_per_mod, tm):
    n, d = x2d.shape
    per = rows_per_mod // tm
    mod_map = (lambda i: (i // per, 0, 0)) if sc.shape[0] > 1 else (lambda i: (0, 0, 0))
    return pl.pallas_call(
        _inproj_kernel,
        out_shape=[jax.ShapeDtypeStruct((n, wd), dt) for (_, _, wd, dt) in _IN_PIECES],
        grid=(n // tm,),
        in_specs=[pl.BlockSpec((tm, d), lambda i: (i, 0)),
                  pl.BlockSpec((1, 1, d), mod_map),
                  pl.BlockSpec((1, 1, d), mod_map),
                  pl.BlockSpec((1, d), lambda i: (0, 0)),
                  pl.BlockSpec(w_bf16.shape, lambda i: (0, 0))],
        out_specs=[pl.BlockSpec((tm, wd), lambda i: (i, 0)) for (_, _, wd, _) in _IN_PIECES],
        compiler_params=_params("arbitrary"),
        name="in_proj",
    )(x2d, sc, sh, g.reshape(1, d), w_bf16)


def _hgrn_gates(zf, lb):
    sig = _sigmoid(zf)
    logf = jnp.log(lb + (1.0 - lb) * sig)
    k = (1.0 - lb) * (1.0 - sig)
    return logf, k


def _chunk_cumsum(cs, logf):
    hi = logf.astype(BF16)
    lo = (logf - hi.astype(F32)).astype(BF16)
    return _dot(cs, hi) + _dot(cs, lo)


def _hgrn_state_step(zf, v, lb, st, cs, reverse):
    logf, k = _hgrn_gates(zf, lb)
    cum = _chunk_cumsum(cs, logf)
    t = 0 if reverse else HG_CHUNK - 1
    total = cum[t:t + 1, :]
    kdec = (k * jnp.exp(total - cum)).astype(BF16)
    st_new = st * jnp.exp(total) + lax.dot_general(v.astype(BF16), kdec, _TN, preferred_element_type=F32)
    return cum, k, st_new


def _hgrn_chunk(q, zf, v, lb, st, mask, cs, reverse):
    cum, k, st_new = _hgrn_state_step(zf, v, lb, st, cs, reverse)
    r = HG_CHUNK // 2 - 1 if reverse else HG_CHUNK // 2
    ref = cum[r:r + 1, :]
    qi = (q * jnp.exp(cum - ref)).astype(BF16)
    ki = (k * jnp.exp(ref - cum)).astype(BF16)
    s = lax.dot_general(qi, ki, _NT, preferred_element_type=F32)
    s = jnp.where(mask, s, 0.0)
    o = _dot(s.astype(BF16), v.astype(BF16))
    q_in = (q * jnp.exp(cum)).astype(BF16)
    o = o + lax.dot_general(q_in, st.astype(BF16), _NT, preferred_element_type=F32)
    return o, st_new


def _hgrn_kernel(*refs, reverse, final, n_ctx_chunks):
    if final:
        q_ref, f_ref, v_ref, cf_ref, cv_ref, lb_ref, of_ref, g_ref, o_ref, st_ref = refs
    else:
        q_ref, f_ref, v_ref, cf_ref, cv_ref, lb_ref, o_ref, st_ref = refs
    c_len = HG_CHUNK
    row = lax.broadcasted_iota(jnp.int32, (c_len, c_len), 0)
    col = lax.broadcasted_iota(jnp.int32, (c_len, c_len), 1)
    mask = (col >= row) if reverse else (col <= row)
    cs = jnp.where(mask, 1.0, 0.0).astype(BF16)

    @pl.when(pl.program_id(1) == 0)
    def _():
        order = range(n_ctx_chunks - 1, -1, -1) if reverse else range(n_ctx_chunks)
        for h in range(HG_HEADS):
            cols = slice(h * HG_DK, (h + 1) * HG_DK)
            st = jnp.zeros((HG_DK, HG_DK), F32)
            for c in order:
                rows = slice(c * c_len, (c + 1) * c_len)
                _, _, st = _hgrn_state_step(cf_ref[0, rows, cols], cv_ref[0, rows, cols].astype(F32),
                                            lb_ref[:, cols], st, cs, reverse)
            st_ref[h] = st

    n_chunks = q_ref.shape[1] // c_len
    order = range(n_chunks - 1, -1, -1) if reverse else range(n_chunks)
    for h in range(HG_HEADS):
        cols = slice(h * HG_DK, (h + 1) * HG_DK)
        st = st_ref[h]
        for c in order:
            rows = slice(c * c_len, (c + 1) * c_len)
            o, st = _hgrn_chunk(q_ref[0, rows, cols].astype(F32), f_ref[0, rows, cols],
                                v_ref[0, rows, cols].astype(F32), lb_ref[:, cols], st, mask, cs, reverse)
            if final:
                o = o + of_ref[0, rows, cols]
                o = o * lax.rsqrt(jnp.mean(o * o, axis=-1, keepdims=True) + EPS) * g_ref[...]
            o_ref[0, rows, cols] = o.astype(o_ref.dtype)
        st_ref[h] = st


def _hgrn_pass(q, f, v, cf, cv, lb, o_prev, g, *, reverse):
    b, rows, _ = q.shape
    nw = GRID_W
    final = o_prev is not None
    wmap = (lambda bi, w: (bi, 0, nw - 1 - w)) if reverse else (lambda bi, w: (bi, 0, w))
    blk = pl.BlockSpec((1, rows, HG_WIDTH), wmap)
    cblk = pl.BlockSpec((1, cf.shape[1], HG_WIDTH), lambda bi, w: (bi, 0, 0))
    in_specs = [blk, blk, blk, cblk, cblk, pl.BlockSpec((1, HG_WIDTH), lambda bi, w: (0, 0))]
    args = [q, f, v, cf, cv, lb]
    if final:
        in_specs += [blk, pl.BlockSpec((1, HG_DK), lambda bi, w: (0, 0))]
        args += [o_prev, g]
    return pl.pallas_call(
        functools.partial(_hgrn_kernel, reverse=reverse, final=final, n_ctx_chunks=cf.shape[1] // HG_CHUNK),
        out_shape=jax.ShapeDtypeStruct(q.shape, BF16 if final else F32),
        grid=(b, nw),
        in_specs=in_specs,
        out_specs=blk,
        scratch_shapes=[pltpu.VMEM((HG_HEADS, HG_DK, HG_DK), F32)],
        compiler_params=_params("arbitrary", "arbitrary"),
        name="hgrn_bwd" if reverse else "hgrn_fwd",
    )(*args)


def _s5_weights(lam_re, lam_im, log_dt, b_re, b_im, c_re, c_im):
    hp = lax.Precision.HIGHEST
    g, p, cc, t = S5_GROUPS, S5_STATE, S5_GROUP, S5_T
    lre = jnp.minimum(lam_re.astype(F32), -1e-4)
    lim = lam_im.astype(F32)
    dt = jnp.exp(log_dt.astype(F32))[..., None]
    ks = jnp.arange(t + 1, dtype=F32)[:, None, None, None]
    mag = jnp.exp(ks * (lre * dt)[None])
    pw_re = mag * jnp.cos(ks * (lim * dt)[None])
    pw_im = mag * jnp.sin(ks * (lim * dt)[None])
    nr, ni = pw_re[1] - 1.0, pw_im[1]
    den = lre * lre + lim * lim
    cf_re = (nr * lre + ni * lim) / den
    cf_im = (ni * lre - nr * lim) / den
    bb_re = cf_re[..., None] * b_re - cf_im[..., None] * b_im
    bb_im = cf_re[..., None] * b_im + cf_im[..., None] * b_re
    cre, cim = c_re.astype(F32), c_im.astype(F32)
    eye = jnp.eye(g, dtype=F32)

    cp_re = cre[None, None, :, :, :] * pw_re[:t, :, :, None, :] - cim[None, None] * pw_im[:t, :, :, None, :]
    cp_im = cre[None, None, :, :, :] * pw_im[:t, :, :, None, :] + cim[None, None] * pw_re[:t, :, :, None, :]
    kk = (jnp.einsum("kdgop,dgpi->dkgoi", cp_re, bb_re, precision=hp)
          - jnp.einsum("kdgop,dgpi->dkgoi", cp_im, bb_im, precision=hp))
    kf, kb = kk[0], kk[1]
    lag = jnp.arange(t)[None, :] - jnp.arange(t)[:, None]
    kall = jnp.concatenate([kb[:0:-1], (kf[0] + kb[0])[None], kf[1:]], axis=0)
    m = kall[lag + t - 1]
    m_full = jnp.einsum("stgoi,gh->sgitho", m, eye).reshape(t * S5_WIDTH, t * S5_WIDTH)

    def in_to_state(pre, pim, bre, bim):
        xre = pre[..., None] * bre[None] - pim[..., None] * bim[None]
        xim = pre[..., None] * bim[None] + pim[..., None] * bre[None]
        x = jnp.stack([xre, xim], axis=0)
        return jnp.einsum("rsgpi,gh->sgirhp", x, eye).reshape(t * S5_WIDTH, 2 * g * p)

    w_in_f = in_to_state(pw_re[t - 1::-1, 0][:t], pw_im[t - 1::-1, 0][:t], bb_re[0], bb_im[0])
    w_in_b = in_to_state(pw_re[:t, 1], pw_im[:t, 1], bb_re[1], bb_im[1])

    def state_to_out(pre, pim):
        are = cre[None] * pre[:, :, None, :] - cim[None] * pim[:, :, None, :]
        aim = cre[None] * pim[:, :, None, :] + cim[None] * pre[:, :, None, :]
        a = jnp.stack([are, -aim], axis=0)
        return jnp.einsum("rtgop,gh->rgptho", a, eye).reshape(2 * g * p, t * S5_WIDTH)

    w_out_f = state_to_out(pw_re[1:, 0], pw_im[1:, 0])
    w_out_b = state_to_out(pw_re[t:0:-1, 1], pw_im[t:0:-1, 1])

    w1 = jnp.concatenate([m_full, w_in_f, w_in_b], axis=1).astype(BF16)
    w2 = jnp.concatenate([w_out_f, w_out_b], axis=0).astype(BF16)
    a = jnp.concatenate([pw_re[t].reshape(2, g * p), pw_im[t].reshape(2, g * p)], axis=1)
    return w1, w2, a


def _mm_kernel(a_ref, b_ref, o_ref):
    o_ref[...] = _dot(a_ref[...], b_ref[...]).astype(o_ref.dtype)


def _matmul(a, b, tm, tn, name):
    m, k = a.shape
    n = b.shape[1]
    return pl.pallas_call(
        _mm_kernel,
        out_shape=jax.ShapeDtypeStruct((m, n), F32),
        grid=(n // tn, m // tm),
        in_specs=[pl.BlockSpec((tm, k), lambda j, i: (i, 0)),
                  pl.BlockSpec((k, tn), lambda j, i: (0, j))],
        out_specs=pl.BlockSpec((tm, tn), lambda j, i: (i, j)),
        compiler_params=_params("arbitrary", "arbitrary"),
        name=name,
    )(a, b)


def _s5_scan_kernel(ere_ref, eim_ref, a_ref, hre_ref, him_ref, *, nb, rows_in, rows_out, n_ctx, reverse):
    are, aim = a_ref[0:1, :], a_ref[1:2, :]
    zero = jnp.zeros_like(are)

    def step(src, carry, store):
        new = []
        for bi in range(nb):
            hre, him = carry[2 * bi], carry[2 * bi + 1]
            if store:
                hre_ref[pl.ds(bi * rows_out + src - n_ctx, 1), :] = hre
                him_ref[pl.ds(bi * rows_out + src - n_ctx, 1), :] = him
            ere = ere_ref[pl.ds(bi * rows_in + src, 1), :]
            eim = eim_ref[pl.ds(bi * rows_in + src, 1), :]
            new += [are * hre - aim * him + ere, are * him + aim * hre + eim]
        return tuple(new)

    ctx_src = (lambda s: rows_in - 1 - s) if reverse else (lambda s: s)
    lat_src = (lambda s: n_ctx + rows_out - 1 - s) if reverse else (lambda s: n_ctx + s)
    carry = lax.fori_loop(0, n_ctx, lambda s, c: step(ctx_src(s), c, False), tuple([zero] * (2 * nb)))
    lax.fori_loop(0, rows_out, lambda s, c: step(lat_src(s), c, True), carry)


def _s5_scan(e, a, nb, rows_in, rows_out, n_ctx, col0, reverse):
    tc = 256
    nct = (S5_GROUPS * S5_STATE) // tc
    re0, im0 = col0 // tc, col0 // tc + nct
    a2 = jnp.stack([a[: S5_GROUPS * S5_STATE], a[S5_GROUPS * S5_STATE:]], axis=0)
    hre, him = pl.pallas_call(
        functools.partial(_s5_scan_kernel, nb=nb, rows_in=rows_in, rows_out=rows_out, n_ctx=n_ctx, reverse=reverse),
        out_shape=[jax.ShapeDtypeStruct((nb * rows_out, S5_GROUPS * S5_STATE), F32)] * 2,
        grid=(nct,),
        in_specs=[pl.BlockSpec((nb * rows_in, tc), lambda j: (0, re0 + j)),
                  pl.BlockSpec((nb * rows_in, tc), lambda j: (0, im0 + j)),
                  pl.BlockSpec((2, tc), lambda j: (0, j))],
        out_specs=[pl.BlockSpec((nb * rows_out, tc), lambda j: (0, j))] * 2,
        compiler_params=_params("arbitrary"),
        name="s5_scan_bwd" if reverse else "s5_scan_fwd",
    )(e, e, a2)
    return hre, him


def _gelu_tanh(x):
    return 0.5 * x * (1.0 + jnp.tanh(math.sqrt(2.0 / math.pi) * (x + 0.044715 * x * x * x)))


def _s5_out_kernel(h_ref, w2_ref, yi_ref, u_ref, d_ref, wg_ref, o_ref):
    y = _dot(h_ref[...], w2_ref[...]) + yi_ref[...] + d_ref[...] * u_ref[...].astype(F32)
    y = _gelu_tanh(y)
    gate = _sigmoid(_dot(y.astype(BF16), wg_ref[...]))
    o_ref[...] = (y * gate).astype(o_ref.dtype)


def _s5_out(h, w2, e, u_rows, d_row, w_glu, tm):
    m = h.shape[0]
    tn = S5_WIDTH
    return pl.pallas_call(
        _s5_out_kernel,
        out_shape=jax.ShapeDtypeStruct((m, S5_T * S5_WIDTH), BF16),
        grid=(m // tm, S5_T),
        in_specs=[pl.BlockSpec((tm, h.shape[1]), lambda i, j: (i, 0)),
                  pl.BlockSpec((w2.shape[0], tn), lambda i, j: (0, j)),
                  pl.BlockSpec((tm, tn), lambda i, j: (i, j)),
                  pl.BlockSpec((tm, tn), lambda i, j: (i, j)),
                  pl.BlockSpec((1, tn), lambda i, j: (0, 0)),
                  pl.BlockSpec((tn, tn), lambda i, j: (0, 0))],
        out_specs=pl.BlockSpec((tm, tn), lambda i, j: (i, j)),
        compiler_params=_params("arbitrary", "arbitrary"),
        name="s5_out",
    )(h, w2, e, u_rows, d_row, w_glu)


LANES = 128
SUBLANES = 8


def _to_token_tiles(ref, val):
    t = val.shape[0]
    for s in range(SUBLANES):
        ref[pl.ds(s, t, stride=SUBLANES), :] = val[:, s * LANES:(s + 1) * LANES]


def _from_token_tiles(ref, t, row0=0):
    return jnp.concatenate([ref[pl.ds(row0 * SUBLANES + s, t, stride=SUBLANES), :] for s in range(SUBLANES)],
                           axis=-1)


def _route(h2b, wr_ref, br_ref, cnt_ref, e8_ref, p8_ref, w8_ref):
    tm = h2b.shape[0]
    per_group = N_EXPERTS // ROUTE_GROUPS
    scores = _sigmoid(lax.dot_general(wr_ref[...], h2b, _NT, preferred_element_type=F32))
    biased = scores + br_ref[...]
    neg = -jnp.inf
    sub = lax.broadcasted_iota(jnp.int32, (per_group, tm), 0)
    grp = []
    for gi in range(ROUTE_GROUPS):
        v = biased[gi * per_group:(gi + 1) * per_group, :]
        m1 = jnp.max(v, axis=0, keepdims=True)
        first = jnp.min(jnp.where(v == m1, sub, per_group), axis=0, keepdims=True)
        m2 = jnp.max(jnp.where(sub == first, neg, v), axis=0, keepdims=True)
        grp.append(m1 + m2)
    grp = jnp.concatenate(grp, axis=0)
    gid = lax.broadcasted_iota(jnp.int32, (ROUTE_GROUPS, tm), 0)
    beaten = jnp.zeros((ROUTE_GROUPS, tm), jnp.int32)
    for gj in range(ROUTE_GROUPS):
        r = grp[gj:gj + 1, :]
        beaten = beaten + jnp.where((r > grp) | ((r == grp) & (gj < gid)), 1, 0)
    group_ok = beaten < TOPK_GROUPS
    expert_ok = jnp.concatenate(
        [jnp.broadcast_to(group_ok[gi:gi + 1, :], (per_group, tm)) for gi in range(ROUTE_GROUPS)], axis=0)
    cur = jnp.where(expert_ok, biased, neg)
    eid = lax.broadcasted_iota(jnp.int32, (N_EXPERTS, tm), 0)
    sel = jnp.zeros((N_EXPERTS, tm), F32)
    picks, wts = [], []
    for _ in range(TOP_K):
        m = jnp.max(cur, axis=0, keepdims=True)
        idx = jnp.min(jnp.where(cur == m, eid, N_EXPERTS), axis=0, keepdims=True)
        hit = eid == idx
        picks.append(idx)
        wts.append(jnp.sum(jnp.where(hit, scores, 0.0), axis=0, keepdims=True))
        sel = jnp.where(hit, 1.0, sel)
        cur = jnp.where(hit, neg, cur)
    wsum = wts[0]
    for w in wts[1:]:
        wsum = wsum + w
    ti = lax.broadcasted_iota(jnp.int32, (tm, tm), 0)
    tj = lax.broadcasted_iota(jnp.int32, (tm, tm), 1)
    before = jnp.where(ti < tj, 1.0, 0.0).astype(BF16)
    pos = cnt_ref[...] + _dot(sel.astype(BF16), before)
    cnt_ref[...] = cnt_ref[...] + jnp.sum(sel, axis=1, keepdims=True)
    for k in range(TOP_K):
        e8_ref[k:k + 1, :] = picks[k]
        w8_ref[k:k + 1, :] = wts[k] / wsum * ROUTED_SCALE
        p8_ref[k:k + 1, :] = jnp.sum(jnp.where(eid == picks[k], pos, 0.0), axis=0, keepdims=True).astype(jnp.int32)


def _merge_kernel(x_ref, ya_ref, on_ref, go_ref, ga_ref, gb_ref, g1_ref, sc_ref, sh_ref, n2_ref,
                  pa_ref, pb_ref, wo_ref, wr_ref, br_ref,
                  x1_ref, h2_ref, e8_ref, p8_ref, w8_ref, cnt_ref):
    @pl.when(pl.program_id(0) == 0)
    def _():
        cnt_ref[...] = jnp.zeros_like(cnt_ref)

    go = go_ref[...].astype(F32)
    y_b = (on_ref[...].astype(F32) * (go * _sigmoid(go))).astype(BF16)
    pa = _dot(ya_ref[...], pa_ref[...])
    pb = _dot(y_b, pb_ref[...])
    merged = _sigmoid(ga_ref[...].astype(F32)) * pa + _sigmoid(gb_ref[...].astype(F32)) * pb
    x1 = x_ref[...] + g1_ref[0] * _dot(merged.astype(BF16), wo_ref[...])
    x1_ref[...] = x1
    y = x1 * lax.rsqrt(jnp.mean(x1 * x1, axis=-1, keepdims=True) + EPS) * n2_ref[...]
    h2 = y * (1.0 + sc_ref[0]) + sh_ref[0]
    _to_token_tiles(h2_ref, h2)
    _route(h2.astype(BF16), wr_ref, br_ref, cnt_ref, e8_ref, p8_ref, w8_ref)


def _merge(x2d, ya, on, go, ga, gb, g1, sc2, sh2, n2g, pa, pb, wo, wr_t, br, rows_per_batch, tm):
    n, d = x2d.shape
    per = rows_per_batch // tm
    row = lambda wd: pl.BlockSpec((tm, wd), lambda i: (i, 0))
    mod = pl.BlockSpec((1, 1, d), lambda i: (i // per, 0, 0))
    full = lambda a: pl.BlockSpec(a.shape, lambda i: (0, 0))
    tok = pl.BlockSpec((TOP_K, tm), lambda i: (0, i))
    return pl.pallas_call(
        _merge_kernel,
        out_shape=[jax.ShapeDtypeStruct((n, d), F32), jax.ShapeDtypeStruct((n * SUBLANES, LANES), F32),
                   jax.ShapeDtypeStruct((TOP_K, n), jnp.int32), jax.ShapeDtypeStruct((TOP_K, n), jnp.int32),
                   jax.ShapeDtypeStruct((TOP_K, n), F32), jax.ShapeDtypeStruct((N_EXPERTS, 1), F32)],
        grid=(n // tm,),
        in_specs=[row(d), row(S5_WIDTH), row(HG_WIDTH), row(HG_WIDTH), row(d), row(d), mod, mod, mod,
                  pl.BlockSpec((1, d), lambda i: (0, 0)), full(pa), full(pb), full(wo), full(wr_t), full(br)],
        out_specs=[row(d), pl.BlockSpec((tm * SUBLANES, LANES), lambda i: (i, 0)), tok, tok, tok,
                   pl.BlockSpec((N_EXPERTS, 1), lambda i: (0, 0))],
        compiler_params=_params("arbitrary"),
        name="merge_out_proj_route",
    )(x2d, ya, on, go, ga, gb, g1, sc2, sh2, n2g.reshape(1, d), pa, pb, wo, wr_t, br)


MOE_BLK = 256


def _wait_rows(any_ref, sem, n_rows):
    pltpu.make_async_copy(any_ref.at[pl.ds(0, n_rows)], any_ref.at[pl.ds(0, n_rows)], sem).wait()


def _dispatch_kernel(meta_ref, dest_ref, h2_hbm, xs_hbm, dest_smem, zrow, sem, zsem, *, tm, n_blocks):
    i = pl.program_id(0)
    cp = pltpu.make_async_copy(dest_ref, dest_smem, sem)
    cp.start()
    cp.wait()

    def body(t, carry):
        for k in range(TOP_K):
            pltpu.make_async_copy(h2_hbm.at[i * tm + t], xs_hbm.at[dest_smem[k, t]], sem).start()
        return carry

    lax.fori_loop(0, tm, body, 0)

    @pl.when(i == 0)
    def _():
        zrow[...] = jnp.zeros_like(zrow)
        total = n_blocks * MOE_BLK

        def per_expert(e, n_issued):
            lo = meta_ref[0, e] + meta_ref[1, e]
            hi = jnp.where(e == N_EXPERTS - 1, total, meta_ref[0, jnp.minimum(e + 1, N_EXPERTS - 1)])

            def pad(r, c):
                pltpu.make_async_copy(zrow, xs_hbm.at[r], zsem).start()
                return c

            lax.fori_loop(lo, hi, pad, 0)
            return n_issued + (hi - lo)

        n_pad = lax.fori_loop(0, N_EXPERTS, per_expert, 0)

        def drain(r, c):
            pltpu.make_async_copy(zrow, xs_hbm.at[0], zsem).wait()
            return c

        lax.fori_loop(0, n_pad, drain, 0)

    _wait_rows(xs_hbm, sem, tm * TOP_K)


def _dispatch(meta, dest8, h2_tiles, n_blocks, tm):
    n = dest8.shape[1]
    cap = n_blocks * MOE_BLK
    return pl.pallas_call(
        functools.partial(_dispatch_kernel, tm=tm, n_blocks=n_blocks),
        out_shape=jax.ShapeDtypeStruct((cap, SUBLANES, LANES), F32),
        grid_spec=pltpu.PrefetchScalarGridSpec(
            num_scalar_prefetch=1,
            grid=(n // tm,),
            in_specs=[pl.BlockSpec((TOP_K, tm), lambda i, m: (0, i)),
                      pl.BlockSpec(memory_space=pl.ANY)],
            out_specs=pl.BlockSpec(memory_space=pl.ANY),
            scratch_shapes=[pltpu.SMEM((TOP_K, tm), jnp.int32), pltpu.VMEM((SUBLANES, LANES), F32),
                            pltpu.SemaphoreType.DMA, pltpu.SemaphoreType.DMA]),
        compiler_params=pltpu.CompilerParams(dimension_semantics=("arbitrary",), vmem_limit_bytes=VMEM_LIMIT,
                                             has_side_effects=True),
        name="moe_dispatch",
    )(meta, dest8, h2_tiles)


def _expert_kernel(be_ref, x_ref, w1_ref, w3_ref, w2_ref, o_ref, w1b, w3b, w2b):
    j = pl.program_id(0)
    e = be_ref[j]
    prev = be_ref[jnp.maximum(j - 1, 0)]

    @pl.when(jnp.logical_or(j == 0, e != prev))
    def _():
        w1b[...] = w1_ref[0].astype(BF16)
        w3b[...] = w3_ref[0].astype(BF16)
        w2b[...] = w2_ref[0].astype(BF16)

    x = _from_token_tiles(x_ref, MOE_BLK).astype(BF16)
    a = _dot(x, w1b[...])
    hid = (a * _sigmoid(a)) * _dot(x, w3b[...])
    _to_token_tiles(o_ref, _dot(hid.astype(BF16), w2b[...]))


def _experts(block_e, xs, w1, w3, w2):
    cap = xs.shape[0] // SUBLANES
    n_blocks = cap // MOE_BLK
    d, f = w1.shape[1], w1.shape[2]
    rows = pl.BlockSpec((MOE_BLK * SUBLANES, LANES), lambda j, be: (j, 0))
    return pl.pallas_call(
        _expert_kernel,
        out_shape=jax.ShapeDtypeStruct((cap * SUBLANES, LANES), F32),
        grid_spec=pltpu.PrefetchScalarGridSpec(
            num_scalar_prefetch=1,
            grid=(n_blocks,),
            in_specs=[rows,
                      pl.BlockSpec((1, d, f), lambda j, be: (be[j], 0, 0)),
                      pl.BlockSpec((1, d, f), lambda j, be: (be[j], 0, 0)),
                      pl.BlockSpec((1, f, d), lambda j, be: (be[j], 0, 0))],
            out_specs=rows,
            scratch_shapes=[pltpu.VMEM((d, f), BF16), pltpu.VMEM((d, f), BF16), pltpu.VMEM((f, d), BF16)]),
        compiler_params=_params("arbitrary"),
        name="moe_experts",
    )(block_e, xs, w1, w3, w2)


def _combine_kernel(dest_ref, w8_ref, x1_ref, h2_ref, g2_ref, ws1_ref, ws3_ref, ws2_ref, fg_ref, ys_hbm,
                    o_ref, dest_smem, gbuf, sem, *, tm):
    cp = pltpu.make_async_copy(dest_ref, dest_smem, sem)
    cp.start()
    cp.wait()

    def body(t, carry):
        for k in range(TOP_K):
            row = pl.multiple_of((k * tm + t) * SUBLANES, SUBLANES)
            pltpu.make_async_copy(ys_hbm.at[dest_smem[k, t]], gbuf.at[pl.ds(row, SUBLANES)], sem).start()
        return carry

    lax.fori_loop(0, tm, body, 0)

    h2 = _from_token_tiles(h2_ref, tm).astype(BF16)
    a = _dot(h2, ws1_ref[...])
    hid = (a * _sigmoid(a)) * _dot(h2, ws3_ref[...])
    acc = _dot(hid.astype(BF16), ws2_ref[...])

    _wait_rows(gbuf, sem, tm * TOP_K * SUBLANES)
    wt = w8_ref[...].T
    for k in range(TOP_K):
        acc = acc + wt[:, k:k + 1] * _from_token_tiles(gbuf, tm, row0=k * tm)
    y = x1_ref[...] + g2_ref[0] * acc
    o_ref[...] = y * lax.rsqrt(jnp.mean(y * y, axis=-1, keepdims=True) + EPS) * fg_ref[...]


def _combine(dest8, w8, x1, h2_tiles, g2, ws1, ws3, ws2, fg, ys, rows_per_batch, tm):
    n, d = x1.shape
    per = rows_per_batch // tm
    tok = pl.BlockSpec((TOP_K, tm), lambda i: (0, i))
    full = lambda a: pl.BlockSpec(a.shape, lambda i: (0, 0))
    return pl.pallas_call(
        functools.partial(_combine_kernel, tm=tm),
        out_shape=jax.ShapeDtypeStruct((n, d), F32),
        grid=(n // tm,),
        in_specs=[tok, tok, pl.BlockSpec((tm, d), lambda i: (i, 0)),
                  pl.BlockSpec((tm * SUBLANES, LANES), lambda i: (i, 0)),
                  pl.BlockSpec((1, 1, d), lambda i: (i // per, 0, 0)),
                  full(ws1), full(ws3), full(ws2), pl.BlockSpec((1, d), lambda i: (0, 0)),
                  pl.BlockSpec(memory_space=pl.ANY)],
        out_specs=pl.BlockSpec((tm, d), lambda i: (i, 0)),
        scratch_shapes=[pltpu.SMEM((TOP_K, tm), jnp.int32), pltpu.VMEM((TOP_K * tm * SUBLANES, LANES), F32),
                        pltpu.SemaphoreType.DMA],
        compiler_params=_params("arbitrary"),
        name="moe_combine_final",
    )(dest8, w8, x1, h2_tiles, g2, ws1, ws3, ws2, fg.reshape(1, d), ys)


def _moe_plan(e8, p8, counts, n_assign):
    cnt = counts.reshape(N_EXPERTS).astype(jnp.int32)
    padded = (cnt + MOE_BLK - 1) // MOE_BLK * MOE_BLK
    pends = jnp.cumsum(padded)
    pstarts = pends - padded
    n_blocks = (n_assign + N_EXPERTS * (MOE_BLK - 1) + MOE_BLK - 1) // MOE_BLK
    ids = jnp.arange(N_EXPERTS, dtype=jnp.int32)[:, None, None]
    dest8 = p8 + jnp.sum(jnp.where(e8[None] == ids, pstarts[:, None, None], 0), axis=0)
    blk_start = jnp.arange(n_blocks, dtype=jnp.int32) * MOE_BLK
    block_e = jnp.minimum(jnp.sum((blk_start[:, None] >= pends[None, :]).astype(jnp.int32), axis=1),
                          N_EXPERTS - 1).astype(jnp.int32)
    meta = jnp.stack([pstarts, cnt], axis=0).astype(jnp.int32)
    return dest8, block_e, meta, n_blocks


def _mixer(x, c, ctx, c_ctx, w_ada, b_ada, norm1_g, norm2_g, w_in, s5_lam_re, s5_lam_im, s5_log_dt,
           s5_b_re, s5_b_im, s5_c_re, s5_c_im, s5_d, s5_w_glu, lb, hg_norm_g, p_a, p_b, w_out,
           moe_w_router, moe_b_router):
    b, l, d = x.shape
    lc = ctx.shape[1]
    n = b * l
    rows = l // GRID_W

    c8 = jnp.concatenate([c, c_ctx[None], jnp.zeros((8 - b - 1, d), F32)], axis=0)
    mod = _ada(c8, w_ada, b_ada)
    sh1, sc1, g1, sh2, sc2, g2 = [mod[:b, k * d:(k + 1) * d].reshape(b, 1, d) for k in range(6)]
    csh1, csc1 = mod[b:b + 1, 0:d].reshape(1, 1, d), mod[b:b + 1, d:2 * d].reshape(1, 1, d)

    w_in_b = w_in.astype(BF16)
    z = dict(zip([p[0] for p in _IN_PIECES], _inproj(x.reshape(n, d), sc1, sh1, norm1_g, w_in_b, l, 512)))
    zc = dict(zip([p[0] for p in _IN_PIECES], _inproj(ctx.reshape(b * lc, d), csc1, csh1, norm1_g, w_in_b, lc, lc)))

    cm = lambda t: t.reshape(b, rows, GRID_W * HG_WIDTH)
    cx = lambda t: t.reshape(b, lc, HG_WIDTH)
    lb_row = lb.reshape(1, HG_WIDTH)
    o_f = _hgrn_pass(cm(z["q"]), cm(z["ff"]), cm(z["i"]), cx(zc["ff"]), cx(zc["i"]), lb_row, None, None,
                     reverse=False)
    o_n = _hgrn_pass(cm(z["q"]), cm(z["fb"]), cm(z["i"]), cx(zc["fb"]), cx(zc["i"]), lb_row, o_f,
                     hg_norm_g.reshape(1, HG_DK), reverse=True)
    o_n = o_n.reshape(n, HG_WIDTH)

    w1, w2, a = _s5_weights(s5_lam_re, s5_lam_im, s5_log_dt, s5_b_re, s5_b_im, s5_c_re, s5_c_im)
    kc, kl = lc // S5_T, l // S5_T
    u_lat = z["u"].reshape(b, kl, S5_T * S5_WIDTH)
    u_ctx = zc["u"].reshape(b, kc, S5_T * S5_WIDTH)
    rows_in = kl + 2 * kc
    u_ext = jnp.concatenate([u_ctx, u_lat, u_ctx], axis=1).reshape(b * rows_in, S5_T * S5_WIDTH)
    e = _matmul(u_ext, w1, (b * rows_in) // 2, 512, "s5_in")
    ns = S5_GROUPS * S5_STATE
    hf_re, hf_im = _s5_scan(e, a[0], b, rows_in, kl, kc, S5_T * S5_WIDTH, False)
    hb_re, hb_im = _s5_scan(e, a[1], b, rows_in, kl, kc, S5_T * S5_WIDTH + 2 * ns, True)
    h = jnp.concatenate([hf_re, hf_im, hb_re, hb_im], axis=1).astype(BF16)
    y_intra = e.reshape(b, rows_in, -1)[:, kc:kc + kl, :S5_T * S5_WIDTH].reshape(b * kl, S5_T * S5_WIDTH)
    d_row = s5_d.astype(F32).reshape(1, S5_WIDTH)
    y_a = _s5_out(h, w2, y_intra, u_lat.reshape(b * kl, -1), d_row, s5_w_glu.astype(BF16), (b * kl) // 2)
    y_a = y_a.reshape(n, S5_WIDTH)

    return _merge(x.reshape(n, d), y_a, o_n, z["go"], z["ga"], z["gb"], g1, sc2, sh2, norm2_g,
                  p_a.astype(BF16), p_b.astype(BF16), w_out.astype(BF16),
                  moe_w_router.T.astype(BF16), moe_b_router.astype(F32).reshape(N_EXPERTS, 1), l, 512) + (g2,)


def kernel(x, c, ctx, c_ctx, w_ada, b_ada, norm1_g, norm2_g, w_in, s5_lam_re, s5_lam_im, s5_log_dt, s5_b_re,
           s5_b_im, s5_c_re, s5_c_im, s5_d, s5_w_glu, hg_lb_logits, hg_norm_g, p_a, p_b, w_out, moe_w_router,
           moe_b_router, moe_w1, moe_w3, moe_w2, moe_ws1, moe_ws3, moe_ws2, final_norm_g):
    b, l, d = x.shape
    n = b * l
    assert w_ada.shape[0] == 1, "single-layer block"
    lb = jnp.cumsum(jax.nn.softmax(hg_lb_logits.astype(F32), axis=0), axis=0)[0]
    x1, h2_tiles, e8, p8, w8, counts, g2 = _mixer(
        x, c, ctx, c_ctx, w_ada[0], b_ada[0], norm1_g[0], norm2_g[0], w_in[0], s5_lam_re[0], s5_lam_im[0],
        s5_log_dt[0], s5_b_re[0], s5_b_im[0], s5_c_re[0], s5_c_im[0], s5_d[0], s5_w_glu[0], lb, hg_norm_g[0],
        p_a[0], p_b[0], w_out[0], moe_w_router[0], moe_b_router[0])
    dest8, block_e, meta, n_blocks = _moe_plan(e8, p8, counts, n * TOP_K)
    xs = _dispatch(meta, dest8, h2_tiles.reshape(n, SUBLANES, LANES), n_blocks, 512)
    ys = _experts(block_e, xs.reshape(n_blocks * MOE_BLK * SUBLANES, LANES), moe_w1[0], moe_w3[0], moe_w2[0])
    out = _combine(dest8, w8, x1, h2_tiles, g2, moe_ws1[0].astype(BF16), moe_ws3[0].astype(BF16),
                   moe_ws2[0].astype(BF16), final_norm_g, ys.reshape(n_blocks * MOE_BLK, SUBLANES, LANES), l, 256)
    return out.reshape(b, l, d)
```

```python
import functools
import math

import jax
import jax.numpy as jnp
from jax import lax
from jax.experimental import pallas as pl
from jax.experimental.pallas import tpu as pltpu

F32 = jnp.float32
BF16 = jnp.bfloat16

GRID_W = 64
S5_WIDTH = 256
S5_GROUP = 16
S5_GROUPS = 16
S5_STATE = 64
HG_HEADS = 6
HG_DK = 128
HG_WIDTH = HG_HEADS * HG_DK
N_EXPERTS = 64
ROUTE_GROUPS = 8
TOPK_GROUPS = 4
TOP_K = 8
ROUTED_SCALE = 2.5
EPS = 1e-6

LANES = 128
SUBLANES = 8

S5_T = 16
HG_CHUNK = 64
VMEM_LIMIT = 56 * 1024 * 1024

_NT = (((1,), (1,)), ((), ()))
_TN = (((0,), (0,)), ((), ()))


def _params(*sem):
    return pltpu.CompilerParams(dimension_semantics=sem, vmem_limit_bytes=VMEM_LIMIT)


def _dot(a, b):
    return jnp.dot(a, b, preferred_element_type=F32)


def _sigmoid(x):
    return 1.0 / (1.0 + jnp.exp(-x))


def _ada_kernel(c_ref, w_ref, b_ref, o_ref):
    c = c_ref[...]
    s = (c * _sigmoid(c)).astype(BF16)
    o_ref[...] = _dot(s, w_ref[...].astype(BF16)) + b_ref[...]


def _ada(c8, w_ada, b_ada):
    d, n = w_ada.shape
    tn = 1536
    return pl.pallas_call(
        _ada_kernel,
        out_shape=jax.ShapeDtypeStruct((8, n), F32),
        grid=(n // tn,),
        in_specs=[pl.BlockSpec((8, d), lambda j: (0, 0)),
                  pl.BlockSpec((d, tn), lambda j: (0, j)),
                  pl.BlockSpec((1, tn), lambda j: (0, j))],
        out_specs=pl.BlockSpec((8, tn), lambda j: (0, j)),
        compiler_params=_params("arbitrary"),
        name="ada_mod",
    )(c8, w_ada, b_ada.reshape(1, n))


_IN_PIECES = (("u", 0, 256, BF16), ("q", 256, 768, BF16), ("ff", 1024, 768, F32),
              ("fb", 1792, 768, F32), ("i", 2560, 768, BF16), ("go", 3328, 768, BF16),
              ("ga", 4096, 1024, BF16), ("gb", 5120, 1024, BF16))


def _inproj_kernel(x_ref, sc_ref, sh_ref, g_ref, w_ref, *o_refs):
    x = x_ref[...]
    y = x * lax.rsqrt(jnp.mean(x * x, axis=-1, keepdims=True) + EPS) * g_ref[...]
    h = (y * (1.0 + sc_ref[0]) + sh_ref[0]).astype(BF16)
    for (_, a, wd, _), o_ref in zip(_IN_PIECES, o_refs):
        o_ref[...] = _dot(h, w_ref[:, a:a + wd]).astype(o_ref.dtype)


def _inproj(x2d, sc, sh, g, w_bf16, rows_per_mod, tm):
    n, d = x2d.shape
    per = rows_per_mod // tm
    mod_map = (lambda i: (i // per, 0, 0)) if sc.shape[0] > 1 else (lambda i: (0, 0, 0))
    return pl.pallas_call(
        _inproj_kernel,
        out_shape=[jax.ShapeDtypeStruct((n, wd), dt) for (_, _, wd, dt) in _IN_PIECES],
        grid=(n // tm,),
        in_specs=[pl.BlockSpec((tm, d), lambda i: (i, 0)),
                  pl.BlockSpec((1, 1, d), mod_map),
                  pl.BlockSpec((1, 1, d), mod_map),
                  pl.BlockSpec((1, d), lambda i: (0, 0)),
                  pl.BlockSpec(w_bf16.shape, lambda i: (0, 0))],
        out_specs=[pl.BlockSpec((tm, wd), lambda i: (i, 0)) for (_, _, wd, _) in _IN_PIECES],
        compiler_params=_params("arbitrary"),
        name="in_proj",
    )(x2d, sc, sh, g.reshape(1, d), w_bf16)


def _hgrn_gates(zf, lb):
    sig = _sigmoid(zf)
    logf = jnp.log(lb + (1.0 - lb) * sig)
    k = (1.0 - lb) * (1.0 - sig)
    return logf, k


def _chunk_cumsum(cs, logf):
    hi = logf.astype(BF16)
    lo = (logf - hi.astype(F32)).astype(BF16)
    return _dot(cs, hi) + _dot(cs, lo)


def _hgrn_state_step(zf, v, lb, st, cs, reverse):
    logf, k = _hgrn_gates(zf, lb)
    cum = _chunk_cumsum(cs, logf)
    t = 0 if reverse else HG_CHUNK - 1
    total = cum[t:t + 1, :]
    kdec = (k * jnp.exp(total - cum)).astype(BF16)
    st_new = st * jnp.exp(total) + lax.dot_general(v.astype(BF16), kdec, _TN, preferred_element_type=F32)
    return cum, k, st_new


def _hgrn_chunk(q, zf, v, lb, st, mask, cs, reverse):
    cum, k, st_new = _hgrn_state_step(zf, v, lb, st, cs, reverse)
    r = HG_CHUNK // 2 - 1 if reverse else HG_CHUNK // 2
    ref = cum[r:r + 1, :]
    qi = (q * jnp.exp(cum - ref)).astype(BF16)
    ki = (k * jnp.exp(ref - cum)).astype(BF16)
    s = lax.dot_general(qi, ki, _NT, preferred_element_type=F32)
    s = jnp.where(mask, s, 0.0)
    o = _dot(s.astype(BF16), v.astype(BF16))
    q_in = (q * jnp.exp(cum)).astype(BF16)
    o = o + lax.dot_general(q_in, st.astype(BF16), _NT, preferred_element_type=F32)
    return o, st_new


def _hgrn_kernel(*refs, reverse, final, n_ctx_chunks):
    if final:
        q_ref, f_ref, v_ref, cf_ref, cv_ref, lb_ref, of_ref, g_ref, o_ref, st_ref = refs
    else:
        q_ref, f_ref, v_ref, cf_ref, cv_ref, lb_ref, o_ref, st_ref = refs
    c_len = HG_CHUNK
    row = lax.broadcasted_iota(jnp.int32, (c_len, c_len), 0)
    col = lax.broadcasted_iota(jnp.int32, (c_len, c_len), 1)
    mask = (col >= row) if reverse else (col <= row)
    cs = jnp.where(mask, 1.0, 0.0).astype(BF16)

    @pl.when(pl.program_id(1) == 0)
    def _():
        order = range(n_ctx_chunks - 1, -1, -1) if reverse else range(n_ctx_chunks)
        for h in range(HG_HEADS):
            cols = slice(h * HG_DK, (h + 1) * HG_DK)
            st = jnp.zeros((HG_DK, HG_DK), F32)
            for c in order:
                rows = slice(c * c_len, (c + 1) * c_len)
                _, _, st = _hgrn_state_step(cf_ref[0, rows, cols], cv_ref[0, rows, cols].astype(F32),
                                            lb_ref[:, cols], st, cs, reverse)
            st_ref[h] = st

    n_chunks = q_ref.shape[1] // c_len
    order = range(n_chunks - 1, -1, -1) if reverse else range(n_chunks)
    for h in range(HG_HEADS):
        cols = slice(h * HG_DK, (h + 1) * HG_DK)
        st = st_ref[h]
        for c in order:
            rows = slice(c * c_len, (c + 1) * c_len)
            o, st = _hgrn_chunk(q_ref[0, rows, cols].astype(F32), f_ref[0, rows, cols],
                                v_ref[0, rows, cols].astype(F32), lb_ref[:, cols], st, mask, cs, reverse)
            if final:
                o = o + of_ref[0, rows, cols]
                o = o * lax.rsqrt(jnp.mean(o * o, axis=-1, keepdims=True) + EPS) * g_ref[...]
            o_ref[0, rows, cols] = o.astype(o_ref.dtype)
        st_ref[h] = st


def _hgrn_pass(q, f, v, cf, cv, lb, o_prev, g, *, reverse):
    b, rows, _ = q.shape
    nw = GRID_W
    final = o_prev is not None
    wmap = (lambda bi, w: (bi, 0, nw - 1 - w)) if reverse else (lambda bi, w: (bi, 0, w))
    blk = pl.BlockSpec((1, rows, HG_WIDTH), wmap)
    cblk = pl.BlockSpec((1, cf.shape[1], HG_WIDTH), lambda bi, w: (bi, 0, 0))
    in_specs = [blk, blk, blk, cblk, cblk, pl.BlockSpec((1, HG_WIDTH), lambda bi, w: (0, 0))]
    args = [q, f, v, cf, cv, lb]
    if final:
        in_specs += [blk, pl.BlockSpec((1, HG_DK), lambda bi, w: (0, 0))]
        args += [o_prev, g]
    return pl.pallas_call(
        functools.partial(_hgrn_kernel, reverse=reverse, final=final, n_ctx_chunks=cf.shape[1] // HG_CHUNK),
        out_shape=jax.ShapeDtypeStruct(q.shape, BF16 if final else F32),
        grid=(b, nw),
        in_specs=in_specs,
        out_specs=blk,
        scratch_shapes=[pltpu.VMEM((HG_HEADS, HG_DK, HG_DK), F32)],
        compiler_params=_params("arbitrary", "arbitrary"),
        name="hgrn_bwd" if reverse else "hgrn_fwd",
    )(*args)


def _s5_weights(lam_re, lam_im, log_dt, b_re, b_im, c_re, c_im):
    hp = lax.Precision.HIGHEST
    g, p, cc, t = S5_GROUPS, S5_STATE, S5_GROUP, S5_T
    lre = jnp.minimum(lam_re.astype(F32), -1e-4)
    lim = lam_im.astype(F32)
    dt = jnp.exp(log_dt.astype(F32))[..., None]
    ks = jnp.arange(t + 1, dtype=F32)[:, None, None, None]
    mag = jnp.exp(ks * (lre * dt)[None])
    pw_re = mag * jnp.cos(ks * (lim * dt)[None])
    pw_im = mag * jnp.sin(ks * (lim * dt)[None])
    nr, ni = pw_re[1] - 1.0, pw_im[1]
    den = lre * lre + lim * lim
    cf_re = (nr * lre + ni * lim) / den
    cf_im = (ni * lre - nr * lim) / den
    bb_re = cf_re[..., None] * b_re - cf_im[..., None] * b_im
    bb_im = cf_re[..., None] * b_im + cf_im[..., None] * b_re
    cre, cim = c_re.astype(F32), c_im.astype(F32)
    sw, ns = S5_WIDTH, 2 * g * p
    grp_of_row = jnp.arange(sw)[:, None] // cc

    cp_re = cre[None, None, :, :, :] * pw_re[:t, :, :, None, :] - cim[None, None] * pw_im[:t, :, :, None, :]
    cp_im = cre[None, None, :, :, :] * pw_im[:t, :, :, None, :] + cim[None, None] * pw_re[:t, :, :, None, :]
    kk = (jnp.einsum("kdgop,dgpi->dkgoi", cp_re, bb_re, precision=hp)
          - jnp.einsum("kdgop,dgpi->dkgoi", cp_im, bb_im, precision=hp))
    kf, kb = kk[0], kk[1]
    kall = jnp.concatenate([kb[:0:-1], (kf[0] + kb[0])[None], kf[1:]], axis=0)
    kt = kall.transpose(0, 1, 3, 2).reshape(2 * t - 1, sw, cc)
    same = grp_of_row == (jnp.arange(sw)[None, :] // cc)
    d_lag = jnp.where(same[None], jnp.tile(kt, (1, 1, g)), 0.0).astype(BF16)

    same_in = grp_of_row == (jnp.arange(ns)[None, :] // (2 * p))

    def in_to_state(pre, pim, bre, bim):
        xre = pre[..., None] * bre[None] - pim[..., None] * bim[None]
        xim = pre[..., None] * bim[None] + pim[..., None] * bre[None]
        x = jnp.concatenate([xre.transpose(0, 1, 3, 2), xim.transpose(0, 1, 3, 2)], axis=-1)
        x = jnp.tile(x.reshape(t, sw, 2 * p), (1, 1, g))
        return jnp.where(same_in[None], x, 0.0).reshape(t * sw, ns)

    w_in = jnp.concatenate([in_to_state(pw_re[t - 1::-1, 0], pw_im[t - 1::-1, 0], bb_re[0], bb_im[0]),
                            in_to_state(pw_re[:t, 1], pw_im[:t, 1], bb_re[1], bb_im[1])], axis=1).astype(BF16)

    same_out = (jnp.arange(ns)[:, None] // (2 * p)) == ((jnp.arange(t * sw)[None, :] // cc) % g)

    def state_to_out(pre, pim):
        are = cre[None] * pre[:, :, None, :] - cim[None] * pim[:, :, None, :]
        aim = cre[None] * pim[:, :, None, :] + cim[None] * pre[:, :, None, :]
        a = jnp.concatenate([are, -aim], axis=-1).transpose(1, 3, 0, 2)
        a = jnp.broadcast_to(a.reshape(ns, t, 1, cc), (ns, t, g, cc)).reshape(ns, t * sw)
        return jnp.where(same_out, a, 0.0).astype(BF16)

    w_out_f = state_to_out(pw_re[1:, 0], pw_im[1:, 0])
    w_out_b = state_to_out(pw_re[t:0:-1, 1], pw_im[t:0:-1, 1])

    are_t, aim_t = pw_re[t], pw_im[t]
    a1 = jnp.concatenate([are_t, are_t], axis=-1).reshape(2, 1, ns)
    a2 = jnp.concatenate([-aim_t, aim_t], axis=-1).reshape(2, 1, ns)
    return d_lag, w_in, w_out_f, w_out_b, jnp.concatenate([a1, a2], axis=1)


def _s5_in_kernel(u_ref, d_ref, w_ref, o_ref):
    j = pl.program_id(0)

    @pl.when(j < S5_T)
    def _():
        acc = _dot(u_ref[:, 0:S5_WIDTH], d_ref[j + S5_T - 1])
        for s in range(1, S5_T):
            acc = acc + _dot(u_ref[:, s * S5_WIDTH:(s + 1) * S5_WIDTH], d_ref[j - s + S5_T - 1])
        o_ref[...] = acc

    @pl.when(j >= S5_T)
    def _():
        o_ref[...] = _dot(u_ref[...], w_ref[...])


def _s5_in(u, d_lag, w_in, tm):
    m, k = u.shape
    tn = S5_WIDTH
    nj = (k + w_in.shape[1]) // tn
    return pl.pallas_call(
        _s5_in_kernel,
        out_shape=jax.ShapeDtypeStruct((m, nj * tn), F32),
        grid=(nj, m // tm),
        in_specs=[pl.BlockSpec((tm, k), lambda j, i: (i, 0)),
                  pl.BlockSpec(d_lag.shape, lambda j, i: (0, 0, 0)),
                  pl.BlockSpec((k, tn), lambda j, i: (0, jnp.maximum(j - S5_T, 0)))],
        out_specs=pl.BlockSpec((tm, tn), lambda j, i: (i, j)),
        compiler_params=_params("arbitrary", "arbitrary"),
        name="s5_in",
    )(u, d_lag, w_in)


def _s5_scan_kernel(e_ref, a_ref, h_ref, *, nb, rows_in, rows_out, reverse):
    a1, a2 = a_ref[0, 0:1, :], a_ref[0, 1:2, :]
    tc = a1.shape[1]
    zero = jnp.zeros_like(a1)

    def swap_halves(v):
        return jnp.concatenate([pltpu.roll(v[:, k * LANES:(k + 1) * LANES], LANES // 2, axis=1)
                                for k in range(tc // LANES)], axis=1)

    def step(src, carry, store):
        new = []
        for bi in range(nb):
            h = carry[bi]
            if store:
                h_ref[pl.ds(bi * rows_out + src, 1), :] = h
            e = e_ref[pl.ds(bi * rows_in + src, 1), :]
            new.append(a1 * h + a2 * swap_halves(h) + e)
        return tuple(new)

    n_ctx = rows_in - rows_out
    ctx_src = (lambda s: rows_in - 1 - s) if reverse else (lambda s: rows_out + s)
    lat_src = (lambda s: rows_out - 1 - s) if reverse else (lambda s: s)
    carry = lax.fori_loop(0, n_ctx, lambda s, c: step(ctx_src(s), c, False), tuple([zero] * nb))
    lax.fori_loop(0, rows_out, lambda s, c: step(lat_src(s), c, True), carry)


def _s5_scan(e, decay, nb, rows_in, rows_out, direction):
    tc = 1024
    ns = 2 * S5_GROUPS * S5_STATE
    col0 = (S5_T * S5_WIDTH + direction * ns) // tc
    return pl.pallas_call(
        functools.partial(_s5_scan_kernel, nb=nb, rows_in=rows_in, rows_out=rows_out, reverse=direction == 1),
        out_shape=jax.ShapeDtypeStruct((nb * rows_out, ns), F32),
        grid=(ns // tc,),
        in_specs=[pl.BlockSpec((nb * rows_in, tc), lambda j: (0, col0 + j)),
                  pl.BlockSpec((1, 2, tc), lambda j: (direction, 0, j))],
        out_specs=pl.BlockSpec((nb * rows_out, tc), lambda j: (0, j)),
        compiler_params=_params("arbitrary"),
        name="s5_scan_bwd" if direction else "s5_scan_fwd",
    )(e, decay)


def _gelu_tanh(x):
    return 0.5 * x * (1.0 + jnp.tanh(math.sqrt(2.0 / math.pi) * (x + 0.044715 * x * x * x)))


def _s5_out_kernel(hf_ref, hb_ref, wf_ref, wb_ref, yi_ref, u_ref, d_ref, wg_ref, o_ref):
    y = (_dot(hf_ref[...].astype(BF16), wf_ref[...]) + _dot(hb_ref[...].astype(BF16), wb_ref[...])
         + yi_ref[0] + d_ref[...] * u_ref[...].astype(F32))
    y = _gelu_tanh(y)
    gate = _sigmoid(_dot(y.astype(BF16), wg_ref[...]))
    o_ref[...] = (y * gate).astype(o_ref.dtype)


def _s5_out(hf, hb, w_out_f, w_out_b, e3, u_rows, d_row, w_glu):
    m, ns = hf.shape
    nb = e3.shape[0]
    tm = m // nb
    tn = S5_WIDTH
    return pl.pallas_call(
        _s5_out_kernel,
        out_shape=jax.ShapeDtypeStruct((m, S5_T * S5_WIDTH), BF16),
        grid=(nb, S5_T),
        in_specs=[pl.BlockSpec((tm, ns), lambda i, j: (i, 0)),
                  pl.BlockSpec((tm, ns), lambda i, j: (i, 0)),
                  pl.BlockSpec((ns, tn), lambda i, j: (0, j)),
                  pl.BlockSpec((ns, tn), lambda i, j: (0, j)),
                  pl.BlockSpec((1, tm, tn), lambda i, j: (i, 0, j)),
                  pl.BlockSpec((tm, tn), lambda i, j: (i, j)),
                  pl.BlockSpec((1, tn), lambda i, j: (0, 0)),
                  pl.BlockSpec((tn, tn), lambda i, j: (0, 0))],
        out_specs=pl.BlockSpec((tm, tn), lambda i, j: (i, j)),
        compiler_params=_params("arbitrary", "arbitrary"),
        name="s5_out",
    )(hf, hb, w_out_f, w_out_b, e3, u_rows, d_row, w_glu)


def _to_token_tiles(ref, val):
    t = val.shape[0]
    for s in range(SUBLANES):
        ref[pl.ds(s, t, stride=SUBLANES), :] = val[:, s * LANES:(s + 1) * LANES]


def _from_token_tiles(ref, t, row0=0):
    return jnp.concatenate([ref[pl.ds(row0 * SUBLANES + s, t, stride=SUBLANES), :] for s in range(SUBLANES)],
                           axis=-1)


def _route(h2b, wr_ref, br_ref, cnt_ref, e8_ref, p8_ref, w8_ref):
    tm = h2b.shape[0]
    per_group = N_EXPERTS // ROUTE_GROUPS
    scores = _sigmoid(lax.dot_general(wr_ref[...], h2b, _NT, preferred_element_type=F32))
    biased = scores + br_ref[...]
    neg = -jnp.inf
    sub = lax.broadcasted_iota(jnp.int32, (per_group, tm), 0)
    grp = []
    for gi in range(ROUTE_GROUPS):
        v = biased[gi * per_group:(gi + 1) * per_group, :]
        m1 = jnp.max(v, axis=0, keepdims=True)
        first = jnp.min(jnp.where(v == m1, sub, per_group), axis=0, keepdims=True)
        m2 = jnp.max(jnp.where(sub == first, neg, v), axis=0, keepdims=True)
        grp.append(m1 + m2)
    grp = jnp.concatenate(grp, axis=0)
    gid = lax.broadcasted_iota(jnp.int32, (ROUTE_GROUPS, tm), 0)
    beaten = jnp.zeros((ROUTE_GROUPS, tm), jnp.int32)
    for gj in range(ROUTE_GROUPS):
        r = grp[gj:gj + 1, :]
        beaten = beaten + jnp.where((r > grp) | ((r == grp) & (gj < gid)), 1, 0)
    group_ok = beaten < TOPK_GROUPS
    expert_ok = jnp.concatenate(
        [jnp.broadcast_to(group_ok[gi:gi + 1, :], (per_group, tm)) for gi in range(ROUTE_GROUPS)], axis=0)
    cur = jnp.where(expert_ok, biased, neg)
    eid = lax.broadcasted_iota(jnp.int32, (N_EXPERTS, tm), 0)
    sel = jnp.zeros((N_EXPERTS, tm), F32)
    picks, wts = [], []
    for _ in range(TOP_K):
        m = jnp.max(cur, axis=0, keepdims=True)
        idx = jnp.min(jnp.where(cur == m, eid, N_EXPERTS), axis=0, keepdims=True)
        hit = eid == idx
        picks.append(idx)
        wts.append(jnp.sum(jnp.where(hit, scores, 0.0), axis=0, keepdims=True))
        sel = jnp.where(hit, 1.0, sel)
        cur = jnp.where(hit, neg, cur)
    wsum = wts[0]
    for w in wts[1:]:
        wsum = wsum + w
    ti = lax.broadcasted_iota(jnp.int32, (tm, tm), 0)
    tj = lax.broadcasted_iota(jnp.int32, (tm, tm), 1)
    before = jnp.where(ti < tj, 1.0, 0.0).astype(BF16)
    pos = cnt_ref[...] + _dot(sel.astype(BF16), before)
    cnt_ref[...] = cnt_ref[...] + jnp.sum(sel, axis=1, keepdims=True)
    for k in range(TOP_K):
        e8_ref[k:k + 1, :] = picks[k]
        w8_ref[k:k + 1, :] = wts[k] / wsum * ROUTED_SCALE
        p8_ref[k:k + 1, :] = jnp.sum(jnp.where(eid == picks[k], pos, 0.0), axis=0, keepdims=True).astype(jnp.int32)


def _merge_kernel(x_ref, ya_ref, on_ref, go_ref, ga_ref, gb_ref, g1_ref, sc_ref, sh_ref, n2_ref,
                  pa_ref, pb_ref, wo_ref, wr_ref, br_ref,
                  x1_ref, h2_ref, e8_ref, p8_ref, w8_ref, cnt_ref):
    @pl.when(pl.program_id(0) == 0)
    def _():
        cnt_ref[...] = jnp.zeros_like(cnt_ref)

    go = go_ref[...].astype(F32)
    y_b = (on_ref[...].astype(F32) * (go * _sigmoid(go))).astype(BF16)
    pa = _dot(ya_ref[...], pa_ref[...])
    pb = _dot(y_b, pb_ref[...])
    merged = _sigmoid(ga_ref[...].astype(F32)) * pa + _sigmoid(gb_ref[...].astype(F32)) * pb
    x1 = x_ref[...] + g1_ref[0] * _dot(merged.astype(BF16), wo_ref[...])
    x1_ref[...] = x1
    y = x1 * lax.rsqrt(jnp.mean(x1 * x1, axis=-1, keepdims=True) + EPS) * n2_ref[...]
    h2 = y * (1.0 + sc_ref[0]) + sh_ref[0]
    _to_token_tiles(h2_ref, h2)
    _route(h2.astype(BF16), wr_ref, br_ref, cnt_ref, e8_ref, p8_ref, w8_ref)


def _merge(x2d, ya, on, go, ga, gb, g1, sc2, sh2, n2g, pa, pb, wo, wr_t, br, rows_per_batch, tm):
    n, d = x2d.shape
    per = rows_per_batch // tm
    row = lambda wd: pl.BlockSpec((tm, wd), lambda i: (i, 0))
    mod = pl.BlockSpec((1, 1, d), lambda i: (i // per, 0, 0))
    full = lambda a: pl.BlockSpec(a.shape, lambda i: (0, 0))
    tok = pl.BlockSpec((TOP_K, tm), lambda i: (0, i))
    return pl.pallas_call(
        _merge_kernel,
        out_shape=[jax.ShapeDtypeStruct((n, d), F32), jax.ShapeDtypeStruct((n * SUBLANES, LANES), F32),
                   jax.ShapeDtypeStruct((TOP_K, n), jnp.int32), jax.ShapeDtypeStruct((TOP_K, n), jnp.int32),
                   jax.ShapeDtypeStruct((TOP_K, n), F32), jax.ShapeDtypeStruct((N_EXPERTS, 1), F32)],
        grid=(n // tm,),
        in_specs=[row(d), row(S5_WIDTH), row(HG_WIDTH), row(HG_WIDTH), row(d), row(d), mod, mod, mod,
                  pl.BlockSpec((1, d), lambda i: (0, 0)), full(pa), full(pb), full(wo), full(wr_t), full(br)],
        out_specs=[row(d), pl.BlockSpec((tm * SUBLANES, LANES), lambda i: (i, 0)), tok, tok, tok,
                   pl.BlockSpec((N_EXPERTS, 1), lambda i: (0, 0))],
        compiler_params=_params("arbitrary"),
        name="merge_out_proj_route",
    )(x2d, ya, on, go, ga, gb, g1, sc2, sh2, n2g.reshape(1, d), pa, pb, wo, wr_t, br)


MOE_BLK = 256


def _wait_rows(any_ref, sem, n_rows):
    pltpu.make_async_copy(any_ref.at[pl.ds(0, n_rows)], any_ref.at[pl.ds(0, n_rows)], sem).wait()


def _dispatch_kernel(meta_ref, dest_ref, h2_ref, xs_hbm, dest_smem, zrow, sem, zsem, *, tm, n_blocks):
    i = pl.program_id(0)
    cp = pltpu.make_async_copy(dest_ref, dest_smem, sem)
    cp.start()
    cp.wait()

    def body(t, carry):
        src = h2_ref.at[pl.ds(pl.multiple_of(t * SUBLANES, SUBLANES), SUBLANES)]
        for k in range(TOP_K):
            pltpu.make_async_copy(src, xs_hbm.at[dest_smem[k, t]], sem).start()
        return carry

    lax.fori_loop(0, tm, body, 0)

    @pl.when(i == 0)
    def _():
        zrow[...] = jnp.zeros_like(zrow)
        total = n_blocks * MOE_BLK

        def per_expert(e, n_issued):
            lo = meta_ref[0, e] + meta_ref[1, e]
            hi = jnp.where(e == N_EXPERTS - 1, total, meta_ref[0, jnp.minimum(e + 1, N_EXPERTS - 1)])

            def pad(r, c):
                pltpu.make_async_copy(zrow, xs_hbm.at[r], zsem).start()
                return c

            lax.fori_loop(lo, hi, pad, 0)
            return n_issued + (hi - lo)

        n_pad = lax.fori_loop(0, N_EXPERTS, per_expert, 0)

        def drain(r, c):
            pltpu.make_async_copy(zrow, xs_hbm.at[0], zsem).wait()
            return c

        lax.fori_loop(0, n_pad, drain, 0)

    _wait_rows(xs_hbm, sem, tm * TOP_K)


def _dispatch(meta, dest8, h2_tiles, n_blocks, tm):
    n = dest8.shape[1]
    cap = n_blocks * MOE_BLK
    return pl.pallas_call(
        functools.partial(_dispatch_kernel, tm=tm, n_blocks=n_blocks),
        out_shape=jax.ShapeDtypeStruct((cap, SUBLANES, LANES), F32),
        grid_spec=pltpu.PrefetchScalarGridSpec(
            num_scalar_prefetch=1,
            grid=(n // tm,),
            in_specs=[pl.BlockSpec((TOP_K, tm), lambda i, m: (0, i)),
                      pl.BlockSpec((tm * SUBLANES, LANES), lambda i, m: (i, 0))],
            out_specs=pl.BlockSpec(memory_space=pl.ANY),
            scratch_shapes=[pltpu.SMEM((TOP_K, tm), jnp.int32), pltpu.VMEM((SUBLANES, LANES), F32),
                            pltpu.SemaphoreType.DMA, pltpu.SemaphoreType.DMA]),
        compiler_params=pltpu.CompilerParams(dimension_semantics=("arbitrary",), vmem_limit_bytes=VMEM_LIMIT,
                                             has_side_effects=True),
        name="moe_dispatch",
    )(meta, dest8, h2_tiles)


def _expert_kernel(be_ref, nu_ref, x_ref, w1_ref, w3_ref, w2_ref, o_ref, w1b, w3b, w2b):
    j = pl.program_id(0)
    e = be_ref[j]
    prev = be_ref[jnp.maximum(j - 1, 0)]
    used = j < nu_ref[0]

    @pl.when(jnp.logical_and(used, jnp.logical_or(j == 0, e != prev)))
    def _():
        w1b[...] = w1_ref[0].astype(BF16)
        w3b[...] = w3_ref[0].astype(BF16)
        w2b[...] = w2_ref[0].astype(BF16)

    @pl.when(used)
    def _():
        x = _from_token_tiles(x_ref, MOE_BLK).astype(BF16)
        a = _dot(x, w1b[...])
        hid = (a * _sigmoid(a)) * _dot(x, w3b[...])
        _to_token_tiles(o_ref, _dot(hid.astype(BF16), w2b[...]))

    @pl.when(jnp.logical_not(used))
    def _():
        o_ref[...] = jnp.zeros_like(o_ref)


def _experts(block_e, n_used, xs, w1, w3, w2):
    cap = xs.shape[0] // SUBLANES
    n_blocks = cap // MOE_BLK
    d, f = w1.shape[1], w1.shape[2]
    rows = pl.BlockSpec((MOE_BLK * SUBLANES, LANES), lambda j, be, nu: (j, 0))
    return pl.pallas_call(
        _expert_kernel,
        out_shape=jax.ShapeDtypeStruct((cap * SUBLANES, LANES), F32),
        grid_spec=pltpu.PrefetchScalarGridSpec(
            num_scalar_prefetch=2,
            grid=(n_blocks,),
            in_specs=[rows,
                      pl.BlockSpec((1, d, f), lambda j, be, nu: (be[j], 0, 0)),
                      pl.BlockSpec((1, d, f), lambda j, be, nu: (be[j], 0, 0)),
                      pl.BlockSpec((1, f, d), lambda j, be, nu: (be[j], 0, 0))],
            out_specs=rows,
            scratch_shapes=[pltpu.VMEM((d, f), BF16), pltpu.VMEM((d, f), BF16), pltpu.VMEM((f, d), BF16)]),
        compiler_params=_params("arbitrary"),
        name="moe_experts",
    )(block_e, n_used, xs, w1, w3, w2)


def _combine_kernel(dest_ref, w8_ref, x1_ref, h2_ref, g2_ref, ws1_ref, ws3_ref, ws2_ref, fg_ref, ys_hbm,
                    o_ref, dest_smem, gbuf, sem, *, tm):
    cp = pltpu.make_async_copy(dest_ref, dest_smem, sem)
    cp.start()
    cp.wait()

    def body(t, carry):
        for k in range(TOP_K):
            row = pl.multiple_of((k * tm + t) * SUBLANES, SUBLANES)
            pltpu.make_async_copy(ys_hbm.at[dest_smem[k, t]], gbuf.at[pl.ds(row, SUBLANES)], sem).start()
        return carry

    lax.fori_loop(0, tm, body, 0)

    h2 = _from_token_tiles(h2_ref, tm).astype(BF16)
    a = _dot(h2, ws1_ref[...])
    hid = (a * _sigmoid(a)) * _dot(h2, ws3_ref[...])
    acc = _dot(hid.astype(BF16), ws2_ref[...])

    _wait_rows(gbuf, sem, tm * TOP_K * SUBLANES)
    wt = w8_ref[...].T
    for k in range(TOP_K):
        acc = acc + wt[:, k:k + 1] * _from_token_tiles(gbuf, tm, row0=k * tm)
    y = x1_ref[...] + g2_ref[0] * acc
    o_ref[...] = y * lax.rsqrt(jnp.mean(y * y, axis=-1, keepdims=True) + EPS) * fg_ref[...]


def _combine(dest8, w8, x1, h2_tiles, g2, ws1, ws3, ws2, fg, ys, rows_per_batch, tm):
    n, d = x1.shape
    per = rows_per_batch // tm
    tok = pl.BlockSpec((TOP_K, tm), lambda i: (0, i))
    full = lambda a: pl.BlockSpec(a.shape, lambda i: (0, 0))
    return pl.pallas_call(
        functools.partial(_combine_kernel, tm=tm),
        out_shape=jax.ShapeDtypeStruct((n, d), F32),
        grid=(n // tm,),
        in_specs=[tok, tok, pl.BlockSpec((tm, d), lambda i: (i, 0)),
                  pl.BlockSpec((tm * SUBLANES, LANES), lambda i: (i, 0)),
                  pl.BlockSpec((1, 1, d), lambda i: (i // per, 0, 0)),
                  full(ws1), full(ws3), full(ws2), pl.BlockSpec((1, d), lambda i: (0, 0)),
                  pl.BlockSpec(memory_space=pl.ANY)],
        out_specs=pl.BlockSpec((tm, d), lambda i: (i, 0)),
        scratch_shapes=[pltpu.SMEM((TOP_K, tm), jnp.int32), pltpu.VMEM((TOP_K * tm * SUBLANES, LANES), F32),
                        pltpu.SemaphoreType.DMA],
        compiler_params=_params("arbitrary"),
        name="moe_combine_final",
    )(dest8, w8, x1, h2_tiles, g2, ws1, ws3, ws2, fg.reshape(1, d), ys)


def _moe_plan(e8, p8, counts, n_assign):
    cnt = counts.reshape(N_EXPERTS).astype(jnp.int32)
    padded = (cnt + MOE_BLK - 1) // MOE_BLK * MOE_BLK
    pends = jnp.cumsum(padded)
    pstarts = pends - padded
    n_blocks = (n_assign + N_EXPERTS * (MOE_BLK - 1) + MOE_BLK - 1) // MOE_BLK
    ids = jnp.arange(N_EXPERTS, dtype=jnp.int32)[:, None, None]
    dest8 = p8 + jnp.sum(jnp.where(e8[None] == ids, pstarts[:, None, None], 0), axis=0)
    blk_start = jnp.arange(n_blocks, dtype=jnp.int32) * MOE_BLK
    block_e = jnp.minimum(jnp.sum((blk_start[:, None] >= pends[None, :]).astype(jnp.int32), axis=1),
                          N_EXPERTS - 1).astype(jnp.int32)
    meta = jnp.stack([pstarts, cnt], axis=0).astype(jnp.int32)
    n_used = (pends[-1:] // MOE_BLK).astype(jnp.int32)
    return dest8, block_e, n_used, meta, n_blocks


def _mixer(x, c, ctx, c_ctx, w_ada, b_ada, norm1_g, norm2_g, w_in, s5_lam_re, s5_lam_im, s5_log_dt,
           s5_b_re, s5_b_im, s5_c_re, s5_c_im, s5_d, s5_w_glu, lb, hg_norm_g, p_a, p_b, w_out,
           moe_w_router, moe_b_router):
    b, l, d = x.shape
    lc = ctx.shape[1]
    n = b * l
    rows = l // GRID_W

    c8 = jnp.concatenate([c, c_ctx[None], jnp.zeros((8 - b - 1, d), F32)], axis=0)
    mod = _ada(c8, w_ada, b_ada)
    sh1, sc1, g1, sh2, sc2, g2 = [mod[:b, k * d:(k + 1) * d].reshape(b, 1, d) for k in range(6)]
    csh1, csc1 = mod[b:b + 1, 0:d].reshape(1, 1, d), mod[b:b + 1, d:2 * d].reshape(1, 1, d)

    w_in_b = w_in.astype(BF16)
    z = dict(zip([p[0] for p in _IN_PIECES], _inproj(x.reshape(n, d), sc1, sh1, norm1_g, w_in_b, l, 512)))
    zc = dict(zip([p[0] for p in _IN_PIECES], _inproj(ctx.reshape(b * lc, d), csc1, csh1, norm1_g, w_in_b, lc, lc)))

    cm = lambda t: t.reshape(b, rows, GRID_W * HG_WIDTH)
    cx = lambda t: t.reshape(b, lc, HG_WIDTH)
    lb_row = lb.reshape(1, HG_WIDTH)
    o_f = _hgrn_pass(cm(z["q"]), cm(z["ff"]), cm(z["i"]), cx(zc["ff"]), cx(zc["i"]), lb_row, None, None,
                     reverse=False)
    o_n = _hgrn_pass(cm(z["q"]), cm(z["fb"]), cm(z["i"]), cx(zc["fb"]), cx(zc["i"]), lb_row, o_f,
                     hg_norm_g.reshape(1, HG_DK), reverse=True)
    o_n = o_n.reshape(n, HG_WIDTH)

    d_lag, w_s5_in, w_out_f, w_out_b, decay = _s5_weights(s5_lam_re, s5_lam_im, s5_log_dt, s5_b_re, s5_b_im,
                                                          s5_c_re, s5_c_im)
    kc, kl = lc // S5_T, l // S5_T
    u_lat = z["u"].reshape(b, kl, S5_T * S5_WIDTH)
    u_ctx = zc["u"].reshape(b, kc, S5_T * S5_WIDTH)
    rows_in = kl + kc
    u_ext = jnp.concatenate([u_lat, u_ctx], axis=1).reshape(b * rows_in, S5_T * S5_WIDTH)
    e = _s5_in(u_ext, d_lag, w_s5_in, (b * rows_in) // 2)
    hf = _s5_scan(e, decay, b, rows_in, kl, 0)
    hb = _s5_scan(e, decay, b, rows_in, kl, 1)
    d_row = s5_d.astype(F32).reshape(1, S5_WIDTH)
    y_a = _s5_out(hf, hb, w_out_f, w_out_b, e.reshape(b, rows_in, -1), u_lat.reshape(b * kl, -1), d_row,
                  s5_w_glu.astype(BF16))
    y_a = y_a.reshape(n, S5_WIDTH)

    return _merge(x.reshape(n, d), y_a, o_n, z["go"], z["ga"], z["gb"], g1, sc2, sh2, norm2_g,
                  p_a.astype(BF16), p_b.astype(BF16), w_out.astype(BF16),
                  moe_w_router.T.astype(BF16), moe_b_router.astype(F32).reshape(N_EXPERTS, 1), l, 512) + (g2,)


def kernel(x, c, ctx, c_ctx, w_ada, b_ada, norm1_g, norm2_g, w_in, s5_lam_re, s5_lam_im, s5_log_dt, s5_b_re,
           s5_b_im, s5_c_re, s5_c_im, s5_d, s5_w_glu, hg_lb_logits, hg_norm_g, p_a, p_b, w_out, moe_w_router,
           moe_b_router, moe_w1, moe_w3, moe_w2, moe_ws1, moe_ws3, moe_ws2, final_norm_g):
    b, l, d = x.shape
    n = b * l
    assert w_ada.shape[0] == 1, "single-layer block"
    lb = jnp.cumsum(jax.nn.softmax(hg_lb_logits.astype(F32), axis=0), axis=0)[0]
    x1, h2_tiles, e8, p8, w8, counts, g2 = _mixer(
        x, c, ctx, c_ctx, w_ada[0], b_ada[0], norm1_g[0], norm2_g[0], w_in[0], s5_lam_re[0], s5_lam_im[0],
        s5_log_dt[0], s5_b_re[0], s5_b_im[0], s5_c_re[0], s5_c_im[0], s5_d[0], s5_w_glu[0], lb, hg_norm_g[0],
        p_a[0], p_b[0], w_out[0], moe_w_router[0], moe_b_router[0])
    dest8, block_e, n_used, meta, n_blocks = _moe_plan(e8, p8, counts, n * TOP_K)
    xs = _dispatch(meta, dest8, h2_tiles, n_blocks, 512)
    ys = _experts(block_e, n_used, xs.reshape(n_blocks * MOE_BLK * SUBLANES, LANES), moe_w1[0], moe_w3[0],
                  moe_w2[0])
    out = _combine(dest8, w8, x1, h2_tiles, g2, moe_ws1[0].astype(BF16), moe_ws3[0].astype(BF16),
                   moe_ws2[0].astype(BF16), final_norm_g, ys.reshape(n_blocks * MOE_BLK, SUBLANES, LANES), l, 256)
    return out.reshape(b, l, d)
```

```python
import functools
import math

import jax
import jax.numpy as jnp
from jax import lax
from jax.experimental import pallas as pl
from jax.experimental.pallas import tpu as pltpu

F32 = jnp.float32
BF16 = jnp.bfloat16

GRID_W = 64
S5_WIDTH = 256
S5_GROUP = 16
S5_GROUPS = 16
S5_STATE = 64
HG_HEADS = 6
HG_DK = 128
HG_WIDTH = HG_HEADS * HG_DK
N_EXPERTS = 64
ROUTE_GROUPS = 8
TOPK_GROUPS = 4
TOP_K = 8
ROUTED_SCALE = 2.5
EPS = 1e-6

LANES = 128
SUBLANES = 8

S5_T = 16
HG_CHUNK = 64
VMEM_LIMIT = 56 * 1024 * 1024

_NT = (((1,), (1,)), ((), ()))
_TN = (((0,), (0,)), ((), ()))


def _params(*sem):
    return pltpu.CompilerParams(dimension_semantics=sem, vmem_limit_bytes=VMEM_LIMIT)


def _dot(a, b):
    return jnp.dot(a, b, preferred_element_type=F32)


def _sigmoid(x):
    return 1.0 / (1.0 + jnp.exp(-x))


def _ada_kernel(c_ref, w_ref, b_ref, o_ref):
    c = c_ref[...]
    s = (c * _sigmoid(c)).astype(BF16)
    o_ref[...] = _dot(s, w_ref[...].astype(BF16)) + b_ref[...]


def _ada(c8, w_ada, b_ada):
    d, n = w_ada.shape
    tn = 1536
    return pl.pallas_call(
        _ada_kernel,
        out_shape=jax.ShapeDtypeStruct((8, n), F32),
        grid=(n // tn,),
        in_specs=[pl.BlockSpec((8, d), lambda j: (0, 0)),
                  pl.BlockSpec((d, tn), lambda j: (0, j)),
                  pl.BlockSpec((1, tn), lambda j: (0, j))],
        out_specs=pl.BlockSpec((8, tn), lambda j: (0, j)),
        compiler_params=_params("arbitrary"),
        name="ada_mod",
    )(c8, w_ada, b_ada.reshape(1, n))


_IN_PIECES = (("u", 0, 256, BF16), ("q", 256, 768, BF16), ("ff", 1024, 768, F32),
              ("fb", 1792, 768, F32), ("i", 2560, 768, BF16), ("go", 3328, 768, BF16),
              ("ga", 4096, 1024, BF16), ("gb", 5120, 1024, BF16))


def _inproj_kernel(x_ref, sc_ref, sh_ref, g_ref, w_ref, *o_refs):
    x = x_ref[...]
    y = x * lax.rsqrt(jnp.mean(x * x, axis=-1, keepdims=True) + EPS) * g_ref[...]
    h = (y * (1.0 + sc_ref[0]) + sh_ref[0]).astype(BF16)
    for (_, a, wd, _), o_ref in zip(_IN_PIECES, o_refs):
        o_ref[...] = _dot(h, w_ref[:, a:a + wd]).astype(o_ref.dtype)


def _inproj(x2d, sc, sh, g, w_bf16, rows_per_mod, tm):
    n, d = x2d.shape
    per = rows_per_mod // tm
    mod_map = (lambda i: (i // per, 0, 0)) if sc.shape[0] > 1 else (lambda i: (0, 0, 0))
    return pl.pallas_call(
        _inproj_kernel,
        out_shape=[jax.ShapeDtypeStruct((n, wd), dt) for (_, _, wd, dt) in _IN_PIECES],
        grid=(n // tm,),
        in_specs=[pl.BlockSpec((tm, d), lambda i: (i, 0)),
                  pl.BlockSpec((1, 1, d), mod_map),
                  pl.BlockSpec((1, 1, d), mod_map),
                  pl.BlockSpec((1, d), lambda i: (0, 0)),
                  pl.BlockSpec(w_bf16.shape, lambda i: (0, 0))],
        out_specs=[pl.BlockSpec((tm, wd), lambda i: (i, 0)) for (_, _, wd, _) in _IN_PIECES],
        compiler_params=_params("arbitrary"),
        name="in_proj",
    )(x2d, sc, sh, g.reshape(1, d), w_bf16)


def _hgrn_gates(zf, lb):
    sig = _sigmoid(zf)
    logf = jnp.log(lb + (1.0 - lb) * sig)
    k = (1.0 - lb) * (1.0 - sig)
    return logf, k


def _chunk_cumsum(cs, logf):
    hi = logf.astype(BF16)
    lo = (logf - hi.astype(F32)).astype(BF16)
    return _dot(cs, hi) + _dot(cs, lo)


def _hgrn_state_step(zf, v, lb, st, cs, reverse):
    logf, k = _hgrn_gates(zf, lb)
    cum = _chunk_cumsum(cs, logf)
    t = 0 if reverse else HG_CHUNK - 1
    total = cum[t:t + 1, :]
    kdec = (k * jnp.exp(total - cum)).astype(BF16)
    st_new = st * jnp.exp(total) + lax.dot_general(v.astype(BF16), kdec, _TN, preferred_element_type=F32)
    return cum, k, st_new


def _hgrn_chunk(q, zf, v, lb, st, mask, cs, reverse):
    cum, k, st_new = _hgrn_state_step(zf, v, lb, st, cs, reverse)
    r = HG_CHUNK // 2 - 1 if reverse else HG_CHUNK // 2
    ref = cum[r:r + 1, :]
    qi = (q * jnp.exp(cum - ref)).astype(BF16)
    ki = (k * jnp.exp(ref - cum)).astype(BF16)
    s = lax.dot_general(qi, ki, _NT, preferred_element_type=F32)
    s = jnp.where(mask, s, 0.0)
    o = _dot(s.astype(BF16), v.astype(BF16))
    q_in = (q * jnp.exp(cum)).astype(BF16)
    o = o + lax.dot_general(q_in, st.astype(BF16), _NT, preferred_element_type=F32)
    return o, st_new


def _hgrn_kernel(*refs, reverse, final, n_ctx_chunks):
    if final:
        q_ref, f_ref, v_ref, cf_ref, cv_ref, lb_ref, of_ref, g_ref, o_ref, st_ref = refs
    else:
        q_ref, f_ref, v_ref, cf_ref, cv_ref, lb_ref, o_ref, st_ref = refs
    c_len = HG_CHUNK
    row = lax.broadcasted_iota(jnp.int32, (c_len, c_len), 0)
    col = lax.broadcasted_iota(jnp.int32, (c_len, c_len), 1)
    mask = (col >= row) if reverse else (col <= row)
    cs = jnp.where(mask, 1.0, 0.0).astype(BF16)

    @pl.when(pl.program_id(1) == 0)
    def _():
        order = range(n_ctx_chunks - 1, -1, -1) if reverse else range(n_ctx_chunks)
        for h in range(HG_HEADS):
            cols = slice(h * HG_DK, (h + 1) * HG_DK)
            st = jnp.zeros((HG_DK, HG_DK), F32)
            for c in order:
                rows = slice(c * c_len, (c + 1) * c_len)
                _, _, st = _hgrn_state_step(cf_ref[0, rows, cols], cv_ref[0, rows, cols].astype(F32),
                                            lb_ref[:, cols], st, cs, reverse)
            st_ref[h] = st

    n_chunks = q_ref.shape[1] // c_len
    order = range(n_chunks - 1, -1, -1) if reverse else range(n_chunks)
    for h in range(HG_HEADS):
        cols = slice(h * HG_DK, (h + 1) * HG_DK)
        st = st_ref[h]
        for c in order:
            rows = slice(c * c_len, (c + 1) * c_len)
            o, st = _hgrn_chunk(q_ref[0, rows, cols].astype(F32), f_ref[0, rows, cols],
                                v_ref[0, rows, cols].astype(F32), lb_ref[:, cols], st, mask, cs, reverse)
            if final:
                o = o + of_ref[0, rows, cols]
                o = o * lax.rsqrt(jnp.mean(o * o, axis=-1, keepdims=True) + EPS) * g_ref[...]
            o_ref[0, rows, cols] = o.astype(o_ref.dtype)
        st_ref[h] = st


def _hgrn_pass(q, f, v, cf, cv, lb, o_prev, g, *, reverse):
    b, rows, _ = q.shape
    nw = GRID_W
    final = o_prev is not None
    wmap = (lambda bi, w: (bi, 0, nw - 1 - w)) if reverse else (lambda bi, w: (bi, 0, w))
    blk = pl.BlockSpec((1, rows, HG_WIDTH), wmap)
    cblk = pl.BlockSpec((1, cf.shape[1], HG_WIDTH), lambda bi, w: (bi, 0, 0))
    in_specs = [blk, blk, blk, cblk, cblk, pl.BlockSpec((1, HG_WIDTH), lambda bi, w: (0, 0))]
    args = [q, f, v, cf, cv, lb]
    if final:
        in_specs += [blk, pl.BlockSpec((1, HG_DK), lambda bi, w: (0, 0))]
        args += [o_prev, g]
    return pl.pallas_call(
        functools.partial(_hgrn_kernel, reverse=reverse, final=final, n_ctx_chunks=cf.shape[1] // HG_CHUNK),
        out_shape=jax.ShapeDtypeStruct(q.shape, BF16 if final else F32),
        grid=(b, nw),
        in_specs=in_specs,
        out_specs=blk,
        scratch_shapes=[pltpu.VMEM((HG_HEADS, HG_DK, HG_DK), F32)],
        compiler_params=_params("arbitrary", "arbitrary"),
        name="hgrn_bwd" if reverse else "hgrn_fwd",
    )(*args)


def _s5_weights(lam_re, lam_im, log_dt, b_re, b_im, c_re, c_im):
    hp = lax.Precision.HIGHEST
    g, p, cc, t = S5_GROUPS, S5_STATE, S5_GROUP, S5_T
    lre = jnp.minimum(lam_re.astype(F32), -1e-4)
    lim = lam_im.astype(F32)
    dt = jnp.exp(log_dt.astype(F32))[..., None]
    ks = jnp.arange(t + 1, dtype=F32)[:, None, None, None]
    mag = jnp.exp(ks * (lre * dt)[None])
    pw_re = mag * jnp.cos(ks * (lim * dt)[None])
    pw_im = mag * jnp.sin(ks * (lim * dt)[None])
    nr, ni = pw_re[1] - 1.0, pw_im[1]
    den = lre * lre + lim * lim
    cf_re = (nr * lre + ni * lim) / den
    cf_im = (ni * lre - nr * lim) / den
    bb_re = cf_re[..., None] * b_re - cf_im[..., None] * b_im
    bb_im = cf_re[..., None] * b_im + cf_im[..., None] * b_re
    cre, cim = c_re.astype(F32), c_im.astype(F32)
    sw, ns = S5_WIDTH, 2 * g * p
    grp_of_row = jnp.arange(sw)[:, None] // cc

    cp_re = cre[None, None, :, :, :] * pw_re[:t, :, :, None, :] - cim[None, None] * pw_im[:t, :, :, None, :]
    cp_im = cre[None, None, :, :, :] * pw_im[:t, :, :, None, :] + cim[None, None] * pw_re[:t, :, :, None, :]
    kk = (jnp.einsum("kdgop,dgpi->dkgoi", cp_re, bb_re, precision=hp)
          - jnp.einsum("kdgop,dgpi->dkgoi", cp_im, bb_im, precision=hp))
    kf, kb = kk[0], kk[1]
    kall = jnp.concatenate([kb[:0:-1], (kf[0] + kb[0])[None], kf[1:]], axis=0)
    kt = kall.transpose(0, 1, 3, 2).reshape(2 * t - 1, sw, cc)
    same = grp_of_row == (jnp.arange(sw)[None, :] // cc)
    d_lag = jnp.where(same[None], jnp.tile(kt, (1, 1, g)), 0.0).astype(BF16)

    same_in = grp_of_row == (jnp.arange(ns)[None, :] // (2 * p))

    def in_to_state(pre, pim, bre, bim):
        xre = pre[..., None] * bre[None] - pim[..., None] * bim[None]
        xim = pre[..., None] * bim[None] + pim[..., None] * bre[None]
        x = jnp.concatenate([xre.transpose(0, 1, 3, 2), xim.transpose(0, 1, 3, 2)], axis=-1)
        x = jnp.tile(x.reshape(t, sw, 2 * p), (1, 1, g))
        return jnp.where(same_in[None], x, 0.0).reshape(t * sw, ns)

    w_in = jnp.concatenate([in_to_state(pw_re[t - 1::-1, 0], pw_im[t - 1::-1, 0], bb_re[0], bb_im[0]),
                            in_to_state(pw_re[:t, 1], pw_im[:t, 1], bb_re[1], bb_im[1])], axis=1).astype(BF16)

    same_out = (jnp.arange(ns)[:, None] // (2 * p)) == ((jnp.arange(t * sw)[None, :] // cc) % g)

    def state_to_out(pre, pim):
        are = cre[None] * pre[:, :, None, :] - cim[None] * pim[:, :, None, :]
        aim = cre[None] * pim[:, :, None, :] + cim[None] * pre[:, :, None, :]
        a = jnp.concatenate([are, -aim], axis=-1).transpose(1, 3, 0, 2)
        a = jnp.broadcast_to(a.reshape(ns, t, 1, cc), (ns, t, g, cc)).reshape(ns, t * sw)
        return jnp.where(same_out, a, 0.0).astype(BF16)

    w_out_f = state_to_out(pw_re[1:, 0], pw_im[1:, 0])
    w_out_b = state_to_out(pw_re[t:0:-1, 1], pw_im[t:0:-1, 1])

    are_t, aim_t = pw_re[t], pw_im[t]
    a1 = jnp.concatenate([are_t, are_t], axis=-1).reshape(2, 1, ns)
    a2 = jnp.concatenate([-aim_t, aim_t], axis=-1).reshape(2, 1, ns)
    return d_lag, w_in, w_out_f, w_out_b, jnp.concatenate([a1, a2], axis=1)


def _s5_in_kernel(u_ref, d_ref, w_ref, o_ref):
    j = pl.program_id(0)

    @pl.when(j < S5_T)
    def _():
        acc = _dot(u_ref[:, 0:S5_WIDTH], d_ref[j + S5_T - 1])
        for s in range(1, S5_T):
            acc = acc + _dot(u_ref[:, s * S5_WIDTH:(s + 1) * S5_WIDTH], d_ref[j - s + S5_T - 1])
        o_ref[...] = acc

    @pl.when(j >= S5_T)
    def _():
        o_ref[...] = _dot(u_ref[...], w_ref[...])


def _s5_in(u, d_lag, w_in, tm):
    m, k = u.shape
    tn = S5_WIDTH
    nj = (k + w_in.shape[1]) // tn
    return pl.pallas_call(
        _s5_in_kernel,
        out_shape=jax.ShapeDtypeStruct((m, nj * tn), F32),
        grid=(nj, m // tm),
        in_specs=[pl.BlockSpec((tm, k), lambda j, i: (i, 0)),
                  pl.BlockSpec(d_lag.shape, lambda j, i: (0, 0, 0)),
                  pl.BlockSpec((k, tn), lambda j, i: (0, jnp.maximum(j - S5_T, 0)))],
        out_specs=pl.BlockSpec((tm, tn), lambda j, i: (i, j)),
        compiler_params=_params("arbitrary", "arbitrary"),
        name="s5_in",
    )(u, d_lag, w_in)


def _s5_scan_kernel(e_ref, a_ref, h_ref, *, nb, rows_in, rows_out, reverse):
    a1, a2 = a_ref[0, 0:1, :], a_ref[0, 1:2, :]
    tc = a1.shape[1]
    zero = jnp.zeros_like(a1)

    def swap_halves(v):
        return jnp.concatenate([pltpu.roll(v[:, k * LANES:(k + 1) * LANES], LANES // 2, axis=1)
                                for k in range(tc // LANES)], axis=1)

    def step(src, carry, store):
        new = []
        for bi in range(nb):
            h = carry[bi]
            if store:
                h_ref[pl.ds(bi * rows_out + src, 1), :] = h
            e = e_ref[pl.ds(bi * rows_in + src, 1), :]
            new.append(a1 * h + a2 * swap_halves(h) + e)
        return tuple(new)

    n_ctx = rows_in - rows_out
    ctx_src = (lambda s: rows_in - 1 - s) if reverse else (lambda s: rows_out + s)
    lat_src = (lambda s: rows_out - 1 - s) if reverse else (lambda s: s)
    carry = lax.fori_loop(0, n_ctx, lambda s, c: step(ctx_src(s), c, False), tuple([zero] * nb))
    lax.fori_loop(0, rows_out, lambda s, c: step(lat_src(s), c, True), carry)


def _s5_scan(e, decay, nb, rows_in, rows_out, direction):
    tc = 1024
    ns = 2 * S5_GROUPS * S5_STATE
    col0 = (S5_T * S5_WIDTH + direction * ns) // tc
    return pl.pallas_call(
        functools.partial(_s5_scan_kernel, nb=nb, rows_in=rows_in, rows_out=rows_out, reverse=direction == 1),
        out_shape=jax.ShapeDtypeStruct((nb * rows_out, ns), F32),
        grid=(ns // tc,),
        in_specs=[pl.BlockSpec((nb * rows_in, tc), lambda j: (0, col0 + j)),
                  pl.BlockSpec((1, 2, tc), lambda j: (direction, 0, j))],
        out_specs=pl.BlockSpec((nb * rows_out, tc), lambda j: (0, j)),
        compiler_params=_params("arbitrary"),
        name="s5_scan_bwd" if direction else "s5_scan_fwd",
    )(e, decay)


def _gelu_tanh(x):
    return 0.5 * x * (1.0 + jnp.tanh(math.sqrt(2.0 / math.pi) * (x + 0.044715 * x * x * x)))


def _s5_out_kernel(hf_ref, hb_ref, wf_ref, wb_ref, yi_ref, u_ref, d_ref, wg_ref, o_ref):
    y = (_dot(hf_ref[...].astype(BF16), wf_ref[...]) + _dot(hb_ref[...].astype(BF16), wb_ref[...])
         + yi_ref[0] + d_ref[...] * u_ref[...].astype(F32))
    y = _gelu_tanh(y)
    gate = _sigmoid(_dot(y.astype(BF16), wg_ref[...]))
    o_ref[...] = (y * gate).astype(o_ref.dtype)


def _s5_out(hf, hb, w_out_f, w_out_b, e3, u_rows, d_row, w_glu):
    m, ns = hf.shape
    nb = e3.shape[0]
    tm = m // nb
    tn = S5_WIDTH
    return pl.pallas_call(
        _s5_out_kernel,
        out_shape=jax.ShapeDtypeStruct((m, S5_T * S5_WIDTH), BF16),
        grid=(nb, S5_T),
        in_specs=[pl.BlockSpec((tm, ns), lambda i, j: (i, 0)),
                  pl.BlockSpec((tm, ns), lambda i, j: (i, 0)),
                  pl.BlockSpec((ns, tn), lambda i, j: (0, j)),
                  pl.BlockSpec((ns, tn), lambda i, j: (0, j)),
                  pl.BlockSpec((1, tm, tn), lambda i, j: (i, 0, j)),
                  pl.BlockSpec((tm, tn), lambda i, j: (i, j)),
                  pl.BlockSpec((1, tn), lambda i, j: (0, 0)),
                  pl.BlockSpec((tn, tn), lambda i, j: (0, 0))],
        out_specs=pl.BlockSpec((tm, tn), lambda i, j: (i, j)),
        compiler_params=_params("arbitrary", "arbitrary"),
        name="s5_out",
    )(hf, hb, w_out_f, w_out_b, e3, u_rows, d_row, w_glu)


U32 = jnp.uint32
ROW_SUB = 4


def _to_token_rows(ref, val):
    t, d = val.shape

    def rounded(x):
        u = lax.bitcast_convert_type(x, U32)
        return u + (jnp.uint32(0x7FFF) + ((u >> 16) & jnp.uint32(1)))

    w = (rounded(val[:, :d // 2]) >> 16) | (rounded(val[:, d // 2:]) & jnp.uint32(0xFFFF0000))
    for s in range(ROW_SUB):
        ref[pl.ds(s, t, stride=ROW_SUB), :] = w[:, s * LANES:(s + 1) * LANES]


def _from_token_rows(ref, t, row0=0):
    w = jnp.concatenate([ref[pl.ds(row0 * ROW_SUB + s, t, stride=ROW_SUB), :] for s in range(ROW_SUB)], axis=-1)
    lo = lax.bitcast_convert_type(w << 16, F32)
    hi = lax.bitcast_convert_type(w & jnp.uint32(0xFFFF0000), F32)
    return jnp.concatenate([lo, hi], axis=-1)


def _route(h2b, wr_ref, br_ref, cnt_ref, e8_ref, p8_ref, w8_ref):
    tm = h2b.shape[0]
    per_group = N_EXPERTS // ROUTE_GROUPS
    scores = _sigmoid(lax.dot_general(wr_ref[...], h2b, _NT, preferred_element_type=F32))
    biased = scores + br_ref[...]
    neg = -jnp.inf
    sub = lax.broadcasted_iota(jnp.int32, (per_group, tm), 0)
    grp = []
    for gi in range(ROUTE_GROUPS):
        v = biased[gi * per_group:(gi + 1) * per_group, :]
        m1 = jnp.max(v, axis=0, keepdims=True)
        first = jnp.min(jnp.where(v == m1, sub, per_group), axis=0, keepdims=True)
        m2 = jnp.max(jnp.where(sub == first, neg, v), axis=0, keepdims=True)
        grp.append(m1 + m2)
    grp = jnp.concatenate(grp, axis=0)
    gid = lax.broadcasted_iota(jnp.int32, (ROUTE_GROUPS, tm), 0)
    beaten = jnp.zeros((ROUTE_GROUPS, tm), jnp.int32)
    for gj in range(ROUTE_GROUPS):
        r = grp[gj:gj + 1, :]
        beaten = beaten + jnp.where((r > grp) | ((r == grp) & (gj < gid)), 1, 0)
    group_ok = beaten < TOPK_GROUPS
    expert_ok = jnp.concatenate(
        [jnp.broadcast_to(group_ok[gi:gi + 1, :], (per_group, tm)) for gi in range(ROUTE_GROUPS)], axis=0)
    cur = jnp.where(expert_ok, biased, neg)
    eid = lax.broadcasted_iota(jnp.int32, (N_EXPERTS, tm), 0)
    sel = jnp.zeros((N_EXPERTS, tm), F32)
    picks, wts = [], []
    for _ in range(TOP_K):
        m = jnp.max(cur, axis=0, keepdims=True)
        idx = jnp.min(jnp.where(cur == m, eid, N_EXPERTS), axis=0, keepdims=True)
        hit = eid == idx
        picks.append(idx)
        wts.append(jnp.sum(jnp.where(hit, scores, 0.0), axis=0, keepdims=True))
        sel = jnp.where(hit, 1.0, sel)
        cur = jnp.where(hit, neg, cur)
    wsum = wts[0]
    for w in wts[1:]:
        wsum = wsum + w
    ti = lax.broadcasted_iota(jnp.int32, (tm, tm), 0)
    tj = lax.broadcasted_iota(jnp.int32, (tm, tm), 1)
    before = jnp.where(ti < tj, 1.0, 0.0).astype(BF16)
    pos = cnt_ref[...] + _dot(sel.astype(BF16), before)
    cnt_ref[...] = cnt_ref[...] + jnp.sum(sel, axis=1, keepdims=True)
    for k in range(TOP_K):
        e8_ref[k:k + 1, :] = picks[k]
        w8_ref[k:k + 1, :] = wts[k] / wsum * ROUTED_SCALE
        p8_ref[k:k + 1, :] = jnp.sum(jnp.where(eid == picks[k], pos, 0.0), axis=0, keepdims=True).astype(jnp.int32)


def _merge_kernel(x_ref, ya_ref, on_ref, go_ref, ga_ref, gb_ref, g1_ref, sc_ref, sh_ref, n2_ref,
                  pa_ref, pb_ref, wo_ref, wr_ref, br_ref,
                  x1_ref, h2_ref, e8_ref, p8_ref, w8_ref, cnt_ref):
    @pl.when(pl.program_id(0) == 0)
    def _():
        cnt_ref[...] = jnp.zeros_like(cnt_ref)

    go = go_ref[...].astype(F32)
    y_b = (on_ref[...].astype(F32) * (go * _sigmoid(go))).astype(BF16)
    pa = _dot(ya_ref[...], pa_ref[...])
    pb = _dot(y_b, pb_ref[...])
    merged = _sigmoid(ga_ref[...].astype(F32)) * pa + _sigmoid(gb_ref[...].astype(F32)) * pb
    x1 = x_ref[...] + g1_ref[0] * _dot(merged.astype(BF16), wo_ref[...])
    x1_ref[...] = x1
    y = x1 * lax.rsqrt(jnp.mean(x1 * x1, axis=-1, keepdims=True) + EPS) * n2_ref[...]
    h2 = y * (1.0 + sc_ref[0]) + sh_ref[0]
    _to_token_rows(h2_ref, h2)
    _route(h2.astype(BF16), wr_ref, br_ref, cnt_ref, e8_ref, p8_ref, w8_ref)


def _merge(x2d, ya, on, go, ga, gb, g1, sc2, sh2, n2g, pa, pb, wo, wr_t, br, rows_per_batch, tm):
    n, d = x2d.shape
    per = rows_per_batch // tm
    row = lambda wd: pl.BlockSpec((tm, wd), lambda i: (i, 0))
    mod = pl.BlockSpec((1, 1, d), lambda i: (i // per, 0, 0))
    full = lambda a: pl.BlockSpec(a.shape, lambda i: (0, 0))
    tok = pl.BlockSpec((TOP_K, tm), lambda i: (0, i))
    return pl.pallas_call(
        _merge_kernel,
        out_shape=[jax.ShapeDtypeStruct((n, d), F32), jax.ShapeDtypeStruct((n * ROW_SUB, LANES), U32),
                   jax.ShapeDtypeStruct((TOP_K, n), jnp.int32), jax.ShapeDtypeStruct((TOP_K, n), jnp.int32),
                   jax.ShapeDtypeStruct((TOP_K, n), F32), jax.ShapeDtypeStruct((N_EXPERTS, 1), F32)],
        grid=(n // tm,),
        in_specs=[row(d), row(S5_WIDTH), row(HG_WIDTH), row(HG_WIDTH), row(d), row(d), mod, mod, mod,
                  pl.BlockSpec((1, d), lambda i: (0, 0)), full(pa), full(pb), full(wo), full(wr_t), full(br)],
        out_specs=[row(d), pl.BlockSpec((tm * ROW_SUB, LANES), lambda i: (i, 0)), tok, tok, tok,
                   pl.BlockSpec((N_EXPERTS, 1), lambda i: (0, 0))],
        compiler_params=_params("arbitrary"),
        name="merge_out_proj_route",
    )(x2d, ya, on, go, ga, gb, g1, sc2, sh2, n2g.reshape(1, d), pa, pb, wo, wr_t, br)


MOE_BLK = 256


def _token_row(ref, r):
    return ref.at[pl.ds(pl.multiple_of(r * ROW_SUB, ROW_SUB), ROW_SUB)]


def _wait_rows(any_ref, sem, n_rows):
    view = any_ref.at[pl.ds(0, n_rows * ROW_SUB)]
    pltpu.make_async_copy(view, view, sem).wait()


def _dispatch_kernel(meta_ref, dest_ref, h2_ref, xs_hbm, dest_smem, zrow, sem, zsem, *, tm, n_blocks):
    i = pl.program_id(0)
    cp = pltpu.make_async_copy(dest_ref, dest_smem, sem)
    cp.start()
    cp.wait()

    def body(t, carry):
        src = _token_row(h2_ref, t)
        for k in range(TOP_K):
            pltpu.make_async_copy(src, _token_row(xs_hbm, dest_smem[k, t]), sem).start(priority=k % 2)
        return carry

    lax.fori_loop(0, tm, body, 0)

    @pl.when(i == 0)
    def _():
        zrow[...] = jnp.zeros_like(zrow)
        total = n_blocks * MOE_BLK

        def per_expert(e, n_issued):
            lo = meta_ref[0, e] + meta_ref[1, e]
            hi = jnp.where(e == N_EXPERTS - 1, total, meta_ref[0, jnp.minimum(e + 1, N_EXPERTS - 1)])

            def pad(r, c):
                pltpu.make_async_copy(zrow, _token_row(xs_hbm, r), zsem).start()
                return c

            lax.fori_loop(lo, hi, pad, 0)
            return n_issued + (hi - lo)

        n_pad = lax.fori_loop(0, N_EXPERTS, per_expert, 0)

        def drain(r, c):
            pltpu.make_async_copy(zrow, _token_row(xs_hbm, 0), zsem).wait()
            return c

        lax.fori_loop(0, n_pad, drain, 0)

    _wait_rows(xs_hbm, sem, tm * TOP_K)


def _dispatch(meta, dest8, h2_rows, n_blocks, tm):
    n = dest8.shape[1]
    cap = n_blocks * MOE_BLK
    return pl.pallas_call(
        functools.partial(_dispatch_kernel, tm=tm, n_blocks=n_blocks),
        out_shape=jax.ShapeDtypeStruct((cap * ROW_SUB, LANES), U32),
        grid_spec=pltpu.PrefetchScalarGridSpec(
            num_scalar_prefetch=1,
            grid=(n // tm,),
            in_specs=[pl.BlockSpec((TOP_K, tm), lambda i, m: (0, i)),
                      pl.BlockSpec((tm * ROW_SUB, LANES), lambda i, m: (i, 0))],
            out_specs=pl.BlockSpec(memory_space=pl.ANY),
            scratch_shapes=[pltpu.SMEM((TOP_K, tm), jnp.int32), pltpu.VMEM((ROW_SUB, LANES), U32),
                            pltpu.SemaphoreType.DMA, pltpu.SemaphoreType.DMA]),
        compiler_params=pltpu.CompilerParams(dimension_semantics=("arbitrary",), vmem_limit_bytes=VMEM_LIMIT,
                                             has_side_effects=True),
        name="moe_dispatch",
    )(meta, dest8, h2_rows)


def _expert_kernel(be_ref, nu_ref, x_ref, w1_ref, w3_ref, w2_ref, o_ref, w1b, w3b, w2b):
    j = pl.program_id(0)
    e = be_ref[j]
    prev = be_ref[jnp.maximum(j - 1, 0)]
    used = j < nu_ref[0]

    @pl.when(jnp.logical_and(used, jnp.logical_or(j == 0, e != prev)))
    def _():
        w1b[...] = w1_ref[0].astype(BF16)
        w3b[...] = w3_ref[0].astype(BF16)
        w2b[...] = w2_ref[0].astype(BF16)

    @pl.when(used)
    def _():
        x = _from_token_rows(x_ref, MOE_BLK).astype(BF16)
        a = _dot(x, w1b[...])
        hid = (a * _sigmoid(a)) * _dot(x, w3b[...])
        _to_token_rows(o_ref, _dot(hid.astype(BF16), w2b[...]))

    @pl.when(jnp.logical_not(used))
    def _():
        o_ref[...] = jnp.zeros_like(o_ref)


def _experts(block_e, n_used, xs, w1, w3, w2):
    n_blocks = xs.shape[0] // (MOE_BLK * ROW_SUB)
    d, f = w1.shape[1], w1.shape[2]
    rows = pl.BlockSpec((MOE_BLK * ROW_SUB, LANES), lambda j, be, nu: (j, 0))
    return pl.pallas_call(
        _expert_kernel,
        out_shape=jax.ShapeDtypeStruct(xs.shape, U32),
        grid_spec=pltpu.PrefetchScalarGridSpec(
            num_scalar_prefetch=2,
            grid=(n_blocks,),
            in_specs=[rows,
                      pl.BlockSpec((1, d, f), lambda j, be, nu: (be[j], 0, 0)),
                      pl.BlockSpec((1, d, f), lambda j, be, nu: (be[j], 0, 0)),
                      pl.BlockSpec((1, f, d), lambda j, be, nu: (be[j], 0, 0))],
            out_specs=rows,
            scratch_shapes=[pltpu.VMEM((d, f), BF16), pltpu.VMEM((d, f), BF16), pltpu.VMEM((f, d), BF16)]),
        compiler_params=_params("arbitrary"),
        name="moe_experts",
    )(block_e, n_used, xs, w1, w3, w2)


def _combine_kernel(dest_ref, w8_ref, x1_ref, h2_ref, g2_ref, ws1_ref, ws3_ref, ws2_ref, fg_ref, ys_hbm,
                    o_ref, dest_smem, gbuf, sem, *, tm):
    cp = pltpu.make_async_copy(dest_ref, dest_smem, sem)
    cp.start()
    cp.wait()

    def body(t, carry):
        for k in range(TOP_K):
            pltpu.make_async_copy(_token_row(ys_hbm, dest_smem[k, t]), _token_row(gbuf, k * tm + t),
                                  sem).start(priority=k % 2)
        return carry

    lax.fori_loop(0, tm, body, 0)

    h2 = _from_token_rows(h2_ref, tm).astype(BF16)
    a = _dot(h2, ws1_ref[...])
    hid = (a * _sigmoid(a)) * _dot(h2, ws3_ref[...])
    acc = _dot(hid.astype(BF16), ws2_ref[...])

    _wait_rows(gbuf, sem, tm * TOP_K)
    wt = w8_ref[...].T
    for k in range(TOP_K):
        acc = acc + wt[:, k:k + 1] * _from_token_rows(gbuf, tm, row0=k * tm)
    y = x1_ref[...] + g2_ref[0] * acc
    o_ref[...] = y * lax.rsqrt(jnp.mean(y * y, axis=-1, keepdims=True) + EPS) * fg_ref[...]


def _combine(dest8, w8, x1, h2_tiles, g2, ws1, ws3, ws2, fg, ys, rows_per_batch, tm):
    n, d = x1.shape
    per = rows_per_batch // tm
    tok = pl.BlockSpec((TOP_K, tm), lambda i: (0, i))
    full = lambda a: pl.BlockSpec(a.shape, lambda i: (0, 0))
    return pl.pallas_call(
        functools.partial(_combine_kernel, tm=tm),
        out_shape=jax.ShapeDtypeStruct((n, d), F32),
        grid=(n // tm,),
        in_specs=[tok, tok, pl.BlockSpec((tm, d), lambda i: (i, 0)),
                  pl.BlockSpec((tm * ROW_SUB, LANES), lambda i: (i, 0)),
                  pl.BlockSpec((1, 1, d), lambda i: (i // per, 0, 0)),
                  full(ws1), full(ws3), full(ws2), pl.BlockSpec((1, d), lambda i: (0, 0)),
                  pl.BlockSpec(memory_space=pl.ANY)],
        out_specs=pl.BlockSpec((tm, d), lambda i: (i, 0)),
        scratch_shapes=[pltpu.SMEM((TOP_K, tm), jnp.int32), pltpu.VMEM((TOP_K * tm * ROW_SUB, LANES), U32),
                        pltpu.SemaphoreType.DMA],
        compiler_params=_params("arbitrary"),
        name="moe_combine_final",
    )(dest8, w8, x1, h2_tiles, g2, ws1, ws3, ws2, fg.reshape(1, d), ys)


def _moe_plan(e8, p8, counts, n_assign):
    cnt = counts.reshape(N_EXPERTS).astype(jnp.int32)
    padded = (cnt + MOE_BLK - 1) // MOE_BLK * MOE_BLK
    pends = jnp.cumsum(padded)
    pstarts = pends - padded
    n_blocks = (n_assign + N_EXPERTS * (MOE_BLK - 1) + MOE_BLK - 1) // MOE_BLK
    ids = jnp.arange(N_EXPERTS, dtype=jnp.int32)[:, None, None]
    dest8 = p8 + jnp.sum(jnp.where(e8[None] == ids, pstarts[:, None, None], 0), axis=0)
    blk_start = jnp.arange(n_blocks, dtype=jnp.int32) * MOE_BLK
    block_e = jnp.minimum(jnp.sum((blk_start[:, None] >= pends[None, :]).astype(jnp.int32), axis=1),
                          N_EXPERTS - 1).astype(jnp.int32)
    meta = jnp.stack([pstarts, cnt], axis=0).astype(jnp.int32)
    n_used = (pends[-1:] // MOE_BLK).astype(jnp.int32)
    return dest8, block_e, n_used, meta, n_blocks


def _mixer(x, c, ctx, c_ctx, w_ada, b_ada, norm1_g, norm2_g, w_in, s5_lam_re, s5_lam_im, s5_log_dt,
           s5_b_re, s5_b_im, s5_c_re, s5_c_im, s5_d, s5_w_glu, lb, hg_norm_g, p_a, p_b, w_out,
           moe_w_router, moe_b_router):
    b, l, d = x.shape
    lc = ctx.shape[1]
    n = b * l
    rows = l // GRID_W

    c8 = jnp.concatenate([c, c_ctx[None], jnp.zeros((8 - b - 1, d), F32)], axis=0)
    mod = _ada(c8, w_ada, b_ada)
    sh1, sc1, g1, sh2, sc2, g2 = [mod[:b, k * d:(k + 1) * d].reshape(b, 1, d) for k in range(6)]
    csh1, csc1 = mod[b:b + 1, 0:d].reshape(1, 1, d), mod[b:b + 1, d:2 * d].reshape(1, 1, d)

    w_in_b = w_in.astype(BF16)
    z = dict(zip([p[0] for p in _IN_PIECES], _inproj(x.reshape(n, d), sc1, sh1, norm1_g, w_in_b, l, 512)))
    zc = dict(zip([p[0] for p in _IN_PIECES], _inproj(ctx.reshape(b * lc, d), csc1, csh1, norm1_g, w_in_b, lc, lc)))

    cm = lambda t: t.reshape(b, rows, GRID_W * HG_WIDTH)
    cx = lambda t: t.reshape(b, lc, HG_WIDTH)
    lb_row = lb.reshape(1, HG_WIDTH)
    o_f = _hgrn_pass(cm(z["q"]), cm(z["ff"]), cm(z["i"]), cx(zc["ff"]), cx(zc["i"]), lb_row, None, None,
                     reverse=False)
    o_n = _hgrn_pass(cm(z["q"]), cm(z["fb"]), cm(z["i"]), cx(zc["fb"]), cx(zc["i"]), lb_row, o_f,
                     hg_norm_g.reshape(1, HG_DK), reverse=True)
    o_n = o_n.reshape(n, HG_WIDTH)

    d_lag, w_s5_in, w_out_f, w_out_b, decay = _s5_weights(s5_lam_re, s5_lam_im, s5_log_dt, s5_b_re, s5_b_im,
                                                          s5_c_re, s5_c_im)
    kc, kl = lc // S5_T, l // S5_T
    u_lat = z["u"].reshape(b, kl, S5_T * S5_WIDTH)
    u_ctx = zc["u"].reshape(b, kc, S5_T * S5_WIDTH)
    rows_in = kl + kc
    u_ext = jnp.concatenate([u_lat, u_ctx], axis=1).reshape(b * rows_in, S5_T * S5_WIDTH)
    e = _s5_in(u_ext, d_lag, w_s5_in, (b * rows_in) // 2)
    hf = _s5_scan(e, decay, b, rows_in, kl, 0)
    hb = _s5_scan(e, decay, b, rows_in, kl, 1)
    d_row = s5_d.astype(F32).reshape(1, S5_WIDTH)
    y_a = _s5_out(hf, hb, w_out_f, w_out_b, e.reshape(b, rows_in, -1), u_lat.reshape(b * kl, -1), d_row,
                  s5_w_glu.astype(BF16))
    y_a = y_a.reshape(n, S5_WIDTH)

    return _merge(x.reshape(n, d), y_a, o_n, z["go"], z["ga"], z["gb"], g1, sc2, sh2, norm2_g,
                  p_a.astype(BF16), p_b.astype(BF16), w_out.astype(BF16),
                  moe_w_router.T.astype(BF16), moe_b_router.astype(F32).reshape(N_EXPERTS, 1), l, 512) + (g2,)


def kernel(x, c, ctx, c_ctx, w_ada, b_ada, norm1_g, norm2_g, w_in, s5_lam_re, s5_lam_im, s5_log_dt, s5_b_re,
           s5_b_im, s5_c_re, s5_c_im, s5_d, s5_w_glu, hg_lb_logits, hg_norm_g, p_a, p_b, w_out, moe_w_router,
           moe_b_router, moe_w1, moe_w3, moe_w2, moe_ws1, moe_ws3, moe_ws2, final_norm_g):
    b, l, d = x.shape
    n = b * l
    assert w_ada.shape[0] == 1, "single-layer block"
    lb = jnp.cumsum(jax.nn.softmax(hg_lb_logits.astype(F32), axis=0), axis=0)[0]
    x1, h2_tiles, e8, p8, w8, counts, g2 = _mixer(
        x, c, ctx, c_ctx, w_ada[0], b_ada[0], norm1_g[0], norm2_g[0], w_in[0], s5_lam_re[0], s5_lam_im[0],
        s5_log_dt[0], s5_b_re[0], s5_b_im[0], s5_c_re[0], s5_c_im[0], s5_d[0], s5_w_glu[0], lb, hg_norm_g[0],
        p_a[0], p_b[0], w_out[0], moe_w_router[0], moe_b_router[0])
    dest8, block_e, n_used, meta, n_blocks = _moe_plan(e8, p8, counts, n * TOP_K)
    xs = _dispatch(meta, dest8, h2_tiles, n_blocks, 512)
    ys = _experts(block_e, n_used, xs, moe_w1[0], moe_w3[0], moe_w2[0])
    out = _combine(dest8, w8, x1, h2_tiles, g2, moe_ws1[0].astype(BF16), moe_ws3[0].astype(BF16),
                   moe_ws2[0].astype(BF16), final_norm_g, ys, l, 256)
    return out.reshape(b, l, d)
```

```python
import functools
import math

import jax
import jax.numpy as jnp
from jax import lax
from jax.experimental import pallas as pl
from jax.experimental.pallas import tpu as pltpu

F32 = jnp.float32
BF16 = jnp.bfloat16

GRID_W = 64
S5_WIDTH = 256
S5_GROUP = 16
S5_GROUPS = 16
S5_STATE = 64
HG_HEADS = 6
HG_DK = 128
HG_WIDTH = HG_HEADS * HG_DK
N_EXPERTS = 64
ROUTE_GROUPS = 8
TOPK_GROUPS = 4
TOP_K = 8
ROUTED_SCALE = 2.5
EPS = 1e-6

LANES = 128
SUBLANES = 8

S5_T = 16
HG_CHUNK = 64
VMEM_LIMIT = 56 * 1024 * 1024

_NT = (((1,), (1,)), ((), ()))
_TN = (((0,), (0,)), ((), ()))


def _params(*sem):
    return pltpu.CompilerParams(dimension_semantics=sem, vmem_limit_bytes=VMEM_LIMIT)


def _dot(a, b):
    return jnp.dot(a, b, preferred_element_type=F32)


def _sigmoid(x):
    return 1.0 / (1.0 + jnp.exp(-x))


def _ada_kernel(c_ref, w_ref, b_ref, o_ref):
    c = c_ref[...]
    s = (c * _sigmoid(c)).astype(BF16)
    o_ref[...] = _dot(s, w_ref[...].astype(BF16)) + b_ref[...]


def _ada(c8, w_ada, b_ada):
    d, n = w_ada.shape
    tn = 1536
    return pl.pallas_call(
        _ada_kernel,
        out_shape=jax.ShapeDtypeStruct((8, n), F32),
        grid=(n // tn,),
        in_specs=[pl.BlockSpec((8, d), lambda j: (0, 0)),
                  pl.BlockSpec((d, tn), lambda j: (0, j)),
                  pl.BlockSpec((1, tn), lambda j: (0, j))],
        out_specs=pl.BlockSpec((8, tn), lambda j: (0, j)),
        compiler_params=_params("arbitrary"),
        name="ada_mod",
    )(c8, w_ada, b_ada.reshape(1, n))


_IN_PIECES = (("u", 0, 256, BF16), ("q", 256, 768, BF16), ("ff", 1024, 768, F32),
              ("fb", 1792, 768, F32), ("i", 2560, 768, BF16), ("go", 3328, 768, BF16),
              ("ga", 4096, 1024, BF16), ("gb", 5120, 1024, BF16))


def _inproj_kernel(x_ref, sc_ref, sh_ref, g_ref, w_ref, *o_refs):
    x = x_ref[...]
    y = x * lax.rsqrt(jnp.mean(x * x, axis=-1, keepdims=True) + EPS) * g_ref[...]
    h = (y * (1.0 + sc_ref[0]) + sh_ref[0]).astype(BF16)
    for (_, a, wd, _), o_ref in zip(_IN_PIECES, o_refs):
        o_ref[...] = _dot(h, w_ref[:, a:a + wd]).astype(o_ref.dtype)


def _inproj(x2d, sc, sh, g, w_bf16, rows_per_mod, tm):
    n, d = x2d.shape
    per = rows_per_mod // tm
    mod_map = (lambda i: (i // per, 0, 0)) if sc.shape[0] > 1 else (lambda i: (0, 0, 0))
    return pl.pallas_call(
        _inproj_kernel,
        out_shape=[jax.ShapeDtypeStruct((n, wd), dt) for (_, _, wd, dt) in _IN_PIECES],
        grid=(n // tm,),
        in_specs=[pl.BlockSpec((tm, d), lambda i: (i, 0)),
                  pl.BlockSpec((1, 1, d), mod_map),
                  pl.BlockSpec((1, 1, d), mod_map),
                  pl.BlockSpec((1, d), lambda i: (0, 0)),
                  pl.BlockSpec(w_bf16.shape, lambda i: (0, 0))],
        out_specs=[pl.BlockSpec((tm, wd), lambda i: (i, 0)) for (_, _, wd, _) in _IN_PIECES],
        compiler_params=_params("arbitrary"),
        name="in_proj",
    )(x2d, sc, sh, g.reshape(1, d), w_bf16)


def _hgrn_gates(zf, lb):
    sig = _sigmoid(zf)
    logf = jnp.log(lb + (1.0 - lb) * sig)
    k = (1.0 - lb) * (1.0 - sig)
    return logf, k


def _chunk_cumsum(cs, logf):
    hi = logf.astype(BF16)
    lo = (logf - hi.astype(F32)).astype(BF16)
    return _dot(cs, hi) + _dot(cs, lo)


def _hgrn_state_step(zf, v, lb, st, cs, reverse):
    logf, k = _hgrn_gates(zf, lb)
    cum = _chunk_cumsum(cs, logf)
    t = 0 if reverse else HG_CHUNK - 1
    total = cum[t:t + 1, :]
    kdec = (k * jnp.exp(total - cum)).astype(BF16)
    st_new = st * jnp.exp(total) + lax.dot_general(v.astype(BF16), kdec, _TN, preferred_element_type=F32)
    return cum, k, st_new


def _hgrn_kernel(*refs, reverse, final, n_ctx_chunks):
    if final:
        q_ref, f_ref, v_ref, cf_ref, cv_ref, lb_ref, of_ref, g_ref, o_ref, st_ref = refs
    else:
        q_ref, f_ref, v_ref, cf_ref, cv_ref, lb_ref, o_ref, st_ref = refs
    c_len = HG_CHUNK
    n_rows = q_ref.shape[1]
    n_chunks = n_rows // c_len
    row = lax.broadcasted_iota(jnp.int32, (n_rows, n_rows), 0)
    col = lax.broadcasted_iota(jnp.int32, (n_rows, n_rows), 1)
    tri = (col >= row) if reverse else (col <= row)
    same_chunk = None
    for c in range(n_chunks):
        lo, hi = c * c_len, (c + 1) * c_len
        blk = (row >= lo) & (row < hi) & (col >= lo) & (col < hi)
        same_chunk = blk if same_chunk is None else (same_chunk | blk)
    mask = tri & same_chunk
    cs = jnp.where(mask, 1.0, 0.0).astype(BF16)

    @pl.when(pl.program_id(1) == 0)
    def _():
        cs1 = cs[:c_len, :c_len]
        order = range(n_ctx_chunks - 1, -1, -1) if reverse else range(n_ctx_chunks)
        for h in range(HG_HEADS):
            cols = slice(h * HG_DK, (h + 1) * HG_DK)
            st = jnp.zeros((HG_DK, HG_DK), F32)
            for c in order:
                rows = slice(c * c_len, (c + 1) * c_len)
                _, _, st = _hgrn_state_step(cf_ref[0, rows, cols], cv_ref[0, rows, cols].astype(F32),
                                            lb_ref[:, cols], st, cs1, reverse)
            st_ref[h] = st

    def per_chunk_rows(x, r):
        return jnp.concatenate([jnp.broadcast_to(x[c * c_len + r:c * c_len + r + 1, :], (c_len, x.shape[1]))
                                for c in range(n_chunks)], axis=0)

    lb = lb_ref[...]
    q = q_ref[0].astype(F32)
    v = v_ref[0]
    logf, k = _hgrn_gates(f_ref[0], lb)
    cum = _chunk_cumsum(cs, logf)
    r_ref = c_len // 2 - 1 if reverse else c_len // 2
    r_tot = 0 if reverse else c_len - 1
    ref = per_chunk_rows(cum, r_ref)
    qe = q * jnp.exp(cum - ref)
    ke = k * jnp.exp(ref - cum)
    qi, ki = qe.astype(BF16), ke.astype(BF16)
    q_in = (qe * jnp.exp(ref)).astype(BF16)
    tail = jnp.exp(per_chunk_rows(cum, r_tot) - ref)
    kdec = (ke * tail).astype(BF16)

    order = range(n_chunks - 1, -1, -1) if reverse else range(n_chunks)
    for h in range(HG_HEADS):
        cols = slice(h * HG_DK, (h + 1) * HG_DK)
        s = lax.dot_general(qi[:, cols], ki[:, cols], _NT, preferred_element_type=F32)
        o_intra = _dot(jnp.where(mask, s, 0.0).astype(BF16), v[:, cols])
        st = st_ref[h]
        for c in order:
            rows = slice(c * c_len, (c + 1) * c_len)
            o = o_intra[rows] + lax.dot_general(q_in[rows, cols], st.astype(BF16), _NT, preferred_element_type=F32)
            total = cum[c * c_len + r_tot:c * c_len + r_tot + 1, cols]
            st = st * jnp.exp(total) + lax.dot_general(v[rows, cols], kdec[rows, cols], _TN,
                                                       preferred_element_type=F32)
            if final:
                o = o + of_ref[0, rows, cols]
                o = o * lax.rsqrt(jnp.mean(o * o, axis=-1, keepdims=True) + EPS) * g_ref[...]
            o_ref[0, rows, cols] = o.astype(o_ref.dtype)
        st_ref[h] = st


def _hgrn_pass(q, f, v, cf, cv, lb, o_prev, g, *, reverse):
    b, rows, _ = q.shape
    nw = GRID_W
    final = o_prev is not None
    wmap = (lambda bi, w: (bi, 0, nw - 1 - w)) if reverse else (lambda bi, w: (bi, 0, w))
    blk = pl.BlockSpec((1, rows, HG_WIDTH), wmap)
    cblk = pl.BlockSpec((1, cf.shape[1], HG_WIDTH), lambda bi, w: (bi, 0, 0))
    in_specs = [blk, blk, blk, cblk, cblk, pl.BlockSpec((1, HG_WIDTH), lambda bi, w: (0, 0))]
    args = [q, f, v, cf, cv, lb]
    if final:
        in_specs += [blk, pl.BlockSpec((1, HG_DK), lambda bi, w: (0, 0))]
        args += [o_prev, g]
    return pl.pallas_call(
        functools.partial(_hgrn_kernel, reverse=reverse, final=final, n_ctx_chunks=cf.shape[1] // HG_CHUNK),
        out_shape=jax.ShapeDtypeStruct(q.shape, BF16 if final else F32),
        grid=(b, nw),
        in_specs=in_specs,
        out_specs=blk,
        scratch_shapes=[pltpu.VMEM((HG_HEADS, HG_DK, HG_DK), F32)],
        compiler_params=_params("arbitrary", "arbitrary"),
        name="hgrn_bwd" if reverse else "hgrn_fwd",
    )(*args)


def _s5_weights(lam_re, lam_im, log_dt, b_re, b_im, c_re, c_im):
    hp = lax.Precision.HIGHEST
    g, p, cc, t = S5_GROUPS, S5_STATE, S5_GROUP, S5_T
    lre = jnp.minimum(lam_re.astype(F32), -1e-4)
    lim = lam_im.astype(F32)
    dt = jnp.exp(log_dt.astype(F32))[..., None]
    ks = jnp.arange(t + 1, dtype=F32)[:, None, None, None]
    mag = jnp.exp(ks * (lre * dt)[None])
    pw_re = mag * jnp.cos(ks * (lim * dt)[None])
    pw_im = mag * jnp.sin(ks * (lim * dt)[None])
    nr, ni = pw_re[1] - 1.0, pw_im[1]
    den = lre * lre + lim * lim
    cf_re = (nr * lre + ni * lim) / den
    cf_im = (ni * lre - nr * lim) / den
    bb_re = cf_re[..., None] * b_re - cf_im[..., None] * b_im
    bb_im = cf_re[..., None] * b_im + cf_im[..., None] * b_re
    cre, cim = c_re.astype(F32), c_im.astype(F32)
    sw, ns = S5_WIDTH, 2 * g * p
    grp_of_row = jnp.arange(sw)[:, None] // cc

    cp_re = cre[None, None, :, :, :] * pw_re[:t, :, :, None, :] - cim[None, None] * pw_im[:t, :, :, None, :]
    cp_im = cre[None, None, :, :, :] * pw_im[:t, :, :, None, :] + cim[None, None] * pw_re[:t, :, :, None, :]
    kk = (jnp.einsum("kdgop,dgpi->dkgoi", cp_re, bb_re, precision=hp)
          - jnp.einsum("kdgop,dgpi->dkgoi", cp_im, bb_im, precision=hp))
    kf, kb = kk[0], kk[1]
    kall = jnp.concatenate([kb[:0:-1], (kf[0] + kb[0])[None], kf[1:]], axis=0)
    kt = kall.transpose(0, 1, 3, 2).reshape(2 * t - 1, sw, cc)
    same = grp_of_row == (jnp.arange(sw)[None, :] // cc)
    d_lag = jnp.where(same[None], jnp.tile(kt, (1, 1, g)), 0.0).astype(BF16)

    same_in = grp_of_row == ((jnp.arange(ns)[None, :] % (g * p)) // p)

    def in_to_state(pre, pim, bre, bim):
        xre = pre[..., None] * bre[None] - pim[..., None] * bim[None]
        xim = pre[..., None] * bim[None] + pim[..., None] * bre[None]
        x = jnp.concatenate([jnp.tile(xre.transpose(0, 1, 3, 2).reshape(t, sw, p), (1, 1, g)),
                             jnp.tile(xim.transpose(0, 1, 3, 2).reshape(t, sw, p), (1, 1, g))], axis=-1)
        return jnp.where(same_in[None], x, 0.0).reshape(t * sw, ns)

    w_in = jnp.concatenate([in_to_state(pw_re[t - 1::-1, 0], pw_im[t - 1::-1, 0], bb_re[0], bb_im[0]),
                            in_to_state(pw_re[:t, 1], pw_im[:t, 1], bb_re[1], bb_im[1])], axis=1).astype(BF16)

    same_out = ((jnp.arange(ns)[:, None] % (g * p)) // p) == ((jnp.arange(t * sw)[None, :] // cc) % g)

    def state_to_out(pre, pim):
        are = cre[None] * pre[:, :, None, :] - cim[None] * pim[:, :, None, :]
        aim = cre[None] * pim[:, :, None, :] + cim[None] * pre[:, :, None, :]
        a = jnp.concatenate([are.transpose(1, 3, 0, 2), -aim.transpose(1, 3, 0, 2)], axis=0)
        a = jnp.broadcast_to(a.reshape(ns, t, 1, cc), (ns, t, g, cc)).reshape(ns, t * sw)
        return jnp.where(same_out, a, 0.0).astype(BF16)

    w_out_f = state_to_out(pw_re[1:, 0], pw_im[1:, 0])
    w_out_b = state_to_out(pw_re[t:0:-1, 1], pw_im[t:0:-1, 1])

    decay = jnp.stack([pw_re[t].reshape(2, g * p), pw_im[t].reshape(2, g * p)], axis=1)
    return d_lag, w_in, w_out_f, w_out_b, decay


def _s5_in_kernel(u_ref, d_ref, w_ref, o_ref):
    j = pl.program_id(0)

    @pl.when(j < S5_T)
    def _():
        acc = _dot(u_ref[:, 0:S5_WIDTH], d_ref[j + S5_T - 1])
        for s in range(1, S5_T):
            acc = acc + _dot(u_ref[:, s * S5_WIDTH:(s + 1) * S5_WIDTH], d_ref[j - s + S5_T - 1])
        o_ref[...] = acc

    @pl.when(j >= S5_T)
    def _():
        o_ref[...] = _dot(u_ref[...], w_ref[...])


def _s5_in(u, d_lag, w_in, tm):
    m, k = u.shape
    tn = S5_WIDTH
    nj = (k + w_in.shape[1]) // tn
    return pl.pallas_call(
        _s5_in_kernel,
        out_shape=jax.ShapeDtypeStruct((m, nj * tn), F32),
        grid=(nj, m // tm),
        in_specs=[pl.BlockSpec((tm, k), lambda j, i: (i, 0)),
                  pl.BlockSpec(d_lag.shape, lambda j, i: (0, 0, 0)),
                  pl.BlockSpec((k, tn), lambda j, i: (0, jnp.maximum(j - S5_T, 0)))],
        out_specs=pl.BlockSpec((tm, tn), lambda j, i: (i, j)),
        compiler_params=_params("arbitrary", "arbitrary"),
        name="s5_in",
    )(u, d_lag, w_in)


def _s5_scan_kernel(efr_ref, efi_ref, ebr_ref, ebi_ref, a_ref, hfr_ref, hfi_ref, hbr_ref, hbi_ref,
                    *, nb, rows_in, rows_out):
    dirs = ((efr_ref, efi_ref, hfr_ref, hfi_ref, a_ref[0, 0:1, :], a_ref[0, 1:2, :]),
            (ebr_ref, ebi_ref, hbr_ref, hbi_ref, a_ref[1, 0:1, :], a_ref[1, 1:2, :]))
    zero = jnp.zeros_like(dirs[0][4])

    def step(srcs, carry, store):
        new = []
        for di, (er_ref, ei_ref, hr_ref, hi_ref, are, aim) in enumerate(dirs):
            for bi in range(nb):
                hre, him = carry[2 * (di * nb + bi)], carry[2 * (di * nb + bi) + 1]
                if store:
                    hr_ref[pl.ds(bi * rows_out + srcs[di], 1), :] = hre
                    hi_ref[pl.ds(bi * rows_out + srcs[di], 1), :] = him
                ere = er_ref[pl.ds(bi * rows_in + srcs[di], 1), :]
                eim = ei_ref[pl.ds(bi * rows_in + srcs[di], 1), :]
                new += [are * hre - aim * him + ere, are * him + aim * hre + eim]
        return tuple(new)

    n_ctx = rows_in - rows_out
    carry = lax.fori_loop(0, n_ctx, lambda s, c: step((rows_out + s, rows_in - 1 - s), c, False),
                          tuple([zero] * (4 * nb)))
    lax.fori_loop(0, rows_out, lambda s, c: step((s, rows_out - 1 - s), c, True), carry)


def _s5_scan(e, decay, nb, rows_in, rows_out):
    tc = 256
    nsr = S5_GROUPS * S5_STATE
    c0 = (S5_T * S5_WIDTH) // tc
    nt = nsr // tc
    eblk = lambda k: pl.BlockSpec((nb * rows_in, tc), lambda j: (0, c0 + k * nt + j))
    hblk = pl.BlockSpec((nb * rows_out, tc), lambda j: (0, j))
    return pl.pallas_call(
        functools.partial(_s5_scan_kernel, nb=nb, rows_in=rows_in, rows_out=rows_out),
        out_shape=[jax.ShapeDtypeStruct((nb * rows_out, nsr), F32)] * 4,
        grid=(nt,),
        in_specs=[eblk(0), eblk(1), eblk(2), eblk(3), pl.BlockSpec((2, 2, tc), lambda j: (0, 0, j))],
        out_specs=[hblk] * 4,
        compiler_params=_params("arbitrary"),
        name="s5_scan",
    )(e, e, e, e, decay)


def _gelu_tanh(x):
    return 0.5 * x * (1.0 + jnp.tanh(math.sqrt(2.0 / math.pi) * (x + 0.044715 * x * x * x)))


def _s5_out_kernel(hfr_ref, hfi_ref, hbr_ref, hbi_ref, wf_ref, wb_ref, yi_ref, u_ref, d_ref, wg_ref, o_ref):
    nsr = hfr_ref.shape[1]
    y = yi_ref[0] + d_ref[...] * u_ref[...].astype(F32)
    for h_ref, w_ref, r0 in ((hfr_ref, wf_ref, 0), (hfi_ref, wf_ref, nsr), (hbr_ref, wb_ref, 0), (hbi_ref, wb_ref, nsr)):
        y = y + _dot(h_ref[...].astype(BF16), w_ref[r0:r0 + nsr, :])
    y = _gelu_tanh(y)
    gate = _sigmoid(_dot(y.astype(BF16), wg_ref[...]))
    o_ref[...] = (y * gate).astype(o_ref.dtype)


def _s5_out(states, w_out_f, w_out_b, e3, u_rows, d_row, w_glu):
    m, nsr = states[0].shape
    nb = e3.shape[0]
    tm = m // nb
    tn = S5_WIDTH
    st = pl.BlockSpec((tm, nsr), lambda i, j: (i, 0))
    wo = pl.BlockSpec((2 * nsr, tn), lambda i, j: (0, j))
    return pl.pallas_call(
        _s5_out_kernel,
        out_shape=jax.ShapeDtypeStruct((m, S5_T * S5_WIDTH), BF16),
        grid=(nb, S5_T),
        in_specs=[st, st, st, st, wo, wo,
                  pl.BlockSpec((1, tm, tn), lambda i, j: (i, 0, j)),
                  pl.BlockSpec((tm, tn), lambda i, j: (i, j)),
                  pl.BlockSpec((1, tn), lambda i, j: (0, 0)),
                  pl.BlockSpec((tn, tn), lambda i, j: (0, 0))],
        out_specs=pl.BlockSpec((tm, tn), lambda i, j: (i, j)),
        compiler_params=_params("arbitrary", "arbitrary"),
        name="s5_out",
    )(*states, w_out_f, w_out_b, e3, u_rows, d_row, w_glu)


U32 = jnp.uint32
ROW_SUB = 4


def _to_token_rows(ref, val):
    t, d = val.shape

    def rounded(x):
        u = lax.bitcast_convert_type(x, U32)
        return u + (jnp.uint32(0x7FFF) + ((u >> 16) & jnp.uint32(1)))

    w = (rounded(val[:, :d // 2]) >> 16) | (rounded(val[:, d // 2:]) & jnp.uint32(0xFFFF0000))
    for s in range(ROW_SUB):
        ref[pl.ds(s, t, stride=ROW_SUB), :] = w[:, s * LANES:(s + 1) * LANES]


def _from_token_rows(ref, t, row0=0):
    w = jnp.concatenate([ref[pl.ds(row0 * ROW_SUB + s, t, stride=ROW_SUB), :] for s in range(ROW_SUB)], axis=-1)
    lo = lax.bitcast_convert_type(w << 16, F32)
    hi = lax.bitcast_convert_type(w & jnp.uint32(0xFFFF0000), F32)
    return jnp.concatenate([lo, hi], axis=-1)


def _route(h2b, wr_ref, br_ref, cnt_ref, e8_ref, p8_ref, w8_ref):
    tm = h2b.shape[0]
    per_group = N_EXPERTS // ROUTE_GROUPS
    scores = _sigmoid(lax.dot_general(wr_ref[...], h2b, _NT, preferred_element_type=F32))
    biased = scores + br_ref[...]
    neg = -jnp.inf
    sub = lax.broadcasted_iota(jnp.int32, (per_group, tm), 0)
    grp = []
    for gi in range(ROUTE_GROUPS):
        v = biased[gi * per_group:(gi + 1) * per_group, :]
        m1 = jnp.max(v, axis=0, keepdims=True)
        first = jnp.min(jnp.where(v == m1, sub, per_group), axis=0, keepdims=True)
        m2 = jnp.max(jnp.where(sub == first, neg, v), axis=0, keepdims=True)
        grp.append(m1 + m2)
    grp = jnp.concatenate(grp, axis=0)
    gid = lax.broadcasted_iota(jnp.int32, (ROUTE_GROUPS, tm), 0)
    beaten = jnp.zeros((ROUTE_GROUPS, tm), jnp.int32)
    for gj in range(ROUTE_GROUPS):
        r = grp[gj:gj + 1, :]
        beaten = beaten + jnp.where((r > grp) | ((r == grp) & (gj < gid)), 1, 0)
    group_ok = beaten < TOPK_GROUPS
    expert_ok = jnp.concatenate(
        [jnp.broadcast_to(group_ok[gi:gi + 1, :], (per_group, tm)) for gi in range(ROUTE_GROUPS)], axis=0)
    cur = jnp.where(expert_ok, biased, neg)
    eid = lax.broadcasted_iota(jnp.int32, (N_EXPERTS, tm), 0)
    sel = jnp.zeros((N_EXPERTS, tm), F32)
    picks, wts = [], []
    for _ in range(TOP_K):
        m = jnp.max(cur, axis=0, keepdims=True)
        idx = jnp.min(jnp.where(cur == m, eid, N_EXPERTS), axis=0, keepdims=True)
        hit = eid == idx
        picks.append(idx)
        wts.append(jnp.sum(jnp.where(hit, scores, 0.0), axis=0, keepdims=True))
        sel = jnp.where(hit, 1.0, sel)
        cur = jnp.where(hit, neg, cur)
    wsum = wts[0]
    for w in wts[1:]:
        wsum = wsum + w
    ti = lax.broadcasted_iota(jnp.int32, (tm, tm), 0)
    tj = lax.broadcasted_iota(jnp.int32, (tm, tm), 1)
    before = jnp.where(ti < tj, 1.0, 0.0).astype(BF16)
    pos = cnt_ref[...] + _dot(sel.astype(BF16), before)
    cnt_ref[...] = cnt_ref[...] + jnp.sum(sel, axis=1, keepdims=True)
    for k in range(TOP_K):
        e8_ref[k:k + 1, :] = picks[k]
        w8_ref[k:k + 1, :] = wts[k] / wsum * ROUTED_SCALE
        p8_ref[k:k + 1, :] = jnp.sum(jnp.where(eid == picks[k], pos, 0.0), axis=0, keepdims=True).astype(jnp.int32)


def _merge_kernel(x_ref, ya_ref, on_ref, go_ref, ga_ref, gb_ref, g1_ref, sc_ref, sh_ref, n2_ref,
                  pa_ref, pb_ref, wo_ref, wr_ref, br_ref,
                  x1_ref, h2_ref, e8_ref, p8_ref, w8_ref, cnt_ref):
    @pl.when(pl.program_id(0) == 0)
    def _():
        cnt_ref[...] = jnp.zeros_like(cnt_ref)

    go = go_ref[...].astype(F32)
    y_b = (on_ref[...].astype(F32) * (go * _sigmoid(go))).astype(BF16)
    pa = _dot(ya_ref[...], pa_ref[...])
    pb = _dot(y_b, pb_ref[...])
    merged = _sigmoid(ga_ref[...].astype(F32)) * pa + _sigmoid(gb_ref[...].astype(F32)) * pb
    x1 = x_ref[...] + g1_ref[0] * _dot(merged.astype(BF16), wo_ref[...])
    x1_ref[...] = x1
    y = x1 * lax.rsqrt(jnp.mean(x1 * x1, axis=-1, keepdims=True) + EPS) * n2_ref[...]
    h2 = y * (1.0 + sc_ref[0]) + sh_ref[0]
    _to_token_rows(h2_ref, h2)
    _route(h2.astype(BF16), wr_ref, br_ref, cnt_ref, e8_ref, p8_ref, w8_ref)


def _merge(x2d, ya, on, go, ga, gb, g1, sc2, sh2, n2g, pa, pb, wo, wr_t, br, rows_per_batch, tm):
    n, d = x2d.shape
    per = rows_per_batch // tm
    row = lambda wd: pl.BlockSpec((tm, wd), lambda i: (i, 0))
    mod = pl.BlockSpec((1, 1, d), lambda i: (i // per, 0, 0))
    full = lambda a: pl.BlockSpec(a.shape, lambda i: (0, 0))
    tok = pl.BlockSpec((TOP_K, tm), lambda i: (0, i))
    return pl.pallas_call(
        _merge_kernel,
        out_shape=[jax.ShapeDtypeStruct((n, d), F32), jax.ShapeDtypeStruct((n * ROW_SUB, LANES), U32),
                   jax.ShapeDtypeStruct((TOP_K, n), jnp.int32), jax.ShapeDtypeStruct((TOP_K, n), jnp.int32),
                   jax.ShapeDtypeStruct((TOP_K, n), F32), jax.ShapeDtypeStruct((N_EXPERTS, 1), F32)],
        grid=(n // tm,),
        in_specs=[row(d), row(S5_WIDTH), row(HG_WIDTH), row(HG_WIDTH), row(d), row(d), mod, mod, mod,
                  pl.BlockSpec((1, d), lambda i: (0, 0)), full(pa), full(pb), full(wo), full(wr_t), full(br)],
        out_specs=[row(d), pl.BlockSpec((tm * ROW_SUB, LANES), lambda i: (i, 0)), tok, tok, tok,
                   pl.BlockSpec((N_EXPERTS, 1), lambda i: (0, 0))],
        compiler_params=_params("arbitrary"),
        name="merge_out_proj_route",
    )(x2d, ya, on, go, ga, gb, g1, sc2, sh2, n2g.reshape(1, d), pa, pb, wo, wr_t, br)


MOE_BLK = 512


def _token_row(ref, r):
    return ref.at[pl.ds(pl.multiple_of(r * ROW_SUB, ROW_SUB), ROW_SUB)]


def _wait_rows(any_ref, sem, n_rows):
    view = any_ref.at[pl.ds(0, n_rows * ROW_SUB)]
    pltpu.make_async_copy(view, view, sem).wait()


def _dispatch_kernel(meta_ref, dest_ref, h2_ref, xs_hbm, dest_smem, zrow, sem, zsem, *, tm, n_blocks):
    i = pl.program_id(0)
    cp = pltpu.make_async_copy(dest_ref, dest_smem, sem)
    cp.start()
    cp.wait()

    def body(t, carry):
        src = _token_row(h2_ref, t)
        for k in range(TOP_K):
            pltpu.make_async_copy(src, _token_row(xs_hbm, dest_smem[k, t]), sem).start(priority=k % 2)
        return carry

    lax.fori_loop(0, tm, body, 0)

    @pl.when(i == 0)
    def _():
        zrow[...] = jnp.zeros_like(zrow)
        total = n_blocks * MOE_BLK

        def per_expert(e, n_issued):
            lo = meta_ref[0, e] + meta_ref[1, e]
            hi = jnp.where(e == N_EXPERTS - 1, total, meta_ref[0, jnp.minimum(e + 1, N_EXPERTS - 1)])

            def pad(r, c):
                pltpu.make_async_copy(zrow, _token_row(xs_hbm, r), zsem).start()
                return c

            lax.fori_loop(lo, hi, pad, 0)
            return n_issued + (hi - lo)

        n_pad = lax.fori_loop(0, N_EXPERTS, per_expert, 0)

        def drain(r, c):
            pltpu.make_async_copy(zrow, _token_row(xs_hbm, 0), zsem).wait()
            return c

        lax.fori_loop(0, n_pad, drain, 0)

    _wait_rows(xs_hbm, sem, tm * TOP_K)


def _dispatch(meta, dest8, h2_rows, n_blocks, tm):
    n = dest8.shape[1]
    cap = n_blocks * MOE_BLK
    return pl.pallas_call(
        functools.partial(_dispatch_kernel, tm=tm, n_blocks=n_blocks),
        out_shape=jax.ShapeDtypeStruct((cap * ROW_SUB, LANES), U32),
        grid_spec=pltpu.PrefetchScalarGridSpec(
            num_scalar_prefetch=1,
            grid=(n // tm,),
            in_specs=[pl.BlockSpec((TOP_K, tm), lambda i, m: (0, i)),
                      pl.BlockSpec((tm * ROW_SUB, LANES), lambda i, m: (i, 0))],
            out_specs=pl.BlockSpec(memory_space=pl.ANY),
            scratch_shapes=[pltpu.SMEM((TOP_K, tm), jnp.int32), pltpu.VMEM((ROW_SUB, LANES), U32),
                            pltpu.SemaphoreType.DMA, pltpu.SemaphoreType.DMA]),
        compiler_params=pltpu.CompilerParams(dimension_semantics=("arbitrary",), vmem_limit_bytes=VMEM_LIMIT,
                                             has_side_effects=True),
        name="moe_dispatch",
    )(meta, dest8, h2_rows)


def _expert_kernel(be_ref, nu_ref, x_ref, w1_ref, w3_ref, w2_ref, o_ref, w1b, w3b, w2b):
    j = pl.program_id(0)
    e = be_ref[j]
    prev = be_ref[jnp.maximum(j - 1, 0)]
    used = j < nu_ref[0]

    @pl.when(jnp.logical_and(used, jnp.logical_or(j == 0, e != prev)))
    def _():
        w1b[...] = w1_ref[0].astype(BF16)
        w3b[...] = w3_ref[0].astype(BF16)
        w2b[...] = w2_ref[0].astype(BF16)

    @pl.when(used)
    def _():
        x = _from_token_rows(x_ref, MOE_BLK).astype(BF16)
        a = _dot(x, w1b[...])
        hid = (a * _sigmoid(a)) * _dot(x, w3b[...])
        _to_token_rows(o_ref, _dot(hid.astype(BF16), w2b[...]))

    @pl.when(jnp.logical_not(used))
    def _():
        o_ref[...] = jnp.zeros_like(o_ref)


def _experts(block_e, n_used, xs, w1, w3, w2):
    n_blocks = xs.shape[0] // (MOE_BLK * ROW_SUB)
    d, f = w1.shape[1], w1.shape[2]
    rows = pl.BlockSpec((MOE_BLK * ROW_SUB, LANES), lambda j, be, nu: (j, 0))
    return pl.pallas_call(
        _expert_kernel,
        out_shape=jax.ShapeDtypeStruct(xs.shape, U32),
        grid_spec=pltpu.PrefetchScalarGridSpec(
            num_scalar_prefetch=2,
            grid=(n_blocks,),
            in_specs=[rows,
                      pl.BlockSpec((1, d, f), lambda j, be, nu: (be[j], 0, 0)),
                      pl.BlockSpec((1, d, f), lambda j, be, nu: (be[j], 0, 0)),
                      pl.BlockSpec((1, f, d), lambda j, be, nu: (be[j], 0, 0))],
            out_specs=rows,
            scratch_shapes=[pltpu.VMEM((d, f), BF16), pltpu.VMEM((d, f), BF16), pltpu.VMEM((f, d), BF16)]),
        compiler_params=_params("arbitrary"),
        name="moe_experts",
    )(block_e, n_used, xs, w1, w3, w2)


def _combine_kernel(dest_ref, w8_ref, x1_ref, h2_ref, g2_ref, ws1_ref, ws3_ref, ws2_ref, fg_ref, ys_hbm,
                    o_ref, dest_smem, gbuf, sem, *, tm):
    cp = pltpu.make_async_copy(dest_ref, dest_smem, sem)
    cp.start()
    cp.wait()

    def body(t, carry):
        for k in range(TOP_K):
            pltpu.make_async_copy(_token_row(ys_hbm, dest_smem[k, t]), _token_row(gbuf, k * tm + t),
                                  sem).start(priority=k % 2)
        return carry

    lax.fori_loop(0, tm, body, 0)

    h2 = _from_token_rows(h2_ref, tm).astype(BF16)
    a = _dot(h2, ws1_ref[...])
    hid = (a * _sigmoid(a)) * _dot(h2, ws3_ref[...])
    acc = _dot(hid.astype(BF16), ws2_ref[...])

    _wait_rows(gbuf, sem, tm * TOP_K)
    wt = w8_ref[...].T
    for k in range(TOP_K):
        acc = acc + wt[:, k:k + 1] * _from_token_rows(gbuf, tm, row0=k * tm)
    y = x1_ref[...] + g2_ref[0] * acc
    o_ref[...] = y * lax.rsqrt(jnp.mean(y * y, axis=-1, keepdims=True) + EPS) * fg_ref[...]


def _combine(dest8, w8, x1, h2_tiles, g2, ws1, ws3, ws2, fg, ys, rows_per_batch, tm):
    n, d = x1.shape
    per = rows_per_batch // tm
    tok = pl.BlockSpec((TOP_K, tm), lambda i: (0, i))
    full = lambda a: pl.BlockSpec(a.shape, lambda i: (0, 0))
    return pl.pallas_call(
        functools.partial(_combine_kernel, tm=tm),
        out_shape=jax.ShapeDtypeStruct((n, d), F32),
        grid=(n // tm,),
        in_specs=[tok, tok, pl.BlockSpec((tm, d), lambda i: (i, 0)),
                  pl.BlockSpec((tm * ROW_SUB, LANES), lambda i: (i, 0)),
                  pl.BlockSpec((1, 1, d), lambda i: (i // per, 0, 0)),
                  full(ws1), full(ws3), full(ws2), pl.BlockSpec((1, d), lambda i: (0, 0)),
                  pl.BlockSpec(memory_space=pl.ANY)],
        out_specs=pl.BlockSpec((tm, d), lambda i: (i, 0)),
        scratch_shapes=[pltpu.SMEM((TOP_K, tm), jnp.int32), pltpu.VMEM((TOP_K * tm * ROW_SUB, LANES), U32),
                        pltpu.SemaphoreType.DMA],
        compiler_params=_params("arbitrary"),
        name="moe_combine_final",
    )(dest8, w8, x1, h2_tiles, g2, ws1, ws3, ws2, fg.reshape(1, d), ys)


def _moe_plan(e8, p8, counts, n_assign):
    cnt = counts.reshape(N_EXPERTS).astype(jnp.int32)
    padded = (cnt + MOE_BLK - 1) // MOE_BLK * MOE_BLK
    pends = jnp.cumsum(padded)
    pstarts = pends - padded
    n_blocks = (n_assign + N_EXPERTS * (MOE_BLK - 1) + MOE_BLK - 1) // MOE_BLK
    ids = jnp.arange(N_EXPERTS, dtype=jnp.int32)[:, None, None]
    dest8 = p8 + jnp.sum(jnp.where(e8[None] == ids, pstarts[:, None, None], 0), axis=0)
    blk_start = jnp.arange(n_blocks, dtype=jnp.int32) * MOE_BLK
    block_e = jnp.minimum(jnp.sum((blk_start[:, None] >= pends[None, :]).astype(jnp.int32), axis=1),
                          N_EXPERTS - 1).astype(jnp.int32)
    meta = jnp.stack([pstarts, cnt], axis=0).astype(jnp.int32)
    n_used = (pends[-1:] // MOE_BLK).astype(jnp.int32)
    return dest8, block_e, n_used, meta, n_blocks


def _mixer(x, c, ctx, c_ctx, w_ada, b_ada, norm1_g, norm2_g, w_in, s5_lam_re, s5_lam_im, s5_log_dt,
           s5_b_re, s5_b_im, s5_c_re, s5_c_im, s5_d, s5_w_glu, lb, hg_norm_g, p_a, p_b, w_out,
           moe_w_router, moe_b_router):
    b, l, d = x.shape
    lc = ctx.shape[1]
    n = b * l
    rows = l // GRID_W

    c8 = jnp.concatenate([c, c_ctx[None], jnp.zeros((8 - b - 1, d), F32)], axis=0)
    mod = _ada(c8, w_ada, b_ada)
    sh1, sc1, g1, sh2, sc2, g2 = [mod[:b, k * d:(k + 1) * d].reshape(b, 1, d) for k in range(6)]
    csh1, csc1 = mod[b:b + 1, 0:d].reshape(1, 1, d), mod[b:b + 1, d:2 * d].reshape(1, 1, d)

    w_in_b = w_in.astype(BF16)
    z = dict(zip([p[0] for p in _IN_PIECES], _inproj(x.reshape(n, d), sc1, sh1, norm1_g, w_in_b, l, 512)))
    zc = dict(zip([p[0] for p in _IN_PIECES], _inproj(ctx.reshape(b * lc, d), csc1, csh1, norm1_g, w_in_b, lc, lc)))

    cm = lambda t: t.reshape(b, rows, GRID_W * HG_WIDTH)
    cx = lambda t: t.reshape(b, lc, HG_WIDTH)
    lb_row = lb.reshape(1, HG_WIDTH)
    o_f = _hgrn_pass(cm(z["q"]), cm(z["ff"]), cm(z["i"]), cx(zc["ff"]), cx(zc["i"]), lb_row, None, None,
                     reverse=False)
    o_n = _hgrn_pass(cm(z["q"]), cm(z["fb"]), cm(z["i"]), cx(zc["fb"]), cx(zc["i"]), lb_row, o_f,
                     hg_norm_g.reshape(1, HG_DK), reverse=True)
    o_n = o_n.reshape(n, HG_WIDTH)

    d_lag, w_s5_in, w_out_f, w_out_b, decay = _s5_weights(s5_lam_re, s5_lam_im, s5_log_dt, s5_b_re, s5_b_im,
                                                          s5_c_re, s5_c_im)
    kc, kl = lc // S5_T, l // S5_T
    u_lat = z["u"].reshape(b, kl, S5_T * S5_WIDTH)
    u_ctx = zc["u"].reshape(b, kc, S5_T * S5_WIDTH)
    rows_in = kl + kc
    u_ext = jnp.concatenate([u_lat, u_ctx], axis=1).reshape(b * rows_in, S5_T * S5_WIDTH)
    e = _s5_in(u_ext, d_lag, w_s5_in, (b * rows_in) // 2)
    states = _s5_scan(e, decay, b, rows_in, kl)
    d_row = s5_d.astype(F32).reshape(1, S5_WIDTH)
    y_a = _s5_out(states, w_out_f, w_out_b, e.reshape(b, rows_in, -1), u_lat.reshape(b * kl, -1), d_row,
                  s5_w_glu.astype(BF16))
    y_a = y_a.reshape(n, S5_WIDTH)

    return _merge(x.reshape(n, d), y_a, o_n, z["go"], z["ga"], z["gb"], g1, sc2, sh2, norm2_g,
                  p_a.astype(BF16), p_b.astype(BF16), w_out.astype(BF16),
                  moe_w_router.T.astype(BF16), moe_b_router.astype(F32).reshape(N_EXPERTS, 1), l, 512) + (g2,)


def kernel(x, c, ctx, c_ctx, w_ada, b_ada, norm1_g, norm2_g, w_in, s5_lam_re, s5_lam_im, s5_log_dt, s5_b_re,
           s5_b_im, s5_c_re, s5_c_im, s5_d, s5_w_glu, hg_lb_logits, hg_norm_g, p_a, p_b, w_out, moe_w_router,
           moe_b_router, moe_w1, moe_w3, moe_w2, moe_ws1, moe_ws3, moe_ws2, final_norm_g):
    b, l, d = x.shape
    n = b * l
    assert w_ada.shape[0] == 1, "single-layer block"
    lb = jnp.cumsum(jax.nn.softmax(hg_lb_logits.astype(F32), axis=0), axis=0)[0]
    x1, h2_tiles, e8, p8, w8, counts, g2 = _mixer(
        x, c, ctx, c_ctx, w_ada[0], b_ada[0], norm1_g[0], norm2_g[0], w_in[0], s5_lam_re[0], s5_lam_im[0],
        s5_log_dt[0], s5_b_re[0], s5_b_im[0], s5_c_re[0], s5_c_im[0], s5_d[0], s5_w_glu[0], lb, hg_norm_g[0],
        p_a[0], p_b[0], w_out[0], moe_w_router[0], moe_b_router[0])
    dest8, block_e, n_used, meta, n_blocks = _moe_plan(e8, p8, counts, n * TOP_K)
    xs = _dispatch(meta, dest8, h2_tiles, n_blocks, 512)
    ys = _experts(block_e, n_used, xs, moe_w1[0], moe_w3[0], moe_w2[0])
    out = _combine(dest8, w8, x1, h2_tiles, g2, moe_ws1[0].astype(BF16), moe_ws3[0].astype(BF16),
                   moe_ws2[0].astype(BF16), final_norm_g, ys, l, 256)
    return out.reshape(b, l, d)
```

```python
import functools
import math

import jax
import jax.numpy as jnp
from jax import lax
from jax.experimental import pallas as pl
from jax.experimental.pallas import tpu as pltpu

F32 = jnp.float32
BF16 = jnp.bfloat16

GRID_W = 64
S5_WIDTH = 256
S5_GROUP = 16
S5_GROUPS = 16
S5_STATE = 64
HG_HEADS = 6
HG_DK = 128
HG_WIDTH = HG_HEADS * HG_DK
N_EXPERTS = 64
ROUTE_GROUPS = 8
TOPK_GROUPS = 4
TOP_K = 8
ROUTED_SCALE = 2.5
EPS = 1e-6

LANES = 128
SUBLANES = 8

S5_T = 16
HG_CHUNK = 64
VMEM_LIMIT = 56 * 1024 * 1024

_NT = (((1,), (1,)), ((), ()))
_TN = (((0,), (0,)), ((), ()))


def _params(*sem):
    return pltpu.CompilerParams(dimension_semantics=sem, vmem_limit_bytes=VMEM_LIMIT)


def _dot(a, b):
    return jnp.dot(a, b, preferred_element_type=F32)


def _sigmoid(x):
    return 1.0 / (1.0 + jnp.exp(-x))


def _ada_kernel(c_ref, w_ref, b_ref, o_ref):
    c = c_ref[...]
    s = (c * _sigmoid(c)).astype(BF16)
    o_ref[...] = _dot(s, w_ref[...].astype(BF16)) + b_ref[...]


def _ada(c8, w_ada, b_ada):
    d, n = w_ada.shape
    tn = 1536
    return pl.pallas_call(
        _ada_kernel,
        out_shape=jax.ShapeDtypeStruct((8, n), F32),
        grid=(n // tn,),
        in_specs=[pl.BlockSpec((8, d), lambda j: (0, 0)),
                  pl.BlockSpec((d, tn), lambda j: (0, j)),
                  pl.BlockSpec((1, tn), lambda j: (0, j))],
        out_specs=pl.BlockSpec((8, tn), lambda j: (0, j)),
        compiler_params=_params("arbitrary"),
        name="ada_mod",
    )(c8, w_ada, b_ada.reshape(1, n))


_IN_PIECES = (("u", 0, 256, BF16), ("q", 256, 768, BF16), ("ff", 1024, 768, F32),
              ("fb", 1792, 768, F32), ("i", 2560, 768, BF16), ("go", 3328, 768, BF16),
              ("ga", 4096, 1024, BF16), ("gb", 5120, 1024, BF16))


def _inproj_kernel(x_ref, sc_ref, sh_ref, g_ref, w_ref, *o_refs):
    x = x_ref[...]
    y = x * lax.rsqrt(jnp.mean(x * x, axis=-1, keepdims=True) + EPS) * g_ref[...]
    h = (y * (1.0 + sc_ref[0]) + sh_ref[0]).astype(BF16)
    for (_, a, wd, _), o_ref in zip(_IN_PIECES, o_refs):
        o_ref[...] = _dot(h, w_ref[:, a:a + wd]).astype(o_ref.dtype)


def _inproj(x2d, sc, sh, g, w_bf16, rows_per_mod, tm):
    n, d = x2d.shape
    per = rows_per_mod // tm
    mod_map = (lambda i: (i // per, 0, 0)) if sc.shape[0] > 1 else (lambda i: (0, 0, 0))
    return pl.pallas_call(
        _inproj_kernel,
        out_shape=[jax.ShapeDtypeStruct((n, wd), dt) for (_, _, wd, dt) in _IN_PIECES],
        grid=(n // tm,),
        in_specs=[pl.BlockSpec((tm, d), lambda i: (i, 0)),
                  pl.BlockSpec((1, 1, d), mod_map),
                  pl.BlockSpec((1, 1, d), mod_map),
                  pl.BlockSpec((1, d), lambda i: (0, 0)),
                  pl.BlockSpec(w_bf16.shape, lambda i: (0, 0))],
        out_specs=[pl.BlockSpec((tm, wd), lambda i: (i, 0)) for (_, _, wd, _) in _IN_PIECES],
        compiler_params=_params("arbitrary"),
        name="in_proj",
    )(x2d, sc, sh, g.reshape(1, d), w_bf16)


def _hgrn_gates(zf, lb):
    sig = _sigmoid(zf)
    logf = jnp.log(lb + (1.0 - lb) * sig)
    k = (1.0 - lb) * (1.0 - sig)
    return logf, k


def _chunk_cumsum(cs, logf):
    hi = logf.astype(BF16)
    lo = (logf - hi.astype(F32)).astype(BF16)
    return _dot(cs, hi) + _dot(cs, lo)


def _hgrn_state_step(zf, v, lb, st, cs, reverse):
    logf, k = _hgrn_gates(zf, lb)
    cum = _chunk_cumsum(cs, logf)
    t = 0 if reverse else HG_CHUNK - 1
    total = cum[t:t + 1, :]
    kdec = (k * jnp.exp(total - cum)).astype(BF16)
    st_new = st * jnp.exp(total) + lax.dot_general(v.astype(BF16), kdec, _TN, preferred_element_type=F32)
    return cum, k, st_new


def _hgrn_kernel(*refs, reverse, final, n_ctx_chunks):
    if final:
        q_ref, f_ref, v_ref, cf_ref, cv_ref, lb_ref, of_ref, g_ref, o_ref, st_ref = refs
    else:
        q_ref, f_ref, v_ref, cf_ref, cv_ref, lb_ref, o_ref, st_ref = refs
    c_len = HG_CHUNK
    n_rows = q_ref.shape[1]
    n_chunks = n_rows // c_len
    row = lax.broadcasted_iota(jnp.int32, (n_rows, n_rows), 0)
    col = lax.broadcasted_iota(jnp.int32, (n_rows, n_rows), 1)
    tri = (col >= row) if reverse else (col <= row)
    same_chunk = None
    for c in range(n_chunks):
        lo, hi = c * c_len, (c + 1) * c_len
        blk = (row >= lo) & (row < hi) & (col >= lo) & (col < hi)
        same_chunk = blk if same_chunk is None else (same_chunk | blk)
    mask = tri & same_chunk
    cs = jnp.where(mask, 1.0, 0.0).astype(BF16)

    @pl.when(pl.program_id(1) == 0)
    def _():
        cs1 = cs[:c_len, :c_len]
        order = range(n_ctx_chunks - 1, -1, -1) if reverse else range(n_ctx_chunks)
        for h in range(HG_HEADS):
            cols = slice(h * HG_DK, (h + 1) * HG_DK)
            st = jnp.zeros((HG_DK, HG_DK), F32)
            for c in order:
                rows = slice(c * c_len, (c + 1) * c_len)
                _, _, st = _hgrn_state_step(cf_ref[0, rows, cols], cv_ref[0, rows, cols].astype(F32),
                                            lb_ref[:, cols], st, cs1, reverse)
            st_ref[h] = st

    def per_chunk_rows(x, r):
        return jnp.concatenate([jnp.broadcast_to(x[c * c_len + r:c * c_len + r + 1, :], (c_len, x.shape[1]))
                                for c in range(n_chunks)], axis=0)

    lb = lb_ref[...]
    q = q_ref[0].astype(F32)
    v = v_ref[0]
    logf, k = _hgrn_gates(f_ref[0], lb)
    cum = _chunk_cumsum(cs, logf)
    r_ref = c_len // 2 - 1 if reverse else c_len // 2
    r_tot = 0 if reverse else c_len - 1
    ref = per_chunk_rows(cum, r_ref)
    qe = q * jnp.exp(cum - ref)
    ke = k * jnp.exp(ref - cum)
    qi, ki = qe.astype(BF16), ke.astype(BF16)
    q_in = (qe * jnp.exp(ref)).astype(BF16)
    tail = jnp.exp(per_chunk_rows(cum, r_tot) - ref)
    kdec = (ke * tail).astype(BF16)

    order = range(n_chunks - 1, -1, -1) if reverse else range(n_chunks)
    for h in range(HG_HEADS):
        cols = slice(h * HG_DK, (h + 1) * HG_DK)
        s = lax.dot_general(qi[:, cols], ki[:, cols], _NT, preferred_element_type=F32)
        o_intra = _dot(jnp.where(mask, s, 0.0).astype(BF16), v[:, cols])
        st = st_ref[h]
        for c in order:
            rows = slice(c * c_len, (c + 1) * c_len)
            o = o_intra[rows] + lax.dot_general(q_in[rows, cols], st.astype(BF16), _NT, preferred_element_type=F32)
            total = cum[c * c_len + r_tot:c * c_len + r_tot + 1, cols]
            st = st * jnp.exp(total) + lax.dot_general(v[rows, cols], kdec[rows, cols], _TN,
                                                       preferred_element_type=F32)
            if final:
                o = o + of_ref[0, rows, cols]
                o = o * lax.rsqrt(jnp.mean(o * o, axis=-1, keepdims=True) + EPS) * g_ref[...]
            o_ref[0, rows, cols] = o.astype(o_ref.dtype)
        st_ref[h] = st


def _hgrn_pass(q, f, v, cf, cv, lb, o_prev, g, *, reverse):
    b, rows, _ = q.shape
    nw = GRID_W
    final = o_prev is not None
    wmap = (lambda bi, w: (bi, 0, nw - 1 - w)) if reverse else (lambda bi, w: (bi, 0, w))
    blk = pl.BlockSpec((1, rows, HG_WIDTH), wmap)
    cblk = pl.BlockSpec((1, cf.shape[1], HG_WIDTH), lambda bi, w: (bi, 0, 0))
    in_specs = [blk, blk, blk, cblk, cblk, pl.BlockSpec((1, HG_WIDTH), lambda bi, w: (0, 0))]
    args = [q, f, v, cf, cv, lb]
    if final:
        in_specs += [blk, pl.BlockSpec((1, HG_DK), lambda bi, w: (0, 0))]
        args += [o_prev, g]
    return pl.pallas_call(
        functools.partial(_hgrn_kernel, reverse=reverse, final=final, n_ctx_chunks=cf.shape[1] // HG_CHUNK),
        out_shape=jax.ShapeDtypeStruct(q.shape, BF16 if final else F32),
        grid=(b, nw),
        in_specs=in_specs,
        out_specs=blk,
        scratch_shapes=[pltpu.VMEM((HG_HEADS, HG_DK, HG_DK), F32)],
        compiler_params=_params("arbitrary", "arbitrary"),
        name="hgrn_bwd" if reverse else "hgrn_fwd",
    )(*args)


def _s5_weights(lam_re, lam_im, log_dt, b_re, b_im, c_re, c_im):
    hp = lax.Precision.HIGHEST
    g, p, cc, t = S5_GROUPS, S5_STATE, S5_GROUP, S5_T
    lre = jnp.minimum(lam_re.astype(F32), -1e-4)
    lim = lam_im.astype(F32)
    dt = jnp.exp(log_dt.astype(F32))[..., None]
    ks = jnp.arange(t + 1, dtype=F32)[:, None, None, None]
    mag = jnp.exp(ks * (lre * dt)[None])
    pw_re = mag * jnp.cos(ks * (lim * dt)[None])
    pw_im = mag * jnp.sin(ks * (lim * dt)[None])
    nr, ni = pw_re[1] - 1.0, pw_im[1]
    den = lre * lre + lim * lim
    cf_re = (nr * lre + ni * lim) / den
    cf_im = (ni * lre - nr * lim) / den
    bb_re = cf_re[..., None] * b_re - cf_im[..., None] * b_im
    bb_im = cf_re[..., None] * b_im + cf_im[..., None] * b_re
    cre, cim = c_re.astype(F32), c_im.astype(F32)
    sw, ns = S5_WIDTH, 2 * g * p
    grp_of_row = jnp.arange(sw)[:, None] // cc

    cp_re = cre[None, None, :, :, :] * pw_re[:t, :, :, None, :] - cim[None, None] * pw_im[:t, :, :, None, :]
    cp_im = cre[None, None, :, :, :] * pw_im[:t, :, :, None, :] + cim[None, None] * pw_re[:t, :, :, None, :]
    kk = (jnp.einsum("kdgop,dgpi->dkgoi", cp_re, bb_re, precision=hp)
          - jnp.einsum("kdgop,dgpi->dkgoi", cp_im, bb_im, precision=hp))
    kf, kb = kk[0], kk[1]
    kall = jnp.concatenate([kb[:0:-1], (kf[0] + kb[0])[None], kf[1:]], axis=0)
    kt = kall.transpose(0, 1, 3, 2).reshape(2 * t - 1, sw, cc)
    same = grp_of_row == (jnp.arange(sw)[None, :] // cc)
    d_lag = jnp.where(same[None], jnp.tile(kt, (1, 1, g)), 0.0).astype(BF16)

    same_in = grp_of_row == ((jnp.arange(ns)[None, :] % (g * p)) // p)

    def in_to_state(pre, pim, bre, bim):
        xre = pre[..., None] * bre[None] - pim[..., None] * bim[None]
        xim = pre[..., None] * bim[None] + pim[..., None] * bre[None]
        x = jnp.concatenate([jnp.tile(xre.transpose(0, 1, 3, 2).reshape(t, sw, p), (1, 1, g)),
                             jnp.tile(xim.transpose(0, 1, 3, 2).reshape(t, sw, p), (1, 1, g))], axis=-1)
        return jnp.where(same_in[None], x, 0.0).reshape(t * sw, ns)

    w_in = jnp.concatenate([in_to_state(pw_re[t - 1::-1, 0], pw_im[t - 1::-1, 0], bb_re[0], bb_im[0]),
                            in_to_state(pw_re[:t, 1], pw_im[:t, 1], bb_re[1], bb_im[1])], axis=1).astype(BF16)

    same_out = ((jnp.arange(ns)[:, None] % (g * p)) // p) == ((jnp.arange(t * sw)[None, :] // cc) % g)

    def state_to_out(pre, pim):
        are = cre[None] * pre[:, :, None, :] - cim[None] * pim[:, :, None, :]
        aim = cre[None] * pim[:, :, None, :] + cim[None] * pre[:, :, None, :]
        a = jnp.concatenate([are.transpose(1, 3, 0, 2), -aim.transpose(1, 3, 0, 2)], axis=0)
        a = jnp.broadcast_to(a.reshape(ns, t, 1, cc), (ns, t, g, cc)).reshape(ns, t * sw)
        return jnp.where(same_out, a, 0.0).astype(BF16)

    w_out_f = state_to_out(pw_re[1:, 0], pw_im[1:, 0])
    w_out_b = state_to_out(pw_re[t:0:-1, 1], pw_im[t:0:-1, 1])

    decay = jnp.stack([pw_re[t].reshape(2, g * p), pw_im[t].reshape(2, g * p)], axis=1)
    return d_lag, w_in, w_out_f, w_out_b, decay


def _s5_in_kernel(u_ref, d_ref, w_ref, o_ref):
    j = pl.program_id(0)

    @pl.when(j < S5_T)
    def _():
        acc = _dot(u_ref[:, 0:S5_WIDTH], d_ref[j + S5_T - 1])
        for s in range(1, S5_T):
            acc = acc + _dot(u_ref[:, s * S5_WIDTH:(s + 1) * S5_WIDTH], d_ref[j - s + S5_T - 1])
        o_ref[...] = acc

    @pl.when(j >= S5_T)
    def _():
        o_ref[...] = _dot(u_ref[...], w_ref[...])


def _s5_in(u, d_lag, w_in, tm):
    m, k = u.shape
    tn = S5_WIDTH
    nj = (k + w_in.shape[1]) // tn
    return pl.pallas_call(
        _s5_in_kernel,
        out_shape=jax.ShapeDtypeStruct((m, nj * tn), F32),
        grid=(nj, m // tm),
        in_specs=[pl.BlockSpec((tm, k), lambda j, i: (i, 0)),
                  pl.BlockSpec(d_lag.shape, lambda j, i: (0, 0, 0)),
                  pl.BlockSpec((k, tn), lambda j, i: (0, jnp.maximum(j - S5_T, 0)))],
        out_specs=pl.BlockSpec((tm, tn), lambda j, i: (i, j)),
        compiler_params=_params("arbitrary", "arbitrary"),
        name="s5_in",
    )(u, d_lag, w_in)


def _s5_scan_kernel(efr_ref, efi_ref, ebr_ref, ebi_ref, a_ref, hfr_ref, hfi_ref, hbr_ref, hbi_ref,
                    *, nb, rows_in, rows_out):
    dirs = ((efr_ref, efi_ref, hfr_ref, hfi_ref, a_ref[0, 0:1, :], a_ref[0, 1:2, :]),
            (ebr_ref, ebi_ref, hbr_ref, hbi_ref, a_ref[1, 0:1, :], a_ref[1, 1:2, :]))
    zero = jnp.zeros_like(dirs[0][4])

    def step(srcs, carry, store):
        new = []
        for di, (er_ref, ei_ref, hr_ref, hi_ref, are, aim) in enumerate(dirs):
            for bi in range(nb):
                hre, him = carry[2 * (di * nb + bi)], carry[2 * (di * nb + bi) + 1]
                if store:
                    hr_ref[pl.ds(bi * rows_out + srcs[di], 1), :] = hre
                    hi_ref[pl.ds(bi * rows_out + srcs[di], 1), :] = him
                ere = er_ref[pl.ds(bi * rows_in + srcs[di], 1), :]
                eim = ei_ref[pl.ds(bi * rows_in + srcs[di], 1), :]
                new += [are * hre - aim * him + ere, are * him + aim * hre + eim]
        return tuple(new)

    n_ctx = rows_in - rows_out
    carry = lax.fori_loop(0, n_ctx, lambda s, c: step((rows_out + s, rows_in - 1 - s), c, False),
                          tuple([zero] * (4 * nb)))
    lax.fori_loop(0, rows_out, lambda s, c: step((s, rows_out - 1 - s), c, True), carry)


def _s5_scan(e, decay, nb, rows_in, rows_out):
    tc = 256
    nsr = S5_GROUPS * S5_STATE
    c0 = (S5_T * S5_WIDTH) // tc
    nt = nsr // tc
    eblk = lambda k: pl.BlockSpec((nb * rows_in, tc), lambda j: (0, c0 + k * nt + j))
    hblk = pl.BlockSpec((nb * rows_out, tc), lambda j: (0, j))
    return pl.pallas_call(
        functools.partial(_s5_scan_kernel, nb=nb, rows_in=rows_in, rows_out=rows_out),
        out_shape=[jax.ShapeDtypeStruct((nb * rows_out, nsr), F32)] * 4,
        grid=(nt,),
        in_specs=[eblk(0), eblk(1), eblk(2), eblk(3), pl.BlockSpec((2, 2, tc), lambda j: (0, 0, j))],
        out_specs=[hblk] * 4,
        compiler_params=_params("arbitrary"),
        name="s5_scan",
    )(e, e, e, e, decay)


def _gelu_tanh(x):
    return 0.5 * x * (1.0 + jnp.tanh(math.sqrt(2.0 / math.pi) * (x + 0.044715 * x * x * x)))


def _s5_out_kernel(hfr_ref, hfi_ref, hbr_ref, hbi_ref, wf_ref, wb_ref, yi_ref, u_ref, d_ref, wg_ref, o_ref):
    nsr = hfr_ref.shape[1]
    y = yi_ref[0] + d_ref[...] * u_ref[...].astype(F32)
    for h_ref, w_ref, r0 in ((hfr_ref, wf_ref, 0), (hfi_ref, wf_ref, nsr), (hbr_ref, wb_ref, 0), (hbi_ref, wb_ref, nsr)):
        y = y + _dot(h_ref[...].astype(BF16), w_ref[r0:r0 + nsr, :])
    y = _gelu_tanh(y)
    gate = _sigmoid(_dot(y.astype(BF16), wg_ref[...]))
    o_ref[...] = (y * gate).astype(o_ref.dtype)


def _s5_out(states, w_out_f, w_out_b, e3, u_rows, d_row, w_glu):
    m, nsr = states[0].shape
    nb = e3.shape[0]
    tm = m // nb
    tn = S5_WIDTH
    st = pl.BlockSpec((tm, nsr), lambda i, j: (i, 0))
    wo = pl.BlockSpec((2 * nsr, tn), lambda i, j: (0, j))
    return pl.pallas_call(
        _s5_out_kernel,
        out_shape=jax.ShapeDtypeStruct((m, S5_T * S5_WIDTH), BF16),
        grid=(nb, S5_T),
        in_specs=[st, st, st, st, wo, wo,
                  pl.BlockSpec((1, tm, tn), lambda i, j: (i, 0, j)),
                  pl.BlockSpec((tm, tn), lambda i, j: (i, j)),
                  pl.BlockSpec((1, tn), lambda i, j: (0, 0)),
                  pl.BlockSpec((tn, tn), lambda i, j: (0, 0))],
        out_specs=pl.BlockSpec((tm, tn), lambda i, j: (i, j)),
        compiler_params=_params("arbitrary", "arbitrary"),
        name="s5_out",
    )(*states, w_out_f, w_out_b, e3, u_rows, d_row, w_glu)


U32 = jnp.uint32
ROW_SUB = 4


def _to_token_rows(ref, val):
    t, d = val.shape

    def rounded(x):
        u = lax.bitcast_convert_type(x, U32)
        return u + (jnp.uint32(0x7FFF) + ((u >> 16) & jnp.uint32(1)))

    w = (rounded(val[:, :d // 2]) >> 16) | (rounded(val[:, d // 2:]) & jnp.uint32(0xFFFF0000))
    for s in range(ROW_SUB):
        ref[pl.ds(s, t, stride=ROW_SUB), :] = w[:, s * LANES:(s + 1) * LANES]


def _from_token_rows(ref, t, row0=0):
    w = jnp.concatenate([ref[pl.ds(row0 * ROW_SUB + s, t, stride=ROW_SUB), :] for s in range(ROW_SUB)], axis=-1)
    lo = lax.bitcast_convert_type(w << 16, F32)
    hi = lax.bitcast_convert_type(w & jnp.uint32(0xFFFF0000), F32)
    return jnp.concatenate([lo, hi], axis=-1)


def _route(h2b, wr_ref, br_ref, cnt_ref, ls8_ref, w8_ref, seg_ref):
    tm = h2b.shape[0]
    per_group = N_EXPERTS // ROUTE_GROUPS
    scores = _sigmoid(lax.dot_general(wr_ref[...], h2b, _NT, preferred_element_type=F32))
    biased = scores + br_ref[...]
    neg = -jnp.inf
    sub = lax.broadcasted_iota(jnp.int32, (per_group, tm), 0)
    grp = []
    for gi in range(ROUTE_GROUPS):
        v = biased[gi * per_group:(gi + 1) * per_group, :]
        m1 = jnp.max(v, axis=0, keepdims=True)
        first = jnp.min(jnp.where(v == m1, sub, per_group), axis=0, keepdims=True)
        m2 = jnp.max(jnp.where(sub == first, neg, v), axis=0, keepdims=True)
        grp.append(m1 + m2)
    grp = jnp.concatenate(grp, axis=0)
    gid = lax.broadcasted_iota(jnp.int32, (ROUTE_GROUPS, tm), 0)
    beaten = jnp.zeros((ROUTE_GROUPS, tm), jnp.int32)
    for gj in range(ROUTE_GROUPS):
        r = grp[gj:gj + 1, :]
        beaten = beaten + jnp.where((r > grp) | ((r == grp) & (gj < gid)), 1, 0)
    group_ok = beaten < TOPK_GROUPS
    expert_ok = jnp.concatenate(
        [jnp.broadcast_to(group_ok[gi:gi + 1, :], (per_group, tm)) for gi in range(ROUTE_GROUPS)], axis=0)
    cur = jnp.where(expert_ok, biased, neg)
    eid = lax.broadcasted_iota(jnp.int32, (N_EXPERTS, tm), 0)
    sel = jnp.zeros((N_EXPERTS, tm), F32)
    picks, wts = [], []
    for _ in range(TOP_K):
        m = jnp.max(cur, axis=0, keepdims=True)
        idx = jnp.min(jnp.where(cur == m, eid, N_EXPERTS), axis=0, keepdims=True)
        hit = eid == idx
        picks.append(idx)
        wts.append(jnp.sum(jnp.where(hit, scores, 0.0), axis=0, keepdims=True))
        sel = jnp.where(hit, 1.0, sel)
        cur = jnp.where(hit, neg, cur)
    wsum = wts[0]
    for w in wts[1:]:
        wsum = wsum + w
    selb = sel.astype(BF16)
    ti = lax.broadcasted_iota(jnp.int32, (tm, tm), 0)
    tj = lax.broadcasted_iota(jnp.int32, (tm, tm), 1)
    rank = _dot(selb, jnp.where(ti < tj, 1.0, 0.0).astype(BF16))
    ei = lax.broadcasted_iota(jnp.int32, (N_EXPERTS, N_EXPERTS), 0)
    ej = lax.broadcasted_iota(jnp.int32, (N_EXPERTS, N_EXPERTS), 1)
    seg_off = jnp.sum(_dot(jnp.where(ej < ei, 1.0, 0.0).astype(BF16), selb), axis=1, keepdims=True)
    seg_cnt = jnp.sum(sel, axis=1, keepdims=True)
    slot = seg_off + rank
    for k in range(TOP_K):
        w8_ref[k:k + 1, :] = wts[k] / wsum * ROUTED_SCALE
        ls8_ref[k:k + 1, :] = jnp.sum(jnp.where(eid == picks[k], slot, 0.0), axis=0, keepdims=True).astype(jnp.int32)
    lane = lax.broadcasted_iota(jnp.int32, (N_EXPERTS, LANES), 1)
    seg_ref[0] = jnp.where(lane == 0, cnt_ref[...], jnp.where(lane == 1, seg_cnt, seg_off))
    cnt_ref[...] = cnt_ref[...] + seg_cnt


def _merge_kernel(x_ref, ya_ref, on_ref, go_ref, ga_ref, gb_ref, g1_ref, sc_ref, sh_ref, n2_ref,
                  pa_ref, pb_ref, wo_ref, wr_ref, br_ref,
                  x1_ref, h2_ref, ls8_ref, w8_ref, seg_ref, cnt_ref):
    @pl.when(pl.program_id(0) == 0)
    def _():
        cnt_ref[...] = jnp.zeros_like(cnt_ref)

    go = go_ref[...].astype(F32)
    y_b = (on_ref[...].astype(F32) * (go * _sigmoid(go))).astype(BF16)
    pa = _dot(ya_ref[...], pa_ref[...])
    pb = _dot(y_b, pb_ref[...])
    merged = _sigmoid(ga_ref[...].astype(F32)) * pa + _sigmoid(gb_ref[...].astype(F32)) * pb
    x1 = x_ref[...] + g1_ref[0] * _dot(merged.astype(BF16), wo_ref[...])
    x1_ref[...] = x1
    y = x1 * lax.rsqrt(jnp.mean(x1 * x1, axis=-1, keepdims=True) + EPS) * n2_ref[...]
    h2 = y * (1.0 + sc_ref[0]) + sh_ref[0]
    _to_token_rows(h2_ref, h2)
    _route(h2.astype(BF16), wr_ref, br_ref, cnt_ref, ls8_ref, w8_ref, seg_ref)


def _merge(x2d, ya, on, go, ga, gb, g1, sc2, sh2, n2g, pa, pb, wo, wr_t, br, rows_per_batch, tm):
    n, d = x2d.shape
    per = rows_per_batch // tm
    row = lambda wd: pl.BlockSpec((tm, wd), lambda i: (i, 0))
    mod = pl.BlockSpec((1, 1, d), lambda i: (i // per, 0, 0))
    full = lambda a: pl.BlockSpec(a.shape, lambda i: (0, 0))
    tok = pl.BlockSpec((TOP_K, tm), lambda i: (0, i))
    return pl.pallas_call(
        _merge_kernel,
        out_shape=[jax.ShapeDtypeStruct((n, d), F32), jax.ShapeDtypeStruct((n * ROW_SUB, LANES), U32),
                   jax.ShapeDtypeStruct((TOP_K, n), jnp.int32), jax.ShapeDtypeStruct((TOP_K, n), F32),
                   jax.ShapeDtypeStruct((n // tm, N_EXPERTS, LANES), F32),
                   jax.ShapeDtypeStruct((N_EXPERTS, 1), F32)],
        grid=(n // tm,),
        in_specs=[row(d), row(S5_WIDTH), row(HG_WIDTH), row(HG_WIDTH), row(d), row(d), mod, mod, mod,
                  pl.BlockSpec((1, d), lambda i: (0, 0)), full(pa), full(pb), full(wo), full(wr_t), full(br)],
        out_specs=[row(d), pl.BlockSpec((tm * ROW_SUB, LANES), lambda i: (i, 0)), tok, tok,
                   pl.BlockSpec((1, N_EXPERTS, LANES), lambda i: (i, 0, 0)),
                   pl.BlockSpec((N_EXPERTS, 1), lambda i: (0, 0))],
        compiler_params=_params("arbitrary"),
        name="merge_out_proj_route",
    )(x2d, ya, on, go, ga, gb, g1, sc2, sh2, n2g.reshape(1, d), pa, pb, wo, wr_t, br)


MOE_TILE = 512
MOE_BLK = 512


def _token_row(ref, r):
    return ref.at[pl.ds(pl.multiple_of(r * ROW_SUB, ROW_SUB), ROW_SUB)]


def _wait_rows(any_ref, sem, n_rows):
    view = any_ref.at[pl.ds(0, n_rows * ROW_SUB)]
    pltpu.make_async_copy(view, view, sem).wait()


def _rows(ref, r0, n):
    return ref.at[pl.ds(pl.multiple_of(r0 * ROW_SUB, ROW_SUB), n * ROW_SUB)]


def _pow2_pieces(n, max_piece, fn):
    done = 0
    piece = max_piece
    while piece >= 1:
        hit = (n & piece) != 0
        pl.when(hit)(functools.partial(fn, done, piece))
        done = done + (n & piece)
        piece //= 2


def _copy_rows(src_ref, src0, dst_ref, dst0, n, max_piece, sem):
    def piece(off, size):
        pltpu.make_async_copy(_rows(src_ref, src0 + off, size), _rows(dst_ref, dst0 + off, size), sem).start()
    _pow2_pieces(n, max_piece, piece)


def _wait_copied_rows(src_ref, dst_ref, n, max_piece, sem):
    def piece(off, size):
        pltpu.make_async_copy(_rows(src_ref, 0, size), _rows(dst_ref, 0, size), sem).wait()
    _pow2_pieces(n, max_piece, piece)


def _dispatch_kernel(gs_ref, cnt_ref, off_ref, pad_ref, ls_ref, h2_ref, xs_hbm, ls_smem, stage, zbuf, sem, zsem,
                     *, tm):
    i = pl.program_id(0)
    cp = pltpu.make_async_copy(ls_ref, ls_smem, sem)
    cp.start()
    cp.wait()

    def body(t, carry):
        row = h2_ref[pl.ds(pl.multiple_of(t * ROW_SUB, ROW_SUB), ROW_SUB), :]
        for k in range(TOP_K):
            stage[pl.ds(pl.multiple_of(ls_smem[k, t] * ROW_SUB, ROW_SUB), ROW_SUB), :] = row
        return carry

    lax.fori_loop(0, tm, body, 0)

    def per_expert(e, carry):
        _copy_rows(stage, off_ref[i, e], xs_hbm, gs_ref[i, e], cnt_ref[i, e], tm, sem)
        return carry

    lax.fori_loop(0, N_EXPERTS, per_expert, 0)

    @pl.when(i == 0)
    def _():
        zbuf[...] = jnp.zeros_like(zbuf)

        def start(e, carry):
            _copy_rows(zbuf, 0, xs_hbm, pad_ref[0, e], pad_ref[1, e], MOE_BLK // 2, zsem)
            return carry

        def wait(e, carry):
            _wait_copied_rows(zbuf, xs_hbm, pad_ref[1, e], MOE_BLK // 2, zsem)
            return carry

        lax.fori_loop(0, N_EXPERTS, start, 0)
        lax.fori_loop(0, N_EXPERTS, wait, 0)

    _wait_rows(xs_hbm, sem, tm * TOP_K)


def _dispatch(gstart, seg_cnt, seg_off, pad, ls8, h2_rows, n_blocks, tm):
    n = ls8.shape[1]
    cap = n_blocks * MOE_BLK
    return pl.pallas_call(
        functools.partial(_dispatch_kernel, tm=tm),
        out_shape=jax.ShapeDtypeStruct((cap * ROW_SUB, LANES), U32),
        grid_spec=pltpu.PrefetchScalarGridSpec(
            num_scalar_prefetch=4,
            grid=(n // tm,),
            in_specs=[pl.BlockSpec((TOP_K, tm), lambda i, *_: (0, i)),
                      pl.BlockSpec((tm * ROW_SUB, LANES), lambda i, *_: (i, 0))],
            out_specs=pl.BlockSpec(memory_space=pl.ANY),
            scratch_shapes=[pltpu.SMEM((TOP_K, tm), jnp.int32),
                            pltpu.VMEM((TOP_K * tm * ROW_SUB, LANES), U32),
                            pltpu.VMEM((MOE_BLK * ROW_SUB, LANES), U32),
                            pltpu.SemaphoreType.DMA, pltpu.SemaphoreType.DMA]),
        compiler_params=pltpu.CompilerParams(dimension_semantics=("arbitrary",), vmem_limit_bytes=VMEM_LIMIT,
                                             has_side_effects=True),
        name="moe_dispatch",
    )(gstart, seg_cnt, seg_off, pad, ls8, h2_rows)


def _expert_kernel(be_ref, nu_ref, x_ref, w1_ref, w3_ref, w2_ref, o_ref, w1b, w3b, w2b):
    j = pl.program_id(0)
    e = be_ref[j]
    prev = be_ref[jnp.maximum(j - 1, 0)]
    used = j < nu_ref[0]

    @pl.when(jnp.logical_and(used, jnp.logical_or(j == 0, e != prev)))
    def _():
        w1b[...] = w1_ref[0].astype(BF16)
        w3b[...] = w3_ref[0].astype(BF16)
        w2b[...] = w2_ref[0].astype(BF16)

    @pl.when(used)
    def _():
        x = _from_token_rows(x_ref, MOE_BLK).astype(BF16)
        a = _dot(x, w1b[...])
        hid = (a * _sigmoid(a)) * _dot(x, w3b[...])
        _to_token_rows(o_ref, _dot(hid.astype(BF16), w2b[...]))

    @pl.when(jnp.logical_not(used))
    def _():
        o_ref[...] = jnp.zeros_like(o_ref)


def _experts(block_e, n_used, xs, w1, w3, w2):
    n_blocks = xs.shape[0] // (MOE_BLK * ROW_SUB)
    d, f = w1.shape[1], w1.shape[2]
    rows = pl.BlockSpec((MOE_BLK * ROW_SUB, LANES), lambda j, be, nu: (j, 0))
    return pl.pallas_call(
        _expert_kernel,
        out_shape=jax.ShapeDtypeStruct(xs.shape, U32),
        grid_spec=pltpu.PrefetchScalarGridSpec(
            num_scalar_prefetch=2,
            grid=(n_blocks,),
            in_specs=[rows,
                      pl.BlockSpec((1, d, f), lambda j, be, nu: (be[j], 0, 0)),
                      pl.BlockSpec((1, d, f), lambda j, be, nu: (be[j], 0, 0)),
                      pl.BlockSpec((1, f, d), lambda j, be, nu: (be[j], 0, 0))],
            out_specs=rows,
            scratch_shapes=[pltpu.VMEM((d, f), BF16), pltpu.VMEM((d, f), BF16), pltpu.VMEM((f, d), BF16)]),
        compiler_params=_params("arbitrary"),
        name="moe_experts",
    )(block_e, n_used, xs, w1, w3, w2)


def _combine_kernel(gs_ref, cnt_ref, off_ref, ls_ref, w8_ref, x1_ref, h2_ref, g2_ref, ws1_ref, ws3_ref, ws2_ref,
                    fg_ref, ys_hbm, o_ref, ls_smem, w_smem, gbuf, acc_rows, sem, *, tm):
    i = pl.program_id(0)
    cp1 = pltpu.make_async_copy(ls_ref, ls_smem, sem)
    cp2 = pltpu.make_async_copy(w8_ref, w_smem, sem)
    cp1.start()
    cp2.start()
    cp1.wait()
    cp2.wait()

    def per_expert(e, carry):
        _copy_rows(ys_hbm, gs_ref[i, e], gbuf, off_ref[i, e], cnt_ref[i, e], tm, sem)
        return carry

    lax.fori_loop(0, N_EXPERTS, per_expert, 0)

    h2 = _from_token_rows(h2_ref, tm).astype(BF16)
    a = _dot(h2, ws1_ref[...])
    hid = (a * _sigmoid(a)) * _dot(h2, ws3_ref[...])
    acc = _dot(hid.astype(BF16), ws2_ref[...])

    _wait_rows(gbuf, sem, tm * TOP_K)

    def body(t, carry):
        lo = jnp.zeros((ROW_SUB, LANES), F32)
        hi = jnp.zeros((ROW_SUB, LANES), F32)
        for k in range(TOP_K):
            w = w_smem[k, t]
            words = gbuf[pl.ds(pl.multiple_of(ls_smem[k, t] * ROW_SUB, ROW_SUB), ROW_SUB), :]
            lo = lo + w * lax.bitcast_convert_type(words << 16, F32)
            hi = hi + w * lax.bitcast_convert_type(words & jnp.uint32(0xFFFF0000), F32)
        acc_rows[pl.ds(pl.multiple_of(t * SUBLANES, SUBLANES), ROW_SUB), :] = lo
        acc_rows[pl.ds(pl.multiple_of(t * SUBLANES, SUBLANES) + ROW_SUB, ROW_SUB), :] = hi
        return carry

    lax.fori_loop(0, tm, body, 0)
    routed = jnp.concatenate([acc_rows[pl.ds(s, tm, stride=SUBLANES), :] for s in range(SUBLANES)], axis=-1)
    y = x1_ref[...] + g2_ref[0] * (acc + routed)
    o_ref[...] = y * lax.rsqrt(jnp.mean(y * y, axis=-1, keepdims=True) + EPS) * fg_ref[...]


def _combine(gstart, seg_cnt, seg_off, ls8, w8, x1, h2_rows, g2, ws1, ws3, ws2, fg, ys, rows_per_batch, tm):
    n, d = x1.shape
    per = rows_per_batch // tm
    tok = pl.BlockSpec((TOP_K, tm), lambda i, *_: (0, i))
    full = lambda a: pl.BlockSpec(a.shape, lambda i, *_: (0, 0))
    return pl.pallas_call(
        functools.partial(_combine_kernel, tm=tm),
        out_shape=jax.ShapeDtypeStruct((n, d), F32),
        grid_spec=pltpu.PrefetchScalarGridSpec(
            num_scalar_prefetch=3,
            grid=(n // tm,),
            in_specs=[tok, tok, pl.BlockSpec((tm, d), lambda i, *_: (i, 0)),
                      pl.BlockSpec((tm * ROW_SUB, LANES), lambda i, *_: (i, 0)),
                      pl.BlockSpec((1, 1, d), lambda i, *_: (i // per, 0, 0)),
                      full(ws1), full(ws3), full(ws2), pl.BlockSpec((1, d), lambda i, *_: (0, 0)),
                      pl.BlockSpec(memory_space=pl.ANY)],
            out_specs=pl.BlockSpec((tm, d), lambda i, *_: (i, 0)),
            scratch_shapes=[pltpu.SMEM((TOP_K, tm), jnp.int32), pltpu.SMEM((TOP_K, tm), F32),
                            pltpu.VMEM((TOP_K * tm * ROW_SUB, LANES), U32),
                            pltpu.VMEM((tm * SUBLANES, LANES), F32), pltpu.SemaphoreType.DMA]),
        compiler_params=_params("arbitrary"),
        name="moe_combine_final",
    )(gstart, seg_cnt, seg_off, ls8, w8, x1, h2_rows, g2, ws1, ws3, ws2, fg.reshape(1, d), ys)


def _moe_plan(seg, counts, n_assign):
    cnt = counts.reshape(N_EXPERTS).astype(jnp.int32)
    padded = (cnt + MOE_BLK - 1) // MOE_BLK * MOE_BLK
    pends = jnp.cumsum(padded)
    pstarts = pends - padded
    n_blocks = (n_assign + N_EXPERTS * (MOE_BLK - 1) + MOE_BLK - 1) // MOE_BLK
    seg = seg[:, :, :3].astype(jnp.int32)
    gstart = pstarts[None, :] + seg[:, :, 0]
    blk_start = jnp.arange(n_blocks, dtype=jnp.int32) * MOE_BLK
    block_e = jnp.minimum(jnp.sum((blk_start[:, None] >= pends[None, :]).astype(jnp.int32), axis=1),
                          N_EXPERTS - 1).astype(jnp.int32)
    pad = jnp.stack([pstarts + cnt, padded - cnt], axis=0).astype(jnp.int32)
    n_used = (pends[-1:] // MOE_BLK).astype(jnp.int32)
    return gstart, seg[:, :, 1], seg[:, :, 2], pad, block_e, n_used, n_blocks


def _mixer(x, c, ctx, c_ctx, w_ada, b_ada, norm1_g, norm2_g, w_in, s5_lam_re, s5_lam_im, s5_log_dt,
           s5_b_re, s5_b_im, s5_c_re, s5_c_im, s5_d, s5_w_glu, lb, hg_norm_g, p_a, p_b, w_out,
           moe_w_router, moe_b_router):
    b, l, d = x.shape
    lc = ctx.shape[1]
    n = b * l
    rows = l // GRID_W

    c8 = jnp.concatenate([c, c_ctx[None], jnp.zeros((8 - b - 1, d), F32)], axis=0)
    mod = _ada(c8, w_ada, b_ada)
    sh1, sc1, g1, sh2, sc2, g2 = [mod[:b, k * d:(k + 1) * d].reshape(b, 1, d) for k in range(6)]
    csh1, csc1 = mod[b:b + 1, 0:d].reshape(1, 1, d), mod[b:b + 1, d:2 * d].reshape(1, 1, d)

    w_in_b = w_in.astype(BF16)
    z = dict(zip([p[0] for p in _IN_PIECES], _inproj(x.reshape(n, d), sc1, sh1, norm1_g, w_in_b, l, 512)))
    zc = dict(zip([p[0] for p in _IN_PIECES], _inproj(ctx.reshape(b * lc, d), csc1, csh1, norm1_g, w_in_b, lc, lc)))

    cm = lambda t: t.reshape(b, rows, GRID_W * HG_WIDTH)
    cx = lambda t: t.reshape(b, lc, HG_WIDTH)
    lb_row = lb.reshape(1, HG_WIDTH)
    o_f = _hgrn_pass(cm(z["q"]), cm(z["ff"]), cm(z["i"]), cx(zc["ff"]), cx(zc["i"]), lb_row, None, None,
                     reverse=False)
    o_n = _hgrn_pass(cm(z["q"]), cm(z["fb"]), cm(z["i"]), cx(zc["fb"]), cx(zc["i"]), lb_row, o_f,
                     hg_norm_g.reshape(1, HG_DK), reverse=True)
    o_n = o_n.reshape(n, HG_WIDTH)

    d_lag, w_s5_in, w_out_f, w_out_b, decay = _s5_weights(s5_lam_re, s5_lam_im, s5_log_dt, s5_b_re, s5_b_im,
                                                          s5_c_re, s5_c_im)
    kc, kl = lc // S5_T, l // S5_T
    u_lat = z["u"].reshape(b, kl, S5_T * S5_WIDTH)
    u_ctx = zc["u"].reshape(b, kc, S5_T * S5_WIDTH)
    rows_in = kl + kc
    u_ext = jnp.concatenate([u_lat, u_ctx], axis=1).reshape(b * rows_in, S5_T * S5_WIDTH)
    e = _s5_in(u_ext, d_lag, w_s5_in, (b * rows_in) // 2)
    states = _s5_scan(e, decay, b, rows_in, kl)
    d_row = s5_d.astype(F32).reshape(1, S5_WIDTH)
    y_a = _s5_out(states, w_out_f, w_out_b, e.reshape(b, rows_in, -1), u_lat.reshape(b * kl, -1), d_row,
                  s5_w_glu.astype(BF16))
    y_a = y_a.reshape(n, S5_WIDTH)

    return _merge(x.reshape(n, d), y_a, o_n, z["go"], z["ga"], z["gb"], g1, sc2, sh2, norm2_g,
                  p_a.astype(BF16), p_b.astype(BF16), w_out.astype(BF16),
                  moe_w_router.T.astype(BF16), moe_b_router.astype(F32).reshape(N_EXPERTS, 1), l, MOE_TILE) + (g2,)


def kernel(x, c, ctx, c_ctx, w_ada, b_ada, norm1_g, norm2_g, w_in, s5_lam_re, s5_lam_im, s5_log_dt, s5_b_re,
           s5_b_im, s5_c_re, s5_c_im, s5_d, s5_w_glu, hg_lb_logits, hg_norm_g, p_a, p_b, w_out, moe_w_router,
           moe_b_router, moe_w1, moe_w3, moe_w2, moe_ws1, moe_ws3, moe_ws2, final_norm_g):
    b, l, d = x.shape
    n = b * l
    assert w_ada.shape[0] == 1, "single-layer block"
    lb = jnp.cumsum(jax.nn.softmax(hg_lb_logits.astype(F32), axis=0), axis=0)[0]
    x1, h2_rows, ls8, w8, seg, counts, g2 = _mixer(
        x, c, ctx, c_ctx, w_ada[0], b_ada[0], norm1_g[0], norm2_g[0], w_in[0], s5_lam_re[0], s5_lam_im[0],
        s5_log_dt[0], s5_b_re[0], s5_b_im[0], s5_c_re[0], s5_c_im[0], s5_d[0], s5_w_glu[0], lb, hg_norm_g[0],
        p_a[0], p_b[0], w_out[0], moe_w_router[0], moe_b_router[0])
    gstart, seg_cnt, seg_off, pad, block_e, n_used, n_blocks = _moe_plan(seg, counts, n * TOP_K)
    xs = _dispatch(gstart, seg_cnt, seg_off, pad, ls8, h2_rows, n_blocks, MOE_TILE)
    ys = _experts(block_e, n_used, xs, moe_w1[0], moe_w3[0], moe_w2[0])
    out = _combine(gstart, seg_cnt, seg_off, ls8, w8, x1, h2_rows, g2, moe_ws1[0].astype(BF16),
                   moe_ws3[0].astype(BF16), moe_ws2[0].astype(BF16), final_norm_g, ys, l, MOE_TILE)
    return out.reshape(b, l, d)
```

```python
import functools
import math

import jax
import jax.numpy as jnp
from jax import lax
from jax.experimental import pallas as pl
from jax.experimental.pallas import tpu as pltpu

F32 = jnp.float32
BF16 = jnp.bfloat16

GRID_W = 64
S5_WIDTH = 256
S5_GROUP = 16
S5_GROUPS = 16
S5_STATE = 64
HG_HEADS = 6
HG_DK = 128
HG_WIDTH = HG_HEADS * HG_DK
N_EXPERTS = 64
ROUTE_GROUPS = 8
TOPK_GROUPS = 4
TOP_K = 8
ROUTED_SCALE = 2.5
EPS = 1e-6

LANES = 128
SUBLANES = 8

S5_T = 16
HG_CHUNK = 64
VMEM_LIMIT = 56 * 1024 * 1024

_NT = (((1,), (1,)), ((), ()))
_TN = (((0,), (0,)), ((), ()))


def _params(*sem):
    return pltpu.CompilerParams(dimension_semantics=sem, vmem_limit_bytes=VMEM_LIMIT)


def _dot(a, b):
    return jnp.dot(a, b, preferred_element_type=F32)


def _sigmoid(x):
    return 1.0 / (1.0 + jnp.exp(-x))


def _ada_kernel(c_ref, w_ref, b_ref, o_ref):
    c = c_ref[...]
    s = (c * _sigmoid(c)).astype(BF16)
    o_ref[...] = _dot(s, w_ref[...].astype(BF16)) + b_ref[...]


def _ada(c8, w_ada, b_ada):
    d, n = w_ada.shape
    tn = 1536
    return pl.pallas_call(
        _ada_kernel,
        out_shape=jax.ShapeDtypeStruct((8, n), F32),
        grid=(n // tn,),
        in_specs=[pl.BlockSpec((8, d), lambda j: (0, 0)),
                  pl.BlockSpec((d, tn), lambda j: (0, j)),
                  pl.BlockSpec((1, tn), lambda j: (0, j))],
        out_specs=pl.BlockSpec((8, tn), lambda j: (0, j)),
        compiler_params=_params("arbitrary"),
        name="ada_mod",
    )(c8, w_ada, b_ada.reshape(1, n))


_IN_PIECES = (("u", 0, 256, BF16), ("q", 256, 768, BF16), ("ff", 1024, 768, F32),
              ("fb", 1792, 768, F32), ("i", 2560, 768, BF16), ("go", 3328, 768, BF16),
              ("ga", 4096, 1024, BF16), ("gb", 5120, 1024, BF16))


def _inproj_kernel(x_ref, sc_ref, sh_ref, g_ref, w_ref, *o_refs):
    x = x_ref[...]
    y = x * lax.rsqrt(jnp.mean(x * x, axis=-1, keepdims=True) + EPS) * g_ref[...]
    h = (y * (1.0 + sc_ref[0]) + sh_ref[0]).astype(BF16)
    for (_, a, wd, _), o_ref in zip(_IN_PIECES, o_refs):
        o_ref[...] = _dot(h, w_ref[:, a:a + wd]).astype(o_ref.dtype)


def _inproj(x2d, sc, sh, g, w_bf16, rows_per_mod, tm):
    n, d = x2d.shape
    per = rows_per_mod // tm
    mod_map = (lambda i: (i // per, 0, 0)) if sc.shape[0] > 1 else (lambda i: (0, 0, 0))
    return pl.pallas_call(
        _inproj_kernel,
        out_shape=[jax.ShapeDtypeStruct((n, wd), dt) for (_, _, wd, dt) in _IN_PIECES],
        grid=(n // tm,),
        in_specs=[pl.BlockSpec((tm, d), lambda i: (i, 0)),
                  pl.BlockSpec((1, 1, d), mod_map),
                  pl.BlockSpec((1, 1, d), mod_map),
                  pl.BlockSpec((1, d), lambda i: (0, 0)),
                  pl.BlockSpec(w_bf16.shape, lambda i: (0, 0))],
        out_specs=[pl.BlockSpec((tm, wd), lambda i: (i, 0)) for (_, _, wd, _) in _IN_PIECES],
        compiler_params=_params("arbitrary"),
        name="in_proj",
    )(x2d, sc, sh, g.reshape(1, d), w_bf16)


def _hgrn_gates(zf, lb):
    sig = _sigmoid(zf)
    logf = jnp.log(lb + (1.0 - lb) * sig)
    k = (1.0 - lb) * (1.0 - sig)
    return logf, k


def _chunk_cumsum(cs, logf):
    hi = logf.astype(BF16)
    lo = (logf - hi.astype(F32)).astype(BF16)
    return _dot(cs, hi) + _dot(cs, lo)


def _hgrn_state_step(zf, v, lb, st, cs, reverse):
    logf, k = _hgrn_gates(zf, lb)
    cum = _chunk_cumsum(cs, logf)
    t = 0 if reverse else HG_CHUNK - 1
    total = cum[t:t + 1, :]
    kdec = (k * jnp.exp(total - cum)).astype(BF16)
    st_new = st * jnp.exp(total) + lax.dot_general(v.astype(BF16), kdec, _TN, preferred_element_type=F32)
    return cum, k, st_new


def _hgrn_kernel(*refs, reverse, final, n_ctx_chunks):
    if final:
        q_ref, f_ref, v_ref, cf_ref, cv_ref, lb_ref, of_ref, g_ref, o_ref, st_ref = refs
    else:
        q_ref, f_ref, v_ref, cf_ref, cv_ref, lb_ref, o_ref, st_ref = refs
    c_len = HG_CHUNK
    n_rows = q_ref.shape[1]
    n_chunks = n_rows // c_len
    row = lax.broadcasted_iota(jnp.int32, (n_rows, n_rows), 0)
    col = lax.broadcasted_iota(jnp.int32, (n_rows, n_rows), 1)
    tri = (col >= row) if reverse else (col <= row)
    same_chunk = None
    for c in range(n_chunks):
        lo, hi = c * c_len, (c + 1) * c_len
        blk = (row >= lo) & (row < hi) & (col >= lo) & (col < hi)
        same_chunk = blk if same_chunk is None else (same_chunk | blk)
    mask = tri & same_chunk
    cs = jnp.where(mask, 1.0, 0.0).astype(BF16)

    @pl.when(pl.program_id(1) == 0)
    def _():
        cs1 = cs[:c_len, :c_len]
        order = range(n_ctx_chunks - 1, -1, -1) if reverse else range(n_ctx_chunks)
        for h in range(HG_HEADS):
            cols = slice(h * HG_DK, (h + 1) * HG_DK)
            st = jnp.zeros((HG_DK, HG_DK), F32)
            for c in order:
                rows = slice(c * c_len, (c + 1) * c_len)
                _, _, st = _hgrn_state_step(cf_ref[0, rows, cols], cv_ref[0, rows, cols].astype(F32),
                                            lb_ref[:, cols], st, cs1, reverse)
            st_ref[h] = st

    def per_chunk_rows(x, r):
        return jnp.concatenate([jnp.broadcast_to(x[c * c_len + r:c * c_len + r + 1, :], (c_len, x.shape[1]))
                                for c in range(n_chunks)], axis=0)

    lb = lb_ref[...]
    q = q_ref[0].astype(F32)
    v = v_ref[0]
    logf, k = _hgrn_gates(f_ref[0], lb)
    cum = _chunk_cumsum(cs, logf)
    r_ref = c_len // 2 - 1 if reverse else c_len // 2
    r_tot = 0 if reverse else c_len - 1
    ref = per_chunk_rows(cum, r_ref)
    qe = q * jnp.exp(cum - ref)
    ke = k * jnp.exp(ref - cum)
    qi, ki = qe.astype(BF16), ke.astype(BF16)
    q_in = (qe * jnp.exp(ref)).astype(BF16)
    tail = jnp.exp(per_chunk_rows(cum, r_tot) - ref)
    kdec = (ke * tail).astype(BF16)

    order = range(n_chunks - 1, -1, -1) if reverse else range(n_chunks)
    for h in range(HG_HEADS):
        cols = slice(h * HG_DK, (h + 1) * HG_DK)
        s = lax.dot_general(qi[:, cols], ki[:, cols], _NT, preferred_element_type=F32)
        o_intra = _dot(jnp.where(mask, s, 0.0).astype(BF16), v[:, cols])
        st = st_ref[h]
        for c in order:
            rows = slice(c * c_len, (c + 1) * c_len)
            o = o_intra[rows] + lax.dot_general(q_in[rows, cols], st.astype(BF16), _NT, preferred_element_type=F32)
            total = cum[c * c_len + r_tot:c * c_len + r_tot + 1, cols]
            st = st * jnp.exp(total) + lax.dot_general(v[rows, cols], kdec[rows, cols], _TN,
                                                       preferred_element_type=F32)
            if final:
                o = o + of_ref[0, rows, cols]
                o = o * lax.rsqrt(jnp.mean(o * o, axis=-1, keepdims=True) + EPS) * g_ref[...]
            o_ref[0, rows, cols] = o.astype(o_ref.dtype)
        st_ref[h] = st


def _hgrn_pass(q, f, v, cf, cv, lb, o_prev, g, *, reverse):
    b, rows, _ = q.shape
    nw = GRID_W
    final = o_prev is not None
    wmap = (lambda bi, w: (bi, 0, nw - 1 - w)) if reverse else (lambda bi, w: (bi, 0, w))
    blk = pl.BlockSpec((1, rows, HG_WIDTH), wmap)
    cblk = pl.BlockSpec((1, cf.shape[1], HG_WIDTH), lambda bi, w: (bi, 0, 0))
    in_specs = [blk, blk, blk, cblk, cblk, pl.BlockSpec((1, HG_WIDTH), lambda bi, w: (0, 0))]
    args = [q, f, v, cf, cv, lb]
    if final:
        in_specs += [blk, pl.BlockSpec((1, HG_DK), lambda bi, w: (0, 0))]
        args += [o_prev, g]
    return pl.pallas_call(
        functools.partial(_hgrn_kernel, reverse=reverse, final=final, n_ctx_chunks=cf.shape[1] // HG_CHUNK),
        out_shape=jax.ShapeDtypeStruct(q.shape, BF16 if final else F32),
        grid=(b, nw),
        in_specs=in_specs,
        out_specs=blk,
        scratch_shapes=[pltpu.VMEM((HG_HEADS, HG_DK, HG_DK), F32)],
        compiler_params=_params("arbitrary", "arbitrary"),
        name="hgrn_bwd" if reverse else "hgrn_fwd",
    )(*args)


def _s5_weights(lam_re, lam_im, log_dt, b_re, b_im, c_re, c_im):
    hp = lax.Precision.HIGHEST
    g, p, cc, t = S5_GROUPS, S5_STATE, S5_GROUP, S5_T
    lre = jnp.minimum(lam_re.astype(F32), -1e-4)
    lim = lam_im.astype(F32)
    dt = jnp.exp(log_dt.astype(F32))[..., None]
    ks = jnp.arange(t + 1, dtype=F32)[:, None, None, None]
    mag = jnp.exp(ks * (lre * dt)[None])
    pw_re = mag * jnp.cos(ks * (lim * dt)[None])
    pw_im = mag * jnp.sin(ks * (lim * dt)[None])
    nr, ni = pw_re[1] - 1.0, pw_im[1]
    den = lre * lre + lim * lim
    cf_re = (nr * lre + ni * lim) / den
    cf_im = (ni * lre - nr * lim) / den
    bb_re = cf_re[..., None] * b_re - cf_im[..., None] * b_im
    bb_im = cf_re[..., None] * b_im + cf_im[..., None] * b_re
    cre, cim = c_re.astype(F32), c_im.astype(F32)
    sw, ns = S5_WIDTH, 2 * g * p
    grp_of_row = jnp.arange(sw)[:, None] // cc

    cp_re = cre[None, None, :, :, :] * pw_re[:t, :, :, None, :] - cim[None, None] * pw_im[:t, :, :, None, :]
    cp_im = cre[None, None, :, :, :] * pw_im[:t, :, :, None, :] + cim[None, None] * pw_re[:t, :, :, None, :]
    kk = (jnp.einsum("kdgop,dgpi->dkgoi", cp_re, bb_re, precision=hp)
          - jnp.einsum("kdgop,dgpi->dkgoi", cp_im, bb_im, precision=hp))
    kf, kb = kk[0], kk[1]
    kall = jnp.concatenate([kb[:0:-1], (kf[0] + kb[0])[None], kf[1:]], axis=0)
    kt = kall.transpose(0, 1, 3, 2).reshape(2 * t - 1, sw, cc)
    same = grp_of_row == (jnp.arange(sw)[None, :] // cc)
    d_lag = jnp.where(same[None], jnp.tile(kt, (1, 1, g)), 0.0).astype(BF16)

    same_in = grp_of_row == ((jnp.arange(ns)[None, :] % (g * p)) // p)

    def in_to_state(pre, pim, bre, bim):
        xre = pre[..., None] * bre[None] - pim[..., None] * bim[None]
        xim = pre[..., None] * bim[None] + pim[..., None] * bre[None]
        x = jnp.concatenate([jnp.tile(xre.transpose(0, 1, 3, 2).reshape(t, sw, p), (1, 1, g)),
                             jnp.tile(xim.transpose(0, 1, 3, 2).reshape(t, sw, p), (1, 1, g))], axis=-1)
        return jnp.where(same_in[None], x, 0.0).reshape(t * sw, ns)

    w_in = jnp.concatenate([in_to_state(pw_re[t - 1::-1, 0], pw_im[t - 1::-1, 0], bb_re[0], bb_im[0]),
                            in_to_state(pw_re[:t, 1], pw_im[:t, 1], bb_re[1], bb_im[1])], axis=1).astype(BF16)

    same_out = ((jnp.arange(ns)[:, None] % (g * p)) // p) == ((jnp.arange(t * sw)[None, :] // cc) % g)

    def state_to_out(pre, pim):
        are = cre[None] * pre[:, :, None, :] - cim[None] * pim[:, :, None, :]
        aim = cre[None] * pim[:, :, None, :] + cim[None] * pre[:, :, None, :]
        a = jnp.concatenate([are.transpose(1, 3, 0, 2), -aim.transpose(1, 3, 0, 2)], axis=0)
        a = jnp.broadcast_to(a.reshape(ns, t, 1, cc), (ns, t, g, cc)).reshape(ns, t * sw)
        return jnp.where(same_out, a, 0.0).astype(BF16)

    w_out_f = state_to_out(pw_re[1:, 0], pw_im[1:, 0])
    w_out_b = state_to_out(pw_re[t:0:-1, 1], pw_im[t:0:-1, 1])

    decay = jnp.stack([pw_re[t].reshape(2, g * p), pw_im[t].reshape(2, g * p)], axis=1)
    return d_lag, w_in, w_out_f, w_out_b, decay


def _s5_in_kernel(u_ref, d_ref, w_ref, o_ref):
    j = pl.program_id(0)

    @pl.when(j < S5_T)
    def _():
        acc = _dot(u_ref[:, 0:S5_WIDTH], d_ref[j + S5_T - 1])
        for s in range(1, S5_T):
            acc = acc + _dot(u_ref[:, s * S5_WIDTH:(s + 1) * S5_WIDTH], d_ref[j - s + S5_T - 1])
        o_ref[...] = acc

    @pl.when(j >= S5_T)
    def _():
        o_ref[...] = _dot(u_ref[...], w_ref[...])


def _s5_in(u, d_lag, w_in, tm):
    m, k = u.shape
    tn = S5_WIDTH
    nj = (k + w_in.shape[1]) // tn
    return pl.pallas_call(
        _s5_in_kernel,
        out_shape=jax.ShapeDtypeStruct((m, nj * tn), F32),
        grid=(nj, m // tm),
        in_specs=[pl.BlockSpec((tm, k), lambda j, i: (i, 0)),
                  pl.BlockSpec(d_lag.shape, lambda j, i: (0, 0, 0)),
                  pl.BlockSpec((k, tn), lambda j, i: (0, jnp.maximum(j - S5_T, 0)))],
        out_specs=pl.BlockSpec((tm, tn), lambda j, i: (i, j)),
        compiler_params=_params("arbitrary", "arbitrary"),
        name="s5_in",
    )(u, d_lag, w_in)


def _s5_scan_kernel(efr_ref, efi_ref, ebr_ref, ebi_ref, a_ref, hfr_ref, hfi_ref, hbr_ref, hbi_ref,
                    *, nb, rows_in, rows_out):
    dirs = ((efr_ref, efi_ref, hfr_ref, hfi_ref, a_ref[0, 0:1, :], a_ref[0, 1:2, :]),
            (ebr_ref, ebi_ref, hbr_ref, hbi_ref, a_ref[1, 0:1, :], a_ref[1, 1:2, :]))
    zero = jnp.zeros_like(dirs[0][4])

    def step(srcs, carry, store):
        new = []
        for di, (er_ref, ei_ref, hr_ref, hi_ref, are, aim) in enumerate(dirs):
            for bi in range(nb):
                hre, him = carry[2 * (di * nb + bi)], carry[2 * (di * nb + bi) + 1]
                if store:
                    hr_ref[pl.ds(bi * rows_out + srcs[di], 1), :] = hre
                    hi_ref[pl.ds(bi * rows_out + srcs[di], 1), :] = him
                ere = er_ref[pl.ds(bi * rows_in + srcs[di], 1), :]
                eim = ei_ref[pl.ds(bi * rows_in + srcs[di], 1), :]
                new += [are * hre - aim * him + ere, are * him + aim * hre + eim]
        return tuple(new)

    n_ctx = rows_in - rows_out
    carry = lax.fori_loop(0, n_ctx, lambda s, c: step((rows_out + s, rows_in - 1 - s), c, False),
                          tuple([zero] * (4 * nb)))
    lax.fori_loop(0, rows_out, lambda s, c: step((s, rows_out - 1 - s), c, True), carry)


def _s5_scan(e, decay, nb, rows_in, rows_out):
    tc = 256
    nsr = S5_GROUPS * S5_STATE
    c0 = (S5_T * S5_WIDTH) // tc
    nt = nsr // tc
    eblk = lambda k: pl.BlockSpec((nb * rows_in, tc), lambda j: (0, c0 + k * nt + j))
    hblk = pl.BlockSpec((nb * rows_out, tc), lambda j: (0, j))
    return pl.pallas_call(
        functools.partial(_s5_scan_kernel, nb=nb, rows_in=rows_in, rows_out=rows_out),
        out_shape=[jax.ShapeDtypeStruct((nb * rows_out, nsr), F32)] * 4,
        grid=(nt,),
        in_specs=[eblk(0), eblk(1), eblk(2), eblk(3), pl.BlockSpec((2, 2, tc), lambda j: (0, 0, j))],
        out_specs=[hblk] * 4,
        compiler_params=_params("arbitrary"),
        name="s5_scan",
    )(e, e, e, e, decay)


def _gelu_tanh(x):
    return 0.5 * x * (1.0 + jnp.tanh(math.sqrt(2.0 / math.pi) * (x + 0.044715 * x * x * x)))


def _s5_out_kernel(hfr_ref, hfi_ref, hbr_ref, hbi_ref, wf_ref, wb_ref, yi_ref, u_ref, d_ref, wg_ref, o_ref):
    nsr = hfr_ref.shape[1]
    y = yi_ref[0] + d_ref[...] * u_ref[...].astype(F32)
    for h_ref, w_ref, r0 in ((hfr_ref, wf_ref, 0), (hfi_ref, wf_ref, nsr), (hbr_ref, wb_ref, 0), (hbi_ref, wb_ref, nsr)):
        y = y + _dot(h_ref[...].astype(BF16), w_ref[r0:r0 + nsr, :])
    y = _gelu_tanh(y)
    gate = _sigmoid(_dot(y.astype(BF16), wg_ref[...]))
    o_ref[...] = (y * gate).astype(o_ref.dtype)


def _s5_out(states, w_out_f, w_out_b, e3, u_rows, d_row, w_glu):
    m, nsr = states[0].shape
    nb = e3.shape[0]
    tm = m // nb
    tn = S5_WIDTH
    st = pl.BlockSpec((tm, nsr), lambda i, j: (i, 0))
    wo = pl.BlockSpec((2 * nsr, tn), lambda i, j: (0, j))
    return pl.pallas_call(
        _s5_out_kernel,
        out_shape=jax.ShapeDtypeStruct((m, S5_T * S5_WIDTH), BF16),
        grid=(nb, S5_T),
        in_specs=[st, st, st, st, wo, wo,
                  pl.BlockSpec((1, tm, tn), lambda i, j: (i, 0, j)),
                  pl.BlockSpec((tm, tn), lambda i, j: (i, j)),
                  pl.BlockSpec((1, tn), lambda i, j: (0, 0)),
                  pl.BlockSpec((tn, tn), lambda i, j: (0, 0))],
        out_specs=pl.BlockSpec((tm, tn), lambda i, j: (i, j)),
        compiler_params=_params("arbitrary", "arbitrary"),
        name="s5_out",
    )(*states, w_out_f, w_out_b, e3, u_rows, d_row, w_glu)


U32 = jnp.uint32
ROW_SUB = 4


def _to_token_rows(ref, val):
    t, d = val.shape

    def rounded(x):
        u = lax.bitcast_convert_type(x, U32)
        return u + (jnp.uint32(0x7FFF) + ((u >> 16) & jnp.uint32(1)))

    w = (rounded(val[:, :d // 2]) >> 16) | (rounded(val[:, d // 2:]) & jnp.uint32(0xFFFF0000))
    for s in range(ROW_SUB):
        ref[pl.ds(s, t, stride=ROW_SUB), :] = w[:, s * LANES:(s + 1) * LANES]


def _from_token_rows(ref, t, row0=0):
    w = jnp.concatenate([ref[pl.ds(row0 * ROW_SUB + s, t, stride=ROW_SUB), :] for s in range(ROW_SUB)], axis=-1)
    lo = lax.bitcast_convert_type(w << 16, F32)
    hi = lax.bitcast_convert_type(w & jnp.uint32(0xFFFF0000), F32)
    return jnp.concatenate([lo, hi], axis=-1)


def _route(h2b, wr_ref, br_ref, cnt_ref, ls8_ref, w8_ref, seg_ref):
    tm = h2b.shape[0]
    per_group = N_EXPERTS // ROUTE_GROUPS
    scores = _sigmoid(lax.dot_general(wr_ref[...], h2b, _NT, preferred_element_type=F32))
    biased = scores + br_ref[...]
    neg = -jnp.inf
    sub = lax.broadcasted_iota(jnp.int32, (per_group, tm), 0)
    grp = []
    for gi in range(ROUTE_GROUPS):
        v = biased[gi * per_group:(gi + 1) * per_group, :]
        m1 = jnp.max(v, axis=0, keepdims=True)
        first = jnp.min(jnp.where(v == m1, sub, per_group), axis=0, keepdims=True)
        m2 = jnp.max(jnp.where(sub == first, neg, v), axis=0, keepdims=True)
        grp.append(m1 + m2)
    grp = jnp.concatenate(grp, axis=0)
    gid = lax.broadcasted_iota(jnp.int32, (ROUTE_GROUPS, tm), 0)
    beaten = jnp.zeros((ROUTE_GROUPS, tm), jnp.int32)
    for gj in range(ROUTE_GROUPS):
        r = grp[gj:gj + 1, :]
        beaten = beaten + jnp.where((r > grp) | ((r == grp) & (gj < gid)), 1, 0)
    group_ok = beaten < TOPK_GROUPS
    expert_ok = jnp.concatenate(
        [jnp.broadcast_to(group_ok[gi:gi + 1, :], (per_group, tm)) for gi in range(ROUTE_GROUPS)], axis=0)
    cur = jnp.where(expert_ok, biased, neg)
    eid = lax.broadcasted_iota(jnp.int32, (N_EXPERTS, tm), 0)
    sel = jnp.zeros((N_EXPERTS, tm), F32)
    picks, wts = [], []
    for _ in range(TOP_K):
        m = jnp.max(cur, axis=0, keepdims=True)
        idx = jnp.min(jnp.where(cur == m, eid, N_EXPERTS), axis=0, keepdims=True)
        hit = eid == idx
        picks.append(idx)
        wts.append(jnp.sum(jnp.where(hit, scores, 0.0), axis=0, keepdims=True))
        sel = jnp.where(hit, 1.0, sel)
        cur = jnp.where(hit, neg, cur)
    wsum = wts[0]
    for w in wts[1:]:
        wsum = wsum + w
    selb = sel.astype(BF16)
    ti = lax.broadcasted_iota(jnp.int32, (tm, tm), 0)
    tj = lax.broadcasted_iota(jnp.int32, (tm, tm), 1)
    rank = _dot(selb, jnp.where(ti < tj, 1.0, 0.0).astype(BF16))
    ei = lax.broadcasted_iota(jnp.int32, (N_EXPERTS, N_EXPERTS), 0)
    ej = lax.broadcasted_iota(jnp.int32, (N_EXPERTS, N_EXPERTS), 1)
    seg_off = jnp.sum(_dot(jnp.where(ej < ei, 1.0, 0.0).astype(BF16), selb), axis=1, keepdims=True)
    seg_cnt = jnp.sum(sel, axis=1, keepdims=True)
    slot = seg_off + rank
    for k in range(TOP_K):
        w8_ref[k:k + 1, :] = wts[k] / wsum * ROUTED_SCALE
        ls8_ref[k:k + 1, :] = jnp.sum(jnp.where(eid == picks[k], slot, 0.0), axis=0, keepdims=True).astype(jnp.int32)
    lane = lax.broadcasted_iota(jnp.int32, (N_EXPERTS, LANES), 1)
    seg_ref[0] = jnp.where(lane == 0, cnt_ref[...], jnp.where(lane == 1, seg_cnt, seg_off))
    cnt_ref[...] = cnt_ref[...] + seg_cnt


def _merge_kernel(x_ref, ya_ref, on_ref, go_ref, ga_ref, gb_ref, g1_ref, sc_ref, sh_ref, n2_ref,
                  pa_ref, pb_ref, wo_ref, wr_ref, br_ref,
                  x1_ref, h2_ref, ls8_ref, w8_ref, seg_ref, cnt_ref):
    @pl.when(pl.program_id(0) == 0)
    def _():
        cnt_ref[...] = jnp.zeros_like(cnt_ref)

    go = go_ref[...].astype(F32)
    y_b = (on_ref[...].astype(F32) * (go * _sigmoid(go))).astype(BF16)
    pa = _dot(ya_ref[...], pa_ref[...])
    pb = _dot(y_b, pb_ref[...])
    merged = _sigmoid(ga_ref[...].astype(F32)) * pa + _sigmoid(gb_ref[...].astype(F32)) * pb
    x1 = x_ref[...] + g1_ref[0] * _dot(merged.astype(BF16), wo_ref[...])
    x1_ref[...] = x1
    y = x1 * lax.rsqrt(jnp.mean(x1 * x1, axis=-1, keepdims=True) + EPS) * n2_ref[...]
    h2 = y * (1.0 + sc_ref[0]) + sh_ref[0]
    _to_token_rows(h2_ref, h2)
    _route(h2.astype(BF16), wr_ref, br_ref, cnt_ref, ls8_ref, w8_ref, seg_ref)


def _merge(x2d, ya, on, go, ga, gb, g1, sc2, sh2, n2g, pa, pb, wo, wr_t, br, rows_per_batch, tm):
    n, d = x2d.shape
    per = rows_per_batch // tm
    row = lambda wd: pl.BlockSpec((tm, wd), lambda i: (i, 0))
    mod = pl.BlockSpec((1, 1, d), lambda i: (i // per, 0, 0))
    full = lambda a: pl.BlockSpec(a.shape, lambda i: (0, 0))
    tok = pl.BlockSpec((TOP_K, tm), lambda i: (0, i))
    return pl.pallas_call(
        _merge_kernel,
        out_shape=[jax.ShapeDtypeStruct((n, d), F32), jax.ShapeDtypeStruct((n * ROW_SUB, LANES), U32),
                   jax.ShapeDtypeStruct((TOP_K, n), jnp.int32), jax.ShapeDtypeStruct((TOP_K, n), F32),
                   jax.ShapeDtypeStruct((n // tm, N_EXPERTS, LANES), F32),
                   jax.ShapeDtypeStruct((N_EXPERTS, 1), F32)],
        grid=(n // tm,),
        in_specs=[row(d), row(S5_WIDTH), row(HG_WIDTH), row(HG_WIDTH), row(d), row(d), mod, mod, mod,
                  pl.BlockSpec((1, d), lambda i: (0, 0)), full(pa), full(pb), full(wo), full(wr_t), full(br)],
        out_specs=[row(d), pl.BlockSpec((tm * ROW_SUB, LANES), lambda i: (i, 0)), tok, tok,
                   pl.BlockSpec((1, N_EXPERTS, LANES), lambda i: (i, 0, 0)),
                   pl.BlockSpec((N_EXPERTS, 1), lambda i: (0, 0))],
        compiler_params=_params("arbitrary"),
        name="merge_out_proj_route",
    )(x2d, ya, on, go, ga, gb, g1, sc2, sh2, n2g.reshape(1, d), pa, pb, wo, wr_t, br)


MOE_TILE = 512
MOE_BLK = 512


def _token_row(ref, r):
    return ref.at[pl.ds(pl.multiple_of(r * ROW_SUB, ROW_SUB), ROW_SUB)]


def _wait_rows(any_ref, sem, n_rows):
    view = any_ref.at[pl.ds(0, n_rows * ROW_SUB)]
    pltpu.make_async_copy(view, view, sem).wait()


def _rows(ref, r0, n):
    return ref.at[pl.ds(pl.multiple_of(r0 * ROW_SUB, ROW_SUB), n * ROW_SUB)]


def _pow2_pieces(n, max_piece, fn):
    done = 0
    piece = max_piece
    while piece >= 1:
        hit = (n & piece) != 0
        pl.when(hit)(functools.partial(fn, done, piece))
        done = done + (n & piece)
        piece //= 2


def _copy_rows(src_ref, src0, dst_ref, dst0, n, max_piece, sem):
    def piece(off, size):
        pltpu.make_async_copy(_rows(src_ref, src0 + off, size), _rows(dst_ref, dst0 + off, size), sem).start()
    _pow2_pieces(n, max_piece, piece)


def _wait_copied_rows(src_ref, dst_ref, n, max_piece, sem):
    def piece(off, size):
        pltpu.make_async_copy(_rows(src_ref, 0, size), _rows(dst_ref, 0, size), sem).wait()
    _pow2_pieces(n, max_piece, piece)


def _dispatch_kernel(gs_ref, cnt_ref, off_ref, pad_ref, ls_ref, h2_ref, xs_hbm, ls_smem, stage0, stage1, zbuf,
                     sem0, sem1, lsem, zsem, *, tm):
    i = pl.program_id(0)
    last = pl.num_programs(0) - 1
    cp = pltpu.make_async_copy(ls_ref, ls_smem, lsem)
    cp.start()
    cp.wait()

    def tile(stage, sem, prev_sem):
        def body(t, carry):
            row = h2_ref[pl.ds(pl.multiple_of(t * ROW_SUB, ROW_SUB), ROW_SUB), :]
            for k in range(TOP_K):
                stage[pl.ds(pl.multiple_of(ls_smem[k, t] * ROW_SUB, ROW_SUB), ROW_SUB), :] = row
            return carry

        lax.fori_loop(0, tm, body, 0)

        def per_expert(e, carry):
            _copy_rows(stage, off_ref[i, e], xs_hbm, gs_ref[i, e], cnt_ref[i, e], tm, sem)
            return carry

        lax.fori_loop(0, N_EXPERTS, per_expert, 0)

        @pl.when(i > 0)
        def _():
            _wait_rows(xs_hbm, prev_sem, tm * TOP_K)

        @pl.when(i == last)
        def _():
            _wait_rows(xs_hbm, sem, tm * TOP_K)

    pl.when(i % 2 == 0)(functools.partial(tile, stage0, sem0, sem1))
    pl.when(i % 2 == 1)(functools.partial(tile, stage1, sem1, sem0))

    @pl.when(i == 0)
    def _():
        zbuf[...] = jnp.zeros_like(zbuf)

        def start(e, carry):
            _copy_rows(zbuf, 0, xs_hbm, pad_ref[0, e], pad_ref[1, e], MOE_BLK // 2, zsem)
            return carry

        def wait(e, carry):
            _wait_copied_rows(zbuf, xs_hbm, pad_ref[1, e], MOE_BLK // 2, zsem)
            return carry

        lax.fori_loop(0, N_EXPERTS, start, 0)
        lax.fori_loop(0, N_EXPERTS, wait, 0)


def _dispatch(gstart, seg_cnt, seg_off, pad, ls8, h2_rows, n_blocks, tm):
    n = ls8.shape[1]
    cap = n_blocks * MOE_BLK
    return pl.pallas_call(
        functools.partial(_dispatch_kernel, tm=tm),
        out_shape=jax.ShapeDtypeStruct((cap * ROW_SUB, LANES), U32),
        grid_spec=pltpu.PrefetchScalarGridSpec(
            num_scalar_prefetch=4,
            grid=(n // tm,),
            in_specs=[pl.BlockSpec((TOP_K, tm), lambda i, *_: (0, i)),
                      pl.BlockSpec((tm * ROW_SUB, LANES), lambda i, *_: (i, 0))],
            out_specs=pl.BlockSpec(memory_space=pl.ANY),
            scratch_shapes=[pltpu.SMEM((TOP_K, tm), jnp.int32),
                            pltpu.VMEM((TOP_K * tm * ROW_SUB, LANES), U32),
                            pltpu.VMEM((TOP_K * tm * ROW_SUB, LANES), U32),
                            pltpu.VMEM((MOE_BLK * ROW_SUB, LANES), U32),
                            pltpu.SemaphoreType.DMA, pltpu.SemaphoreType.DMA, pltpu.SemaphoreType.DMA,
                            pltpu.SemaphoreType.DMA]),
        compiler_params=pltpu.CompilerParams(dimension_semantics=("arbitrary",), vmem_limit_bytes=VMEM_LIMIT,
                                             has_side_effects=True),
        name="moe_dispatch",
    )(gstart, seg_cnt, seg_off, pad, ls8, h2_rows)


def _expert_kernel(be_ref, nu_ref, x_ref, w1_ref, w3_ref, w2_ref, o_ref, w1b, w3b, w2b):
    j = pl.program_id(0)
    e = be_ref[j]
    prev = be_ref[jnp.maximum(j - 1, 0)]
    used = j < nu_ref[0]

    @pl.when(jnp.logical_and(used, jnp.logical_or(j == 0, e != prev)))
    def _():
        w1b[...] = w1_ref[0].astype(BF16)
        w3b[...] = w3_ref[0].astype(BF16)
        w2b[...] = w2_ref[0].astype(BF16)

    @pl.when(used)
    def _():
        x = _from_token_rows(x_ref, MOE_BLK).astype(BF16)
        a = _dot(x, w1b[...])
        hid = (a * _sigmoid(a)) * _dot(x, w3b[...])
        _to_token_rows(o_ref, _dot(hid.astype(BF16), w2b[...]))

    @pl.when(jnp.logical_not(used))
    def _():
        o_ref[...] = jnp.zeros_like(o_ref)


def _experts(block_e, n_used, xs, w1, w3, w2):
    n_blocks = xs.shape[0] // (MOE_BLK * ROW_SUB)
    d, f = w1.shape[1], w1.shape[2]
    rows = pl.BlockSpec((MOE_BLK * ROW_SUB, LANES), lambda j, be, nu: (j, 0))
    return pl.pallas_call(
        _expert_kernel,
        out_shape=jax.ShapeDtypeStruct(xs.shape, U32),
        grid_spec=pltpu.PrefetchScalarGridSpec(
            num_scalar_prefetch=2,
            grid=(n_blocks,),
            in_specs=[rows,
                      pl.BlockSpec((1, d, f), lambda j, be, nu: (be[j], 0, 0)),
                      pl.BlockSpec((1, d, f), lambda j, be, nu: (be[j], 0, 0)),
                      pl.BlockSpec((1, f, d), lambda j, be, nu: (be[j], 0, 0))],
            out_specs=rows,
            scratch_shapes=[pltpu.VMEM((d, f), BF16), pltpu.VMEM((d, f), BF16), pltpu.VMEM((f, d), BF16)]),
        compiler_params=_params("arbitrary"),
        name="moe_experts",
    )(block_e, n_used, xs, w1, w3, w2)


def _combine_kernel(gs_ref, cnt_ref, off_ref, ls_ref, w8_ref, x1_ref, h2_ref, g2_ref, ws1_ref, ws3_ref, ws2_ref,
                    fg_ref, ys_hbm, o_ref, ls_smem, w_smem, gbuf0, gbuf1, acc_rows, sem0, sem1, lsem, *, tm):
    i = pl.program_id(0)
    last = pl.num_programs(0) - 1
    cp1 = pltpu.make_async_copy(ls_ref, ls_smem, lsem)
    cp2 = pltpu.make_async_copy(w8_ref, w_smem, lsem)
    cp1.start()
    cp2.start()

    def fetch(tile, gbuf, sem):
        def per_expert(e, carry):
            _copy_rows(ys_hbm, gs_ref[tile, e], gbuf, off_ref[tile, e], cnt_ref[tile, e], tm, sem)
            return carry

        lax.fori_loop(0, N_EXPERTS, per_expert, 0)

    @pl.when(i == 0)
    def _():
        fetch(0, gbuf0, sem0)

    @pl.when(jnp.logical_and(i < last, i % 2 == 0))
    def _():
        fetch(i + 1, gbuf1, sem1)

    @pl.when(jnp.logical_and(i < last, i % 2 == 1))
    def _():
        fetch(i + 1, gbuf0, sem0)

    h2 = _from_token_rows(h2_ref, tm).astype(BF16)
    a = _dot(h2, ws1_ref[...])
    hid = (a * _sigmoid(a)) * _dot(h2, ws3_ref[...])
    acc = _dot(hid.astype(BF16), ws2_ref[...])
    cp1.wait()
    cp2.wait()

    def reduce_rows(gbuf, sem):
        _wait_rows(gbuf, sem, tm * TOP_K)

        def body(t, carry):
            lo = jnp.zeros((ROW_SUB, LANES), F32)
            hi = jnp.zeros((ROW_SUB, LANES), F32)
            for k in range(TOP_K):
                w = w_smem[k, t]
                words = gbuf[pl.ds(pl.multiple_of(ls_smem[k, t] * ROW_SUB, ROW_SUB), ROW_SUB), :]
                lo = lo + w * lax.bitcast_convert_type(words << 16, F32)
                hi = hi + w * lax.bitcast_convert_type(words & jnp.uint32(0xFFFF0000), F32)
            acc_rows[pl.ds(pl.multiple_of(t * SUBLANES, SUBLANES), ROW_SUB), :] = lo
            acc_rows[pl.ds(pl.multiple_of(t * SUBLANES, SUBLANES) + ROW_SUB, ROW_SUB), :] = hi
            return carry

        lax.fori_loop(0, tm, body, 0)

    pl.when(i % 2 == 0)(functools.partial(reduce_rows, gbuf0, sem0))
    pl.when(i % 2 == 1)(functools.partial(reduce_rows, gbuf1, sem1))
    routed = jnp.concatenate([acc_rows[pl.ds(s, tm, stride=SUBLANES), :] for s in range(SUBLANES)], axis=-1)
    y = x1_ref[...] + g2_ref[0] * (acc + routed)
    o_ref[...] = y * lax.rsqrt(jnp.mean(y * y, axis=-1, keepdims=True) + EPS) * fg_ref[...]


def _combine(gstart, seg_cnt, seg_off, ls8, w8, x1, h2_rows, g2, ws1, ws3, ws2, fg, ys, rows_per_batch, tm):
    n, d = x1.shape
    per = rows_per_batch // tm
    tok = pl.BlockSpec((TOP_K, tm), lambda i, *_: (0, i))
    full = lambda a: pl.BlockSpec(a.shape, lambda i, *_: (0, 0))
    return pl.pallas_call(
        functools.partial(_combine_kernel, tm=tm),
        out_shape=jax.ShapeDtypeStruct((n, d), F32),
        grid_spec=pltpu.PrefetchScalarGridSpec(
            num_scalar_prefetch=3,
            grid=(n // tm,),
            in_specs=[tok, tok, pl.BlockSpec((tm, d), lambda i, *_: (i, 0)),
                      pl.BlockSpec((tm * ROW_SUB, LANES), lambda i, *_: (i, 0)),
                      pl.BlockSpec((1, 1, d), lambda i, *_: (i // per, 0, 0)),
                      full(ws1), full(ws3), full(ws2), pl.BlockSpec((1, d), lambda i, *_: (0, 0)),
                      pl.BlockSpec(memory_space=pl.ANY)],
            out_specs=pl.BlockSpec((tm, d), lambda i, *_: (i, 0)),
            scratch_shapes=[pltpu.SMEM((TOP_K, tm), jnp.int32), pltpu.SMEM((TOP_K, tm), F32),
                            pltpu.VMEM((TOP_K * tm * ROW_SUB, LANES), U32),
                            pltpu.VMEM((TOP_K * tm * ROW_SUB, LANES), U32),
                            pltpu.VMEM((tm * SUBLANES, LANES), F32), pltpu.SemaphoreType.DMA,
                            pltpu.SemaphoreType.DMA, pltpu.SemaphoreType.DMA]),
        compiler_params=_params("arbitrary"),
        name="moe_combine_final",
    )(gstart, seg_cnt, seg_off, ls8, w8, x1, h2_rows, g2, ws1, ws3, ws2, fg.reshape(1, d), ys)


def _moe_plan(seg, counts, n_assign):
    cnt = counts.reshape(N_EXPERTS).astype(jnp.int32)
    padded = (cnt + MOE_BLK - 1) // MOE_BLK * MOE_BLK
    pends = jnp.cumsum(padded)
    pstarts = pends - padded
    n_blocks = (n_assign + N_EXPERTS * (MOE_BLK - 1) + MOE_BLK - 1) // MOE_BLK
    seg = seg[:, :, :3].astype(jnp.int32)
    gstart = pstarts[None, :] + seg[:, :, 0]
    blk_start = jnp.arange(n_blocks, dtype=jnp.int32) * MOE_BLK
    block_e = jnp.minimum(jnp.sum((blk_start[:, None] >= pends[None, :]).astype(jnp.int32), axis=1),
                          N_EXPERTS - 1).astype(jnp.int32)
    pad = jnp.stack([pstarts + cnt, padded - cnt], axis=0).astype(jnp.int32)
    n_used = (pends[-1:] // MOE_BLK).astype(jnp.int32)
    return gstart, seg[:, :, 1], seg[:, :, 2], pad, block_e, n_used, n_blocks


def _mixer(x, c, ctx, c_ctx, w_ada, b_ada, norm1_g, norm2_g, w_in, s5_lam_re, s5_lam_im, s5_log_dt,
           s5_b_re, s5_b_im, s5_c_re, s5_c_im, s5_d, s5_w_glu, lb, hg_norm_g, p_a, p_b, w_out,
           moe_w_router, moe_b_router):
    b, l, d = x.shape
    lc = ctx.shape[1]
    n = b * l
    rows = l // GRID_W

    c8 = jnp.concatenate([c, c_ctx[None], jnp.zeros((8 - b - 1, d), F32)], axis=0)
    mod = _ada(c8, w_ada, b_ada)
    sh1, sc1, g1, sh2, sc2, g2 = [mod[:b, k * d:(k + 1) * d].reshape(b, 1, d) for k in range(6)]
    csh1, csc1 = mod[b:b + 1, 0:d].reshape(1, 1, d), mod[b:b + 1, d:2 * d].reshape(1, 1, d)

    w_in_b = w_in.astype(BF16)
    z = dict(zip([p[0] for p in _IN_PIECES], _inproj(x.reshape(n, d), sc1, sh1, norm1_g, w_in_b, l, 512)))
    zc = dict(zip([p[0] for p in _IN_PIECES], _inproj(ctx.reshape(b * lc, d), csc1, csh1, norm1_g, w_in_b, lc, lc)))

    cm = lambda t: t.reshape(b, rows, GRID_W * HG_WIDTH)
    cx = lambda t: t.reshape(b, lc, HG_WIDTH)
    lb_row = lb.reshape(1, HG_WIDTH)
    o_f = _hgrn_pass(cm(z["q"]), cm(z["ff"]), cm(z["i"]), cx(zc["ff"]), cx(zc["i"]), lb_row, None, None,
                     reverse=False)
    o_n = _hgrn_pass(cm(z["q"]), cm(z["fb"]), cm(z["i"]), cx(zc["fb"]), cx(zc["i"]), lb_row, o_f,
                     hg_norm_g.reshape(1, HG_DK), reverse=True)
    o_n = o_n.reshape(n, HG_WIDTH)

    d_lag, w_s5_in, w_out_f, w_out_b, decay = _s5_weights(s5_lam_re, s5_lam_im, s5_log_dt, s5_b_re, s5_b_im,
                                                          s5_c_re, s5_c_im)
    kc, kl = lc // S5_T, l // S5_T
    u_lat = z["u"].reshape(b, kl, S5_T * S5_WIDTH)
    u_ctx = zc["u"].reshape(b, kc, S5_T * S5_WIDTH)
    rows_in = kl + kc
    u_ext = jnp.concatenate([u_lat, u_ctx], axis=1).reshape(b * rows_in, S5_T * S5_WIDTH)
    e = _s5_in(u_ext, d_lag, w_s5_in, (b * rows_in) // 2)
    states = _s5_scan(e, decay, b, rows_in, kl)
    d_row = s5_d.astype(F32).reshape(1, S5_WIDTH)
    y_a = _s5_out(states, w_out_f, w_out_b, e.reshape(b, rows_in, -1), u_lat.reshape(b * kl, -1), d_row,
                  s5_w_glu.astype(BF16))
    y_a = y_a.reshape(n, S5_WIDTH)

    return _merge(x.reshape(n, d), y_a, o_n, z["go"], z["ga"], z["gb"], g1, sc2, sh2, norm2_g,
                  p_a.astype(BF16), p_b.astype(BF16), w_out.astype(BF16),
                  moe_w_router.T.astype(BF16), moe_b_router.astype(F32).reshape(N_EXPERTS, 1), l, MOE_TILE) + (g2,)


def kernel(x, c, ctx, c_ctx, w_ada, b_ada, norm1_g, norm2_g, w_in, s5_lam_re, s5_lam_im, s5_log_dt, s5_b_re,
           s5_b_im, s5_c_re, s5_c_im, s5_d, s5_w_glu, hg_lb_logits, hg_norm_g, p_a, p_b, w_out, moe_w_router,
           moe_b_router, moe_w1, moe_w3, moe_w2, moe_ws1, moe_ws3, moe_ws2, final_norm_g):
    b, l, d = x.shape
    n = b * l
    assert w_ada.shape[0] == 1, "single-layer block"
    lb = jnp.cumsum(jax.nn.softmax(hg_lb_logits.astype(F32), axis=0), axis=0)[0]
    x1, h2_rows, ls8, w8, seg, counts, g2 = _mixer(
        x, c, ctx, c_ctx, w_ada[0], b_ada[0], norm1_g[0], norm2_g[0], w_in[0], s5_lam_re[0], s5_lam_im[0],
        s5_log_dt[0], s5_b_re[0], s5_b_im[0], s5_c_re[0], s5_c_im[0], s5_d[0], s5_w_glu[0], lb, hg_norm_g[0],
        p_a[0], p_b[0], w_out[0], moe_w_router[0], moe_b_router[0])
    gstart, seg_cnt, seg_off, pad, block_e, n_used, n_blocks = _moe_plan(seg, counts, n * TOP_K)
    xs = _dispatch(gstart, seg_cnt, seg_off, pad, ls8, h2_rows, n_blocks, MOE_TILE)
    ys = _experts(block_e, n_used, xs, moe_w1[0], moe_w3[0], moe_w2[0])
    out = _combine(gstart, seg_cnt, seg_off, ls8, w8, x1, h2_rows, g2, moe_ws1[0].astype(BF16),
                   moe_ws3[0].astype(BF16), moe_ws2[0].astype(BF16), final_norm_g, ys, l, MOE_TILE)
    return out.reshape(b, l, d)
```

```python
import functools
import math

import jax
import jax.numpy as jnp
from jax import lax
from jax.experimental import pallas as pl
from jax.experimental.pallas import tpu as pltpu

F32 = jnp.float32
BF16 = jnp.bfloat16

GRID_W = 64
S5_WIDTH = 256
S5_GROUP = 16
S5_GROUPS = 16
S5_STATE = 64
HG_HEADS = 6
HG_DK = 128
HG_WIDTH = HG_HEADS * HG_DK
N_EXPERTS = 64
ROUTE_GROUPS = 8
TOPK_GROUPS = 4
TOP_K = 8
ROUTED_SCALE = 2.5
EPS = 1e-6

LANES = 128
SUBLANES = 8

TOK_TILE = 512
S5_T = 16
HG_CHUNK = 64
VMEM_LIMIT = 56 * 1024 * 1024

_NT = (((1,), (1,)), ((), ()))
_TN = (((0,), (0,)), ((), ()))


def _params(*sem):
    return pltpu.CompilerParams(dimension_semantics=sem, vmem_limit_bytes=VMEM_LIMIT)


def _dot(a, b):
    return jnp.dot(a, b, preferred_element_type=F32)


def _sigmoid(x):
    return 1.0 / (1.0 + jnp.exp(-x))


def _ada_kernel(c_ref, w_ref, b_ref, o_ref):
    c = c_ref[...]
    s = (c * _sigmoid(c)).astype(BF16)
    o_ref[...] = _dot(s, w_ref[...].astype(BF16)) + b_ref[...]


def _ada(c8, w_ada, b_ada):
    d, n = w_ada.shape
    tn = 1536
    return pl.pallas_call(
        _ada_kernel,
        out_shape=jax.ShapeDtypeStruct((8, n), F32),
        grid=(n // tn,),
        in_specs=[pl.BlockSpec((8, d), lambda j: (0, 0)),
                  pl.BlockSpec((d, tn), lambda j: (0, j)),
                  pl.BlockSpec((1, tn), lambda j: (0, j))],
        out_specs=pl.BlockSpec((8, tn), lambda j: (0, j)),
        compiler_params=_params("arbitrary"),
        name="ada_mod",
    )(c8, w_ada, b_ada.reshape(1, n))


_IN_PIECES = (("u", 0, 256, BF16), ("q", 256, 768, BF16), ("ff", 1024, 768, F32),
              ("fb", 1792, 768, F32), ("i", 2560, 768, BF16), ("go", 3328, 768, BF16),
              ("ga", 4096, 1024, BF16), ("gb", 5120, 1024, BF16))


def _inproj_kernel(x_ref, sc_ref, sh_ref, g_ref, w_ref, *o_refs):
    x = x_ref[...]
    y = x * lax.rsqrt(jnp.mean(x * x, axis=-1, keepdims=True) + EPS) * g_ref[...]
    h = (y * (1.0 + sc_ref[0]) + sh_ref[0]).astype(BF16)
    for (_, a, wd, _), o_ref in zip(_IN_PIECES, o_refs):
        res = _dot(h, w_ref[:, a:a + wd]).astype(o_ref.dtype)
        if len(o_ref.shape) == 2:
            o_ref[...] = res
        else:
            for rr in range(o_ref.shape[2]):
                o_ref[0, :, rr, :] = res[rr * GRID_W:(rr + 1) * GRID_W, :]


_COLMAJOR_PIECES = ("q", "ff", "fb", "i")


def _inproj(x2d, sc, sh, g, w_bf16, rows_per_mod, tm, colmajor):
    n, d = x2d.shape
    per = rows_per_mod // tm
    mod_map = (lambda i: (i // per, 0, 0)) if sc.shape[0] > 1 else (lambda i: (0, 0, 0))
    shapes, specs = [], []
    for name, _, wd, dt in _IN_PIECES:
        if colmajor and name in _COLMAJOR_PIECES:
            shapes.append(jax.ShapeDtypeStruct((n // rows_per_mod, GRID_W, rows_per_mod // GRID_W, wd), dt))
            specs.append(pl.BlockSpec((1, GRID_W, tm // GRID_W, wd), lambda i: (i // per, 0, i % per, 0)))
        else:
            shapes.append(jax.ShapeDtypeStruct((n, wd), dt))
            specs.append(pl.BlockSpec((tm, wd), lambda i: (i, 0)))
    return pl.pallas_call(
        _inproj_kernel,
        out_shape=shapes,
        grid=(n // tm,),
        in_specs=[pl.BlockSpec((tm, d), lambda i: (i, 0)),
                  pl.BlockSpec((1, 1, d), mod_map),
                  pl.BlockSpec((1, 1, d), mod_map),
                  pl.BlockSpec((1, d), lambda i: (0, 0)),
                  pl.BlockSpec(w_bf16.shape, lambda i: (0, 0))],
        out_specs=specs,
        compiler_params=_params("arbitrary"),
        name="in_proj",
    )(x2d, sc, sh, g.reshape(1, d), w_bf16)


def _hgrn_gates(zf, lb):
    sig = _sigmoid(zf)
    logf = jnp.log(lb + (1.0 - lb) * sig)
    k = (1.0 - lb) * (1.0 - sig)
    return logf, k


def _chunk_cumsum(cs, logf):
    hi = logf.astype(BF16)
    lo = (logf - hi.astype(F32)).astype(BF16)
    return _dot(cs, hi) + _dot(cs, lo)


def _hgrn_state_step(zf, v, lb, st, cs, reverse):
    logf, k = _hgrn_gates(zf, lb)
    cum = _chunk_cumsum(cs, logf)
    t = 0 if reverse else HG_CHUNK - 1
    total = cum[t:t + 1, :]
    kdec = (k * jnp.exp(total - cum)).astype(BF16)
    st_new = st * jnp.exp(total) + lax.dot_general(v.astype(BF16), kdec, _TN, preferred_element_type=F32)
    return cum, k, st_new


def _hgrn_kernel(*refs, reverse, final, n_ctx_chunks):
    if final:
        q_ref, f_ref, v_ref, cf_ref, cv_ref, lb_ref, of_ref, g_ref, o_ref, st_ref = refs
        of_ref = of_ref.at[0]
    else:
        q_ref, f_ref, v_ref, cf_ref, cv_ref, lb_ref, o_ref, st_ref = refs
    q_ref, f_ref, v_ref, o_ref = q_ref.at[0], f_ref.at[0], v_ref.at[0], o_ref.at[0]
    c_len = HG_CHUNK
    n_rows = q_ref.shape[1]
    n_chunks = n_rows // c_len
    row = lax.broadcasted_iota(jnp.int32, (n_rows, n_rows), 0)
    col = lax.broadcasted_iota(jnp.int32, (n_rows, n_rows), 1)
    tri = (col >= row) if reverse else (col <= row)
    same_chunk = None
    for c in range(n_chunks):
        lo, hi = c * c_len, (c + 1) * c_len
        blk = (row >= lo) & (row < hi) & (col >= lo) & (col < hi)
        same_chunk = blk if same_chunk is None else (same_chunk | blk)
    mask = tri & same_chunk
    cs = jnp.where(mask, 1.0, 0.0).astype(BF16)

    @pl.when(pl.program_id(1) == 0)
    def _():
        cs1 = cs[:c_len, :c_len]
        order = range(n_ctx_chunks - 1, -1, -1) if reverse else range(n_ctx_chunks)
        for h in range(HG_HEADS):
            cols = slice(h * HG_DK, (h + 1) * HG_DK)
            st = jnp.zeros((HG_DK, HG_DK), F32)
            for c in order:
                rows = slice(c * c_len, (c + 1) * c_len)
                _, _, st = _hgrn_state_step(cf_ref[0, rows, cols], cv_ref[0, rows, cols].astype(F32),
                                            lb_ref[:, cols], st, cs1, reverse)
            st_ref[h] = st

    def per_chunk_rows(x, r):
        return jnp.concatenate([jnp.broadcast_to(x[c * c_len + r:c * c_len + r + 1, :], (c_len, x.shape[1]))
                                for c in range(n_chunks)], axis=0)

    lb = lb_ref[...]
    q = q_ref[0].astype(F32)
    v = v_ref[0]
    logf, k = _hgrn_gates(f_ref[0], lb)
    cum = _chunk_cumsum(cs, logf)
    r_ref = c_len // 2 - 1 if reverse else c_len // 2
    r_tot = 0 if reverse else c_len - 1
    ref = per_chunk_rows(cum, r_ref)
    qe = q * jnp.exp(cum - ref)
    ke = k * jnp.exp(ref - cum)
    qi, ki = qe.astype(BF16), ke.astype(BF16)
    q_in = (qe * jnp.exp(ref)).astype(BF16)
    tail = jnp.exp(per_chunk_rows(cum, r_tot) - ref)
    kdec = (ke * tail).astype(BF16)

    order = range(n_chunks - 1, -1, -1) if reverse else range(n_chunks)
    for h in range(HG_HEADS):
        cols = slice(h * HG_DK, (h + 1) * HG_DK)
        s = lax.dot_general(qi[:, cols], ki[:, cols], _NT, preferred_element_type=F32)
        o_intra = _dot(jnp.where(mask, s, 0.0).astype(BF16), v[:, cols])
        st = st_ref[h]
        for c in order:
            rows = slice(c * c_len, (c + 1) * c_len)
            o = o_intra[rows] + lax.dot_general(q_in[rows, cols], st.astype(BF16), _NT, preferred_element_type=F32)
            total = cum[c * c_len + r_tot:c * c_len + r_tot + 1, cols]
            st = st * jnp.exp(total) + lax.dot_general(v[rows, cols], kdec[rows, cols], _TN,
                                                       preferred_element_type=F32)
            if final:
                o = o + of_ref[0, rows, cols]
                o = o * lax.rsqrt(jnp.mean(o * o, axis=-1, keepdims=True) + EPS) * g_ref[...]
            o_ref[0, rows, cols] = o.astype(o_ref.dtype)
        st_ref[h] = st


def _hgrn_pass(q, f, v, cf, cv, lb, o_prev, g, *, reverse):
    b, nw, rows, _ = q.shape
    final = o_prev is not None
    wmap = (lambda bi, w: (bi, nw - 1 - w, 0, 0)) if reverse else (lambda bi, w: (bi, w, 0, 0))
    blk = pl.BlockSpec((1, 1, rows, HG_WIDTH), wmap)
    cblk = pl.BlockSpec((1, cf.shape[1], HG_WIDTH), lambda bi, w: (bi, 0, 0))
    in_specs = [blk, blk, blk, cblk, cblk, pl.BlockSpec((1, HG_WIDTH), lambda bi, w: (0, 0))]
    args = [q, f, v, cf, cv, lb]
    if final:
        in_specs += [blk, pl.BlockSpec((1, HG_DK), lambda bi, w: (0, 0))]
        args += [o_prev, g]
    return pl.pallas_call(
        functools.partial(_hgrn_kernel, reverse=reverse, final=final, n_ctx_chunks=cf.shape[1] // HG_CHUNK),
        out_shape=jax.ShapeDtypeStruct(q.shape, BF16 if final else F32),
        grid=(b, nw),
        in_specs=in_specs,
        out_specs=blk,
        scratch_shapes=[pltpu.VMEM((HG_HEADS, HG_DK, HG_DK), F32)],
        compiler_params=_params("arbitrary", "arbitrary"),
        name="hgrn_bwd" if reverse else "hgrn_fwd",
    )(*args)


def _s5_weights(lam_re, lam_im, log_dt, b_re, b_im, c_re, c_im):
    hp = lax.Precision.HIGHEST
    g, p, cc, t = S5_GROUPS, S5_STATE, S5_GROUP, S5_T
    lre = jnp.minimum(lam_re.astype(F32), -1e-4)
    lim = lam_im.astype(F32)
    dt = jnp.exp(log_dt.astype(F32))[..., None]
    ks = jnp.arange(t + 1, dtype=F32)[:, None, None, None]
    mag = jnp.exp(ks * (lre * dt)[None])
    pw_re = mag * jnp.cos(ks * (lim * dt)[None])
    pw_im = mag * jnp.sin(ks * (lim * dt)[None])
    nr, ni = pw_re[1] - 1.0, pw_im[1]
    den = lre * lre + lim * lim
    cf_re = (nr * lre + ni * lim) / den
    cf_im = (ni * lre - nr * lim) / den
    bb_re = cf_re[..., None] * b_re - cf_im[..., None] * b_im
    bb_im = cf_re[..., None] * b_im + cf_im[..., None] * b_re
    cre, cim = c_re.astype(F32), c_im.astype(F32)
    sw, ns = S5_WIDTH, 2 * g * p
    grp_of_row = jnp.arange(sw)[:, None] // cc

    cp_re = cre[None, None, :, :, :] * pw_re[:t, :, :, None, :] - cim[None, None] * pw_im[:t, :, :, None, :]
    cp_im = cre[None, None, :, :, :] * pw_im[:t, :, :, None, :] + cim[None, None] * pw_re[:t, :, :, None, :]
    kk = (jnp.einsum("kdgop,dgpi->dkgoi", cp_re, bb_re, precision=hp)
          - jnp.einsum("kdgop,dgpi->dkgoi", cp_im, bb_im, precision=hp))
    kf, kb = kk[0], kk[1]
    kall = jnp.concatenate([kb[:0:-1], (kf[0] + kb[0])[None], kf[1:]], axis=0)
    kt = kall.transpose(0, 1, 3, 2).reshape(2 * t - 1, sw, cc)
    same = grp_of_row == (jnp.arange(sw)[None, :] // cc)
    d_lag = jnp.where(same[None], jnp.tile(kt, (1, 1, g)), 0.0).astype(BF16)

    same_in = grp_of_row == ((jnp.arange(ns)[None, :] % (g * p)) // p)

    def in_to_state(pre, pim, bre, bim):
        xre = pre[..., None] * bre[None] - pim[..., None] * bim[None]
        xim = pre[..., None] * bim[None] + pim[..., None] * bre[None]
        x = jnp.concatenate([jnp.tile(xre.transpose(0, 1, 3, 2).reshape(t, sw, p), (1, 1, g)),
                             jnp.tile(xim.transpose(0, 1, 3, 2).reshape(t, sw, p), (1, 1, g))], axis=-1)
        return jnp.where(same_in[None], x, 0.0).reshape(t * sw, ns)

    w_in = jnp.concatenate([in_to_state(pw_re[t - 1::-1, 0], pw_im[t - 1::-1, 0], bb_re[0], bb_im[0]),
                            in_to_state(pw_re[:t, 1], pw_im[:t, 1], bb_re[1], bb_im[1])], axis=1).astype(BF16)

    same_out = ((jnp.arange(ns)[:, None] % (g * p)) // p) == ((jnp.arange(t * sw)[None, :] // cc) % g)

    def state_to_out(pre, pim):
        are = cre[None] * pre[:, :, None, :] - cim[None] * pim[:, :, None, :]
        aim = cre[None] * pim[:, :, None, :] + cim[None] * pre[:, :, None, :]
        a = jnp.concatenate([are.transpose(1, 3, 0, 2), -aim.transpose(1, 3, 0, 2)], axis=0)
        a = jnp.broadcast_to(a.reshape(ns, t, 1, cc), (ns, t, g, cc)).reshape(ns, t * sw)
        return jnp.where(same_out, a, 0.0).astype(BF16)

    w_out_f = state_to_out(pw_re[1:, 0], pw_im[1:, 0])
    w_out_b = state_to_out(pw_re[t:0:-1, 1], pw_im[t:0:-1, 1])

    decay = jnp.stack([pw_re[t].reshape(2, g * p), pw_im[t].reshape(2, g * p)], axis=1)
    return d_lag, w_in, w_out_f, w_out_b, decay


def _s5_in_kernel(u_ref, d_ref, w_ref, o_ref):
    j = pl.program_id(0)

    @pl.when(j < S5_T)
    def _():
        acc = _dot(u_ref[:, 0:S5_WIDTH], d_ref[j + S5_T - 1])
        for s in range(1, S5_T):
            acc = acc + _dot(u_ref[:, s * S5_WIDTH:(s + 1) * S5_WIDTH], d_ref[j - s + S5_T - 1])
        o_ref[...] = acc

    @pl.when(j >= S5_T)
    def _():
        o_ref[...] = _dot(u_ref[...], w_ref[...])


def _s5_in(u, d_lag, w_in, tm):
    m, k = u.shape
    tn = S5_WIDTH
    nj = (k + w_in.shape[1]) // tn
    return pl.pallas_call(
        _s5_in_kernel,
        out_shape=jax.ShapeDtypeStruct((m, nj * tn), F32),
        grid=(nj, m // tm),
        in_specs=[pl.BlockSpec((tm, k), lambda j, i: (i, 0)),
                  pl.BlockSpec(d_lag.shape, lambda j, i: (0, 0, 0)),
                  pl.BlockSpec((k, tn), lambda j, i: (0, jnp.maximum(j - S5_T, 0)))],
        out_specs=pl.BlockSpec((tm, tn), lambda j, i: (i, j)),
        compiler_params=_params("arbitrary", "arbitrary"),
        name="s5_in",
    )(u, d_lag, w_in)


def _s5_scan_kernel(efr_ref, efi_ref, ebr_ref, ebi_ref, a_ref, hfr_ref, hfi_ref, hbr_ref, hbi_ref,
                    *, nb, rows_in, rows_out):
    dirs = ((efr_ref, efi_ref, hfr_ref, hfi_ref, a_ref[0, 0:1, :], a_ref[0, 1:2, :]),
            (ebr_ref, ebi_ref, hbr_ref, hbi_ref, a_ref[1, 0:1, :], a_ref[1, 1:2, :]))
    zero = jnp.zeros_like(dirs[0][4])

    def step(srcs, carry, store):
        new = []
        for di, (er_ref, ei_ref, hr_ref, hi_ref, are, aim) in enumerate(dirs):
            for bi in range(nb):
                hre, him = carry[2 * (di * nb + bi)], carry[2 * (di * nb + bi) + 1]
                if store:
                    hr_ref[pl.ds(bi * rows_out + srcs[di], 1), :] = hre
                    hi_ref[pl.ds(bi * rows_out + srcs[di], 1), :] = him
                ere = er_ref[pl.ds(bi * rows_in + srcs[di], 1), :]
                eim = ei_ref[pl.ds(bi * rows_in + srcs[di], 1), :]
                new += [are * hre - aim * him + ere, are * him + aim * hre + eim]
        return tuple(new)

    n_ctx = rows_in - rows_out
    carry = lax.fori_loop(0, n_ctx, lambda s, c: step((rows_out + s, rows_in - 1 - s), c, False),
                          tuple([zero] * (4 * nb)))
    lax.fori_loop(0, rows_out, lambda s, c: step((s, rows_out - 1 - s), c, True), carry)


def _s5_scan(e, decay, nb, rows_in, rows_out):
    tc = 256
    nsr = S5_GROUPS * S5_STATE
    c0 = (S5_T * S5_WIDTH) // tc
    nt = nsr // tc
    eblk = lambda k: pl.BlockSpec((nb * rows_in, tc), lambda j: (0, c0 + k * nt + j))
    hblk = pl.BlockSpec((nb * rows_out, tc), lambda j: (0, j))
    return pl.pallas_call(
        functools.partial(_s5_scan_kernel, nb=nb, rows_in=rows_in, rows_out=rows_out),
        out_shape=[jax.ShapeDtypeStruct((nb * rows_out, nsr), F32)] * 4,
        grid=(nt,),
        in_specs=[eblk(0), eblk(1), eblk(2), eblk(3), pl.BlockSpec((2, 2, tc), lambda j: (0, 0, j))],
        out_specs=[hblk] * 4,
        compiler_params=_params("arbitrary"),
        name="s5_scan",
    )(e, e, e, e, decay)


def _gelu_tanh(x):
    return 0.5 * x * (1.0 + jnp.tanh(math.sqrt(2.0 / math.pi) * (x + 0.044715 * x * x * x)))


def _s5_out_kernel(hfr_ref, hfi_ref, hbr_ref, hbi_ref, wf_ref, wb_ref, yi_ref, u_ref, d_ref, wg_ref, o_ref):
    nsr = hfr_ref.shape[1]
    y = yi_ref[0] + d_ref[...] * u_ref[...].astype(F32)
    for h_ref, w_ref, r0 in ((hfr_ref, wf_ref, 0), (hfi_ref, wf_ref, nsr), (hbr_ref, wb_ref, 0), (hbi_ref, wb_ref, nsr)):
        y = y + _dot(h_ref[...].astype(BF16), w_ref[r0:r0 + nsr, :])
    y = _gelu_tanh(y)
    gate = _sigmoid(_dot(y.astype(BF16), wg_ref[...]))
    o_ref[...] = (y * gate).astype(o_ref.dtype)


def _s5_out(states, w_out_f, w_out_b, e3, u_rows, d_row, w_glu):
    m, nsr = states[0].shape
    nb = e3.shape[0]
    tm = m // nb
    tn = S5_WIDTH
    st = pl.BlockSpec((tm, nsr), lambda i, j: (i, 0))
    wo = pl.BlockSpec((2 * nsr, tn), lambda i, j: (0, j))
    return pl.pallas_call(
        _s5_out_kernel,
        out_shape=jax.ShapeDtypeStruct((m, S5_T * S5_WIDTH), BF16),
        grid=(nb, S5_T),
        in_specs=[st, st, st, st, wo, wo,
                  pl.BlockSpec((1, tm, tn), lambda i, j: (i, 0, j)),
                  pl.BlockSpec((tm, tn), lambda i, j: (i, j)),
                  pl.BlockSpec((1, tn), lambda i, j: (0, 0)),
                  pl.BlockSpec((tn, tn), lambda i, j: (0, 0))],
        out_specs=pl.BlockSpec((tm, tn), lambda i, j: (i, j)),
        compiler_params=_params("arbitrary", "arbitrary"),
        name="s5_out",
    )(*states, w_out_f, w_out_b, e3, u_rows, d_row, w_glu)


U32 = jnp.uint32
ROW_SUB = 4


def _to_token_rows(ref, val):
    t, d = val.shape

    def rounded(x):
        u = lax.bitcast_convert_type(x, U32)
        return u + (jnp.uint32(0x7FFF) + ((u >> 16) & jnp.uint32(1)))

    w = (rounded(val[:, :d // 2]) >> 16) | (rounded(val[:, d // 2:]) & jnp.uint32(0xFFFF0000))
    for s in range(ROW_SUB):
        ref[pl.ds(s, t, stride=ROW_SUB), :] = w[:, s * LANES:(s + 1) * LANES]


def _from_token_rows(ref, t, row0=0):
    w = jnp.concatenate([ref[pl.ds(row0 * ROW_SUB + s, t, stride=ROW_SUB), :] for s in range(ROW_SUB)], axis=-1)
    lo = lax.bitcast_convert_type(w << 16, F32)
    hi = lax.bitcast_convert_type(w & jnp.uint32(0xFFFF0000), F32)
    return jnp.concatenate([lo, hi], axis=-1)


def _route(h2b, wr_ref, br_ref, cnt_ref, ls8_ref, w8_ref, seg_ref):
    tm = h2b.shape[0]
    per_group = N_EXPERTS // ROUTE_GROUPS
    scores = _sigmoid(lax.dot_general(wr_ref[...], h2b, _NT, preferred_element_type=F32))
    biased = scores + br_ref[...]
    neg = -jnp.inf
    sub = lax.broadcasted_iota(jnp.int32, (per_group, tm), 0)
    grp = []
    for gi in range(ROUTE_GROUPS):
        v = biased[gi * per_group:(gi + 1) * per_group, :]
        m1 = jnp.max(v, axis=0, keepdims=True)
        first = jnp.min(jnp.where(v == m1, sub, per_group), axis=0, keepdims=True)
        m2 = jnp.max(jnp.where(sub == first, neg, v), axis=0, keepdims=True)
        grp.append(m1 + m2)
    grp = jnp.concatenate(grp, axis=0)
    gid = lax.broadcasted_iota(jnp.int32, (ROUTE_GROUPS, tm), 0)
    beaten = jnp.zeros((ROUTE_GROUPS, tm), jnp.int32)
    for gj in range(ROUTE_GROUPS):
        r = grp[gj:gj + 1, :]
        beaten = beaten + jnp.where((r > grp) | ((r == grp) & (gj < gid)), 1, 0)
    group_ok = beaten < TOPK_GROUPS
    expert_ok = jnp.concatenate(
        [jnp.broadcast_to(group_ok[gi:gi + 1, :], (per_group, tm)) for gi in range(ROUTE_GROUPS)], axis=0)
    cur = jnp.where(expert_ok, biased, neg)
    eid = lax.broadcasted_iota(jnp.int32, (N_EXPERTS, tm), 0)
    sel = jnp.zeros((N_EXPERTS, tm), F32)
    picks, wts = [], []
    for _ in range(TOP_K):
        m = jnp.max(cur, axis=0, keepdims=True)
        idx = jnp.min(jnp.where(cur == m, eid, N_EXPERTS), axis=0, keepdims=True)
        hit = eid == idx
        picks.append(idx)
        wts.append(jnp.sum(jnp.where(hit, scores, 0.0), axis=0, keepdims=True))
        sel = jnp.where(hit, 1.0, sel)
        cur = jnp.where(hit, neg, cur)
    wsum = wts[0]
    for w in wts[1:]:
        wsum = wsum + w
    selb = sel.astype(BF16)
    ti = lax.broadcasted_iota(jnp.int32, (tm, tm), 0)
    tj = lax.broadcasted_iota(jnp.int32, (tm, tm), 1)
    rank = _dot(selb, jnp.where(ti < tj, 1.0, 0.0).astype(BF16))
    ei = lax.broadcasted_iota(jnp.int32, (N_EXPERTS, N_EXPERTS), 0)
    ej = lax.broadcasted_iota(jnp.int32, (N_EXPERTS, N_EXPERTS), 1)
    seg_off = jnp.sum(_dot(jnp.where(ej < ei, 1.0, 0.0).astype(BF16), selb), axis=1, keepdims=True)
    seg_cnt = jnp.sum(sel, axis=1, keepdims=True)
    slot = seg_off + rank
    for k in range(TOP_K):
        w8_ref[k:k + 1, :] = wts[k] / wsum * ROUTED_SCALE
        ls8_ref[k:k + 1, :] = jnp.sum(jnp.where(eid == picks[k], slot, 0.0), axis=0, keepdims=True).astype(jnp.int32)
    lane = lax.broadcasted_iota(jnp.int32, (N_EXPERTS, LANES), 1)
    seg_ref[0] = jnp.where(lane == 0, cnt_ref[...], jnp.where(lane == 1, seg_cnt, seg_off))
    cnt_ref[...] = cnt_ref[...] + seg_cnt


def _merge_kernel(x_ref, ya_ref, on_ref, go_ref, ga_ref, gb_ref, g1_ref, sc_ref, sh_ref, n2_ref,
                  pa_ref, pb_ref, wo_ref, wr_ref, br_ref,
                  x1_ref, h2_ref, ls8_ref, w8_ref, seg_ref, cnt_ref):
    @pl.when(pl.program_id(0) == 0)
    def _():
        cnt_ref[...] = jnp.zeros_like(cnt_ref)

    go = go_ref[...].astype(F32)
    on = jnp.concatenate([on_ref[0, :, rr, :] for rr in range(on_ref.shape[2])], axis=0)
    y_b = (on.astype(F32) * (go * _sigmoid(go))).astype(BF16)
    pa = _dot(ya_ref[...], pa_ref[...])
    pb = _dot(y_b, pb_ref[...])
    merged = _sigmoid(ga_ref[...].astype(F32)) * pa + _sigmoid(gb_ref[...].astype(F32)) * pb
    x1 = x_ref[...] + g1_ref[0] * _dot(merged.astype(BF16), wo_ref[...])
    x1_ref[...] = x1
    y = x1 * lax.rsqrt(jnp.mean(x1 * x1, axis=-1, keepdims=True) + EPS) * n2_ref[...]
    h2 = y * (1.0 + sc_ref[0]) + sh_ref[0]
    _to_token_rows(h2_ref, h2)
    _route(h2.astype(BF16), wr_ref, br_ref, cnt_ref, ls8_ref, w8_ref, seg_ref)


def _merge(x2d, ya, on, go, ga, gb, g1, sc2, sh2, n2g, pa, pb, wo, wr_t, br, rows_per_batch, tm):
    n, d = x2d.shape
    per = rows_per_batch // tm
    row = lambda wd: pl.BlockSpec((tm, wd), lambda i: (i, 0))
    mod = pl.BlockSpec((1, 1, d), lambda i: (i // per, 0, 0))
    full = lambda a: pl.BlockSpec(a.shape, lambda i: (0, 0))
    tok = pl.BlockSpec((TOP_K, tm), lambda i: (0, i))
    return pl.pallas_call(
        _merge_kernel,
        out_shape=[jax.ShapeDtypeStruct((n, d), F32), jax.ShapeDtypeStruct((n * ROW_SUB, LANES), U32),
                   jax.ShapeDtypeStruct((TOP_K, n), jnp.int32), jax.ShapeDtypeStruct((TOP_K, n), F32),
                   jax.ShapeDtypeStruct((n // tm, N_EXPERTS, LANES), F32),
                   jax.ShapeDtypeStruct((N_EXPERTS, 1), F32)],
        grid=(n // tm,),
        in_specs=[row(d), row(S5_WIDTH),
                  pl.BlockSpec((1, GRID_W, tm // GRID_W, HG_WIDTH), lambda i: (i // per, 0, i % per, 0)),
                  row(HG_WIDTH), row(d), row(d), mod, mod, mod,
                  pl.BlockSpec((1, d), lambda i: (0, 0)), full(pa), full(pb), full(wo), full(wr_t), full(br)],
        out_specs=[row(d), pl.BlockSpec((tm * ROW_SUB, LANES), lambda i: (i, 0)), tok, tok,
                   pl.BlockSpec((1, N_EXPERTS, LANES), lambda i: (i, 0, 0)),
                   pl.BlockSpec((N_EXPERTS, 1), lambda i: (0, 0))],
        compiler_params=_params("arbitrary"),
        name="merge_out_proj_route",
    )(x2d, ya, on, go, ga, gb, g1, sc2, sh2, n2g.reshape(1, d), pa, pb, wo, wr_t, br)


MOE_TILE = TOK_TILE
MOE_BLK = 512


def _token_row(ref, r):
    return ref.at[pl.ds(pl.multiple_of(r * ROW_SUB, ROW_SUB), ROW_SUB)]


def _wait_rows(any_ref, sem, n_rows):
    view = any_ref.at[pl.ds(0, n_rows * ROW_SUB)]
    pltpu.make_async_copy(view, view, sem).wait()


def _rows(ref, r0, n):
    return ref.at[pl.ds(pl.multiple_of(r0 * ROW_SUB, ROW_SUB), n * ROW_SUB)]


def _pow2_pieces(n, max_piece, fn):
    done = 0
    piece = max_piece
    while piece >= 1:
        hit = (n & piece) != 0
        pl.when(hit)(functools.partial(fn, done, piece))
        done = done + (n & piece)
        piece //= 2


def _copy_rows(src_ref, src0, dst_ref, dst0, n, max_piece, sem):
    def piece(off, size):
        pltpu.make_async_copy(_rows(src_ref, src0 + off, size), _rows(dst_ref, dst0 + off, size), sem).start()
    _pow2_pieces(n, max_piece, piece)


def _wait_copied_rows(src_ref, dst_ref, n, max_piece, sem):
    def piece(off, size):
        pltpu.make_async_copy(_rows(src_ref, 0, size), _rows(dst_ref, 0, size), sem).wait()
    _pow2_pieces(n, max_piece, piece)


def _dispatch_kernel(gs_ref, cnt_ref, off_ref, pad_ref, ls_ref, h2_ref, xs_hbm, ls_smem, stage0, stage1, zbuf,
                     sem0, sem1, lsem, zsem, *, tm, n_blocks):
    i = pl.program_id(0)
    last = pl.num_programs(0) - 1
    cp = pltpu.make_async_copy(ls_ref, ls_smem, lsem)
    cp.start()
    cp.wait()

    def tile(stage, sem, prev_sem):
        def body(t, carry):
            row = h2_ref[pl.ds(pl.multiple_of(t * ROW_SUB, ROW_SUB), ROW_SUB), :]
            for k in range(TOP_K):
                stage[pl.ds(pl.multiple_of(ls_smem[k, t] * ROW_SUB, ROW_SUB), ROW_SUB), :] = row
            return carry

        lax.fori_loop(0, tm, body, 0)

        def per_expert(e, carry):
            _copy_rows(stage, off_ref[i, e], xs_hbm, gs_ref[i, e], cnt_ref[i, e], tm, sem)
            return carry

        lax.fori_loop(0, N_EXPERTS, per_expert, 0)

        @pl.when(i > 0)
        def _():
            _wait_rows(xs_hbm, prev_sem, tm * TOP_K)

        @pl.when(i == last)
        def _():
            _wait_rows(xs_hbm, sem, tm * TOP_K)

    pl.when(i % 2 == 0)(functools.partial(tile, stage0, sem0, sem1))
    pl.when(i % 2 == 1)(functools.partial(tile, stage1, sem1, sem0))

    @pl.when(i == 0)
    def _():
        zbuf[...] = jnp.zeros_like(zbuf)

        def start(e, carry):
            _copy_rows(zbuf, 0, xs_hbm, pad_ref[0, e], pad_ref[1, e], MOE_BLK // 2, zsem)
            return carry

        def wait(e, carry):
            _wait_copied_rows(zbuf, xs_hbm, pad_ref[1, e], MOE_BLK // 2, zsem)
            return carry

        lax.fori_loop(0, N_EXPERTS, start, 0)
        lax.fori_loop(0, N_EXPERTS, wait, 0)

        def zero_block(j, carry):
            pltpu.make_async_copy(zbuf, _rows(xs_hbm, j * MOE_BLK, MOE_BLK), zsem).start()
            return carry

        def wait_block(j, carry):
            pltpu.make_async_copy(zbuf, _rows(xs_hbm, 0, MOE_BLK), zsem).wait()
            return carry

        lax.fori_loop(pad_ref[2, 0], n_blocks, zero_block, 0)
        lax.fori_loop(pad_ref[2, 0], n_blocks, wait_block, 0)


def _dispatch(gstart, seg_cnt, seg_off, pad, ls8, h2_rows, n_blocks, tm):
    n = ls8.shape[1]
    cap = n_blocks * MOE_BLK
    return pl.pallas_call(
        functools.partial(_dispatch_kernel, tm=tm, n_blocks=n_blocks),
        out_shape=jax.ShapeDtypeStruct((cap * ROW_SUB, LANES), U32),
        grid_spec=pltpu.PrefetchScalarGridSpec(
            num_scalar_prefetch=4,
            grid=(n // tm,),
            in_specs=[pl.BlockSpec((TOP_K, tm), lambda i, *_: (0, i)),
                      pl.BlockSpec((tm * ROW_SUB, LANES), lambda i, *_: (i, 0))],
            out_specs=pl.BlockSpec(memory_space=pl.ANY),
            scratch_shapes=[pltpu.SMEM((TOP_K, tm), jnp.int32),
                            pltpu.VMEM((TOP_K * tm * ROW_SUB, LANES), U32),
                            pltpu.VMEM((TOP_K * tm * ROW_SUB, LANES), U32),
                            pltpu.VMEM((MOE_BLK * ROW_SUB, LANES), U32),
                            pltpu.SemaphoreType.DMA, pltpu.SemaphoreType.DMA, pltpu.SemaphoreType.DMA,
                            pltpu.SemaphoreType.DMA]),
        compiler_params=pltpu.CompilerParams(dimension_semantics=("arbitrary",), vmem_limit_bytes=VMEM_LIMIT,
                                             has_side_effects=True),
        name="moe_dispatch",
    )(gstart, seg_cnt, seg_off, pad, ls8, h2_rows)


def _expert_kernel(be_ref, nu_ref, x_ref, w1_ref, w3_ref, w2_ref, o_ref, w1b, w3b, w2b):
    j = pl.program_id(0)
    e = be_ref[j]
    prev = be_ref[jnp.maximum(j - 1, 0)]
    used = j < nu_ref[0]

    @pl.when(jnp.logical_and(used, jnp.logical_or(j == 0, e != prev)))
    def _():
        w1b[...] = w1_ref[0].astype(BF16)
        w3b[...] = w3_ref[0].astype(BF16)
        w2b[...] = w2_ref[0].astype(BF16)

    @pl.when(used)
    def _():
        x = _from_token_rows(x_ref, MOE_BLK).astype(BF16)
        a = _dot(x, w1b[...])
        hid = (a * _sigmoid(a)) * _dot(x, w3b[...])
        _to_token_rows(o_ref, _dot(hid.astype(BF16), w2b[...]))

    @pl.when(jnp.logical_not(used))
    def _():
        o_ref[...] = jnp.zeros_like(o_ref)


def _experts(block_e, n_used, xs, w1, w3, w2):
    n_blocks = xs.shape[0] // (MOE_BLK * ROW_SUB)
    d, f = w1.shape[1], w1.shape[2]
    rows = pl.BlockSpec((MOE_BLK * ROW_SUB, LANES), lambda j, be, nu: (j, 0))
    return pl.pallas_call(
        _expert_kernel,
        out_shape=jax.ShapeDtypeStruct(xs.shape, U32),
        grid_spec=pltpu.PrefetchScalarGridSpec(
            num_scalar_prefetch=2,
            grid=(n_blocks,),
            in_specs=[rows,
                      pl.BlockSpec((1, d, f), lambda j, be, nu: (be[j], 0, 0)),
                      pl.BlockSpec((1, d, f), lambda j, be, nu: (be[j], 0, 0)),
                      pl.BlockSpec((1, f, d), lambda j, be, nu: (be[j], 0, 0))],
            out_specs=rows,
            scratch_shapes=[pltpu.VMEM((d, f), BF16), pltpu.VMEM((d, f), BF16), pltpu.VMEM((f, d), BF16)]),
        compiler_params=_params("arbitrary"),
        name="moe_experts",
    )(block_e, n_used, xs, w1, w3, w2)


def _combine_kernel(gs_ref, cnt_ref, off_ref, ls_ref, w8_ref, x1_ref, h2_ref, g2_ref, ws1_ref, ws3_ref, ws2_ref,
                    fg_ref, ys_hbm, o_ref, ls_smem, w_smem, gbuf0, gbuf1, acc_rows, sem0, sem1, lsem, *, tm):
    i = pl.program_id(0)
    last = pl.num_programs(0) - 1
    cp1 = pltpu.make_async_copy(ls_ref, ls_smem, lsem)
    cp2 = pltpu.make_async_copy(w8_ref, w_smem, lsem)
    cp1.start()
    cp2.start()

    def fetch(tile, gbuf, sem):
        def per_expert(e, carry):
            _copy_rows(ys_hbm, gs_ref[tile, e], gbuf, off_ref[tile, e], cnt_ref[tile, e], tm, sem)
            return carry

        lax.fori_loop(0, N_EXPERTS, per_expert, 0)

    @pl.when(i == 0)
    def _():
        fetch(0, gbuf0, sem0)

    @pl.when(jnp.logical_and(i < last, i % 2 == 0))
    def _():
        fetch(i + 1, gbuf1, sem1)

    @pl.when(jnp.logical_and(i < last, i % 2 == 1))
    def _():
        fetch(i + 1, gbuf0, sem0)

    h2 = _from_token_rows(h2_ref, tm).astype(BF16)
    a = _dot(h2, ws1_ref[...])
    hid = (a * _sigmoid(a)) * _dot(h2, ws3_ref[...])
    acc = _dot(hid.astype(BF16), ws2_ref[...])
    cp1.wait()
    cp2.wait()

    def reduce_rows(gbuf, sem):
        _wait_rows(gbuf, sem, tm * TOP_K)

        def body(t, carry):
            lo = jnp.zeros((ROW_SUB, LANES), F32)
            hi = jnp.zeros((ROW_SUB, LANES), F32)
            for k in range(TOP_K):
                w = w_smem[k, t]
                words = gbuf[pl.ds(pl.multiple_of(ls_smem[k, t] * ROW_SUB, ROW_SUB), ROW_SUB), :]
                lo = lo + w * lax.bitcast_convert_type(words << 16, F32)
                hi = hi + w * lax.bitcast_convert_type(words & jnp.uint32(0xFFFF0000), F32)
            acc_rows[pl.ds(pl.multiple_of(t * SUBLANES, SUBLANES), ROW_SUB), :] = lo
            acc_rows[pl.ds(pl.multiple_of(t * SUBLANES, SUBLANES) + ROW_SUB, ROW_SUB), :] = hi
            return carry

        lax.fori_loop(0, tm, body, 0)

    pl.when(i % 2 == 0)(functools.partial(reduce_rows, gbuf0, sem0))
    pl.when(i % 2 == 1)(functools.partial(reduce_rows, gbuf1, sem1))
    routed = jnp.concatenate([acc_rows[pl.ds(s, tm, stride=SUBLANES), :] for s in range(SUBLANES)], axis=-1)
    y = x1_ref[...] + g2_ref[0] * (acc + routed)
    o_ref[...] = y * lax.rsqrt(jnp.mean(y * y, axis=-1, keepdims=True) + EPS) * fg_ref[...]


def _combine(gstart, seg_cnt, seg_off, ls8, w8, x1, h2_rows, g2, ws1, ws3, ws2, fg, ys, rows_per_batch, tm):
    n, d = x1.shape
    per = rows_per_batch // tm
    tok = pl.BlockSpec((TOP_K, tm), lambda i, *_: (0, i))
    full = lambda a: pl.BlockSpec(a.shape, lambda i, *_: (0, 0))
    return pl.pallas_call(
        functools.partial(_combine_kernel, tm=tm),
        out_shape=jax.ShapeDtypeStruct((n, d), F32),
        grid_spec=pltpu.PrefetchScalarGridSpec(
            num_scalar_prefetch=3,
            grid=(n // tm,),
            in_specs=[tok, tok, pl.BlockSpec((tm, d), lambda i, *_: (i, 0)),
                      pl.BlockSpec((tm * ROW_SUB, LANES), lambda i, *_: (i, 0)),
                      pl.BlockSpec((1, 1, d), lambda i, *_: (i // per, 0, 0)),
                      full(ws1), full(ws3), full(ws2), pl.BlockSpec((1, d), lambda i, *_: (0, 0)),
                      pl.BlockSpec(memory_space=pl.ANY)],
            out_specs=pl.BlockSpec((tm, d), lambda i, *_: (i, 0)),
            scratch_shapes=[pltpu.SMEM((TOP_K, tm), jnp.int32), pltpu.SMEM((TOP_K, tm), F32),
                            pltpu.VMEM((TOP_K * tm * ROW_SUB, LANES), U32),
                            pltpu.VMEM((TOP_K * tm * ROW_SUB, LANES), U32),
                            pltpu.VMEM((tm * SUBLANES, LANES), F32), pltpu.SemaphoreType.DMA,
                            pltpu.SemaphoreType.DMA, pltpu.SemaphoreType.DMA]),
        compiler_params=_params("arbitrary"),
        name="moe_combine_final",
    )(gstart, seg_cnt, seg_off, ls8, w8, x1, h2_rows, g2, ws1, ws3, ws2, fg.reshape(1, d), ys)


def _moe_plan(seg, counts, n_assign):
    cnt = counts.reshape(N_EXPERTS).astype(jnp.int32)
    padded = (cnt + MOE_BLK - 1) // MOE_BLK * MOE_BLK
    pends = jnp.cumsum(padded)
    pstarts = pends - padded
    n_blocks = (n_assign + N_EXPERTS * (MOE_BLK - 1) + MOE_BLK - 1) // MOE_BLK
    seg = seg[:, :, :3].astype(jnp.int32)
    gstart = pstarts[None, :] + seg[:, :, 0]
    blk_start = jnp.arange(n_blocks, dtype=jnp.int32) * MOE_BLK
    block_e = jnp.minimum(jnp.sum((blk_start[:, None] >= pends[None, :]).astype(jnp.int32), axis=1),
                          N_EXPERTS - 1).astype(jnp.int32)
    n_used = (pends[-1:] // MOE_BLK).astype(jnp.int32)
    pad = jnp.stack([pstarts + cnt, padded - cnt, jnp.broadcast_to(n_used, (N_EXPERTS,))], axis=0).astype(jnp.int32)
    return gstart, seg[:, :, 1], seg[:, :, 2], pad, block_e, n_used, n_blocks


def _mixer(x, c, ctx, c_ctx, w_ada, b_ada, norm1_g, norm2_g, w_in, s5_lam_re, s5_lam_im, s5_log_dt,
           s5_b_re, s5_b_im, s5_c_re, s5_c_im, s5_d, s5_w_glu, lb, hg_norm_g, p_a, p_b, w_out,
           moe_w_router, moe_b_router):
    b, l, d = x.shape
    lc = ctx.shape[1]
    n = b * l
    rows = l // GRID_W

    c8 = jnp.concatenate([c, c_ctx[None], jnp.zeros((8 - b - 1, d), F32)], axis=0)
    mod = _ada(c8, w_ada, b_ada)
    sh1, sc1, g1, sh2, sc2, g2 = [mod[:b, k * d:(k + 1) * d].reshape(b, 1, d) for k in range(6)]
    csh1, csc1 = mod[b:b + 1, 0:d].reshape(1, 1, d), mod[b:b + 1, d:2 * d].reshape(1, 1, d)

    w_in_b = w_in.astype(BF16)
    z = dict(zip([p[0] for p in _IN_PIECES],
                 _inproj(x.reshape(n, d), sc1, sh1, norm1_g, w_in_b, l, TOK_TILE, True)))
    zc = dict(zip([p[0] for p in _IN_PIECES],
                  _inproj(ctx.reshape(b * lc, d), csc1, csh1, norm1_g, w_in_b, lc, lc, False)))

    cx = lambda t: t.reshape(b, lc, HG_WIDTH)
    lb_row = lb.reshape(1, HG_WIDTH)
    o_f = _hgrn_pass(z["q"], z["ff"], z["i"], cx(zc["ff"]), cx(zc["i"]), lb_row, None, None, reverse=False)
    o_n = _hgrn_pass(z["q"], z["fb"], z["i"], cx(zc["fb"]), cx(zc["i"]), lb_row, o_f,
                     hg_norm_g.reshape(1, HG_DK), reverse=True)

    d_lag, w_s5_in, w_out_f, w_out_b, decay = _s5_weights(s5_lam_re, s5_lam_im, s5_log_dt, s5_b_re, s5_b_im,
                                                          s5_c_re, s5_c_im)
    kc, kl = lc // S5_T, l // S5_T
    u_lat = z["u"].reshape(b, kl, S5_T * S5_WIDTH)
    u_ctx = zc["u"].reshape(b, kc, S5_T * S5_WIDTH)
    rows_in = kl + kc
    u_ext = jnp.concatenate([u_lat, u_ctx], axis=1).reshape(b * rows_in, S5_T * S5_WIDTH)
    e = _s5_in(u_ext, d_lag, w_s5_in, (b * rows_in) // 2)
    states = _s5_scan(e, decay, b, rows_in, kl)
    d_row = s5_d.astype(F32).reshape(1, S5_WIDTH)
    y_a = _s5_out(states, w_out_f, w_out_b, e.reshape(b, rows_in, -1), u_lat.reshape(b * kl, -1), d_row,
                  s5_w_glu.astype(BF16))
    y_a = y_a.reshape(n, S5_WIDTH)

    return _merge(x.reshape(n, d), y_a, o_n, z["go"], z["ga"], z["gb"], g1, sc2, sh2, norm2_g,
                  p_a.astype(BF16), p_b.astype(BF16), w_out.astype(BF16),
                  moe_w_router.T.astype(BF16), moe_b_router.astype(F32).reshape(N_EXPERTS, 1), l, MOE_TILE) + (g2,)


def kernel(x, c, ctx, c_ctx, w_ada, b_ada, norm1_g, norm2_g, w_in, s5_lam_re, s5_lam_im, s5_log_dt, s5_b_re,
           s5_b_im, s5_c_re, s5_c_im, s5_d, s5_w_glu, hg_lb_logits, hg_norm_g, p_a, p_b, w_out, moe_w_router,
           moe_b_router, moe_w1, moe_w3, moe_w2, moe_ws1, moe_ws3, moe_ws2, final_norm_g):
    b, l, d = x.shape
    n = b * l
    assert w_ada.shape[0] == 1, "single-layer block"
    lb = jnp.cumsum(jax.nn.softmax(hg_lb_logits.astype(F32), axis=0), axis=0)[0]
    x1, h2_rows, ls8, w8, seg, counts, g2 = _mixer(
        x, c, ctx, c_ctx, w_ada[0], b_ada[0], norm1_g[0], norm2_g[0], w_in[0], s5_lam_re[0], s5_lam_im[0],
        s5_log_dt[0], s5_b_re[0], s5_b_im[0], s5_c_re[0], s5_c_im[0], s5_d[0], s5_w_glu[0], lb, hg_norm_g[0],
        p_a[0], p_b[0], w_out[0], moe_w_router[0], moe_b_router[0])
    gstart, seg_cnt, seg_off, pad, block_e, n_used, n_blocks = _moe_plan(seg, counts, n * TOP_K)
    xs = _dispatch(gstart, seg_cnt, seg_off, pad, ls8, h2_rows, n_blocks, MOE_TILE)
    ys = _experts(block_e, n_used, xs, moe_w1[0], moe_w3[0], moe_w2[0])
    out = _combine(gstart, seg_cnt, seg_off, ls8, w8, x1, h2_rows, g2, moe_ws1[0].astype(BF16),
                   moe_ws3[0].astype(BF16), moe_ws2[0].astype(BF16), final_norm_g, ys, l, MOE_TILE)
    return out.reshape(b, l, d)
```

```python
import functools
import math

import jax
import jax.numpy as jnp
from jax import lax
from jax.experimental import pallas as pl
from jax.experimental.pallas import tpu as pltpu

F32 = jnp.float32
BF16 = jnp.bfloat16

GRID_W = 64
S5_WIDTH = 256
S5_GROUP = 16
S5_GROUPS = 16
S5_STATE = 64
HG_HEADS = 6
HG_DK = 128
HG_WIDTH = HG_HEADS * HG_DK
N_EXPERTS = 64
ROUTE_GROUPS = 8
TOPK_GROUPS = 4
TOP_K = 8
ROUTED_SCALE = 2.5
EPS = 1e-6

LANES = 128
SUBLANES = 8

TOK_TILE = 512
S5_T = 16
HG_CHUNK = 64
VMEM_LIMIT = 56 * 1024 * 1024

_NT = (((1,), (1,)), ((), ()))
_TN = (((0,), (0,)), ((), ()))


def _params(*sem):
    return pltpu.CompilerParams(dimension_semantics=sem, vmem_limit_bytes=VMEM_LIMIT)


def _dot(a, b):
    return jnp.dot(a, b, preferred_element_type=F32)


def _sigmoid(x):
    return 1.0 / (1.0 + jnp.exp(-x))


def _ada_kernel(c_ref, w_ref, b_ref, o_ref):
    c = c_ref[...]
    s = (c * _sigmoid(c)).astype(BF16)
    o_ref[...] = _dot(s, w_ref[...].astype(BF16)) + b_ref[...]


def _ada(c8, w_ada, b_ada):
    d, n = w_ada.shape
    tn = 1536
    return pl.pallas_call(
        _ada_kernel,
        out_shape=jax.ShapeDtypeStruct((8, n), F32),
        grid=(n // tn,),
        in_specs=[pl.BlockSpec((8, d), lambda j: (0, 0)),
                  pl.BlockSpec((d, tn), lambda j: (0, j)),
                  pl.BlockSpec((1, tn), lambda j: (0, j))],
        out_specs=pl.BlockSpec((8, tn), lambda j: (0, j)),
        compiler_params=_params("arbitrary"),
        name="ada_mod",
    )(c8, w_ada, b_ada.reshape(1, n))


_IN_PIECES = (("u", 0, 256, BF16), ("q", 256, 768, BF16), ("ff", 1024, 768, F32),
              ("fb", 1792, 768, F32), ("i", 2560, 768, BF16), ("go", 3328, 768, BF16),
              ("ga", 4096, 1024, BF16), ("gb", 5120, 1024, BF16))


def _inproj_kernel(x_ref, sc_ref, sh_ref, g_ref, w_ref, *o_refs):
    x = x_ref[...]
    y = x * lax.rsqrt(jnp.mean(x * x, axis=-1, keepdims=True) + EPS) * g_ref[...]
    h = (y * (1.0 + sc_ref[0]) + sh_ref[0]).astype(BF16)
    for (_, a, wd, _), o_ref in zip(_IN_PIECES, o_refs):
        res = _dot(h, w_ref[:, a:a + wd]).astype(o_ref.dtype)
        if len(o_ref.shape) == 2:
            o_ref[...] = res
        else:
            for rr in range(o_ref.shape[2]):
                o_ref[0, :, rr, :] = res[rr * GRID_W:(rr + 1) * GRID_W, :]


_COLMAJOR_PIECES = ("q", "ff", "fb", "i")


def _inproj(x2d, sc, sh, g, w_bf16, rows_per_mod, tm, colmajor):
    n, d = x2d.shape
    per = rows_per_mod // tm
    mod_map = (lambda i: (i // per, 0, 0)) if sc.shape[0] > 1 else (lambda i: (0, 0, 0))
    shapes, specs = [], []
    for name, _, wd, dt in _IN_PIECES:
        if colmajor and name in _COLMAJOR_PIECES:
            shapes.append(jax.ShapeDtypeStruct((n // rows_per_mod, GRID_W, rows_per_mod // GRID_W, wd), dt))
            specs.append(pl.BlockSpec((1, GRID_W, tm // GRID_W, wd), lambda i: (i // per, 0, i % per, 0)))
        else:
            shapes.append(jax.ShapeDtypeStruct((n, wd), dt))
            specs.append(pl.BlockSpec((tm, wd), lambda i: (i, 0)))
    return pl.pallas_call(
        _inproj_kernel,
        out_shape=shapes,
        grid=(n // tm,),
        in_specs=[pl.BlockSpec((tm, d), lambda i: (i, 0)),
                  pl.BlockSpec((1, 1, d), mod_map),
                  pl.BlockSpec((1, 1, d), mod_map),
                  pl.BlockSpec((1, d), lambda i: (0, 0)),
                  pl.BlockSpec(w_bf16.shape, lambda i: (0, 0))],
        out_specs=specs,
        compiler_params=_params("arbitrary"),
        name="in_proj",
    )(x2d, sc, sh, g.reshape(1, d), w_bf16)


def _hgrn_gates(zf, lb):
    sig = _sigmoid(zf)
    logf = jnp.log(lb + (1.0 - lb) * sig)
    k = (1.0 - lb) * (1.0 - sig)
    return logf, k


def _chunk_cumsum(cs, logf):
    hi = logf.astype(BF16)
    lo = (logf - hi.astype(F32)).astype(BF16)
    return _dot(cs, hi) + _dot(cs, lo)


def _hgrn_state_step(zf, v, lb, st, cs, reverse):
    logf, k = _hgrn_gates(zf, lb)
    cum = _chunk_cumsum(cs, logf)
    t = 0 if reverse else HG_CHUNK - 1
    total = cum[t:t + 1, :]
    kdec = (k * jnp.exp(total - cum)).astype(BF16)
    st_new = st * jnp.exp(total) + lax.dot_general(v.astype(BF16), kdec, _TN, preferred_element_type=F32)
    return cum, k, st_new


def _hgrn_kernel(*refs, reverse, final, n_ctx_chunks):
    if final:
        q_ref, f_ref, v_ref, cf_ref, cv_ref, lb_ref, of_ref, g_ref, o_ref, st_ref = refs
        of_ref = of_ref.at[0]
    else:
        q_ref, f_ref, v_ref, cf_ref, cv_ref, lb_ref, o_ref, st_ref = refs
    q_ref, f_ref, v_ref, o_ref = q_ref.at[0], f_ref.at[0], v_ref.at[0], o_ref.at[0]
    c_len = HG_CHUNK
    n_rows = q_ref.shape[1]
    n_chunks = n_rows // c_len
    row = lax.broadcasted_iota(jnp.int32, (n_rows, n_rows), 0)
    col = lax.broadcasted_iota(jnp.int32, (n_rows, n_rows), 1)
    tri = (col >= row) if reverse else (col <= row)
    same_chunk = None
    for c in range(n_chunks):
        lo, hi = c * c_len, (c + 1) * c_len
        blk = (row >= lo) & (row < hi) & (col >= lo) & (col < hi)
        same_chunk = blk if same_chunk is None else (same_chunk | blk)
    mask = tri & same_chunk
    cs = jnp.where(mask, 1.0, 0.0).astype(BF16)

    @pl.when(pl.program_id(1) == 0)
    def _():
        cs1 = cs[:c_len, :c_len]
        order = range(n_ctx_chunks - 1, -1, -1) if reverse else range(n_ctx_chunks)
        for h in range(HG_HEADS):
            cols = slice(h * HG_DK, (h + 1) * HG_DK)
            st = jnp.zeros((HG_DK, HG_DK), F32)
            for c in order:
                rows = slice(c * c_len, (c + 1) * c_len)
                _, _, st = _hgrn_state_step(cf_ref[0, rows, cols], cv_ref[0, rows, cols].astype(F32),
                                            lb_ref[:, cols], st, cs1, reverse)
            st_ref[h] = st

    def per_chunk_rows(x, r):
        return jnp.concatenate([jnp.broadcast_to(x[c * c_len + r:c * c_len + r + 1, :], (c_len, x.shape[1]))
                                for c in range(n_chunks)], axis=0)

    lb = lb_ref[...]
    q = q_ref[0].astype(F32)
    v = v_ref[0]
    logf, k = _hgrn_gates(f_ref[0], lb)
    cum = _chunk_cumsum(cs, logf)
    r_ref = c_len // 2 - 1 if reverse else c_len // 2
    r_tot = 0 if reverse else c_len - 1
    ref = per_chunk_rows(cum, r_ref)
    qe = q * jnp.exp(cum - ref)
    ke = k * jnp.exp(ref - cum)
    qi, ki = qe.astype(BF16), ke.astype(BF16)
    q_in = (qe * jnp.exp(ref)).astype(BF16)
    tail = jnp.exp(per_chunk_rows(cum, r_tot) - ref)
    kdec = (ke * tail).astype(BF16)

    order = range(n_chunks - 1, -1, -1) if reverse else range(n_chunks)
    for h in range(HG_HEADS):
        cols = slice(h * HG_DK, (h + 1) * HG_DK)
        s = lax.dot_general(qi[:, cols], ki[:, cols], _NT, preferred_element_type=F32)
        o_intra = _dot(jnp.where(mask, s, 0.0).astype(BF16), v[:, cols])
        st = st_ref[h]
        for c in order:
            rows = slice(c * c_len, (c + 1) * c_len)
            o = o_intra[rows] + lax.dot_general(q_in[rows, cols], st.astype(BF16), _NT, preferred_element_type=F32)
            total = cum[c * c_len + r_tot:c * c_len + r_tot + 1, cols]
            st = st * jnp.exp(total) + lax.dot_general(v[rows, cols], kdec[rows, cols], _TN,
                                                       preferred_element_type=F32)
            if final:
                o = o + of_ref[0, rows, cols]
                o = o * lax.rsqrt(jnp.mean(o * o, axis=-1, keepdims=True) + EPS) * g_ref[...]
            o_ref[0, rows, cols] = o.astype(o_ref.dtype)
        st_ref[h] = st


def _hgrn_pass(q, f, v, cf, cv, lb, o_prev, g, *, reverse):
    b, nw, rows, _ = q.shape
    final = o_prev is not None
    wmap = (lambda bi, w: (bi, nw - 1 - w, 0, 0)) if reverse else (lambda bi, w: (bi, w, 0, 0))
    blk = pl.BlockSpec((1, 1, rows, HG_WIDTH), wmap)
    cblk = pl.BlockSpec((1, cf.shape[1], HG_WIDTH), lambda bi, w: (bi, 0, 0))
    in_specs = [blk, blk, blk, cblk, cblk, pl.BlockSpec((1, HG_WIDTH), lambda bi, w: (0, 0))]
    args = [q, f, v, cf, cv, lb]
    if final:
        in_specs += [blk, pl.BlockSpec((1, HG_DK), lambda bi, w: (0, 0))]
        args += [o_prev, g]
    return pl.pallas_call(
        functools.partial(_hgrn_kernel, reverse=reverse, final=final, n_ctx_chunks=cf.shape[1] // HG_CHUNK),
        out_shape=jax.ShapeDtypeStruct(q.shape, BF16 if final else F32),
        grid=(b, nw),
        in_specs=in_specs,
        out_specs=blk,
        scratch_shapes=[pltpu.VMEM((HG_HEADS, HG_DK, HG_DK), F32)],
        compiler_params=_params("arbitrary", "arbitrary"),
        name="hgrn_bwd" if reverse else "hgrn_fwd",
    )(*args)


def _s5_weights(lam_re, lam_im, log_dt, b_re, b_im, c_re, c_im):
    hp = lax.Precision.HIGHEST
    g, p, cc, t = S5_GROUPS, S5_STATE, S5_GROUP, S5_T
    lre = jnp.minimum(lam_re.astype(F32), -1e-4)
    lim = lam_im.astype(F32)
    dt = jnp.exp(log_dt.astype(F32))[..., None]
    ks = jnp.arange(t + 1, dtype=F32)[:, None, None, None]
    mag = jnp.exp(ks * (lre * dt)[None])
    pw_re = mag * jnp.cos(ks * (lim * dt)[None])
    pw_im = mag * jnp.sin(ks * (lim * dt)[None])
    nr, ni = pw_re[1] - 1.0, pw_im[1]
    den = lre * lre + lim * lim
    cf_re = (nr * lre + ni * lim) / den
    cf_im = (ni * lre - nr * lim) / den
    bb_re = cf_re[..., None] * b_re - cf_im[..., None] * b_im
    bb_im = cf_re[..., None] * b_im + cf_im[..., None] * b_re
    cre, cim = c_re.astype(F32), c_im.astype(F32)
    sw, ns = S5_WIDTH, 2 * g * p
    grp_of_row = jnp.arange(sw)[:, None] // cc

    cp_re = cre[None, None, :, :, :] * pw_re[:t, :, :, None, :] - cim[None, None] * pw_im[:t, :, :, None, :]
    cp_im = cre[None, None, :, :, :] * pw_im[:t, :, :, None, :] + cim[None, None] * pw_re[:t, :, :, None, :]
    kk = (jnp.einsum("kdgop,dgpi->dkgoi", cp_re, bb_re, precision=hp)
          - jnp.einsum("kdgop,dgpi->dkgoi", cp_im, bb_im, precision=hp))
    kf, kb = kk[0], kk[1]
    kall = jnp.concatenate([kb[:0:-1], (kf[0] + kb[0])[None], kf[1:]], axis=0)
    kt = kall.transpose(0, 1, 3, 2).reshape(2 * t - 1, sw, cc)
    same = grp_of_row == (jnp.arange(sw)[None, :] // cc)
    d_lag = jnp.where(same[None], jnp.tile(kt, (1, 1, g)), 0.0).astype(BF16)

    same_in = grp_of_row == ((jnp.arange(ns)[None, :] % (g * p)) // p)

    def in_to_state(pre, pim, bre, bim):
        xre = pre[..., None] * bre[None] - pim[..., None] * bim[None]
        xim = pre[..., None] * bim[None] + pim[..., None] * bre[None]
        x = jnp.concatenate([jnp.tile(xre.transpose(0, 1, 3, 2).reshape(t, sw, p), (1, 1, g)),
                             jnp.tile(xim.transpose(0, 1, 3, 2).reshape(t, sw, p), (1, 1, g))], axis=-1)
        return jnp.where(same_in[None], x, 0.0).reshape(t * sw, ns)

    w_in = jnp.concatenate([in_to_state(pw_re[t - 1::-1, 0], pw_im[t - 1::-1, 0], bb_re[0], bb_im[0]),
                            in_to_state(pw_re[:t, 1], pw_im[:t, 1], bb_re[1], bb_im[1])], axis=1).astype(BF16)

    same_out = ((jnp.arange(ns)[:, None] % (g * p)) // p) == ((jnp.arange(t * sw)[None, :] // cc) % g)

    def state_to_out(pre, pim):
        are = cre[None] * pre[:, :, None, :] - cim[None] * pim[:, :, None, :]
        aim = cre[None] * pim[:, :, None, :] + cim[None] * pre[:, :, None, :]
        a = jnp.concatenate([are.transpose(1, 3, 0, 2), -aim.transpose(1, 3, 0, 2)], axis=0)
        a = jnp.broadcast_to(a.reshape(ns, t, 1, cc), (ns, t, g, cc)).reshape(ns, t * sw)
        return jnp.where(same_out, a, 0.0).astype(BF16)

    w_out_f = state_to_out(pw_re[1:, 0], pw_im[1:, 0])
    w_out_b = state_to_out(pw_re[t:0:-1, 1], pw_im[t:0:-1, 1])

    decay = jnp.stack([pw_re[t].reshape(2, g * p), pw_im[t].reshape(2, g * p)], axis=1)
    return d_lag, w_in, w_out_f, w_out_b, decay


def _s5_in_kernel(u_ref, d_ref, w_ref, o_ref):
    j = pl.program_id(0)

    @pl.when(j < S5_T)
    def _():
        acc = _dot(u_ref[:, 0:S5_WIDTH], d_ref[j + S5_T - 1])
        for s in range(1, S5_T):
            acc = acc + _dot(u_ref[:, s * S5_WIDTH:(s + 1) * S5_WIDTH], d_ref[j - s + S5_T - 1])
        o_ref[...] = acc

    @pl.when(j >= S5_T)
    def _():
        o_ref[...] = _dot(u_ref[...], w_ref[...])


def _s5_in(u, d_lag, w_in, tm):
    m, k = u.shape
    tn = S5_WIDTH
    nj = (k + w_in.shape[1]) // tn
    return pl.pallas_call(
        _s5_in_kernel,
        out_shape=jax.ShapeDtypeStruct((m, nj * tn), F32),
        grid=(nj, m // tm),
        in_specs=[pl.BlockSpec((tm, k), lambda j, i: (i, 0)),
                  pl.BlockSpec(d_lag.shape, lambda j, i: (0, 0, 0)),
                  pl.BlockSpec((k, tn), lambda j, i: (0, jnp.maximum(j - S5_T, 0)))],
        out_specs=pl.BlockSpec((tm, tn), lambda j, i: (i, j)),
        compiler_params=_params("arbitrary", "arbitrary"),
        name="s5_in",
    )(u, d_lag, w_in)


def _s5_scan_kernel(efr_ref, efi_ref, ebr_ref, ebi_ref, a_ref, hfr_ref, hfi_ref, hbr_ref, hbi_ref,
                    *, nb, rows_in, rows_out):
    dirs = ((efr_ref, efi_ref, hfr_ref, hfi_ref, a_ref[0, 0:1, :], a_ref[0, 1:2, :]),
            (ebr_ref, ebi_ref, hbr_ref, hbi_ref, a_ref[1, 0:1, :], a_ref[1, 1:2, :]))
    zero = jnp.zeros_like(dirs[0][4])

    def step(srcs, carry, store):
        new = []
        for di, (er_ref, ei_ref, hr_ref, hi_ref, are, aim) in enumerate(dirs):
            for bi in range(nb):
                hre, him = carry[2 * (di * nb + bi)], carry[2 * (di * nb + bi) + 1]
                if store:
                    hr_ref[pl.ds(bi * rows_out + srcs[di], 1), :] = hre
                    hi_ref[pl.ds(bi * rows_out + srcs[di], 1), :] = him
                ere = er_ref[pl.ds(bi * rows_in + srcs[di], 1), :]
                eim = ei_ref[pl.ds(bi * rows_in + srcs[di], 1), :]
                new += [are * hre - aim * him + ere, are * him + aim * hre + eim]
        return tuple(new)

    n_ctx = rows_in - rows_out
    carry = lax.fori_loop(0, n_ctx, lambda s, c: step((rows_out + s, rows_in - 1 - s), c, False),
                          tuple([zero] * (4 * nb)))
    lax.fori_loop(0, rows_out, lambda s, c: step((s, rows_out - 1 - s), c, True), carry)


def _s5_scan(e, decay, nb, rows_in, rows_out):
    tc = 256
    nsr = S5_GROUPS * S5_STATE
    c0 = (S5_T * S5_WIDTH) // tc
    nt = nsr // tc
    eblk = lambda k: pl.BlockSpec((nb * rows_in, tc), lambda j: (0, c0 + k * nt + j))
    hblk = pl.BlockSpec((nb * rows_out, tc), lambda j: (0, j))
    return pl.pallas_call(
        functools.partial(_s5_scan_kernel, nb=nb, rows_in=rows_in, rows_out=rows_out),
        out_shape=[jax.ShapeDtypeStruct((nb * rows_out, nsr), F32)] * 4,
        grid=(nt,),
        in_specs=[eblk(0), eblk(1), eblk(2), eblk(3), pl.BlockSpec((2, 2, tc), lambda j: (0, 0, j))],
        out_specs=[hblk] * 4,
        compiler_params=_params("arbitrary"),
        name="s5_scan",
    )(e, e, e, e, decay)


def _gelu_tanh(x):
    return 0.5 * x * (1.0 + jnp.tanh(math.sqrt(2.0 / math.pi) * (x + 0.044715 * x * x * x)))


def _s5_out_kernel(hfr_ref, hfi_ref, hbr_ref, hbi_ref, wf_ref, wb_ref, yi_ref, u_ref, d_ref, wg_ref, o_ref):
    nsr = hfr_ref.shape[1]
    y = yi_ref[0] + d_ref[...] * u_ref[...].astype(F32)
    for h_ref, w_ref, r0 in ((hfr_ref, wf_ref, 0), (hfi_ref, wf_ref, nsr), (hbr_ref, wb_ref, 0), (hbi_ref, wb_ref, nsr)):
        y = y + _dot(h_ref[...].astype(BF16), w_ref[r0:r0 + nsr, :])
    y = _gelu_tanh(y)
    gate = _sigmoid(_dot(y.astype(BF16), wg_ref[...]))
    o_ref[...] = (y * gate).astype(o_ref.dtype)


def _s5_out(states, w_out_f, w_out_b, e3, u_rows, d_row, w_glu):
    m, nsr = states[0].shape
    nb = e3.shape[0]
    tm = m // nb
    tn = S5_WIDTH
    st = pl.BlockSpec((tm, nsr), lambda i, j: (i, 0))
    wo = pl.BlockSpec((2 * nsr, tn), lambda i, j: (0, j))
    return pl.pallas_call(
        _s5_out_kernel,
        out_shape=jax.ShapeDtypeStruct((m, S5_T * S5_WIDTH), BF16),
        grid=(nb, S5_T),
        in_specs=[st, st, st, st, wo, wo,
                  pl.BlockSpec((1, tm, tn), lambda i, j: (i, 0, j)),
                  pl.BlockSpec((tm, tn), lambda i, j: (i, j)),
                  pl.BlockSpec((1, tn), lambda i, j: (0, 0)),
                  pl.BlockSpec((tn, tn), lambda i, j: (0, 0))],
        out_specs=pl.BlockSpec((tm, tn), lambda i, j: (i, j)),
        compiler_params=_params("arbitrary", "arbitrary"),
        name="s5_out",
    )(*states, w_out_f, w_out_b, e3, u_rows, d_row, w_glu)


U32 = jnp.uint32
ROW_SUB = 4


def _to_token_rows(ref, val):
    t, d = val.shape

    def rounded(x):
        u = lax.bitcast_convert_type(x, U32)
        return u + (jnp.uint32(0x7FFF) + ((u >> 16) & jnp.uint32(1)))

    w = (rounded(val[:, :d // 2]) >> 16) | (rounded(val[:, d // 2:]) & jnp.uint32(0xFFFF0000))
    for s in range(ROW_SUB):
        ref[pl.ds(s, t, stride=ROW_SUB), :] = w[:, s * LANES:(s + 1) * LANES]


def _from_token_rows(ref, t, row0=0):
    w = jnp.concatenate([ref[pl.ds(row0 * ROW_SUB + s, t, stride=ROW_SUB), :] for s in range(ROW_SUB)], axis=-1)
    lo = lax.bitcast_convert_type(w << 16, F32)
    hi = lax.bitcast_convert_type(w & jnp.uint32(0xFFFF0000), F32)
    return jnp.concatenate([lo, hi], axis=-1)


def _route(h2b, wr_ref, br_ref, cnt_ref, ls8_ref, w8_ref, seg_ref):
    tm = h2b.shape[0]
    per_group = N_EXPERTS // ROUTE_GROUPS
    scores = _sigmoid(lax.dot_general(wr_ref[...], h2b, _NT, preferred_element_type=F32))
    biased = scores + br_ref[...]
    neg = -jnp.inf
    sub = lax.broadcasted_iota(jnp.int32, (per_group, tm), 0)
    grp = []
    for gi in range(ROUTE_GROUPS):
        v = biased[gi * per_group:(gi + 1) * per_group, :]
        m1 = jnp.max(v, axis=0, keepdims=True)
        first = jnp.min(jnp.where(v == m1, sub, per_group), axis=0, keepdims=True)
        m2 = jnp.max(jnp.where(sub == first, neg, v), axis=0, keepdims=True)
        grp.append(m1 + m2)
    grp = jnp.concatenate(grp, axis=0)
    gid = lax.broadcasted_iota(jnp.int32, (ROUTE_GROUPS, tm), 0)
    beaten = jnp.zeros((ROUTE_GROUPS, tm), jnp.int32)
    for gj in range(ROUTE_GROUPS):
        r = grp[gj:gj + 1, :]
        beaten = beaten + jnp.where((r > grp) | ((r == grp) & (gj < gid)), 1, 0)
    group_ok = beaten < TOPK_GROUPS
    expert_ok = jnp.concatenate(
        [jnp.broadcast_to(group_ok[gi:gi + 1, :], (per_group, tm)) for gi in range(ROUTE_GROUPS)], axis=0)
    cur = jnp.where(expert_ok, biased, neg)
    eid = lax.broadcasted_iota(jnp.int32, (N_EXPERTS, tm), 0)
    sel = jnp.zeros((N_EXPERTS, tm), F32)
    picks, wts = [], []
    for _ in range(TOP_K):
        m = jnp.max(cur, axis=0, keepdims=True)
        idx = jnp.min(jnp.where(cur == m, eid, N_EXPERTS), axis=0, keepdims=True)
        hit = eid == idx
        picks.append(idx)
        wts.append(jnp.sum(jnp.where(hit, scores, 0.0), axis=0, keepdims=True))
        sel = jnp.where(hit, 1.0, sel)
        cur = jnp.where(hit, neg, cur)
    wsum = wts[0]
    for w in wts[1:]:
        wsum = wsum + w
    selb = sel.astype(BF16)
    ti = lax.broadcasted_iota(jnp.int32, (tm, tm), 0)
    tj = lax.broadcasted_iota(jnp.int32, (tm, tm), 1)
    rank = _dot(selb, jnp.where(ti < tj, 1.0, 0.0).astype(BF16))
    ei = lax.broadcasted_iota(jnp.int32, (N_EXPERTS, N_EXPERTS), 0)
    ej = lax.broadcasted_iota(jnp.int32, (N_EXPERTS, N_EXPERTS), 1)
    seg_off = jnp.sum(_dot(jnp.where(ej < ei, 1.0, 0.0).astype(BF16), selb), axis=1, keepdims=True)
    seg_cnt = jnp.sum(sel, axis=1, keepdims=True)
    slot = seg_off + rank
    for k in range(TOP_K):
        w8_ref[k:k + 1, :] = wts[k] / wsum * ROUTED_SCALE
        ls8_ref[k:k + 1, :] = jnp.sum(jnp.where(eid == picks[k], slot, 0.0), axis=0, keepdims=True).astype(jnp.int32)
    lane = lax.broadcasted_iota(jnp.int32, (N_EXPERTS, LANES), 1)
    seg_ref[0] = jnp.where(lane == 0, cnt_ref[...], jnp.where(lane == 1, seg_cnt, seg_off))
    cnt_ref[...] = cnt_ref[...] + seg_cnt


def _merge_kernel(x_ref, ya_ref, on_ref, go_ref, ga_ref, gb_ref, g1_ref, sc_ref, sh_ref, n2_ref,
                  pa_ref, pb_ref, wo_ref, wr_ref, br_ref,
                  x1_ref, h2_ref, ls8_ref, w8_ref, seg_ref, cnt_ref):
    @pl.when(pl.program_id(0) == 0)
    def _():
        cnt_ref[...] = jnp.zeros_like(cnt_ref)

    go = go_ref[...].astype(F32)
    on = jnp.concatenate([on_ref[0, :, rr, :] for rr in range(on_ref.shape[2])], axis=0)
    y_b = (on.astype(F32) * (go * _sigmoid(go))).astype(BF16)
    pa = _dot(ya_ref[...], pa_ref[...])
    pb = _dot(y_b, pb_ref[...])
    merged = _sigmoid(ga_ref[...].astype(F32)) * pa + _sigmoid(gb_ref[...].astype(F32)) * pb
    x1 = x_ref[...] + g1_ref[0] * _dot(merged.astype(BF16), wo_ref[...])
    x1_ref[...] = x1
    y = x1 * lax.rsqrt(jnp.mean(x1 * x1, axis=-1, keepdims=True) + EPS) * n2_ref[...]
    h2 = y * (1.0 + sc_ref[0]) + sh_ref[0]
    _to_token_rows(h2_ref, h2)
    _route(h2.astype(BF16), wr_ref, br_ref, cnt_ref, ls8_ref, w8_ref, seg_ref)


def _merge(x2d, ya, on, go, ga, gb, g1, sc2, sh2, n2g, pa, pb, wo, wr_t, br, rows_per_batch, tm):
    n, d = x2d.shape
    per = rows_per_batch // tm
    row = lambda wd: pl.BlockSpec((tm, wd), lambda i: (i, 0))
    mod = pl.BlockSpec((1, 1, d), lambda i: (i // per, 0, 0))
    full = lambda a: pl.BlockSpec(a.shape, lambda i: (0, 0))
    tok = pl.BlockSpec((TOP_K, tm), lambda i: (0, i))
    return pl.pallas_call(
        _merge_kernel,
        out_shape=[jax.ShapeDtypeStruct((n, d), F32), jax.ShapeDtypeStruct((n * ROW_SUB, LANES), U32),
                   jax.ShapeDtypeStruct((TOP_K, n), jnp.int32), jax.ShapeDtypeStruct((TOP_K, n), F32),
                   jax.ShapeDtypeStruct((n // tm, N_EXPERTS, LANES), F32),
                   jax.ShapeDtypeStruct((N_EXPERTS, 1), F32)],
        grid=(n // tm,),
        in_specs=[row(d), row(S5_WIDTH),
                  pl.BlockSpec((1, GRID_W, tm // GRID_W, HG_WIDTH), lambda i: (i // per, 0, i % per, 0)),
                  row(HG_WIDTH), row(d), row(d), mod, mod, mod,
                  pl.BlockSpec((1, d), lambda i: (0, 0)), full(pa), full(pb), full(wo), full(wr_t), full(br)],
        out_specs=[row(d), pl.BlockSpec((tm * ROW_SUB, LANES), lambda i: (i, 0)), tok, tok,
                   pl.BlockSpec((1, N_EXPERTS, LANES), lambda i: (i, 0, 0)),
                   pl.BlockSpec((N_EXPERTS, 1), lambda i: (0, 0))],
        compiler_params=_params("arbitrary"),
        name="merge_out_proj_route",
    )(x2d, ya, on, go, ga, gb, g1, sc2, sh2, n2g.reshape(1, d), pa, pb, wo, wr_t, br)


MOE_TILE = TOK_TILE
MOE_BLK = 512


def _token_row(ref, r):
    return ref.at[pl.ds(pl.multiple_of(r * ROW_SUB, ROW_SUB), ROW_SUB)]


def _wait_rows(any_ref, sem, n_rows):
    view = any_ref.at[pl.ds(0, n_rows * ROW_SUB)]
    pltpu.make_async_copy(view, view, sem).wait()


def _rows(ref, r0, n):
    return ref.at[pl.ds(pl.multiple_of(r0 * ROW_SUB, ROW_SUB), n * ROW_SUB)]


def _pow2_pieces(n, max_piece, fn):
    done = 0
    piece = max_piece
    while piece >= 1:
        hit = (n & piece) != 0
        pl.when(hit)(functools.partial(fn, done, piece))
        done = done + (n & piece)
        piece //= 2


def _copy_rows(src_ref, src0, dst_ref, dst0, n, max_piece, sem):
    def piece(off, size):
        pltpu.make_async_copy(_rows(src_ref, src0 + off, size), _rows(dst_ref, dst0 + off, size), sem).start()
    _pow2_pieces(n, max_piece, piece)


def _wait_copied_rows(src_ref, dst_ref, n, max_piece, sem):
    def piece(off, size):
        pltpu.make_async_copy(_rows(src_ref, 0, size), _rows(dst_ref, 0, size), sem).wait()
    _pow2_pieces(n, max_piece, piece)


def _dispatch_kernel(gs_ref, cnt_ref, off_ref, pad_ref, ls_ref, h2_ref, xs_hbm, ls_smem, stage0, stage1, zbuf,
                     sem0, sem1, lsem, zsem, *, tm, n_blocks):
    i = pl.program_id(0)
    last = pl.num_programs(0) - 1
    cp = pltpu.make_async_copy(ls_ref, ls_smem, lsem)
    cp.start()
    cp.wait()

    def tile(stage, sem, prev_sem):
        def body(t, carry):
            row = h2_ref[pl.ds(pl.multiple_of(t * ROW_SUB, ROW_SUB), ROW_SUB), :]
            for k in range(TOP_K):
                slot = ls_smem[t * TOP_K + k]
                stage[pl.ds(pl.multiple_of(slot * ROW_SUB, ROW_SUB), ROW_SUB), :] = row
            return carry

        lax.fori_loop(0, tm, body, 0)

        def per_expert(e, carry):
            _copy_rows(stage, off_ref[i, e], xs_hbm, gs_ref[i, e], cnt_ref[i, e], tm, sem)
            return carry

        lax.fori_loop(0, N_EXPERTS, per_expert, 0)

        @pl.when(i > 0)
        def _():
            _wait_rows(xs_hbm, prev_sem, tm * TOP_K)

        @pl.when(i == last)
        def _():
            _wait_rows(xs_hbm, sem, tm * TOP_K)

    pl.when(i % 2 == 0)(functools.partial(tile, stage0, sem0, sem1))
    pl.when(i % 2 == 1)(functools.partial(tile, stage1, sem1, sem0))

    @pl.when(i == 0)
    def _():
        zbuf[...] = jnp.zeros_like(zbuf)

        def start(e, carry):
            _copy_rows(zbuf, 0, xs_hbm, pad_ref[0, e], pad_ref[1, e], MOE_BLK // 2, zsem)
            return carry

        def wait(e, carry):
            _wait_copied_rows(zbuf, xs_hbm, pad_ref[1, e], MOE_BLK // 2, zsem)
            return carry

        lax.fori_loop(0, N_EXPERTS, start, 0)
        lax.fori_loop(0, N_EXPERTS, wait, 0)

        def zero_block(j, carry):
            pltpu.make_async_copy(zbuf, _rows(xs_hbm, j * MOE_BLK, MOE_BLK), zsem).start()
            return carry

        def wait_block(j, carry):
            pltpu.make_async_copy(zbuf, _rows(xs_hbm, 0, MOE_BLK), zsem).wait()
            return carry

        lax.fori_loop(pad_ref[2, 0], n_blocks, zero_block, 0)
        lax.fori_loop(pad_ref[2, 0], n_blocks, wait_block, 0)


def _dispatch(gstart, seg_cnt, seg_off, pad, ls8, h2_rows, n_blocks, tm):
    n = ls8.shape[0] // TOP_K
    cap = n_blocks * MOE_BLK
    return pl.pallas_call(
        functools.partial(_dispatch_kernel, tm=tm, n_blocks=n_blocks),
        out_shape=jax.ShapeDtypeStruct((cap * ROW_SUB, LANES), U32),
        grid_spec=pltpu.PrefetchScalarGridSpec(
            num_scalar_prefetch=4,
            grid=(n // tm,),
            in_specs=[pl.BlockSpec((tm * TOP_K,), lambda i, *_: (i,)),
                      pl.BlockSpec((tm * ROW_SUB, LANES), lambda i, *_: (i, 0))],
            out_specs=pl.BlockSpec(memory_space=pl.ANY),
            scratch_shapes=[pltpu.SMEM((tm * TOP_K,), jnp.int32),
                            pltpu.VMEM((TOP_K * tm * ROW_SUB, LANES), U32),
                            pltpu.VMEM((TOP_K * tm * ROW_SUB, LANES), U32),
                            pltpu.VMEM((MOE_BLK * ROW_SUB, LANES), U32),
                            pltpu.SemaphoreType.DMA, pltpu.SemaphoreType.DMA, pltpu.SemaphoreType.DMA,
                            pltpu.SemaphoreType.DMA]),
        compiler_params=pltpu.CompilerParams(dimension_semantics=("arbitrary",), vmem_limit_bytes=VMEM_LIMIT,
                                             has_side_effects=True),
        name="moe_dispatch",
    )(gstart, seg_cnt, seg_off, pad, ls8, h2_rows)


def _expert_kernel(be_ref, nu_ref, x_ref, w1_ref, w3_ref, w2_ref, o_ref, w1b, w3b, w2b):
    j = pl.program_id(0)
    e = be_ref[j]
    prev = be_ref[jnp.maximum(j - 1, 0)]
    used = j < nu_ref[0]

    @pl.when(jnp.logical_and(used, jnp.logical_or(j == 0, e != prev)))
    def _():
        w1b[...] = w1_ref[0].astype(BF16)
        w3b[...] = w3_ref[0].astype(BF16)
        w2b[...] = w2_ref[0].astype(BF16)

    @pl.when(used)
    def _():
        x = _from_token_rows(x_ref, MOE_BLK).astype(BF16)
        a = _dot(x, w1b[...])
        hid = (a * _sigmoid(a)) * _dot(x, w3b[...])
        _to_token_rows(o_ref, _dot(hid.astype(BF16), w2b[...]))

    @pl.when(jnp.logical_not(used))
    def _():
        o_ref[...] = jnp.zeros_like(o_ref)


def _experts(block_e, n_used, xs, w1, w3, w2):
    n_blocks = xs.shape[0] // (MOE_BLK * ROW_SUB)
    d, f = w1.shape[1], w1.shape[2]
    rows = pl.BlockSpec((MOE_BLK * ROW_SUB, LANES), lambda j, be, nu: (j, 0))
    return pl.pallas_call(
        _expert_kernel,
        out_shape=jax.ShapeDtypeStruct(xs.shape, U32),
        grid_spec=pltpu.PrefetchScalarGridSpec(
            num_scalar_prefetch=2,
            grid=(n_blocks,),
            in_specs=[rows,
                      pl.BlockSpec((1, d, f), lambda j, be, nu: (be[j], 0, 0)),
                      pl.BlockSpec((1, d, f), lambda j, be, nu: (be[j], 0, 0)),
                      pl.BlockSpec((1, f, d), lambda j, be, nu: (be[j], 0, 0))],
            out_specs=rows,
            scratch_shapes=[pltpu.VMEM((d, f), BF16), pltpu.VMEM((d, f), BF16), pltpu.VMEM((f, d), BF16)]),
        compiler_params=_params("arbitrary"),
        name="moe_experts",
    )(block_e, n_used, xs, w1, w3, w2)


def _combine_kernel(gs_ref, cnt_ref, off_ref, ls_ref, w8_ref, x1_ref, h2_ref, g2_ref, ws1_ref, ws3_ref, ws2_ref,
                    fg_ref, ys_hbm, o_ref, ls_smem, w_smem, gbuf0, gbuf1, acc_rows, sem0, sem1, lsem, *, tm):
    i = pl.program_id(0)
    last = pl.num_programs(0) - 1
    cp1 = pltpu.make_async_copy(ls_ref, ls_smem, lsem)
    cp2 = pltpu.make_async_copy(w8_ref, w_smem, lsem)
    cp1.start()
    cp2.start()

    def fetch(tile, gbuf, sem):
        def per_expert(e, carry):
            _copy_rows(ys_hbm, gs_ref[tile, e], gbuf, off_ref[tile, e], cnt_ref[tile, e], tm, sem)
            return carry

        lax.fori_loop(0, N_EXPERTS, per_expert, 0)

    @pl.when(i == 0)
    def _():
        fetch(0, gbuf0, sem0)

    @pl.when(jnp.logical_and(i < last, i % 2 == 0))
    def _():
        fetch(i + 1, gbuf1, sem1)

    @pl.when(jnp.logical_and(i < last, i % 2 == 1))
    def _():
        fetch(i + 1, gbuf0, sem0)

    h2 = _from_token_rows(h2_ref, tm).astype(BF16)
    a = _dot(h2, ws1_ref[...])
    hid = (a * _sigmoid(a)) * _dot(h2, ws3_ref[...])
    acc = _dot(hid.astype(BF16), ws2_ref[...])
    cp1.wait()
    cp2.wait()

    def reduce_rows(gbuf, sem):
        _wait_rows(gbuf, sem, tm * TOP_K)

        def body(t, carry):
            lo = jnp.zeros((ROW_SUB, LANES), F32)
            hi = jnp.zeros((ROW_SUB, LANES), F32)
            for k in range(TOP_K):
                w = w_smem[t * TOP_K + k]
                words = gbuf[pl.ds(pl.multiple_of(ls_smem[t * TOP_K + k] * ROW_SUB, ROW_SUB), ROW_SUB), :]
                lo = lo + w * lax.bitcast_convert_type(words << 16, F32)
                hi = hi + w * lax.bitcast_convert_type(words & jnp.uint32(0xFFFF0000), F32)
            acc_rows[pl.ds(pl.multiple_of(t * SUBLANES, SUBLANES), ROW_SUB), :] = lo
            acc_rows[pl.ds(pl.multiple_of(t * SUBLANES, SUBLANES) + ROW_SUB, ROW_SUB), :] = hi
            return carry

        lax.fori_loop(0, tm, body, 0)

    pl.when(i % 2 == 0)(functools.partial(reduce_rows, gbuf0, sem0))
    pl.when(i % 2 == 1)(functools.partial(reduce_rows, gbuf1, sem1))
    routed = jnp.concatenate([acc_rows[pl.ds(s, tm, stride=SUBLANES), :] for s in range(SUBLANES)], axis=-1)
    y = x1_ref[...] + g2_ref[0] * (acc + routed)
    o_ref[...] = y * lax.rsqrt(jnp.mean(y * y, axis=-1, keepdims=True) + EPS) * fg_ref[...]


def _combine(gstart, seg_cnt, seg_off, ls8, w8, x1, h2_rows, g2, ws1, ws3, ws2, fg, ys, rows_per_batch, tm):
    n, d = x1.shape
    per = rows_per_batch // tm
    tok = pl.BlockSpec((tm * TOP_K,), lambda i, *_: (i,))
    full = lambda a: pl.BlockSpec(a.shape, lambda i, *_: (0, 0))
    return pl.pallas_call(
        functools.partial(_combine_kernel, tm=tm),
        out_shape=jax.ShapeDtypeStruct((n, d), F32),
        grid_spec=pltpu.PrefetchScalarGridSpec(
            num_scalar_prefetch=3,
            grid=(n // tm,),
            in_specs=[tok, tok, pl.BlockSpec((tm, d), lambda i, *_: (i, 0)),
                      pl.BlockSpec((tm * ROW_SUB, LANES), lambda i, *_: (i, 0)),
                      pl.BlockSpec((1, 1, d), lambda i, *_: (i // per, 0, 0)),
                      full(ws1), full(ws3), full(ws2), pl.BlockSpec((1, d), lambda i, *_: (0, 0)),
                      pl.BlockSpec(memory_space=pl.ANY)],
            out_specs=pl.BlockSpec((tm, d), lambda i, *_: (i, 0)),
            scratch_shapes=[pltpu.SMEM((tm * TOP_K,), jnp.int32), pltpu.SMEM((tm * TOP_K,), F32),
                            pltpu.VMEM((TOP_K * tm * ROW_SUB, LANES), U32),
                            pltpu.VMEM((TOP_K * tm * ROW_SUB, LANES), U32),
                            pltpu.VMEM((tm * SUBLANES, LANES), F32), pltpu.SemaphoreType.DMA,
                            pltpu.SemaphoreType.DMA, pltpu.SemaphoreType.DMA]),
        compiler_params=_params("arbitrary"),
        name="moe_combine_final",
    )(gstart, seg_cnt, seg_off, ls8, w8, x1, h2_rows, g2, ws1, ws3, ws2, fg.reshape(1, d), ys)


def _moe_plan(seg, counts, n_assign):
    cnt = counts.reshape(N_EXPERTS).astype(jnp.int32)
    padded = (cnt + MOE_BLK - 1) // MOE_BLK * MOE_BLK
    pends = jnp.cumsum(padded)
    pstarts = pends - padded
    n_blocks = (n_assign + N_EXPERTS * (MOE_BLK - 1) + MOE_BLK - 1) // MOE_BLK
    seg = seg[:, :, :3].astype(jnp.int32)
    gstart = pstarts[None, :] + seg[:, :, 0]
    blk_start = jnp.arange(n_blocks, dtype=jnp.int32) * MOE_BLK
    block_e = jnp.minimum(jnp.sum((blk_start[:, None] >= pends[None, :]).astype(jnp.int32), axis=1),
                          N_EXPERTS - 1).astype(jnp.int32)
    n_used = (pends[-1:] // MOE_BLK).astype(jnp.int32)
    pad = jnp.stack([pstarts + cnt, padded - cnt, jnp.broadcast_to(n_used, (N_EXPERTS,))], axis=0).astype(jnp.int32)
    return gstart, seg[:, :, 1], seg[:, :, 2], pad, block_e, n_used, n_blocks


def _mixer(x, c, ctx, c_ctx, w_ada, b_ada, norm1_g, norm2_g, w_in, s5_lam_re, s5_lam_im, s5_log_dt,
           s5_b_re, s5_b_im, s5_c_re, s5_c_im, s5_d, s5_w_glu, lb, hg_norm_g, p_a, p_b, w_out,
           moe_w_router, moe_b_router):
    b, l, d = x.shape
    lc = ctx.shape[1]
    n = b * l
    rows = l // GRID_W

    c8 = jnp.concatenate([c, c_ctx[None], jnp.zeros((8 - b - 1, d), F32)], axis=0)
    mod = _ada(c8, w_ada, b_ada)
    sh1, sc1, g1, sh2, sc2, g2 = [mod[:b, k * d:(k + 1) * d].reshape(b, 1, d) for k in range(6)]
    csh1, csc1 = mod[b:b + 1, 0:d].reshape(1, 1, d), mod[b:b + 1, d:2 * d].reshape(1, 1, d)

    w_in_b = w_in.astype(BF16)
    z = dict(zip([p[0] for p in _IN_PIECES],
                 _inproj(x.reshape(n, d), sc1, sh1, norm1_g, w_in_b, l, TOK_TILE, True)))
    zc = dict(zip([p[0] for p in _IN_PIECES],
                  _inproj(ctx.reshape(b * lc, d), csc1, csh1, norm1_g, w_in_b, lc, lc, False)))

    cx = lambda t: t.reshape(b, lc, HG_WIDTH)
    lb_row = lb.reshape(1, HG_WIDTH)
    o_f = _hgrn_pass(z["q"], z["ff"], z["i"], cx(zc["ff"]), cx(zc["i"]), lb_row, None, None, reverse=False)
    o_n = _hgrn_pass(z["q"], z["fb"], z["i"], cx(zc["fb"]), cx(zc["i"]), lb_row, o_f,
                     hg_norm_g.reshape(1, HG_DK), reverse=True)

    d_lag, w_s5_in, w_out_f, w_out_b, decay = _s5_weights(s5_lam_re, s5_lam_im, s5_log_dt, s5_b_re, s5_b_im,
                                                          s5_c_re, s5_c_im)
    kc, kl = lc // S5_T, l // S5_T
    u_lat = z["u"].reshape(b, kl, S5_T * S5_WIDTH)
    u_ctx = zc["u"].reshape(b, kc, S5_T * S5_WIDTH)
    rows_in = kl + kc
    u_ext = jnp.concatenate([u_lat, u_ctx], axis=1).reshape(b * rows_in, S5_T * S5_WIDTH)
    e = _s5_in(u_ext, d_lag, w_s5_in, (b * rows_in) // 2)
    states = _s5_scan(e, decay, b, rows_in, kl)
    d_row = s5_d.astype(F32).reshape(1, S5_WIDTH)
    y_a = _s5_out(states, w_out_f, w_out_b, e.reshape(b, rows_in, -1), u_lat.reshape(b * kl, -1), d_row,
                  s5_w_glu.astype(BF16))
    y_a = y_a.reshape(n, S5_WIDTH)

    return _merge(x.reshape(n, d), y_a, o_n, z["go"], z["ga"], z["gb"], g1, sc2, sh2, norm2_g,
                  p_a.astype(BF16), p_b.astype(BF16), w_out.astype(BF16),
                  moe_w_router.T.astype(BF16), moe_b_router.astype(F32).reshape(N_EXPERTS, 1), l, MOE_TILE) + (g2,)


def kernel(x, c, ctx, c_ctx, w_ada, b_ada, norm1_g, norm2_g, w_in, s5_lam_re, s5_lam_im, s5_log_dt, s5_b_re,
           s5_b_im, s5_c_re, s5_c_im, s5_d, s5_w_glu, hg_lb_logits, hg_norm_g, p_a, p_b, w_out, moe_w_router,
           moe_b_router, moe_w1, moe_w3, moe_w2, moe_ws1, moe_ws3, moe_ws2, final_norm_g):
    b, l, d = x.shape
    n = b * l
    assert w_ada.shape[0] == 1, "single-layer block"
    lb = jnp.cumsum(jax.nn.softmax(hg_lb_logits.astype(F32), axis=0), axis=0)[0]
    x1, h2_rows, ls8, w8, seg, counts, g2 = _mixer(
        x, c, ctx, c_ctx, w_ada[0], b_ada[0], norm1_g[0], norm2_g[0], w_in[0], s5_lam_re[0], s5_lam_im[0],
        s5_log_dt[0], s5_b_re[0], s5_b_im[0], s5_c_re[0], s5_c_im[0], s5_d[0], s5_w_glu[0], lb, hg_norm_g[0],
        p_a[0], p_b[0], w_out[0], moe_w_router[0], moe_b_router[0])
    gstart, seg_cnt, seg_off, pad, block_e, n_used, n_blocks = _moe_plan(seg, counts, n * TOP_K)
    ls_flat, w_flat = ls8.T.reshape(n * TOP_K), w8.T.reshape(n * TOP_K)
    xs = _dispatch(gstart, seg_cnt, seg_off, pad, ls_flat, h2_rows, n_blocks, MOE_TILE)
    ys = _experts(block_e, n_used, xs, moe_w1[0], moe_w3[0], moe_w2[0])
    out = _combine(gstart, seg_cnt, seg_off, ls_flat, w_flat, x1, h2_rows, g2, moe_ws1[0].astype(BF16),
                   moe_ws3[0].astype(BF16), moe_ws2[0].astype(BF16), final_norm_g, ys, l, MOE_TILE)
    return out.reshape(b, l, d)
```

```python
import functools
import math

import jax
import jax.numpy as jnp
from jax import lax
from jax.experimental import pallas as pl
from jax.experimental.pallas import tpu as pltpu

F32 = jnp.float32
BF16 = jnp.bfloat16

GRID_W = 64
S5_WIDTH = 256
S5_GROUP = 16
S5_GROUPS = 16
S5_STATE = 64
HG_HEADS = 6
HG_DK = 128
HG_WIDTH = HG_HEADS * HG_DK
N_EXPERTS = 64
ROUTE_GROUPS = 8
TOPK_GROUPS = 4
TOP_K = 8
ROUTED_SCALE = 2.5
EPS = 1e-6

LANES = 128
SUBLANES = 8

TOK_TILE = 512
S5_T = 16
HG_CHUNK = 64
VMEM_LIMIT = 56 * 1024 * 1024

_NT = (((1,), (1,)), ((), ()))
_TN = (((0,), (0,)), ((), ()))


def _params(*sem):
    return pltpu.CompilerParams(dimension_semantics=sem, vmem_limit_bytes=VMEM_LIMIT)


def _dot(a, b):
    return jnp.dot(a, b, preferred_element_type=F32)


def _sigmoid(x):
    return 1.0 / (1.0 + jnp.exp(-x))


def _ada_kernel(c_ref, w_ref, b_ref, o_ref):
    c = c_ref[...]
    s = (c * _sigmoid(c)).astype(BF16)
    o_ref[...] = _dot(s, w_ref[...].astype(BF16)) + b_ref[...]


def _ada(c8, w_ada, b_ada):
    d, n = w_ada.shape
    tn = 1536
    return pl.pallas_call(
        _ada_kernel,
        out_shape=jax.ShapeDtypeStruct((8, n), F32),
        grid=(n // tn,),
        in_specs=[pl.BlockSpec((8, d), lambda j: (0, 0)),
                  pl.BlockSpec((d, tn), lambda j: (0, j)),
                  pl.BlockSpec((1, tn), lambda j: (0, j))],
        out_specs=pl.BlockSpec((8, tn), lambda j: (0, j)),
        compiler_params=_params("arbitrary"),
        name="ada_mod",
    )(c8, w_ada, b_ada.reshape(1, n))


_IN_PIECES = (("u", 0, 256, BF16), ("q", 256, 768, BF16), ("ff", 1024, 768, F32),
              ("fb", 1792, 768, F32), ("i", 2560, 768, BF16), ("go", 3328, 768, BF16),
              ("ga", 4096, 1024, BF16), ("gb", 5120, 1024, BF16))


def _inproj_kernel(x_ref, sc_ref, sh_ref, g_ref, w_ref, *o_refs):
    x = x_ref[...]
    y = x * lax.rsqrt(jnp.mean(x * x, axis=-1, keepdims=True) + EPS) * g_ref[...]
    h = (y * (1.0 + sc_ref[0]) + sh_ref[0]).astype(BF16)
    for (_, a, wd, _), o_ref in zip(_IN_PIECES, o_refs):
        res = _dot(h, w_ref[:, a:a + wd]).astype(o_ref.dtype)
        if len(o_ref.shape) == 2:
            o_ref[...] = res
        else:
            for rr in range(o_ref.shape[2]):
                o_ref[0, :, rr, :] = res[rr * GRID_W:(rr + 1) * GRID_W, :]


_COLMAJOR_PIECES = ("q", "ff", "fb", "i")


def _inproj(x2d, sc, sh, g, w_bf16, rows_per_mod, tm, colmajor):
    n, d = x2d.shape
    per = rows_per_mod // tm
    mod_map = (lambda i: (i // per, 0, 0)) if sc.shape[0] > 1 else (lambda i: (0, 0, 0))
    shapes, specs = [], []
    for name, _, wd, dt in _IN_PIECES:
        if colmajor and name in _COLMAJOR_PIECES:
            shapes.append(jax.ShapeDtypeStruct((n // rows_per_mod, GRID_W, rows_per_mod // GRID_W, wd), dt))
            specs.append(pl.BlockSpec((1, GRID_W, tm // GRID_W, wd), lambda i: (i // per, 0, i % per, 0)))
        else:
            shapes.append(jax.ShapeDtypeStruct((n, wd), dt))
            specs.append(pl.BlockSpec((tm, wd), lambda i: (i, 0)))
    return pl.pallas_call(
        _inproj_kernel,
        out_shape=shapes,
        grid=(n // tm,),
        in_specs=[pl.BlockSpec((tm, d), lambda i: (i, 0)),
                  pl.BlockSpec((1, 1, d), mod_map),
                  pl.BlockSpec((1, 1, d), mod_map),
                  pl.BlockSpec((1, d), lambda i: (0, 0)),
                  pl.BlockSpec(w_bf16.shape, lambda i: (0, 0))],
        out_specs=specs,
        compiler_params=_params("arbitrary"),
        name="in_proj",
    )(x2d, sc, sh, g.reshape(1, d), w_bf16)


def _hgrn_gates(zf, lb):
    sig = _sigmoid(zf)
    logf = jnp.log(lb + (1.0 - lb) * sig)
    k = (1.0 - lb) * (1.0 - sig)
    return logf, k


def _chunk_cumsum(cs, logf):
    hi = logf.astype(BF16)
    lo = (logf - hi.astype(F32)).astype(BF16)
    return _dot(cs, hi) + _dot(cs, lo)


def _hgrn_state_step(zf, v, lb, st, cs, reverse):
    logf, k = _hgrn_gates(zf, lb)
    cum = _chunk_cumsum(cs, logf)
    t = 0 if reverse else HG_CHUNK - 1
    total = cum[t:t + 1, :]
    kdec = (k * jnp.exp(total - cum)).astype(BF16)
    st_new = st * jnp.exp(total) + lax.dot_general(v.astype(BF16), kdec, _TN, preferred_element_type=F32)
    return cum, k, st_new


def _hgrn_kernel(*refs, reverse, final, n_ctx_chunks):
    if final:
        q_ref, f_ref, v_ref, cf_ref, cv_ref, lb_ref, of_ref, g_ref, o_ref, st_ref = refs
        of_ref = of_ref.at[0]
    else:
        q_ref, f_ref, v_ref, cf_ref, cv_ref, lb_ref, o_ref, st_ref = refs
    q_ref, f_ref, v_ref, o_ref = q_ref.at[0], f_ref.at[0], v_ref.at[0], o_ref.at[0]
    c_len = HG_CHUNK
    n_rows = q_ref.shape[1]
    n_chunks = n_rows // c_len
    row = lax.broadcasted_iota(jnp.int32, (n_rows, n_rows), 0)
    col = lax.broadcasted_iota(jnp.int32, (n_rows, n_rows), 1)
    tri = (col >= row) if reverse else (col <= row)
    same_chunk = None
    for c in range(n_chunks):
        lo, hi = c * c_len, (c + 1) * c_len
        blk = (row >= lo) & (row < hi) & (col >= lo) & (col < hi)
        same_chunk = blk if same_chunk is None else (same_chunk | blk)
    mask = tri & same_chunk
    cs = jnp.where(mask, 1.0, 0.0).astype(BF16)

    @pl.when(pl.program_id(1) == 0)
    def _():
        cs1 = cs[:c_len, :c_len]
        order = range(n_ctx_chunks - 1, -1, -1) if reverse else range(n_ctx_chunks)
        for h in range(HG_HEADS):
            cols = slice(h * HG_DK, (h + 1) * HG_DK)
            st = jnp.zeros((HG_DK, HG_DK), F32)
            for c in order:
                rows = slice(c * c_len, (c + 1) * c_len)
                _, _, st = _hgrn_state_step(cf_ref[0, rows, cols], cv_ref[0, rows, cols].astype(F32),
                                            lb_ref[:, cols], st, cs1, reverse)
            st_ref[h] = st

    def per_chunk_rows(x, r):
        return jnp.concatenate([jnp.broadcast_to(x[c * c_len + r:c * c_len + r + 1, :], (c_len, x.shape[1]))
                                for c in range(n_chunks)], axis=0)

    lb = lb_ref[...]
    q = q_ref[0].astype(F32)
    v = v_ref[0]
    logf, k = _hgrn_gates(f_ref[0], lb)
    cum = _chunk_cumsum(cs, logf)
    r_ref = c_len // 2 - 1 if reverse else c_len // 2
    r_tot = 0 if reverse else c_len - 1
    ref = per_chunk_rows(cum, r_ref)
    qe = q * jnp.exp(cum - ref)
    ke = k * jnp.exp(ref - cum)
    qi, ki = qe.astype(BF16), ke.astype(BF16)
    q_in = (qe * jnp.exp(ref)).astype(BF16)
    tail = jnp.exp(per_chunk_rows(cum, r_tot) - ref)
    kdec = (ke * tail).astype(BF16)

    order = range(n_chunks - 1, -1, -1) if reverse else range(n_chunks)
    for h in range(HG_HEADS):
        cols = slice(h * HG_DK, (h + 1) * HG_DK)
        s = lax.dot_general(qi[:, cols], ki[:, cols], _NT, preferred_element_type=F32)
        o_intra = _dot(jnp.where(mask, s, 0.0).astype(BF16), v[:, cols])
        st = st_ref[h]
        for c in order:
            rows = slice(c * c_len, (c + 1) * c_len)
            o = o_intra[rows] + lax.dot_general(q_in[rows, cols], st.astype(BF16), _NT, preferred_element_type=F32)
            total = cum[c * c_len + r_tot:c * c_len + r_tot + 1, cols]
            st = st * jnp.exp(total) + lax.dot_general(v[rows, cols], kdec[rows, cols], _TN,
                                                       preferred_element_type=F32)
            if final:
                o = o + of_ref[0, rows, cols]
                o = o * lax.rsqrt(jnp.mean(o * o, axis=-1, keepdims=True) + EPS) * g_ref[...]
            o_ref[0, rows, cols] = o.astype(o_ref.dtype)
        st_ref[h] = st


def _hgrn_pass(q, f, v, cf, cv, lb, o_prev, g, *, reverse):
    b, nw, rows, _ = q.shape
    final = o_prev is not None
    wmap = (lambda bi, w: (bi, nw - 1 - w, 0, 0)) if reverse else (lambda bi, w: (bi, w, 0, 0))
    blk = pl.BlockSpec((1, 1, rows, HG_WIDTH), wmap)
    cblk = pl.BlockSpec((1, cf.shape[1], HG_WIDTH), lambda bi, w: (bi, 0, 0))
    in_specs = [blk, blk, blk, cblk, cblk, pl.BlockSpec((1, HG_WIDTH), lambda bi, w: (0, 0))]
    args = [q, f, v, cf, cv, lb]
    if final:
        in_specs += [blk, pl.BlockSpec((1, HG_DK), lambda bi, w: (0, 0))]
        args += [o_prev, g]
    return pl.pallas_call(
        functools.partial(_hgrn_kernel, reverse=reverse, final=final, n_ctx_chunks=cf.shape[1] // HG_CHUNK),
        out_shape=jax.ShapeDtypeStruct(q.shape, BF16 if final else F32),
        grid=(b, nw),
        in_specs=in_specs,
        out_specs=blk,
        scratch_shapes=[pltpu.VMEM((HG_HEADS, HG_DK, HG_DK), F32)],
        compiler_params=_params("arbitrary", "arbitrary"),
        name="hgrn_bwd" if reverse else "hgrn_fwd",
    )(*args)


def _s5_weights(lam_re, lam_im, log_dt, b_re, b_im, c_re, c_im):
    hp = lax.Precision.HIGHEST
    g, p, cc, t = S5_GROUPS, S5_STATE, S5_GROUP, S5_T
    lre = jnp.minimum(lam_re.astype(F32), -1e-4)
    lim = lam_im.astype(F32)
    dt = jnp.exp(log_dt.astype(F32))[..., None]
    ks = jnp.arange(t + 1, dtype=F32)[:, None, None, None]
    mag = jnp.exp(ks * (lre * dt)[None])
    pw_re = mag * jnp.cos(ks * (lim * dt)[None])
    pw_im = mag * jnp.sin(ks * (lim * dt)[None])
    nr, ni = pw_re[1] - 1.0, pw_im[1]
    den = lre * lre + lim * lim
    cf_re = (nr * lre + ni * lim) / den
    cf_im = (ni * lre - nr * lim) / den
    bb_re = cf_re[..., None] * b_re - cf_im[..., None] * b_im
    bb_im = cf_re[..., None] * b_im + cf_im[..., None] * b_re
    cre, cim = c_re.astype(F32), c_im.astype(F32)
    sw, ns = S5_WIDTH, 2 * g * p
    grp_of_row = jnp.arange(sw)[:, None] // cc

    cp_re = cre[None, None, :, :, :] * pw_re[:t, :, :, None, :] - cim[None, None] * pw_im[:t, :, :, None, :]
    cp_im = cre[None, None, :, :, :] * pw_im[:t, :, :, None, :] + cim[None, None] * pw_re[:t, :, :, None, :]
    kk = (jnp.einsum("kdgop,dgpi->dkgoi", cp_re, bb_re, precision=hp)
          - jnp.einsum("kdgop,dgpi->dkgoi", cp_im, bb_im, precision=hp))
    kf, kb = kk[0], kk[1]
    kall = jnp.concatenate([kb[:0:-1], (kf[0] + kb[0])[None], kf[1:]], axis=0)
    kt = kall.transpose(0, 1, 3, 2).reshape(2 * t - 1, sw, cc)

    def spread(x, period, reps):
        sel = (jnp.arange(period)[:, None] == (jnp.arange(period * reps)[None, :] % period)).astype(BF16)
        return jnp.dot(x.astype(BF16), sel, preferred_element_type=BF16)

    same = grp_of_row == (jnp.arange(sw)[None, :] // cc)
    d_lag = jnp.where(same[None], spread(kt, cc, g), 0)

    same_in = jnp.tile(grp_of_row, (t, 1)) == ((jnp.arange(ns)[None, :] % (g * p)) // p)

    def in_to_state(pre, pim, bre, bim):
        xre = pre[..., None] * bre[None] - pim[..., None] * bim[None]
        xim = pre[..., None] * bim[None] + pim[..., None] * bre[None]
        x = jnp.concatenate([spread(xre.transpose(0, 1, 3, 2).reshape(t * sw, p), p, g),
                             spread(xim.transpose(0, 1, 3, 2).reshape(t * sw, p), p, g)], axis=-1)
        return jnp.where(same_in, x, 0)

    w_in = jnp.concatenate([in_to_state(pw_re[t - 1::-1, 0], pw_im[t - 1::-1, 0], bb_re[0], bb_im[0]),
                            in_to_state(pw_re[:t, 1], pw_im[:t, 1], bb_re[1], bb_im[1])], axis=1)

    same_out = ((jnp.arange(ns)[:, None] % (g * p)) // p) == ((jnp.arange(t * sw)[None, :] // cc) % g)
    col = jnp.arange(t * sw)
    pick = (jnp.arange(t * cc)[:, None] == ((col // sw) * cc + col % cc)[None, :]).astype(BF16)

    def state_to_out(pre, pim):
        are = cre[None] * pre[:, :, None, :] - cim[None] * pim[:, :, None, :]
        aim = cre[None] * pim[:, :, None, :] + cim[None] * pre[:, :, None, :]
        a = jnp.concatenate([are.transpose(1, 3, 0, 2), -aim.transpose(1, 3, 0, 2)], axis=0)
        a = jnp.dot(a.reshape(ns, t * cc).astype(BF16), pick, preferred_element_type=BF16)
        return jnp.where(same_out, a, 0)

    w_out_f = state_to_out(pw_re[1:, 0], pw_im[1:, 0])
    w_out_b = state_to_out(pw_re[t:0:-1, 1], pw_im[t:0:-1, 1])

    decay = jnp.stack([pw_re[t].reshape(2, g * p), pw_im[t].reshape(2, g * p)], axis=1)
    return d_lag, w_in, w_out_f, w_out_b, decay


def _s5_in_kernel(u_ref, d_ref, w_ref, o_ref):
    j = pl.program_id(0)

    @pl.when(j < S5_T)
    def _():
        acc = _dot(u_ref[:, 0:S5_WIDTH], d_ref[j + S5_T - 1])
        for s in range(1, S5_T):
            acc = acc + _dot(u_ref[:, s * S5_WIDTH:(s + 1) * S5_WIDTH], d_ref[j - s + S5_T - 1])
        o_ref[...] = acc

    @pl.when(j >= S5_T)
    def _():
        o_ref[...] = _dot(u_ref[...], w_ref[...])


def _s5_in(u, d_lag, w_in, tm):
    m, k = u.shape
    tn = S5_WIDTH
    nj = (k + w_in.shape[1]) // tn
    return pl.pallas_call(
        _s5_in_kernel,
        out_shape=jax.ShapeDtypeStruct((m, nj * tn), F32),
        grid=(nj, m // tm),
        in_specs=[pl.BlockSpec((tm, k), lambda j, i: (i, 0)),
                  pl.BlockSpec(d_lag.shape, lambda j, i: (0, 0, 0)),
                  pl.BlockSpec((k, tn), lambda j, i: (0, jnp.maximum(j - S5_T, 0)))],
        out_specs=pl.BlockSpec((tm, tn), lambda j, i: (i, j)),
        compiler_params=_params("arbitrary", "arbitrary"),
        name="s5_in",
    )(u, d_lag, w_in)


def _s5_scan_kernel(efr_ref, efi_ref, ebr_ref, ebi_ref, a_ref, hfr_ref, hfi_ref, hbr_ref, hbi_ref,
                    *, nb, rows_in, rows_out):
    dirs = ((efr_ref, efi_ref, hfr_ref, hfi_ref, a_ref[0, 0:1, :], a_ref[0, 1:2, :]),
            (ebr_ref, ebi_ref, hbr_ref, hbi_ref, a_ref[1, 0:1, :], a_ref[1, 1:2, :]))
    zero = jnp.zeros_like(dirs[0][4])

    def step(srcs, carry, store):
        new = []
        for di, (er_ref, ei_ref, hr_ref, hi_ref, are, aim) in enumerate(dirs):
            for bi in range(nb):
                hre, him = carry[2 * (di * nb + bi)], carry[2 * (di * nb + bi) + 1]
                if store:
                    hr_ref[pl.ds(bi * rows_out + srcs[di], 1), :] = hre
                    hi_ref[pl.ds(bi * rows_out + srcs[di], 1), :] = him
                ere = er_ref[pl.ds(bi * rows_in + srcs[di], 1), :]
                eim = ei_ref[pl.ds(bi * rows_in + srcs[di], 1), :]
                new += [are * hre - aim * him + ere, are * him + aim * hre + eim]
        return tuple(new)

    n_ctx = rows_in - rows_out
    carry = lax.fori_loop(0, n_ctx, lambda s, c: step((rows_out + s, rows_in - 1 - s), c, False),
                          tuple([zero] * (4 * nb)))
    lax.fori_loop(0, rows_out, lambda s, c: step((s, rows_out - 1 - s), c, True), carry)


def _s5_scan(e, decay, nb, rows_in, rows_out):
    tc = 256
    nsr = S5_GROUPS * S5_STATE
    c0 = (S5_T * S5_WIDTH) // tc
    nt = nsr // tc
    eblk = lambda k: pl.BlockSpec((nb * rows_in, tc), lambda j: (0, c0 + k * nt + j))
    hblk = pl.BlockSpec((nb * rows_out, tc), lambda j: (0, j))
    return pl.pallas_call(
        functools.partial(_s5_scan_kernel, nb=nb, rows_in=rows_in, rows_out=rows_out),
        out_shape=[jax.ShapeDtypeStruct((nb * rows_out, nsr), F32)] * 4,
        grid=(nt,),
        in_specs=[eblk(0), eblk(1), eblk(2), eblk(3), pl.BlockSpec((2, 2, tc), lambda j: (0, 0, j))],
        out_specs=[hblk] * 4,
        compiler_params=_params("arbitrary"),
        name="s5_scan",
    )(e, e, e, e, decay)


def _gelu_tanh(x):
    return 0.5 * x * (1.0 + jnp.tanh(math.sqrt(2.0 / math.pi) * (x + 0.044715 * x * x * x)))


def _s5_out_kernel(hfr_ref, hfi_ref, hbr_ref, hbi_ref, wf_ref, wb_ref, yi_ref, u_ref, d_ref, wg_ref, o_ref):
    nsr = hfr_ref.shape[1]
    y = yi_ref[0] + d_ref[...] * u_ref[...].astype(F32)
    for h_ref, w_ref, r0 in ((hfr_ref, wf_ref, 0), (hfi_ref, wf_ref, nsr), (hbr_ref, wb_ref, 0), (hbi_ref, wb_ref, nsr)):
        y = y + _dot(h_ref[...].astype(BF16), w_ref[r0:r0 + nsr, :])
    y = _gelu_tanh(y)
    gate = _sigmoid(_dot(y.astype(BF16), wg_ref[...]))
    o_ref[...] = (y * gate).astype(o_ref.dtype)


def _s5_out(states, w_out_f, w_out_b, e3, u_rows, d_row, w_glu):
    m, nsr = states[0].shape
    nb = e3.shape[0]
    tm = m // nb
    tn = S5_WIDTH
    st = pl.BlockSpec((tm, nsr), lambda i, j: (i, 0))
    wo = pl.BlockSpec((2 * nsr, tn), lambda i, j: (0, j))
    return pl.pallas_call(
        _s5_out_kernel,
        out_shape=jax.ShapeDtypeStruct((m, S5_T * S5_WIDTH), BF16),
        grid=(nb, S5_T),
        in_specs=[st, st, st, st, wo, wo,
                  pl.BlockSpec((1, tm, tn), lambda i, j: (i, 0, j)),
                  pl.BlockSpec((tm, tn), lambda i, j: (i, j)),
                  pl.BlockSpec((1, tn), lambda i, j: (0, 0)),
                  pl.BlockSpec((tn, tn), lambda i, j: (0, 0))],
        out_specs=pl.BlockSpec((tm, tn), lambda i, j: (i, j)),
        compiler_params=_params("arbitrary", "arbitrary"),
        name="s5_out",
    )(*states, w_out_f, w_out_b, e3, u_rows, d_row, w_glu)


U32 = jnp.uint32
ROW_SUB = 4


def _to_token_rows(ref, val):
    t, d = val.shape

    def rounded(x):
        u = lax.bitcast_convert_type(x, U32)
        return u + (jnp.uint32(0x7FFF) + ((u >> 16) & jnp.uint32(1)))

    w = (rounded(val[:, :d // 2]) >> 16) | (rounded(val[:, d // 2:]) & jnp.uint32(0xFFFF0000))
    for s in range(ROW_SUB):
        ref[pl.ds(s, t, stride=ROW_SUB), :] = w[:, s * LANES:(s + 1) * LANES]


def _from_token_rows(ref, t, row0=0):
    w = jnp.concatenate([ref[pl.ds(row0 * ROW_SUB + s, t, stride=ROW_SUB), :] for s in range(ROW_SUB)], axis=-1)
    lo = lax.bitcast_convert_type(w << 16, F32)
    hi = lax.bitcast_convert_type(w & jnp.uint32(0xFFFF0000), F32)
    return jnp.concatenate([lo, hi], axis=-1)


def _route(h2b, wr_ref, br_ref, cnt_ref, ls8_ref, w8_ref, seg_ref):
    tm = h2b.shape[0]
    per_group = N_EXPERTS // ROUTE_GROUPS
    scores = _sigmoid(lax.dot_general(wr_ref[...], h2b, _NT, preferred_element_type=F32))
    biased = scores + br_ref[...]
    neg = -jnp.inf
    sub = lax.broadcasted_iota(jnp.int32, (per_group, tm), 0)
    grp = []
    for gi in range(ROUTE_GROUPS):
        v = biased[gi * per_group:(gi + 1) * per_group, :]
        m1 = jnp.max(v, axis=0, keepdims=True)
        first = jnp.min(jnp.where(v == m1, sub, per_group), axis=0, keepdims=True)
        m2 = jnp.max(jnp.where(sub == first, neg, v), axis=0, keepdims=True)
        grp.append(m1 + m2)
    grp = jnp.concatenate(grp, axis=0)
    gid = lax.broadcasted_iota(jnp.int32, (ROUTE_GROUPS, tm), 0)
    beaten = jnp.zeros((ROUTE_GROUPS, tm), jnp.int32)
    for gj in range(ROUTE_GROUPS):
        r = grp[gj:gj + 1, :]
        beaten = beaten + jnp.where((r > grp) | ((r == grp) & (gj < gid)), 1, 0)
    group_ok = beaten < TOPK_GROUPS
    expert_ok = jnp.concatenate(
        [jnp.broadcast_to(group_ok[gi:gi + 1, :], (per_group, tm)) for gi in range(ROUTE_GROUPS)], axis=0)
    cur = jnp.where(expert_ok, biased, neg)
    eid = lax.broadcasted_iota(jnp.int32, (N_EXPERTS, tm), 0)
    sel = jnp.zeros((N_EXPERTS, tm), F32)
    picks, wts = [], []
    for _ in range(TOP_K):
        m = jnp.max(cur, axis=0, keepdims=True)
        idx = jnp.min(jnp.where(cur == m, eid, N_EXPERTS), axis=0, keepdims=True)
        hit = eid == idx
        picks.append(idx)
        wts.append(jnp.sum(jnp.where(hit, scores, 0.0), axis=0, keepdims=True))
        sel = jnp.where(hit, 1.0, sel)
        cur = jnp.where(hit, neg, cur)
    wsum = wts[0]
    for w in wts[1:]:
        wsum = wsum + w
    selb = sel.astype(BF16)
    ti = lax.broadcasted_iota(jnp.int32, (tm, tm), 0)
    tj = lax.broadcasted_iota(jnp.int32, (tm, tm), 1)
    rank = _dot(selb, jnp.where(ti < tj, 1.0, 0.0).astype(BF16))
    ei = lax.broadcasted_iota(jnp.int32, (N_EXPERTS, N_EXPERTS), 0)
    ej = lax.broadcasted_iota(jnp.int32, (N_EXPERTS, N_EXPERTS), 1)
    seg_off = jnp.sum(_dot(jnp.where(ej < ei, 1.0, 0.0).astype(BF16), selb), axis=1, keepdims=True)
    seg_cnt = jnp.sum(sel, axis=1, keepdims=True)
    slot = seg_off + rank
    for k in range(TOP_K):
        w8_ref[k:k + 1, :] = wts[k] / wsum * ROUTED_SCALE
        ls8_ref[k:k + 1, :] = jnp.sum(jnp.where(eid == picks[k], slot, 0.0), axis=0, keepdims=True).astype(jnp.int32)
    lane = lax.broadcasted_iota(jnp.int32, (N_EXPERTS, LANES), 1)
    seg_ref[0] = jnp.where(lane == 0, cnt_ref[...], jnp.where(lane == 1, seg_cnt, seg_off))
    cnt_ref[...] = cnt_ref[...] + seg_cnt


def _merge_kernel(x_ref, ya_ref, on_ref, go_ref, ga_ref, gb_ref, g1_ref, sc_ref, sh_ref, n2_ref,
                  pa_ref, pb_ref, wo_ref, wr_ref, br_ref,
                  x1_ref, h2_ref, ls8_ref, w8_ref, seg_ref, cnt_ref):
    @pl.when(pl.program_id(0) == 0)
    def _():
        cnt_ref[...] = jnp.zeros_like(cnt_ref)

    go = go_ref[...].astype(F32)
    on = jnp.concatenate([on_ref[0, :, rr, :] for rr in range(on_ref.shape[2])], axis=0)
    y_b = (on.astype(F32) * (go * _sigmoid(go))).astype(BF16)
    pa = _dot(ya_ref[...], pa_ref[...])
    pb = _dot(y_b, pb_ref[...])
    merged = _sigmoid(ga_ref[...].astype(F32)) * pa + _sigmoid(gb_ref[...].astype(F32)) * pb
    x1 = x_ref[...] + g1_ref[0] * _dot(merged.astype(BF16), wo_ref[...])
    x1_ref[...] = x1
    y = x1 * lax.rsqrt(jnp.mean(x1 * x1, axis=-1, keepdims=True) + EPS) * n2_ref[...]
    h2 = y * (1.0 + sc_ref[0]) + sh_ref[0]
    _to_token_rows(h2_ref, h2)
    _route(h2.astype(BF16), wr_ref, br_ref, cnt_ref, ls8_ref, w8_ref, seg_ref)


def _merge(x2d, ya, on, go, ga, gb, g1, sc2, sh2, n2g, pa, pb, wo, wr_t, br, rows_per_batch, tm):
    n, d = x2d.shape
    per = rows_per_batch // tm
    row = lambda wd: pl.BlockSpec((tm, wd), lambda i: (i, 0))
    mod = pl.BlockSpec((1, 1, d), lambda i: (i // per, 0, 0))
    full = lambda a: pl.BlockSpec(a.shape, lambda i: (0, 0))
    tok = pl.BlockSpec((TOP_K, tm), lambda i: (0, i))
    return pl.pallas_call(
        _merge_kernel,
        out_shape=[jax.ShapeDtypeStruct((n, d), F32), jax.ShapeDtypeStruct((n * ROW_SUB, LANES), U32),
                   jax.ShapeDtypeStruct((TOP_K, n), jnp.int32), jax.ShapeDtypeStruct((TOP_K, n), F32),
                   jax.ShapeDtypeStruct((n // tm, N_EXPERTS, LANES), F32),
                   jax.ShapeDtypeStruct((N_EXPERTS, 1), F32)],
        grid=(n // tm,),
        in_specs=[row(d), row(S5_WIDTH),
                  pl.BlockSpec((1, GRID_W, tm // GRID_W, HG_WIDTH), lambda i: (i // per, 0, i % per, 0)),
                  row(HG_WIDTH), row(d), row(d), mod, mod, mod,
                  pl.BlockSpec((1, d), lambda i: (0, 0)), full(pa), full(pb), full(wo), full(wr_t), full(br)],
        out_specs=[row(d), pl.BlockSpec((tm * ROW_SUB, LANES), lambda i: (i, 0)), tok, tok,
                   pl.BlockSpec((1, N_EXPERTS, LANES), lambda i: (i, 0, 0)),
                   pl.BlockSpec((N_EXPERTS, 1), lambda i: (0, 0))],
        compiler_params=_params("arbitrary"),
        name="merge_out_proj_route",
    )(x2d, ya, on, go, ga, gb, g1, sc2, sh2, n2g.reshape(1, d), pa, pb, wo, wr_t, br)


MOE_TILE = TOK_TILE
MOE_BLK = 1024


def _token_row(ref, r):
    return ref.at[pl.ds(pl.multiple_of(r * ROW_SUB, ROW_SUB), ROW_SUB)]


def _wait_rows(any_ref, sem, n_rows):
    view = any_ref.at[pl.ds(0, n_rows * ROW_SUB)]
    pltpu.make_async_copy(view, view, sem).wait()


def _rows(ref, r0, n):
    return ref.at[pl.ds(pl.multiple_of(r0 * ROW_SUB, ROW_SUB), n * ROW_SUB)]


def _pow2_pieces(n, max_piece, fn):
    done = 0
    piece = max_piece
    while piece >= 1:
        hit = (n & piece) != 0
        pl.when(hit)(functools.partial(fn, done, piece))
        done = done + (n & piece)
        piece //= 2


def _copy_rows(src_ref, src0, dst_ref, dst0, n, max_piece, sem):
    def piece(off, size):
        pltpu.make_async_copy(_rows(src_ref, src0 + off, size), _rows(dst_ref, dst0 + off, size), sem).start()
    _pow2_pieces(n, max_piece, piece)


def _wait_copied_rows(src_ref, dst_ref, n, max_piece, sem):
    def piece(off, size):
        pltpu.make_async_copy(_rows(src_ref, 0, size), _rows(dst_ref, 0, size), sem).wait()
    _pow2_pieces(n, max_piece, piece)


def _dispatch_kernel(gs_ref, cnt_ref, off_ref, pad_ref, ls_ref, h2_ref, xs_hbm, ls_smem, stage0, stage1, zbuf,
                     sem0, sem1, lsem, zsem, *, tm, n_blocks):
    i = pl.program_id(0)
    last = pl.num_programs(0) - 1
    cp = pltpu.make_async_copy(ls_ref, ls_smem, lsem)
    cp.start()
    cp.wait()

    def tile(stage, sem, prev_sem):
        def body(t, carry):
            row = h2_ref[pl.ds(pl.multiple_of(t * ROW_SUB, ROW_SUB), ROW_SUB), :]
            for k in range(TOP_K):
                slot = ls_smem[t * TOP_K + k]
                stage[pl.ds(pl.multiple_of(slot * ROW_SUB, ROW_SUB), ROW_SUB), :] = row
            return carry

        lax.fori_loop(0, tm, body, 0)

        def per_expert(e, carry):
            _copy_rows(stage, off_ref[i, e], xs_hbm, gs_ref[i, e], cnt_ref[i, e], tm, sem)
            return carry

        lax.fori_loop(0, N_EXPERTS, per_expert, 0)

        @pl.when(i > 0)
        def _():
            _wait_rows(xs_hbm, prev_sem, tm * TOP_K)

        @pl.when(i == last)
        def _():
            _wait_rows(xs_hbm, sem, tm * TOP_K)

    pl.when(i % 2 == 0)(functools.partial(tile, stage0, sem0, sem1))
    pl.when(i % 2 == 1)(functools.partial(tile, stage1, sem1, sem0))

    @pl.when(i == 0)
    def _():
        zbuf[...] = jnp.zeros_like(zbuf)

        def start(e, carry):
            _copy_rows(zbuf, 0, xs_hbm, pad_ref[0, e], pad_ref[1, e], MOE_BLK // 2, zsem)
            return carry

        def wait(e, carry):
            _wait_copied_rows(zbuf, xs_hbm, pad_ref[1, e], MOE_BLK // 2, zsem)
            return carry

        lax.fori_loop(0, N_EXPERTS, start, 0)
        lax.fori_loop(0, N_EXPERTS, wait, 0)

        def zero_block(j, carry):
            pltpu.make_async_copy(zbuf, _rows(xs_hbm, j * MOE_BLK, MOE_BLK), zsem).start()
            return carry

        def wait_block(j, carry):
            pltpu.make_async_copy(zbuf, _rows(xs_hbm, 0, MOE_BLK), zsem).wait()
            return carry

        lax.fori_loop(pad_ref[2, 0], n_blocks, zero_block, 0)
        lax.fori_loop(pad_ref[2, 0], n_blocks, wait_block, 0)


def _dispatch(gstart, seg_cnt, seg_off, pad, ls8, h2_rows, n_blocks, tm):
    n = ls8.shape[0] // TOP_K
    cap = n_blocks * MOE_BLK
    return pl.pallas_call(
        functools.partial(_dispatch_kernel, tm=tm, n_blocks=n_blocks),
        out_shape=jax.ShapeDtypeStruct((cap * ROW_SUB, LANES), U32),
        grid_spec=pltpu.PrefetchScalarGridSpec(
            num_scalar_prefetch=4,
            grid=(n // tm,),
            in_specs=[pl.BlockSpec((tm * TOP_K,), lambda i, *_: (i,)),
                      pl.BlockSpec((tm * ROW_SUB, LANES), lambda i, *_: (i, 0))],
            out_specs=pl.BlockSpec(memory_space=pl.ANY),
            scratch_shapes=[pltpu.SMEM((tm * TOP_K,), jnp.int32),
                            pltpu.VMEM((TOP_K * tm * ROW_SUB, LANES), U32),
                            pltpu.VMEM((TOP_K * tm * ROW_SUB, LANES), U32),
                            pltpu.VMEM((MOE_BLK * ROW_SUB, LANES), U32),
                            pltpu.SemaphoreType.DMA, pltpu.SemaphoreType.DMA, pltpu.SemaphoreType.DMA,
                            pltpu.SemaphoreType.DMA]),
        compiler_params=pltpu.CompilerParams(dimension_semantics=("arbitrary",), vmem_limit_bytes=VMEM_LIMIT,
                                             has_side_effects=True),
        name="moe_dispatch",
    )(gstart, seg_cnt, seg_off, pad, ls8, h2_rows)


def _expert_kernel(be_ref, nu_ref, x_ref, w1_ref, w3_ref, w2_ref, o_ref, w1b, w3b, w2b):
    j = pl.program_id(0)
    e = be_ref[j]
    prev = be_ref[jnp.maximum(j - 1, 0)]
    used = j < nu_ref[0]

    @pl.when(jnp.logical_and(used, jnp.logical_or(j == 0, e != prev)))
    def _():
        w1b[...] = w1_ref[0].astype(BF16)
        w3b[...] = w3_ref[0].astype(BF16)
        w2b[...] = w2_ref[0].astype(BF16)

    @pl.when(used)
    def _():
        x = _from_token_rows(x_ref, MOE_BLK).astype(BF16)
        a = _dot(x, w1b[...])
        hid = (a * _sigmoid(a)) * _dot(x, w3b[...])
        _to_token_rows(o_ref, _dot(hid.astype(BF16), w2b[...]))

    @pl.when(jnp.logical_not(used))
    def _():
        o_ref[...] = jnp.zeros_like(o_ref)


def _experts(block_e, n_used, xs, w1, w3, w2):
    n_blocks = xs.shape[0] // (MOE_BLK * ROW_SUB)
    d, f = w1.shape[1], w1.shape[2]
    rows = pl.BlockSpec((MOE_BLK * ROW_SUB, LANES), lambda j, be, nu: (j, 0))
    return pl.pallas_call(
        _expert_kernel,
        out_shape=jax.ShapeDtypeStruct(xs.shape, U32),
        grid_spec=pltpu.PrefetchScalarGridSpec(
            num_scalar_prefetch=2,
            grid=(n_blocks,),
            in_specs=[rows,
                      pl.BlockSpec((1, d, f), lambda j, be, nu: (be[j], 0, 0)),
                      pl.BlockSpec((1, d, f), lambda j, be, nu: (be[j], 0, 0)),
                      pl.BlockSpec((1, f, d), lambda j, be, nu: (be[j], 0, 0))],
            out_specs=rows,
            scratch_shapes=[pltpu.VMEM((d, f), BF16), pltpu.VMEM((d, f), BF16), pltpu.VMEM((f, d), BF16)]),
        compiler_params=_params("arbitrary"),
        name="moe_experts",
    )(block_e, n_used, xs, w1, w3, w2)


def _combine_kernel(gs_ref, cnt_ref, off_ref, ls_ref, w8_ref, x1_ref, h2_ref, g2_ref, ws1_ref, ws3_ref, ws2_ref,
                    fg_ref, ys_hbm, o_ref, ls_smem, w_smem, gbuf0, gbuf1, acc_rows, sem0, sem1, lsem, *, tm):
    i = pl.program_id(0)
    last = pl.num_programs(0) - 1
    cp1 = pltpu.make_async_copy(ls_ref, ls_smem, lsem)
    cp2 = pltpu.make_async_copy(w8_ref, w_smem, lsem)
    cp1.start()
    cp2.start()

    def fetch(tile, gbuf, sem):
        def per_expert(e, carry):
            _copy_rows(ys_hbm, gs_ref[tile, e], gbuf, off_ref[tile, e], cnt_ref[tile, e], tm, sem)
            return carry

        lax.fori_loop(0, N_EXPERTS, per_expert, 0)

    @pl.when(i == 0)
    def _():
        fetch(0, gbuf0, sem0)

    @pl.when(jnp.logical_and(i < last, i % 2 == 0))
    def _():
        fetch(i + 1, gbuf1, sem1)

    @pl.when(jnp.logical_and(i < last, i % 2 == 1))
    def _():
        fetch(i + 1, gbuf0, sem0)

    h2 = _from_token_rows(h2_ref, tm).astype(BF16)
    a = _dot(h2, ws1_ref[...])
    hid = (a * _sigmoid(a)) * _dot(h2, ws3_ref[...])
    acc = _dot(hid.astype(BF16), ws2_ref[...])
    cp1.wait()
    cp2.wait()

    def reduce_rows(gbuf, sem):
        _wait_rows(gbuf, sem, tm * TOP_K)

        def body(t, carry):
            lo = jnp.zeros((ROW_SUB, LANES), F32)
            hi = jnp.zeros((ROW_SUB, LANES), F32)
            for k in range(TOP_K):
                w = w_smem[t * TOP_K + k]
                words = gbuf[pl.ds(pl.multiple_of(ls_smem[t * TOP_K + k] * ROW_SUB, ROW_SUB), ROW_SUB), :]
                lo = lo + w * lax.bitcast_convert_type(words << 16, F32)
                hi = hi + w * lax.bitcast_convert_type(words & jnp.uint32(0xFFFF0000), F32)
            acc_rows[pl.ds(pl.multiple_of(t * SUBLANES, SUBLANES), ROW_SUB), :] = lo
            acc_rows[pl.ds(pl.multiple_of(t * SUBLANES, SUBLANES) + ROW_SUB, ROW_SUB), :] = hi
            return carry

        lax.fori_loop(0, tm, body, 0)

    pl.when(i % 2 == 0)(functools.partial(reduce_rows, gbuf0, sem0))
    pl.when(i % 2 == 1)(functools.partial(reduce_rows, gbuf1, sem1))
    routed = jnp.concatenate([acc_rows[pl.ds(s, tm, stride=SUBLANES), :] for s in range(SUBLANES)], axis=-1)
    y = x1_ref[...] + g2_ref[0] * (acc + routed)
    o_ref[...] = y * lax.rsqrt(jnp.mean(y * y, axis=-1, keepdims=True) + EPS) * fg_ref[...]


def _combine(gstart, seg_cnt, seg_off, ls8, w8, x1, h2_rows, g2, ws1, ws3, ws2, fg, ys, rows_per_batch, tm):
    n, d = x1.shape
    per = rows_per_batch // tm
    tok = pl.BlockSpec((tm * TOP_K,), lambda i, *_: (i,))
    full = lambda a: pl.BlockSpec(a.shape, lambda i, *_: (0, 0))
    return pl.pallas_call(
        functools.partial(_combine_kernel, tm=tm),
        out_shape=jax.ShapeDtypeStruct((n, d), F32),
        grid_spec=pltpu.PrefetchScalarGridSpec(
            num_scalar_prefetch=3,
            grid=(n // tm,),
            in_specs=[tok, tok, pl.BlockSpec((tm, d), lambda i, *_: (i, 0)),
                      pl.BlockSpec((tm * ROW_SUB, LANES), lambda i, *_: (i, 0)),
                      pl.BlockSpec((1, 1, d), lambda i, *_: (i // per, 0, 0)),
                      full(ws1), full(ws3), full(ws2), pl.BlockSpec((1, d), lambda i, *_: (0, 0)),
                      pl.BlockSpec(memory_space=pl.ANY)],
            out_specs=pl.BlockSpec((tm, d), lambda i, *_: (i, 0)),
            scratch_shapes=[pltpu.SMEM((tm * TOP_K,), jnp.int32), pltpu.SMEM((tm * TOP_K,), F32),
                            pltpu.VMEM((TOP_K * tm * ROW_SUB, LANES), U32),
                            pltpu.VMEM((TOP_K * tm * ROW_SUB, LANES), U32),
                            pltpu.VMEM((tm * SUBLANES, LANES), F32), pltpu.SemaphoreType.DMA,
                            pltpu.SemaphoreType.DMA, pltpu.SemaphoreType.DMA]),
        compiler_params=_params("arbitrary"),
        name="moe_combine_final",
    )(gstart, seg_cnt, seg_off, ls8, w8, x1, h2_rows, g2, ws1, ws3, ws2, fg.reshape(1, d), ys)


def _moe_plan(seg, counts, n_assign):
    cnt = counts.reshape(N_EXPERTS).astype(jnp.int32)
    padded = (cnt + MOE_BLK - 1) // MOE_BLK * MOE_BLK
    pends = jnp.cumsum(padded)
    pstarts = pends - padded
    n_blocks = (n_assign + N_EXPERTS * (MOE_BLK - 1) + MOE_BLK - 1) // MOE_BLK
    seg = seg[:, :, :3].astype(jnp.int32)
    gstart = pstarts[None, :] + seg[:, :, 0]
    blk_start = jnp.arange(n_blocks, dtype=jnp.int32) * MOE_BLK
    block_e = jnp.minimum(jnp.sum((blk_start[:, None] >= pends[None, :]).astype(jnp.int32), axis=1),
                          N_EXPERTS - 1).astype(jnp.int32)
    n_used = (pends[-1:] // MOE_BLK).astype(jnp.int32)
    pad = jnp.stack([pstarts + cnt, padded - cnt, jnp.broadcast_to(n_used, (N_EXPERTS,))], axis=0).astype(jnp.int32)
    return gstart, seg[:, :, 1], seg[:, :, 2], pad, block_e, n_used, n_blocks


def _mixer(x, c, ctx, c_ctx, w_ada, b_ada, norm1_g, norm2_g, w_in, s5_lam_re, s5_lam_im, s5_log_dt,
           s5_b_re, s5_b_im, s5_c_re, s5_c_im, s5_d, s5_w_glu, lb, hg_norm_g, p_a, p_b, w_out,
           moe_w_router, moe_b_router):
    b, l, d = x.shape
    lc = ctx.shape[1]
    n = b * l
    rows = l // GRID_W

    c8 = jnp.concatenate([c, c_ctx[None], jnp.zeros((8 - b - 1, d), F32)], axis=0)
    mod = _ada(c8, w_ada, b_ada)
    sh1, sc1, g1, sh2, sc2, g2 = [mod[:b, k * d:(k + 1) * d].reshape(b, 1, d) for k in range(6)]
    csh1, csc1 = mod[b:b + 1, 0:d].reshape(1, 1, d), mod[b:b + 1, d:2 * d].reshape(1, 1, d)

    w_in_b = w_in.astype(BF16)
    z = dict(zip([p[0] for p in _IN_PIECES],
                 _inproj(x.reshape(n, d), sc1, sh1, norm1_g, w_in_b, l, TOK_TILE, True)))
    zc = dict(zip([p[0] for p in _IN_PIECES],
                  _inproj(ctx.reshape(b * lc, d), csc1, csh1, norm1_g, w_in_b, lc, lc, False)))

    cx = lambda t: t.reshape(b, lc, HG_WIDTH)
    lb_row = lb.reshape(1, HG_WIDTH)
    o_f = _hgrn_pass(z["q"], z["ff"], z["i"], cx(zc["ff"]), cx(zc["i"]), lb_row, None, None, reverse=False)
    o_n = _hgrn_pass(z["q"], z["fb"], z["i"], cx(zc["fb"]), cx(zc["i"]), lb_row, o_f,
                     hg_norm_g.reshape(1, HG_DK), reverse=True)

    d_lag, w_s5_in, w_out_f, w_out_b, decay = _s5_weights(s5_lam_re, s5_lam_im, s5_log_dt, s5_b_re, s5_b_im,
                                                          s5_c_re, s5_c_im)
    kc, kl = lc // S5_T, l // S5_T
    u_lat = z["u"].reshape(b, kl, S5_T * S5_WIDTH)
    u_ctx = zc["u"].reshape(b, kc, S5_T * S5_WIDTH)
    rows_in = kl + kc
    u_ext = jnp.concatenate([u_lat, u_ctx], axis=1).reshape(b * rows_in, S5_T * S5_WIDTH)
    e = _s5_in(u_ext, d_lag, w_s5_in, (b * rows_in) // 2)
    states = _s5_scan(e, decay, b, rows_in, kl)
    d_row = s5_d.astype(F32).reshape(1, S5_WIDTH)
    y_a = _s5_out(states, w_out_f, w_out_b, e.reshape(b, rows_in, -1), u_lat.reshape(b * kl, -1), d_row,
                  s5_w_glu.astype(BF16))
    y_a = y_a.reshape(n, S5_WIDTH)

    return _merge(x.reshape(n, d), y_a, o_n, z["go"], z["ga"], z["gb"], g1, sc2, sh2, norm2_g,
                  p_a.astype(BF16), p_b.astype(BF16), w_out.astype(BF16),
                  moe_w_router.T.astype(BF16), moe_b_router.astype(F32).reshape(N_EXPERTS, 1), l, MOE_TILE) + (g2,)


def kernel(x, c, ctx, c_ctx, w_ada, b_ada, norm1_g, norm2_g, w_in, s5_lam_re, s5_lam_im, s5_log_dt, s5_b_re,
           s5_b_im, s5_c_re, s5_c_im, s5_d, s5_w_glu, hg_lb_logits, hg_norm_g, p_a, p_b, w_out, moe_w_router,
           moe_b_router, moe_w1, moe_w3, moe_w2, moe_ws1, moe_ws3, moe_ws2, final_norm_g):
    b, l, d = x.shape
    n = b * l
    assert w_ada.shape[0] == 1, "single-layer block"
    lb = jnp.cumsum(jax.nn.softmax(hg_lb_logits.astype(F32), axis=0), axis=0)[0]
    x1, h2_rows, ls8, w8, seg, counts, g2 = _mixer(
        x, c, ctx, c_ctx, w_ada[0], b_ada[0], norm1_g[0], norm2_g[0], w_in[0], s5_lam_re[0], s5_lam_im[0],
        s5_log_dt[0], s5_b_re[0], s5_b_im[0], s5_c_re[0], s5_c_im[0], s5_d[0], s5_w_glu[0], lb, hg_norm_g[0],
        p_a[0], p_b[0], w_out[0], moe_w_router[0], moe_b_router[0])
    gstart, seg_cnt, seg_off, pad, block_e, n_used, n_blocks = _moe_plan(seg, counts, n * TOP_K)
    ls_flat, w_flat = ls8.T.reshape(n * TOP_K), w8.T.reshape(n * TOP_K)
    xs = _dispatch(gstart, seg_cnt, seg_off, pad, ls_flat, h2_rows, n_blocks, MOE_TILE)
    ys = _experts(block_e, n_used, xs, moe_w1[0], moe_w3[0], moe_w2[0])
    out = _combine(gstart, seg_cnt, seg_off, ls_flat, w_flat, x1, h2_rows, g2, moe_ws1[0].astype(BF16),
                   moe_ws3[0].astype(BF16), moe_ws2[0].astype(BF16), final_norm_g, ys, l, MOE_TILE)
    return out.reshape(b, l, d)
```

```python
import functools
import math

import jax
import jax.numpy as jnp
from jax import lax
from jax.experimental import pallas as pl
from jax.experimental.pallas import tpu as pltpu

F32 = jnp.float32
BF16 = jnp.bfloat16

GRID_W = 64
S5_WIDTH = 256
S5_GROUP = 16
S5_GROUPS = 16
S5_STATE = 64
HG_HEADS = 6
HG_DK = 128
HG_WIDTH = HG_HEADS * HG_DK
N_EXPERTS = 64
ROUTE_GROUPS = 8
TOPK_GROUPS = 4
TOP_K = 8
ROUTED_SCALE = 2.5
EPS = 1e-6

LANES = 128
SUBLANES = 8

TOK_TILE = 512
S5_T = 16
HG_CHUNK = 64
VMEM_LIMIT = 56 * 1024 * 1024

_NT = (((1,), (1,)), ((), ()))
_TN = (((0,), (0,)), ((), ()))


def _params(*sem):
    return pltpu.CompilerParams(dimension_semantics=sem, vmem_limit_bytes=VMEM_LIMIT)


def _dot(a, b):
    return jnp.dot(a, b, preferred_element_type=F32)


def _sigmoid(x):
    return 1.0 / (1.0 + jnp.exp(-x))


def _ada_kernel(c_ref, w_ref, b_ref, o_ref):
    c = c_ref[...]
    s = (c * _sigmoid(c)).astype(BF16)
    o_ref[...] = _dot(s, w_ref[...].astype(BF16)) + b_ref[...]


def _ada(c8, w_ada, b_ada):
    d, n = w_ada.shape
    tn = 1536
    return pl.pallas_call(
        _ada_kernel,
        out_shape=jax.ShapeDtypeStruct((8, n), F32),
        grid=(n // tn,),
        in_specs=[pl.BlockSpec((8, d), lambda j: (0, 0)),
                  pl.BlockSpec((d, tn), lambda j: (0, j)),
                  pl.BlockSpec((1, tn), lambda j: (0, j))],
        out_specs=pl.BlockSpec((8, tn), lambda j: (0, j)),
        compiler_params=_params("arbitrary"),
        name="ada_mod",
    )(c8, w_ada, b_ada.reshape(1, n))


_IN_PIECES = (("u", 0, 256, BF16), ("q", 256, 768, BF16), ("ff", 1024, 768, F32),
              ("fb", 1792, 768, F32), ("i", 2560, 768, BF16), ("go", 3328, 768, BF16),
              ("ga", 4096, 1024, BF16), ("gb", 5120, 1024, BF16))


def _fold_rows(val, buf_a, buf_b):
    t = val.shape[0]
    buf_a[...] = val[:, :LANES]
    buf_b[...] = val[:, LANES:]
    pieces = []
    for s in range(S5_T):
        pieces += [buf_a[pl.ds(s, t // S5_T, stride=S5_T), :], buf_b[pl.ds(s, t // S5_T, stride=S5_T), :]]
    return jnp.concatenate(pieces, axis=-1)


def _unfold_rows(val, buf_a, buf_b):
    r = val.shape[0]
    for s in range(S5_T):
        buf_a[pl.ds(s, r, stride=S5_T), :] = val[:, s * S5_WIDTH:s * S5_WIDTH + LANES]
        buf_b[pl.ds(s, r, stride=S5_T), :] = val[:, s * S5_WIDTH + LANES:(s + 1) * S5_WIDTH]
    return jnp.concatenate([buf_a[...], buf_b[...]], axis=-1)


def _inproj_kernel(x_ref, sc_ref, sh_ref, g_ref, w_ref, *o_refs):
    o_refs, (fold_a, fold_b) = o_refs[:len(_IN_PIECES)], o_refs[len(_IN_PIECES):]
    x = x_ref[...]
    y = x * lax.rsqrt(jnp.mean(x * x, axis=-1, keepdims=True) + EPS) * g_ref[...]
    h = (y * (1.0 + sc_ref[0]) + sh_ref[0]).astype(BF16)
    for (name, a, wd, _), o_ref in zip(_IN_PIECES, o_refs):
        res = _dot(h, w_ref[:, a:a + wd])
        if name == "u":
            o_ref[...] = _fold_rows(res, fold_a, fold_b).astype(o_ref.dtype)
        elif len(o_ref.shape) == 2:
            o_ref[...] = res.astype(o_ref.dtype)
        else:
            res = res.astype(o_ref.dtype)
            for rr in range(o_ref.shape[2]):
                o_ref[0, :, rr, :] = res[rr * GRID_W:(rr + 1) * GRID_W, :]


_COLMAJOR_PIECES = ("q", "ff", "fb", "i")


def _inproj(x2d, sc, sh, g, w_bf16, rows_per_mod, tm, colmajor):
    n, d = x2d.shape
    per = rows_per_mod // tm
    mod_map = (lambda i: (i // per, 0, 0)) if sc.shape[0] > 1 else (lambda i: (0, 0, 0))
    shapes, specs = [], []
    for name, _, wd, dt in _IN_PIECES:
        if colmajor and name in _COLMAJOR_PIECES:
            shapes.append(jax.ShapeDtypeStruct((n // rows_per_mod, GRID_W, rows_per_mod // GRID_W, wd), dt))
            specs.append(pl.BlockSpec((1, GRID_W, tm // GRID_W, wd), lambda i: (i // per, 0, i % per, 0)))
        elif name == "u":
            shapes.append(jax.ShapeDtypeStruct((n // S5_T, S5_T * wd), dt))
            specs.append(pl.BlockSpec((tm // S5_T, S5_T * wd), lambda i: (i, 0)))
        else:
            shapes.append(jax.ShapeDtypeStruct((n, wd), dt))
            specs.append(pl.BlockSpec((tm, wd), lambda i: (i, 0)))
    return pl.pallas_call(
        _inproj_kernel,
        out_shape=shapes,
        grid=(n // tm,),
        in_specs=[pl.BlockSpec((tm, d), lambda i: (i, 0)),
                  pl.BlockSpec((1, 1, d), mod_map),
                  pl.BlockSpec((1, 1, d), mod_map),
                  pl.BlockSpec((1, d), lambda i: (0, 0)),
                  pl.BlockSpec(w_bf16.shape, lambda i: (0, 0))],
        out_specs=specs,
        scratch_shapes=[pltpu.VMEM((tm, LANES), F32), pltpu.VMEM((tm, LANES), F32)],
        compiler_params=_params("arbitrary"),
        name="in_proj",
    )(x2d, sc, sh, g.reshape(1, d), w_bf16)


def _hgrn_gates(zf, lb):
    sig = _sigmoid(zf)
    logf = jnp.log(lb + (1.0 - lb) * sig)
    k = (1.0 - lb) * (1.0 - sig)
    return logf, k


def _chunk_cumsum(cs, logf):
    hi = logf.astype(BF16)
    lo = (logf - hi.astype(F32)).astype(BF16)
    return _dot(cs, hi) + _dot(cs, lo)


def _hgrn_state_step(zf, v, lb, st, cs, reverse):
    logf, k = _hgrn_gates(zf, lb)
    cum = _chunk_cumsum(cs, logf)
    t = 0 if reverse else HG_CHUNK - 1
    total = cum[t:t + 1, :]
    kdec = (k * jnp.exp(total - cum)).astype(BF16)
    st_new = st * jnp.exp(total) + lax.dot_general(v.astype(BF16), kdec, _TN, preferred_element_type=F32)
    return cum, k, st_new


def _hgrn_kernel(*refs, reverse, final, n_ctx_chunks):
    if final:
        q_ref, f_ref, v_ref, cf_ref, cv_ref, lb_ref, of_ref, g_ref, o_ref, st_ref = refs
        of_ref = of_ref.at[0]
    else:
        q_ref, f_ref, v_ref, cf_ref, cv_ref, lb_ref, o_ref, st_ref = refs
    q_ref, f_ref, v_ref, o_ref = q_ref.at[0], f_ref.at[0], v_ref.at[0], o_ref.at[0]
    c_len = HG_CHUNK
    n_rows = q_ref.shape[1]
    n_chunks = n_rows // c_len
    row = lax.broadcasted_iota(jnp.int32, (n_rows, n_rows), 0)
    col = lax.broadcasted_iota(jnp.int32, (n_rows, n_rows), 1)
    tri = (col >= row) if reverse else (col <= row)
    same_chunk = None
    for c in range(n_chunks):
        lo, hi = c * c_len, (c + 1) * c_len
        blk = (row >= lo) & (row < hi) & (col >= lo) & (col < hi)
        same_chunk = blk if same_chunk is None else (same_chunk | blk)
    mask = tri & same_chunk
    cs = jnp.where(mask, 1.0, 0.0).astype(BF16)

    @pl.when(pl.program_id(1) == 0)
    def _():
        cs1 = cs[:c_len, :c_len]
        order = range(n_ctx_chunks - 1, -1, -1) if reverse else range(n_ctx_chunks)
        for h in range(HG_HEADS):
            cols = slice(h * HG_DK, (h + 1) * HG_DK)
            st = jnp.zeros((HG_DK, HG_DK), F32)
            for c in order:
                rows = slice(c * c_len, (c + 1) * c_len)
                _, _, st = _hgrn_state_step(cf_ref[0, rows, cols], cv_ref[0, rows, cols].astype(F32),
                                            lb_ref[:, cols], st, cs1, reverse)
            st_ref[h] = st

    def per_chunk_rows(x, r):
        return jnp.concatenate([jnp.broadcast_to(x[c * c_len + r:c * c_len + r + 1, :], (c_len, x.shape[1]))
                                for c in range(n_chunks)], axis=0)

    lb = lb_ref[...]
    q = q_ref[0].astype(F32)
    v = v_ref[0]
    logf, k = _hgrn_gates(f_ref[0], lb)
    cum = _chunk_cumsum(cs, logf)
    r_ref = c_len // 2 - 1 if reverse else c_len // 2
    r_tot = 0 if reverse else c_len - 1
    ref = per_chunk_rows(cum, r_ref)
    qe = q * jnp.exp(cum - ref)
    ke = k * jnp.exp(ref - cum)
    qi, ki = qe.astype(BF16), ke.astype(BF16)
    q_in = (qe * jnp.exp(ref)).astype(BF16)
    tail = jnp.exp(per_chunk_rows(cum, r_tot) - ref)
    kdec = (ke * tail).astype(BF16)

    order = range(n_chunks - 1, -1, -1) if reverse else range(n_chunks)
    for h in range(HG_HEADS):
        cols = slice(h * HG_DK, (h + 1) * HG_DK)
        s = lax.dot_general(qi[:, cols], ki[:, cols], _NT, preferred_element_type=F32)
        o_intra = _dot(jnp.where(mask, s, 0.0).astype(BF16), v[:, cols])
        st = st_ref[h]
        for c in order:
            rows = slice(c * c_len, (c + 1) * c_len)
            o = o_intra[rows] + lax.dot_general(q_in[rows, cols], st.astype(BF16), _NT, preferred_element_type=F32)
            total = cum[c * c_len + r_tot:c * c_len + r_tot + 1, cols]
            st = st * jnp.exp(total) + lax.dot_general(v[rows, cols], kdec[rows, cols], _TN,
                                                       preferred_element_type=F32)
            if final:
                o = o + of_ref[0, rows, cols]
                o = o * lax.rsqrt(jnp.mean(o * o, axis=-1, keepdims=True) + EPS) * g_ref[...]
            o_ref[0, rows, cols] = o.astype(o_ref.dtype)
        st_ref[h] = st


def _hgrn_pass(q, f, v, cf, cv, lb, o_prev, g, *, reverse):
    b, nw, rows, _ = q.shape
    final = o_prev is not None
    wmap = (lambda bi, w: (bi, nw - 1 - w, 0, 0)) if reverse else (lambda bi, w: (bi, w, 0, 0))
    blk = pl.BlockSpec((1, 1, rows, HG_WIDTH), wmap)
    cblk = pl.BlockSpec((1, cf.shape[1], HG_WIDTH), lambda bi, w: (bi, 0, 0))
    in_specs = [blk, blk, blk, cblk, cblk, pl.BlockSpec((1, HG_WIDTH), lambda bi, w: (0, 0))]
    args = [q, f, v, cf, cv, lb]
    if final:
        in_specs += [blk, pl.BlockSpec((1, HG_DK), lambda bi, w: (0, 0))]
        args += [o_prev, g]
    return pl.pallas_call(
        functools.partial(_hgrn_kernel, reverse=reverse, final=final, n_ctx_chunks=cf.shape[1] // HG_CHUNK),
        out_shape=jax.ShapeDtypeStruct(q.shape, BF16 if final else F32),
        grid=(b, nw),
        in_specs=in_specs,
        out_specs=blk,
        scratch_shapes=[pltpu.VMEM((HG_HEADS, HG_DK, HG_DK), F32)],
        compiler_params=_params("arbitrary", "arbitrary"),
        name="hgrn_bwd" if reverse else "hgrn_fwd",
    )(*args)


def _s5_weights(lam_re, lam_im, log_dt, b_re, b_im, c_re, c_im):
    hp = lax.Precision.HIGHEST
    g, p, cc, t = S5_GROUPS, S5_STATE, S5_GROUP, S5_T
    lre = jnp.minimum(lam_re.astype(F32), -1e-4)
    lim = lam_im.astype(F32)
    dt = jnp.exp(log_dt.astype(F32))[..., None]
    ks = jnp.arange(t + 1, dtype=F32)[:, None, None, None]
    mag = jnp.exp(ks * (lre * dt)[None])
    pw_re = mag * jnp.cos(ks * (lim * dt)[None])
    pw_im = mag * jnp.sin(ks * (lim * dt)[None])
    nr, ni = pw_re[1] - 1.0, pw_im[1]
    den = lre * lre + lim * lim
    cf_re = (nr * lre + ni * lim) / den
    cf_im = (ni * lre - nr * lim) / den
    bb_re = cf_re[..., None] * b_re - cf_im[..., None] * b_im
    bb_im = cf_re[..., None] * b_im + cf_im[..., None] * b_re
    cre, cim = c_re.astype(F32), c_im.astype(F32)
    sw, ns = S5_WIDTH, 2 * g * p
    grp_of_row = jnp.arange(sw)[:, None] // cc

    cp_re = cre[None, None, :, :, :] * pw_re[:t, :, :, None, :] - cim[None, None] * pw_im[:t, :, :, None, :]
    cp_im = cre[None, None, :, :, :] * pw_im[:t, :, :, None, :] + cim[None, None] * pw_re[:t, :, :, None, :]
    def contract_p(cp, bb):
        return jnp.sum(cp.transpose(4, 1, 0, 2, 3)[..., None] * bb.transpose(2, 0, 1, 3)[:, :, None, :, None, :],
                       axis=0)

    kk = contract_p(cp_re, bb_re) - contract_p(cp_im, bb_im)
    kf, kb = kk[0], kk[1]
    kall = jnp.concatenate([kb[:0:-1], (kf[0] + kb[0])[None], kf[1:]], axis=0)
    kt = kall.transpose(0, 1, 3, 2).reshape(2 * t - 1, sw, cc)

    def spread(x, period, reps):
        sel = (jnp.arange(period)[:, None] == (jnp.arange(period * reps)[None, :] % period)).astype(BF16)
        return jnp.dot(x.astype(BF16), sel, preferred_element_type=BF16)

    same = grp_of_row == (jnp.arange(sw)[None, :] // cc)
    d_lag = jnp.where(same[None], spread(kt, cc, g), 0)

    same_in = jnp.tile(grp_of_row, (t, 1)) == ((jnp.arange(ns)[None, :] % (g * p)) // p)

    def in_to_state(pre, pim, bre, bim):
        xre = pre[..., None] * bre[None] - pim[..., None] * bim[None]
        xim = pre[..., None] * bim[None] + pim[..., None] * bre[None]
        return [spread(xre.transpose(0, 1, 3, 2).reshape(t * sw, p), p, g),
                spread(xim.transpose(0, 1, 3, 2).reshape(t * sw, p), p, g)]

    w_in = jnp.concatenate(in_to_state(pw_re[t - 1::-1, 0], pw_im[t - 1::-1, 0], bb_re[0], bb_im[0])
                           + in_to_state(pw_re[:t, 1], pw_im[:t, 1], bb_re[1], bb_im[1]), axis=1)
    w_in = jnp.where(jnp.tile(same_in, (1, 2)), w_in, 0)

    same_out = ((jnp.arange(ns)[:, None] % (g * p)) // p) == ((jnp.arange(t * sw)[None, :] // cc) % g)
    col = jnp.arange(t * sw)
    pick = (jnp.arange(t * cc)[:, None] == ((col // sw) * cc + col % cc)[None, :]).astype(BF16)

    def state_to_out(pre, pim):
        are = cre[None] * pre[:, :, None, :] - cim[None] * pim[:, :, None, :]
        aim = cre[None] * pim[:, :, None, :] + cim[None] * pre[:, :, None, :]
        a = jnp.concatenate([are.transpose(1, 3, 0, 2), -aim.transpose(1, 3, 0, 2)], axis=0)
        a = jnp.dot(a.reshape(ns, t * cc).astype(BF16), pick, preferred_element_type=BF16)
        return jnp.where(same_out, a, 0)

    w_out_f = state_to_out(pw_re[1:, 0], pw_im[1:, 0])
    w_out_b = state_to_out(pw_re[t:0:-1, 1], pw_im[t:0:-1, 1])

    decay = jnp.stack([pw_re[t].reshape(2, g * p), pw_im[t].reshape(2, g * p)], axis=1)
    return d_lag, w_in, w_out_f, w_out_b, decay


def _s5_in_kernel(u_ref, d_ref, w_ref, o_ref):
    j = pl.program_id(0)

    @pl.when(j < S5_T)
    def _():
        acc = _dot(u_ref[:, 0:S5_WIDTH], d_ref[j + S5_T - 1])
        for s in range(1, S5_T):
            acc = acc + _dot(u_ref[:, s * S5_WIDTH:(s + 1) * S5_WIDTH], d_ref[j - s + S5_T - 1])
        o_ref[...] = acc

    @pl.when(j >= S5_T)
    def _():
        o_ref[...] = _dot(u_ref[...], w_ref[...])


def _s5_in(u, d_lag, w_in, tm):
    m, k = u.shape
    tn = S5_WIDTH
    nj = (k + w_in.shape[1]) // tn
    return pl.pallas_call(
        _s5_in_kernel,
        out_shape=jax.ShapeDtypeStruct((m, nj * tn), F32),
        grid=(nj, m // tm),
        in_specs=[pl.BlockSpec((tm, k), lambda j, i: (i, 0)),
                  pl.BlockSpec(d_lag.shape, lambda j, i: (0, 0, 0)),
                  pl.BlockSpec((k, tn), lambda j, i: (0, jnp.maximum(j - S5_T, 0)))],
        out_specs=pl.BlockSpec((tm, tn), lambda j, i: (i, j)),
        compiler_params=_params("arbitrary", "arbitrary"),
        name="s5_in",
    )(u, d_lag, w_in)


def _s5_scan_kernel(efr_ref, efi_ref, ebr_ref, ebi_ref, a_ref, hfr_ref, hfi_ref, hbr_ref, hbi_ref,
                    *, nb, rows_in, rows_out):
    dirs = ((efr_ref, efi_ref, hfr_ref, hfi_ref, a_ref[0, 0:1, :], a_ref[0, 1:2, :]),
            (ebr_ref, ebi_ref, hbr_ref, hbi_ref, a_ref[1, 0:1, :], a_ref[1, 1:2, :]))
    zero = jnp.zeros_like(dirs[0][4])

    def step(srcs, carry, store):
        new = []
        for di, (er_ref, ei_ref, hr_ref, hi_ref, are, aim) in enumerate(dirs):
            for bi in range(nb):
                hre, him = carry[2 * (di * nb + bi)], carry[2 * (di * nb + bi) + 1]
                if store:
                    hr_ref[pl.ds(bi * rows_out + srcs[di], 1), :] = hre
                    hi_ref[pl.ds(bi * rows_out + srcs[di], 1), :] = him
                ere = er_ref[pl.ds(bi * rows_in + srcs[di], 1), :]
                eim = ei_ref[pl.ds(bi * rows_in + srcs[di], 1), :]
                new += [are * hre - aim * him + ere, are * him + aim * hre + eim]
        return tuple(new)

    n_ctx = rows_in - rows_out
    carry = lax.fori_loop(0, n_ctx, lambda s, c: step((rows_out + s, rows_in - 1 - s), c, False),
                          tuple([zero] * (4 * nb)))
    lax.fori_loop(0, rows_out, lambda s, c: step((s, rows_out - 1 - s), c, True), carry)


def _s5_scan(e, decay, nb, rows_in, rows_out):
    tc = 256
    nsr = S5_GROUPS * S5_STATE
    c0 = (S5_T * S5_WIDTH) // tc
    nt = nsr // tc
    eblk = lambda k: pl.BlockSpec((nb * rows_in, tc), lambda j: (0, c0 + k * nt + j))
    hblk = pl.BlockSpec((nb * rows_out, tc), lambda j: (0, j))
    return pl.pallas_call(
        functools.partial(_s5_scan_kernel, nb=nb, rows_in=rows_in, rows_out=rows_out),
        out_shape=[jax.ShapeDtypeStruct((nb * rows_out, nsr), F32)] * 4,
        grid=(nt,),
        in_specs=[eblk(0), eblk(1), eblk(2), eblk(3), pl.BlockSpec((2, 2, tc), lambda j: (0, 0, j))],
        out_specs=[hblk] * 4,
        compiler_params=_params("arbitrary"),
        name="s5_scan",
    )(e, e, e, e, decay)


def _gelu_tanh(x):
    return 0.5 * x * (1.0 + jnp.tanh(math.sqrt(2.0 / math.pi) * (x + 0.044715 * x * x * x)))


def _s5_out_kernel(hfr_ref, hfi_ref, hbr_ref, hbi_ref, wf_ref, wb_ref, yi_ref, u_ref, d_ref, wg_ref, o_ref):
    nsr = hfr_ref.shape[1]
    y = yi_ref[0] + d_ref[...] * u_ref[...].astype(F32)
    for h_ref, w_ref, r0 in ((hfr_ref, wf_ref, 0), (hfi_ref, wf_ref, nsr), (hbr_ref, wb_ref, 0), (hbi_ref, wb_ref, nsr)):
        y = y + _dot(h_ref[...].astype(BF16), w_ref[r0:r0 + nsr, :])
    y = _gelu_tanh(y)
    gate = _sigmoid(_dot(y.astype(BF16), wg_ref[...]))
    o_ref[...] = (y * gate).astype(o_ref.dtype)


def _s5_out(states, w_out_f, w_out_b, e3, u_rows, d_row, w_glu):
    m, nsr = states[0].shape
    nb = e3.shape[0]
    tm = m // nb
    tn = S5_WIDTH
    st = pl.BlockSpec((tm, nsr), lambda i, j: (i, 0))
    wo = pl.BlockSpec((2 * nsr, tn), lambda i, j: (0, j))
    return pl.pallas_call(
        _s5_out_kernel,
        out_shape=jax.ShapeDtypeStruct((m, S5_T * S5_WIDTH), BF16),
        grid=(nb, S5_T),
        in_specs=[st, st, st, st, wo, wo,
                  pl.BlockSpec((1, tm, tn), lambda i, j: (i, 0, j)),
                  pl.BlockSpec((tm, tn), lambda i, j: (i, j)),
                  pl.BlockSpec((1, tn), lambda i, j: (0, 0)),
                  pl.BlockSpec((tn, tn), lambda i, j: (0, 0))],
        out_specs=pl.BlockSpec((tm, tn), lambda i, j: (i, j)),
        compiler_params=_params("arbitrary", "arbitrary"),
        name="s5_out",
    )(*states, w_out_f, w_out_b, e3, u_rows, d_row, w_glu)


U32 = jnp.uint32
ROW_SUB = 4


def _to_token_rows(ref, val):
    t, d = val.shape

    def rounded(x):
        u = lax.bitcast_convert_type(x, U32)
        return u + (jnp.uint32(0x7FFF) + ((u >> 16) & jnp.uint32(1)))

    w = (rounded(val[:, :d // 2]) >> 16) | (rounded(val[:, d // 2:]) & jnp.uint32(0xFFFF0000))
    for s in range(ROW_SUB):
        ref[pl.ds(s, t, stride=ROW_SUB), :] = w[:, s * LANES:(s + 1) * LANES]


def _from_token_rows(ref, t, row0=0):
    w = jnp.concatenate([ref[pl.ds(row0 * ROW_SUB + s, t, stride=ROW_SUB), :] for s in range(ROW_SUB)], axis=-1)
    lo = lax.bitcast_convert_type(w << 16, F32)
    hi = lax.bitcast_convert_type(w & jnp.uint32(0xFFFF0000), F32)
    return jnp.concatenate([lo, hi], axis=-1)


def _route(h2b, wr_ref, br_ref, cnt_ref, ls8_ref, w8_ref, seg_ref):
    tm = h2b.shape[0]
    per_group = N_EXPERTS // ROUTE_GROUPS
    scores = _sigmoid(lax.dot_general(wr_ref[...], h2b, _NT, preferred_element_type=F32))
    biased = scores + br_ref[...]
    neg = -jnp.inf
    sub = lax.broadcasted_iota(jnp.int32, (per_group, tm), 0)
    grp = []
    for gi in range(ROUTE_GROUPS):
        v = biased[gi * per_group:(gi + 1) * per_group, :]
        m1 = jnp.max(v, axis=0, keepdims=True)
        first = jnp.min(jnp.where(v == m1, sub, per_group), axis=0, keepdims=True)
        m2 = jnp.max(jnp.where(sub == first, neg, v), axis=0, keepdims=True)
        grp.append(m1 + m2)
    grp = jnp.concatenate(grp, axis=0)
    gid = lax.broadcasted_iota(jnp.int32, (ROUTE_GROUPS, tm), 0)
    beaten = jnp.zeros((ROUTE_GROUPS, tm), jnp.int32)
    for gj in range(ROUTE_GROUPS):
        r = grp[gj:gj + 1, :]
        beaten = beaten + jnp.where((r > grp) | ((r == grp) & (gj < gid)), 1, 0)
    group_ok = beaten < TOPK_GROUPS
    expert_ok = jnp.concatenate(
        [jnp.broadcast_to(group_ok[gi:gi + 1, :], (per_group, tm)) for gi in range(ROUTE_GROUPS)], axis=0)
    cur = jnp.where(expert_ok, biased, neg)
    eid = lax.broadcasted_iota(jnp.int32, (N_EXPERTS, tm), 0)
    sel = jnp.zeros((N_EXPERTS, tm), F32)
    picks, wts = [], []
    for _ in range(TOP_K):
        m = jnp.max(cur, axis=0, keepdims=True)
        idx = jnp.min(jnp.where(cur == m, eid, N_EXPERTS), axis=0, keepdims=True)
        hit = eid == idx
        picks.append(idx)
        wts.append(jnp.sum(jnp.where(hit, scores, 0.0), axis=0, keepdims=True))
        sel = jnp.where(hit, 1.0, sel)
        cur = jnp.where(hit, neg, cur)
    wsum = wts[0]
    for w in wts[1:]:
        wsum = wsum + w
    selb = sel.astype(BF16)
    ti = lax.broadcasted_iota(jnp.int32, (tm, tm), 0)
    tj = lax.broadcasted_iota(jnp.int32, (tm, tm), 1)
    rank = _dot(selb, jnp.where(ti < tj, 1.0, 0.0).astype(BF16))
    ei = lax.broadcasted_iota(jnp.int32, (N_EXPERTS, N_EXPERTS), 0)
    ej = lax.broadcasted_iota(jnp.int32, (N_EXPERTS, N_EXPERTS), 1)
    seg_off = jnp.sum(_dot(jnp.where(ej < ei, 1.0, 0.0).astype(BF16), selb), axis=1, keepdims=True)
    seg_cnt = jnp.sum(sel, axis=1, keepdims=True)
    slot = seg_off + rank
    for k in range(TOP_K):
        w8_ref[k:k + 1, :] = wts[k] / wsum * ROUTED_SCALE
        ls8_ref[k:k + 1, :] = jnp.sum(jnp.where(eid == picks[k], slot, 0.0), axis=0, keepdims=True).astype(jnp.int32)
    lane = lax.broadcasted_iota(jnp.int32, (N_EXPERTS, LANES), 1)
    seg_ref[0] = jnp.where(lane == 0, cnt_ref[...], jnp.where(lane == 1, seg_cnt, seg_off))
    cnt_ref[...] = cnt_ref[...] + seg_cnt


def _merge_kernel(x_ref, ya_ref, on_ref, go_ref, ga_ref, gb_ref, g1_ref, sc_ref, sh_ref, n2_ref,
                  pa_ref, pb_ref, wo_ref, wr_ref, br_ref,
                  x1_ref, h2_ref, ls8_ref, w8_ref, seg_ref, cnt_ref, fold_a, fold_b):
    @pl.when(pl.program_id(0) == 0)
    def _():
        cnt_ref[...] = jnp.zeros_like(cnt_ref)

    go = go_ref[...].astype(F32)
    on = jnp.concatenate([on_ref[0, :, rr, :] for rr in range(on_ref.shape[2])], axis=0)
    y_b = (on.astype(F32) * (go * _sigmoid(go))).astype(BF16)
    y_a = _unfold_rows(ya_ref[...].astype(F32), fold_a, fold_b).astype(BF16)
    pa = _dot(y_a, pa_ref[...])
    pb = _dot(y_b, pb_ref[...])
    merged = _sigmoid(ga_ref[...].astype(F32)) * pa + _sigmoid(gb_ref[...].astype(F32)) * pb
    x1 = x_ref[...] + g1_ref[0] * _dot(merged.astype(BF16), wo_ref[...])
    x1_ref[...] = x1
    y = x1 * lax.rsqrt(jnp.mean(x1 * x1, axis=-1, keepdims=True) + EPS) * n2_ref[...]
    h2 = y * (1.0 + sc_ref[0]) + sh_ref[0]
    _to_token_rows(h2_ref, h2)
    _route(h2.astype(BF16), wr_ref, br_ref, cnt_ref, ls8_ref, w8_ref, seg_ref)


def _merge(x2d, ya, on, go, ga, gb, g1, sc2, sh2, n2g, pa, pb, wo, wr_t, br, rows_per_batch, tm):
    n, d = x2d.shape
    per = rows_per_batch // tm
    row = lambda wd: pl.BlockSpec((tm, wd), lambda i: (i, 0))
    mod = pl.BlockSpec((1, 1, d), lambda i: (i // per, 0, 0))
    full = lambda a: pl.BlockSpec(a.shape, lambda i: (0, 0))
    tok = pl.BlockSpec((TOP_K, tm), lambda i: (0, i))
    return pl.pallas_call(
        _merge_kernel,
        out_shape=[jax.ShapeDtypeStruct((n, d), F32), jax.ShapeDtypeStruct((n * ROW_SUB, LANES), U32),
                   jax.ShapeDtypeStruct((TOP_K, n), jnp.int32), jax.ShapeDtypeStruct((TOP_K, n), F32),
                   jax.ShapeDtypeStruct((n // tm, N_EXPERTS, LANES), F32),
                   jax.ShapeDtypeStruct((N_EXPERTS, 1), F32)],
        grid=(n // tm,),
        in_specs=[row(d), pl.BlockSpec((tm // S5_T, S5_T * S5_WIDTH), lambda i: (i, 0)),
                  pl.BlockSpec((1, GRID_W, tm // GRID_W, HG_WIDTH), lambda i: (i // per, 0, i % per, 0)),
                  row(HG_WIDTH), row(d), row(d), mod, mod, mod,
                  pl.BlockSpec((1, d), lambda i: (0, 0)), full(pa), full(pb), full(wo), full(wr_t), full(br)],
        out_specs=[row(d), pl.BlockSpec((tm * ROW_SUB, LANES), lambda i: (i, 0)), tok, tok,
                   pl.BlockSpec((1, N_EXPERTS, LANES), lambda i: (i, 0, 0)),
                   pl.BlockSpec((N_EXPERTS, 1), lambda i: (0, 0))],
        scratch_shapes=[pltpu.VMEM((tm, LANES), F32), pltpu.VMEM((tm, LANES), F32)],
        compiler_params=_params("arbitrary"),
        name="merge_out_proj_route",
    )(x2d, ya, on, go, ga, gb, g1, sc2, sh2, n2g.reshape(1, d), pa, pb, wo, wr_t, br)


MOE_TILE = TOK_TILE
MOE_BLK = 1024


def _token_row(ref, r):
    return ref.at[pl.ds(pl.multiple_of(r * ROW_SUB, ROW_SUB), ROW_SUB)]


def _wait_rows(any_ref, sem, n_rows):
    view = any_ref.at[pl.ds(0, n_rows * ROW_SUB)]
    pltpu.make_async_copy(view, view, sem).wait()


def _rows(ref, r0, n):
    return ref.at[pl.ds(pl.multiple_of(r0 * ROW_SUB, ROW_SUB), n * ROW_SUB)]


def _pow2_pieces(n, max_piece, fn):
    done = 0
    piece = max_piece
    while piece >= 1:
        hit = (n & piece) != 0
        pl.when(hit)(functools.partial(fn, done, piece))
        done = done + (n & piece)
        piece //= 2


def _copy_rows(src_ref, src0, dst_ref, dst0, n, max_piece, sem):
    def piece(off, size):
        pltpu.make_async_copy(_rows(src_ref, src0 + off, size), _rows(dst_ref, dst0 + off, size), sem).start()
    _pow2_pieces(n, max_piece, piece)


def _wait_copied_rows(src_ref, dst_ref, n, max_piece, sem):
    def piece(off, size):
        pltpu.make_async_copy(_rows(src_ref, 0, size), _rows(dst_ref, 0, size), sem).wait()
    _pow2_pieces(n, max_piece, piece)


def _dispatch_kernel(gs_ref, cnt_ref, off_ref, pad_ref, ls_ref, h2_ref, xs_hbm, ls_smem, stage0, stage1, zbuf,
                     sem0, sem1, lsem, zsem, *, tm, n_blocks):
    i = pl.program_id(0)
    last = pl.num_programs(0) - 1
    cp = pltpu.make_async_copy(ls_ref, ls_smem, lsem)
    cp.start()
    cp.wait()

    def tile(stage, sem, prev_sem):
        def body(t, carry):
            row = h2_ref[pl.ds(pl.multiple_of(t * ROW_SUB, ROW_SUB), ROW_SUB), :]
            for k in range(TOP_K):
                slot = ls_smem[t * TOP_K + k]
                stage[pl.ds(pl.multiple_of(slot * ROW_SUB, ROW_SUB), ROW_SUB), :] = row
            return carry

        lax.fori_loop(0, tm, body, 0)

        def per_expert(e, carry):
            _copy_rows(stage, off_ref[i, e], xs_hbm, gs_ref[i, e], cnt_ref[i, e], tm, sem)
            return carry

        lax.fori_loop(0, N_EXPERTS, per_expert, 0)

        @pl.when(i > 0)
        def _():
            _wait_rows(xs_hbm, prev_sem, tm * TOP_K)

        @pl.when(i == last)
        def _():
            _wait_rows(xs_hbm, sem, tm * TOP_K)

    pl.when(i % 2 == 0)(functools.partial(tile, stage0, sem0, sem1))
    pl.when(i % 2 == 1)(functools.partial(tile, stage1, sem1, sem0))

    @pl.when(i == 0)
    def _():
        zbuf[...] = jnp.zeros_like(zbuf)

        def start(e, carry):
            _copy_rows(zbuf, 0, xs_hbm, pad_ref[0, e], pad_ref[1, e], MOE_BLK // 2, zsem)
            return carry

        def wait(e, carry):
            _wait_copied_rows(zbuf, xs_hbm, pad_ref[1, e], MOE_BLK // 2, zsem)
            return carry

        lax.fori_loop(0, N_EXPERTS, start, 0)
        lax.fori_loop(0, N_EXPERTS, wait, 0)

        def zero_block(j, carry):
            pltpu.make_async_copy(zbuf, _rows(xs_hbm, j * MOE_BLK, MOE_BLK), zsem).start()
            return carry

        def wait_block(j, carry):
            pltpu.make_async_copy(zbuf, _rows(xs_hbm, 0, MOE_BLK), zsem).wait()
            return carry

        lax.fori_loop(pad_ref[2, 0], n_blocks, zero_block, 0)
        lax.fori_loop(pad_ref[2, 0], n_blocks, wait_block, 0)


def _dispatch(gstart, seg_cnt, seg_off, pad, ls8, h2_rows, n_blocks, tm):
    n = ls8.shape[0] // TOP_K
    cap = n_blocks * MOE_BLK
    return pl.pallas_call(
        functools.partial(_dispatch_kernel, tm=tm, n_blocks=n_blocks),
        out_shape=jax.ShapeDtypeStruct((cap * ROW_SUB, LANES), U32),
        grid_spec=pltpu.PrefetchScalarGridSpec(
            num_scalar_prefetch=4,
            grid=(n // tm,),
            in_specs=[pl.BlockSpec((tm * TOP_K,), lambda i, *_: (i,)),
                      pl.BlockSpec((tm * ROW_SUB, LANES), lambda i, *_: (i, 0))],
            out_specs=pl.BlockSpec(memory_space=pl.ANY),
            scratch_shapes=[pltpu.SMEM((tm * TOP_K,), jnp.int32),
                            pltpu.VMEM((TOP_K * tm * ROW_SUB, LANES), U32),
                            pltpu.VMEM((TOP_K * tm * ROW_SUB, LANES), U32),
                            pltpu.VMEM((MOE_BLK * ROW_SUB, LANES), U32),
                            pltpu.SemaphoreType.DMA, pltpu.SemaphoreType.DMA, pltpu.SemaphoreType.DMA,
                            pltpu.SemaphoreType.DMA]),
        compiler_params=pltpu.CompilerParams(dimension_semantics=("arbitrary",), vmem_limit_bytes=VMEM_LIMIT,
                                             has_side_effects=True),
        name="moe_dispatch",
    )(gstart, seg_cnt, seg_off, pad, ls8, h2_rows)


def _expert_kernel(be_ref, nu_ref, x_ref, w1_ref, w3_ref, w2_ref, o_ref, w1b, w3b, w2b):
    j = pl.program_id(0)
    e = be_ref[j]
    prev = be_ref[jnp.maximum(j - 1, 0)]
    used = j < nu_ref[0]

    @pl.when(jnp.logical_and(used, jnp.logical_or(j == 0, e != prev)))
    def _():
        w1b[...] = w1_ref[0].astype(BF16)
        w3b[...] = w3_ref[0].astype(BF16)
        w2b[...] = w2_ref[0].astype(BF16)

    @pl.when(used)
    def _():
        x = _from_token_rows(x_ref, MOE_BLK).astype(BF16)
        a = _dot(x, w1b[...])
        hid = (a * _sigmoid(a)) * _dot(x, w3b[...])
        _to_token_rows(o_ref, _dot(hid.astype(BF16), w2b[...]))

    @pl.when(jnp.logical_not(used))
    def _():
        o_ref[...] = jnp.zeros_like(o_ref)


def _experts(block_e, n_used, xs, w1, w3, w2):
    n_blocks = xs.shape[0] // (MOE_BLK * ROW_SUB)
    d, f = w1.shape[1], w1.shape[2]
    rows = pl.BlockSpec((MOE_BLK * ROW_SUB, LANES), lambda j, be, nu: (j, 0))
    return pl.pallas_call(
        _expert_kernel,
        out_shape=jax.ShapeDtypeStruct(xs.shape, U32),
        grid_spec=pltpu.PrefetchScalarGridSpec(
            num_scalar_prefetch=2,
            grid=(n_blocks,),
            in_specs=[rows,
                      pl.BlockSpec((1, d, f), lambda j, be, nu: (be[j], 0, 0)),
                      pl.BlockSpec((1, d, f), lambda j, be, nu: (be[j], 0, 0)),
                      pl.BlockSpec((1, f, d), lambda j, be, nu: (be[j], 0, 0))],
            out_specs=rows,
            scratch_shapes=[pltpu.VMEM((d, f), BF16), pltpu.VMEM((d, f), BF16), pltpu.VMEM((f, d), BF16)]),
        compiler_params=_params("arbitrary"),
        name="moe_experts",
    )(block_e, n_used, xs, w1, w3, w2)


def _combine_kernel(gs_ref, cnt_ref, off_ref, ls_ref, w8_ref, x1_ref, h2_ref, g2_ref, ws1_ref, ws3_ref, ws2_ref,
                    fg_ref, ys_hbm, o_ref, ls_smem, w_smem, gbuf0, gbuf1, acc_rows, sem0, sem1, lsem, *, tm):
    i = pl.program_id(0)
    last = pl.num_programs(0) - 1
    cp1 = pltpu.make_async_copy(ls_ref, ls_smem, lsem)
    cp2 = pltpu.make_async_copy(w8_ref, w_smem, lsem)
    cp1.start()
    cp2.start()

    def fetch(tile, gbuf, sem):
        def per_expert(e, carry):
            _copy_rows(ys_hbm, gs_ref[tile, e], gbuf, off_ref[tile, e], cnt_ref[tile, e], tm, sem)
            return carry

        lax.fori_loop(0, N_EXPERTS, per_expert, 0)

    @pl.when(i == 0)
    def _():
        fetch(0, gbuf0, sem0)

    @pl.when(jnp.logical_and(i < last, i % 2 == 0))
    def _():
        fetch(i + 1, gbuf1, sem1)

    @pl.when(jnp.logical_and(i < last, i % 2 == 1))
    def _():
        fetch(i + 1, gbuf0, sem0)

    h2 = _from_token_rows(h2_ref, tm).astype(BF16)
    a = _dot(h2, ws1_ref[...])
    hid = (a * _sigmoid(a)) * _dot(h2, ws3_ref[...])
    acc = _dot(hid.astype(BF16), ws2_ref[...])
    cp1.wait()
    cp2.wait()

    def reduce_rows(gbuf, sem):
        _wait_rows(gbuf, sem, tm * TOP_K)

        def body(t, carry):
            lo = jnp.zeros((ROW_SUB, LANES), F32)
            hi = jnp.zeros((ROW_SUB, LANES), F32)
            for k in range(TOP_K):
                w = w_smem[t * TOP_K + k]
                words = gbuf[pl.ds(pl.multiple_of(ls_smem[t * TOP_K + k] * ROW_SUB, ROW_SUB), ROW_SUB), :]
                lo = lo + w * lax.bitcast_convert_type(words << 16, F32)
                hi = hi + w * lax.bitcast_convert_type(words & jnp.uint32(0xFFFF0000), F32)
            acc_rows[pl.ds(pl.multiple_of(t * SUBLANES, SUBLANES), ROW_SUB), :] = lo
            acc_rows[pl.ds(pl.multiple_of(t * SUBLANES, SUBLANES) + ROW_SUB, ROW_SUB), :] = hi
            return carry

        lax.fori_loop(0, tm, body, 0)

    pl.when(i % 2 == 0)(functools.partial(reduce_rows, gbuf0, sem0))
    pl.when(i % 2 == 1)(functools.partial(reduce_rows, gbuf1, sem1))
    routed = jnp.concatenate([acc_rows[pl.ds(s, tm, stride=SUBLANES), :] for s in range(SUBLANES)], axis=-1)
    y = x1_ref[...] + g2_ref[0] * (acc + routed)
    o_ref[...] = y * lax.rsqrt(jnp.mean(y * y, axis=-1, keepdims=True) + EPS) * fg_ref[...]


def _combine(gstart, seg_cnt, seg_off, ls8, w8, x1, h2_rows, g2, ws1, ws3, ws2, fg, ys, rows_per_batch, tm):
    n, d = x1.shape
    per = rows_per_batch // tm
    tok = pl.BlockSpec((tm * TOP_K,), lambda i, *_: (i,))
    full = lambda a: pl.BlockSpec(a.shape, lambda i, *_: (0, 0))
    return pl.pallas_call(
        functools.partial(_combine_kernel, tm=tm),
        out_shape=jax.ShapeDtypeStruct((n, d), F32),
        grid_spec=pltpu.PrefetchScalarGridSpec(
            num_scalar_prefetch=3,
            grid=(n // tm,),
            in_specs=[tok, tok, pl.BlockSpec((tm, d), lambda i, *_: (i, 0)),
                      pl.BlockSpec((tm * ROW_SUB, LANES), lambda i, *_: (i, 0)),
                      pl.BlockSpec((1, 1, d), lambda i, *_: (i // per, 0, 0)),
                      full(ws1), full(ws3), full(ws2), pl.BlockSpec((1, d), lambda i, *_: (0, 0)),
                      pl.BlockSpec(memory_space=pl.ANY)],
            out_specs=pl.BlockSpec((tm, d), lambda i, *_: (i, 0)),
            scratch_shapes=[pltpu.SMEM((tm * TOP_K,), jnp.int32), pltpu.SMEM((tm * TOP_K,), F32),
                            pltpu.VMEM((TOP_K * tm * ROW_SUB, LANES), U32),
                            pltpu.VMEM((TOP_K * tm * ROW_SUB, LANES), U32),
                            pltpu.VMEM((tm * SUBLANES, LANES), F32), pltpu.SemaphoreType.DMA,
                            pltpu.SemaphoreType.DMA, pltpu.SemaphoreType.DMA]),
        compiler_params=_params("arbitrary"),
        name="moe_combine_final",
    )(gstart, seg_cnt, seg_off, ls8, w8, x1, h2_rows, g2, ws1, ws3, ws2, fg.reshape(1, d), ys)


def _moe_plan(seg, counts, n_assign):
    cnt = counts.reshape(N_EXPERTS).astype(jnp.int32)
    padded = (cnt + MOE_BLK - 1) // MOE_BLK * MOE_BLK
    pends = jnp.cumsum(padded)
    pstarts = pends - padded
    n_blocks = (n_assign + N_EXPERTS * (MOE_BLK - 1) + MOE_BLK - 1) // MOE_BLK
    seg = seg[:, :, :3].astype(jnp.int32)
    gstart = pstarts[None, :] + seg[:, :, 0]
    blk_start = jnp.arange(n_blocks, dtype=jnp.int32) * MOE_BLK
    block_e = jnp.minimum(jnp.sum((blk_start[:, None] >= pends[None, :]).astype(jnp.int32), axis=1),
                          N_EXPERTS - 1).astype(jnp.int32)
    n_used = (pends[-1:] // MOE_BLK).astype(jnp.int32)
    pad = jnp.stack([pstarts + cnt, padded - cnt, jnp.broadcast_to(n_used, (N_EXPERTS,))], axis=0).astype(jnp.int32)
    return gstart, seg[:, :, 1], seg[:, :, 2], pad, block_e, n_used, n_blocks


def _mixer(x, c, ctx, c_ctx, w_ada, b_ada, norm1_g, norm2_g, w_in, s5_lam_re, s5_lam_im, s5_log_dt,
           s5_b_re, s5_b_im, s5_c_re, s5_c_im, s5_d, s5_w_glu, lb, hg_norm_g, p_a, p_b, w_out,
           moe_w_router, moe_b_router):
    b, l, d = x.shape
    lc = ctx.shape[1]
    n = b * l
    rows = l // GRID_W

    c8 = jnp.concatenate([c, c_ctx[None], jnp.zeros((8 - b - 1, d), F32)], axis=0)
    mod = _ada(c8, w_ada, b_ada)
    sh1, sc1, g1, sh2, sc2, g2 = [mod[:b, k * d:(k + 1) * d].reshape(b, 1, d) for k in range(6)]
    csh1, csc1 = mod[b:b + 1, 0:d].reshape(1, 1, d), mod[b:b + 1, d:2 * d].reshape(1, 1, d)

    w_in_b = w_in.astype(BF16)
    z = dict(zip([p[0] for p in _IN_PIECES],
                 _inproj(x.reshape(n, d), sc1, sh1, norm1_g, w_in_b, l, TOK_TILE, True)))
    zc = dict(zip([p[0] for p in _IN_PIECES],
                  _inproj(ctx.reshape(b * lc, d), csc1, csh1, norm1_g, w_in_b, lc, lc, False)))

    cx = lambda t: t.reshape(b, lc, HG_WIDTH)
    lb_row = lb.reshape(1, HG_WIDTH)
    o_f = _hgrn_pass(z["q"], z["ff"], z["i"], cx(zc["ff"]), cx(zc["i"]), lb_row, None, None, reverse=False)
    o_n = _hgrn_pass(z["q"], z["fb"], z["i"], cx(zc["fb"]), cx(zc["i"]), lb_row, o_f,
                     hg_norm_g.reshape(1, HG_DK), reverse=True)

    d_lag, w_s5_in, w_out_f, w_out_b, decay = _s5_weights(s5_lam_re, s5_lam_im, s5_log_dt, s5_b_re, s5_b_im,
                                                          s5_c_re, s5_c_im)
    kc, kl = lc // S5_T, l // S5_T
    u_lat = z["u"].reshape(b, kl, S5_T * S5_WIDTH)
    u_ctx = zc["u"].reshape(b, kc, S5_T * S5_WIDTH)
    rows_in = kl + kc
    u_ext = jnp.concatenate([u_lat, u_ctx], axis=1).reshape(b * rows_in, S5_T * S5_WIDTH)
    e = _s5_in(u_ext, d_lag, w_s5_in, (b * rows_in) // 2)
    states = _s5_scan(e, decay, b, rows_in, kl)
    d_row = s5_d.astype(F32).reshape(1, S5_WIDTH)
    y_a = _s5_out(states, w_out_f, w_out_b, e.reshape(b, rows_in, -1), z["u"], d_row, s5_w_glu.astype(BF16))

    return _merge(x.reshape(n, d), y_a, o_n, z["go"], z["ga"], z["gb"], g1, sc2, sh2, norm2_g,
                  p_a.astype(BF16), p_b.astype(BF16), w_out.astype(BF16),
                  moe_w_router.T.astype(BF16), moe_b_router.astype(F32).reshape(N_EXPERTS, 1), l, MOE_TILE) + (g2,)


def kernel(x, c, ctx, c_ctx, w_ada, b_ada, norm1_g, norm2_g, w_in, s5_lam_re, s5_lam_im, s5_log_dt, s5_b_re,
           s5_b_im, s5_c_re, s5_c_im, s5_d, s5_w_glu, hg_lb_logits, hg_norm_g, p_a, p_b, w_out, moe_w_router,
           moe_b_router, moe_w1, moe_w3, moe_w2, moe_ws1, moe_ws3, moe_ws2, final_norm_g):
    b, l, d = x.shape
    n = b * l
    assert w_ada.shape[0] == 1, "single-layer block"
    lb = jnp.cumsum(jax.nn.softmax(hg_lb_logits.astype(F32), axis=0), axis=0)[0]
    x1, h2_rows, ls8, w8, seg, counts, g2 = _mixer(
        x, c, ctx, c_ctx, w_ada[0], b_ada[0], norm1_g[0], norm2_g[0], w_in[0], s5_lam_re[0], s5_lam_im[0],
        s5_log_dt[0], s5_b_re[0], s5_b_im[0], s5_c_re[0], s5_c_im[0], s5_d[0], s5_w_glu[0], lb, hg_norm_g[0],
        p_a[0], p_b[0], w_out[0], moe_w_router[0], moe_b_router[0])
    gstart, seg_cnt, seg_off, pad, block_e, n_used, n_blocks = _moe_plan(seg, counts, n * TOP_K)
    ls_flat, w_flat = ls8.T.reshape(n * TOP_K), w8.T.reshape(n * TOP_K)
    xs = _dispatch(gstart, seg_cnt, seg_off, pad, ls_flat, h2_rows, n_blocks, MOE_TILE)
    ys = _experts(block_e, n_used, xs, moe_w1[0], moe_w3[0], moe_w2[0])
    out = _combine(gstart, seg_cnt, seg_off, ls_flat, w_flat, x1, h2_rows, g2, moe_ws1[0].astype(BF16),
                   moe_ws3[0].astype(BF16), moe_ws2[0].astype(BF16), final_norm_g, ys, l, MOE_TILE)
    return out.reshape(b, l, d)
```

```python
import functools
import math

import jax
import jax.numpy as jnp
from jax import lax
from jax.experimental import pallas as pl
from jax.experimental.pallas import tpu as pltpu

F32 = jnp.float32
BF16 = jnp.bfloat16

GRID_W = 64
S5_WIDTH = 256
S5_GROUP = 16
S5_GROUPS = 16
S5_STATE = 64
HG_HEADS = 6
HG_DK = 128
HG_WIDTH = HG_HEADS * HG_DK
N_EXPERTS = 64
ROUTE_GROUPS = 8
TOPK_GROUPS = 4
TOP_K = 8
ROUTED_SCALE = 2.5
EPS = 1e-6

LANES = 128
SUBLANES = 8

TOK_TILE = 512
S5_T = 16
HG_CHUNK = 64
HG_BATCH = 2
VMEM_LIMIT = 56 * 1024 * 1024

_NT = (((1,), (1,)), ((), ()))
_TN = (((0,), (0,)), ((), ()))


def _params(*sem):
    return pltpu.CompilerParams(dimension_semantics=sem, vmem_limit_bytes=VMEM_LIMIT)


def _dot(a, b):
    return jnp.dot(a, b, preferred_element_type=F32)


def _sigmoid(x):
    return 1.0 / (1.0 + jnp.exp(-x))


def _ada_kernel(c_ref, w_ref, b_ref, o_ref):
    c = c_ref[...]
    s = (c * _sigmoid(c)).astype(BF16)
    o_ref[...] = _dot(s, w_ref[...].astype(BF16)) + b_ref[...]


def _ada(c8, w_ada, b_ada):
    d, n = w_ada.shape
    tn = 1536
    return pl.pallas_call(
        _ada_kernel,
        out_shape=jax.ShapeDtypeStruct((8, n), F32),
        grid=(n // tn,),
        in_specs=[pl.BlockSpec((8, d), lambda j: (0, 0)),
                  pl.BlockSpec((d, tn), lambda j: (0, j)),
                  pl.BlockSpec((1, tn), lambda j: (0, j))],
        out_specs=pl.BlockSpec((8, tn), lambda j: (0, j)),
        compiler_params=_params("arbitrary"),
        name="ada_mod",
    )(c8, w_ada, b_ada.reshape(1, n))


_IN_PIECES = (("u", 0, 256, BF16), ("q", 256, 768, BF16), ("ff", 1024, 768, F32),
              ("fb", 1792, 768, F32), ("i", 2560, 768, BF16), ("go", 3328, 768, BF16),
              ("ga", 4096, 1024, BF16), ("gb", 5120, 1024, BF16))


def _fold_rows(val, buf_a, buf_b):
    t = val.shape[0]
    buf_a[...] = val[:, :LANES]
    buf_b[...] = val[:, LANES:]
    pieces = []
    for s in range(S5_T):
        pieces += [buf_a[pl.ds(s, t // S5_T, stride=S5_T), :], buf_b[pl.ds(s, t // S5_T, stride=S5_T), :]]
    return jnp.concatenate(pieces, axis=-1)


def _unfold_rows(val, buf_a, buf_b):
    r = val.shape[0]
    for s in range(S5_T):
        buf_a[pl.ds(s, r, stride=S5_T), :] = val[:, s * S5_WIDTH:s * S5_WIDTH + LANES]
        buf_b[pl.ds(s, r, stride=S5_T), :] = val[:, s * S5_WIDTH + LANES:(s + 1) * S5_WIDTH]
    return jnp.concatenate([buf_a[...], buf_b[...]], axis=-1)


def _inproj_kernel(x_ref, sc_ref, sh_ref, g_ref, w_ref, *o_refs):
    o_refs, (fold_a, fold_b) = o_refs[:len(_IN_PIECES)], o_refs[len(_IN_PIECES):]
    x = x_ref[...]
    y = x * lax.rsqrt(jnp.mean(x * x, axis=-1, keepdims=True) + EPS) * g_ref[...]
    h = (y * (1.0 + sc_ref[0]) + sh_ref[0]).astype(BF16)
    for (name, a, wd, _), o_ref in zip(_IN_PIECES, o_refs):
        res = _dot(h, w_ref[:, a:a + wd])
        if name == "u":
            o_ref[...] = _fold_rows(res, fold_a, fold_b).astype(o_ref.dtype)
        elif len(o_ref.shape) == 2:
            o_ref[...] = res.astype(o_ref.dtype)
        else:
            res = res.astype(o_ref.dtype)
            for rr in range(o_ref.shape[2]):
                o_ref[0, :, rr, :] = res[rr * GRID_W:(rr + 1) * GRID_W, :]


_COLMAJOR_PIECES = ("q", "ff", "fb", "i")


def _inproj(x2d, sc, sh, g, w_bf16, rows_per_mod, tm, colmajor):
    n, d = x2d.shape
    per = rows_per_mod // tm
    mod_map = (lambda i: (i // per, 0, 0)) if sc.shape[0] > 1 else (lambda i: (0, 0, 0))
    shapes, specs = [], []
    for name, _, wd, dt in _IN_PIECES:
        if colmajor and name in _COLMAJOR_PIECES:
            shapes.append(jax.ShapeDtypeStruct((n // rows_per_mod, GRID_W, rows_per_mod // GRID_W, wd), dt))
            specs.append(pl.BlockSpec((1, GRID_W, tm // GRID_W, wd), lambda i: (i // per, 0, i % per, 0)))
        elif name == "u":
            shapes.append(jax.ShapeDtypeStruct((n // S5_T, S5_T * wd), dt))
            specs.append(pl.BlockSpec((tm // S5_T, S5_T * wd), lambda i: (i, 0)))
        else:
            shapes.append(jax.ShapeDtypeStruct((n, wd), dt))
            specs.append(pl.BlockSpec((tm, wd), lambda i: (i, 0)))
    return pl.pallas_call(
        _inproj_kernel,
        out_shape=shapes,
        grid=(n // tm,),
        in_specs=[pl.BlockSpec((tm, d), lambda i: (i, 0)),
                  pl.BlockSpec((1, 1, d), mod_map),
                  pl.BlockSpec((1, 1, d), mod_map),
                  pl.BlockSpec((1, d), lambda i: (0, 0)),
                  pl.BlockSpec(w_bf16.shape, lambda i: (0, 0))],
        out_specs=specs,
        scratch_shapes=[pltpu.VMEM((tm, LANES), F32), pltpu.VMEM((tm, LANES), F32)],
        compiler_params=_params("arbitrary"),
        name="in_proj",
    )(x2d, sc, sh, g.reshape(1, d), w_bf16)


def _hgrn_gates(zf, lb):
    sig = _sigmoid(zf)
    logf = jnp.log(lb + (1.0 - lb) * sig)
    k = (1.0 - lb) * (1.0 - sig)
    return logf, k


def _chunk_cumsum(cs, logf):
    hi = logf.astype(BF16)
    lo = (logf - hi.astype(F32)).astype(BF16)
    return _dot(cs, hi) + _dot(cs, lo)


def _hgrn_state_step(zf, v, lb, st, cs, reverse):
    logf, k = _hgrn_gates(zf, lb)
    cum = _chunk_cumsum(cs, logf)
    t = 0 if reverse else HG_CHUNK - 1
    total = cum[t:t + 1, :]
    kdec = (k * jnp.exp(total - cum)).astype(BF16)
    st_new = st * jnp.exp(total) + lax.dot_general(v.astype(BF16), kdec, _TN, preferred_element_type=F32)
    return cum, k, st_new


def _hgrn_kernel(*refs, reverse, final, n_ctx_chunks):
    if final:
        q_all, f_all, v_all, cf_ref, cv_ref, lb_ref, of_all, g_ref, o_all, st_ref = refs
    else:
        q_all, f_all, v_all, cf_ref, cv_ref, lb_ref, o_all, st_ref = refs
        of_all = None
    n_batch = q_all.shape[0]
    c_len = HG_CHUNK
    n_rows = q_all.shape[2]
    n_chunks = n_rows // c_len
    row = lax.broadcasted_iota(jnp.int32, (n_rows, n_rows), 0)
    col = lax.broadcasted_iota(jnp.int32, (n_rows, n_rows), 1)
    tri = (col >= row) if reverse else (col <= row)
    same_chunk = None
    for c in range(n_chunks):
        lo, hi = c * c_len, (c + 1) * c_len
        blk = (row >= lo) & (row < hi) & (col >= lo) & (col < hi)
        same_chunk = blk if same_chunk is None else (same_chunk | blk)
    mask = tri & same_chunk
    cs = jnp.where(mask, 1.0, 0.0).astype(BF16)

    @pl.when(pl.program_id(1) == 0)
    def _():
        cs1 = cs[:c_len, :c_len]
        order = range(n_ctx_chunks - 1, -1, -1) if reverse else range(n_ctx_chunks)
        for bi in range(n_batch):
            for h in range(HG_HEADS):
                cols = slice(h * HG_DK, (h + 1) * HG_DK)
                st = jnp.zeros((HG_DK, HG_DK), F32)
                for c in order:
                    rows = slice(c * c_len, (c + 1) * c_len)
                    _, _, st = _hgrn_state_step(cf_ref[bi, rows, cols], cv_ref[bi, rows, cols].astype(F32),
                                                lb_ref[:, cols], st, cs1, reverse)
                st_ref[bi * HG_HEADS + h] = st

    def per_chunk_rows(x, r):
        return jnp.concatenate([jnp.broadcast_to(x[c * c_len + r:c * c_len + r + 1, :], (c_len, x.shape[1]))
                                for c in range(n_chunks)], axis=0)

    lb = lb_ref[...]
    r_ref = c_len // 2 - 1 if reverse else c_len // 2
    r_tot = 0 if reverse else c_len - 1
    order = range(n_chunks - 1, -1, -1) if reverse else range(n_chunks)
    for bi in range(n_batch):
        q = q_all[bi, 0].astype(F32)
        v = v_all[bi, 0]
        logf, k = _hgrn_gates(f_all[bi, 0], lb)
        cum = _chunk_cumsum(cs, logf)
        ref = per_chunk_rows(cum, r_ref)
        qe = q * jnp.exp(cum - ref)
        ke = k * jnp.exp(ref - cum)
        qi, ki = qe.astype(BF16), ke.astype(BF16)
        q_in = (qe * jnp.exp(ref)).astype(BF16)
        tail = jnp.exp(per_chunk_rows(cum, r_tot) - ref)
        kdec = (ke * tail).astype(BF16)
        for h in range(HG_HEADS):
            cols = slice(h * HG_DK, (h + 1) * HG_DK)
            s = lax.dot_general(qi[:, cols], ki[:, cols], _NT, preferred_element_type=F32)
            o_intra = _dot(jnp.where(mask, s, 0.0).astype(BF16), v[:, cols])
            st = st_ref[bi * HG_HEADS + h]
            for c in order:
                rows = slice(c * c_len, (c + 1) * c_len)
                o = o_intra[rows] + lax.dot_general(q_in[rows, cols], st.astype(BF16), _NT,
                                                    preferred_element_type=F32)
                total = cum[c * c_len + r_tot:c * c_len + r_tot + 1, cols]
                st = st * jnp.exp(total) + lax.dot_general(v[rows, cols], kdec[rows, cols], _TN,
                                                           preferred_element_type=F32)
                if final:
                    o = o + of_all[bi, 0, rows, cols]
                    o = o * lax.rsqrt(jnp.mean(o * o, axis=-1, keepdims=True) + EPS) * g_ref[...]
                o_all[bi, 0, rows, cols] = o.astype(o_all.dtype)
            st_ref[bi * HG_HEADS + h] = st


def _hgrn_pass(q, f, v, cf, cv, lb, o_prev, g, *, reverse):
    b, nw, rows, _ = q.shape
    nb = HG_BATCH if b % HG_BATCH == 0 else 1
    final = o_prev is not None
    wmap = (lambda bi, w: (bi, nw - 1 - w, 0, 0)) if reverse else (lambda bi, w: (bi, w, 0, 0))
    blk = pl.BlockSpec((nb, 1, rows, HG_WIDTH), wmap)
    cblk = pl.BlockSpec((nb, cf.shape[1], HG_WIDTH), lambda bi, w: (bi, 0, 0))
    in_specs = [blk, blk, blk, cblk, cblk, pl.BlockSpec((1, HG_WIDTH), lambda bi, w: (0, 0))]
    args = [q, f, v, cf, cv, lb]
    if final:
        in_specs += [blk, pl.BlockSpec((1, HG_DK), lambda bi, w: (0, 0))]
        args += [o_prev, g]
    return pl.pallas_call(
        functools.partial(_hgrn_kernel, reverse=reverse, final=final, n_ctx_chunks=cf.shape[1] // HG_CHUNK),
        out_shape=jax.ShapeDtypeStruct(q.shape, BF16 if final else F32),
        grid=(b // nb, nw),
        in_specs=in_specs,
        out_specs=blk,
        scratch_shapes=[pltpu.VMEM((nb * HG_HEADS, HG_DK, HG_DK), F32)],
        compiler_params=_params("arbitrary", "arbitrary"),
        name="hgrn_bwd" if reverse else "hgrn_fwd",
    )(*args)


def _s5_weights(lam_re, lam_im, log_dt, b_re, b_im, c_re, c_im):
    hp = lax.Precision.HIGHEST
    g, p, cc, t = S5_GROUPS, S5_STATE, S5_GROUP, S5_T
    lre = jnp.minimum(lam_re.astype(F32), -1e-4)
    lim = lam_im.astype(F32)
    dt = jnp.exp(log_dt.astype(F32))[..., None]
    ks = jnp.arange(t + 1, dtype=F32)[:, None, None, None]
    mag = jnp.exp(ks * (lre * dt)[None])
    pw_re = mag * jnp.cos(ks * (lim * dt)[None])
    pw_im = mag * jnp.sin(ks * (lim * dt)[None])
    nr, ni = pw_re[1] - 1.0, pw_im[1]
    den = lre * lre + lim * lim
    cf_re = (nr * lre + ni * lim) / den
    cf_im = (ni * lre - nr * lim) / den
    bb_re = cf_re[..., None] * b_re - cf_im[..., None] * b_im
    bb_im = cf_re[..., None] * b_im + cf_im[..., None] * b_re
    cre, cim = c_re.astype(F32), c_im.astype(F32)
    sw, ns = S5_WIDTH, 2 * g * p
    grp_of_row = jnp.arange(sw)[:, None] // cc

    cp_re = cre[None, None, :, :, :] * pw_re[:t, :, :, None, :] - cim[None, None] * pw_im[:t, :, :, None, :]
    cp_im = cre[None, None, :, :, :] * pw_im[:t, :, :, None, :] + cim[None, None] * pw_re[:t, :, :, None, :]
    def contract_p(cp, bb):
        return jnp.sum(cp.transpose(4, 1, 0, 2, 3)[..., None] * bb.transpose(2, 0, 1, 3)[:, :, None, :, None, :],
                       axis=0)

    kk = contract_p(cp_re, bb_re) - contract_p(cp_im, bb_im)
    kf, kb = kk[0], kk[1]
    kall = jnp.concatenate([kb[:0:-1], (kf[0] + kb[0])[None], kf[1:]], axis=0)
    kt = kall.transpose(0, 1, 3, 2).reshape(2 * t - 1, sw, cc)

    def spread(x, period, reps):
        sel = (jnp.arange(period)[:, None] == (jnp.arange(period * reps)[None, :] % period)).astype(BF16)
        return jnp.dot(x.astype(BF16), sel, preferred_element_type=BF16)

    same = grp_of_row == (jnp.arange(sw)[None, :] // cc)
    d_lag = jnp.where(same[None], spread(kt, cc, g), 0)

    same_in = jnp.tile(grp_of_row, (t, 1)) == ((jnp.arange(ns)[None, :] % (g * p)) // p)

    def in_to_state(pre, pim, bre, bim):
        xre = pre[..., None] * bre[None] - pim[..., None] * bim[None]
        xim = pre[..., None] * bim[None] + pim[..., None] * bre[None]
        return [spread(xre.transpose(0, 1, 3, 2).reshape(t * sw, p), p, g),
                spread(xim.transpose(0, 1, 3, 2).reshape(t * sw, p), p, g)]

    w_in = jnp.concatenate(in_to_state(pw_re[t - 1::-1, 0], pw_im[t - 1::-1, 0], bb_re[0], bb_im[0])
                           + in_to_state(pw_re[:t, 1], pw_im[:t, 1], bb_re[1], bb_im[1]), axis=1)
    w_in = jnp.where(jnp.tile(same_in, (1, 2)), w_in, 0)

    same_out = ((jnp.arange(ns)[:, None] % (g * p)) // p) == ((jnp.arange(t * sw)[None, :] // cc) % g)
    col = jnp.arange(t * sw)
    pick = (jnp.arange(t * cc)[:, None] == ((col // sw) * cc + col % cc)[None, :]).astype(BF16)

    def state_to_out(pre, pim):
        are = cre[None] * pre[:, :, None, :] - cim[None] * pim[:, :, None, :]
        aim = cre[None] * pim[:, :, None, :] + cim[None] * pre[:, :, None, :]
        a = jnp.concatenate([are.transpose(1, 3, 0, 2), -aim.transpose(1, 3, 0, 2)], axis=0)
        a = jnp.dot(a.reshape(ns, t * cc).astype(BF16), pick, preferred_element_type=BF16)
        return jnp.where(same_out, a, 0)

    w_out_f = state_to_out(pw_re[1:, 0], pw_im[1:, 0])
    w_out_b = state_to_out(pw_re[t:0:-1, 1], pw_im[t:0:-1, 1])

    decay = jnp.stack([pw_re[t].reshape(2, g * p), pw_im[t].reshape(2, g * p)], axis=1)
    return d_lag, w_in, w_out_f, w_out_b, decay


def _s5_in_kernel(u_ref, d_ref, w_ref, o_ref):
    j = pl.program_id(0)

    @pl.when(j < S5_T)
    def _():
        acc = _dot(u_ref[:, 0:S5_WIDTH], d_ref[j + S5_T - 1])
        for s in range(1, S5_T):
            acc = acc + _dot(u_ref[:, s * S5_WIDTH:(s + 1) * S5_WIDTH], d_ref[j - s + S5_T - 1])
        o_ref[...] = acc

    @pl.when(j >= S5_T)
    def _():
        o_ref[...] = _dot(u_ref[...], w_ref[...])


def _s5_in(u, d_lag, w_in, tm):
    m, k = u.shape
    tn = S5_WIDTH
    nj = (k + w_in.shape[1]) // tn
    return pl.pallas_call(
        _s5_in_kernel,
        out_shape=jax.ShapeDtypeStruct((m, nj * tn), F32),
        grid=(nj, m // tm),
        in_specs=[pl.BlockSpec((tm, k), lambda j, i: (i, 0)),
                  pl.BlockSpec(d_lag.shape, lambda j, i: (0, 0, 0)),
                  pl.BlockSpec((k, tn), lambda j, i: (0, jnp.maximum(j - S5_T, 0)))],
        out_specs=pl.BlockSpec((tm, tn), lambda j, i: (i, j)),
        compiler_params=_params("arbitrary", "arbitrary"),
        name="s5_in",
    )(u, d_lag, w_in)


def _s5_scan_kernel(efr_ref, efi_ref, ebr_ref, ebi_ref, a_ref, hfr_ref, hfi_ref, hbr_ref, hbi_ref,
                    *, nb, rows_in, rows_out):
    dirs = ((efr_ref, efi_ref, hfr_ref, hfi_ref, a_ref[0, 0:1, :], a_ref[0, 1:2, :]),
            (ebr_ref, ebi_ref, hbr_ref, hbi_ref, a_ref[1, 0:1, :], a_ref[1, 1:2, :]))
    zero = jnp.zeros_like(dirs[0][4])

    def step(srcs, carry, store):
        new = []
        for di, (er_ref, ei_ref, hr_ref, hi_ref, are, aim) in enumerate(dirs):
            for bi in range(nb):
                hre, him = carry[2 * (di * nb + bi)], carry[2 * (di * nb + bi) + 1]
                if store:
                    hr_ref[pl.ds(bi * rows_out + srcs[di], 1), :] = hre
                    hi_ref[pl.ds(bi * rows_out + srcs[di], 1), :] = him
                ere = er_ref[pl.ds(bi * rows_in + srcs[di], 1), :]
                eim = ei_ref[pl.ds(bi * rows_in + srcs[di], 1), :]
                new += [are * hre - aim * him + ere, are * him + aim * hre + eim]
        return tuple(new)

    n_ctx = rows_in - rows_out
    carry = lax.fori_loop(0, n_ctx, lambda s, c: step((rows_out + s, rows_in - 1 - s), c, False),
                          tuple([zero] * (4 * nb)))
    lax.fori_loop(0, rows_out, lambda s, c: step((s, rows_out - 1 - s), c, True), carry)


def _s5_scan(e, decay, nb, rows_in, rows_out):
    tc = 256
    nsr = S5_GROUPS * S5_STATE
    c0 = (S5_T * S5_WIDTH) // tc
    nt = nsr // tc
    eblk = lambda k: pl.BlockSpec((nb * rows_in, tc), lambda j: (0, c0 + k * nt + j))
    hblk = pl.BlockSpec((nb * rows_out, tc), lambda j: (0, j))
    return pl.pallas_call(
        functools.partial(_s5_scan_kernel, nb=nb, rows_in=rows_in, rows_out=rows_out),
        out_shape=[jax.ShapeDtypeStruct((nb * rows_out, nsr), F32)] * 4,
        grid=(nt,),
        in_specs=[eblk(0), eblk(1), eblk(2), eblk(3), pl.BlockSpec((2, 2, tc), lambda j: (0, 0, j))],
        out_specs=[hblk] * 4,
        compiler_params=_params("arbitrary"),
        name="s5_scan",
    )(e, e, e, e, decay)


def _gelu_tanh(x):
    return 0.5 * x * (1.0 + jnp.tanh(math.sqrt(2.0 / math.pi) * (x + 0.044715 * x * x * x)))


def _s5_out_kernel(hfr_ref, hfi_ref, hbr_ref, hbi_ref, wf_ref, wb_ref, yi_ref, u_ref, d_ref, wg_ref, o_ref):
    nsr = hfr_ref.shape[1]
    y = yi_ref[0] + d_ref[...] * u_ref[...].astype(F32)
    for h_ref, w_ref, r0 in ((hfr_ref, wf_ref, 0), (hfi_ref, wf_ref, nsr), (hbr_ref, wb_ref, 0), (hbi_ref, wb_ref, nsr)):
        y = y + _dot(h_ref[...].astype(BF16), w_ref[r0:r0 + nsr, :])
    y = _gelu_tanh(y)
    gate = _sigmoid(_dot(y.astype(BF16), wg_ref[...]))
    o_ref[...] = (y * gate).astype(o_ref.dtype)


def _s5_out(states, w_out_f, w_out_b, e3, u_rows, d_row, w_glu):
    m, nsr = states[0].shape
    nb = e3.shape[0]
    tm = m // nb
    tn = S5_WIDTH
    st = pl.BlockSpec((tm, nsr), lambda i, j: (i, 0))
    wo = pl.BlockSpec((2 * nsr, tn), lambda i, j: (0, j))
    return pl.pallas_call(
        _s5_out_kernel,
        out_shape=jax.ShapeDtypeStruct((m, S5_T * S5_WIDTH), BF16),
        grid=(nb, S5_T),
        in_specs=[st, st, st, st, wo, wo,
                  pl.BlockSpec((1, tm, tn), lambda i, j: (i, 0, j)),
                  pl.BlockSpec((tm, tn), lambda i, j: (i, j)),
                  pl.BlockSpec((1, tn), lambda i, j: (0, 0)),
                  pl.BlockSpec((tn, tn), lambda i, j: (0, 0))],
        out_specs=pl.BlockSpec((tm, tn), lambda i, j: (i, j)),
        compiler_params=_params("arbitrary", "arbitrary"),
        name="s5_out",
    )(*states, w_out_f, w_out_b, e3, u_rows, d_row, w_glu)


U32 = jnp.uint32
ROW_SUB = 4


def _to_token_rows(ref, val):
    t, d = val.shape

    def rounded(x):
        u = lax.bitcast_convert_type(x, U32)
        return u + (jnp.uint32(0x7FFF) + ((u >> 16) & jnp.uint32(1)))

    w = (rounded(val[:, :d // 2]) >> 16) | (rounded(val[:, d // 2:]) & jnp.uint32(0xFFFF0000))
    for s in range(ROW_SUB):
        ref[pl.ds(s, t, stride=ROW_SUB), :] = w[:, s * LANES:(s + 1) * LANES]


def _from_token_rows(ref, t, row0=0):
    w = jnp.concatenate([ref[pl.ds(row0 * ROW_SUB + s, t, stride=ROW_SUB), :] for s in range(ROW_SUB)], axis=-1)
    lo = lax.bitcast_convert_type(w << 16, F32)
    hi = lax.bitcast_convert_type(w & jnp.uint32(0xFFFF0000), F32)
    return jnp.concatenate([lo, hi], axis=-1)


def _route(h2b, wr_ref, br_ref, cnt_ref, ls8_ref, w8_ref, seg_ref):
    tm = h2b.shape[0]
    per_group = N_EXPERTS // ROUTE_GROUPS
    scores = _sigmoid(lax.dot_general(wr_ref[...], h2b, _NT, preferred_element_type=F32))
    biased = scores + br_ref[...]
    neg = -jnp.inf
    sub = lax.broadcasted_iota(jnp.int32, (per_group, tm), 0)
    grp = []
    for gi in range(ROUTE_GROUPS):
        v = biased[gi * per_group:(gi + 1) * per_group, :]
        m1 = jnp.max(v, axis=0, keepdims=True)
        first = jnp.min(jnp.where(v == m1, sub, per_group), axis=0, keepdims=True)
        m2 = jnp.max(jnp.where(sub == first, neg, v), axis=0, keepdims=True)
        grp.append(m1 + m2)
    grp = jnp.concatenate(grp, axis=0)
    gid = lax.broadcasted_iota(jnp.int32, (ROUTE_GROUPS, tm), 0)
    beaten = jnp.zeros((ROUTE_GROUPS, tm), jnp.int32)
    for gj in range(ROUTE_GROUPS):
        r = grp[gj:gj + 1, :]
        beaten = beaten + jnp.where((r > grp) | ((r == grp) & (gj < gid)), 1, 0)
    group_ok = beaten < TOPK_GROUPS
    expert_ok = jnp.concatenate(
        [jnp.broadcast_to(group_ok[gi:gi + 1, :], (per_group, tm)) for gi in range(ROUTE_GROUPS)], axis=0)
    cur = jnp.where(expert_ok, biased, neg)
    eid = lax.broadcasted_iota(jnp.int32, (N_EXPERTS, tm), 0)
    sel = jnp.zeros((N_EXPERTS, tm), F32)
    picks, wts = [], []
    for _ in range(TOP_K):
        m = jnp.max(cur, axis=0, keepdims=True)
        idx = jnp.min(jnp.where(cur == m, eid, N_EXPERTS), axis=0, keepdims=True)
        hit = eid == idx
        picks.append(idx)
        wts.append(jnp.sum(jnp.where(hit, scores, 0.0), axis=0, keepdims=True))
        sel = jnp.where(hit, 1.0, sel)
        cur = jnp.where(hit, neg, cur)
    wsum = wts[0]
    for w in wts[1:]:
        wsum = wsum + w
    selb = sel.astype(BF16)
    ti = lax.broadcasted_iota(jnp.int32, (tm, tm), 0)
    tj = lax.broadcasted_iota(jnp.int32, (tm, tm), 1)
    rank = _dot(selb, jnp.where(ti < tj, 1.0, 0.0).astype(BF16))
    ei = lax.broadcasted_iota(jnp.int32, (N_EXPERTS, N_EXPERTS), 0)
    ej = lax.broadcasted_iota(jnp.int32, (N_EXPERTS, N_EXPERTS), 1)
    seg_off = jnp.sum(_dot(jnp.where(ej < ei, 1.0, 0.0).astype(BF16), selb), axis=1, keepdims=True)
    seg_cnt = jnp.sum(sel, axis=1, keepdims=True)
    slot = seg_off + rank
    for k in range(TOP_K):
        w8_ref[k:k + 1, :] = wts[k] / wsum * ROUTED_SCALE
        ls8_ref[k:k + 1, :] = jnp.sum(jnp.where(eid == picks[k], slot, 0.0), axis=0, keepdims=True).astype(jnp.int32)
    lane = lax.broadcasted_iota(jnp.int32, (N_EXPERTS, LANES), 1)
    seg_ref[0] = jnp.where(lane == 0, cnt_ref[...], jnp.where(lane == 1, seg_cnt, seg_off))
    cnt_ref[...] = cnt_ref[...] + seg_cnt


def _merge_kernel(x_ref, ya_ref, on_ref, go_ref, ga_ref, gb_ref, g1_ref, sc_ref, sh_ref, n2_ref,
                  pa_ref, pb_ref, wo_ref, wr_ref, br_ref,
                  x1_ref, h2_ref, ls8_ref, w8_ref, seg_ref, cnt_ref, fold_a, fold_b):
    @pl.when(pl.program_id(0) == 0)
    def _():
        cnt_ref[...] = jnp.zeros_like(cnt_ref)

    go = go_ref[...].astype(F32)
    on = jnp.concatenate([on_ref[0, :, rr, :] for rr in range(on_ref.shape[2])], axis=0)
    y_b = (on.astype(F32) * (go * _sigmoid(go))).astype(BF16)
    y_a = _unfold_rows(ya_ref[...].astype(F32), fold_a, fold_b).astype(BF16)
    pa = _dot(y_a, pa_ref[...])
    pb = _dot(y_b, pb_ref[...])
    merged = _sigmoid(ga_ref[...].astype(F32)) * pa + _sigmoid(gb_ref[...].astype(F32)) * pb
    x1 = x_ref[...] + g1_ref[0] * _dot(merged.astype(BF16), wo_ref[...])
    x1_ref[...] = x1
    y = x1 * lax.rsqrt(jnp.mean(x1 * x1, axis=-1, keepdims=True) + EPS) * n2_ref[...]
    h2 = y * (1.0 + sc_ref[0]) + sh_ref[0]
    _to_token_rows(h2_ref, h2)
    _route(h2.astype(BF16), wr_ref, br_ref, cnt_ref, ls8_ref, w8_ref, seg_ref)


def _merge(x2d, ya, on, go, ga, gb, g1, sc2, sh2, n2g, pa, pb, wo, wr_t, br, rows_per_batch, tm):
    n, d = x2d.shape
    per = rows_per_batch // tm
    row = lambda wd: pl.BlockSpec((tm, wd), lambda i: (i, 0))
    mod = pl.BlockSpec((1, 1, d), lambda i: (i // per, 0, 0))
    full = lambda a: pl.BlockSpec(a.shape, lambda i: (0, 0))
    tok = pl.BlockSpec((TOP_K, tm), lambda i: (0, i))
    return pl.pallas_call(
        _merge_kernel,
        out_shape=[jax.ShapeDtypeStruct((n, d), F32), jax.ShapeDtypeStruct((n * ROW_SUB, LANES), U32),
                   jax.ShapeDtypeStruct((TOP_K, n), jnp.int32), jax.ShapeDtypeStruct((TOP_K, n), F32),
                   jax.ShapeDtypeStruct((n // tm, N_EXPERTS, LANES), F32),
                   jax.ShapeDtypeStruct((N_EXPERTS, 1), F32)],
        grid=(n // tm,),
        in_specs=[row(d), pl.BlockSpec((tm // S5_T, S5_T * S5_WIDTH), lambda i: (i, 0)),
                  pl.BlockSpec((1, GRID_W, tm // GRID_W, HG_WIDTH), lambda i: (i // per, 0, i % per, 0)),
                  row(HG_WIDTH), row(d), row(d), mod, mod, mod,
                  pl.BlockSpec((1, d), lambda i: (0, 0)), full(pa), full(pb), full(wo), full(wr_t), full(br)],
        out_specs=[row(d), pl.BlockSpec((tm * ROW_SUB, LANES), lambda i: (i, 0)), tok, tok,
                   pl.BlockSpec((1, N_EXPERTS, LANES), lambda i: (i, 0, 0)),
                   pl.BlockSpec((N_EXPERTS, 1), lambda i: (0, 0))],
        scratch_shapes=[pltpu.VMEM((tm, LANES), F32), pltpu.VMEM((tm, LANES), F32)],
        compiler_params=_params("arbitrary"),
        name="merge_out_proj_route",
    )(x2d, ya, on, go, ga, gb, g1, sc2, sh2, n2g.reshape(1, d), pa, pb, wo, wr_t, br)


MOE_TILE = TOK_TILE
MOE_BLK = 1024


def _token_row(ref, r):
    return ref.at[pl.ds(pl.multiple_of(r * ROW_SUB, ROW_SUB), ROW_SUB)]


def _wait_rows(any_ref, sem, n_rows):
    view = any_ref.at[pl.ds(0, n_rows * ROW_SUB)]
    pltpu.make_async_copy(view, view, sem).wait()


def _rows(ref, r0, n):
    return ref.at[pl.ds(pl.multiple_of(r0 * ROW_SUB, ROW_SUB), n * ROW_SUB)]


def _pow2_pieces(n, max_piece, fn):
    done = 0
    piece = max_piece
    while piece >= 1:
        hit = (n & piece) != 0
        pl.when(hit)(functools.partial(fn, done, piece))
        done = done + (n & piece)
        piece //= 2


def _copy_rows(src_ref, src0, dst_ref, dst0, n, max_piece, sem):
    def piece(off, size):
        pltpu.make_async_copy(_rows(src_ref, src0 + off, size), _rows(dst_ref, dst0 + off, size), sem).start()
    _pow2_pieces(n, max_piece, piece)


def _wait_copied_rows(src_ref, dst_ref, n, max_piece, sem):
    def piece(off, size):
        pltpu.make_async_copy(_rows(src_ref, 0, size), _rows(dst_ref, 0, size), sem).wait()
    _pow2_pieces(n, max_piece, piece)


def _dispatch_kernel(gs_ref, cnt_ref, off_ref, pad_ref, ls_ref, h2_ref, xs_hbm, ls_smem, stage0, stage1, zbuf,
                     sem0, sem1, lsem, zsem, *, tm, n_blocks):
    i = pl.program_id(0)
    last = pl.num_programs(0) - 1
    cp = pltpu.make_async_copy(ls_ref, ls_smem, lsem)
    cp.start()
    cp.wait()

    def tile(stage, sem, prev_sem):
        def body(t, carry):
            row = h2_ref[pl.ds(pl.multiple_of(t * ROW_SUB, ROW_SUB), ROW_SUB), :]
            for k in range(TOP_K):
                slot = ls_smem[t * TOP_K + k]
                stage[pl.ds(pl.multiple_of(slot * ROW_SUB, ROW_SUB), ROW_SUB), :] = row
            return carry

        lax.fori_loop(0, tm, body, 0)

        def per_expert(e, carry):
            _copy_rows(stage, off_ref[i, e], xs_hbm, gs_ref[i, e], cnt_ref[i, e], tm, sem)
            return carry

        lax.fori_loop(0, N_EXPERTS, per_expert, 0)

        @pl.when(i > 0)
        def _():
            _wait_rows(xs_hbm, prev_sem, tm * TOP_K)

        @pl.when(i == last)
        def _():
            _wait_rows(xs_hbm, sem, tm * TOP_K)

    pl.when(i % 2 == 0)(functools.partial(tile, stage0, sem0, sem1))
    pl.when(i % 2 == 1)(functools.partial(tile, stage1, sem1, sem0))

    @pl.when(i == 0)
    def _():
        zbuf[...] = jnp.zeros_like(zbuf)

        def start(e, carry):
            _copy_rows(zbuf, 0, xs_hbm, pad_ref[0, e], pad_ref[1, e], MOE_BLK // 2, zsem)
            return carry

        def wait(e, carry):
            _wait_copied_rows(zbuf, xs_hbm, pad_ref[1, e], MOE_BLK // 2, zsem)
            return carry

        lax.fori_loop(0, N_EXPERTS, start, 0)
        lax.fori_loop(0, N_EXPERTS, wait, 0)

        def zero_block(j, carry):
            pltpu.make_async_copy(zbuf, _rows(xs_hbm, j * MOE_BLK, MOE_BLK), zsem).start()
            return carry

        def wait_block(j, carry):
            pltpu.make_async_copy(zbuf, _rows(xs_hbm, 0, MOE_BLK), zsem).wait()
            return carry

        lax.fori_loop(pad_ref[2, 0], n_blocks, zero_block, 0)
        lax.fori_loop(pad_ref[2, 0], n_blocks, wait_block, 0)


def _dispatch(gstart, seg_cnt, seg_off, pad, ls8, h2_rows, n_blocks, tm):
    n = ls8.shape[0] // TOP_K
    cap = n_blocks * MOE_BLK
    return pl.pallas_call(
        functools.partial(_dispatch_kernel, tm=tm, n_blocks=n_blocks),
        out_shape=jax.ShapeDtypeStruct((cap * ROW_SUB, LANES), U32),
        grid_spec=pltpu.PrefetchScalarGridSpec(
            num_scalar_prefetch=4,
            grid=(n // tm,),
            in_specs=[pl.BlockSpec((tm * TOP_K,), lambda i, *_: (i,)),
                      pl.BlockSpec((tm * ROW_SUB, LANES), lambda i, *_: (i, 0))],
            out_specs=pl.BlockSpec(memory_space=pl.ANY),
            scratch_shapes=[pltpu.SMEM((tm * TOP_K,), jnp.int32),
                            pltpu.VMEM((TOP_K * tm * ROW_SUB, LANES), U32),
                            pltpu.VMEM((TOP_K * tm * ROW_SUB, LANES), U32),
                            pltpu.VMEM((MOE_BLK * ROW_SUB, LANES), U32),
                            pltpu.SemaphoreType.DMA, pltpu.SemaphoreType.DMA, pltpu.SemaphoreType.DMA,
                            pltpu.SemaphoreType.DMA]),
        compiler_params=pltpu.CompilerParams(dimension_semantics=("arbitrary",), vmem_limit_bytes=VMEM_LIMIT,
                                             has_side_effects=True),
        name="moe_dispatch",
    )(gstart, seg_cnt, seg_off, pad, ls8, h2_rows)


def _expert_kernel(be_ref, nu_ref, x_ref, w1_ref, w3_ref, w2_ref, o_ref, w1b, w3b, w2b):
    j = pl.program_id(0)
    e = be_ref[j]
    prev = be_ref[jnp.maximum(j - 1, 0)]
    used = j < nu_ref[0]

    @pl.when(jnp.logical_and(used, jnp.logical_or(j == 0, e != prev)))
    def _():
        w1b[...] = w1_ref[0].astype(BF16)
        w3b[...] = w3_ref[0].astype(BF16)
        w2b[...] = w2_ref[0].astype(BF16)

    @pl.when(used)
    def _():
        x = _from_token_rows(x_ref, MOE_BLK).astype(BF16)
        a = _dot(x, w1b[...])
        hid = (a * _sigmoid(a)) * _dot(x, w3b[...])
        _to_token_rows(o_ref, _dot(hid.astype(BF16), w2b[...]))

    @pl.when(jnp.logical_not(used))
    def _():
        o_ref[...] = jnp.zeros_like(o_ref)


def _experts(block_e, n_used, xs, w1, w3, w2):
    n_blocks = xs.shape[0] // (MOE_BLK * ROW_SUB)
    d, f = w1.shape[1], w1.shape[2]
    rows = pl.BlockSpec((MOE_BLK * ROW_SUB, LANES), lambda j, be, nu: (j, 0))
    return pl.pallas_call(
        _expert_kernel,
        out_shape=jax.ShapeDtypeStruct(xs.shape, U32),
        grid_spec=pltpu.PrefetchScalarGridSpec(
            num_scalar_prefetch=2,
            grid=(n_blocks,),
            in_specs=[rows,
                      pl.BlockSpec((1, d, f), lambda j, be, nu: (be[j], 0, 0)),
                      pl.BlockSpec((1, d, f), lambda j, be, nu: (be[j], 0, 0)),
                      pl.BlockSpec((1, f, d), lambda j, be, nu: (be[j], 0, 0))],
            out_specs=rows,
            scratch_shapes=[pltpu.VMEM((d, f), BF16), pltpu.VMEM((d, f), BF16), pltpu.VMEM((f, d), BF16)]),
        compiler_params=_params("arbitrary"),
        name="moe_experts",
    )(block_e, n_used, xs, w1, w3, w2)


def _combine_kernel(gs_ref, cnt_ref, off_ref, ls_ref, w8_ref, x1_ref, h2_ref, g2_ref, ws1_ref, ws3_ref, ws2_ref,
                    fg_ref, ys_hbm, o_ref, ls_smem, w_smem, gbuf0, gbuf1, acc_rows, sem0, sem1, lsem, *, tm):
    i = pl.program_id(0)
    last = pl.num_programs(0) - 1
    cp1 = pltpu.make_async_copy(ls_ref, ls_smem, lsem)
    cp2 = pltpu.make_async_copy(w8_ref, w_smem, lsem)
    cp1.start()
    cp2.start()

    def fetch(tile, gbuf, sem):
        def per_expert(e, carry):
            _copy_rows(ys_hbm, gs_ref[tile, e], gbuf, off_ref[tile, e], cnt_ref[tile, e], tm, sem)
            return carry

        lax.fori_loop(0, N_EXPERTS, per_expert, 0)

    @pl.when(i == 0)
    def _():
        fetch(0, gbuf0, sem0)

    @pl.when(jnp.logical_and(i < last, i % 2 == 0))
    def _():
        fetch(i + 1, gbuf1, sem1)

    @pl.when(jnp.logical_and(i < last, i % 2 == 1))
    def _():
        fetch(i + 1, gbuf0, sem0)

    h2 = _from_token_rows(h2_ref, tm).astype(BF16)
    a = _dot(h2, ws1_ref[...])
    hid = (a * _sigmoid(a)) * _dot(h2, ws3_ref[...])
    acc = _dot(hid.astype(BF16), ws2_ref[...])
    cp1.wait()
    cp2.wait()

    def reduce_rows(gbuf, sem):
        _wait_rows(gbuf, sem, tm * TOP_K)

        def body(t, carry):
            lo = jnp.zeros((ROW_SUB, LANES), F32)
            hi = jnp.zeros((ROW_SUB, LANES), F32)
            for k in range(TOP_K):
                w = w_smem[t * TOP_K + k]
                words = gbuf[pl.ds(pl.multiple_of(ls_smem[t * TOP_K + k] * ROW_SUB, ROW_SUB), ROW_SUB), :]
                lo = lo + w * lax.bitcast_convert_type(words << 16, F32)
                hi = hi + w * lax.bitcast_convert_type(words & jnp.uint32(0xFFFF0000), F32)
            acc_rows[pl.ds(pl.multiple_of(t * SUBLANES, SUBLANES), ROW_SUB), :] = lo
            acc_rows[pl.ds(pl.multiple_of(t * SUBLANES, SUBLANES) + ROW_SUB, ROW_SUB), :] = hi
            return carry

        lax.fori_loop(0, tm, body, 0)

    pl.when(i % 2 == 0)(functools.partial(reduce_rows, gbuf0, sem0))
    pl.when(i % 2 == 1)(functools.partial(reduce_rows, gbuf1, sem1))
    routed = jnp.concatenate([acc_rows[pl.ds(s, tm, stride=SUBLANES), :] for s in range(SUBLANES)], axis=-1)
    y = x1_ref[...] + g2_ref[0] * (acc + routed)
    o_ref[...] = y * lax.rsqrt(jnp.mean(y * y, axis=-1, keepdims=True) + EPS) * fg_ref[...]


def _combine(gstart, seg_cnt, seg_off, ls8, w8, x1, h2_rows, g2, ws1, ws3, ws2, fg, ys, rows_per_batch, tm):
    n, d = x1.shape
    per = rows_per_batch // tm
    tok = pl.BlockSpec((tm * TOP_K,), lambda i, *_: (i,))
    full = lambda a: pl.BlockSpec(a.shape, lambda i, *_: (0, 0))
    return pl.pallas_call(
        functools.partial(_combine_kernel, tm=tm),
        out_shape=jax.ShapeDtypeStruct((n, d), F32),
        grid_spec=pltpu.PrefetchScalarGridSpec(
            num_scalar_prefetch=3,
            grid=(n // tm,),
            in_specs=[tok, tok, pl.BlockSpec((tm, d), lambda i, *_: (i, 0)),
                      pl.BlockSpec((tm * ROW_SUB, LANES), lambda i, *_: (i, 0)),
                      pl.BlockSpec((1, 1, d), lambda i, *_: (i // per, 0, 0)),
                      full(ws1), full(ws3), full(ws2), pl.BlockSpec((1, d), lambda i, *_: (0, 0)),
                      pl.BlockSpec(memory_space=pl.ANY)],
            out_specs=pl.BlockSpec((tm, d), lambda i, *_: (i, 0)),
            scratch_shapes=[pltpu.SMEM((tm * TOP_K,), jnp.int32), pltpu.SMEM((tm * TOP_K,), F32),
                            pltpu.VMEM((TOP_K * tm * ROW_SUB, LANES), U32),
                            pltpu.VMEM((TOP_K * tm * ROW_SUB, LANES), U32),
                            pltpu.VMEM((tm * SUBLANES, LANES), F32), pltpu.SemaphoreType.DMA,
                            pltpu.SemaphoreType.DMA, pltpu.SemaphoreType.DMA]),
        compiler_params=_params("arbitrary"),
        name="moe_combine_final",
    )(gstart, seg_cnt, seg_off, ls8, w8, x1, h2_rows, g2, ws1, ws3, ws2, fg.reshape(1, d), ys)


def _moe_plan(seg, counts, n_assign):
    cnt = counts.reshape(N_EXPERTS).astype(jnp.int32)
    padded = (cnt + MOE_BLK - 1) // MOE_BLK * MOE_BLK
    pends = jnp.cumsum(padded)
    pstarts = pends - padded
    n_blocks = (n_assign + N_EXPERTS * (MOE_BLK - 1) + MOE_BLK - 1) // MOE_BLK
    seg = seg[:, :, :3].astype(jnp.int32)
    gstart = pstarts[None, :] + seg[:, :, 0]
    blk_start = jnp.arange(n_blocks, dtype=jnp.int32) * MOE_BLK
    block_e = jnp.minimum(jnp.sum((blk_start[:, None] >= pends[None, :]).astype(jnp.int32), axis=1),
                          N_EXPERTS - 1).astype(jnp.int32)
    n_used = (pends[-1:] // MOE_BLK).astype(jnp.int32)
    pad = jnp.stack([pstarts + cnt, padded - cnt, jnp.broadcast_to(n_used, (N_EXPERTS,))], axis=0).astype(jnp.int32)
    return gstart, seg[:, :, 1], seg[:, :, 2], pad, block_e, n_used, n_blocks


def _mixer(x, c, ctx, c_ctx, w_ada, b_ada, norm1_g, norm2_g, w_in, s5_lam_re, s5_lam_im, s5_log_dt,
           s5_b_re, s5_b_im, s5_c_re, s5_c_im, s5_d, s5_w_glu, lb, hg_norm_g, p_a, p_b, w_out,
           moe_w_router, moe_b_router):
    b, l, d = x.shape
    lc = ctx.shape[1]
    n = b * l
    rows = l // GRID_W

    c8 = jnp.concatenate([c, c_ctx[None], jnp.zeros((8 - b - 1, d), F32)], axis=0)
    mod = _ada(c8, w_ada, b_ada)
    sh1, sc1, g1, sh2, sc2, g2 = [mod[:b, k * d:(k + 1) * d].reshape(b, 1, d) for k in range(6)]
    csh1, csc1 = mod[b:b + 1, 0:d].reshape(1, 1, d), mod[b:b + 1, d:2 * d].reshape(1, 1, d)

    w_in_b = w_in.astype(BF16)
    z = dict(zip([p[0] for p in _IN_PIECES],
                 _inproj(x.reshape(n, d), sc1, sh1, norm1_g, w_in_b, l, TOK_TILE, True)))
    zc = dict(zip([p[0] for p in _IN_PIECES],
                  _inproj(ctx.reshape(b * lc, d), csc1, csh1, norm1_g, w_in_b, lc, lc, False)))

    cx = lambda t: t.reshape(b, lc, HG_WIDTH)
    lb_row = lb.reshape(1, HG_WIDTH)
    o_f = _hgrn_pass(z["q"], z["ff"], z["i"], cx(zc["ff"]), cx(zc["i"]), lb_row, None, None, reverse=False)
    o_n = _hgrn_pass(z["q"], z["fb"], z["i"], cx(zc["fb"]), cx(zc["i"]), lb_row, o_f,
                     hg_norm_g.reshape(1, HG_DK), reverse=True)

    d_lag, w_s5_in, w_out_f, w_out_b, decay = _s5_weights(s5_lam_re, s5_lam_im, s5_log_dt, s5_b_re, s5_b_im,
                                                          s5_c_re, s5_c_im)
    kc, kl = lc // S5_T, l // S5_T
    u_lat = z["u"].reshape(b, kl, S5_T * S5_WIDTH)
    u_ctx = zc["u"].reshape(b, kc, S5_T * S5_WIDTH)
    rows_in = kl + kc
    u_ext = jnp.concatenate([u_lat, u_ctx], axis=1).reshape(b * rows_in, S5_T * S5_WIDTH)
    e = _s5_in(u_ext, d_lag, w_s5_in, (b * rows_in) // 2)
    states = _s5_scan(e, decay, b, rows_in, kl)
    d_row = s5_d.astype(F32).reshape(1, S5_WIDTH)
    y_a = _s5_out(states, w_out_f, w_out_b, e.reshape(b, rows_in, -1), z["u"], d_row, s5_w_glu.astype(BF16))

    return _merge(x.reshape(n, d), y_a, o_n, z["go"], z["ga"], z["gb"], g1, sc2, sh2, norm2_g,
                  p_a.astype(BF16), p_b.astype(BF16), w_out.astype(BF16),
                  moe_w_router.T.astype(BF16), moe_b_router.astype(F32).reshape(N_EXPERTS, 1), l, MOE_TILE) + (g2,)


def kernel(x, c, ctx, c_ctx, w_ada, b_ada, norm1_g, norm2_g, w_in, s5_lam_re, s5_lam_im, s5_log_dt, s5_b_re,
           s5_b_im, s5_c_re, s5_c_im, s5_d, s5_w_glu, hg_lb_logits, hg_norm_g, p_a, p_b, w_out, moe_w_router,
           moe_b_router, moe_w1, moe_w3, moe_w2, moe_ws1, moe_ws3, moe_ws2, final_norm_g):
    b, l, d = x.shape
    n = b * l
    assert w_ada.shape[0] == 1, "single-layer block"
    lb = jnp.cumsum(jax.nn.softmax(hg_lb_logits.astype(F32), axis=0), axis=0)[0]
    x1, h2_rows, ls8, w8, seg, counts, g2 = _mixer(
        x, c, ctx, c_ctx, w_ada[0], b_ada[0], norm1_g[0], norm2_g[0], w_in[0], s5_lam_re[0], s5_lam_im[0],
        s5_log_dt[0], s5_b_re[0], s5_b_im[0], s5_c_re[0], s5_c_im[0], s5_d[0], s5_w_glu[0], lb, hg_norm_g[0],
        p_a[0], p_b[0], w_out[0], moe_w_router[0], moe_b_router[0])
    gstart, seg_cnt, seg_off, pad, block_e, n_used, n_blocks = _moe_plan(seg, counts, n * TOP_K)
    ls_flat, w_flat = ls8.T.reshape(n * TOP_K), w8.T.reshape(n * TOP_K)
    xs = _dispatch(gstart, seg_cnt, seg_off, pad, ls_flat, h2_rows, n_blocks, MOE_TILE)
    ys = _experts(block_e, n_used, xs, moe_w1[0], moe_w3[0], moe_w2[0])
    out = _combine(gstart, seg_cnt, seg_off, ls_flat, w_flat, x1, h2_rows, g2, moe_ws1[0].astype(BF16),
                   moe_ws3[0].astype(BF16), moe_ws2[0].astype(BF16), final_norm_g, ys, l, MOE_TILE)
    return out.reshape(b, l, d)
```

```python
import functools
import math

import jax
import jax.numpy as jnp
from jax import lax
from jax.experimental import pallas as pl
from jax.experimental.pallas import tpu as pltpu

F32 = jnp.float32
BF16 = jnp.bfloat16

GRID_W = 64
S5_WIDTH = 256
S5_GROUP = 16
S5_GROUPS = 16
S5_STATE = 64
HG_HEADS = 6
HG_DK = 128
HG_WIDTH = HG_HEADS * HG_DK
N_EXPERTS = 64
ROUTE_GROUPS = 8
TOPK_GROUPS = 4
TOP_K = 8
ROUTED_SCALE = 2.5
EPS = 1e-6

LANES = 128
SUBLANES = 8

TOK_TILE = 512
S5_T = 16
HG_CHUNK = 64
HG_BATCH = 2
VMEM_LIMIT = 56 * 1024 * 1024

_NT = (((1,), (1,)), ((), ()))
_TN = (((0,), (0,)), ((), ()))


def _params(*sem):
    return pltpu.CompilerParams(dimension_semantics=sem, vmem_limit_bytes=VMEM_LIMIT)


def _dot(a, b):
    return jnp.dot(a, b, preferred_element_type=F32)


def _sigmoid(x):
    return 1.0 / (1.0 + jnp.exp(-x))


def _ada_kernel(c_ref, w_ref, b_ref, o_ref):
    c = c_ref[...]
    s = (c * _sigmoid(c)).astype(BF16)
    o_ref[...] = _dot(s, w_ref[...].astype(BF16)) + b_ref[...]


def _ada(c8, w_ada, b_ada):
    d, n = w_ada.shape
    tn = 1536
    return pl.pallas_call(
        _ada_kernel,
        out_shape=jax.ShapeDtypeStruct((8, n), F32),
        grid=(n // tn,),
        in_specs=[pl.BlockSpec((8, d), lambda j: (0, 0)),
                  pl.BlockSpec((d, tn), lambda j: (0, j)),
                  pl.BlockSpec((1, tn), lambda j: (0, j))],
        out_specs=pl.BlockSpec((8, tn), lambda j: (0, j)),
        compiler_params=_params("arbitrary"),
        name="ada_mod",
    )(c8, w_ada, b_ada.reshape(1, n))


_IN_PIECES = (("u", 0, 256, BF16), ("q", 256, 768, BF16), ("ff", 1024, 768, F32),
              ("fb", 1792, 768, F32), ("i", 2560, 768, BF16), ("go", 3328, 768, BF16),
              ("ga", 4096, 1024, BF16), ("gb", 5120, 1024, BF16))


def _fold_rows(val, buf_a, buf_b):
    t = val.shape[0]
    buf_a[...] = val[:, :LANES]
    buf_b[...] = val[:, LANES:]
    pieces = []
    for s in range(S5_T):
        pieces += [buf_a[pl.ds(s, t // S5_T, stride=S5_T), :], buf_b[pl.ds(s, t // S5_T, stride=S5_T), :]]
    return jnp.concatenate(pieces, axis=-1)


def _unfold_rows(val, buf_a, buf_b):
    r = val.shape[0]
    for s in range(S5_T):
        buf_a[pl.ds(s, r, stride=S5_T), :] = val[:, s * S5_WIDTH:s * S5_WIDTH + LANES]
        buf_b[pl.ds(s, r, stride=S5_T), :] = val[:, s * S5_WIDTH + LANES:(s + 1) * S5_WIDTH]
    return jnp.concatenate([buf_a[...], buf_b[...]], axis=-1)


def _grid_transpose_matrix(tm):
    i = jnp.arange(tm)
    src = (i % (tm // GRID_W)) * GRID_W + i // (tm // GRID_W)
    return (src[:, None] == jnp.arange(tm)[None, :]).astype(BF16)


def _inproj_kernel(x_ref, sc_ref, sh_ref, g_ref, w_ref, p_ref, *o_refs):
    o_refs, (fold_a, fold_b) = o_refs[:len(_IN_PIECES)], o_refs[len(_IN_PIECES):]
    x = x_ref[...]
    y = x * lax.rsqrt(jnp.mean(x * x, axis=-1, keepdims=True) + EPS) * g_ref[...]
    h = (y * (1.0 + sc_ref[0]) + sh_ref[0]).astype(BF16)
    h_cm = None
    for (name, a, wd, _), o_ref in zip(_IN_PIECES, o_refs):
        if name == "u":
            o_ref[...] = _fold_rows(_dot(h, w_ref[:, a:a + wd]), fold_a, fold_b).astype(o_ref.dtype)
        elif len(o_ref.shape) == 2:
            o_ref[...] = _dot(h, w_ref[:, a:a + wd]).astype(o_ref.dtype)
        else:
            if h_cm is None:
                h_cm = _dot(p_ref[...], h).astype(BF16)
            o_ref[0] = _dot(h_cm, w_ref[:, a:a + wd]).astype(o_ref.dtype).reshape(o_ref.shape[1:])


_COLMAJOR_PIECES = ("q", "ff", "fb", "i")


def _inproj(x2d, sc, sh, g, w_bf16, rows_per_mod, tm, colmajor):
    n, d = x2d.shape
    per = rows_per_mod // tm
    mod_map = (lambda i: (i // per, 0, 0)) if sc.shape[0] > 1 else (lambda i: (0, 0, 0))
    shapes, specs = [], []
    for name, _, wd, dt in _IN_PIECES:
        if colmajor and name in _COLMAJOR_PIECES:
            shapes.append(jax.ShapeDtypeStruct((n // rows_per_mod, GRID_W, rows_per_mod // GRID_W, wd), dt))
            specs.append(pl.BlockSpec((1, GRID_W, tm // GRID_W, wd), lambda i: (i // per, 0, i % per, 0)))
        elif name == "u":
            shapes.append(jax.ShapeDtypeStruct((n // S5_T, S5_T * wd), dt))
            specs.append(pl.BlockSpec((tm // S5_T, S5_T * wd), lambda i: (i, 0)))
        else:
            shapes.append(jax.ShapeDtypeStruct((n, wd), dt))
            specs.append(pl.BlockSpec((tm, wd), lambda i: (i, 0)))
    return pl.pallas_call(
        _inproj_kernel,
        out_shape=shapes,
        grid=(n // tm,),
        in_specs=[pl.BlockSpec((tm, d), lambda i: (i, 0)),
                  pl.BlockSpec((1, 1, d), mod_map),
                  pl.BlockSpec((1, 1, d), mod_map),
                  pl.BlockSpec((1, d), lambda i: (0, 0)),
                  pl.BlockSpec(w_bf16.shape, lambda i: (0, 0)),
                  pl.BlockSpec((tm, tm), lambda i: (0, 0))],
        out_specs=specs,
        scratch_shapes=[pltpu.VMEM((tm, LANES), F32), pltpu.VMEM((tm, LANES), F32)],
        compiler_params=_params("arbitrary"),
        name="in_proj",
    )(x2d, sc, sh, g.reshape(1, d), w_bf16, _grid_transpose_matrix(tm))


def _hgrn_gates(zf, lb):
    sig = _sigmoid(zf)
    logf = jnp.log(lb + (1.0 - lb) * sig)
    k = (1.0 - lb) * (1.0 - sig)
    return logf, k


def _chunk_cumsum(cs, logf):
    hi = logf.astype(BF16)
    lo = (logf - hi.astype(F32)).astype(BF16)
    return _dot(cs, hi) + _dot(cs, lo)


def _hgrn_state_step(zf, v, lb, st, cs, reverse):
    logf, k = _hgrn_gates(zf, lb)
    cum = _chunk_cumsum(cs, logf)
    t = 0 if reverse else HG_CHUNK - 1
    total = cum[t:t + 1, :]
    kdec = (k * jnp.exp(total - cum)).astype(BF16)
    st_new = st * jnp.exp(total) + lax.dot_general(v.astype(BF16), kdec, _TN, preferred_element_type=F32)
    return cum, k, st_new


def _hgrn_kernel(*refs, reverse, final, n_ctx_chunks):
    if final:
        q_all, f_all, v_all, cf_ref, cv_ref, lb_ref, of_all, g_ref, o_all, st_ref = refs
    else:
        q_all, f_all, v_all, cf_ref, cv_ref, lb_ref, o_all, st_ref = refs
        of_all = None
    n_batch = q_all.shape[0]
    c_len = HG_CHUNK
    n_rows = q_all.shape[2]
    n_chunks = n_rows // c_len
    row = lax.broadcasted_iota(jnp.int32, (n_rows, n_rows), 0)
    col = lax.broadcasted_iota(jnp.int32, (n_rows, n_rows), 1)
    tri = (col >= row) if reverse else (col <= row)
    same_chunk = None
    for c in range(n_chunks):
        lo, hi = c * c_len, (c + 1) * c_len
        blk = (row >= lo) & (row < hi) & (col >= lo) & (col < hi)
        same_chunk = blk if same_chunk is None else (same_chunk | blk)
    mask = tri & same_chunk
    cs = jnp.where(mask, 1.0, 0.0).astype(BF16)

    @pl.when(pl.program_id(1) == 0)
    def _():
        cs1 = cs[:c_len, :c_len]
        order = range(n_ctx_chunks - 1, -1, -1) if reverse else range(n_ctx_chunks)
        for bi in range(n_batch):
            for h in range(HG_HEADS):
                cols = slice(h * HG_DK, (h + 1) * HG_DK)
                st = jnp.zeros((HG_DK, HG_DK), F32)
                for c in order:
                    rows = slice(c * c_len, (c + 1) * c_len)
                    _, _, st = _hgrn_state_step(cf_ref[bi, rows, cols], cv_ref[bi, rows, cols].astype(F32),
                                                lb_ref[:, cols], st, cs1, reverse)
                st_ref[bi * HG_HEADS + h] = st

    def per_chunk_rows(x, r):
        return jnp.concatenate([jnp.broadcast_to(x[c * c_len + r:c * c_len + r + 1, :], (c_len, x.shape[1]))
                                for c in range(n_chunks)], axis=0)

    lb = lb_ref[...]
    r_ref = c_len // 2 - 1 if reverse else c_len // 2
    r_tot = 0 if reverse else c_len - 1
    order = range(n_chunks - 1, -1, -1) if reverse else range(n_chunks)
    for bi in range(n_batch):
        q = q_all[bi, 0].astype(F32)
        v = v_all[bi, 0]
        logf, k = _hgrn_gates(f_all[bi, 0], lb)
        cum = _chunk_cumsum(cs, logf)
        ref = per_chunk_rows(cum, r_ref)
        qe = q * jnp.exp(cum - ref)
        ke = k * jnp.exp(ref - cum)
        qi, ki = qe.astype(BF16), ke.astype(BF16)
        q_in = (qe * jnp.exp(ref)).astype(BF16)
        tail = jnp.exp(per_chunk_rows(cum, r_tot) - ref)
        kdec = (ke * tail).astype(BF16)
        for h in range(HG_HEADS):
            cols = slice(h * HG_DK, (h + 1) * HG_DK)
            s = lax.dot_general(qi[:, cols], ki[:, cols], _NT, preferred_element_type=F32)
            o_intra = _dot(jnp.where(mask, s, 0.0).astype(BF16), v[:, cols])
            st = st_ref[bi * HG_HEADS + h]
            for c in order:
                rows = slice(c * c_len, (c + 1) * c_len)
                o = o_intra[rows] + lax.dot_general(q_in[rows, cols], st.astype(BF16), _NT,
                                                    preferred_element_type=F32)
                total = cum[c * c_len + r_tot:c * c_len + r_tot + 1, cols]
                st = st * jnp.exp(total) + lax.dot_general(v[rows, cols], kdec[rows, cols], _TN,
                                                           preferred_element_type=F32)
                if final:
                    o = o + of_all[bi, 0, rows, cols]
                    o = o * lax.rsqrt(jnp.mean(o * o, axis=-1, keepdims=True) + EPS) * g_ref[...]
                o_all[bi, 0, rows, cols] = o.astype(o_all.dtype)
            st_ref[bi * HG_HEADS + h] = st


def _hgrn_pass(q, f, v, cf, cv, lb, o_prev, g, *, reverse):
    b, nw, rows, _ = q.shape
    nb = HG_BATCH if b % HG_BATCH == 0 else 1
    final = o_prev is not None
    wmap = (lambda bi, w: (bi, nw - 1 - w, 0, 0)) if reverse else (lambda bi, w: (bi, w, 0, 0))
    blk = pl.BlockSpec((nb, 1, rows, HG_WIDTH), wmap)
    cblk = pl.BlockSpec((nb, cf.shape[1], HG_WIDTH), lambda bi, w: (bi, 0, 0))
    in_specs = [blk, blk, blk, cblk, cblk, pl.BlockSpec((1, HG_WIDTH), lambda bi, w: (0, 0))]
    args = [q, f, v, cf, cv, lb]
    if final:
        in_specs += [blk, pl.BlockSpec((1, HG_DK), lambda bi, w: (0, 0))]
        args += [o_prev, g]
    return pl.pallas_call(
        functools.partial(_hgrn_kernel, reverse=reverse, final=final, n_ctx_chunks=cf.shape[1] // HG_CHUNK),
        out_shape=jax.ShapeDtypeStruct(q.shape, BF16 if final else F32),
        grid=(b // nb, nw),
        in_specs=in_specs,
        out_specs=blk,
        scratch_shapes=[pltpu.VMEM((nb * HG_HEADS, HG_DK, HG_DK), F32)],
        compiler_params=_params("arbitrary", "arbitrary"),
        name="hgrn_bwd" if reverse else "hgrn_fwd",
    )(*args)


def _s5_weights(lam_re, lam_im, log_dt, b_re, b_im, c_re, c_im):
    hp = lax.Precision.HIGHEST
    g, p, cc, t = S5_GROUPS, S5_STATE, S5_GROUP, S5_T
    lre = jnp.minimum(lam_re.astype(F32), -1e-4)
    lim = lam_im.astype(F32)
    dt = jnp.exp(log_dt.astype(F32))[..., None]
    ks = jnp.arange(t + 1, dtype=F32)[:, None, None, None]
    mag = jnp.exp(ks * (lre * dt)[None])
    pw_re = mag * jnp.cos(ks * (lim * dt)[None])
    pw_im = mag * jnp.sin(ks * (lim * dt)[None])
    nr, ni = pw_re[1] - 1.0, pw_im[1]
    den = lre * lre + lim * lim
    cf_re = (nr * lre + ni * lim) / den
    cf_im = (ni * lre - nr * lim) / den
    bb_re = cf_re[..., None] * b_re - cf_im[..., None] * b_im
    bb_im = cf_re[..., None] * b_im + cf_im[..., None] * b_re
    cre, cim = c_re.astype(F32), c_im.astype(F32)
    sw, ns = S5_WIDTH, 2 * g * p
    grp_of_row = jnp.arange(sw)[:, None] // cc

    cp_re = cre[None, None, :, :, :] * pw_re[:t, :, :, None, :] - cim[None, None] * pw_im[:t, :, :, None, :]
    cp_im = cre[None, None, :, :, :] * pw_im[:t, :, :, None, :] + cim[None, None] * pw_re[:t, :, :, None, :]
    def contract_p(cp, bb):
        return jnp.sum(cp.transpose(4, 1, 0, 2, 3)[..., None] * bb.transpose(2, 0, 1, 3)[:, :, None, :, None, :],
                       axis=0)

    kk = contract_p(cp_re, bb_re) - contract_p(cp_im, bb_im)
    kf, kb = kk[0], kk[1]
    kall = jnp.concatenate([kb[:0:-1], (kf[0] + kb[0])[None], kf[1:]], axis=0)
    kt = kall.transpose(0, 1, 3, 2).reshape(2 * t - 1, sw, cc)

    def spread(x, period, reps):
        sel = (jnp.arange(period)[:, None] == (jnp.arange(period * reps)[None, :] % period)).astype(BF16)
        return jnp.dot(x.astype(BF16), sel, preferred_element_type=BF16)

    same = grp_of_row == (jnp.arange(sw)[None, :] // cc)
    d_lag = jnp.where(same[None], spread(kt, cc, g), 0)

    same_in = jnp.tile(grp_of_row, (t, 1)) == ((jnp.arange(ns)[None, :] % (g * p)) // p)

    def in_to_state(pre, pim, bre, bim):
        xre = pre[..., None] * bre[None] - pim[..., None] * bim[None]
        xim = pre[..., None] * bim[None] + pim[..., None] * bre[None]
        return [spread(xre.transpose(0, 1, 3, 2).reshape(t * sw, p), p, g),
                spread(xim.transpose(0, 1, 3, 2).reshape(t * sw, p), p, g)]

    w_in = jnp.concatenate(in_to_state(pw_re[t - 1::-1, 0], pw_im[t - 1::-1, 0], bb_re[0], bb_im[0])
                           + in_to_state(pw_re[:t, 1], pw_im[:t, 1], bb_re[1], bb_im[1]), axis=1)
    w_in = jnp.where(jnp.tile(same_in, (1, 2)), w_in, 0)

    same_out = ((jnp.arange(ns)[:, None] % (g * p)) // p) == ((jnp.arange(t * sw)[None, :] // cc) % g)
    col = jnp.arange(t * sw)
    pick = (jnp.arange(t * cc)[:, None] == ((col // sw) * cc + col % cc)[None, :]).astype(BF16)

    def state_to_out(pre, pim):
        are = cre[None] * pre[:, :, None, :] - cim[None] * pim[:, :, None, :]
        aim = cre[None] * pim[:, :, None, :] + cim[None] * pre[:, :, None, :]
        a = jnp.concatenate([are.transpose(1, 3, 0, 2), -aim.transpose(1, 3, 0, 2)], axis=0)
        a = jnp.dot(a.reshape(ns, t * cc).astype(BF16), pick, preferred_element_type=BF16)
        return jnp.where(same_out, a, 0)

    w_out_f = state_to_out(pw_re[1:, 0], pw_im[1:, 0])
    w_out_b = state_to_out(pw_re[t:0:-1, 1], pw_im[t:0:-1, 1])

    decay = jnp.stack([pw_re[t].reshape(2, g * p), pw_im[t].reshape(2, g * p)], axis=1)
    return d_lag, w_in, w_out_f, w_out_b, decay


def _s5_in_kernel(u_ref, d_ref, w_ref, o_ref):
    j = pl.program_id(0)

    @pl.when(j < S5_T)
    def _():
        acc = _dot(u_ref[:, 0:S5_WIDTH], d_ref[j + S5_T - 1])
        for s in range(1, S5_T):
            acc = acc + _dot(u_ref[:, s * S5_WIDTH:(s + 1) * S5_WIDTH], d_ref[j - s + S5_T - 1])
        o_ref[...] = acc

    @pl.when(j >= S5_T)
    def _():
        o_ref[...] = _dot(u_ref[...], w_ref[...])


def _s5_in(u, d_lag, w_in, tm):
    m, k = u.shape
    tn = S5_WIDTH
    nj = (k + w_in.shape[1]) // tn
    return pl.pallas_call(
        _s5_in_kernel,
        out_shape=jax.ShapeDtypeStruct((m, nj * tn), F32),
        grid=(nj, m // tm),
        in_specs=[pl.BlockSpec((tm, k), lambda j, i: (i, 0)),
                  pl.BlockSpec(d_lag.shape, lambda j, i: (0, 0, 0)),
                  pl.BlockSpec((k, tn), lambda j, i: (0, jnp.maximum(j - S5_T, 0)))],
        out_specs=pl.BlockSpec((tm, tn), lambda j, i: (i, j)),
        compiler_params=_params("arbitrary", "arbitrary"),
        name="s5_in",
    )(u, d_lag, w_in)


def _s5_scan_kernel(efr_ref, efi_ref, ebr_ref, ebi_ref, a_ref, hfr_ref, hfi_ref, hbr_ref, hbi_ref,
                    *, nb, rows_in, rows_out):
    dirs = ((efr_ref, efi_ref, hfr_ref, hfi_ref, a_ref[0, 0:1, :], a_ref[0, 1:2, :]),
            (ebr_ref, ebi_ref, hbr_ref, hbi_ref, a_ref[1, 0:1, :], a_ref[1, 1:2, :]))
    zero = jnp.zeros_like(dirs[0][4])

    def step(srcs, carry, store):
        new = []
        for di, (er_ref, ei_ref, hr_ref, hi_ref, are, aim) in enumerate(dirs):
            for bi in range(nb):
                hre, him = carry[2 * (di * nb + bi)], carry[2 * (di * nb + bi) + 1]
                if store:
                    hr_ref[pl.ds(bi * rows_out + srcs[di], 1), :] = hre
                    hi_ref[pl.ds(bi * rows_out + srcs[di], 1), :] = him
                ere = er_ref[pl.ds(bi * rows_in + srcs[di], 1), :]
                eim = ei_ref[pl.ds(bi * rows_in + srcs[di], 1), :]
                new += [are * hre - aim * him + ere, are * him + aim * hre + eim]
        return tuple(new)

    n_ctx = rows_in - rows_out
    carry = lax.fori_loop(0, n_ctx, lambda s, c: step((rows_out + s, rows_in - 1 - s), c, False),
                          tuple([zero] * (4 * nb)))
    lax.fori_loop(0, rows_out, lambda s, c: step((s, rows_out - 1 - s), c, True), carry)


def _s5_scan(e, decay, nb, rows_in, rows_out):
    tc = 256
    nsr = S5_GROUPS * S5_STATE
    c0 = (S5_T * S5_WIDTH) // tc
    nt = nsr // tc
    eblk = lambda k: pl.BlockSpec((nb * rows_in, tc), lambda j: (0, c0 + k * nt + j))
    hblk = pl.BlockSpec((nb * rows_out, tc), lambda j: (0, j))
    return pl.pallas_call(
        functools.partial(_s5_scan_kernel, nb=nb, rows_in=rows_in, rows_out=rows_out),
        out_shape=[jax.ShapeDtypeStruct((nb * rows_out, nsr), F32)] * 4,
        grid=(nt,),
        in_specs=[eblk(0), eblk(1), eblk(2), eblk(3), pl.BlockSpec((2, 2, tc), lambda j: (0, 0, j))],
        out_specs=[hblk] * 4,
        compiler_params=_params("arbitrary"),
        name="s5_scan",
    )(e, e, e, e, decay)


def _gelu_tanh(x):
    return 0.5 * x * (1.0 + jnp.tanh(math.sqrt(2.0 / math.pi) * (x + 0.044715 * x * x * x)))


def _s5_out_kernel(hfr_ref, hfi_ref, hbr_ref, hbi_ref, wf_ref, wb_ref, yi_ref, u_ref, d_ref, wg_ref, o_ref):
    nsr = hfr_ref.shape[1]
    y = yi_ref[0] + d_ref[...] * u_ref[...].astype(F32)
    for h_ref, w_ref, r0 in ((hfr_ref, wf_ref, 0), (hfi_ref, wf_ref, nsr), (hbr_ref, wb_ref, 0), (hbi_ref, wb_ref, nsr)):
        y = y + _dot(h_ref[...].astype(BF16), w_ref[r0:r0 + nsr, :])
    y = _gelu_tanh(y)
    gate = _sigmoid(_dot(y.astype(BF16), wg_ref[...]))
    o_ref[...] = (y * gate).astype(o_ref.dtype)


def _s5_out(states, w_out_f, w_out_b, e3, u_rows, d_row, w_glu):
    m, nsr = states[0].shape
    nb = e3.shape[0]
    tm = m // nb
    tn = S5_WIDTH
    st = pl.BlockSpec((tm, nsr), lambda i, j: (i, 0))
    wo = pl.BlockSpec((2 * nsr, tn), lambda i, j: (0, j))
    return pl.pallas_call(
        _s5_out_kernel,
        out_shape=jax.ShapeDtypeStruct((m, S5_T * S5_WIDTH), BF16),
        grid=(nb, S5_T),
        in_specs=[st, st, st, st, wo, wo,
                  pl.BlockSpec((1, tm, tn), lambda i, j: (i, 0, j)),
                  pl.BlockSpec((tm, tn), lambda i, j: (i, j)),
                  pl.BlockSpec((1, tn), lambda i, j: (0, 0)),
                  pl.BlockSpec((tn, tn), lambda i, j: (0, 0))],
        out_specs=pl.BlockSpec((tm, tn), lambda i, j: (i, j)),
        compiler_params=_params("arbitrary", "arbitrary"),
        name="s5_out",
    )(*states, w_out_f, w_out_b, e3, u_rows, d_row, w_glu)


U32 = jnp.uint32
ROW_SUB = 4


def _to_token_rows(ref, val):
    t, d = val.shape

    def rounded(x):
        u = lax.bitcast_convert_type(x, U32)
        return u + (jnp.uint32(0x7FFF) + ((u >> 16) & jnp.uint32(1)))

    w = (rounded(val[:, :d // 2]) >> 16) | (rounded(val[:, d // 2:]) & jnp.uint32(0xFFFF0000))
    for s in range(ROW_SUB):
        ref[pl.ds(s, t, stride=ROW_SUB), :] = w[:, s * LANES:(s + 1) * LANES]


def _from_token_rows(ref, t, row0=0):
    w = jnp.concatenate([ref[pl.ds(row0 * ROW_SUB + s, t, stride=ROW_SUB), :] for s in range(ROW_SUB)], axis=-1)
    lo = lax.bitcast_convert_type(w << 16, F32)
    hi = lax.bitcast_convert_type(w & jnp.uint32(0xFFFF0000), F32)
    return jnp.concatenate([lo, hi], axis=-1)


def _route(h2b, wr_ref, br_ref, cnt_ref, ls8_ref, w8_ref, seg_ref):
    tm = h2b.shape[0]
    per_group = N_EXPERTS // ROUTE_GROUPS
    scores = _sigmoid(lax.dot_general(wr_ref[...], h2b, _NT, preferred_element_type=F32))
    biased = scores + br_ref[...]
    neg = -jnp.inf
    sub = lax.broadcasted_iota(jnp.int32, (per_group, tm), 0)
    grp = []
    for gi in range(ROUTE_GROUPS):
        v = biased[gi * per_group:(gi + 1) * per_group, :]
        m1 = jnp.max(v, axis=0, keepdims=True)
        first = jnp.min(jnp.where(v == m1, sub, per_group), axis=0, keepdims=True)
        m2 = jnp.max(jnp.where(sub == first, neg, v), axis=0, keepdims=True)
        grp.append(m1 + m2)
    grp = jnp.concatenate(grp, axis=0)
    gid = lax.broadcasted_iota(jnp.int32, (ROUTE_GROUPS, tm), 0)
    beaten = jnp.zeros((ROUTE_GROUPS, tm), jnp.int32)
    for gj in range(ROUTE_GROUPS):
        r = grp[gj:gj + 1, :]
        beaten = beaten + jnp.where((r > grp) | ((r == grp) & (gj < gid)), 1, 0)
    group_ok = beaten < TOPK_GROUPS
    expert_ok = jnp.concatenate(
        [jnp.broadcast_to(group_ok[gi:gi + 1, :], (per_group, tm)) for gi in range(ROUTE_GROUPS)], axis=0)
    cur = jnp.where(expert_ok, biased, neg)
    eid = lax.broadcasted_iota(jnp.int32, (N_EXPERTS, tm), 0)
    sel = jnp.zeros((N_EXPERTS, tm), F32)
    picks, wts = [], []
    for _ in range(TOP_K):
        m = jnp.max(cur, axis=0, keepdims=True)
        idx = jnp.min(jnp.where(cur == m, eid, N_EXPERTS), axis=0, keepdims=True)
        hit = eid == idx
        picks.append(idx)
        wts.append(jnp.sum(jnp.where(hit, scores, 0.0), axis=0, keepdims=True))
        sel = jnp.where(hit, 1.0, sel)
        cur = jnp.where(hit, neg, cur)
    wsum = wts[0]
    for w in wts[1:]:
        wsum = wsum + w
    selb = sel.astype(BF16)
    ti = lax.broadcasted_iota(jnp.int32, (tm, tm), 0)
    tj = lax.broadcasted_iota(jnp.int32, (tm, tm), 1)
    rank = _dot(selb, jnp.where(ti < tj, 1.0, 0.0).astype(BF16))
    ei = lax.broadcasted_iota(jnp.int32, (N_EXPERTS, N_EXPERTS), 0)
    ej = lax.broadcasted_iota(jnp.int32, (N_EXPERTS, N_EXPERTS), 1)
    seg_off = jnp.sum(_dot(jnp.where(ej < ei, 1.0, 0.0).astype(BF16), selb), axis=1, keepdims=True)
    seg_cnt = jnp.sum(sel, axis=1, keepdims=True)
    slot = seg_off + rank
    for k in range(TOP_K):
        w8_ref[k:k + 1, :] = wts[k] / wsum * ROUTED_SCALE
        ls8_ref[k:k + 1, :] = jnp.sum(jnp.where(eid == picks[k], slot, 0.0), axis=0, keepdims=True).astype(jnp.int32)
    lane = lax.broadcasted_iota(jnp.int32, (N_EXPERTS, LANES), 1)
    seg_ref[0] = jnp.where(lane == 0, cnt_ref[...], jnp.where(lane == 1, seg_cnt, seg_off))
    cnt_ref[...] = cnt_ref[...] + seg_cnt


def _merge_kernel(x_ref, ya_ref, on_ref, go_ref, ga_ref, gb_ref, g1_ref, sc_ref, sh_ref, n2_ref,
                  pa_ref, pb_ref, wo_ref, wr_ref, br_ref, pt_ref,
                  x1_ref, h2_ref, ls8_ref, w8_ref, seg_ref, cnt_ref, fold_a, fold_b):
    @pl.when(pl.program_id(0) == 0)
    def _():
        cnt_ref[...] = jnp.zeros_like(cnt_ref)

    go = go_ref[...].astype(F32)
    on = _dot(pt_ref[...], on_ref[0].reshape(x_ref.shape[0], HG_WIDTH))
    y_b = (on * (go * _sigmoid(go))).astype(BF16)
    y_a = _unfold_rows(ya_ref[...].astype(F32), fold_a, fold_b).astype(BF16)
    pa = _dot(y_a, pa_ref[...])
    pb = _dot(y_b, pb_ref[...])
    merged = _sigmoid(ga_ref[...].astype(F32)) * pa + _sigmoid(gb_ref[...].astype(F32)) * pb
    x1 = x_ref[...] + g1_ref[0] * _dot(merged.astype(BF16), wo_ref[...])
    x1_ref[...] = x1
    y = x1 * lax.rsqrt(jnp.mean(x1 * x1, axis=-1, keepdims=True) + EPS) * n2_ref[...]
    h2 = y * (1.0 + sc_ref[0]) + sh_ref[0]
    _to_token_rows(h2_ref, h2)
    _route(h2.astype(BF16), wr_ref, br_ref, cnt_ref, ls8_ref, w8_ref, seg_ref)


def _merge(x2d, ya, on, go, ga, gb, g1, sc2, sh2, n2g, pa, pb, wo, wr_t, br, rows_per_batch, tm):
    n, d = x2d.shape
    per = rows_per_batch // tm
    row = lambda wd: pl.BlockSpec((tm, wd), lambda i: (i, 0))
    mod = pl.BlockSpec((1, 1, d), lambda i: (i // per, 0, 0))
    full = lambda a: pl.BlockSpec(a.shape, lambda i: (0, 0))
    tok = pl.BlockSpec((TOP_K, tm), lambda i: (0, i))
    return pl.pallas_call(
        _merge_kernel,
        out_shape=[jax.ShapeDtypeStruct((n, d), F32), jax.ShapeDtypeStruct((n * ROW_SUB, LANES), U32),
                   jax.ShapeDtypeStruct((TOP_K, n), jnp.int32), jax.ShapeDtypeStruct((TOP_K, n), F32),
                   jax.ShapeDtypeStruct((n // tm, N_EXPERTS, LANES), F32),
                   jax.ShapeDtypeStruct((N_EXPERTS, 1), F32)],
        grid=(n // tm,),
        in_specs=[row(d), pl.BlockSpec((tm // S5_T, S5_T * S5_WIDTH), lambda i: (i, 0)),
                  pl.BlockSpec((1, GRID_W, tm // GRID_W, HG_WIDTH), lambda i: (i // per, 0, i % per, 0)),
                  row(HG_WIDTH), row(d), row(d), mod, mod, mod,
                  pl.BlockSpec((1, d), lambda i: (0, 0)), full(pa), full(pb), full(wo), full(wr_t), full(br),
                  pl.BlockSpec((tm, tm), lambda i: (0, 0))],
        out_specs=[row(d), pl.BlockSpec((tm * ROW_SUB, LANES), lambda i: (i, 0)), tok, tok,
                   pl.BlockSpec((1, N_EXPERTS, LANES), lambda i: (i, 0, 0)),
                   pl.BlockSpec((N_EXPERTS, 1), lambda i: (0, 0))],
        scratch_shapes=[pltpu.VMEM((tm, LANES), F32), pltpu.VMEM((tm, LANES), F32)],
        compiler_params=_params("arbitrary"),
        name="merge_out_proj_route",
    )(x2d, ya, on, go, ga, gb, g1, sc2, sh2, n2g.reshape(1, d), pa, pb, wo, wr_t, br,
      _grid_transpose_matrix(tm).T)


MOE_TILE = TOK_TILE
MOE_BLK = 1024


def _token_row(ref, r):
    return ref.at[pl.ds(pl.multiple_of(r * ROW_SUB, ROW_SUB), ROW_SUB)]


def _wait_rows(any_ref, sem, n_rows):
    view = any_ref.at[pl.ds(0, n_rows * ROW_SUB)]
    pltpu.make_async_copy(view, view, sem).wait()


def _rows(ref, r0, n):
    return ref.at[pl.ds(pl.multiple_of(r0 * ROW_SUB, ROW_SUB), n * ROW_SUB)]


def _pow2_pieces(n, max_piece, fn):
    done = 0
    piece = max_piece
    while piece >= 1:
        hit = (n & piece) != 0
        pl.when(hit)(functools.partial(fn, done, piece))
        done = done + (n & piece)
        piece //= 2


def _copy_rows(src_ref, src0, dst_ref, dst0, n, max_piece, sem):
    def piece(off, size):
        pltpu.make_async_copy(_rows(src_ref, src0 + off, size), _rows(dst_ref, dst0 + off, size), sem).start()
    _pow2_pieces(n, max_piece, piece)


def _wait_copied_rows(src_ref, dst_ref, n, max_piece, sem):
    def piece(off, size):
        pltpu.make_async_copy(_rows(src_ref, 0, size), _rows(dst_ref, 0, size), sem).wait()
    _pow2_pieces(n, max_piece, piece)


def _dispatch_kernel(gs_ref, cnt_ref, off_ref, pad_ref, ls_ref, h2_ref, xs_hbm, ls_smem, stage0, stage1, zbuf,
                     sem0, sem1, lsem, zsem, *, tm, n_blocks):
    i = pl.program_id(0)
    last = pl.num_programs(0) - 1
    cp = pltpu.make_async_copy(ls_ref, ls_smem, lsem)
    cp.start()
    cp.wait()

    def tile(stage, sem, prev_sem):
        def body(t, carry):
            row = h2_ref[pl.ds(pl.multiple_of(t * ROW_SUB, ROW_SUB), ROW_SUB), :]
            for k in range(TOP_K):
                slot = ls_smem[t * TOP_K + k]
                stage[pl.ds(pl.multiple_of(slot * ROW_SUB, ROW_SUB), ROW_SUB), :] = row
            return carry

        lax.fori_loop(0, tm, body, 0)

        def per_expert(e, carry):
            _copy_rows(stage, off_ref[i, e], xs_hbm, gs_ref[i, e], cnt_ref[i, e], tm, sem)
            return carry

        lax.fori_loop(0, N_EXPERTS, per_expert, 0)

        @pl.when(i > 0)
        def _():
            _wait_rows(xs_hbm, prev_sem, tm * TOP_K)

        @pl.when(i == last)
        def _():
            _wait_rows(xs_hbm, sem, tm * TOP_K)

    pl.when(i % 2 == 0)(functools.partial(tile, stage0, sem0, sem1))
    pl.when(i % 2 == 1)(functools.partial(tile, stage1, sem1, sem0))

    @pl.when(i == 0)
    def _():
        zbuf[...] = jnp.zeros_like(zbuf)

        def start(e, carry):
            _copy_rows(zbuf, 0, xs_hbm, pad_ref[0, e], pad_ref[1, e], MOE_BLK // 2, zsem)
            return carry

        def wait(e, carry):
            _wait_copied_rows(zbuf, xs_hbm, pad_ref[1, e], MOE_BLK // 2, zsem)
            return carry

        lax.fori_loop(0, N_EXPERTS, start, 0)
        lax.fori_loop(0, N_EXPERTS, wait, 0)

        def zero_block(j, carry):
            pltpu.make_async_copy(zbuf, _rows(xs_hbm, j * MOE_BLK, MOE_BLK), zsem).start()
            return carry

        def wait_block(j, carry):
            pltpu.make_async_copy(zbuf, _rows(xs_hbm, 0, MOE_BLK), zsem).wait()
            return carry

        lax.fori_loop(pad_ref[2, 0], n_blocks, zero_block, 0)
        lax.fori_loop(pad_ref[2, 0], n_blocks, wait_block, 0)


def _dispatch(gstart, seg_cnt, seg_off, pad, ls8, h2_rows, n_blocks, tm):
    n = ls8.shape[0] // TOP_K
    cap = n_blocks * MOE_BLK
    return pl.pallas_call(
        functools.partial(_dispatch_kernel, tm=tm, n_blocks=n_blocks),
        out_shape=jax.ShapeDtypeStruct((cap * ROW_SUB, LANES), U32),
        grid_spec=pltpu.PrefetchScalarGridSpec(
            num_scalar_prefetch=4,
            grid=(n // tm,),
            in_specs=[pl.BlockSpec((tm * TOP_K,), lambda i, *_: (i,)),
                      pl.BlockSpec((tm * ROW_SUB, LANES), lambda i, *_: (i, 0))],
            out_specs=pl.BlockSpec(memory_space=pl.ANY),
            scratch_shapes=[pltpu.SMEM((tm * TOP_K,), jnp.int32),
                            pltpu.VMEM((TOP_K * tm * ROW_SUB, LANES), U32),
                            pltpu.VMEM((TOP_K * tm * ROW_SUB, LANES), U32),
                            pltpu.VMEM((MOE_BLK * ROW_SUB, LANES), U32),
                            pltpu.SemaphoreType.DMA, pltpu.SemaphoreType.DMA, pltpu.SemaphoreType.DMA,
                            pltpu.SemaphoreType.DMA]),
        compiler_params=pltpu.CompilerParams(dimension_semantics=("arbitrary",), vmem_limit_bytes=VMEM_LIMIT,
                                             has_side_effects=True),
        name="moe_dispatch",
    )(gstart, seg_cnt, seg_off, pad, ls8, h2_rows)


def _expert_kernel(be_ref, nu_ref, x_ref, w1_ref, w3_ref, w2_ref, o_ref, w1b, w3b, w2b):
    j = pl.program_id(0)
    e = be_ref[j]
    prev = be_ref[jnp.maximum(j - 1, 0)]
    used = j < nu_ref[0]

    @pl.when(jnp.logical_and(used, jnp.logical_or(j == 0, e != prev)))
    def _():
        w1b[...] = w1_ref[0].astype(BF16)
        w3b[...] = w3_ref[0].astype(BF16)
        w2b[...] = w2_ref[0].astype(BF16)

    @pl.when(used)
    def _():
        x = _from_token_rows(x_ref, MOE_BLK).astype(BF16)
        a = _dot(x, w1b[...])
        hid = (a * _sigmoid(a)) * _dot(x, w3b[...])
        _to_token_rows(o_ref, _dot(hid.astype(BF16), w2b[...]))

    @pl.when(jnp.logical_not(used))
    def _():
        o_ref[...] = jnp.zeros_like(o_ref)


def _experts(block_e, n_used, xs, w1, w3, w2):
    n_blocks = xs.shape[0] // (MOE_BLK * ROW_SUB)
    d, f = w1.shape[1], w1.shape[2]
    rows = pl.BlockSpec((MOE_BLK * ROW_SUB, LANES), lambda j, be, nu: (j, 0))
    return pl.pallas_call(
        _expert_kernel,
        out_shape=jax.ShapeDtypeStruct(xs.shape, U32),
        grid_spec=pltpu.PrefetchScalarGridSpec(
            num_scalar_prefetch=2,
            grid=(n_blocks,),
            in_specs=[rows,
                      pl.BlockSpec((1, d, f), lambda j, be, nu: (be[j], 0, 0)),
                      pl.BlockSpec((1, d, f), lambda j, be, nu: (be[j], 0, 0)),
                      pl.BlockSpec((1, f, d), lambda j, be, nu: (be[j], 0, 0))],
            out_specs=rows,
            scratch_shapes=[pltpu.VMEM((d, f), BF16), pltpu.VMEM((d, f), BF16), pltpu.VMEM((f, d), BF16)]),
        compiler_params=_params("arbitrary"),
        name="moe_experts",
    )(block_e, n_used, xs, w1, w3, w2)


def _combine_kernel(gs_ref, cnt_ref, off_ref, ls_ref, w8_ref, x1_ref, h2_ref, g2_ref, ws1_ref, ws3_ref, ws2_ref,
                    fg_ref, ys_hbm, o_ref, ls_smem, w_smem, gbuf0, gbuf1, acc_rows, sem0, sem1, lsem, *, tm):
    i = pl.program_id(0)
    last = pl.num_programs(0) - 1
    cp1 = pltpu.make_async_copy(ls_ref, ls_smem, lsem)
    cp2 = pltpu.make_async_copy(w8_ref, w_smem, lsem)
    cp1.start()
    cp2.start()

    def fetch(tile, gbuf, sem):
        def per_expert(e, carry):
            _copy_rows(ys_hbm, gs_ref[tile, e], gbuf, off_ref[tile, e], cnt_ref[tile, e], tm, sem)
            return carry

        lax.fori_loop(0, N_EXPERTS, per_expert, 0)

    @pl.when(i == 0)
    def _():
        fetch(0, gbuf0, sem0)

    @pl.when(jnp.logical_and(i < last, i % 2 == 0))
    def _():
        fetch(i + 1, gbuf1, sem1)

    @pl.when(jnp.logical_and(i < last, i % 2 == 1))
    def _():
        fetch(i + 1, gbuf0, sem0)

    h2 = _from_token_rows(h2_ref, tm).astype(BF16)
    a = _dot(h2, ws1_ref[...])
    hid = (a * _sigmoid(a)) * _dot(h2, ws3_ref[...])
    acc = _dot(hid.astype(BF16), ws2_ref[...])
    cp1.wait()
    cp2.wait()

    def reduce_rows(gbuf, sem):
        _wait_rows(gbuf, sem, tm * TOP_K)

        def body(t, carry):
            lo = jnp.zeros((ROW_SUB, LANES), F32)
            hi = jnp.zeros((ROW_SUB, LANES), F32)
            for k in range(TOP_K):
                w = w_smem[t * TOP_K + k]
                words = gbuf[pl.ds(pl.multiple_of(ls_smem[t * TOP_K + k] * ROW_SUB, ROW_SUB), ROW_SUB), :]
                lo = lo + w * lax.bitcast_convert_type(words << 16, F32)
                hi = hi + w * lax.bitcast_convert_type(words & jnp.uint32(0xFFFF0000), F32)
            acc_rows[pl.ds(pl.multiple_of(t * SUBLANES, SUBLANES), ROW_SUB), :] = lo
            acc_rows[pl.ds(pl.multiple_of(t * SUBLANES, SUBLANES) + ROW_SUB, ROW_SUB), :] = hi
            return carry

        lax.fori_loop(0, tm, body, 0)

    pl.when(i % 2 == 0)(functools.partial(reduce_rows, gbuf0, sem0))
    pl.when(i % 2 == 1)(functools.partial(reduce_rows, gbuf1, sem1))
    routed = jnp.concatenate([acc_rows[pl.ds(s, tm, stride=SUBLANES), :] for s in range(SUBLANES)], axis=-1)
    y = x1_ref[...] + g2_ref[0] * (acc + routed)
    o_ref[...] = y * lax.rsqrt(jnp.mean(y * y, axis=-1, keepdims=True) + EPS) * fg_ref[...]


def _combine(gstart, seg_cnt, seg_off, ls8, w8, x1, h2_rows, g2, ws1, ws3, ws2, fg, ys, rows_per_batch, tm):
    n, d = x1.shape
    per = rows_per_batch // tm
    tok = pl.BlockSpec((tm * TOP_K,), lambda i, *_: (i,))
    full = lambda a: pl.BlockSpec(a.shape, lambda i, *_: (0, 0))
    return pl.pallas_call(
        functools.partial(_combine_kernel, tm=tm),
        out_shape=jax.ShapeDtypeStruct((n, d), F32),
        grid_spec=pltpu.PrefetchScalarGridSpec(
            num_scalar_prefetch=3,
            grid=(n // tm,),
            in_specs=[tok, tok, pl.BlockSpec((tm, d), lambda i, *_: (i, 0)),
                      pl.BlockSpec((tm * ROW_SUB, LANES), lambda i, *_: (i, 0)),
                      pl.BlockSpec((1, 1, d), lambda i, *_: (i // per, 0, 0)),
                      full(ws1), full(ws3), full(ws2), pl.BlockSpec((1, d), lambda i, *_: (0, 0)),
                      pl.BlockSpec(memory_space=pl.ANY)],
            out_specs=pl.BlockSpec((tm, d), lambda i, *_: (i, 0)),
            scratch_shapes=[pltpu.SMEM((tm * TOP_K,), jnp.int32), pltpu.SMEM((tm * TOP_K,), F32),
                            pltpu.VMEM((TOP_K * tm * ROW_SUB, LANES), U32),
                            pltpu.VMEM((TOP_K * tm * ROW_SUB, LANES), U32),
                            pltpu.VMEM((tm * SUBLANES, LANES), F32), pltpu.SemaphoreType.DMA,
                            pltpu.SemaphoreType.DMA, pltpu.SemaphoreType.DMA]),
        compiler_params=_params("arbitrary"),
        name="moe_combine_final",
    )(gstart, seg_cnt, seg_off, ls8, w8, x1, h2_rows, g2, ws1, ws3, ws2, fg.reshape(1, d), ys)


def _moe_plan(seg, counts, n_assign):
    cnt = counts.reshape(N_EXPERTS).astype(jnp.int32)
    padded = (cnt + MOE_BLK - 1) // MOE_BLK * MOE_BLK
    pends = jnp.cumsum(padded)
    pstarts = pends - padded
    n_blocks = (n_assign + N_EXPERTS * (MOE_BLK - 1) + MOE_BLK - 1) // MOE_BLK
    seg = seg[:, :, :3].astype(jnp.int32)
    gstart = pstarts[None, :] + seg[:, :, 0]
    blk_start = jnp.arange(n_blocks, dtype=jnp.int32) * MOE_BLK
    block_e = jnp.minimum(jnp.sum((blk_start[:, None] >= pends[None, :]).astype(jnp.int32), axis=1),
                          N_EXPERTS - 1).astype(jnp.int32)
    n_used = (pends[-1:] // MOE_BLK).astype(jnp.int32)
    pad = jnp.stack([pstarts + cnt, padded - cnt, jnp.broadcast_to(n_used, (N_EXPERTS,))], axis=0).astype(jnp.int32)
    return gstart, seg[:, :, 1], seg[:, :, 2], pad, block_e, n_used, n_blocks


def _mixer(x, c, ctx, c_ctx, w_ada, b_ada, norm1_g, norm2_g, w_in, s5_lam_re, s5_lam_im, s5_log_dt,
           s5_b_re, s5_b_im, s5_c_re, s5_c_im, s5_d, s5_w_glu, lb, hg_norm_g, p_a, p_b, w_out,
           moe_w_router, moe_b_router):
    b, l, d = x.shape
    lc = ctx.shape[1]
    n = b * l
    rows = l // GRID_W

    c8 = jnp.concatenate([c, c_ctx[None], jnp.zeros((8 - b - 1, d), F32)], axis=0)
    mod = _ada(c8, w_ada, b_ada)
    sh1, sc1, g1, sh2, sc2, g2 = [mod[:b, k * d:(k + 1) * d].reshape(b, 1, d) for k in range(6)]
    csh1, csc1 = mod[b:b + 1, 0:d].reshape(1, 1, d), mod[b:b + 1, d:2 * d].reshape(1, 1, d)

    w_in_b = w_in.astype(BF16)
    z = dict(zip([p[0] for p in _IN_PIECES],
                 _inproj(x.reshape(n, d), sc1, sh1, norm1_g, w_in_b, l, TOK_TILE, True)))
    zc = dict(zip([p[0] for p in _IN_PIECES],
                  _inproj(ctx.reshape(b * lc, d), csc1, csh1, norm1_g, w_in_b, lc, lc, False)))

    cx = lambda t: t.reshape(b, lc, HG_WIDTH)
    lb_row = lb.reshape(1, HG_WIDTH)
    o_f = _hgrn_pass(z["q"], z["ff"], z["i"], cx(zc["ff"]), cx(zc["i"]), lb_row, None, None, reverse=False)
    o_n = _hgrn_pass(z["q"], z["fb"], z["i"], cx(zc["fb"]), cx(zc["i"]), lb_row, o_f,
                     hg_norm_g.reshape(1, HG_DK), reverse=True)

    d_lag, w_s5_in, w_out_f, w_out_b, decay = _s5_weights(s5_lam_re, s5_lam_im, s5_log_dt, s5_b_re, s5_b_im,
                                                          s5_c_re, s5_c_im)
    kc, kl = lc // S5_T, l // S5_T
    u_lat = z["u"].reshape(b, kl, S5_T * S5_WIDTH)
    u_ctx = zc["u"].reshape(b, kc, S5_T * S5_WIDTH)
    rows_in = kl + kc
    u_ext = jnp.concatenate([u_lat, u_ctx], axis=1).reshape(b * rows_in, S5_T * S5_WIDTH)
    e = _s5_in(u_ext, d_lag, w_s5_in, (b * rows_in) // 2)
    states = _s5_scan(e, decay, b, rows_in, kl)
    d_row = s5_d.astype(F32).reshape(1, S5_WIDTH)
    y_a = _s5_out(states, w_out_f, w_out_b, e.reshape(b, rows_in, -1), z["u"], d_row, s5_w_glu.astype(BF16))

    return _merge(x.reshape(n, d), y_a, o_n, z["go"], z["ga"], z["gb"], g1, sc2, sh2, norm2_g,
                  p_a.astype(BF16), p_b.astype(BF16), w_out.astype(BF16),
                  moe_w_router.T.astype(BF16), moe_b_router.astype(F32).reshape(N_EXPERTS, 1), l, MOE_TILE) + (g2,)


def kernel(x, c, ctx, c_ctx, w_ada, b_ada, norm1_g, norm2_g, w_in, s5_lam_re, s5_lam_im, s5_log_dt, s5_b_re,
           s5_b_im, s5_c_re, s5_c_im, s5_d, s5_w_glu, hg_lb_logits, hg_norm_g, p_a, p_b, w_out, moe_w_router,
           moe_b_router, moe_w1, moe_w3, moe_w2, moe_ws1, moe_ws3, moe_ws2, final_norm_g):
    b, l, d = x.shape
    n = b * l
    assert w_ada.shape[0] == 1, "single-layer block"
    lb = jnp.cumsum(jax.nn.softmax(hg_lb_logits.astype(F32), axis=0), axis=0)[0]
    x1, h2_rows, ls8, w8, seg, counts, g2 = _mixer(
        x, c, ctx, c_ctx, w_ada[0], b_ada[0], norm1_g[0], norm2_g[0], w_in[0], s5_lam_re[0], s5_lam_im[0],
        s5_log_dt[0], s5_b_re[0], s5_b_im[0], s5_c_re[0], s5_c_im[0], s5_d[0], s5_w_glu[0], lb, hg_norm_g[0],
        p_a[0], p_b[0], w_out[0], moe_w_router[0], moe_b_router[0])
    gstart, seg_cnt, seg_off, pad, block_e, n_used, n_blocks = _moe_plan(seg, counts, n * TOP_K)
    ls_flat, w_flat = ls8.T.reshape(n * TOP_K), w8.T.reshape(n * TOP_K)
    xs = _dispatch(gstart, seg_cnt, seg_off, pad, ls_flat, h2_rows, n_blocks, MOE_TILE)
    ys = _experts(block_e, n_used, xs, moe_w1[0], moe_w3[0], moe_w2[0])
    out = _combine(gstart, seg_cnt, seg_off, ls_flat, w_flat, x1, h2_rows, g2, moe_ws1[0].astype(BF16),
                   moe_ws3[0].astype(BF16), moe_ws2[0].astype(BF16), final_norm_g, ys, l, MOE_TILE)
    return out.reshape(b, l, d)
```

```python
import functools
import math

import jax
import jax.numpy as jnp
from jax import lax
from jax.experimental import pallas as pl
from jax.experimental.pallas import tpu as pltpu

F32 = jnp.float32
BF16 = jnp.bfloat16

GRID_W = 64
S5_WIDTH = 256
S5_GROUP = 16
S5_GROUPS = 16
S5_STATE = 64
HG_HEADS = 6
HG_DK = 128
HG_WIDTH = HG_HEADS * HG_DK
N_EXPERTS = 64
ROUTE_GROUPS = 8
TOPK_GROUPS = 4
TOP_K = 8
ROUTED_SCALE = 2.5
EPS = 1e-6

LANES = 128
SUBLANES = 8

TOK_TILE = 512
S5_T = 16
HG_CHUNK = 64
HG_BATCH = 4
VMEM_LIMIT = 56 * 1024 * 1024

_NT = (((1,), (1,)), ((), ()))
_TN = (((0,), (0,)), ((), ()))


def _params(*sem):
    return pltpu.CompilerParams(dimension_semantics=sem, vmem_limit_bytes=VMEM_LIMIT)


def _dot(a, b):
    return jnp.dot(a, b, preferred_element_type=F32)


def _sigmoid(x):
    return 1.0 / (1.0 + jnp.exp(-x))


def _ada_kernel(c_ref, w_ref, b_ref, o_ref):
    c = c_ref[...]
    s = (c * _sigmoid(c)).astype(BF16)
    o_ref[...] = _dot(s, w_ref[...].astype(BF16)) + b_ref[...]


def _ada(c8, w_ada, b_ada):
    d, n = w_ada.shape
    tn = 1536
    return pl.pallas_call(
        _ada_kernel,
        out_shape=jax.ShapeDtypeStruct((8, n), F32),
        grid=(n // tn,),
        in_specs=[pl.BlockSpec((8, d), lambda j: (0, 0)),
                  pl.BlockSpec((d, tn), lambda j: (0, j)),
                  pl.BlockSpec((1, tn), lambda j: (0, j))],
        out_specs=pl.BlockSpec((8, tn), lambda j: (0, j)),
        compiler_params=_params("arbitrary"),
        name="ada_mod",
    )(c8, w_ada, b_ada.reshape(1, n))


_IN_PIECES = (("u", 0, 256, BF16), ("q", 256, 768, BF16), ("ff", 1024, 768, F32),
              ("fb", 1792, 768, F32), ("i", 2560, 768, BF16), ("go", 3328, 768, BF16),
              ("ga", 4096, 1024, BF16), ("gb", 5120, 1024, BF16))


def _fold_rows(val, buf_a, buf_b):
    t = val.shape[0]
    buf_a[...] = val[:, :LANES]
    buf_b[...] = val[:, LANES:]
    pieces = []
    for s in range(S5_T):
        pieces += [buf_a[pl.ds(s, t // S5_T, stride=S5_T), :], buf_b[pl.ds(s, t // S5_T, stride=S5_T), :]]
    return jnp.concatenate(pieces, axis=-1)


def _unfold_rows(val, buf_a, buf_b):
    r = val.shape[0]
    for s in range(S5_T):
        buf_a[pl.ds(s, r, stride=S5_T), :] = val[:, s * S5_WIDTH:s * S5_WIDTH + LANES]
        buf_b[pl.ds(s, r, stride=S5_T), :] = val[:, s * S5_WIDTH + LANES:(s + 1) * S5_WIDTH]
    return jnp.concatenate([buf_a[...], buf_b[...]], axis=-1)


def _grid_transpose_matrix(tm):
    i = jnp.arange(tm)
    src = (i % (tm // GRID_W)) * GRID_W + i // (tm // GRID_W)
    return (src[:, None] == jnp.arange(tm)[None, :]).astype(BF16)


def _inproj_kernel(x_ref, sc_ref, sh_ref, g_ref, w_ref, p_ref, *o_refs):
    o_refs, (fold_a, fold_b) = o_refs[:len(_IN_PIECES)], o_refs[len(_IN_PIECES):]
    x = x_ref[...]
    y = x * lax.rsqrt(jnp.mean(x * x, axis=-1, keepdims=True) + EPS) * g_ref[...]
    h = (y * (1.0 + sc_ref[0]) + sh_ref[0]).astype(BF16)
    h_cm = None
    for (name, a, wd, _), o_ref in zip(_IN_PIECES, o_refs):
        if name == "u":
            o_ref[...] = _fold_rows(_dot(h, w_ref[:, a:a + wd]), fold_a, fold_b).astype(o_ref.dtype)
        elif len(o_ref.shape) == 2:
            o_ref[...] = _dot(h, w_ref[:, a:a + wd]).astype(o_ref.dtype)
        else:
            if h_cm is None:
                h_cm = _dot(p_ref[...], h).astype(BF16)
            o_ref[0] = _dot(h_cm, w_ref[:, a:a + wd]).astype(o_ref.dtype).reshape(o_ref.shape[1:])


_COLMAJOR_PIECES = ("q", "ff", "fb", "i")


def _inproj(x2d, sc, sh, g, w_bf16, rows_per_mod, tm, colmajor):
    n, d = x2d.shape
    per = rows_per_mod // tm
    mod_map = (lambda i: (i // per, 0, 0)) if sc.shape[0] > 1 else (lambda i: (0, 0, 0))
    shapes, specs = [], []
    for name, _, wd, dt in _IN_PIECES:
        if colmajor and name in _COLMAJOR_PIECES:
            shapes.append(jax.ShapeDtypeStruct((n // rows_per_mod, GRID_W, rows_per_mod // GRID_W, wd), dt))
            specs.append(pl.BlockSpec((1, GRID_W, tm // GRID_W, wd), lambda i: (i // per, 0, i % per, 0)))
        elif name == "u":
            shapes.append(jax.ShapeDtypeStruct((n // S5_T, S5_T * wd), dt))
            specs.append(pl.BlockSpec((tm // S5_T, S5_T * wd), lambda i: (i, 0)))
        else:
            shapes.append(jax.ShapeDtypeStruct((n, wd), dt))
            specs.append(pl.BlockSpec((tm, wd), lambda i: (i, 0)))
    return pl.pallas_call(
        _inproj_kernel,
        out_shape=shapes,
        grid=(n // tm,),
        in_specs=[pl.BlockSpec((tm, d), lambda i: (i, 0)),
                  pl.BlockSpec((1, 1, d), mod_map),
                  pl.BlockSpec((1, 1, d), mod_map),
                  pl.BlockSpec((1, d), lambda i: (0, 0)),
                  pl.BlockSpec(w_bf16.shape, lambda i: (0, 0)),
                  pl.BlockSpec((tm, tm), lambda i: (0, 0))],
        out_specs=specs,
        scratch_shapes=[pltpu.VMEM((tm, LANES), F32), pltpu.VMEM((tm, LANES), F32)],
        compiler_params=_params("arbitrary"),
        name="in_proj",
    )(x2d, sc, sh, g.reshape(1, d), w_bf16, _grid_transpose_matrix(tm))


def _hgrn_gates(zf, lb):
    sig = _sigmoid(zf)
    logf = jnp.log(lb + (1.0 - lb) * sig)
    k = (1.0 - lb) * (1.0 - sig)
    return logf, k


def _chunk_cumsum(cs, logf):
    hi = logf.astype(BF16)
    lo = (logf - hi.astype(F32)).astype(BF16)
    return _dot(cs, hi) + _dot(cs, lo)


def _hgrn_state_step(zf, v, lb, st, cs, reverse):
    logf, k = _hgrn_gates(zf, lb)
    cum = _chunk_cumsum(cs, logf)
    t = 0 if reverse else HG_CHUNK - 1
    total = cum[t:t + 1, :]
    kdec = (k * jnp.exp(total - cum)).astype(BF16)
    st_new = st * jnp.exp(total) + lax.dot_general(v.astype(BF16), kdec, _TN, preferred_element_type=F32)
    return cum, k, st_new


def _hgrn_kernel(*refs, reverse, final, n_ctx_chunks):
    if final:
        q_all, f_all, v_all, cf_ref, cv_ref, lb_ref, of_all, g_ref, o_all, st_ref = refs
    else:
        q_all, f_all, v_all, cf_ref, cv_ref, lb_ref, o_all, st_ref = refs
        of_all = None
    n_batch = q_all.shape[0]
    c_len = HG_CHUNK
    n_rows = q_all.shape[2]
    n_chunks = n_rows // c_len
    row = lax.broadcasted_iota(jnp.int32, (n_rows, n_rows), 0)
    col = lax.broadcasted_iota(jnp.int32, (n_rows, n_rows), 1)
    tri = (col >= row) if reverse else (col <= row)
    same_chunk = None
    for c in range(n_chunks):
        lo, hi = c * c_len, (c + 1) * c_len
        blk = (row >= lo) & (row < hi) & (col >= lo) & (col < hi)
        same_chunk = blk if same_chunk is None else (same_chunk | blk)
    mask = tri & same_chunk
    cs = jnp.where(mask, 1.0, 0.0).astype(BF16)

    @pl.when(pl.program_id(1) == 0)
    def _():
        cs1 = cs[:c_len, :c_len]
        order = range(n_ctx_chunks - 1, -1, -1) if reverse else range(n_ctx_chunks)
        for bi in range(n_batch):
            for h in range(HG_HEADS):
                cols = slice(h * HG_DK, (h + 1) * HG_DK)
                st = jnp.zeros((HG_DK, HG_DK), F32)
                for c in order:
                    rows = slice(c * c_len, (c + 1) * c_len)
                    _, _, st = _hgrn_state_step(cf_ref[bi, rows, cols], cv_ref[bi, rows, cols].astype(F32),
                                                lb_ref[:, cols], st, cs1, reverse)
                st_ref[bi * HG_HEADS + h] = st

    def per_chunk_rows(x, r):
        return jnp.concatenate([jnp.broadcast_to(x[c * c_len + r:c * c_len + r + 1, :], (c_len, x.shape[1]))
                                for c in range(n_chunks)], axis=0)

    lb = lb_ref[...]
    r_ref = c_len // 2 - 1 if reverse else c_len // 2
    r_tot = 0 if reverse else c_len - 1
    order = range(n_chunks - 1, -1, -1) if reverse else range(n_chunks)
    for bi in range(n_batch):
        q = q_all[bi, 0].astype(F32)
        v = v_all[bi, 0]
        logf, k = _hgrn_gates(f_all[bi, 0], lb)
        cum = _chunk_cumsum(cs, logf)
        ref = per_chunk_rows(cum, r_ref)
        qe = q * jnp.exp(cum - ref)
        ke = k * jnp.exp(ref - cum)
        qi, ki = qe.astype(BF16), ke.astype(BF16)
        q_in = (qe * jnp.exp(ref)).astype(BF16)
        tail = jnp.exp(per_chunk_rows(cum, r_tot) - ref)
        kdec = (ke * tail).astype(BF16)
        for h in range(HG_HEADS):
            cols = slice(h * HG_DK, (h + 1) * HG_DK)
            s = lax.dot_general(qi[:, cols], ki[:, cols], _NT, preferred_element_type=F32)
            o_intra = _dot(jnp.where(mask, s, 0.0).astype(BF16), v[:, cols])
            st = st_ref[bi * HG_HEADS + h]
            for c in order:
                rows = slice(c * c_len, (c + 1) * c_len)
                o = o_intra[rows] + lax.dot_general(q_in[rows, cols], st.astype(BF16), _NT,
                                                    preferred_element_type=F32)
                total = cum[c * c_len + r_tot:c * c_len + r_tot + 1, cols]
                st = st * jnp.exp(total) + lax.dot_general(v[rows, cols], kdec[rows, cols], _TN,
                                                           preferred_element_type=F32)
                if final:
                    o = o + of_all[bi, 0, rows, cols]
                    o = o * lax.rsqrt(jnp.mean(o * o, axis=-1, keepdims=True) + EPS) * g_ref[...]
                o_all[bi, 0, rows, cols] = o.astype(o_all.dtype)
            st_ref[bi * HG_HEADS + h] = st


def _hgrn_pass(q, f, v, cf, cv, lb, o_prev, g, *, reverse):
    b, nw, rows, _ = q.shape
    nb = HG_BATCH if b % HG_BATCH == 0 else 1
    final = o_prev is not None
    wmap = (lambda bi, w: (bi, nw - 1 - w, 0, 0)) if reverse else (lambda bi, w: (bi, w, 0, 0))
    blk = pl.BlockSpec((nb, 1, rows, HG_WIDTH), wmap)
    cblk = pl.BlockSpec((nb, cf.shape[1], HG_WIDTH), lambda bi, w: (bi, 0, 0))
    in_specs = [blk, blk, blk, cblk, cblk, pl.BlockSpec((1, HG_WIDTH), lambda bi, w: (0, 0))]
    args = [q, f, v, cf, cv, lb]
    if final:
        in_specs += [blk, pl.BlockSpec((1, HG_DK), lambda bi, w: (0, 0))]
        args += [o_prev, g]
    return pl.pallas_call(
        functools.partial(_hgrn_kernel, reverse=reverse, final=final, n_ctx_chunks=cf.shape[1] // HG_CHUNK),
        out_shape=jax.ShapeDtypeStruct(q.shape, BF16 if final else F32),
        grid=(b // nb, nw),
        in_specs=in_specs,
        out_specs=blk,
        scratch_shapes=[pltpu.VMEM((nb * HG_HEADS, HG_DK, HG_DK), F32)],
        compiler_params=_params("arbitrary", "arbitrary"),
        name="hgrn_bwd" if reverse else "hgrn_fwd",
    )(*args)


def _s5_weights(lam_re, lam_im, log_dt, b_re, b_im, c_re, c_im):
    hp = lax.Precision.HIGHEST
    g, p, cc, t = S5_GROUPS, S5_STATE, S5_GROUP, S5_T
    lre = jnp.minimum(lam_re.astype(F32), -1e-4)
    lim = lam_im.astype(F32)
    dt = jnp.exp(log_dt.astype(F32))[..., None]
    ks = jnp.arange(t + 1, dtype=F32)[:, None, None, None]
    mag = jnp.exp(ks * (lre * dt)[None])
    pw_re = mag * jnp.cos(ks * (lim * dt)[None])
    pw_im = mag * jnp.sin(ks * (lim * dt)[None])
    nr, ni = pw_re[1] - 1.0, pw_im[1]
    den = lre * lre + lim * lim
    cf_re = (nr * lre + ni * lim) / den
    cf_im = (ni * lre - nr * lim) / den
    bb_re = cf_re[..., None] * b_re - cf_im[..., None] * b_im
    bb_im = cf_re[..., None] * b_im + cf_im[..., None] * b_re
    cre, cim = c_re.astype(F32), c_im.astype(F32)
    sw, ns = S5_WIDTH, 2 * g * p
    grp_of_row = jnp.arange(sw)[:, None] // cc

    cp_re = cre[None, None, :, :, :] * pw_re[:t, :, :, None, :] - cim[None, None] * pw_im[:t, :, :, None, :]
    cp_im = cre[None, None, :, :, :] * pw_im[:t, :, :, None, :] + cim[None, None] * pw_re[:t, :, :, None, :]
    def contract_p(cp, bb):
        return jnp.sum(cp.transpose(4, 1, 0, 2, 3)[..., None] * bb.transpose(2, 0, 1, 3)[:, :, None, :, None, :],
                       axis=0)

    kk = contract_p(cp_re, bb_re) - contract_p(cp_im, bb_im)
    kf, kb = kk[0], kk[1]
    kall = jnp.concatenate([kb[:0:-1], (kf[0] + kb[0])[None], kf[1:]], axis=0)
    kt = kall.transpose(0, 1, 3, 2).reshape(2 * t - 1, sw, cc)

    def spread(x, period, reps):
        sel = (jnp.arange(period)[:, None] == (jnp.arange(period * reps)[None, :] % period)).astype(BF16)
        return jnp.dot(x.astype(BF16), sel, preferred_element_type=BF16)

    same = grp_of_row == (jnp.arange(sw)[None, :] // cc)
    d_lag = jnp.where(same[None], spread(kt, cc, g), 0)

    same_in = jnp.tile(grp_of_row, (t, 1)) == ((jnp.arange(ns)[None, :] % (g * p)) // p)

    def in_to_state(pre, pim, bre, bim):
        xre = pre[..., None] * bre[None] - pim[..., None] * bim[None]
        xim = pre[..., None] * bim[None] + pim[..., None] * bre[None]
        return [spread(xre.transpose(0, 1, 3, 2).reshape(t * sw, p), p, g),
                spread(xim.transpose(0, 1, 3, 2).reshape(t * sw, p), p, g)]

    w_in = jnp.concatenate(in_to_state(pw_re[t - 1::-1, 0], pw_im[t - 1::-1, 0], bb_re[0], bb_im[0])
                           + in_to_state(pw_re[:t, 1], pw_im[:t, 1], bb_re[1], bb_im[1]), axis=1)
    w_in = jnp.where(jnp.tile(same_in, (1, 2)), w_in, 0)

    same_out = ((jnp.arange(ns)[:, None] % (g * p)) // p) == ((jnp.arange(t * sw)[None, :] // cc) % g)
    col = jnp.arange(t * sw)
    pick = (jnp.arange(t * cc)[:, None] == ((col // sw) * cc + col % cc)[None, :]).astype(BF16)

    def state_to_out(pre, pim):
        are = cre[None] * pre[:, :, None, :] - cim[None] * pim[:, :, None, :]
        aim = cre[None] * pim[:, :, None, :] + cim[None] * pre[:, :, None, :]
        a = jnp.concatenate([are.transpose(1, 3, 0, 2), -aim.transpose(1, 3, 0, 2)], axis=0)
        a = jnp.dot(a.reshape(ns, t * cc).astype(BF16), pick, preferred_element_type=BF16)
        return jnp.where(same_out, a, 0)

    w_out_f = state_to_out(pw_re[1:, 0], pw_im[1:, 0])
    w_out_b = state_to_out(pw_re[t:0:-1, 1], pw_im[t:0:-1, 1])

    decay = jnp.stack([pw_re[t].reshape(2, g * p), pw_im[t].reshape(2, g * p)], axis=1)
    return d_lag, w_in, w_out_f, w_out_b, decay


def _s5_in_kernel(u_ref, d_ref, w_ref, o_ref):
    j = pl.program_id(0)

    @pl.when(j < S5_T)
    def _():
        acc = _dot(u_ref[:, 0:S5_WIDTH], d_ref[j + S5_T - 1])
        for s in range(1, S5_T):
            acc = acc + _dot(u_ref[:, s * S5_WIDTH:(s + 1) * S5_WIDTH], d_ref[j - s + S5_T - 1])
        o_ref[...] = acc

    @pl.when(j >= S5_T)
    def _():
        o_ref[...] = _dot(u_ref[...], w_ref[...])


def _s5_in(u, d_lag, w_in, tm):
    m, k = u.shape
    tn = S5_WIDTH
    nj = (k + w_in.shape[1]) // tn
    return pl.pallas_call(
        _s5_in_kernel,
        out_shape=jax.ShapeDtypeStruct((m, nj * tn), F32),
        grid=(nj, m // tm),
        in_specs=[pl.BlockSpec((tm, k), lambda j, i: (i, 0)),
                  pl.BlockSpec(d_lag.shape, lambda j, i: (0, 0, 0)),
                  pl.BlockSpec((k, tn), lambda j, i: (0, jnp.maximum(j - S5_T, 0)))],
        out_specs=pl.BlockSpec((tm, tn), lambda j, i: (i, j)),
        compiler_params=_params("arbitrary", "arbitrary"),
        name="s5_in",
    )(u, d_lag, w_in)


def _s5_scan_kernel(efr_ref, efi_ref, ebr_ref, ebi_ref, a_ref, hfr_ref, hfi_ref, hbr_ref, hbi_ref,
                    *, nb, rows_in, rows_out):
    dirs = ((efr_ref, efi_ref, hfr_ref, hfi_ref, a_ref[0, 0:1, :], a_ref[0, 1:2, :]),
            (ebr_ref, ebi_ref, hbr_ref, hbi_ref, a_ref[1, 0:1, :], a_ref[1, 1:2, :]))
    zero = jnp.zeros_like(dirs[0][4])

    def step(srcs, carry, store):
        new = []
        for di, (er_ref, ei_ref, hr_ref, hi_ref, are, aim) in enumerate(dirs):
            for bi in range(nb):
                hre, him = carry[2 * (di * nb + bi)], carry[2 * (di * nb + bi) + 1]
                if store:
                    hr_ref[pl.ds(bi * rows_out + srcs[di], 1), :] = hre
                    hi_ref[pl.ds(bi * rows_out + srcs[di], 1), :] = him
                ere = er_ref[pl.ds(bi * rows_in + srcs[di], 1), :]
                eim = ei_ref[pl.ds(bi * rows_in + srcs[di], 1), :]
                new += [are * hre - aim * him + ere, are * him + aim * hre + eim]
        return tuple(new)

    n_ctx = rows_in - rows_out
    carry = lax.fori_loop(0, n_ctx, lambda s, c: step((rows_out + s, rows_in - 1 - s), c, False),
                          tuple([zero] * (4 * nb)))
    lax.fori_loop(0, rows_out, lambda s, c: step((s, rows_out - 1 - s), c, True), carry)


def _s5_scan(e, decay, nb, rows_in, rows_out):
    tc = 256
    nsr = S5_GROUPS * S5_STATE
    c0 = (S5_T * S5_WIDTH) // tc
    nt = nsr // tc
    eblk = lambda k: pl.BlockSpec((nb * rows_in, tc), lambda j: (0, c0 + k * nt + j))
    hblk = pl.BlockSpec((nb * rows_out, tc), lambda j: (0, j))
    return pl.pallas_call(
        functools.partial(_s5_scan_kernel, nb=nb, rows_in=rows_in, rows_out=rows_out),
        out_shape=[jax.ShapeDtypeStruct((nb * rows_out, nsr), F32)] * 4,
        grid=(nt,),
        in_specs=[eblk(0), eblk(1), eblk(2), eblk(3), pl.BlockSpec((2, 2, tc), lambda j: (0, 0, j))],
        out_specs=[hblk] * 4,
        compiler_params=_params("arbitrary"),
        name="s5_scan",
    )(e, e, e, e, decay)


def _gelu_tanh(x):
    return 0.5 * x * (1.0 + jnp.tanh(math.sqrt(2.0 / math.pi) * (x + 0.044715 * x * x * x)))


def _s5_out_kernel(hfr_ref, hfi_ref, hbr_ref, hbi_ref, wf_ref, wb_ref, yi_ref, u_ref, d_ref, wg_ref, o_ref):
    nsr = hfr_ref.shape[1]
    y = yi_ref[0] + d_ref[...] * u_ref[...].astype(F32)
    for h_ref, w_ref, r0 in ((hfr_ref, wf_ref, 0), (hfi_ref, wf_ref, nsr), (hbr_ref, wb_ref, 0), (hbi_ref, wb_ref, nsr)):
        y = y + _dot(h_ref[...].astype(BF16), w_ref[r0:r0 + nsr, :])
    y = _gelu_tanh(y)
    gate = _sigmoid(_dot(y.astype(BF16), wg_ref[...]))
    o_ref[...] = (y * gate).astype(o_ref.dtype)


def _s5_out(states, w_out_f, w_out_b, e3, u_rows, d_row, w_glu):
    m, nsr = states[0].shape
    nb = e3.shape[0]
    tm = m // nb
    tn = S5_WIDTH
    st = pl.BlockSpec((tm, nsr), lambda i, j: (i, 0))
    wo = pl.BlockSpec((2 * nsr, tn), lambda i, j: (0, j))
    return pl.pallas_call(
        _s5_out_kernel,
        out_shape=jax.ShapeDtypeStruct((m, S5_T * S5_WIDTH), BF16),
        grid=(nb, S5_T),
        in_specs=[st, st, st, st, wo, wo,
                  pl.BlockSpec((1, tm, tn), lambda i, j: (i, 0, j)),
                  pl.BlockSpec((tm, tn), lambda i, j: (i, j)),
                  pl.BlockSpec((1, tn), lambda i, j: (0, 0)),
                  pl.BlockSpec((tn, tn), lambda i, j: (0, 0))],
        out_specs=pl.BlockSpec((tm, tn), lambda i, j: (i, j)),
        compiler_params=_params("arbitrary", "arbitrary"),
        name="s5_out",
    )(*states, w_out_f, w_out_b, e3, u_rows, d_row, w_glu)


U32 = jnp.uint32
ROW_SUB = 4


def _to_token_rows(ref, val):
    t, d = val.shape

    def rounded(x):
        u = lax.bitcast_convert_type(x, U32)
        return u + (jnp.uint32(0x7FFF) + ((u >> 16) & jnp.uint32(1)))

    w = (rounded(val[:, :d // 2]) >> 16) | (rounded(val[:, d // 2:]) & jnp.uint32(0xFFFF0000))
    for s in range(ROW_SUB):
        ref[pl.ds(s, t, stride=ROW_SUB), :] = w[:, s * LANES:(s + 1) * LANES]


def _from_token_rows(ref, t, row0=0):
    w = jnp.concatenate([ref[pl.ds(row0 * ROW_SUB + s, t, stride=ROW_SUB), :] for s in range(ROW_SUB)], axis=-1)
    lo = lax.bitcast_convert_type(w << 16, F32)
    hi = lax.bitcast_convert_type(w & jnp.uint32(0xFFFF0000), F32)
    return jnp.concatenate([lo, hi], axis=-1)


def _route(h2b, wr_ref, br_ref, cnt_ref, ls8_ref, w8_ref, seg_ref):
    tm = h2b.shape[0]
    per_group = N_EXPERTS // ROUTE_GROUPS
    scores = _sigmoid(lax.dot_general(wr_ref[...], h2b, _NT, preferred_element_type=F32))
    biased = scores + br_ref[...]
    neg = -jnp.inf
    sub = lax.broadcasted_iota(jnp.int32, (per_group, tm), 0)
    grp = []
    for gi in range(ROUTE_GROUPS):
        v = biased[gi * per_group:(gi + 1) * per_group, :]
        m1 = jnp.max(v, axis=0, keepdims=True)
        first = jnp.min(jnp.where(v == m1, sub, per_group), axis=0, keepdims=True)
        m2 = jnp.max(jnp.where(sub == first, neg, v), axis=0, keepdims=True)
        grp.append(m1 + m2)
    grp = jnp.concatenate(grp, axis=0)
    gid = lax.broadcasted_iota(jnp.int32, (ROUTE_GROUPS, tm), 0)
    beaten = jnp.zeros((ROUTE_GROUPS, tm), jnp.int32)
    for gj in range(ROUTE_GROUPS):
        r = grp[gj:gj + 1, :]
        beaten = beaten + jnp.where((r > grp) | ((r == grp) & (gj < gid)), 1, 0)
    group_ok = beaten < TOPK_GROUPS
    expert_ok = jnp.concatenate(
        [jnp.broadcast_to(group_ok[gi:gi + 1, :], (per_group, tm)) for gi in range(ROUTE_GROUPS)], axis=0)
    cur = jnp.where(expert_ok, biased, neg)
    eid = lax.broadcasted_iota(jnp.int32, (N_EXPERTS, tm), 0)
    sel = jnp.zeros((N_EXPERTS, tm), F32)
    picks, wts = [], []
    for _ in range(TOP_K):
        m = jnp.max(cur, axis=0, keepdims=True)
        idx = jnp.min(jnp.where(cur == m, eid, N_EXPERTS), axis=0, keepdims=True)
        hit = eid == idx
        picks.append(idx)
        wts.append(jnp.sum(jnp.where(hit, scores, 0.0), axis=0, keepdims=True))
        sel = jnp.where(hit, 1.0, sel)
        cur = jnp.where(hit, neg, cur)
    wsum = wts[0]
    for w in wts[1:]:
        wsum = wsum + w
    selb = sel.astype(BF16)
    ti = lax.broadcasted_iota(jnp.int32, (tm, tm), 0)
    tj = lax.broadcasted_iota(jnp.int32, (tm, tm), 1)
    rank = _dot(selb, jnp.where(ti < tj, 1.0, 0.0).astype(BF16))
    seg_units = jnp.ceil(jnp.sum(sel, axis=1, keepdims=True) * (1.0 / SEG_ALIGN))
    ei = lax.broadcasted_iota(jnp.int32, (N_EXPERTS, N_EXPERTS), 0)
    ej = lax.broadcasted_iota(jnp.int32, (N_EXPERTS, N_EXPERTS), 1)
    units_row = jnp.broadcast_to(seg_units, (N_EXPERTS, LANES)).astype(BF16)
    seg_off = _dot(jnp.where(ej < ei, 1.0, 0.0).astype(BF16), units_row)[:, 0:1] * SEG_ALIGN
    seg_rows = seg_units * SEG_ALIGN
    slot = seg_off + rank
    for k in range(TOP_K):
        w8_ref[k:k + 1, :] = wts[k] / wsum * ROUTED_SCALE
        ls8_ref[k:k + 1, :] = jnp.sum(jnp.where(eid == picks[k], slot, 0.0), axis=0, keepdims=True).astype(jnp.int32)
    lane = lax.broadcasted_iota(jnp.int32, (N_EXPERTS, LANES), 1)
    seg_ref[0] = jnp.where(lane == 0, cnt_ref[...], jnp.where(lane == 1, seg_rows, seg_off))
    cnt_ref[...] = cnt_ref[...] + seg_rows


def _merge_kernel(x_ref, ya_ref, on_ref, go_ref, ga_ref, gb_ref, g1_ref, sc_ref, sh_ref, n2_ref,
                  pa_ref, pb_ref, wo_ref, wr_ref, br_ref, pt_ref,
                  x1_ref, h2_ref, ls8_ref, w8_ref, seg_ref, cnt_ref, fold_a, fold_b):
    @pl.when(pl.program_id(0) == 0)
    def _():
        cnt_ref[...] = jnp.zeros_like(cnt_ref)

    go = go_ref[...].astype(F32)
    on = _dot(pt_ref[...], on_ref[0].reshape(x_ref.shape[0], HG_WIDTH))
    y_b = (on * (go * _sigmoid(go))).astype(BF16)
    y_a = _unfold_rows(ya_ref[...].astype(F32), fold_a, fold_b).astype(BF16)
    pa = _dot(y_a, pa_ref[...])
    pb = _dot(y_b, pb_ref[...])
    merged = _sigmoid(ga_ref[...].astype(F32)) * pa + _sigmoid(gb_ref[...].astype(F32)) * pb
    x1 = x_ref[...] + g1_ref[0] * _dot(merged.astype(BF16), wo_ref[...])
    x1_ref[...] = x1
    y = x1 * lax.rsqrt(jnp.mean(x1 * x1, axis=-1, keepdims=True) + EPS) * n2_ref[...]
    h2 = y * (1.0 + sc_ref[0]) + sh_ref[0]
    _to_token_rows(h2_ref, h2)
    _route(h2.astype(BF16), wr_ref, br_ref, cnt_ref, ls8_ref, w8_ref, seg_ref)


def _merge(x2d, ya, on, go, ga, gb, g1, sc2, sh2, n2g, pa, pb, wo, wr_t, br, rows_per_batch, tm):
    n, d = x2d.shape
    per = rows_per_batch // tm
    row = lambda wd: pl.BlockSpec((tm, wd), lambda i: (i, 0))
    mod = pl.BlockSpec((1, 1, d), lambda i: (i // per, 0, 0))
    full = lambda a: pl.BlockSpec(a.shape, lambda i: (0, 0))
    tok = pl.BlockSpec((TOP_K, tm), lambda i: (0, i))
    return pl.pallas_call(
        _merge_kernel,
        out_shape=[jax.ShapeDtypeStruct((n, d), F32), jax.ShapeDtypeStruct((n * ROW_SUB, LANES), U32),
                   jax.ShapeDtypeStruct((TOP_K, n), jnp.int32), jax.ShapeDtypeStruct((TOP_K, n), F32),
                   jax.ShapeDtypeStruct((n // tm, N_EXPERTS, LANES), F32),
                   jax.ShapeDtypeStruct((N_EXPERTS, 1), F32)],
        grid=(n // tm,),
        in_specs=[row(d), pl.BlockSpec((tm // S5_T, S5_T * S5_WIDTH), lambda i: (i, 0)),
                  pl.BlockSpec((1, GRID_W, tm // GRID_W, HG_WIDTH), lambda i: (i // per, 0, i % per, 0)),
                  row(HG_WIDTH), row(d), row(d), mod, mod, mod,
                  pl.BlockSpec((1, d), lambda i: (0, 0)), full(pa), full(pb), full(wo), full(wr_t), full(br),
                  pl.BlockSpec((tm, tm), lambda i: (0, 0))],
        out_specs=[row(d), pl.BlockSpec((tm * ROW_SUB, LANES), lambda i: (i, 0)), tok, tok,
                   pl.BlockSpec((1, N_EXPERTS, LANES), lambda i: (i, 0, 0)),
                   pl.BlockSpec((N_EXPERTS, 1), lambda i: (0, 0))],
        scratch_shapes=[pltpu.VMEM((tm, LANES), F32), pltpu.VMEM((tm, LANES), F32)],
        compiler_params=_params("arbitrary"),
        name="merge_out_proj_route",
    )(x2d, ya, on, go, ga, gb, g1, sc2, sh2, n2g.reshape(1, d), pa, pb, wo, wr_t, br,
      _grid_transpose_matrix(tm).T)


MOE_TILE = TOK_TILE
SEG_ALIGN = 8
FILL_ROWS = 512
STAGE_ROWS = MOE_TILE * TOP_K + FILL_ROWS
MOE_BLK = 1024


def _token_row(ref, r):
    return ref.at[pl.ds(pl.multiple_of(r * ROW_SUB, ROW_SUB), ROW_SUB)]


def _wait_rows(any_ref, sem, n_rows):
    view = any_ref.at[pl.ds(0, n_rows * ROW_SUB)]
    pltpu.make_async_copy(view, view, sem).wait()


def _rows(ref, r0, n):
    return ref.at[pl.ds(pl.multiple_of(r0 * ROW_SUB, ROW_SUB), n * ROW_SUB)]


def _pow2_pieces(n, max_piece, fn, min_piece=1):
    done = 0
    piece = max_piece
    while piece >= min_piece:
        hit = (n & piece) != 0
        pl.when(hit)(functools.partial(fn, done, piece))
        done = done + (n & piece)
        piece //= 2


def _copy_rows(src_ref, src0, dst_ref, dst0, n, max_piece, sem, min_piece=1):
    def piece(off, size):
        pltpu.make_async_copy(_rows(src_ref, src0 + off, size), _rows(dst_ref, dst0 + off, size), sem).start()
    _pow2_pieces(n, max_piece, piece, min_piece)


def _wait_copied_rows(src_ref, dst_ref, n, max_piece, sem):
    def piece(off, size):
        pltpu.make_async_copy(_rows(src_ref, 0, size), _rows(dst_ref, 0, size), sem).wait()
    _pow2_pieces(n, max_piece, piece)


def _copy_tile_segments(i, src_ref, src_tab, dst_ref, dst_tab, cnt_ref, off_ref, fill_src0, fill_dst0, sem):
    def per_expert(e, carry):
        _copy_rows(src_ref, src_tab[i, e], dst_ref, dst_tab[i, e], cnt_ref[i, e], MOE_TILE, sem, SEG_ALIGN)
        return carry

    lax.fori_loop(0, N_EXPERTS, per_expert, 0)
    used = off_ref[i, N_EXPERTS - 1] + cnt_ref[i, N_EXPERTS - 1]
    _copy_rows(src_ref, fill_src0(used), dst_ref, fill_dst0(used), STAGE_ROWS - used, FILL_ROWS, sem, SEG_ALIGN)


def _dispatch_kernel(gs_ref, cnt_ref, off_ref, pad_ref, ls_ref, h2_ref, xs_hbm, ls_smem, stage0, stage1, zbuf,
                     sem0, sem1, lsem, zsem, *, tm, n_blocks):
    i = pl.program_id(0)
    last = pl.num_programs(0) - 1
    cp = pltpu.make_async_copy(ls_ref, ls_smem, lsem)
    cp.start()
    trash0 = n_blocks * MOE_BLK

    @pl.when(i == 0)
    def _():
        stage0[...] = jnp.zeros_like(stage0)
        stage1[...] = jnp.zeros_like(stage1)
        zbuf[...] = jnp.zeros_like(zbuf)
        cpz = pltpu.make_async_copy(zbuf, _rows(xs_hbm, trash0, 2 * FILL_ROWS), zsem)
        cpz.start()
        cpz.wait()

    cp.wait()

    def tile(stage, sem, prev_sem, trash):
        def body(t, carry):
            row = h2_ref[pl.ds(pl.multiple_of(t * ROW_SUB, ROW_SUB), ROW_SUB), :]
            for k in range(TOP_K):
                slot = ls_smem[t * TOP_K + k]
                stage[pl.ds(pl.multiple_of(slot * ROW_SUB, ROW_SUB), ROW_SUB), :] = row
            return carry

        lax.fori_loop(0, tm, body, 0)
        _copy_tile_segments(i, stage, off_ref, xs_hbm, gs_ref, cnt_ref, off_ref,
                            lambda used: used, lambda used: trash, sem)

        @pl.when(i > 0)
        def _():
            _wait_rows(xs_hbm, prev_sem, STAGE_ROWS)

        @pl.when(i == last)
        def _():
            _wait_rows(xs_hbm, sem, STAGE_ROWS)

    pl.when(i % 2 == 0)(functools.partial(tile, stage0, sem0, sem1, trash0))
    pl.when(i % 2 == 1)(functools.partial(tile, stage1, sem1, sem0, trash0 + FILL_ROWS))

    @pl.when(i == 0)
    def _():
        def start(e, carry):
            _copy_rows(zbuf, 0, xs_hbm, pad_ref[0, e], pad_ref[1, e], MOE_BLK // 2, zsem)
            return carry

        def wait(e, carry):
            _wait_copied_rows(zbuf, xs_hbm, pad_ref[1, e], MOE_BLK // 2, zsem)
            return carry

        lax.fori_loop(0, N_EXPERTS, start, 0)
        lax.fori_loop(0, N_EXPERTS, wait, 0)

        def zero_block(j, carry):
            pltpu.make_async_copy(zbuf, _rows(xs_hbm, j * MOE_BLK, MOE_BLK), zsem).start()
            return carry

        def wait_block(j, carry):
            pltpu.make_async_copy(zbuf, _rows(xs_hbm, 0, MOE_BLK), zsem).wait()
            return carry

        lax.fori_loop(pad_ref[2, 0], n_blocks, zero_block, 0)
        lax.fori_loop(pad_ref[2, 0], n_blocks, wait_block, 0)


def _dispatch(gstart, seg_cnt, seg_off, pad, ls8, h2_rows, n_blocks, tm):
    n = ls8.shape[0] // TOP_K
    cap = n_blocks * MOE_BLK + 2 * FILL_ROWS
    return pl.pallas_call(
        functools.partial(_dispatch_kernel, tm=tm, n_blocks=n_blocks),
        out_shape=jax.ShapeDtypeStruct((cap * ROW_SUB, LANES), U32),
        grid_spec=pltpu.PrefetchScalarGridSpec(
            num_scalar_prefetch=4,
            grid=(n // tm,),
            in_specs=[pl.BlockSpec((tm * TOP_K,), lambda i, *_: (i,)),
                      pl.BlockSpec((tm * ROW_SUB, LANES), lambda i, *_: (i, 0))],
            out_specs=pl.BlockSpec(memory_space=pl.ANY),
            scratch_shapes=[pltpu.SMEM((tm * TOP_K,), jnp.int32),
                            pltpu.VMEM((STAGE_ROWS * ROW_SUB, LANES), U32),
                            pltpu.VMEM((STAGE_ROWS * ROW_SUB, LANES), U32),
                            pltpu.VMEM((MOE_BLK * ROW_SUB, LANES), U32),
                            pltpu.SemaphoreType.DMA, pltpu.SemaphoreType.DMA, pltpu.SemaphoreType.DMA,
                            pltpu.SemaphoreType.DMA]),
        compiler_params=pltpu.CompilerParams(dimension_semantics=("arbitrary",), vmem_limit_bytes=VMEM_LIMIT,
                                             has_side_effects=True),
        name="moe_dispatch",
    )(gstart, seg_cnt, seg_off, pad, ls8, h2_rows)


def _expert_kernel(be_ref, nu_ref, x_ref, w1_ref, w3_ref, w2_ref, o_ref, w1b, w3b, w2b):
    j = pl.program_id(0)
    e = be_ref[j]
    prev = be_ref[jnp.maximum(j - 1, 0)]
    used = j < nu_ref[0]

    @pl.when(jnp.logical_and(used, jnp.logical_or(j == 0, e != prev)))
    def _():
        w1b[...] = w1_ref[0].astype(BF16)
        w3b[...] = w3_ref[0].astype(BF16)
        w2b[...] = w2_ref[0].astype(BF16)

    @pl.when(used)
    def _():
        x = _from_token_rows(x_ref, MOE_BLK).astype(BF16)
        a = _dot(x, w1b[...])
        hid = (a * _sigmoid(a)) * _dot(x, w3b[...])
        _to_token_rows(o_ref, _dot(hid.astype(BF16), w2b[...]))

    @pl.when(jnp.logical_not(used))
    def _():
        o_ref[...] = jnp.zeros_like(o_ref)


def _experts(block_e, n_used, xs, w1, w3, w2):
    n_blocks = block_e.shape[0]
    d, f = w1.shape[1], w1.shape[2]
    rows = pl.BlockSpec((MOE_BLK * ROW_SUB, LANES), lambda j, be, nu: (j, 0))
    return pl.pallas_call(
        _expert_kernel,
        out_shape=jax.ShapeDtypeStruct((n_blocks * MOE_BLK * ROW_SUB, LANES), U32),
        grid_spec=pltpu.PrefetchScalarGridSpec(
            num_scalar_prefetch=2,
            grid=(n_blocks,),
            in_specs=[rows,
                      pl.BlockSpec((1, d, f), lambda j, be, nu: (be[j], 0, 0)),
                      pl.BlockSpec((1, d, f), lambda j, be, nu: (be[j], 0, 0)),
                      pl.BlockSpec((1, f, d), lambda j, be, nu: (be[j], 0, 0))],
            out_specs=rows,
            scratch_shapes=[pltpu.VMEM((d, f), BF16), pltpu.VMEM((d, f), BF16), pltpu.VMEM((f, d), BF16)]),
        compiler_params=_params("arbitrary"),
        name="moe_experts",
    )(block_e, n_used, xs, w1, w3, w2)


def _combine_kernel(gs_ref, cnt_ref, off_ref, ls_ref, w8_ref, x1_ref, h2_ref, g2_ref, ws1_ref, ws3_ref, ws2_ref,
                    fg_ref, ys_hbm, o_ref, ls_smem, w_smem, gbuf0, gbuf1, acc_rows, sem0, sem1, lsem, *, tm):
    i = pl.program_id(0)
    last = pl.num_programs(0) - 1
    cp1 = pltpu.make_async_copy(ls_ref, ls_smem, lsem)
    cp2 = pltpu.make_async_copy(w8_ref, w_smem, lsem)
    cp1.start()
    cp2.start()

    def fetch(tile, gbuf, sem):
        _copy_tile_segments(tile, ys_hbm, gs_ref, gbuf, off_ref, cnt_ref, off_ref,
                            lambda used: 0, lambda used: used, sem)

    @pl.when(i == 0)
    def _():
        fetch(0, gbuf0, sem0)

    @pl.when(jnp.logical_and(i < last, i % 2 == 0))
    def _():
        fetch(i + 1, gbuf1, sem1)

    @pl.when(jnp.logical_and(i < last, i % 2 == 1))
    def _():
        fetch(i + 1, gbuf0, sem0)

    h2 = _from_token_rows(h2_ref, tm).astype(BF16)
    a = _dot(h2, ws1_ref[...])
    hid = (a * _sigmoid(a)) * _dot(h2, ws3_ref[...])
    acc = _dot(hid.astype(BF16), ws2_ref[...])
    cp1.wait()
    cp2.wait()

    def reduce_rows(gbuf, sem):
        _wait_rows(gbuf, sem, STAGE_ROWS)

        def body(t, carry):
            lo = jnp.zeros((ROW_SUB, LANES), F32)
            hi = jnp.zeros((ROW_SUB, LANES), F32)
            for k in range(TOP_K):
                w = w_smem[t * TOP_K + k]
                words = gbuf[pl.ds(pl.multiple_of(ls_smem[t * TOP_K + k] * ROW_SUB, ROW_SUB), ROW_SUB), :]
                lo = lo + w * lax.bitcast_convert_type(words << 16, F32)
                hi = hi + w * lax.bitcast_convert_type(words & jnp.uint32(0xFFFF0000), F32)
            acc_rows[pl.ds(pl.multiple_of(t * SUBLANES, SUBLANES), ROW_SUB), :] = lo
            acc_rows[pl.ds(pl.multiple_of(t * SUBLANES, SUBLANES) + ROW_SUB, ROW_SUB), :] = hi
            return carry

        lax.fori_loop(0, tm, body, 0)

    pl.when(i % 2 == 0)(functools.partial(reduce_rows, gbuf0, sem0))
    pl.when(i % 2 == 1)(functools.partial(reduce_rows, gbuf1, sem1))
    routed = jnp.concatenate([acc_rows[pl.ds(s, tm, stride=SUBLANES), :] for s in range(SUBLANES)], axis=-1)
    y = x1_ref[...] + g2_ref[0] * (acc + routed)
    o_ref[...] = y * lax.rsqrt(jnp.mean(y * y, axis=-1, keepdims=True) + EPS) * fg_ref[...]


def _combine(gstart, seg_cnt, seg_off, ls8, w8, x1, h2_rows, g2, ws1, ws3, ws2, fg, ys, rows_per_batch, tm):
    n, d = x1.shape
    per = rows_per_batch // tm
    tok = pl.BlockSpec((tm * TOP_K,), lambda i, *_: (i,))
    full = lambda a: pl.BlockSpec(a.shape, lambda i, *_: (0, 0))
    return pl.pallas_call(
        functools.partial(_combine_kernel, tm=tm),
        out_shape=jax.ShapeDtypeStruct((n, d), F32),
        grid_spec=pltpu.PrefetchScalarGridSpec(
            num_scalar_prefetch=3,
            grid=(n // tm,),
            in_specs=[tok, tok, pl.BlockSpec((tm, d), lambda i, *_: (i, 0)),
                      pl.BlockSpec((tm * ROW_SUB, LANES), lambda i, *_: (i, 0)),
                      pl.BlockSpec((1, 1, d), lambda i, *_: (i // per, 0, 0)),
                      full(ws1), full(ws3), full(ws2), pl.BlockSpec((1, d), lambda i, *_: (0, 0)),
                      pl.BlockSpec(memory_space=pl.ANY)],
            out_specs=pl.BlockSpec((tm, d), lambda i, *_: (i, 0)),
            scratch_shapes=[pltpu.SMEM((tm * TOP_K,), jnp.int32), pltpu.SMEM((tm * TOP_K,), F32),
                            pltpu.VMEM((STAGE_ROWS * ROW_SUB, LANES), U32),
                            pltpu.VMEM((STAGE_ROWS * ROW_SUB, LANES), U32),
                            pltpu.VMEM((tm * SUBLANES, LANES), F32), pltpu.SemaphoreType.DMA,
                            pltpu.SemaphoreType.DMA, pltpu.SemaphoreType.DMA]),
        compiler_params=_params("arbitrary"),
        name="moe_combine_final",
    )(gstart, seg_cnt, seg_off, ls8, w8, x1, h2_rows, g2, ws1, ws3, ws2, fg.reshape(1, d), ys)


def _moe_plan(seg, counts, n_assign):
    cnt = counts.reshape(N_EXPERTS).astype(jnp.int32)
    padded = (cnt + MOE_BLK - 1) // MOE_BLK * MOE_BLK
    pends = jnp.cumsum(padded)
    pstarts = pends - padded
    max_rows = n_assign + seg.shape[0] * N_EXPERTS * (SEG_ALIGN - 1)
    n_blocks = (max_rows + N_EXPERTS * (MOE_BLK - 1) + MOE_BLK - 1) // MOE_BLK
    seg = seg[:, :, :3].astype(jnp.int32)
    gstart = pstarts[None, :] + seg[:, :, 0]
    blk_start = jnp.arange(n_blocks, dtype=jnp.int32) * MOE_BLK
    block_e = jnp.minimum(jnp.sum((blk_start[:, None] >= pends[None, :]).astype(jnp.int32), axis=1),
                          N_EXPERTS - 1).astype(jnp.int32)
    n_used = (pends[-1:] // MOE_BLK).astype(jnp.int32)
    pad = jnp.stack([pstarts + cnt, padded - cnt, jnp.broadcast_to(n_used, (N_EXPERTS,))], axis=0).astype(jnp.int32)
    return gstart, seg[:, :, 1], seg[:, :, 2], pad, block_e, n_used, n_blocks


def _mixer(x, c, ctx, c_ctx, w_ada, b_ada, norm1_g, norm2_g, w_in, s5_lam_re, s5_lam_im, s5_log_dt,
           s5_b_re, s5_b_im, s5_c_re, s5_c_im, s5_d, s5_w_glu, lb, hg_norm_g, p_a, p_b, w_out,
           moe_w_router, moe_b_router):
    b, l, d = x.shape
    lc = ctx.shape[1]
    n = b * l
    rows = l // GRID_W

    c8 = jnp.concatenate([c, c_ctx[None], jnp.zeros((8 - b - 1, d), F32)], axis=0)
    mod = _ada(c8, w_ada, b_ada)
    sh1, sc1, g1, sh2, sc2, g2 = [mod[:b, k * d:(k + 1) * d].reshape(b, 1, d) for k in range(6)]
    csh1, csc1 = mod[b:b + 1, 0:d].reshape(1, 1, d), mod[b:b + 1, d:2 * d].reshape(1, 1, d)

    w_in_b = w_in.astype(BF16)
    z = dict(zip([p[0] for p in _IN_PIECES],
                 _inproj(x.reshape(n, d), sc1, sh1, norm1_g, w_in_b, l, TOK_TILE, True)))
    zc = dict(zip([p[0] for p in _IN_PIECES],
                  _inproj(ctx.reshape(b * lc, d), csc1, csh1, norm1_g, w_in_b, lc, lc, False)))

    cx = lambda t: t.reshape(b, lc, HG_WIDTH)
    lb_row = lb.reshape(1, HG_WIDTH)
    o_f = _hgrn_pass(z["q"], z["ff"], z["i"], cx(zc["ff"]), cx(zc["i"]), lb_row, None, None, reverse=False)
    o_n = _hgrn_pass(z["q"], z["fb"], z["i"], cx(zc["fb"]), cx(zc["i"]), lb_row, o_f,
                     hg_norm_g.reshape(1, HG_DK), reverse=True)

    d_lag, w_s5_in, w_out_f, w_out_b, decay = _s5_weights(s5_lam_re, s5_lam_im, s5_log_dt, s5_b_re, s5_b_im,
                                                          s5_c_re, s5_c_im)
    kc, kl = lc // S5_T, l // S5_T
    u_lat = z["u"].reshape(b, kl, S5_T * S5_WIDTH)
    u_ctx = zc["u"].reshape(b, kc, S5_T * S5_WIDTH)
    rows_in = kl + kc
    u_ext = jnp.concatenate([u_lat, u_ctx], axis=1).reshape(b * rows_in, S5_T * S5_WIDTH)
    e = _s5_in(u_ext, d_lag, w_s5_in, (b * rows_in) // 2)
    states = _s5_scan(e, decay, b, rows_in, kl)
    d_row = s5_d.astype(F32).reshape(1, S5_WIDTH)
    y_a = _s5_out(states, w_out_f, w_out_b, e.reshape(b, rows_in, -1), z["u"], d_row, s5_w_glu.astype(BF16))

    return _merge(x.reshape(n, d), y_a, o_n, z["go"], z["ga"], z["gb"], g1, sc2, sh2, norm2_g,
                  p_a.astype(BF16), p_b.astype(BF16), w_out.astype(BF16),
                  moe_w_router.T.astype(BF16), moe_b_router.astype(F32).reshape(N_EXPERTS, 1), l, MOE_TILE) + (g2,)


def kernel(x, c, ctx, c_ctx, w_ada, b_ada, norm1_g, norm2_g, w_in, s5_lam_re, s5_lam_im, s5_log_dt, s5_b_re,
           s5_b_im, s5_c_re, s5_c_im, s5_d, s5_w_glu, hg_lb_logits, hg_norm_g, p_a, p_b, w_out, moe_w_router,
           moe_b_router, moe_w1, moe_w3, moe_w2, moe_ws1, moe_ws3, moe_ws2, final_norm_g):
    b, l, d = x.shape
    n = b * l
    assert w_ada.shape[0] == 1, "single-layer block"
    lb = jnp.cumsum(jax.nn.softmax(hg_lb_logits.astype(F32), axis=0), axis=0)[0]
    x1, h2_rows, ls8, w8, seg, counts, g2 = _mixer(
        x, c, ctx, c_ctx, w_ada[0], b_ada[0], norm1_g[0], norm2_g[0], w_in[0], s5_lam_re[0], s5_lam_im[0],
        s5_log_dt[0], s5_b_re[0], s5_b_im[0], s5_c_re[0], s5_c_im[0], s5_d[0], s5_w_glu[0], lb, hg_norm_g[0],
        p_a[0], p_b[0], w_out[0], moe_w_router[0], moe_b_router[0])
    gstart, seg_cnt, seg_off, pad, block_e, n_used, n_blocks = _moe_plan(seg, counts, n * TOP_K)
    ls_flat, w_flat = ls8.T.reshape(n * TOP_K), w8.T.reshape(n * TOP_K)
    xs = _dispatch(gstart, seg_cnt, seg_off, pad, ls_flat, h2_rows, n_blocks, MOE_TILE)
    ys = _experts(block_e, n_used, xs, moe_w1[0], moe_w3[0], moe_w2[0])
    out = _combine(gstart, seg_cnt, seg_off, ls_flat, w_flat, x1, h2_rows, g2, moe_ws1[0].astype(BF16),
                   moe_ws3[0].astype(BF16), moe_ws2[0].astype(BF16), final_norm_g, ys, l, MOE_TILE)
    return out.reshape(b, l, d)
```

```python
import functools
import math

import jax
import jax.numpy as jnp
from jax import lax
from jax.experimental import pallas as pl
from jax.experimental.pallas import tpu as pltpu

F32 = jnp.float32
BF16 = jnp.bfloat16

GRID_W = 64
S5_WIDTH = 256
S5_GROUP = 16
S5_GROUPS = 16
S5_STATE = 64
HG_HEADS = 6
HG_DK = 128
HG_WIDTH = HG_HEADS * HG_DK
N_EXPERTS = 64
ROUTE_GROUPS = 8
TOPK_GROUPS = 4
TOP_K = 8
ROUTED_SCALE = 2.5
EPS = 1e-6

LANES = 128
SUBLANES = 8

TOK_TILE = 512
S5_T = 16
HG_CHUNK = 64
HG_BATCH = 4
VMEM_LIMIT = 56 * 1024 * 1024

_NT = (((1,), (1,)), ((), ()))
_TN = (((0,), (0,)), ((), ()))


def _params(*sem):
    return pltpu.CompilerParams(dimension_semantics=sem, vmem_limit_bytes=VMEM_LIMIT)


def _dot(a, b):
    return jnp.dot(a, b, preferred_element_type=F32)


def _sigmoid(x):
    return 1.0 / (1.0 + jnp.exp(-x))


def _ada_kernel(c_ref, w_ref, b_ref, o_ref):
    c = c_ref[...]
    s = (c * _sigmoid(c)).astype(BF16)
    o_ref[...] = _dot(s, w_ref[...].astype(BF16)) + b_ref[...]


def _ada(c8, w_ada, b_ada):
    d, n = w_ada.shape
    tn = 1536
    return pl.pallas_call(
        _ada_kernel,
        out_shape=jax.ShapeDtypeStruct((8, n), F32),
        grid=(n // tn,),
        in_specs=[pl.BlockSpec((8, d), lambda j: (0, 0)),
                  pl.BlockSpec((d, tn), lambda j: (0, j)),
                  pl.BlockSpec((1, tn), lambda j: (0, j))],
        out_specs=pl.BlockSpec((8, tn), lambda j: (0, j)),
        compiler_params=_params("arbitrary"),
        name="ada_mod",
    )(c8, w_ada, b_ada.reshape(1, n))


_IN_PIECES = (("u", 0, 256, BF16), ("q", 256, 768, BF16), ("ff", 1024, 768, BF16),
              ("fb", 1792, 768, BF16), ("i", 2560, 768, BF16), ("go", 3328, 768, BF16),
              ("ga", 4096, 1024, BF16), ("gb", 5120, 1024, BF16))


def _fold_rows(val, buf_a, buf_b):
    t = val.shape[0]
    buf_a[...] = val[:, :LANES]
    buf_b[...] = val[:, LANES:]
    pieces = []
    for s in range(S5_T):
        pieces += [buf_a[pl.ds(s, t // S5_T, stride=S5_T), :], buf_b[pl.ds(s, t // S5_T, stride=S5_T), :]]
    return jnp.concatenate(pieces, axis=-1)


def _unfold_rows(val, buf_a, buf_b):
    r = val.shape[0]
    for s in range(S5_T):
        buf_a[pl.ds(s, r, stride=S5_T), :] = val[:, s * S5_WIDTH:s * S5_WIDTH + LANES]
        buf_b[pl.ds(s, r, stride=S5_T), :] = val[:, s * S5_WIDTH + LANES:(s + 1) * S5_WIDTH]
    return jnp.concatenate([buf_a[...], buf_b[...]], axis=-1)


def _grid_transpose_matrix(tm):
    i = jnp.arange(tm)
    src = (i % (tm // GRID_W)) * GRID_W + i // (tm // GRID_W)
    return (src[:, None] == jnp.arange(tm)[None, :]).astype(BF16)


def _inproj_kernel(x_ref, sc_ref, sh_ref, g_ref, w_ref, p_ref, *o_refs):
    o_refs, (fold_a, fold_b) = o_refs[:len(_IN_PIECES)], o_refs[len(_IN_PIECES):]
    x = x_ref[...]
    y = x * lax.rsqrt(jnp.mean(x * x, axis=-1, keepdims=True) + EPS) * g_ref[...]
    h = (y * (1.0 + sc_ref[0]) + sh_ref[0]).astype(BF16)
    h_cm = None
    for (name, a, wd, _), o_ref in zip(_IN_PIECES, o_refs):
        if name == "u":
            o_ref[...] = _fold_rows(_dot(h, w_ref[:, a:a + wd]), fold_a, fold_b).astype(o_ref.dtype)
        elif len(o_ref.shape) == 2:
            o_ref[...] = _dot(h, w_ref[:, a:a + wd]).astype(o_ref.dtype)
        else:
            if h_cm is None:
                h_cm = _dot(p_ref[...], h).astype(BF16)
            o_ref[0] = _dot(h_cm, w_ref[:, a:a + wd]).astype(o_ref.dtype).reshape(o_ref.shape[1:])


_COLMAJOR_PIECES = ("q", "ff", "fb", "i")


def _inproj(x2d, sc, sh, g, w_bf16, rows_per_mod, tm, colmajor):
    n, d = x2d.shape
    per = rows_per_mod // tm
    mod_map = (lambda i: (i // per, 0, 0)) if sc.shape[0] > 1 else (lambda i: (0, 0, 0))
    shapes, specs = [], []
    for name, _, wd, dt in _IN_PIECES:
        if colmajor and name in _COLMAJOR_PIECES:
            shapes.append(jax.ShapeDtypeStruct((n // rows_per_mod, GRID_W, rows_per_mod // GRID_W, wd), dt))
            specs.append(pl.BlockSpec((1, GRID_W, tm // GRID_W, wd), lambda i: (i // per, 0, i % per, 0)))
        elif name == "u":
            shapes.append(jax.ShapeDtypeStruct((n // S5_T, S5_T * wd), dt))
            specs.append(pl.BlockSpec((tm // S5_T, S5_T * wd), lambda i: (i, 0)))
        else:
            shapes.append(jax.ShapeDtypeStruct((n, wd), dt))
            specs.append(pl.BlockSpec((tm, wd), lambda i: (i, 0)))
    return pl.pallas_call(
        _inproj_kernel,
        out_shape=shapes,
        grid=(n // tm,),
        in_specs=[pl.BlockSpec((tm, d), lambda i: (i, 0)),
                  pl.BlockSpec((1, 1, d), mod_map),
                  pl.BlockSpec((1, 1, d), mod_map),
                  pl.BlockSpec((1, d), lambda i: (0, 0)),
                  pl.BlockSpec(w_bf16.shape, lambda i: (0, 0)),
                  pl.BlockSpec((tm, tm), lambda i: (0, 0))],
        out_specs=specs,
        scratch_shapes=[pltpu.VMEM((tm, LANES), F32), pltpu.VMEM((tm, LANES), F32)],
        compiler_params=_params("arbitrary"),
        name="in_proj",
    )(x2d, sc, sh, g.reshape(1, d), w_bf16, _grid_transpose_matrix(tm))


def _hgrn_gates(zf, lb):
    sig = _sigmoid(zf)
    logf = jnp.log(lb + (1.0 - lb) * sig)
    k = (1.0 - lb) * (1.0 - sig)
    return logf, k


def _chunk_cumsum(cs, logf):
    hi = logf.astype(BF16)
    lo = (logf - hi.astype(F32)).astype(BF16)
    return _dot(cs, hi) + _dot(cs, lo)


def _hgrn_state_step(zf, v, lb, st, cs, reverse):
    logf, k = _hgrn_gates(zf, lb)
    cum = _chunk_cumsum(cs, logf)
    t = 0 if reverse else HG_CHUNK - 1
    total = cum[t:t + 1, :]
    kdec = (k * jnp.exp(total - cum)).astype(BF16)
    st_new = st * jnp.exp(total) + lax.dot_general(v.astype(BF16), kdec, _TN, preferred_element_type=F32)
    return cum, k, st_new


def _hgrn_kernel(*refs, reverse, final, n_ctx_chunks):
    if final:
        q_all, f_all, v_all, cf_ref, cv_ref, lb_ref, of_all, g_ref, o_all, st_ref = refs
    else:
        q_all, f_all, v_all, cf_ref, cv_ref, lb_ref, o_all, st_ref = refs
        of_all = None
    n_batch = q_all.shape[0]
    c_len = HG_CHUNK
    n_rows = q_all.shape[2]
    n_chunks = n_rows // c_len
    row = lax.broadcasted_iota(jnp.int32, (n_rows, n_rows), 0)
    col = lax.broadcasted_iota(jnp.int32, (n_rows, n_rows), 1)
    tri = (col >= row) if reverse else (col <= row)
    same_chunk = None
    for c in range(n_chunks):
        lo, hi = c * c_len, (c + 1) * c_len
        blk = (row >= lo) & (row < hi) & (col >= lo) & (col < hi)
        same_chunk = blk if same_chunk is None else (same_chunk | blk)
    mask = tri & same_chunk
    cs = jnp.where(mask, 1.0, 0.0).astype(BF16)

    @pl.when(pl.program_id(1) == 0)
    def _():
        cs1 = cs[:c_len, :c_len]
        order = range(n_ctx_chunks - 1, -1, -1) if reverse else range(n_ctx_chunks)
        for bi in range(n_batch):
            for h in range(HG_HEADS):
                cols = slice(h * HG_DK, (h + 1) * HG_DK)
                st = jnp.zeros((HG_DK, HG_DK), F32)
                for c in order:
                    rows = slice(c * c_len, (c + 1) * c_len)
                    _, _, st = _hgrn_state_step(cf_ref[bi, rows, cols].astype(F32), cv_ref[bi, rows, cols].astype(F32),
                                                lb_ref[:, cols], st, cs1, reverse)
                st_ref[bi * HG_HEADS + h] = st

    def per_chunk_rows(x, r):
        return jnp.concatenate([jnp.broadcast_to(x[c * c_len + r:c * c_len + r + 1, :], (c_len, x.shape[1]))
                                for c in range(n_chunks)], axis=0)

    lb = lb_ref[...]
    r_ref = c_len // 2 - 1 if reverse else c_len // 2
    r_tot = 0 if reverse else c_len - 1
    order = range(n_chunks - 1, -1, -1) if reverse else range(n_chunks)
    for bi in range(n_batch):
        q = q_all[bi, 0].astype(F32)
        v = v_all[bi, 0]
        logf, k = _hgrn_gates(f_all[bi, 0].astype(F32), lb)
        cum = _chunk_cumsum(cs, logf)
        ref = per_chunk_rows(cum, r_ref)
        qe = q * jnp.exp(cum - ref)
        ke = k * jnp.exp(ref - cum)
        qi, ki = qe.astype(BF16), ke.astype(BF16)
        q_in = (qe * jnp.exp(ref)).astype(BF16)
        tail = jnp.exp(per_chunk_rows(cum, r_tot) - ref)
        kdec = (ke * tail).astype(BF16)
        for h in range(HG_HEADS):
            cols = slice(h * HG_DK, (h + 1) * HG_DK)
            s = lax.dot_general(qi[:, cols], ki[:, cols], _NT, preferred_element_type=F32)
            o_intra = _dot(jnp.where(mask, s, 0.0).astype(BF16), v[:, cols])
            st = st_ref[bi * HG_HEADS + h]
            for c in order:
                rows = slice(c * c_len, (c + 1) * c_len)
                o = o_intra[rows] + lax.dot_general(q_in[rows, cols], st.astype(BF16), _NT,
                                                    preferred_element_type=F32)
                total = cum[c * c_len + r_tot:c * c_len + r_tot + 1, cols]
                st = st * jnp.exp(total) + lax.dot_general(v[rows, cols], kdec[rows, cols], _TN,
                                                           preferred_element_type=F32)
                if final:
                    o = o + of_all[bi, 0, rows, cols].astype(F32)
                    o = o * lax.rsqrt(jnp.mean(o * o, axis=-1, keepdims=True) + EPS) * g_ref[...]
                o_all[bi, 0, rows, cols] = o.astype(o_all.dtype)
            st_ref[bi * HG_HEADS + h] = st


def _hgrn_pass(q, f, v, cf, cv, lb, o_prev, g, *, reverse):
    b, nw, rows, _ = q.shape
    nb = HG_BATCH if b % HG_BATCH == 0 else 1
    final = o_prev is not None
    wmap = (lambda bi, w: (bi, nw - 1 - w, 0, 0)) if reverse else (lambda bi, w: (bi, w, 0, 0))
    blk = pl.BlockSpec((nb, 1, rows, HG_WIDTH), wmap)
    cblk = pl.BlockSpec((nb, cf.shape[1], HG_WIDTH), lambda bi, w: (bi, 0, 0))
    in_specs = [blk, blk, blk, cblk, cblk, pl.BlockSpec((1, HG_WIDTH), lambda bi, w: (0, 0))]
    args = [q, f, v, cf, cv, lb]
    if final:
        in_specs += [blk, pl.BlockSpec((1, HG_DK), lambda bi, w: (0, 0))]
        args += [o_prev, g]
    return pl.pallas_call(
        functools.partial(_hgrn_kernel, reverse=reverse, final=final, n_ctx_chunks=cf.shape[1] // HG_CHUNK),
        out_shape=jax.ShapeDtypeStruct(q.shape, BF16),
        grid=(b // nb, nw),
        in_specs=in_specs,
        out_specs=blk,
        scratch_shapes=[pltpu.VMEM((nb * HG_HEADS, HG_DK, HG_DK), F32)],
        compiler_params=_params("arbitrary", "arbitrary"),
        name="hgrn_bwd" if reverse else "hgrn_fwd",
    )(*args)


def _s5_weights(lam_re, lam_im, log_dt, b_re, b_im, c_re, c_im):
    hp = lax.Precision.HIGHEST
    g, p, cc, t = S5_GROUPS, S5_STATE, S5_GROUP, S5_T
    lre = jnp.minimum(lam_re.astype(F32), -1e-4)
    lim = lam_im.astype(F32)
    dt = jnp.exp(log_dt.astype(F32))[..., None]
    ks = jnp.arange(t + 1, dtype=F32)[:, None, None, None]
    mag = jnp.exp(ks * (lre * dt)[None])
    pw_re = mag * jnp.cos(ks * (lim * dt)[None])
    pw_im = mag * jnp.sin(ks * (lim * dt)[None])
    nr, ni = pw_re[1] - 1.0, pw_im[1]
    den = lre * lre + lim * lim
    cf_re = (nr * lre + ni * lim) / den
    cf_im = (ni * lre - nr * lim) / den
    bb_re = cf_re[..., None] * b_re - cf_im[..., None] * b_im
    bb_im = cf_re[..., None] * b_im + cf_im[..., None] * b_re
    cre, cim = c_re.astype(F32), c_im.astype(F32)
    sw, ns = S5_WIDTH, 2 * g * p
    grp_of_row = jnp.arange(sw)[:, None] // cc

    cp_re = cre[None, None, :, :, :] * pw_re[:t, :, :, None, :] - cim[None, None] * pw_im[:t, :, :, None, :]
    cp_im = cre[None, None, :, :, :] * pw_im[:t, :, :, None, :] + cim[None, None] * pw_re[:t, :, :, None, :]
    def contract_p(cp, bb):
        return jnp.sum(cp.transpose(4, 1, 0, 2, 3)[..., None] * bb.transpose(2, 0, 1, 3)[:, :, None, :, None, :],
                       axis=0)

    kk = contract_p(cp_re, bb_re) - contract_p(cp_im, bb_im)
    kf, kb = kk[0], kk[1]
    kall = jnp.concatenate([kb[:0:-1], (kf[0] + kb[0])[None], kf[1:]], axis=0)
    kt = kall.transpose(0, 1, 3, 2).reshape(2 * t - 1, sw, cc)

    def spread(x, period, reps):
        sel = (jnp.arange(period)[:, None] == (jnp.arange(period * reps)[None, :] % period)).astype(BF16)
        return jnp.dot(x.astype(BF16), sel, preferred_element_type=BF16)

    same = grp_of_row == (jnp.arange(sw)[None, :] // cc)
    d_lag = jnp.where(same[None], spread(kt, cc, g), 0)

    same_in = jnp.tile(grp_of_row, (t, 1)) == ((jnp.arange(ns)[None, :] % (g * p)) // p)

    def in_to_state(pre, pim, bre, bim):
        xre = pre[..., None] * bre[None] - pim[..., None] * bim[None]
        xim = pre[..., None] * bim[None] + pim[..., None] * bre[None]
        return [spread(xre.transpose(0, 1, 3, 2).reshape(t * sw, p), p, g),
                spread(xim.transpose(0, 1, 3, 2).reshape(t * sw, p), p, g)]

    w_in = jnp.concatenate(in_to_state(pw_re[t - 1::-1, 0], pw_im[t - 1::-1, 0], bb_re[0], bb_im[0])
                           + in_to_state(pw_re[:t, 1], pw_im[:t, 1], bb_re[1], bb_im[1]), axis=1)
    w_in = jnp.where(jnp.tile(same_in, (1, 2)), w_in, 0)

    same_out = ((jnp.arange(ns)[:, None] % (g * p)) // p) == ((jnp.arange(t * sw)[None, :] // cc) % g)
    col = jnp.arange(t * sw)
    pick = (jnp.arange(t * cc)[:, None] == ((col // sw) * cc + col % cc)[None, :]).astype(BF16)

    def state_to_out(pre, pim):
        are = cre[None] * pre[:, :, None, :] - cim[None] * pim[:, :, None, :]
        aim = cre[None] * pim[:, :, None, :] + cim[None] * pre[:, :, None, :]
        a = jnp.concatenate([are.transpose(1, 3, 0, 2), -aim.transpose(1, 3, 0, 2)], axis=0)
        a = jnp.dot(a.reshape(ns, t * cc).astype(BF16), pick, preferred_element_type=BF16)
        return jnp.where(same_out, a, 0)

    w_out_f = state_to_out(pw_re[1:, 0], pw_im[1:, 0])
    w_out_b = state_to_out(pw_re[t:0:-1, 1], pw_im[t:0:-1, 1])

    decay = jnp.stack([pw_re[t].reshape(2, g * p), pw_im[t].reshape(2, g * p)], axis=1)
    return d_lag, w_in, w_out_f, w_out_b, decay


def _s5_in_kernel(u_ref, d_ref, w_ref, o_ref):
    j = pl.program_id(0)

    @pl.when(j < S5_T)
    def _():
        acc = _dot(u_ref[:, 0:S5_WIDTH], d_ref[j + S5_T - 1])
        for s in range(1, S5_T):
            acc = acc + _dot(u_ref[:, s * S5_WIDTH:(s + 1) * S5_WIDTH], d_ref[j - s + S5_T - 1])
        o_ref[...] = acc

    @pl.when(j >= S5_T)
    def _():
        o_ref[...] = _dot(u_ref[...], w_ref[...])


def _s5_in(u, d_lag, w_in, tm):
    m, k = u.shape
    tn = S5_WIDTH
    nj = (k + w_in.shape[1]) // tn
    return pl.pallas_call(
        _s5_in_kernel,
        out_shape=jax.ShapeDtypeStruct((m, nj * tn), F32),
        grid=(nj, m // tm),
        in_specs=[pl.BlockSpec((tm, k), lambda j, i: (i, 0)),
                  pl.BlockSpec(d_lag.shape, lambda j, i: (0, 0, 0)),
                  pl.BlockSpec((k, tn), lambda j, i: (0, jnp.maximum(j - S5_T, 0)))],
        out_specs=pl.BlockSpec((tm, tn), lambda j, i: (i, j)),
        compiler_params=_params("arbitrary", "arbitrary"),
        name="s5_in",
    )(u, d_lag, w_in)


def _s5_scan_kernel(efr_ref, efi_ref, ebr_ref, ebi_ref, a_ref, hfr_ref, hfi_ref, hbr_ref, hbi_ref,
                    *, nb, rows_in, rows_out):
    dirs = ((efr_ref, efi_ref, hfr_ref, hfi_ref, a_ref[0, 0:1, :], a_ref[0, 1:2, :]),
            (ebr_ref, ebi_ref, hbr_ref, hbi_ref, a_ref[1, 0:1, :], a_ref[1, 1:2, :]))
    zero = jnp.zeros_like(dirs[0][4])

    def step(srcs, carry, store):
        new = []
        for di, (er_ref, ei_ref, hr_ref, hi_ref, are, aim) in enumerate(dirs):
            for bi in range(nb):
                hre, him = carry[2 * (di * nb + bi)], carry[2 * (di * nb + bi) + 1]
                if store:
                    hr_ref[pl.ds(bi * rows_out + srcs[di], 1), :] = hre
                    hi_ref[pl.ds(bi * rows_out + srcs[di], 1), :] = him
                ere = er_ref[pl.ds(bi * rows_in + srcs[di], 1), :]
                eim = ei_ref[pl.ds(bi * rows_in + srcs[di], 1), :]
                new += [are * hre - aim * him + ere, are * him + aim * hre + eim]
        return tuple(new)

    n_ctx = rows_in - rows_out
    carry = lax.fori_loop(0, n_ctx, lambda s, c: step((rows_out + s, rows_in - 1 - s), c, False),
                          tuple([zero] * (4 * nb)))
    lax.fori_loop(0, rows_out, lambda s, c: step((s, rows_out - 1 - s), c, True), carry)


def _s5_scan(e, decay, nb, rows_in, rows_out):
    tc = 256
    nsr = S5_GROUPS * S5_STATE
    c0 = (S5_T * S5_WIDTH) // tc
    nt = nsr // tc
    eblk = lambda k: pl.BlockSpec((nb * rows_in, tc), lambda j: (0, c0 + k * nt + j))
    hblk = pl.BlockSpec((nb * rows_out, tc), lambda j: (0, j))
    return pl.pallas_call(
        functools.partial(_s5_scan_kernel, nb=nb, rows_in=rows_in, rows_out=rows_out),
        out_shape=[jax.ShapeDtypeStruct((nb * rows_out, nsr), F32)] * 4,
        grid=(nt,),
        in_specs=[eblk(0), eblk(1), eblk(2), eblk(3), pl.BlockSpec((2, 2, tc), lambda j: (0, 0, j))],
        out_specs=[hblk] * 4,
        compiler_params=_params("arbitrary"),
        name="s5_scan",
    )(e, e, e, e, decay)


def _gelu_tanh(x):
    return 0.5 * x * (1.0 + jnp.tanh(math.sqrt(2.0 / math.pi) * (x + 0.044715 * x * x * x)))


def _s5_out_kernel(hfr_ref, hfi_ref, hbr_ref, hbi_ref, wf_ref, wb_ref, yi_ref, u_ref, d_ref, wg_ref, o_ref):
    nsr = hfr_ref.shape[1]
    y = yi_ref[0] + d_ref[...] * u_ref[...].astype(F32)
    for h_ref, w_ref, r0 in ((hfr_ref, wf_ref, 0), (hfi_ref, wf_ref, nsr), (hbr_ref, wb_ref, 0), (hbi_ref, wb_ref, nsr)):
        y = y + _dot(h_ref[...].astype(BF16), w_ref[r0:r0 + nsr, :])
    y = _gelu_tanh(y)
    gate = _sigmoid(_dot(y.astype(BF16), wg_ref[...]))
    o_ref[...] = (y * gate).astype(o_ref.dtype)


def _s5_out(states, w_out_f, w_out_b, e3, u_rows, d_row, w_glu):
    m, nsr = states[0].shape
    nb = e3.shape[0]
    tm = m // nb
    tn = S5_WIDTH
    st = pl.BlockSpec((tm, nsr), lambda i, j: (i, 0))
    wo = pl.BlockSpec((2 * nsr, tn), lambda i, j: (0, j))
    return pl.pallas_call(
        _s5_out_kernel,
        out_shape=jax.ShapeDtypeStruct((m, S5_T * S5_WIDTH), BF16),
        grid=(nb, S5_T),
        in_specs=[st, st, st, st, wo, wo,
                  pl.BlockSpec((1, tm, tn), lambda i, j: (i, 0, j)),
                  pl.BlockSpec((tm, tn), lambda i, j: (i, j)),
                  pl.BlockSpec((1, tn), lambda i, j: (0, 0)),
                  pl.BlockSpec((tn, tn), lambda i, j: (0, 0))],
        out_specs=pl.BlockSpec((tm, tn), lambda i, j: (i, j)),
        compiler_params=_params("arbitrary", "arbitrary"),
        name="s5_out",
    )(*states, w_out_f, w_out_b, e3, u_rows, d_row, w_glu)


U32 = jnp.uint32
ROW_SUB = 4


def _to_token_rows(ref, val):
    t, d = val.shape

    def rounded(x):
        u = lax.bitcast_convert_type(x, U32)
        return u + (jnp.uint32(0x7FFF) + ((u >> 16) & jnp.uint32(1)))

    w = (rounded(val[:, :d // 2]) >> 16) | (rounded(val[:, d // 2:]) & jnp.uint32(0xFFFF0000))
    for s in range(ROW_SUB):
        ref[pl.ds(s, t, stride=ROW_SUB), :] = w[:, s * LANES:(s + 1) * LANES]


def _from_token_rows(ref, t, row0=0):
    w = jnp.concatenate([ref[pl.ds(row0 * ROW_SUB + s, t, stride=ROW_SUB), :] for s in range(ROW_SUB)], axis=-1)
    lo = lax.bitcast_convert_type(w << 16, F32)
    hi = lax.bitcast_convert_type(w & jnp.uint32(0xFFFF0000), F32)
    return jnp.concatenate([lo, hi], axis=-1)


def _route(h2b, wr_ref, br_ref, cnt_ref, ls8_ref, w8_ref, seg_ref):
    tm = h2b.shape[0]
    per_group = N_EXPERTS // ROUTE_GROUPS
    scores = _sigmoid(lax.dot_general(wr_ref[...], h2b, _NT, preferred_element_type=F32))
    biased = scores + br_ref[...]
    neg = -jnp.inf
    sub = lax.broadcasted_iota(jnp.int32, (per_group, tm), 0)
    grp = []
    for gi in range(ROUTE_GROUPS):
        v = biased[gi * per_group:(gi + 1) * per_group, :]
        m1 = jnp.max(v, axis=0, keepdims=True)
        first = jnp.min(jnp.where(v == m1, sub, per_group), axis=0, keepdims=True)
        m2 = jnp.max(jnp.where(sub == first, neg, v), axis=0, keepdims=True)
        grp.append(m1 + m2)
    grp = jnp.concatenate(grp, axis=0)
    gid = lax.broadcasted_iota(jnp.int32, (ROUTE_GROUPS, tm), 0)
    beaten = jnp.zeros((ROUTE_GROUPS, tm), jnp.int32)
    for gj in range(ROUTE_GROUPS):
        r = grp[gj:gj + 1, :]
        beaten = beaten + jnp.where((r > grp) | ((r == grp) & (gj < gid)), 1, 0)
    group_ok = beaten < TOPK_GROUPS
    expert_ok = jnp.concatenate(
        [jnp.broadcast_to(group_ok[gi:gi + 1, :], (per_group, tm)) for gi in range(ROUTE_GROUPS)], axis=0)
    cur = jnp.where(expert_ok, biased, neg)
    eid = lax.broadcasted_iota(jnp.int32, (N_EXPERTS, tm), 0)
    sel = jnp.zeros((N_EXPERTS, tm), F32)
    picks, wts = [], []
    for _ in range(TOP_K):
        m = jnp.max(cur, axis=0, keepdims=True)
        idx = jnp.min(jnp.where(cur == m, eid, N_EXPERTS), axis=0, keepdims=True)
        hit = eid == idx
        picks.append(idx)
        wts.append(jnp.sum(jnp.where(hit, scores, 0.0), axis=0, keepdims=True))
        sel = jnp.where(hit, 1.0, sel)
        cur = jnp.where(hit, neg, cur)
    wsum = wts[0]
    for w in wts[1:]:
        wsum = wsum + w
    selb = sel.astype(BF16)
    ti = lax.broadcasted_iota(jnp.int32, (tm, tm), 0)
    tj = lax.broadcasted_iota(jnp.int32, (tm, tm), 1)
    rank = _dot(selb, jnp.where(ti < tj, 1.0, 0.0).astype(BF16))
    seg_units = jnp.ceil(jnp.sum(sel, axis=1, keepdims=True) * (1.0 / SEG_ALIGN))
    ei = lax.broadcasted_iota(jnp.int32, (N_EXPERTS, N_EXPERTS), 0)
    ej = lax.broadcasted_iota(jnp.int32, (N_EXPERTS, N_EXPERTS), 1)
    units_row = jnp.broadcast_to(seg_units, (N_EXPERTS, LANES)).astype(BF16)
    seg_off = _dot(jnp.where(ej < ei, 1.0, 0.0).astype(BF16), units_row)[:, 0:1] * SEG_ALIGN
    seg_rows = seg_units * SEG_ALIGN
    slot = seg_off + rank
    for k in range(TOP_K):
        w8_ref[k:k + 1, :] = wts[k] / wsum * ROUTED_SCALE
        ls8_ref[k:k + 1, :] = jnp.sum(jnp.where(eid == picks[k], slot, 0.0), axis=0, keepdims=True).astype(jnp.int32)
    lane = lax.broadcasted_iota(jnp.int32, (N_EXPERTS, LANES), 1)
    seg_ref[0] = jnp.where(lane == 0, cnt_ref[...], jnp.where(lane == 1, seg_rows, seg_off))
    cnt_ref[...] = cnt_ref[...] + seg_rows


def _merge_kernel(x_ref, ya_ref, on_ref, go_ref, ga_ref, gb_ref, g1_ref, sc_ref, sh_ref, n2_ref,
                  pa_ref, pb_ref, wo_ref, wr_ref, br_ref, pt_ref,
                  x1_ref, h2_ref, ls8_ref, w8_ref, seg_ref, cnt_ref, fold_a, fold_b):
    @pl.when(pl.program_id(0) == 0)
    def _():
        cnt_ref[...] = jnp.zeros_like(cnt_ref)

    go = go_ref[...].astype(F32)
    on = _dot(pt_ref[...], on_ref[0].reshape(x_ref.shape[0], HG_WIDTH))
    y_b = (on * (go * _sigmoid(go))).astype(BF16)
    y_a = _unfold_rows(ya_ref[...].astype(F32), fold_a, fold_b).astype(BF16)
    pa = _dot(y_a, pa_ref[...])
    pb = _dot(y_b, pb_ref[...])
    merged = _sigmoid(ga_ref[...].astype(F32)) * pa + _sigmoid(gb_ref[...].astype(F32)) * pb
    x1 = x_ref[...] + g1_ref[0] * _dot(merged.astype(BF16), wo_ref[...])
    x1_ref[...] = x1
    y = x1 * lax.rsqrt(jnp.mean(x1 * x1, axis=-1, keepdims=True) + EPS) * n2_ref[...]
    h2 = y * (1.0 + sc_ref[0]) + sh_ref[0]
    _to_token_rows(h2_ref, h2)
    _route(h2.astype(BF16), wr_ref, br_ref, cnt_ref, ls8_ref, w8_ref, seg_ref)


def _merge(x2d, ya, on, go, ga, gb, g1, sc2, sh2, n2g, pa, pb, wo, wr_t, br, rows_per_batch, tm):
    n, d = x2d.shape
    per = rows_per_batch // tm
    row = lambda wd: pl.BlockSpec((tm, wd), lambda i: (i, 0))
    mod = pl.BlockSpec((1, 1, d), lambda i: (i // per, 0, 0))
    full = lambda a: pl.BlockSpec(a.shape, lambda i: (0, 0))
    tok = pl.BlockSpec((TOP_K, tm), lambda i: (0, i))
    return pl.pallas_call(
        _merge_kernel,
        out_shape=[jax.ShapeDtypeStruct((n, d), F32), jax.ShapeDtypeStruct((n * ROW_SUB, LANES), U32),
                   jax.ShapeDtypeStruct((TOP_K, n), jnp.int32), jax.ShapeDtypeStruct((TOP_K, n), F32),
                   jax.ShapeDtypeStruct((n // tm, N_EXPERTS, LANES), F32),
                   jax.ShapeDtypeStruct((N_EXPERTS, 1), F32)],
        grid=(n // tm,),
        in_specs=[row(d), pl.BlockSpec((tm // S5_T, S5_T * S5_WIDTH), lambda i: (i, 0)),
                  pl.BlockSpec((1, GRID_W, tm // GRID_W, HG_WIDTH), lambda i: (i // per, 0, i % per, 0)),
                  row(HG_WIDTH), row(d), row(d), mod, mod, mod,
                  pl.BlockSpec((1, d), lambda i: (0, 0)), full(pa), full(pb), full(wo), full(wr_t), full(br),
                  pl.BlockSpec((tm, tm), lambda i: (0, 0))],
        out_specs=[row(d), pl.BlockSpec((tm * ROW_SUB, LANES), lambda i: (i, 0)), tok, tok,
                   pl.BlockSpec((1, N_EXPERTS, LANES), lambda i: (i, 0, 0)),
                   pl.BlockSpec((N_EXPERTS, 1), lambda i: (0, 0))],
        scratch_shapes=[pltpu.VMEM((tm, LANES), F32), pltpu.VMEM((tm, LANES), F32)],
        compiler_params=_params("arbitrary"),
        name="merge_out_proj_route",
    )(x2d, ya, on, go, ga, gb, g1, sc2, sh2, n2g.reshape(1, d), pa, pb, wo, wr_t, br,
      _grid_transpose_matrix(tm).T)


MOE_TILE = TOK_TILE
SEG_ALIGN = 8
FILL_ROWS = 512
STAGE_ROWS = MOE_TILE * TOP_K + FILL_ROWS
MOE_BLK = 1024


def _token_row(ref, r):
    return ref.at[pl.ds(pl.multiple_of(r * ROW_SUB, ROW_SUB), ROW_SUB)]


def _wait_rows(any_ref, sem, n_rows):
    view = any_ref.at[pl.ds(0, n_rows * ROW_SUB)]
    pltpu.make_async_copy(view, view, sem).wait()


def _rows(ref, r0, n):
    return ref.at[pl.ds(pl.multiple_of(r0 * ROW_SUB, ROW_SUB), n * ROW_SUB)]


def _pow2_pieces(n, max_piece, fn, min_piece=1):
    done = 0
    piece = max_piece
    while piece >= min_piece:
        hit = (n & piece) != 0
        pl.when(hit)(functools.partial(fn, done, piece))
        done = done + (n & piece)
        piece //= 2


def _copy_rows(src_ref, src0, dst_ref, dst0, n, max_piece, sem, min_piece=1):
    def piece(off, size):
        pltpu.make_async_copy(_rows(src_ref, src0 + off, size), _rows(dst_ref, dst0 + off, size), sem).start()
    _pow2_pieces(n, max_piece, piece, min_piece)


def _wait_copied_rows(src_ref, dst_ref, n, max_piece, sem):
    def piece(off, size):
        pltpu.make_async_copy(_rows(src_ref, 0, size), _rows(dst_ref, 0, size), sem).wait()
    _pow2_pieces(n, max_piece, piece)


def _copy_tile_segments(i, src_ref, src_tab, dst_ref, dst_tab, cnt_ref, off_ref, fill_src0, fill_dst0, sem):
    def per_expert(e, carry):
        _copy_rows(src_ref, src_tab[i, e], dst_ref, dst_tab[i, e], cnt_ref[i, e], MOE_TILE, sem, SEG_ALIGN)
        return carry

    lax.fori_loop(0, N_EXPERTS, per_expert, 0)
    used = off_ref[i, N_EXPERTS - 1] + cnt_ref[i, N_EXPERTS - 1]
    _copy_rows(src_ref, fill_src0(used), dst_ref, fill_dst0(used), STAGE_ROWS - used, FILL_ROWS, sem, SEG_ALIGN)


def _dispatch_kernel(gs_ref, cnt_ref, off_ref, pad_ref, ls_ref, h2_ref, xs_hbm, ls_smem, stage0, stage1, zbuf,
                     sem0, sem1, lsem, zsem, *, tm, n_blocks):
    i = pl.program_id(0)
    last = pl.num_programs(0) - 1
    cp = pltpu.make_async_copy(ls_ref, ls_smem, lsem)
    cp.start()
    trash0 = n_blocks * MOE_BLK

    @pl.when(i == 0)
    def _():
        stage0[...] = jnp.zeros_like(stage0)
        stage1[...] = jnp.zeros_like(stage1)
        zbuf[...] = jnp.zeros_like(zbuf)
        cpz = pltpu.make_async_copy(zbuf, _rows(xs_hbm, trash0, 2 * FILL_ROWS), zsem)
        cpz.start()
        cpz.wait()

    cp.wait()

    def tile(stage, sem, prev_sem, trash):
        def body(t, carry):
            row = h2_ref[pl.ds(pl.multiple_of(t * ROW_SUB, ROW_SUB), ROW_SUB), :]
            for k in range(TOP_K):
                slot = ls_smem[t * TOP_K + k]
                stage[pl.ds(pl.multiple_of(slot * ROW_SUB, ROW_SUB), ROW_SUB), :] = row
            return carry

        lax.fori_loop(0, tm, body, 0)
        _copy_tile_segments(i, stage, off_ref, xs_hbm, gs_ref, cnt_ref, off_ref,
                            lambda used: used, lambda used: trash, sem)

        @pl.when(i > 0)
        def _():
            _wait_rows(xs_hbm, prev_sem, STAGE_ROWS)

        @pl.when(i == last)
        def _():
            _wait_rows(xs_hbm, sem, STAGE_ROWS)

    pl.when(i % 2 == 0)(functools.partial(tile, stage0, sem0, sem1, trash0))
    pl.when(i % 2 == 1)(functools.partial(tile, stage1, sem1, sem0, trash0 + FILL_ROWS))

    @pl.when(i == 0)
    def _():
        def start(e, carry):
            _copy_rows(zbuf, 0, xs_hbm, pad_ref[0, e], pad_ref[1, e], MOE_BLK // 2, zsem)
            return carry

        def wait(e, carry):
            _wait_copied_rows(zbuf, xs_hbm, pad_ref[1, e], MOE_BLK // 2, zsem)
            return carry

        lax.fori_loop(0, N_EXPERTS, start, 0)
        lax.fori_loop(0, N_EXPERTS, wait, 0)

        def zero_block(j, carry):
            pltpu.make_async_copy(zbuf, _rows(xs_hbm, j * MOE_BLK, MOE_BLK), zsem).start()
            return carry

        def wait_block(j, carry):
            pltpu.make_async_copy(zbuf, _rows(xs_hbm, 0, MOE_BLK), zsem).wait()
            return carry

        lax.fori_loop(pad_ref[2, 0], n_blocks, zero_block, 0)
        lax.fori_loop(pad_ref[2, 0], n_blocks, wait_block, 0)


def _dispatch(gstart, seg_cnt, seg_off, pad, ls8, h2_rows, n_blocks, tm):
    n = ls8.shape[0] // TOP_K
    cap = n_blocks * MOE_BLK + 2 * FILL_ROWS
    return pl.pallas_call(
        functools.partial(_dispatch_kernel, tm=tm, n_blocks=n_blocks),
        out_shape=jax.ShapeDtypeStruct((cap * ROW_SUB, LANES), U32),
        grid_spec=pltpu.PrefetchScalarGridSpec(
            num_scalar_prefetch=4,
            grid=(n // tm,),
            in_specs=[pl.BlockSpec((tm * TOP_K,), lambda i, *_: (i,)),
                      pl.BlockSpec((tm * ROW_SUB, LANES), lambda i, *_: (i, 0))],
            out_specs=pl.BlockSpec(memory_space=pl.ANY),
            scratch_shapes=[pltpu.SMEM((tm * TOP_K,), jnp.int32),
                            pltpu.VMEM((STAGE_ROWS * ROW_SUB, LANES), U32),
                            pltpu.VMEM((STAGE_ROWS * ROW_SUB, LANES), U32),
                            pltpu.VMEM((MOE_BLK * ROW_SUB, LANES), U32),
                            pltpu.SemaphoreType.DMA, pltpu.SemaphoreType.DMA, pltpu.SemaphoreType.DMA,
                            pltpu.SemaphoreType.DMA]),
        compiler_params=pltpu.CompilerParams(dimension_semantics=("arbitrary",), vmem_limit_bytes=VMEM_LIMIT,
                                             has_side_effects=True),
        name="moe_dispatch",
    )(gstart, seg_cnt, seg_off, pad, ls8, h2_rows)


def _expert_kernel(be_ref, nu_ref, x_ref, w1_ref, w3_ref, w2_ref, o_ref, w1b, w3b, w2b):
    j = pl.program_id(0)
    e = be_ref[j]
    prev = be_ref[jnp.maximum(j - 1, 0)]
    used = j < nu_ref[0]

    @pl.when(jnp.logical_and(used, jnp.logical_or(j == 0, e != prev)))
    def _():
        w1b[...] = w1_ref[0].astype(BF16)
        w3b[...] = w3_ref[0].astype(BF16)
        w2b[...] = w2_ref[0].astype(BF16)

    @pl.when(used)
    def _():
        x = _from_token_rows(x_ref, MOE_BLK).astype(BF16)
        a = _dot(x, w1b[...])
        hid = (a * _sigmoid(a)) * _dot(x, w3b[...])
        _to_token_rows(o_ref, _dot(hid.astype(BF16), w2b[...]))

    @pl.when(jnp.logical_not(used))
    def _():
        o_ref[...] = jnp.zeros_like(o_ref)


def _experts(block_e, n_used, xs, w1, w3, w2):
    n_blocks = block_e.shape[0]
    d, f = w1.shape[1], w1.shape[2]
    rows = pl.BlockSpec((MOE_BLK * ROW_SUB, LANES), lambda j, be, nu: (j, 0))
    rows_in = pl.BlockSpec((MOE_BLK * ROW_SUB, LANES), lambda j, be, nu: (jnp.minimum(j, nu[0] - 1), 0))
    return pl.pallas_call(
        _expert_kernel,
        out_shape=jax.ShapeDtypeStruct((n_blocks * MOE_BLK * ROW_SUB, LANES), U32),
        grid_spec=pltpu.PrefetchScalarGridSpec(
            num_scalar_prefetch=2,
            grid=(n_blocks,),
            in_specs=[rows_in,
                      pl.BlockSpec((1, d, f), lambda j, be, nu: (be[j], 0, 0)),
                      pl.BlockSpec((1, d, f), lambda j, be, nu: (be[j], 0, 0)),
                      pl.BlockSpec((1, f, d), lambda j, be, nu: (be[j], 0, 0))],
            out_specs=rows,
            scratch_shapes=[pltpu.VMEM((d, f), BF16), pltpu.VMEM((d, f), BF16), pltpu.VMEM((f, d), BF16)]),
        compiler_params=_params("arbitrary"),
        name="moe_experts",
    )(block_e, n_used, xs, w1, w3, w2)


def _combine_kernel(gs_ref, cnt_ref, off_ref, ls_ref, w8_ref, x1_ref, h2_ref, g2_ref, ws1_ref, ws3_ref, ws2_ref,
                    fg_ref, ys_hbm, o_ref, ls_smem, w_smem, gbuf0, gbuf1, acc_rows, sem0, sem1, lsem, *, tm):
    i = pl.program_id(0)
    last = pl.num_programs(0) - 1
    cp1 = pltpu.make_async_copy(ls_ref, ls_smem, lsem)
    cp2 = pltpu.make_async_copy(w8_ref, w_smem, lsem)
    cp1.start()
    cp2.start()

    def fetch(tile, gbuf, sem):
        _copy_tile_segments(tile, ys_hbm, gs_ref, gbuf, off_ref, cnt_ref, off_ref,
                            lambda used: 0, lambda used: used, sem)

    @pl.when(i == 0)
    def _():
        fetch(0, gbuf0, sem0)

    @pl.when(jnp.logical_and(i < last, i % 2 == 0))
    def _():
        fetch(i + 1, gbuf1, sem1)

    @pl.when(jnp.logical_and(i < last, i % 2 == 1))
    def _():
        fetch(i + 1, gbuf0, sem0)

    h2 = _from_token_rows(h2_ref, tm).astype(BF16)
    a = _dot(h2, ws1_ref[...])
    hid = (a * _sigmoid(a)) * _dot(h2, ws3_ref[...])
    acc = _dot(hid.astype(BF16), ws2_ref[...])
    cp1.wait()
    cp2.wait()

    def reduce_rows(gbuf, sem):
        _wait_rows(gbuf, sem, STAGE_ROWS)

        def body(t, carry):
            lo = jnp.zeros((ROW_SUB, LANES), F32)
            hi = jnp.zeros((ROW_SUB, LANES), F32)
            for k in range(TOP_K):
                w = w_smem[t * TOP_K + k]
                words = gbuf[pl.ds(pl.multiple_of(ls_smem[t * TOP_K + k] * ROW_SUB, ROW_SUB), ROW_SUB), :]
                lo = lo + w * lax.bitcast_convert_type(words << 16, F32)
                hi = hi + w * lax.bitcast_convert_type(words & jnp.uint32(0xFFFF0000), F32)
            acc_rows[pl.ds(pl.multiple_of(t * SUBLANES, SUBLANES), ROW_SUB), :] = lo
            acc_rows[pl.ds(pl.multiple_of(t * SUBLANES, SUBLANES) + ROW_SUB, ROW_SUB), :] = hi
            return carry

        lax.fori_loop(0, tm, body, 0)

    pl.when(i % 2 == 0)(functools.partial(reduce_rows, gbuf0, sem0))
    pl.when(i % 2 == 1)(functools.partial(reduce_rows, gbuf1, sem1))
    routed = jnp.concatenate([acc_rows[pl.ds(s, tm, stride=SUBLANES), :] for s in range(SUBLANES)], axis=-1)
    y = x1_ref[...] + g2_ref[0] * (acc + routed)
    o_ref[...] = y * lax.rsqrt(jnp.mean(y * y, axis=-1, keepdims=True) + EPS) * fg_ref[...]


def _combine(gstart, seg_cnt, seg_off, ls8, w8, x1, h2_rows, g2, ws1, ws3, ws2, fg, ys, rows_per_batch, tm):
    n, d = x1.shape
    per = rows_per_batch // tm
    tok = pl.BlockSpec((tm * TOP_K,), lambda i, *_: (i,))
    full = lambda a: pl.BlockSpec(a.shape, lambda i, *_: (0, 0))
    return pl.pallas_call(
        functools.partial(_combine_kernel, tm=tm),
        out_shape=jax.ShapeDtypeStruct((n, d), F32),
        grid_spec=pltpu.PrefetchScalarGridSpec(
            num_scalar_prefetch=3,
            grid=(n // tm,),
            in_specs=[tok, tok, pl.BlockSpec((tm, d), lambda i, *_: (i, 0)),
                      pl.BlockSpec((tm * ROW_SUB, LANES), lambda i, *_: (i, 0)),
                      pl.BlockSpec((1, 1, d), lambda i, *_: (i // per, 0, 0)),
                      full(ws1), full(ws3), full(ws2), pl.BlockSpec((1, d), lambda i, *_: (0, 0)),
                      pl.BlockSpec(memory_space=pl.ANY)],
            out_specs=pl.BlockSpec((tm, d), lambda i, *_: (i, 0)),
            scratch_shapes=[pltpu.SMEM((tm * TOP_K,), jnp.int32), pltpu.SMEM((tm * TOP_K,), F32),
                            pltpu.VMEM((STAGE_ROWS * ROW_SUB, LANES), U32),
                            pltpu.VMEM((STAGE_ROWS * ROW_SUB, LANES), U32),
                            pltpu.VMEM((tm * SUBLANES, LANES), F32), pltpu.SemaphoreType.DMA,
                            pltpu.SemaphoreType.DMA, pltpu.SemaphoreType.DMA]),
        compiler_params=_params("arbitrary"),
        name="moe_combine_final",
    )(gstart, seg_cnt, seg_off, ls8, w8, x1, h2_rows, g2, ws1, ws3, ws2, fg.reshape(1, d), ys)


def _moe_plan(seg, counts, n_assign):
    cnt = counts.reshape(N_EXPERTS).astype(jnp.int32)
    padded = (cnt + MOE_BLK - 1) // MOE_BLK * MOE_BLK
    pends = jnp.cumsum(padded)
    pstarts = pends - padded
    max_rows = n_assign + seg.shape[0] * N_EXPERTS * (SEG_ALIGN - 1)
    n_blocks = (max_rows + N_EXPERTS * (MOE_BLK - 1) + MOE_BLK - 1) // MOE_BLK
    seg = seg[:, :, :3].astype(jnp.int32)
    gstart = pstarts[None, :] + seg[:, :, 0]
    blk_start = jnp.arange(n_blocks, dtype=jnp.int32) * MOE_BLK
    block_e = jnp.minimum(jnp.sum((blk_start[:, None] >= pends[None, :]).astype(jnp.int32), axis=1),
                          N_EXPERTS - 1).astype(jnp.int32)
    n_used = (pends[-1:] // MOE_BLK).astype(jnp.int32)
    pad = jnp.stack([pstarts + cnt, padded - cnt, jnp.broadcast_to(n_used, (N_EXPERTS,))], axis=0).astype(jnp.int32)
    return gstart, seg[:, :, 1], seg[:, :, 2], pad, block_e, n_used, n_blocks


def _mixer(x, c, ctx, c_ctx, w_ada, b_ada, norm1_g, norm2_g, w_in, s5_lam_re, s5_lam_im, s5_log_dt,
           s5_b_re, s5_b_im, s5_c_re, s5_c_im, s5_d, s5_w_glu, lb, hg_norm_g, p_a, p_b, w_out,
           moe_w_router, moe_b_router):
    b, l, d = x.shape
    lc = ctx.shape[1]
    n = b * l
    rows = l // GRID_W

    c8 = jnp.concatenate([c, c_ctx[None], jnp.zeros((8 - b - 1, d), F32)], axis=0)
    mod = _ada(c8, w_ada, b_ada)
    sh1, sc1, g1, sh2, sc2, g2 = [mod[:b, k * d:(k + 1) * d].reshape(b, 1, d) for k in range(6)]
    csh1, csc1 = mod[b:b + 1, 0:d].reshape(1, 1, d), mod[b:b + 1, d:2 * d].reshape(1, 1, d)

    w_in_b = w_in.astype(BF16)
    z = dict(zip([p[0] for p in _IN_PIECES],
                 _inproj(x.reshape(n, d), sc1, sh1, norm1_g, w_in_b, l, TOK_TILE, True)))
    zc = dict(zip([p[0] for p in _IN_PIECES],
                  _inproj(ctx.reshape(b * lc, d), csc1, csh1, norm1_g, w_in_b, lc, lc, False)))

    cx = lambda t: t.reshape(b, lc, HG_WIDTH)
    lb_row = lb.reshape(1, HG_WIDTH)
    o_f = _hgrn_pass(z["q"], z["ff"], z["i"], cx(zc["ff"]), cx(zc["i"]), lb_row, None, None, reverse=False)
    o_n = _hgrn_pass(z["q"], z["fb"], z["i"], cx(zc["fb"]), cx(zc["i"]), lb_row, o_f,
                     hg_norm_g.reshape(1, HG_DK), reverse=True)

    d_lag, w_s5_in, w_out_f, w_out_b, decay = _s5_weights(s5_lam_re, s5_lam_im, s5_log_dt, s5_b_re, s5_b_im,
                                                          s5_c_re, s5_c_im)
    kc, kl = lc // S5_T, l // S5_T
    u_lat = z["u"].reshape(b, kl, S5_T * S5_WIDTH)
    u_ctx = zc["u"].reshape(b, kc, S5_T * S5_WIDTH)
    rows_in = kl + kc
    u_ext = jnp.concatenate([u_lat, u_ctx], axis=1).reshape(b * rows_in, S5_T * S5_WIDTH)
    e = _s5_in(u_ext, d_lag, w_s5_in, (b * rows_in) // 2)
    states = _s5_scan(e, decay, b, rows_in, kl)
    d_row = s5_d.astype(F32).reshape(1, S5_WIDTH)
    y_a = _s5_out(states, w_out_f, w_out_b, e.reshape(b, rows_in, -1), z["u"], d_row, s5_w_glu.astype(BF16))

    return _merge(x.reshape(n, d), y_a, o_n, z["go"], z["ga"], z["gb"], g1, sc2, sh2, norm2_g,
                  p_a.astype(BF16), p_b.astype(BF16), w_out.astype(BF16),
                  moe_w_router.T.astype(BF16), moe_b_router.astype(F32).reshape(N_EXPERTS, 1), l, MOE_TILE) + (g2,)


def kernel(x, c, ctx, c_ctx, w_ada, b_ada, norm1_g, norm2_g, w_in, s5_lam_re, s5_lam_im, s5_log_dt, s5_b_re,
           s5_b_im, s5_c_re, s5_c_im, s5_d, s5_w_glu, hg_lb_logits, hg_norm_g, p_a, p_b, w_out, moe_w_router,
           moe_b_router, moe_w1, moe_w3, moe_w2, moe_ws1, moe_ws3, moe_ws2, final_norm_g):
    b, l, d = x.shape
    n = b * l
    assert w_ada.shape[0] == 1, "single-layer block"
    lb = jnp.cumsum(jax.nn.softmax(hg_lb_logits.astype(F32), axis=0), axis=0)[0]
    x1, h2_rows, ls8, w8, seg, counts, g2 = _mixer(
        x, c, ctx, c_ctx, w_ada[0], b_ada[0], norm1_g[0], norm2_g[0], w_in[0], s5_lam_re[0], s5_lam_im[0],
        s5_log_dt[0], s5_b_re[0], s5_b_im[0], s5_c_re[0], s5_c_im[0], s5_d[0], s5_w_glu[0], lb, hg_norm_g[0],
        p_a[0], p_b[0], w_out[0], moe_w_router[0], moe_b_router[0])
    gstart, seg_cnt, seg_off, pad, block_e, n_used, n_blocks = _moe_plan(seg, counts, n * TOP_K)
    ls_flat, w_flat = ls8.T.reshape(n * TOP_K), w8.T.reshape(n * TOP_K)
    xs = _dispatch(gstart, seg_cnt, seg_off, pad, ls_flat, h2_rows, n_blocks, MOE_TILE)
    ys = _experts(block_e, n_used, xs, moe_w1[0], moe_w3[0], moe_w2[0])
    out = _combine(gstart, seg_cnt, seg_off, ls_flat, w_flat, x1, h2_rows, g2, moe_ws1[0].astype(BF16),
                   moe_ws3[0].astype(BF16), moe_ws2[0].astype(BF16), final_norm_g, ys, l, MOE_TILE)
    return out.reshape(b, l, d)
```

```python
import functools
import math

import jax
import jax.numpy as jnp
from jax import lax
from jax.experimental import pallas as pl
from jax.experimental.pallas import tpu as pltpu

F32 = jnp.float32
BF16 = jnp.bfloat16

GRID_W = 64
S5_WIDTH = 256
S5_GROUP = 16
S5_GROUPS = 16
S5_STATE = 64
HG_HEADS = 6
HG_DK = 128
HG_WIDTH = HG_HEADS * HG_DK
N_EXPERTS = 64
ROUTE_GROUPS = 8
TOPK_GROUPS = 4
TOP_K = 8
ROUTED_SCALE = 2.5
EPS = 1e-6

LANES = 128
SUBLANES = 8

TOK_TILE = 512
S5_T = 16
HG_CHUNK = 64
HG_BATCH = 4
VMEM_LIMIT = 56 * 1024 * 1024

_NT = (((1,), (1,)), ((), ()))
_TN = (((0,), (0,)), ((), ()))


def _params(*sem):
    return pltpu.CompilerParams(dimension_semantics=sem, vmem_limit_bytes=VMEM_LIMIT)


def _dot(a, b):
    return jnp.dot(a, b, preferred_element_type=F32)


def _sigmoid(x):
    return 1.0 / (1.0 + jnp.exp(-x))


def _ada_kernel(c_ref, w_ref, b_ref, o_ref):
    c = c_ref[...]
    s = (c * _sigmoid(c)).astype(BF16)
    o_ref[...] = _dot(s, w_ref[...].astype(BF16)) + b_ref[...]


def _ada(c8, w_ada, b_ada):
    d, n = w_ada.shape
    tn = 1536
    return pl.pallas_call(
        _ada_kernel,
        out_shape=jax.ShapeDtypeStruct((8, n), F32),
        grid=(n // tn,),
        in_specs=[pl.BlockSpec((8, d), lambda j: (0, 0)),
                  pl.BlockSpec((d, tn), lambda j: (0, j)),
                  pl.BlockSpec((1, tn), lambda j: (0, j))],
        out_specs=pl.BlockSpec((8, tn), lambda j: (0, j)),
        compiler_params=_params("arbitrary"),
        name="ada_mod",
    )(c8, w_ada, b_ada.reshape(1, n))


_IN_PIECES = (("u", 0, 256, BF16), ("q", 256, 768, BF16), ("ff", 1024, 768, BF16),
              ("fb", 1792, 768, BF16), ("i", 2560, 768, BF16), ("go", 3328, 768, BF16),
              ("ga", 4096, 1024, BF16), ("gb", 5120, 1024, BF16))


def _fold_rows(val, buf_a, buf_b):
    t = val.shape[0]
    buf_a[...] = val[:, :LANES]
    buf_b[...] = val[:, LANES:]
    pieces = []
    for s in range(S5_T):
        pieces += [buf_a[pl.ds(s, t // S5_T, stride=S5_T), :], buf_b[pl.ds(s, t // S5_T, stride=S5_T), :]]
    return jnp.concatenate(pieces, axis=-1)


def _unfold_rows(val, buf_a, buf_b):
    r = val.shape[0]
    for s in range(S5_T):
        buf_a[pl.ds(s, r, stride=S5_T), :] = val[:, s * S5_WIDTH:s * S5_WIDTH + LANES]
        buf_b[pl.ds(s, r, stride=S5_T), :] = val[:, s * S5_WIDTH + LANES:(s + 1) * S5_WIDTH]
    return jnp.concatenate([buf_a[...], buf_b[...]], axis=-1)


def _grid_transpose_matrix(tm):
    i = jnp.arange(tm)
    src = (i % (tm // GRID_W)) * GRID_W + i // (tm // GRID_W)
    return (src[:, None] == jnp.arange(tm)[None, :]).astype(BF16)


def _inproj_kernel(x_ref, sc_ref, sh_ref, g_ref, w_ref, p_ref, *o_refs):
    o_refs, (fold_a, fold_b) = o_refs[:len(_IN_PIECES)], o_refs[len(_IN_PIECES):]
    x = x_ref[...]
    y = x * lax.rsqrt(jnp.mean(x * x, axis=-1, keepdims=True) + EPS) * g_ref[...]
    h = (y * (1.0 + sc_ref[0]) + sh_ref[0]).astype(BF16)
    h_cm = None
    for (name, a, wd, _), o_ref in zip(_IN_PIECES, o_refs):
        if name == "u":
            o_ref[...] = _fold_rows(_dot(h, w_ref[:, a:a + wd]), fold_a, fold_b).astype(o_ref.dtype)
        elif len(o_ref.shape) == 2:
            o_ref[...] = _dot(h, w_ref[:, a:a + wd]).astype(o_ref.dtype)
        else:
            if h_cm is None:
                h_cm = _dot(p_ref[...], h).astype(BF16)
            o_ref[0] = _dot(h_cm, w_ref[:, a:a + wd]).astype(o_ref.dtype).reshape(o_ref.shape[1:])


_COLMAJOR_PIECES = ("q", "ff", "fb", "i")


def _inproj(x2d, sc, sh, g, w_bf16, rows_per_mod, tm, colmajor):
    n, d = x2d.shape
    per = rows_per_mod // tm
    mod_map = (lambda i: (i // per, 0, 0)) if sc.shape[0] > 1 else (lambda i: (0, 0, 0))
    shapes, specs = [], []
    for name, _, wd, dt in _IN_PIECES:
        if colmajor and name in _COLMAJOR_PIECES:
            shapes.append(jax.ShapeDtypeStruct((n // rows_per_mod, GRID_W, rows_per_mod // GRID_W, wd), dt))
            specs.append(pl.BlockSpec((1, GRID_W, tm // GRID_W, wd), lambda i: (i // per, 0, i % per, 0)))
        elif name == "u":
            shapes.append(jax.ShapeDtypeStruct((n // S5_T, S5_T * wd), dt))
            specs.append(pl.BlockSpec((tm // S5_T, S5_T * wd), lambda i: (i, 0)))
        else:
            shapes.append(jax.ShapeDtypeStruct((n, wd), dt))
            specs.append(pl.BlockSpec((tm, wd), lambda i: (i, 0)))
    return pl.pallas_call(
        _inproj_kernel,
        out_shape=shapes,
        grid=(n // tm,),
        in_specs=[pl.BlockSpec((tm, d), lambda i: (i, 0)),
                  pl.BlockSpec((1, 1, d), mod_map),
                  pl.BlockSpec((1, 1, d), mod_map),
                  pl.BlockSpec((1, d), lambda i: (0, 0)),
                  pl.BlockSpec(w_bf16.shape, lambda i: (0, 0)),
                  pl.BlockSpec((tm, tm), lambda i: (0, 0))],
        out_specs=specs,
        scratch_shapes=[pltpu.VMEM((tm, LANES), F32), pltpu.VMEM((tm, LANES), F32)],
        compiler_params=_params("arbitrary"),
        name="in_proj",
    )(x2d, sc, sh, g.reshape(1, d), w_bf16, _grid_transpose_matrix(tm))


def _hgrn_gates(zf, lb):
    sig = _sigmoid(zf)
    logf = jnp.log(lb + (1.0 - lb) * sig)
    k = (1.0 - lb) * (1.0 - sig)
    return logf, k


def _chunk_cumsum(cs, logf):
    hi = logf.astype(BF16)
    lo = (logf - hi.astype(F32)).astype(BF16)
    return _dot(cs, hi) + _dot(cs, lo)


def _hgrn_state_step(zf, v, lb, st, cs, reverse):
    logf, k = _hgrn_gates(zf, lb)
    cum = _chunk_cumsum(cs, logf)
    t = 0 if reverse else HG_CHUNK - 1
    total = cum[t:t + 1, :]
    kdec = (k * jnp.exp(total - cum)).astype(BF16)
    st_new = st * jnp.exp(total) + lax.dot_general(v.astype(BF16), kdec, _TN, preferred_element_type=F32)
    return cum, k, st_new


def _hgrn_kernel(*refs, reverse, final, n_ctx_chunks):
    if final:
        q_all, f_all, v_all, cf_ref, cv_ref, lb_ref, of_all, g_ref, o_all, st_ref = refs
    else:
        q_all, f_all, v_all, cf_ref, cv_ref, lb_ref, o_all, st_ref = refs
        of_all = None
    n_batch = q_all.shape[0]
    c_len = HG_CHUNK
    n_rows = q_all.shape[2]
    n_chunks = n_rows // c_len
    row = lax.broadcasted_iota(jnp.int32, (n_rows, n_rows), 0)
    col = lax.broadcasted_iota(jnp.int32, (n_rows, n_rows), 1)
    tri = (col >= row) if reverse else (col <= row)
    same_chunk = None
    for c in range(n_chunks):
        lo, hi = c * c_len, (c + 1) * c_len
        blk = (row >= lo) & (row < hi) & (col >= lo) & (col < hi)
        same_chunk = blk if same_chunk is None else (same_chunk | blk)
    mask = tri & same_chunk
    cs = jnp.where(mask, 1.0, 0.0).astype(BF16)

    @pl.when(pl.program_id(1) == 0)
    def _():
        cs1 = cs[:c_len, :c_len]
        order = range(n_ctx_chunks - 1, -1, -1) if reverse else range(n_ctx_chunks)
        for bi in range(n_batch):
            for h in range(HG_HEADS):
                cols = slice(h * HG_DK, (h + 1) * HG_DK)
                st = jnp.zeros((HG_DK, HG_DK), F32)
                for c in order:
                    rows = slice(c * c_len, (c + 1) * c_len)
                    _, _, st = _hgrn_state_step(cf_ref[bi, rows, cols].astype(F32), cv_ref[bi, rows, cols].astype(F32),
                                                lb_ref[:, cols], st, cs1, reverse)
                st_ref[bi * HG_HEADS + h] = st

    def chunk_rows(x, r):
        return [x[c * c_len + r:c * c_len + r + 1, :] for c in range(n_chunks)]

    def over_chunks(rows):
        return jnp.concatenate([jnp.broadcast_to(r, (c_len, r.shape[1])) for r in rows], axis=0)

    lb = lb_ref[...]
    r_ref = c_len // 2 - 1 if reverse else c_len // 2
    r_tot = 0 if reverse else c_len - 1
    order = range(n_chunks - 1, -1, -1) if reverse else range(n_chunks)
    for bi in range(n_batch):
        q = q_all[bi, 0].astype(F32)
        v = v_all[bi, 0]
        logf, k = _hgrn_gates(f_all[bi, 0].astype(F32), lb)
        cum = _chunk_cumsum(cs, logf)
        ref_rows, tot_rows = chunk_rows(cum, r_ref), chunk_rows(cum, r_tot)
        ref = over_chunks(ref_rows)
        qe = q * jnp.exp(cum - ref)
        ke = k * jnp.exp(ref - cum)
        qi, ki = qe.astype(BF16), ke.astype(BF16)
        q_in = (qe * over_chunks([jnp.exp(r) for r in ref_rows])).astype(BF16)
        kdec = (ke * over_chunks([jnp.exp(t - r) for t, r in zip(tot_rows, ref_rows)])).astype(BF16)
        for h in range(HG_HEADS):
            cols = slice(h * HG_DK, (h + 1) * HG_DK)
            s = lax.dot_general(qi[:, cols], ki[:, cols], _NT, preferred_element_type=F32)
            o_intra = _dot(jnp.where(mask, s, 0.0).astype(BF16), v[:, cols])
            st = st_ref[bi * HG_HEADS + h]
            for c in order:
                rows = slice(c * c_len, (c + 1) * c_len)
                o = o_intra[rows] + lax.dot_general(q_in[rows, cols], st.astype(BF16), _NT,
                                                    preferred_element_type=F32)
                total = cum[c * c_len + r_tot:c * c_len + r_tot + 1, cols]
                st = st * jnp.exp(total) + lax.dot_general(v[rows, cols], kdec[rows, cols], _TN,
                                                           preferred_element_type=F32)
                if final:
                    o = o + of_all[bi, 0, rows, cols].astype(F32)
                    o = o * lax.rsqrt(jnp.mean(o * o, axis=-1, keepdims=True) + EPS) * g_ref[...]
                o_all[bi, 0, rows, cols] = o.astype(o_all.dtype)
            st_ref[bi * HG_HEADS + h] = st


def _hgrn_pass(q, f, v, cf, cv, lb, o_prev, g, *, reverse):
    b, nw, rows, _ = q.shape
    nb = HG_BATCH if b % HG_BATCH == 0 else 1
    final = o_prev is not None
    wmap = (lambda bi, w: (bi, nw - 1 - w, 0, 0)) if reverse else (lambda bi, w: (bi, w, 0, 0))
    blk = pl.BlockSpec((nb, 1, rows, HG_WIDTH), wmap)
    cblk = pl.BlockSpec((nb, cf.shape[1], HG_WIDTH), lambda bi, w: (bi, 0, 0))
    in_specs = [blk, blk, blk, cblk, cblk, pl.BlockSpec((1, HG_WIDTH), lambda bi, w: (0, 0))]
    args = [q, f, v, cf, cv, lb]
    if final:
        in_specs += [blk, pl.BlockSpec((1, HG_DK), lambda bi, w: (0, 0))]
        args += [o_prev, g]
    return pl.pallas_call(
        functools.partial(_hgrn_kernel, reverse=reverse, final=final, n_ctx_chunks=cf.shape[1] // HG_CHUNK),
        out_shape=jax.ShapeDtypeStruct(q.shape, BF16),
        grid=(b // nb, nw),
        in_specs=in_specs,
        out_specs=blk,
        scratch_shapes=[pltpu.VMEM((nb * HG_HEADS, HG_DK, HG_DK), F32)],
        compiler_params=_params("arbitrary", "arbitrary"),
        name="hgrn_bwd" if reverse else "hgrn_fwd",
    )(*args)


def _s5_weights(lam_re, lam_im, log_dt, b_re, b_im, c_re, c_im):
    hp = lax.Precision.HIGHEST
    g, p, cc, t = S5_GROUPS, S5_STATE, S5_GROUP, S5_T
    lre = jnp.minimum(lam_re.astype(F32), -1e-4)
    lim = lam_im.astype(F32)
    dt = jnp.exp(log_dt.astype(F32))[..., None]
    ks = jnp.arange(t + 1, dtype=F32)[:, None, None, None]
    mag = jnp.exp(ks * (lre * dt)[None])
    pw_re = mag * jnp.cos(ks * (lim * dt)[None])
    pw_im = mag * jnp.sin(ks * (lim * dt)[None])
    nr, ni = pw_re[1] - 1.0, pw_im[1]
    den = lre * lre + lim * lim
    cf_re = (nr * lre + ni * lim) / den
    cf_im = (ni * lre - nr * lim) / den
    bb_re = cf_re[..., None] * b_re - cf_im[..., None] * b_im
    bb_im = cf_re[..., None] * b_im + cf_im[..., None] * b_re
    cre, cim = c_re.astype(F32), c_im.astype(F32)
    sw, ns = S5_WIDTH, 2 * g * p
    grp_of_row = jnp.arange(sw)[:, None] // cc

    cp_re = cre[None, None, :, :, :] * pw_re[:t, :, :, None, :] - cim[None, None] * pw_im[:t, :, :, None, :]
    cp_im = cre[None, None, :, :, :] * pw_im[:t, :, :, None, :] + cim[None, None] * pw_re[:t, :, :, None, :]
    def contract_p(cp, bb):
        return jnp.sum(cp.transpose(4, 1, 0, 2, 3)[..., None] * bb.transpose(2, 0, 1, 3)[:, :, None, :, None, :],
                       axis=0)

    kk = contract_p(cp_re, bb_re) - contract_p(cp_im, bb_im)
    kf, kb = kk[0], kk[1]
    kall = jnp.concatenate([kb[:0:-1], (kf[0] + kb[0])[None], kf[1:]], axis=0)
    kt = kall.transpose(0, 1, 3, 2).reshape(2 * t - 1, sw, cc)

    def spread(x, period, reps):
        sel = (jnp.arange(period)[:, None] == (jnp.arange(period * reps)[None, :] % period)).astype(BF16)
        return jnp.dot(x.astype(BF16), sel, preferred_element_type=BF16)

    same = grp_of_row == (jnp.arange(sw)[None, :] // cc)
    d_lag = jnp.where(same[None], spread(kt, cc, g), 0)

    same_in = jnp.tile(grp_of_row, (t, 1)) == ((jnp.arange(ns)[None, :] % (g * p)) // p)

    def in_to_state(pre, pim, bre, bim):
        xre = pre[..., None] * bre[None] - pim[..., None] * bim[None]
        xim = pre[..., None] * bim[None] + pim[..., None] * bre[None]
        return [spread(xre.transpose(0, 1, 3, 2).reshape(t * sw, p), p, g),
                spread(xim.transpose(0, 1, 3, 2).reshape(t * sw, p), p, g)]

    w_in = jnp.concatenate(in_to_state(pw_re[t - 1::-1, 0], pw_im[t - 1::-1, 0], bb_re[0], bb_im[0])
                           + in_to_state(pw_re[:t, 1], pw_im[:t, 1], bb_re[1], bb_im[1]), axis=1)
    w_in = jnp.where(jnp.tile(same_in, (1, 2)), w_in, 0)

    same_out = ((jnp.arange(ns)[:, None] % (g * p)) // p) == ((jnp.arange(t * sw)[None, :] // cc) % g)
    col = jnp.arange(t * sw)
    pick = (jnp.arange(t * cc)[:, None] == ((col // sw) * cc + col % cc)[None, :]).astype(BF16)

    def state_to_out(pre, pim):
        are = cre[None] * pre[:, :, None, :] - cim[None] * pim[:, :, None, :]
        aim = cre[None] * pim[:, :, None, :] + cim[None] * pre[:, :, None, :]
        a = jnp.concatenate([are.transpose(1, 3, 0, 2), -aim.transpose(1, 3, 0, 2)], axis=0)
        a = jnp.dot(a.reshape(ns, t * cc).astype(BF16), pick, preferred_element_type=BF16)
        return jnp.where(same_out, a, 0)

    w_out_f = state_to_out(pw_re[1:, 0], pw_im[1:, 0])
    w_out_b = state_to_out(pw_re[t:0:-1, 1], pw_im[t:0:-1, 1])

    decay = jnp.stack([pw_re[t].reshape(2, g * p), pw_im[t].reshape(2, g * p)], axis=1)
    return d_lag, w_in, w_out_f, w_out_b, decay


def _s5_in_kernel(u_ref, d_ref, w_ref, o_ref):
    j = pl.program_id(0)

    @pl.when(j < S5_T)
    def _():
        acc = _dot(u_ref[:, 0:S5_WIDTH], d_ref[j + S5_T - 1])
        for s in range(1, S5_T):
            acc = acc + _dot(u_ref[:, s * S5_WIDTH:(s + 1) * S5_WIDTH], d_ref[j - s + S5_T - 1])
        o_ref[...] = acc

    @pl.when(j >= S5_T)
    def _():
        o_ref[...] = _dot(u_ref[...], w_ref[...])


def _s5_in(u, d_lag, w_in, tm):
    m, k = u.shape
    tn = S5_WIDTH
    nj = (k + w_in.shape[1]) // tn
    return pl.pallas_call(
        _s5_in_kernel,
        out_shape=jax.ShapeDtypeStruct((m, nj * tn), F32),
        grid=(nj, m // tm),
        in_specs=[pl.BlockSpec((tm, k), lambda j, i: (i, 0)),
                  pl.BlockSpec(d_lag.shape, lambda j, i: (0, 0, 0)),
                  pl.BlockSpec((k, tn), lambda j, i: (0, jnp.maximum(j - S5_T, 0)))],
        out_specs=pl.BlockSpec((tm, tn), lambda j, i: (i, j)),
        compiler_params=_params("arbitrary", "arbitrary"),
        name="s5_in",
    )(u, d_lag, w_in)


def _s5_scan_kernel(efr_ref, efi_ref, ebr_ref, ebi_ref, a_ref, hfr_ref, hfi_ref, hbr_ref, hbi_ref,
                    *, nb, rows_in, rows_out):
    dirs = ((efr_ref, efi_ref, hfr_ref, hfi_ref, a_ref[0, 0:1, :], a_ref[0, 1:2, :]),
            (ebr_ref, ebi_ref, hbr_ref, hbi_ref, a_ref[1, 0:1, :], a_ref[1, 1:2, :]))
    zero = jnp.zeros_like(dirs[0][4])

    def step(srcs, carry, store):
        new = []
        for di, (er_ref, ei_ref, hr_ref, hi_ref, are, aim) in enumerate(dirs):
            for bi in range(nb):
                hre, him = carry[2 * (di * nb + bi)], carry[2 * (di * nb + bi) + 1]
                if store:
                    hr_ref[pl.ds(bi * rows_out + srcs[di], 1), :] = hre
                    hi_ref[pl.ds(bi * rows_out + srcs[di], 1), :] = him
                ere = er_ref[pl.ds(bi * rows_in + srcs[di], 1), :]
                eim = ei_ref[pl.ds(bi * rows_in + srcs[di], 1), :]
                new += [are * hre - aim * him + ere, are * him + aim * hre + eim]
        return tuple(new)

    n_ctx = rows_in - rows_out
    carry = lax.fori_loop(0, n_ctx, lambda s, c: step((rows_out + s, rows_in - 1 - s), c, False),
                          tuple([zero] * (4 * nb)))
    lax.fori_loop(0, rows_out, lambda s, c: step((s, rows_out - 1 - s), c, True), carry)


def _s5_scan(e, decay, nb, rows_in, rows_out):
    tc = 256
    nsr = S5_GROUPS * S5_STATE
    c0 = (S5_T * S5_WIDTH) // tc
    nt = nsr // tc
    eblk = lambda k: pl.BlockSpec((nb * rows_in, tc), lambda j: (0, c0 + k * nt + j))
    hblk = pl.BlockSpec((nb * rows_out, tc), lambda j: (0, j))
    return pl.pallas_call(
        functools.partial(_s5_scan_kernel, nb=nb, rows_in=rows_in, rows_out=rows_out),
        out_shape=[jax.ShapeDtypeStruct((nb * rows_out, nsr), F32)] * 4,
        grid=(nt,),
        in_specs=[eblk(0), eblk(1), eblk(2), eblk(3), pl.BlockSpec((2, 2, tc), lambda j: (0, 0, j))],
        out_specs=[hblk] * 4,
        compiler_params=_params("arbitrary"),
        name="s5_scan",
    )(e, e, e, e, decay)


def _gelu_tanh(x):
    return 0.5 * x * (1.0 + jnp.tanh(math.sqrt(2.0 / math.pi) * (x + 0.044715 * x * x * x)))


def _s5_out_kernel(hfr_ref, hfi_ref, hbr_ref, hbi_ref, wf_ref, wb_ref, yi_ref, u_ref, d_ref, wg_ref, o_ref):
    nsr = hfr_ref.shape[1]
    y = yi_ref[0] + d_ref[...] * u_ref[...].astype(F32)
    for h_ref, w_ref, r0 in ((hfr_ref, wf_ref, 0), (hfi_ref, wf_ref, nsr), (hbr_ref, wb_ref, 0), (hbi_ref, wb_ref, nsr)):
        y = y + _dot(h_ref[...].astype(BF16), w_ref[r0:r0 + nsr, :])
    y = _gelu_tanh(y)
    gate = _sigmoid(_dot(y.astype(BF16), wg_ref[...]))
    o_ref[...] = (y * gate).astype(o_ref.dtype)


def _s5_out(states, w_out_f, w_out_b, e3, u_rows, d_row, w_glu):
    m, nsr = states[0].shape
    nb = e3.shape[0]
    tm = m // nb
    tn = S5_WIDTH
    st = pl.BlockSpec((tm, nsr), lambda i, j: (i, 0))
    wo = pl.BlockSpec((2 * nsr, tn), lambda i, j: (0, j))
    return pl.pallas_call(
        _s5_out_kernel,
        out_shape=jax.ShapeDtypeStruct((m, S5_T * S5_WIDTH), BF16),
        grid=(nb, S5_T),
        in_specs=[st, st, st, st, wo, wo,
                  pl.BlockSpec((1, tm, tn), lambda i, j: (i, 0, j)),
                  pl.BlockSpec((tm, tn), lambda i, j: (i, j)),
                  pl.BlockSpec((1, tn), lambda i, j: (0, 0)),
                  pl.BlockSpec((tn, tn), lambda i, j: (0, 0))],
        out_specs=pl.BlockSpec((tm, tn), lambda i, j: (i, j)),
        compiler_params=_params("arbitrary", "arbitrary"),
        name="s5_out",
    )(*states, w_out_f, w_out_b, e3, u_rows, d_row, w_glu)


U32 = jnp.uint32
ROW_SUB = 4


def _to_token_rows(ref, val):
    t, d = val.shape

    def rounded(x):
        u = lax.bitcast_convert_type(x, U32)
        return u + (jnp.uint32(0x7FFF) + ((u >> 16) & jnp.uint32(1)))

    w = (rounded(val[:, :d // 2]) >> 16) | (rounded(val[:, d // 2:]) & jnp.uint32(0xFFFF0000))
    for s in range(ROW_SUB):
        ref[pl.ds(s, t, stride=ROW_SUB), :] = w[:, s * LANES:(s + 1) * LANES]


def _from_token_rows(ref, t, row0=0):
    w = jnp.concatenate([ref[pl.ds(row0 * ROW_SUB + s, t, stride=ROW_SUB), :] for s in range(ROW_SUB)], axis=-1)
    lo = lax.bitcast_convert_type(w << 16, F32)
    hi = lax.bitcast_convert_type(w & jnp.uint32(0xFFFF0000), F32)
    return jnp.concatenate([lo, hi], axis=-1)


def _route(h2b, wr_ref, br_ref, cnt_ref, ls8_ref, w8_ref, seg_ref):
    tm = h2b.shape[0]
    per_group = N_EXPERTS // ROUTE_GROUPS
    scores = _sigmoid(lax.dot_general(wr_ref[...], h2b, _NT, preferred_element_type=F32))
    biased = scores + br_ref[...]
    neg = -jnp.inf
    sub = lax.broadcasted_iota(jnp.int32, (per_group, tm), 0)
    grp = []
    for gi in range(ROUTE_GROUPS):
        v = biased[gi * per_group:(gi + 1) * per_group, :]
        m1 = jnp.max(v, axis=0, keepdims=True)
        first = jnp.min(jnp.where(v == m1, sub, per_group), axis=0, keepdims=True)
        m2 = jnp.max(jnp.where(sub == first, neg, v), axis=0, keepdims=True)
        grp.append(m1 + m2)
    grp = jnp.concatenate(grp, axis=0)
    gid = lax.broadcasted_iota(jnp.int32, (ROUTE_GROUPS, tm), 0)
    beaten = jnp.zeros((ROUTE_GROUPS, tm), jnp.int32)
    for gj in range(ROUTE_GROUPS):
        r = grp[gj:gj + 1, :]
        beaten = beaten + jnp.where((r > grp) | ((r == grp) & (gj < gid)), 1, 0)
    group_ok = beaten < TOPK_GROUPS
    expert_ok = jnp.concatenate(
        [jnp.broadcast_to(group_ok[gi:gi + 1, :], (per_group, tm)) for gi in range(ROUTE_GROUPS)], axis=0)
    cur = jnp.where(expert_ok, biased, neg)
    eid = lax.broadcasted_iota(jnp.int32, (N_EXPERTS, tm), 0)
    sel = jnp.zeros((N_EXPERTS, tm), F32)
    picks, wts = [], []
    for _ in range(TOP_K):
        m = jnp.max(cur, axis=0, keepdims=True)
        idx = jnp.min(jnp.where(cur == m, eid, N_EXPERTS), axis=0, keepdims=True)
        hit = eid == idx
        picks.append(idx)
        wts.append(jnp.sum(jnp.where(hit, scores, 0.0), axis=0, keepdims=True))
        sel = jnp.where(hit, 1.0, sel)
        cur = jnp.where(hit, neg, cur)
    wsum = wts[0]
    for w in wts[1:]:
        wsum = wsum + w
    selb = sel.astype(BF16)
    ti = lax.broadcasted_iota(jnp.int32, (tm, tm), 0)
    tj = lax.broadcasted_iota(jnp.int32, (tm, tm), 1)
    rank = _dot(selb, jnp.where(ti < tj, 1.0, 0.0).astype(BF16))
    seg_units = jnp.ceil(jnp.sum(sel, axis=1, keepdims=True) * (1.0 / SEG_ALIGN))
    ei = lax.broadcasted_iota(jnp.int32, (N_EXPERTS, N_EXPERTS), 0)
    ej = lax.broadcasted_iota(jnp.int32, (N_EXPERTS, N_EXPERTS), 1)
    units_row = jnp.broadcast_to(seg_units, (N_EXPERTS, LANES)).astype(BF16)
    seg_off = _dot(jnp.where(ej < ei, 1.0, 0.0).astype(BF16), units_row)[:, 0:1] * SEG_ALIGN
    seg_rows = seg_units * SEG_ALIGN
    slot = seg_off + rank
    for k in range(TOP_K):
        w8_ref[k:k + 1, :] = wts[k] / wsum * ROUTED_SCALE
        ls8_ref[k:k + 1, :] = (jnp.sum(jnp.where(eid == picks[k], slot, 0.0), axis=0, keepdims=True)
                               * ROW_SUB).astype(jnp.int32)
    lane = lax.broadcasted_iota(jnp.int32, (N_EXPERTS, LANES), 1)
    seg_ref[0] = jnp.where(lane == 0, cnt_ref[...], jnp.where(lane == 1, seg_rows, seg_off))
    cnt_ref[...] = cnt_ref[...] + seg_rows


def _merge_kernel(x_ref, ya_ref, on_ref, go_ref, ga_ref, gb_ref, g1_ref, sc_ref, sh_ref, n2_ref,
                  pa_ref, pb_ref, wo_ref, wr_ref, br_ref, pt_ref,
                  x1_ref, h2_ref, ls8_ref, w8_ref, seg_ref, cnt_ref, fold_a, fold_b):
    @pl.when(pl.program_id(0) == 0)
    def _():
        cnt_ref[...] = jnp.zeros_like(cnt_ref)

    go = go_ref[...].astype(F32)
    on = _dot(pt_ref[...], on_ref[0].reshape(x_ref.shape[0], HG_WIDTH))
    y_b = (on * (go * _sigmoid(go))).astype(BF16)
    y_a = _unfold_rows(ya_ref[...].astype(F32), fold_a, fold_b).astype(BF16)
    pa = _dot(y_a, pa_ref[...])
    pb = _dot(y_b, pb_ref[...])
    merged = _sigmoid(ga_ref[...].astype(F32)) * pa + _sigmoid(gb_ref[...].astype(F32)) * pb
    x1 = x_ref[...] + g1_ref[0] * _dot(merged.astype(BF16), wo_ref[...])
    x1_ref[...] = x1
    y = x1 * lax.rsqrt(jnp.mean(x1 * x1, axis=-1, keepdims=True) + EPS) * n2_ref[...]
    h2 = y * (1.0 + sc_ref[0]) + sh_ref[0]
    _to_token_rows(h2_ref, h2)
    _route(h2.astype(BF16), wr_ref, br_ref, cnt_ref, ls8_ref, w8_ref, seg_ref)


def _merge(x2d, ya, on, go, ga, gb, g1, sc2, sh2, n2g, pa, pb, wo, wr_t, br, rows_per_batch, tm):
    n, d = x2d.shape
    per = rows_per_batch // tm
    row = lambda wd: pl.BlockSpec((tm, wd), lambda i: (i, 0))
    mod = pl.BlockSpec((1, 1, d), lambda i: (i // per, 0, 0))
    full = lambda a: pl.BlockSpec(a.shape, lambda i: (0, 0))
    tok = pl.BlockSpec((TOP_K, tm), lambda i: (0, i))
    return pl.pallas_call(
        _merge_kernel,
        out_shape=[jax.ShapeDtypeStruct((n, d), F32), jax.ShapeDtypeStruct((n * ROW_SUB, LANES), U32),
                   jax.ShapeDtypeStruct((TOP_K, n), jnp.int32), jax.ShapeDtypeStruct((TOP_K, n), F32),
                   jax.ShapeDtypeStruct((n // tm, N_EXPERTS, LANES), F32),
                   jax.ShapeDtypeStruct((N_EXPERTS, 1), F32)],
        grid=(n // tm,),
        in_specs=[row(d), pl.BlockSpec((tm // S5_T, S5_T * S5_WIDTH), lambda i: (i, 0)),
                  pl.BlockSpec((1, GRID_W, tm // GRID_W, HG_WIDTH), lambda i: (i // per, 0, i % per, 0)),
                  row(HG_WIDTH), row(d), row(d), mod, mod, mod,
                  pl.BlockSpec((1, d), lambda i: (0, 0)), full(pa), full(pb), full(wo), full(wr_t), full(br),
                  pl.BlockSpec((tm, tm), lambda i: (0, 0))],
        out_specs=[row(d), pl.BlockSpec((tm * ROW_SUB, LANES), lambda i: (i, 0)), tok, tok,
                   pl.BlockSpec((1, N_EXPERTS, LANES), lambda i: (i, 0, 0)),
                   pl.BlockSpec((N_EXPERTS, 1), lambda i: (0, 0))],
        scratch_shapes=[pltpu.VMEM((tm, LANES), F32), pltpu.VMEM((tm, LANES), F32)],
        compiler_params=_params("arbitrary"),
        name="merge_out_proj_route",
    )(x2d, ya, on, go, ga, gb, g1, sc2, sh2, n2g.reshape(1, d), pa, pb, wo, wr_t, br,
      _grid_transpose_matrix(tm).T)


MOE_TILE = TOK_TILE
TOKEN_UNROLL = 4
SEG_ALIGN = 8
FILL_ROWS = 512
STAGE_ROWS = MOE_TILE * TOP_K + FILL_ROWS
MOE_BLK = 1024


def _token_row(ref, r):
    return ref.at[pl.ds(pl.multiple_of(r * ROW_SUB, ROW_SUB), ROW_SUB)]


def _wait_rows(any_ref, sem, n_rows):
    view = any_ref.at[pl.ds(0, n_rows * ROW_SUB)]
    pltpu.make_async_copy(view, view, sem).wait()


def _rows(ref, r0, n):
    return ref.at[pl.ds(pl.multiple_of(r0 * ROW_SUB, ROW_SUB), n * ROW_SUB)]


def _pow2_pieces(n, max_piece, fn, min_piece=1):
    done = 0
    piece = max_piece
    while piece >= min_piece:
        hit = (n & piece) != 0
        pl.when(hit)(functools.partial(fn, done, piece))
        done = done + (n & piece)
        piece //= 2


def _copy_rows(src_ref, src0, dst_ref, dst0, n, max_piece, sem, min_piece=1):
    def piece(off, size):
        pltpu.make_async_copy(_rows(src_ref, src0 + off, size), _rows(dst_ref, dst0 + off, size), sem).start()
    _pow2_pieces(n, max_piece, piece, min_piece)


def _wait_copied_rows(src_ref, dst_ref, n, max_piece, sem):
    def piece(off, size):
        pltpu.make_async_copy(_rows(src_ref, 0, size), _rows(dst_ref, 0, size), sem).wait()
    _pow2_pieces(n, max_piece, piece)


def _copy_tile_segments(i, src_ref, src_tab, dst_ref, dst_tab, cnt_ref, off_ref, fill_src0, fill_dst0, sem):
    def per_expert(e, carry):
        _copy_rows(src_ref, src_tab[i, e], dst_ref, dst_tab[i, e], cnt_ref[i, e], MOE_TILE, sem, SEG_ALIGN)
        return carry

    lax.fori_loop(0, N_EXPERTS, per_expert, 0)
    used = off_ref[i, N_EXPERTS - 1] + cnt_ref[i, N_EXPERTS - 1]
    _copy_rows(src_ref, fill_src0(used), dst_ref, fill_dst0(used), STAGE_ROWS - used, FILL_ROWS, sem, SEG_ALIGN)


def _dispatch_kernel(gs_ref, cnt_ref, off_ref, pad_ref, ls_ref, h2_ref, xs_hbm, ls_smem, stage0, stage1, zbuf,
                     sem0, sem1, lsem, zsem, *, tm, n_blocks):
    i = pl.program_id(0)
    last = pl.num_programs(0) - 1
    cp = pltpu.make_async_copy(ls_ref, ls_smem, lsem)
    cp.start()
    trash0 = n_blocks * MOE_BLK

    @pl.when(i == 0)
    def _():
        stage0[...] = jnp.zeros_like(stage0)
        stage1[...] = jnp.zeros_like(stage1)
        zbuf[...] = jnp.zeros_like(zbuf)
        cpz = pltpu.make_async_copy(zbuf, _rows(xs_hbm, trash0, 2 * FILL_ROWS), zsem)
        cpz.start()
        cpz.wait()

    cp.wait()

    def tile(stage, sem, prev_sem, trash):
        def body(tu, carry):
            for u in range(TOKEN_UNROLL):
                t = tu * TOKEN_UNROLL + u
                row = h2_ref[pl.ds(pl.multiple_of(t * ROW_SUB, ROW_SUB), ROW_SUB), :]
                for k in range(TOP_K):
                    stage[pl.ds(pl.multiple_of(ls_smem[t * TOP_K + k], ROW_SUB), ROW_SUB), :] = row
            return carry

        lax.fori_loop(0, tm // TOKEN_UNROLL, body, 0)
        _copy_tile_segments(i, stage, off_ref, xs_hbm, gs_ref, cnt_ref, off_ref,
                            lambda used: used, lambda used: trash, sem)

        @pl.when(i > 0)
        def _():
            _wait_rows(xs_hbm, prev_sem, STAGE_ROWS)

        @pl.when(i == last)
        def _():
            _wait_rows(xs_hbm, sem, STAGE_ROWS)

    pl.when(i % 2 == 0)(functools.partial(tile, stage0, sem0, sem1, trash0))
    pl.when(i % 2 == 1)(functools.partial(tile, stage1, sem1, sem0, trash0 + FILL_ROWS))

    @pl.when(i == 0)
    def _():
        def start(e, carry):
            _copy_rows(zbuf, 0, xs_hbm, pad_ref[0, e], pad_ref[1, e], MOE_BLK // 2, zsem)
            return carry

        def wait(e, carry):
            _wait_copied_rows(zbuf, xs_hbm, pad_ref[1, e], MOE_BLK // 2, zsem)
            return carry

        lax.fori_loop(0, N_EXPERTS, start, 0)
        lax.fori_loop(0, N_EXPERTS, wait, 0)

        def zero_block(j, carry):
            pltpu.make_async_copy(zbuf, _rows(xs_hbm, j * MOE_BLK, MOE_BLK), zsem).start()
            return carry

        def wait_block(j, carry):
            pltpu.make_async_copy(zbuf, _rows(xs_hbm, 0, MOE_BLK), zsem).wait()
            return carry

        lax.fori_loop(pad_ref[2, 0], n_blocks, zero_block, 0)
        lax.fori_loop(pad_ref[2, 0], n_blocks, wait_block, 0)


def _dispatch(gstart, seg_cnt, seg_off, pad, ls8, h2_rows, n_blocks, tm):
    n = ls8.shape[0] // TOP_K
    cap = n_blocks * MOE_BLK + 2 * FILL_ROWS
    return pl.pallas_call(
        functools.partial(_dispatch_kernel, tm=tm, n_blocks=n_blocks),
        out_shape=jax.ShapeDtypeStruct((cap * ROW_SUB, LANES), U32),
        grid_spec=pltpu.PrefetchScalarGridSpec(
            num_scalar_prefetch=4,
            grid=(n // tm,),
            in_specs=[pl.BlockSpec((tm * TOP_K,), lambda i, *_: (i,)),
                      pl.BlockSpec((tm * ROW_SUB, LANES), lambda i, *_: (i, 0))],
            out_specs=pl.BlockSpec(memory_space=pl.ANY),
            scratch_shapes=[pltpu.SMEM((tm * TOP_K,), jnp.int32),
                            pltpu.VMEM((STAGE_ROWS * ROW_SUB, LANES), U32),
                            pltpu.VMEM((STAGE_ROWS * ROW_SUB, LANES), U32),
                            pltpu.VMEM((MOE_BLK * ROW_SUB, LANES), U32),
                            pltpu.SemaphoreType.DMA, pltpu.SemaphoreType.DMA, pltpu.SemaphoreType.DMA,
                            pltpu.SemaphoreType.DMA]),
        compiler_params=pltpu.CompilerParams(dimension_semantics=("arbitrary",), vmem_limit_bytes=VMEM_LIMIT,
                                             has_side_effects=True),
        name="moe_dispatch",
    )(gstart, seg_cnt, seg_off, pad, ls8, h2_rows)


def _expert_kernel(be_ref, nu_ref, x_ref, w1_ref, w3_ref, w2_ref, o_ref, w1b, w3b, w2b):
    j = pl.program_id(0)
    e = be_ref[j]
    prev = be_ref[jnp.maximum(j - 1, 0)]
    used = j < nu_ref[0]

    @pl.when(jnp.logical_and(used, jnp.logical_or(j == 0, e != prev)))
    def _():
        w1b[...] = w1_ref[0].astype(BF16)
        w3b[...] = w3_ref[0].astype(BF16)
        w2b[...] = w2_ref[0].astype(BF16)

    @pl.when(used)
    def _():
        x = _from_token_rows(x_ref, MOE_BLK).astype(BF16)
        a = _dot(x, w1b[...])
        hid = (a * _sigmoid(a)) * _dot(x, w3b[...])
        _to_token_rows(o_ref, _dot(hid.astype(BF16), w2b[...]))

    @pl.when(jnp.logical_not(used))
    def _():
        o_ref[...] = jnp.zeros_like(o_ref)


def _experts(block_e, n_used, xs, w1, w3, w2):
    n_blocks = block_e.shape[0]
    d, f = w1.shape[1], w1.shape[2]
    rows = pl.BlockSpec((MOE_BLK * ROW_SUB, LANES), lambda j, be, nu: (j, 0))
    rows_in = pl.BlockSpec((MOE_BLK * ROW_SUB, LANES), lambda j, be, nu: (jnp.minimum(j, nu[0] - 1), 0))
    return pl.pallas_call(
        _expert_kernel,
        out_shape=jax.ShapeDtypeStruct((n_blocks * MOE_BLK * ROW_SUB, LANES), U32),
        grid_spec=pltpu.PrefetchScalarGridSpec(
            num_scalar_prefetch=2,
            grid=(n_blocks,),
            in_specs=[rows_in,
                      pl.BlockSpec((1, d, f), lambda j, be, nu: (be[j], 0, 0)),
                      pl.BlockSpec((1, d, f), lambda j, be, nu: (be[j], 0, 0)),
                      pl.BlockSpec((1, f, d), lambda j, be, nu: (be[j], 0, 0))],
            out_specs=rows,
            scratch_shapes=[pltpu.VMEM((d, f), BF16), pltpu.VMEM((d, f), BF16), pltpu.VMEM((f, d), BF16)]),
        compiler_params=_params("arbitrary"),
        name="moe_experts",
    )(block_e, n_used, xs, w1, w3, w2)


def _combine_kernel(gs_ref, cnt_ref, off_ref, ls_ref, w8_ref, x1_ref, h2_ref, g2_ref, ws1_ref, ws3_ref, ws2_ref,
                    fg_ref, ys_hbm, o_ref, ls_smem, w_smem, gbuf0, gbuf1, acc_rows, sem0, sem1, lsem, *, tm):
    i = pl.program_id(0)
    last = pl.num_programs(0) - 1
    cp1 = pltpu.make_async_copy(ls_ref, ls_smem, lsem)
    cp2 = pltpu.make_async_copy(w8_ref, w_smem, lsem)
    cp1.start()
    cp2.start()

    def fetch(tile, gbuf, sem):
        _copy_tile_segments(tile, ys_hbm, gs_ref, gbuf, off_ref, cnt_ref, off_ref,
                            lambda used: 0, lambda used: used, sem)

    @pl.when(i == 0)
    def _():
        fetch(0, gbuf0, sem0)

    @pl.when(jnp.logical_and(i < last, i % 2 == 0))
    def _():
        fetch(i + 1, gbuf1, sem1)

    @pl.when(jnp.logical_and(i < last, i % 2 == 1))
    def _():
        fetch(i + 1, gbuf0, sem0)

    h2 = _from_token_rows(h2_ref, tm).astype(BF16)
    a = _dot(h2, ws1_ref[...])
    hid = (a * _sigmoid(a)) * _dot(h2, ws3_ref[...])
    acc = _dot(hid.astype(BF16), ws2_ref[...])
    cp1.wait()
    cp2.wait()

    def reduce_rows(gbuf, sem):
        _wait_rows(gbuf, sem, STAGE_ROWS)

        def body(tu, carry):
            for u in range(TOKEN_UNROLL):
                t = tu * TOKEN_UNROLL + u
                lo = jnp.zeros((ROW_SUB, LANES), F32)
                hi = jnp.zeros((ROW_SUB, LANES), F32)
                for k in range(TOP_K):
                    w = w_smem[t * TOP_K + k]
                    words = gbuf[pl.ds(pl.multiple_of(ls_smem[t * TOP_K + k], ROW_SUB), ROW_SUB), :]
                    lo = lo + w * lax.bitcast_convert_type(words << 16, F32)
                    hi = hi + w * lax.bitcast_convert_type(words & jnp.uint32(0xFFFF0000), F32)
                acc_rows[pl.ds(pl.multiple_of(t * SUBLANES, SUBLANES), ROW_SUB), :] = lo
                acc_rows[pl.ds(pl.multiple_of(t * SUBLANES, SUBLANES) + ROW_SUB, ROW_SUB), :] = hi
            return carry

        lax.fori_loop(0, tm // TOKEN_UNROLL, body, 0)

    pl.when(i % 2 == 0)(functools.partial(reduce_rows, gbuf0, sem0))
    pl.when(i % 2 == 1)(functools.partial(reduce_rows, gbuf1, sem1))
    routed = jnp.concatenate([acc_rows[pl.ds(s, tm, stride=SUBLANES), :] for s in range(SUBLANES)], axis=-1)
    y = x1_ref[...] + g2_ref[0] * (acc + routed)
    o_ref[...] = y * lax.rsqrt(jnp.mean(y * y, axis=-1, keepdims=True) + EPS) * fg_ref[...]


def _combine(gstart, seg_cnt, seg_off, ls8, w8, x1, h2_rows, g2, ws1, ws3, ws2, fg, ys, rows_per_batch, tm):
    n, d = x1.shape
    per = rows_per_batch // tm
    tok = pl.BlockSpec((tm * TOP_K,), lambda i, *_: (i,))
    full = lambda a: pl.BlockSpec(a.shape, lambda i, *_: (0, 0))
    return pl.pallas_call(
        functools.partial(_combine_kernel, tm=tm),
        out_shape=jax.ShapeDtypeStruct((n, d), F32),
        grid_spec=pltpu.PrefetchScalarGridSpec(
            num_scalar_prefetch=3,
            grid=(n // tm,),
            in_specs=[tok, tok, pl.BlockSpec((tm, d), lambda i, *_: (i, 0)),
                      pl.BlockSpec((tm * ROW_SUB, LANES), lambda i, *_: (i, 0)),
                      pl.BlockSpec((1, 1, d), lambda i, *_: (i // per, 0, 0)),
                      full(ws1), full(ws3), full(ws2), pl.BlockSpec((1, d), lambda i, *_: (0, 0)),
                      pl.BlockSpec(memory_space=pl.ANY)],
            out_specs=pl.BlockSpec((tm, d), lambda i, *_: (i, 0)),
            scratch_shapes=[pltpu.SMEM((tm * TOP_K,), jnp.int32), pltpu.SMEM((tm * TOP_K,), F32),
                            pltpu.VMEM((STAGE_ROWS * ROW_SUB, LANES), U32),
                            pltpu.VMEM((STAGE_ROWS * ROW_SUB, LANES), U32),
                            pltpu.VMEM((tm * SUBLANES, LANES), F32), pltpu.SemaphoreType.DMA,
                            pltpu.SemaphoreType.DMA, pltpu.SemaphoreType.DMA]),
        compiler_params=_params("arbitrary"),
        name="moe_combine_final",
    )(gstart, seg_cnt, seg_off, ls8, w8, x1, h2_rows, g2, ws1, ws3, ws2, fg.reshape(1, d), ys)


def _moe_plan(seg, counts, n_assign):
    cnt = counts.reshape(N_EXPERTS).astype(jnp.int32)
    padded = (cnt + MOE_BLK - 1) // MOE_BLK * MOE_BLK
    pends = jnp.cumsum(padded)
    pstarts = pends - padded
    max_rows = n_assign + seg.shape[0] * N_EXPERTS * (SEG_ALIGN - 1)
    n_blocks = (max_rows + N_EXPERTS * (MOE_BLK - 1) + MOE_BLK - 1) // MOE_BLK
    seg = seg[:, :, :3].astype(jnp.int32)
    gstart = pstarts[None, :] + seg[:, :, 0]
    blk_start = jnp.arange(n_blocks, dtype=jnp.int32) * MOE_BLK
    block_e = jnp.minimum(jnp.sum((blk_start[:, None] >= pends[None, :]).astype(jnp.int32), axis=1),
                          N_EXPERTS - 1).astype(jnp.int32)
    n_used = (pends[-1:] // MOE_BLK).astype(jnp.int32)
    pad = jnp.stack([pstarts + cnt, padded - cnt, jnp.broadcast_to(n_used, (N_EXPERTS,))], axis=0).astype(jnp.int32)
    return gstart, seg[:, :, 1], seg[:, :, 2], pad, block_e, n_used, n_blocks


def _mixer(x, c, ctx, c_ctx, w_ada, b_ada, norm1_g, norm2_g, w_in, s5_lam_re, s5_lam_im, s5_log_dt,
           s5_b_re, s5_b_im, s5_c_re, s5_c_im, s5_d, s5_w_glu, lb, hg_norm_g, p_a, p_b, w_out,
           moe_w_router, moe_b_router):
    b, l, d = x.shape
    lc = ctx.shape[1]
    n = b * l
    rows = l // GRID_W

    c8 = jnp.concatenate([c, c_ctx[None], jnp.zeros((8 - b - 1, d), F32)], axis=0)
    mod = _ada(c8, w_ada, b_ada)
    sh1, sc1, g1, sh2, sc2, g2 = [mod[:b, k * d:(k + 1) * d].reshape(b, 1, d) for k in range(6)]
    csh1, csc1 = mod[b:b + 1, 0:d].reshape(1, 1, d), mod[b:b + 1, d:2 * d].reshape(1, 1, d)

    w_in_b = w_in.astype(BF16)
    z = dict(zip([p[0] for p in _IN_PIECES],
                 _inproj(x.reshape(n, d), sc1, sh1, norm1_g, w_in_b, l, TOK_TILE, True)))
    zc = dict(zip([p[0] for p in _IN_PIECES],
                  _inproj(ctx.reshape(b * lc, d), csc1, csh1, norm1_g, w_in_b, lc, lc, False)))

    cx = lambda t: t.reshape(b, lc, HG_WIDTH)
    lb_row = lb.reshape(1, HG_WIDTH)
    o_f = _hgrn_pass(z["q"], z["ff"], z["i"], cx(zc["ff"]), cx(zc["i"]), lb_row, None, None, reverse=False)
    o_n = _hgrn_pass(z["q"], z["fb"], z["i"], cx(zc["fb"]), cx(zc["i"]), lb_row, o_f,
                     hg_norm_g.reshape(1, HG_DK), reverse=True)

    d_lag, w_s5_in, w_out_f, w_out_b, decay = _s5_weights(s5_lam_re, s5_lam_im, s5_log_dt, s5_b_re, s5_b_im,
                                                          s5_c_re, s5_c_im)
    kc, kl = lc // S5_T, l // S5_T
    u_lat = z["u"].reshape(b, kl, S5_T * S5_WIDTH)
    u_ctx = zc["u"].reshape(b, kc, S5_T * S5_WIDTH)
    rows_in = kl + kc
    u_ext = jnp.concatenate([u_lat, u_ctx], axis=1).reshape(b * rows_in, S5_T * S5_WIDTH)
    e = _s5_in(u_ext, d_lag, w_s5_in, (b * rows_in) // 2)
    states = _s5_scan(e, decay, b, rows_in, kl)
    d_row = s5_d.astype(F32).reshape(1, S5_WIDTH)
    y_a = _s5_out(states, w_out_f, w_out_b, e.reshape(b, rows_in, -1), z["u"], d_row, s5_w_glu.astype(BF16))

    return _merge(x.reshape(n, d), y_a, o_n, z["go"], z["ga"], z["gb"], g1, sc2, sh2, norm2_g,
                  p_a.astype(BF16), p_b.astype(BF16), w_out.astype(BF16),
                  moe_w_router.T.astype(BF16), moe_b_router.astype(F32).reshape(N_EXPERTS, 1), l, MOE_TILE) + (g2,)


def kernel(x, c, ctx, c_ctx, w_ada, b_ada, norm1_g, norm2_g, w_in, s5_lam_re, s5_lam_im, s5_log_dt, s5_b_re,
           s5_b_im, s5_c_re, s5_c_im, s5_d, s5_w_glu, hg_lb_logits, hg_norm_g, p_a, p_b, w_out, moe_w_router,
           moe_b_router, moe_w1, moe_w3, moe_w2, moe_ws1, moe_ws3, moe_ws2, final_norm_g):
    b, l, d = x.shape
    n = b * l
    assert w_ada.shape[0] == 1, "single-layer block"
    lb = jnp.cumsum(jax.nn.softmax(hg_lb_logits.astype(F32), axis=0), axis=0)[0]
    x1, h2_rows, ls8, w8, seg, counts, g2 = _mixer(
        x, c, ctx, c_ctx, w_ada[0], b_ada[0], norm1_g[0], norm2_g[0], w_in[0], s5_lam_re[0], s5_lam_im[0],
        s5_log_dt[0], s5_b_re[0], s5_b_im[0], s5_c_re[0], s5_c_im[0], s5_d[0], s5_w_glu[0], lb, hg_norm_g[0],
        p_a[0], p_b[0], w_out[0], moe_w_router[0], moe_b_router[0])
    gstart, seg_cnt, seg_off, pad, block_e, n_used, n_blocks = _moe_plan(seg, counts, n * TOP_K)
    ls_flat, w_flat = ls8.T.reshape(n * TOP_K), w8.T.reshape(n * TOP_K)
    xs = _dispatch(gstart, seg_cnt, seg_off, pad, ls_flat, h2_rows, n_blocks, MOE_TILE)
    ys = _experts(block_e, n_used, xs, moe_w1[0], moe_w3[0], moe_w2[0])
    out = _combine(gstart, seg_cnt, seg_off, ls_flat, w_flat, x1, h2_rows, g2, moe_ws1[0].astype(BF16),
                   moe_ws3[0].astype(BF16), moe_ws2[0].astype(BF16), final_norm_g, ys, l, MOE_TILE)
    return out.reshape(b, l, d)
```

```python
import functools
import math

import jax
import jax.numpy as jnp
from jax import lax
from jax.experimental import pallas as pl
from jax.experimental.pallas import tpu as pltpu

F32 = jnp.float32
BF16 = jnp.bfloat16

GRID_W = 64
S5_WIDTH = 256
S5_GROUP = 16
S5_GROUPS = 16
S5_STATE = 64
HG_HEADS = 6
HG_DK = 128
HG_WIDTH = HG_HEADS * HG_DK
N_EXPERTS = 64
ROUTE_GROUPS = 8
TOPK_GROUPS = 4
TOP_K = 8
ROUTED_SCALE = 2.5
EPS = 1e-6

LANES = 128
SUBLANES = 8

TOK_TILE = 512
S5_T = 16
HG_CHUNK = 64
HG_BATCH = 4
VMEM_LIMIT = 56 * 1024 * 1024

_NT = (((1,), (1,)), ((), ()))
_TN = (((0,), (0,)), ((), ()))


def _params(*sem):
    return pltpu.CompilerParams(dimension_semantics=sem, vmem_limit_bytes=VMEM_LIMIT)


def _dot(a, b):
    return jnp.dot(a, b, preferred_element_type=F32)


def _sigmoid(x):
    return 1.0 / (1.0 + jnp.exp(-x))


def _ada_kernel(c_ref, w_ref, b_ref, o_ref):
    c = c_ref[...]
    s = (c * _sigmoid(c)).astype(BF16)
    o_ref[...] = _dot(s, w_ref[...].astype(BF16)) + b_ref[...]


def _ada(c8, w_ada, b_ada):
    d, n = w_ada.shape
    tn = 1536
    return pl.pallas_call(
        _ada_kernel,
        out_shape=jax.ShapeDtypeStruct((8, n), F32),
        grid=(n // tn,),
        in_specs=[pl.BlockSpec((8, d), lambda j: (0, 0)),
                  pl.BlockSpec((d, tn), lambda j: (0, j)),
                  pl.BlockSpec((1, tn), lambda j: (0, j))],
        out_specs=pl.BlockSpec((8, tn), lambda j: (0, j)),
        compiler_params=_params("arbitrary"),
        name="ada_mod",
    )(c8, w_ada, b_ada.reshape(1, n))


_IN_PIECES = (("u", 0, 256, BF16), ("q", 256, 768, BF16), ("ff", 1024, 768, BF16),
              ("fb", 1792, 768, BF16), ("i", 2560, 768, BF16), ("go", 3328, 768, BF16),
              ("ga", 4096, 1024, BF16), ("gb", 5120, 1024, BF16))


def _fold_rows(val, buf_a, buf_b):
    t = val.shape[0]
    buf_a[...] = val[:, :LANES]
    buf_b[...] = val[:, LANES:]
    pieces = []
    for s in range(S5_T):
        pieces += [buf_a[pl.ds(s, t // S5_T, stride=S5_T), :], buf_b[pl.ds(s, t // S5_T, stride=S5_T), :]]
    return jnp.concatenate(pieces, axis=-1)


def _unfold_rows(val, buf_a, buf_b):
    r = val.shape[0]
    for s in range(S5_T):
        buf_a[pl.ds(s, r, stride=S5_T), :] = val[:, s * S5_WIDTH:s * S5_WIDTH + LANES]
        buf_b[pl.ds(s, r, stride=S5_T), :] = val[:, s * S5_WIDTH + LANES:(s + 1) * S5_WIDTH]
    return jnp.concatenate([buf_a[...], buf_b[...]], axis=-1)


def _grid_transpose_matrix(tm):
    i = jnp.arange(tm)
    src = (i % (tm // GRID_W)) * GRID_W + i // (tm // GRID_W)
    return (src[:, None] == jnp.arange(tm)[None, :]).astype(BF16)


def _inproj_kernel(x_ref, sc_ref, sh_ref, g_ref, w_ref, p_ref, *o_refs):
    o_refs, (fold_a, fold_b) = o_refs[:len(_IN_PIECES)], o_refs[len(_IN_PIECES):]
    x = x_ref[...]
    y = x * lax.rsqrt(jnp.mean(x * x, axis=-1, keepdims=True) + EPS) * g_ref[...]
    h = (y * (1.0 + sc_ref[0]) + sh_ref[0]).astype(BF16)
    h_cm = None
    for (name, a, wd, _), o_ref in zip(_IN_PIECES, o_refs):
        if name == "u":
            o_ref[...] = _fold_rows(_dot(h, w_ref[:, a:a + wd]), fold_a, fold_b).astype(o_ref.dtype)
        elif len(o_ref.shape) == 2:
            o_ref[...] = _dot(h, w_ref[:, a:a + wd]).astype(o_ref.dtype)
        else:
            if h_cm is None:
                h_cm = _dot(p_ref[...], h).astype(BF16)
            o_ref[0] = _dot(h_cm, w_ref[:, a:a + wd]).astype(o_ref.dtype).reshape(o_ref.shape[1:])


_COLMAJOR_PIECES = ("q", "ff", "fb", "i")


def _inproj(x2d, sc, sh, g, w_bf16, rows_per_mod, tm, colmajor):
    n, d = x2d.shape
    per = rows_per_mod // tm
    mod_map = (lambda i: (i // per, 0, 0)) if sc.shape[0] > 1 else (lambda i: (0, 0, 0))
    shapes, specs = [], []
    for name, _, wd, dt in _IN_PIECES:
        if colmajor and name in _COLMAJOR_PIECES:
            shapes.append(jax.ShapeDtypeStruct((n // rows_per_mod, GRID_W, rows_per_mod // GRID_W, wd), dt))
            specs.append(pl.BlockSpec((1, GRID_W, tm // GRID_W, wd), lambda i: (i // per, 0, i % per, 0)))
        elif name == "u":
            shapes.append(jax.ShapeDtypeStruct((n // S5_T, S5_T * wd), dt))
            specs.append(pl.BlockSpec((tm // S5_T, S5_T * wd), lambda i: (i, 0)))
        else:
            shapes.append(jax.ShapeDtypeStruct((n, wd), dt))
            specs.append(pl.BlockSpec((tm, wd), lambda i: (i, 0)))
    return pl.pallas_call(
        _inproj_kernel,
        out_shape=shapes,
        grid=(n // tm,),
        in_specs=[pl.BlockSpec((tm, d), lambda i: (i, 0)),
                  pl.BlockSpec((1, 1, d), mod_map),
                  pl.BlockSpec((1, 1, d), mod_map),
                  pl.BlockSpec((1, d), lambda i: (0, 0)),
                  pl.BlockSpec(w_bf16.shape, lambda i: (0, 0)),
                  pl.BlockSpec((tm, tm), lambda i: (0, 0))],
        out_specs=specs,
        scratch_shapes=[pltpu.VMEM((tm, LANES), F32), pltpu.VMEM((tm, LANES), F32)],
        compiler_params=_params("arbitrary"),
        name="in_proj",
    )(x2d, sc, sh, g.reshape(1, d), w_bf16, _grid_transpose_matrix(tm))


def _hgrn_gates(zf, lb):
    sig = _sigmoid(zf)
    logf = jnp.log(lb + (1.0 - lb) * sig)
    k = (1.0 - lb) * (1.0 - sig)
    return logf, k


def _chunk_cumsum(cs, logf):
    hi = logf.astype(BF16)
    lo = (logf - hi.astype(F32)).astype(BF16)
    return _dot(cs, hi) + _dot(cs, lo)


def _hgrn_state_step(zf, v, lb, st, cs, reverse):
    logf, k = _hgrn_gates(zf, lb)
    cum = _chunk_cumsum(cs, logf)
    t = 0 if reverse else HG_CHUNK - 1
    total = cum[t:t + 1, :]
    kdec = (k * jnp.exp(total - cum)).astype(BF16)
    st_new = st * jnp.exp(total) + lax.dot_general(v.astype(BF16), kdec, _TN, preferred_element_type=F32)
    return cum, k, st_new


def _hgrn_kernel(*refs, reverse, final, n_ctx_chunks):
    if final:
        q_all, f_all, v_all, cf_ref, cv_ref, lb_ref, of_all, g_ref, o_all, st_ref = refs
    else:
        q_all, f_all, v_all, cf_ref, cv_ref, lb_ref, o_all, st_ref = refs
        of_all = None
    n_batch = q_all.shape[0]
    c_len = HG_CHUNK
    n_rows = q_all.shape[2]
    n_chunks = n_rows // c_len
    row = lax.broadcasted_iota(jnp.int32, (n_rows, n_rows), 0)
    col = lax.broadcasted_iota(jnp.int32, (n_rows, n_rows), 1)
    tri = (col >= row) if reverse else (col <= row)
    same_chunk = None
    for c in range(n_chunks):
        lo, hi = c * c_len, (c + 1) * c_len
        blk = (row >= lo) & (row < hi) & (col >= lo) & (col < hi)
        same_chunk = blk if same_chunk is None else (same_chunk | blk)
    mask = tri & same_chunk
    cs = jnp.where(mask, 1.0, 0.0).astype(BF16)

    @pl.when(pl.program_id(1) == 0)
    def _():
        cs1 = cs[:c_len, :c_len]
        order = range(n_ctx_chunks - 1, -1, -1) if reverse else range(n_ctx_chunks)
        for bi in range(n_batch):
            for h in range(HG_HEADS):
                cols = slice(h * HG_DK, (h + 1) * HG_DK)
                st = jnp.zeros((HG_DK, HG_DK), F32)
                for c in order:
                    rows = slice(c * c_len, (c + 1) * c_len)
                    _, _, st = _hgrn_state_step(cf_ref[bi, rows, cols].astype(F32), cv_ref[bi, rows, cols].astype(F32),
                                                lb_ref[:, cols], st, cs1, reverse)
                st_ref[bi * HG_HEADS + h] = st

    def chunk_rows(x, r):
        return [x[c * c_len + r:c * c_len + r + 1, :] for c in range(n_chunks)]

    def over_chunks(rows):
        return jnp.concatenate([jnp.broadcast_to(r, (c_len, r.shape[1])) for r in rows], axis=0)

    lb = lb_ref[...]
    r_ref = c_len // 2 - 1 if reverse else c_len // 2
    r_tot = 0 if reverse else c_len - 1
    order = range(n_chunks - 1, -1, -1) if reverse else range(n_chunks)
    for bi in range(n_batch):
        q = q_all[bi, 0].astype(F32)
        v = v_all[bi, 0]
        logf, k = _hgrn_gates(f_all[bi, 0].astype(F32), lb)
        cum = _chunk_cumsum(cs, logf)
        ref_rows, tot_rows = chunk_rows(cum, r_ref), chunk_rows(cum, r_tot)
        ref = over_chunks(ref_rows)
        qe = q * jnp.exp(cum - ref)
        ke = k * jnp.exp(ref - cum)
        qi, ki = qe.astype(BF16), ke.astype(BF16)
        q_in = (qe * over_chunks([jnp.exp(r) for r in ref_rows])).astype(BF16)
        kdec = (ke * over_chunks([jnp.exp(t - r) for t, r in zip(tot_rows, ref_rows)])).astype(BF16)
        for h in range(HG_HEADS):
            cols = slice(h * HG_DK, (h + 1) * HG_DK)
            s = lax.dot_general(qi[:, cols], ki[:, cols], _NT, preferred_element_type=F32)
            o_intra = _dot(jnp.where(mask, s, 0.0).astype(BF16), v[:, cols])
            st = st_ref[bi * HG_HEADS + h]
            for c in order:
                rows = slice(c * c_len, (c + 1) * c_len)
                o = o_intra[rows] + lax.dot_general(q_in[rows, cols], st.astype(BF16), _NT,
                                                    preferred_element_type=F32)
                total = cum[c * c_len + r_tot:c * c_len + r_tot + 1, cols]
                st = st * jnp.exp(total) + lax.dot_general(v[rows, cols], kdec[rows, cols], _TN,
                                                           preferred_element_type=F32)
                if final:
                    o = o + of_all[bi, 0, rows, cols].astype(F32)
                    o = o * lax.rsqrt(jnp.mean(o * o, axis=-1, keepdims=True) + EPS) * g_ref[...]
                o_all[bi, 0, rows, cols] = o.astype(o_all.dtype)
            st_ref[bi * HG_HEADS + h] = st


def _hgrn_pass(q, f, v, cf, cv, lb, o_prev, g, *, reverse):
    b, nw, rows, _ = q.shape
    nb = HG_BATCH if b % HG_BATCH == 0 else 1
    final = o_prev is not None
    wmap = (lambda bi, w: (bi, nw - 1 - w, 0, 0)) if reverse else (lambda bi, w: (bi, w, 0, 0))
    blk = pl.BlockSpec((nb, 1, rows, HG_WIDTH), wmap)
    cblk = pl.BlockSpec((nb, cf.shape[1], HG_WIDTH), lambda bi, w: (bi, 0, 0))
    in_specs = [blk, blk, blk, cblk, cblk, pl.BlockSpec((1, HG_WIDTH), lambda bi, w: (0, 0))]
    args = [q, f, v, cf, cv, lb]
    if final:
        in_specs += [blk, pl.BlockSpec((1, HG_DK), lambda bi, w: (0, 0))]
        args += [o_prev, g]
    return pl.pallas_call(
        functools.partial(_hgrn_kernel, reverse=reverse, final=final, n_ctx_chunks=cf.shape[1] // HG_CHUNK),
        out_shape=jax.ShapeDtypeStruct(q.shape, BF16),
        grid=(b // nb, nw),
        in_specs=in_specs,
        out_specs=blk,
        scratch_shapes=[pltpu.VMEM((nb * HG_HEADS, HG_DK, HG_DK), F32)],
        compiler_params=_params("arbitrary", "arbitrary"),
        name="hgrn_bwd" if reverse else "hgrn_fwd",
    )(*args)


def _s5_weights(lam_re, lam_im, log_dt, b_re, b_im, c_re, c_im):
    hp = lax.Precision.HIGHEST
    g, p, cc, t = S5_GROUPS, S5_STATE, S5_GROUP, S5_T
    lre = jnp.minimum(lam_re.astype(F32), -1e-4)
    lim = lam_im.astype(F32)
    dt = jnp.exp(log_dt.astype(F32))[..., None]
    ks = jnp.arange(t + 1, dtype=F32)[:, None, None, None]
    mag = jnp.exp(ks * (lre * dt)[None])
    pw_re = mag * jnp.cos(ks * (lim * dt)[None])
    pw_im = mag * jnp.sin(ks * (lim * dt)[None])
    nr, ni = pw_re[1] - 1.0, pw_im[1]
    den = lre * lre + lim * lim
    cf_re = (nr * lre + ni * lim) / den
    cf_im = (ni * lre - nr * lim) / den
    bb_re = cf_re[..., None] * b_re - cf_im[..., None] * b_im
    bb_im = cf_re[..., None] * b_im + cf_im[..., None] * b_re
    cre, cim = c_re.astype(F32), c_im.astype(F32)
    sw, ns = S5_WIDTH, 2 * g * p
    grp_of_row = jnp.arange(sw)[:, None] // cc

    cp_re = cre[None, None, :, :, :] * pw_re[:t, :, :, None, :] - cim[None, None] * pw_im[:t, :, :, None, :]
    cp_im = cre[None, None, :, :, :] * pw_im[:t, :, :, None, :] + cim[None, None] * pw_re[:t, :, :, None, :]
    def contract_p(cp, bb):
        return jnp.sum(cp.transpose(4, 1, 0, 2, 3)[..., None] * bb.transpose(2, 0, 1, 3)[:, :, None, :, None, :],
                       axis=0)

    kk = contract_p(cp_re, bb_re) - contract_p(cp_im, bb_im)
    kf, kb = kk[0], kk[1]
    kall = jnp.concatenate([kb[:0:-1], (kf[0] + kb[0])[None], kf[1:]], axis=0)
    kt = kall.transpose(0, 1, 3, 2).reshape(2 * t - 1, sw, cc)

    def spread(x, period, reps):
        sel = (jnp.arange(period)[:, None] == (jnp.arange(period * reps)[None, :] % period)).astype(BF16)
        return jnp.dot(x.astype(BF16), sel, preferred_element_type=BF16)

    same = grp_of_row == (jnp.arange(sw)[None, :] // cc)
    d_lag = jnp.where(same[None], spread(kt, cc, g), 0)

    same_in = jnp.tile(grp_of_row, (t, 1)) == ((jnp.arange(ns)[None, :] % (g * p)) // p)

    def in_to_state(pre, pim, bre, bim):
        xre = pre[..., None] * bre[None] - pim[..., None] * bim[None]
        xim = pre[..., None] * bim[None] + pim[..., None] * bre[None]
        return [spread(xre.transpose(0, 1, 3, 2).reshape(t * sw, p), p, g),
                spread(xim.transpose(0, 1, 3, 2).reshape(t * sw, p), p, g)]

    w_in = jnp.concatenate(in_to_state(pw_re[t - 1::-1, 0], pw_im[t - 1::-1, 0], bb_re[0], bb_im[0])
                           + in_to_state(pw_re[:t, 1], pw_im[:t, 1], bb_re[1], bb_im[1]), axis=1)
    w_in = jnp.where(jnp.tile(same_in, (1, 2)), w_in, 0)

    same_out = ((jnp.arange(ns)[:, None] % (g * p)) // p) == ((jnp.arange(t * sw)[None, :] // cc) % g)
    col = jnp.arange(t * sw)
    pick = (jnp.arange(t * cc)[:, None] == ((col // sw) * cc + col % cc)[None, :]).astype(BF16)

    def state_to_out(pre, pim):
        are = cre[None] * pre[:, :, None, :] - cim[None] * pim[:, :, None, :]
        aim = cre[None] * pim[:, :, None, :] + cim[None] * pre[:, :, None, :]
        a = jnp.concatenate([are.transpose(1, 3, 0, 2), -aim.transpose(1, 3, 0, 2)], axis=0)
        a = jnp.dot(a.reshape(ns, t * cc).astype(BF16), pick, preferred_element_type=BF16)
        return jnp.where(same_out, a, 0)

    w_out_f = state_to_out(pw_re[1:, 0], pw_im[1:, 0])
    w_out_b = state_to_out(pw_re[t:0:-1, 1], pw_im[t:0:-1, 1])

    decay = jnp.stack([pw_re[t].reshape(2, g * p), pw_im[t].reshape(2, g * p)], axis=1)
    return d_lag, w_in, w_out_f, w_out_b, decay


def _s5_in_kernel(u_ref, d_ref, w_ref, o_ref):
    j = pl.program_id(0)

    @pl.when(j < S5_T)
    def _():
        acc = _dot(u_ref[:, 0:S5_WIDTH], d_ref[j + S5_T - 1])
        for s in range(1, S5_T):
            acc = acc + _dot(u_ref[:, s * S5_WIDTH:(s + 1) * S5_WIDTH], d_ref[j - s + S5_T - 1])
        o_ref[...] = acc

    @pl.when(j >= S5_T)
    def _():
        o_ref[...] = _dot(u_ref[...], w_ref[...])


def _s5_in(u, d_lag, w_in, tm):
    m, k = u.shape
    tn = S5_WIDTH
    nj = (k + w_in.shape[1]) // tn
    return pl.pallas_call(
        _s5_in_kernel,
        out_shape=jax.ShapeDtypeStruct((m, nj * tn), F32),
        grid=(nj, m // tm),
        in_specs=[pl.BlockSpec((tm, k), lambda j, i: (i, 0)),
                  pl.BlockSpec(d_lag.shape, lambda j, i: (0, 0, 0)),
                  pl.BlockSpec((k, tn), lambda j, i: (0, jnp.maximum(j - S5_T, 0)))],
        out_specs=pl.BlockSpec((tm, tn), lambda j, i: (i, j)),
        compiler_params=_params("arbitrary", "arbitrary"),
        name="s5_in",
    )(u, d_lag, w_in)


def _s5_scan_kernel(efr_ref, efi_ref, ebr_ref, ebi_ref, a_ref, hfr_ref, hfi_ref, hbr_ref, hbi_ref,
                    *, nb, rows_in, rows_out):
    dirs = ((efr_ref, efi_ref, hfr_ref, hfi_ref, a_ref[0, 0:1, :], a_ref[0, 1:2, :]),
            (ebr_ref, ebi_ref, hbr_ref, hbi_ref, a_ref[1, 0:1, :], a_ref[1, 1:2, :]))
    zero = jnp.zeros_like(dirs[0][4])

    def step(srcs, carry, store):
        new = []
        for di, (er_ref, ei_ref, hr_ref, hi_ref, are, aim) in enumerate(dirs):
            for bi in range(nb):
                hre, him = carry[2 * (di * nb + bi)], carry[2 * (di * nb + bi) + 1]
                if store:
                    hr_ref[pl.ds(bi * rows_out + srcs[di], 1), :] = hre
                    hi_ref[pl.ds(bi * rows_out + srcs[di], 1), :] = him
                ere = er_ref[pl.ds(bi * rows_in + srcs[di], 1), :]
                eim = ei_ref[pl.ds(bi * rows_in + srcs[di], 1), :]
                new += [are * hre - aim * him + ere, are * him + aim * hre + eim]
        return tuple(new)

    n_ctx = rows_in - rows_out
    carry = lax.fori_loop(0, n_ctx, lambda s, c: step((rows_out + s, rows_in - 1 - s), c, False),
                          tuple([zero] * (4 * nb)))
    lax.fori_loop(0, rows_out, lambda s, c: step((s, rows_out - 1 - s), c, True), carry)


def _s5_scan(e, decay, nb, rows_in, rows_out):
    tc = 256
    nsr = S5_GROUPS * S5_STATE
    c0 = (S5_T * S5_WIDTH) // tc
    nt = nsr // tc
    eblk = lambda k: pl.BlockSpec((nb * rows_in, tc), lambda j: (0, c0 + k * nt + j))
    hblk = pl.BlockSpec((nb * rows_out, tc), lambda j: (0, j))
    return pl.pallas_call(
        functools.partial(_s5_scan_kernel, nb=nb, rows_in=rows_in, rows_out=rows_out),
        out_shape=[jax.ShapeDtypeStruct((nb * rows_out, nsr), F32)] * 4,
        grid=(nt,),
        in_specs=[eblk(0), eblk(1), eblk(2), eblk(3), pl.BlockSpec((2, 2, tc), lambda j: (0, 0, j))],
        out_specs=[hblk] * 4,
        compiler_params=_params("arbitrary"),
        name="s5_scan",
    )(e, e, e, e, decay)


def _gelu_tanh(x):
    return 0.5 * x * (1.0 + jnp.tanh(math.sqrt(2.0 / math.pi) * (x + 0.044715 * x * x * x)))


def _s5_out_kernel(hfr_ref, hfi_ref, hbr_ref, hbi_ref, wf_ref, wb_ref, yi_ref, u_ref, d_ref, wg_ref, o_ref):
    nsr = hfr_ref.shape[1]
    y = yi_ref[0] + d_ref[...] * u_ref[...].astype(F32)
    for h_ref, w_ref, r0 in ((hfr_ref, wf_ref, 0), (hfi_ref, wf_ref, nsr), (hbr_ref, wb_ref, 0), (hbi_ref, wb_ref, nsr)):
        y = y + _dot(h_ref[...].astype(BF16), w_ref[r0:r0 + nsr, :])
    y = _gelu_tanh(y)
    gate = _sigmoid(_dot(y.astype(BF16), wg_ref[...]))
    o_ref[...] = (y * gate).astype(o_ref.dtype)


def _s5_out(states, w_out_f, w_out_b, e3, u_rows, d_row, w_glu):
    m, nsr = states[0].shape
    nb = e3.shape[0]
    tm = m // nb
    tn = S5_WIDTH
    st = pl.BlockSpec((tm, nsr), lambda i, j: (i, 0))
    wo = pl.BlockSpec((2 * nsr, tn), lambda i, j: (0, j))
    return pl.pallas_call(
        _s5_out_kernel,
        out_shape=jax.ShapeDtypeStruct((m, S5_T * S5_WIDTH), BF16),
        grid=(nb, S5_T),
        in_specs=[st, st, st, st, wo, wo,
                  pl.BlockSpec((1, tm, tn), lambda i, j: (i, 0, j)),
                  pl.BlockSpec((tm, tn), lambda i, j: (i, j)),
                  pl.BlockSpec((1, tn), lambda i, j: (0, 0)),
                  pl.BlockSpec((tn, tn), lambda i, j: (0, 0))],
        out_specs=pl.BlockSpec((tm, tn), lambda i, j: (i, j)),
        compiler_params=_params("arbitrary", "arbitrary"),
        name="s5_out",
    )(*states, w_out_f, w_out_b, e3, u_rows, d_row, w_glu)


U32 = jnp.uint32
ROW_SUB = 4


def _to_token_rows(ref, val):
    t, d = val.shape
    bits = lax.bitcast_convert_type(val.astype(BF16).astype(F32), U32)
    w = (bits[:, :d // 2] >> 16) | bits[:, d // 2:]
    for s in range(ROW_SUB):
        ref[pl.ds(s, t, stride=ROW_SUB), :] = w[:, s * LANES:(s + 1) * LANES]


def _from_token_rows(ref, t, row0=0):
    w = jnp.concatenate([ref[pl.ds(row0 * ROW_SUB + s, t, stride=ROW_SUB), :] for s in range(ROW_SUB)], axis=-1)
    lo = lax.bitcast_convert_type(w << 16, F32)
    hi = lax.bitcast_convert_type(w & jnp.uint32(0xFFFF0000), F32)
    return jnp.concatenate([lo, hi], axis=-1)


def _route(h2b, wr_ref, br_ref, cnt_ref, ls8_ref, w8_ref, seg_ref):
    tm = h2b.shape[0]
    per_group = N_EXPERTS // ROUTE_GROUPS
    scores = _sigmoid(lax.dot_general(wr_ref[...], h2b, _NT, preferred_element_type=F32))
    biased = scores + br_ref[...]
    neg = -jnp.inf
    sub = lax.broadcasted_iota(jnp.int32, (per_group, tm), 0)
    grp = []
    for gi in range(ROUTE_GROUPS):
        v = biased[gi * per_group:(gi + 1) * per_group, :]
        m1 = jnp.max(v, axis=0, keepdims=True)
        first = jnp.min(jnp.where(v == m1, sub, per_group), axis=0, keepdims=True)
        m2 = jnp.max(jnp.where(sub == first, neg, v), axis=0, keepdims=True)
        grp.append(m1 + m2)
    grp = jnp.concatenate(grp, axis=0)
    gid = lax.broadcasted_iota(jnp.int32, (ROUTE_GROUPS, tm), 0)
    beaten = jnp.zeros((ROUTE_GROUPS, tm), jnp.int32)
    for gj in range(ROUTE_GROUPS):
        r = grp[gj:gj + 1, :]
        beaten = beaten + jnp.where((r > grp) | ((r == grp) & (gj < gid)), 1, 0)
    group_ok = beaten < TOPK_GROUPS
    expert_ok = jnp.concatenate(
        [jnp.broadcast_to(group_ok[gi:gi + 1, :], (per_group, tm)) for gi in range(ROUTE_GROUPS)], axis=0)
    cur = jnp.where(expert_ok, biased, neg)
    eid = lax.broadcasted_iota(jnp.int32, (N_EXPERTS, tm), 0)
    sel = jnp.zeros((N_EXPERTS, tm), F32)
    picks, wts = [], []
    for _ in range(TOP_K):
        m = jnp.max(cur, axis=0, keepdims=True)
        idx = jnp.min(jnp.where(cur == m, eid, N_EXPERTS), axis=0, keepdims=True)
        hit = eid == idx
        picks.append(idx)
        wts.append(jnp.sum(jnp.where(hit, scores, 0.0), axis=0, keepdims=True))
        sel = jnp.where(hit, 1.0, sel)
        cur = jnp.where(hit, neg, cur)
    wsum = wts[0]
    for w in wts[1:]:
        wsum = wsum + w
    selb = sel.astype(BF16)
    ti = lax.broadcasted_iota(jnp.int32, (tm, tm), 0)
    tj = lax.broadcasted_iota(jnp.int32, (tm, tm), 1)
    rank = _dot(selb, jnp.where(ti < tj, 1.0, 0.0).astype(BF16))
    seg_units = jnp.ceil(jnp.sum(sel, axis=1, keepdims=True) * (1.0 / SEG_ALIGN))
    ei = lax.broadcasted_iota(jnp.int32, (N_EXPERTS, N_EXPERTS), 0)
    ej = lax.broadcasted_iota(jnp.int32, (N_EXPERTS, N_EXPERTS), 1)
    units_row = jnp.broadcast_to(seg_units, (N_EXPERTS, LANES)).astype(BF16)
    seg_off = _dot(jnp.where(ej < ei, 1.0, 0.0).astype(BF16), units_row)[:, 0:1] * SEG_ALIGN
    seg_rows = seg_units * SEG_ALIGN
    slot = seg_off + rank
    for k in range(TOP_K):
        w8_ref[k:k + 1, :] = wts[k] / wsum * ROUTED_SCALE
        ls8_ref[k:k + 1, :] = (jnp.sum(jnp.where(eid == picks[k], slot, 0.0), axis=0, keepdims=True)
                               * ROW_SUB).astype(jnp.int32)
    lane = lax.broadcasted_iota(jnp.int32, (N_EXPERTS, LANES), 1)
    seg_ref[0] = jnp.where(lane == 0, cnt_ref[...], jnp.where(lane == 1, seg_rows, seg_off))
    cnt_ref[...] = cnt_ref[...] + seg_rows


def _merge_kernel(x_ref, ya_ref, on_ref, go_ref, ga_ref, gb_ref, g1_ref, sc_ref, sh_ref, n2_ref,
                  pa_ref, pb_ref, wo_ref, wr_ref, br_ref, pt_ref,
                  x1_ref, h2_ref, ls8_ref, w8_ref, seg_ref, cnt_ref, fold_a, fold_b):
    @pl.when(pl.program_id(0) == 0)
    def _():
        cnt_ref[...] = jnp.zeros_like(cnt_ref)

    go = go_ref[...].astype(F32)
    on = _dot(pt_ref[...], on_ref[0].reshape(x_ref.shape[0], HG_WIDTH))
    y_b = (on * (go * _sigmoid(go))).astype(BF16)
    y_a = _unfold_rows(ya_ref[...].astype(F32), fold_a, fold_b).astype(BF16)
    pa = _dot(y_a, pa_ref[...])
    pb = _dot(y_b, pb_ref[...])
    merged = _sigmoid(ga_ref[...].astype(F32)) * pa + _sigmoid(gb_ref[...].astype(F32)) * pb
    x1 = x_ref[...] + g1_ref[0] * _dot(merged.astype(BF16), wo_ref[...])
    x1_ref[...] = x1
    y = x1 * lax.rsqrt(jnp.mean(x1 * x1, axis=-1, keepdims=True) + EPS) * n2_ref[...]
    h2 = y * (1.0 + sc_ref[0]) + sh_ref[0]
    _to_token_rows(h2_ref, h2)
    _route(h2.astype(BF16), wr_ref, br_ref, cnt_ref, ls8_ref, w8_ref, seg_ref)


def _merge(x2d, ya, on, go, ga, gb, g1, sc2, sh2, n2g, pa, pb, wo, wr_t, br, rows_per_batch, tm):
    n, d = x2d.shape
    per = rows_per_batch // tm
    row = lambda wd: pl.BlockSpec((tm, wd), lambda i: (i, 0))
    mod = pl.BlockSpec((1, 1, d), lambda i: (i // per, 0, 0))
    full = lambda a: pl.BlockSpec(a.shape, lambda i: (0, 0))
    tok = pl.BlockSpec((TOP_K, tm), lambda i: (0, i))
    return pl.pallas_call(
        _merge_kernel,
        out_shape=[jax.ShapeDtypeStruct((n, d), F32), jax.ShapeDtypeStruct((n * ROW_SUB, LANES), U32),
                   jax.ShapeDtypeStruct((TOP_K, n), jnp.int32), jax.ShapeDtypeStruct((TOP_K, n), F32),
                   jax.ShapeDtypeStruct((n // tm, N_EXPERTS, LANES), F32),
                   jax.ShapeDtypeStruct((N_EXPERTS, 1), F32)],
        grid=(n // tm,),
        in_specs=[row(d), pl.BlockSpec((tm // S5_T, S5_T * S5_WIDTH), lambda i: (i, 0)),
                  pl.BlockSpec((1, GRID_W, tm // GRID_W, HG_WIDTH), lambda i: (i // per, 0, i % per, 0)),
                  row(HG_WIDTH), row(d), row(d), mod, mod, mod,
                  pl.BlockSpec((1, d), lambda i: (0, 0)), full(pa), full(pb), full(wo), full(wr_t), full(br),
                  pl.BlockSpec((tm, tm), lambda i: (0, 0))],
        out_specs=[row(d), pl.BlockSpec((tm * ROW_SUB, LANES), lambda i: (i, 0)), tok, tok,
                   pl.BlockSpec((1, N_EXPERTS, LANES), lambda i: (i, 0, 0)),
                   pl.BlockSpec((N_EXPERTS, 1), lambda i: (0, 0))],
        scratch_shapes=[pltpu.VMEM((tm, LANES), F32), pltpu.VMEM((tm, LANES), F32)],
        compiler_params=_params("arbitrary"),
        name="merge_out_proj_route",
    )(x2d, ya, on, go, ga, gb, g1, sc2, sh2, n2g.reshape(1, d), pa, pb, wo, wr_t, br,
      _grid_transpose_matrix(tm).T)


MOE_TILE = TOK_TILE
TOKEN_UNROLL = 4
SEG_ALIGN = 8
FILL_ROWS = 512
STAGE_ROWS = MOE_TILE * TOP_K + FILL_ROWS
MOE_BLK = 1024


def _token_row(ref, r):
    return ref.at[pl.ds(pl.multiple_of(r * ROW_SUB, ROW_SUB), ROW_SUB)]


def _wait_rows(any_ref, sem, n_rows):
    view = any_ref.at[pl.ds(0, n_rows * ROW_SUB)]
    pltpu.make_async_copy(view, view, sem).wait()


def _rows(ref, r0, n):
    return ref.at[pl.ds(pl.multiple_of(r0 * ROW_SUB, ROW_SUB), n * ROW_SUB)]


def _pow2_pieces(n, max_piece, fn, min_piece=1):
    done = 0
    piece = max_piece
    while piece >= min_piece:
        hit = (n & piece) != 0
        pl.when(hit)(functools.partial(fn, done, piece))
        done = done + (n & piece)
        piece //= 2


def _copy_rows(src_ref, src0, dst_ref, dst0, n, max_piece, sem, min_piece=1):
    def piece(off, size):
        pltpu.make_async_copy(_rows(src_ref, src0 + off, size), _rows(dst_ref, dst0 + off, size), sem).start()
    _pow2_pieces(n, max_piece, piece, min_piece)


def _wait_copied_rows(src_ref, dst_ref, n, max_piece, sem):
    def piece(off, size):
        pltpu.make_async_copy(_rows(src_ref, 0, size), _rows(dst_ref, 0, size), sem).wait()
    _pow2_pieces(n, max_piece, piece)


def _copy_tile_segments(i, src_ref, src_tab, dst_ref, dst_tab, cnt_ref, off_ref, fill_src0, fill_dst0, sem):
    def per_expert(e, carry):
        _copy_rows(src_ref, src_tab[i, e], dst_ref, dst_tab[i, e], cnt_ref[i, e], MOE_TILE, sem, SEG_ALIGN)
        return carry

    lax.fori_loop(0, N_EXPERTS, per_expert, 0)
    used = off_ref[i, N_EXPERTS - 1] + cnt_ref[i, N_EXPERTS - 1]
    _copy_rows(src_ref, fill_src0(used), dst_ref, fill_dst0(used), STAGE_ROWS - used, FILL_ROWS, sem, SEG_ALIGN)


def _dispatch_kernel(gs_ref, cnt_ref, off_ref, pad_ref, ls_ref, h2_ref, xs_hbm, ls_smem, stage0, stage1, zbuf,
                     sem0, sem1, lsem, zsem, *, tm, n_blocks):
    i = pl.program_id(0)
    last = pl.num_programs(0) - 1
    cp = pltpu.make_async_copy(ls_ref, ls_smem, lsem)
    cp.start()
    trash0 = n_blocks * MOE_BLK

    @pl.when(i == 0)
    def _():
        stage0[...] = jnp.zeros_like(stage0)
        stage1[...] = jnp.zeros_like(stage1)
        zbuf[...] = jnp.zeros_like(zbuf)
        cpz = pltpu.make_async_copy(zbuf, _rows(xs_hbm, trash0, 2 * FILL_ROWS), zsem)
        cpz.start()
        cpz.wait()

    cp.wait()

    def tile(stage, sem, prev_sem, trash):
        def body(tu, carry):
            for u in range(TOKEN_UNROLL):
                t = tu * TOKEN_UNROLL + u
                row = h2_ref[pl.ds(pl.multiple_of(t * ROW_SUB, ROW_SUB), ROW_SUB), :]
                for k in range(TOP_K):
                    stage[pl.ds(pl.multiple_of(ls_smem[t * TOP_K + k], ROW_SUB), ROW_SUB), :] = row
            return carry

        lax.fori_loop(0, tm // TOKEN_UNROLL, body, 0)
        _copy_tile_segments(i, stage, off_ref, xs_hbm, gs_ref, cnt_ref, off_ref,
                            lambda used: used, lambda used: trash, sem)

        @pl.when(i > 0)
        def _():
            _wait_rows(xs_hbm, prev_sem, STAGE_ROWS)

        @pl.when(i == last)
        def _():
            _wait_rows(xs_hbm, sem, STAGE_ROWS)

    pl.when(i % 2 == 0)(functools.partial(tile, stage0, sem0, sem1, trash0))
    pl.when(i % 2 == 1)(functools.partial(tile, stage1, sem1, sem0, trash0 + FILL_ROWS))

    @pl.when(i == 0)
    def _():
        def start(e, carry):
            _copy_rows(zbuf, 0, xs_hbm, pad_ref[0, e], pad_ref[1, e], MOE_BLK // 2, zsem)
            return carry

        def wait(e, carry):
            _wait_copied_rows(zbuf, xs_hbm, pad_ref[1, e], MOE_BLK // 2, zsem)
            return carry

        lax.fori_loop(0, N_EXPERTS, start, 0)
        lax.fori_loop(0, N_EXPERTS, wait, 0)

        def zero_block(j, carry):
            pltpu.make_async_copy(zbuf, _rows(xs_hbm, j * MOE_BLK, MOE_BLK), zsem).start()
            return carry

        def wait_block(j, carry):
            pltpu.make_async_copy(zbuf, _rows(xs_hbm, 0, MOE_BLK), zsem).wait()
            return carry

        lax.fori_loop(pad_ref[2, 0], n_blocks, zero_block, 0)
        lax.fori_loop(pad_ref[2, 0], n_blocks, wait_block, 0)


def _dispatch(gstart, seg_cnt, seg_off, pad, ls8, h2_rows, n_blocks, tm):
    n = ls8.shape[0] // TOP_K
    cap = n_blocks * MOE_BLK + 2 * FILL_ROWS
    return pl.pallas_call(
        functools.partial(_dispatch_kernel, tm=tm, n_blocks=n_blocks),
        out_shape=jax.ShapeDtypeStruct((cap * ROW_SUB, LANES), U32),
        grid_spec=pltpu.PrefetchScalarGridSpec(
            num_scalar_prefetch=4,
            grid=(n // tm,),
            in_specs=[pl.BlockSpec((tm * TOP_K,), lambda i, *_: (i,)),
                      pl.BlockSpec((tm * ROW_SUB, LANES), lambda i, *_: (i, 0))],
            out_specs=pl.BlockSpec(memory_space=pl.ANY),
            scratch_shapes=[pltpu.SMEM((tm * TOP_K,), jnp.int32),
                            pltpu.VMEM((STAGE_ROWS * ROW_SUB, LANES), U32),
                            pltpu.VMEM((STAGE_ROWS * ROW_SUB, LANES), U32),
                            pltpu.VMEM((MOE_BLK * ROW_SUB, LANES), U32),
                            pltpu.SemaphoreType.DMA, pltpu.SemaphoreType.DMA, pltpu.SemaphoreType.DMA,
                            pltpu.SemaphoreType.DMA]),
        compiler_params=pltpu.CompilerParams(dimension_semantics=("arbitrary",), vmem_limit_bytes=VMEM_LIMIT,
                                             has_side_effects=True),
        name="moe_dispatch",
    )(gstart, seg_cnt, seg_off, pad, ls8, h2_rows)


def _expert_kernel(be_ref, nu_ref, x_ref, w1_ref, w3_ref, w2_ref, o_ref, w1b, w3b, w2b):
    j = pl.program_id(0)
    e = be_ref[j]
    prev = be_ref[jnp.maximum(j - 1, 0)]
    used = j < nu_ref[0]

    @pl.when(jnp.logical_and(used, jnp.logical_or(j == 0, e != prev)))
    def _():
        w1b[...] = w1_ref[0].astype(BF16)
        w3b[...] = w3_ref[0].astype(BF16)
        w2b[...] = w2_ref[0].astype(BF16)

    @pl.when(used)
    def _():
        x = _from_token_rows(x_ref, MOE_BLK).astype(BF16)
        a = _dot(x, w1b[...])
        hid = (a * _sigmoid(a)) * _dot(x, w3b[...])
        _to_token_rows(o_ref, _dot(hid.astype(BF16), w2b[...]))

    @pl.when(jnp.logical_not(used))
    def _():
        o_ref[...] = jnp.zeros_like(o_ref)


def _experts(block_e, n_used, xs, w1, w3, w2):
    n_blocks = block_e.shape[0]
    d, f = w1.shape[1], w1.shape[2]
    rows = pl.BlockSpec((MOE_BLK * ROW_SUB, LANES), lambda j, be, nu: (j, 0))
    rows_in = pl.BlockSpec((MOE_BLK * ROW_SUB, LANES), lambda j, be, nu: (jnp.minimum(j, nu[0] - 1), 0))
    return pl.pallas_call(
        _expert_kernel,
        out_shape=jax.ShapeDtypeStruct((n_blocks * MOE_BLK * ROW_SUB, LANES), U32),
        grid_spec=pltpu.PrefetchScalarGridSpec(
            num_scalar_prefetch=2,
            grid=(n_blocks,),
            in_specs=[rows_in,
                      pl.BlockSpec((1, d, f), lambda j, be, nu: (be[j], 0, 0)),
                      pl.BlockSpec((1, d, f), lambda j, be, nu: (be[j], 0, 0)),
                      pl.BlockSpec((1, f, d), lambda j, be, nu: (be[j], 0, 0))],
            out_specs=rows,
            scratch_shapes=[pltpu.VMEM((d, f), BF16), pltpu.VMEM((d, f), BF16), pltpu.VMEM((f, d), BF16)]),
        compiler_params=_params("arbitrary"),
        name="moe_experts",
    )(block_e, n_used, xs, w1, w3, w2)


def _combine_kernel(gs_ref, cnt_ref, off_ref, ls_ref, w8_ref, x1_ref, h2_ref, g2_ref, ws1_ref, ws3_ref, ws2_ref,
                    fg_ref, ys_hbm, o_ref, ls_smem, w_smem, gbuf0, gbuf1, acc_rows, sem0, sem1, lsem, *, tm):
    i = pl.program_id(0)
    last = pl.num_programs(0) - 1
    cp1 = pltpu.make_async_copy(ls_ref, ls_smem, lsem)
    cp2 = pltpu.make_async_copy(w8_ref, w_smem, lsem)
    cp1.start()
    cp2.start()

    def fetch(tile, gbuf, sem):
        _copy_tile_segments(tile, ys_hbm, gs_ref, gbuf, off_ref, cnt_ref, off_ref,
                            lambda used: 0, lambda used: used, sem)

    @pl.when(i == 0)
    def _():
        fetch(0, gbuf0, sem0)

    @pl.when(jnp.logical_and(i < last, i % 2 == 0))
    def _():
        fetch(i + 1, gbuf1, sem1)

    @pl.when(jnp.logical_and(i < last, i % 2 == 1))
    def _():
        fetch(i + 1, gbuf0, sem0)

    h2 = _from_token_rows(h2_ref, tm).astype(BF16)
    a = _dot(h2, ws1_ref[...])
    hid = (a * _sigmoid(a)) * _dot(h2, ws3_ref[...])
    acc = _dot(hid.astype(BF16), ws2_ref[...])
    cp1.wait()
    cp2.wait()

    def reduce_rows(gbuf, sem):
        _wait_rows(gbuf, sem, STAGE_ROWS)

        def body(tu, carry):
            for u in range(TOKEN_UNROLL):
                t = tu * TOKEN_UNROLL + u
                lo = jnp.zeros((ROW_SUB, LANES), F32)
                hi = jnp.zeros((ROW_SUB, LANES), F32)
                for k in range(TOP_K):
                    w = w_smem[t * TOP_K + k]
                    words = gbuf[pl.ds(pl.multiple_of(ls_smem[t * TOP_K + k], ROW_SUB), ROW_SUB), :]
                    lo = lo + w * lax.bitcast_convert_type(words << 16, F32)
                    hi = hi + w * lax.bitcast_convert_type(words & jnp.uint32(0xFFFF0000), F32)
                acc_rows[pl.ds(pl.multiple_of(t * SUBLANES, SUBLANES), ROW_SUB), :] = lo
                acc_rows[pl.ds(pl.multiple_of(t * SUBLANES, SUBLANES) + ROW_SUB, ROW_SUB), :] = hi
            return carry

        lax.fori_loop(0, tm // TOKEN_UNROLL, body, 0)

    pl.when(i % 2 == 0)(functools.partial(reduce_rows, gbuf0, sem0))
    pl.when(i % 2 == 1)(functools.partial(reduce_rows, gbuf1, sem1))
    routed = jnp.concatenate([acc_rows[pl.ds(s, tm, stride=SUBLANES), :] for s in range(SUBLANES)], axis=-1)
    y = x1_ref[...] + g2_ref[0] * (acc + routed)
    o_ref[...] = y * lax.rsqrt(jnp.mean(y * y, axis=-1, keepdims=True) + EPS) * fg_ref[...]


def _combine(gstart, seg_cnt, seg_off, ls8, w8, x1, h2_rows, g2, ws1, ws3, ws2, fg, ys, rows_per_batch, tm):
    n, d = x1.shape
    per = rows_per_batch // tm
    tok = pl.BlockSpec((tm * TOP_K,), lambda i, *_: (i,))
    full = lambda a: pl.BlockSpec(a.shape, lambda i, *_: (0, 0))
    return pl.pallas_call(
        functools.partial(_combine_kernel, tm=tm),
        out_shape=jax.ShapeDtypeStruct((n, d), F32),
        grid_spec=pltpu.PrefetchScalarGridSpec(
            num_scalar_prefetch=3,
            grid=(n // tm,),
            in_specs=[tok, tok, pl.BlockSpec((tm, d), lambda i, *_: (i, 0)),
                      pl.BlockSpec((tm * ROW_SUB, LANES), lambda i, *_: (i, 0)),
                      pl.BlockSpec((1, 1, d), lambda i, *_: (i // per, 0, 0)),
                      full(ws1), full(ws3), full(ws2), pl.BlockSpec((1, d), lambda i, *_: (0, 0)),
                      pl.BlockSpec(memory_space=pl.ANY)],
            out_specs=pl.BlockSpec((tm, d), lambda i, *_: (i, 0)),
            scratch_shapes=[pltpu.SMEM((tm * TOP_K,), jnp.int32), pltpu.SMEM((tm * TOP_K,), F32),
                            pltpu.VMEM((STAGE_ROWS * ROW_SUB, LANES), U32),
                            pltpu.VMEM((STAGE_ROWS * ROW_SUB, LANES), U32),
                            pltpu.VMEM((tm * SUBLANES, LANES), F32), pltpu.SemaphoreType.DMA,
                            pltpu.SemaphoreType.DMA, pltpu.SemaphoreType.DMA]),
        compiler_params=_params("arbitrary"),
        name="moe_combine_final",
    )(gstart, seg_cnt, seg_off, ls8, w8, x1, h2_rows, g2, ws1, ws3, ws2, fg.reshape(1, d), ys)


def _moe_plan(seg, counts, n_assign):
    cnt = counts.reshape(N_EXPERTS).astype(jnp.int32)
    padded = (cnt + MOE_BLK - 1) // MOE_BLK * MOE_BLK
    pends = jnp.cumsum(padded)
    pstarts = pends - padded
    max_rows = n_assign + seg.shape[0] * N_EXPERTS * (SEG_ALIGN - 1)
    n_blocks = (max_rows + N_EXPERTS * (MOE_BLK - 1) + MOE_BLK - 1) // MOE_BLK
    seg = seg[:, :, :3].astype(jnp.int32)
    gstart = pstarts[None, :] + seg[:, :, 0]
    blk_start = jnp.arange(n_blocks, dtype=jnp.int32) * MOE_BLK
    block_e = jnp.minimum(jnp.sum((blk_start[:, None] >= pends[None, :]).astype(jnp.int32), axis=1),
                          N_EXPERTS - 1).astype(jnp.int32)
    n_used = (pends[-1:] // MOE_BLK).astype(jnp.int32)
    pad = jnp.stack([pstarts + cnt, padded - cnt, jnp.broadcast_to(n_used, (N_EXPERTS,))], axis=0).astype(jnp.int32)
    return gstart, seg[:, :, 1], seg[:, :, 2], pad, block_e, n_used, n_blocks


def _mixer(x, c, ctx, c_ctx, w_ada, b_ada, norm1_g, norm2_g, w_in, s5_lam_re, s5_lam_im, s5_log_dt,
           s5_b_re, s5_b_im, s5_c_re, s5_c_im, s5_d, s5_w_glu, lb, hg_norm_g, p_a, p_b, w_out,
           moe_w_router, moe_b_router):
    b, l, d = x.shape
    lc = ctx.shape[1]
    n = b * l
    rows = l // GRID_W

    c8 = jnp.concatenate([c, c_ctx[None], jnp.zeros((8 - b - 1, d), F32)], axis=0)
    mod = _ada(c8, w_ada, b_ada)
    sh1, sc1, g1, sh2, sc2, g2 = [mod[:b, k * d:(k + 1) * d].reshape(b, 1, d) for k in range(6)]
    csh1, csc1 = mod[b:b + 1, 0:d].reshape(1, 1, d), mod[b:b + 1, d:2 * d].reshape(1, 1, d)

    w_in_b = w_in.astype(BF16)
    z = dict(zip([p[0] for p in _IN_PIECES],
                 _inproj(x.reshape(n, d), sc1, sh1, norm1_g, w_in_b, l, TOK_TILE, True)))
    zc = dict(zip([p[0] for p in _IN_PIECES],
                  _inproj(ctx.reshape(b * lc, d), csc1, csh1, norm1_g, w_in_b, lc, lc, False)))

    cx = lambda t: t.reshape(b, lc, HG_WIDTH)
    lb_row = lb.reshape(1, HG_WIDTH)
    o_f = _hgrn_pass(z["q"], z["ff"], z["i"], cx(zc["ff"]), cx(zc["i"]), lb_row, None, None, reverse=False)
    o_n = _hgrn_pass(z["q"], z["fb"], z["i"], cx(zc["fb"]), cx(zc["i"]), lb_row, o_f,
                     hg_norm_g.reshape(1, HG_DK), reverse=True)

    d_lag, w_s5_in, w_out_f, w_out_b, decay = _s5_weights(s5_lam_re, s5_lam_im, s5_log_dt, s5_b_re, s5_b_im,
                                                          s5_c_re, s5_c_im)
    kc, kl = lc // S5_T, l // S5_T
    u_lat = z["u"].reshape(b, kl, S5_T * S5_WIDTH)
    u_ctx = zc["u"].reshape(b, kc, S5_T * S5_WIDTH)
    rows_in = kl + kc
    u_ext = jnp.concatenate([u_lat, u_ctx], axis=1).reshape(b * rows_in, S5_T * S5_WIDTH)
    e = _s5_in(u_ext, d_lag, w_s5_in, (b * rows_in) // 2)
    states = _s5_scan(e, decay, b, rows_in, kl)
    d_row = s5_d.astype(F32).reshape(1, S5_WIDTH)
    y_a = _s5_out(states, w_out_f, w_out_b, e.reshape(b, rows_in, -1), z["u"], d_row, s5_w_glu.astype(BF16))

    return _merge(x.reshape(n, d), y_a, o_n, z["go"], z["ga"], z["gb"], g1, sc2, sh2, norm2_g,
                  p_a.astype(BF16), p_b.astype(BF16), w_out.astype(BF16),
                  moe_w_router.T.astype(BF16), moe_b_router.astype(F32).reshape(N_EXPERTS, 1), l, MOE_TILE) + (g2,)


def kernel(x, c, ctx, c_ctx, w_ada, b_ada, norm1_g, norm2_g, w_in, s5_lam_re, s5_lam_im, s5_log_dt, s5_b_re,
           s5_b_im, s5_c_re, s5_c_im, s5_d, s5_w_glu, hg_lb_logits, hg_norm_g, p_a, p_b, w_out, moe_w_router,
           moe_b_router, moe_w1, moe_w3, moe_w2, moe_ws1, moe_ws3, moe_ws2, final_norm_g):
    b, l, d = x.shape
    n = b * l
    assert w_ada.shape[0] == 1, "single-layer block"
    lb = jnp.cumsum(jax.nn.softmax(hg_lb_logits.astype(F32), axis=0), axis=0)[0]
    x1, h2_rows, ls8, w8, seg, counts, g2 = _mixer(
        x, c, ctx, c_ctx, w_ada[0], b_ada[0], norm1_g[0], norm2_g[0], w_in[0], s5_lam_re[0], s5_lam_im[0],
        s5_log_dt[0], s5_b_re[0], s5_b_im[0], s5_c_re[0], s5_c_im[0], s5_d[0], s5_w_glu[0], lb, hg_norm_g[0],
        p_a[0], p_b[0], w_out[0], moe_w_router[0], moe_b_router[0])
    gstart, seg_cnt, seg_off, pad, block_e, n_used, n_blocks = _moe_plan(seg, counts, n * TOP_K)
    ls_flat, w_flat = ls8.T.reshape(n * TOP_K), w8.T.reshape(n * TOP_K)
    xs = _dispatch(gstart, seg_cnt, seg_off, pad, ls_flat, h2_rows, n_blocks, MOE_TILE)
    ys = _experts(block_e, n_used, xs, moe_w1[0], moe_w3[0], moe_w2[0])
    out = _combine(gstart, seg_cnt, seg_off, ls_flat, w_flat, x1, h2_rows, g2, moe_ws1[0].astype(BF16),
                   moe_ws3[0].astype(BF16), moe_ws2[0].astype(BF16), final_norm_g, ys, l, MOE_TILE)
    return out.reshape(b, l, d)
```

```python
import functools
import math

import jax
import jax.numpy as jnp
from jax import lax
from jax.experimental import pallas as pl
from jax.experimental.pallas import tpu as pltpu

F32 = jnp.float32
BF16 = jnp.bfloat16

GRID_W = 64
S5_WIDTH = 256
S5_GROUP = 16
S5_GROUPS = 16
S5_STATE = 64
HG_HEADS = 6
HG_DK = 128
HG_WIDTH = HG_HEADS * HG_DK
N_EXPERTS = 64
ROUTE_GROUPS = 8
TOPK_GROUPS = 4
TOP_K = 8
ROUTED_SCALE = 2.5
EPS = 1e-6

LANES = 128
SUBLANES = 8

TOK_TILE = 512
S5_T = 16
HG_CHUNK = 64
HG_BATCH = 4
VMEM_LIMIT = 56 * 1024 * 1024

_NT = (((1,), (1,)), ((), ()))
_TN = (((0,), (0,)), ((), ()))


def _params(*sem):
    return pltpu.CompilerParams(dimension_semantics=sem, vmem_limit_bytes=VMEM_LIMIT)


def _dot(a, b):
    return jnp.dot(a, b, preferred_element_type=F32)


def _sigmoid(x):
    return 1.0 / (1.0 + jnp.exp(-x))


def _ada_kernel(c_ref, w_ref, b_ref, o_ref):
    c = c_ref[...]
    s = (c * _sigmoid(c)).astype(BF16)
    o_ref[...] = _dot(s, w_ref[...].astype(BF16)) + b_ref[...]


def _ada(c8, w_ada, b_ada):
    d, n = w_ada.shape
    tn = 1536
    return pl.pallas_call(
        _ada_kernel,
        out_shape=jax.ShapeDtypeStruct((8, n), F32),
        grid=(n // tn,),
        in_specs=[pl.BlockSpec((8, d), lambda j: (0, 0)),
                  pl.BlockSpec((d, tn), lambda j: (0, j)),
                  pl.BlockSpec((1, tn), lambda j: (0, j))],
        out_specs=pl.BlockSpec((8, tn), lambda j: (0, j)),
        compiler_params=_params("arbitrary"),
        name="ada_mod",
    )(c8, w_ada, b_ada.reshape(1, n))


_IN_PIECES = (("u", 0, 256, BF16), ("q", 256, 768, BF16), ("ff", 1024, 768, BF16),
              ("fb", 1792, 768, BF16), ("i", 2560, 768, BF16), ("go", 3328, 768, BF16),
              ("ga", 4096, 1024, BF16), ("gb", 5120, 1024, BF16))


def _fold_rows(val, buf_a, buf_b):
    t = val.shape[0]
    buf_a[...] = val[:, :LANES]
    buf_b[...] = val[:, LANES:]
    pieces = []
    for s in range(S5_T):
        pieces += [buf_a[pl.ds(s, t // S5_T, stride=S5_T), :], buf_b[pl.ds(s, t // S5_T, stride=S5_T), :]]
    return jnp.concatenate(pieces, axis=-1)


def _unfold_rows(val, buf_a, buf_b):
    r = val.shape[0]
    for s in range(S5_T):
        buf_a[pl.ds(s, r, stride=S5_T), :] = val[:, s * S5_WIDTH:s * S5_WIDTH + LANES]
        buf_b[pl.ds(s, r, stride=S5_T), :] = val[:, s * S5_WIDTH + LANES:(s + 1) * S5_WIDTH]
    return jnp.concatenate([buf_a[...], buf_b[...]], axis=-1)


def _grid_transpose_matrix(tm):
    i = jnp.arange(tm)
    src = (i % (tm // GRID_W)) * GRID_W + i // (tm // GRID_W)
    return (src[:, None] == jnp.arange(tm)[None, :]).astype(BF16)


def _inproj_kernel(x_ref, sc_ref, sh_ref, g_ref, w_ref, p_ref, *o_refs):
    o_refs, (fold_a, fold_b) = o_refs[:len(_IN_PIECES)], o_refs[len(_IN_PIECES):]
    x = x_ref[...]
    y = x * lax.rsqrt(jnp.mean(x * x, axis=-1, keepdims=True) + EPS) * g_ref[...]
    h = (y * (1.0 + sc_ref[0]) + sh_ref[0]).astype(BF16)
    h_cm = None
    for (name, a, wd, _), o_ref in zip(_IN_PIECES, o_refs):
        if name == "u":
            o_ref[...] = _fold_rows(_dot(h, w_ref[:, a:a + wd]), fold_a, fold_b).astype(o_ref.dtype)
        elif len(o_ref.shape) == 2:
            o_ref[...] = _dot(h, w_ref[:, a:a + wd]).astype(o_ref.dtype)
        else:
            if h_cm is None:
                h_cm = _dot(p_ref[...], h).astype(BF16)
            o_ref[0] = _dot(h_cm, w_ref[:, a:a + wd]).astype(o_ref.dtype).reshape(o_ref.shape[1:])


_COLMAJOR_PIECES = ("q", "ff", "fb", "i")


def _inproj(x2d, sc, sh, g, w_bf16, rows_per_mod, tm, colmajor):
    n, d = x2d.shape
    per = rows_per_mod // tm
    mod_map = (lambda i: (i // per, 0, 0)) if sc.shape[0] > 1 else (lambda i: (0, 0, 0))
    shapes, specs = [], []
    for name, _, wd, dt in _IN_PIECES:
        if colmajor and name in _COLMAJOR_PIECES:
            shapes.append(jax.ShapeDtypeStruct((n // rows_per_mod, GRID_W, rows_per_mod // GRID_W, wd), dt))
            specs.append(pl.BlockSpec((1, GRID_W, tm // GRID_W, wd), lambda i: (i // per, 0, i % per, 0)))
        elif name == "u":
            shapes.append(jax.ShapeDtypeStruct((n // S5_T, S5_T * wd), dt))
            specs.append(pl.BlockSpec((tm // S5_T, S5_T * wd), lambda i: (i, 0)))
        else:
            shapes.append(jax.ShapeDtypeStruct((n, wd), dt))
            specs.append(pl.BlockSpec((tm, wd), lambda i: (i, 0)))
    return pl.pallas_call(
        _inproj_kernel,
        out_shape=shapes,
        grid=(n // tm,),
        in_specs=[pl.BlockSpec((tm, d), lambda i: (i, 0)),
                  pl.BlockSpec((1, 1, d), mod_map),
                  pl.BlockSpec((1, 1, d), mod_map),
                  pl.BlockSpec((1, d), lambda i: (0, 0)),
                  pl.BlockSpec(w_bf16.shape, lambda i: (0, 0)),
                  pl.BlockSpec((tm, tm), lambda i: (0, 0))],
        out_specs=specs,
        scratch_shapes=[pltpu.VMEM((tm, LANES), F32), pltpu.VMEM((tm, LANES), F32)],
        compiler_params=_params("arbitrary"),
        name="in_proj",
    )(x2d, sc, sh, g.reshape(1, d), w_bf16, _grid_transpose_matrix(tm))


def _hgrn_gates(zf, lb):
    sig = _sigmoid(zf)
    logf = jnp.log(lb + (1.0 - lb) * sig)
    k = (1.0 - lb) * (1.0 - sig)
    return logf, k


def _chunk_cumsum(cs, logf):
    hi = logf.astype(BF16)
    lo = (logf - hi.astype(F32)).astype(BF16)
    return _dot(cs, hi) + _dot(cs, lo)


def _hgrn_state_step(zf, v, lb, st, cs, reverse):
    logf, k = _hgrn_gates(zf, lb)
    cum = _chunk_cumsum(cs, logf)
    t = 0 if reverse else HG_CHUNK - 1
    total = cum[t:t + 1, :]
    kdec = (k * jnp.exp(total - cum)).astype(BF16)
    st_new = st * jnp.exp(total) + lax.dot_general(v.astype(BF16), kdec, _TN, preferred_element_type=F32)
    return cum, k, st_new


def _hgrn_kernel(*refs, reverse, final, n_ctx_chunks):
    if final:
        q_all, f_all, v_all, cf_ref, cv_ref, lb_ref, of_all, g_ref, o_all, st_ref = refs
    else:
        q_all, f_all, v_all, cf_ref, cv_ref, lb_ref, o_all, st_ref = refs
        of_all = None
    n_batch = q_all.shape[0]
    c_len = HG_CHUNK
    n_rows = q_all.shape[2]
    n_chunks = n_rows // c_len
    row = lax.broadcasted_iota(jnp.int32, (n_rows, n_rows), 0)
    col = lax.broadcasted_iota(jnp.int32, (n_rows, n_rows), 1)
    tri = (col >= row) if reverse else (col <= row)
    same_chunk = None
    for c in range(n_chunks):
        lo, hi = c * c_len, (c + 1) * c_len
        blk = (row >= lo) & (row < hi) & (col >= lo) & (col < hi)
        same_chunk = blk if same_chunk is None else (same_chunk | blk)
    mask = tri & same_chunk
    cs = jnp.where(mask, 1.0, 0.0).astype(BF16)

    @pl.when(pl.program_id(1) == 0)
    def _():
        cs1 = cs[:c_len, :c_len]
        order = range(n_ctx_chunks - 1, -1, -1) if reverse else range(n_ctx_chunks)
        for bi in range(n_batch):
            for h in range(HG_HEADS):
                cols = slice(h * HG_DK, (h + 1) * HG_DK)
                st = jnp.zeros((HG_DK, HG_DK), F32)
                for c in order:
                    rows = slice(c * c_len, (c + 1) * c_len)
                    _, _, st = _hgrn_state_step(cf_ref[bi, rows, cols].astype(F32), cv_ref[bi, rows, cols].astype(F32),
                                                lb_ref[:, cols], st, cs1, reverse)
                st_ref[bi * HG_HEADS + h] = st

    def chunk_rows(x, r):
        return [x[c * c_len + r:c * c_len + r + 1, :] for c in range(n_chunks)]

    def over_chunks(rows):
        return jnp.concatenate([jnp.broadcast_to(r, (c_len, r.shape[1])) for r in rows], axis=0)

    lb = lb_ref[...]
    r_ref = c_len // 2 - 1 if reverse else c_len // 2
    r_tot = 0 if reverse else c_len - 1
    order = range(n_chunks - 1, -1, -1) if reverse else range(n_chunks)
    for bi in range(n_batch):
        q = q_all[bi, 0].astype(F32)
        v = v_all[bi, 0]
        logf, k = _hgrn_gates(f_all[bi, 0].astype(F32), lb)
        cum = _chunk_cumsum(cs, logf)
        ref_rows, tot_rows = chunk_rows(cum, r_ref), chunk_rows(cum, r_tot)
        ref = over_chunks(ref_rows)
        qe = q * jnp.exp(cum - ref)
        ke = k * jnp.exp(ref - cum)
        qi, ki = qe.astype(BF16), ke.astype(BF16)
        q_in = (qe * over_chunks([jnp.exp(r) for r in ref_rows])).astype(BF16)
        kdec = (ke * over_chunks([jnp.exp(t - r) for t, r in zip(tot_rows, ref_rows)])).astype(BF16)
        for h in range(HG_HEADS):
            cols = slice(h * HG_DK, (h + 1) * HG_DK)
            s = lax.dot_general(qi[:, cols], ki[:, cols], _NT, preferred_element_type=F32)
            o_intra = _dot(jnp.where(mask, s, 0.0).astype(BF16), v[:, cols])
            st = st_ref[bi * HG_HEADS + h]
            for c in order:
                rows = slice(c * c_len, (c + 1) * c_len)
                o = o_intra[rows] + lax.dot_general(q_in[rows, cols], st.astype(BF16), _NT,
                                                    preferred_element_type=F32)
                total = cum[c * c_len + r_tot:c * c_len + r_tot + 1, cols]
                st = st * jnp.exp(total) + lax.dot_general(v[rows, cols], kdec[rows, cols], _TN,
                                                           preferred_element_type=F32)
                if final:
                    o = o + of_all[bi, 0, rows, cols].astype(F32)
                    o = o * lax.rsqrt(jnp.mean(o * o, axis=-1, keepdims=True) + EPS) * g_ref[...]
                o_all[bi, 0, rows, cols] = o.astype(o_all.dtype)
            st_ref[bi * HG_HEADS + h] = st


def _hgrn_pass(q, f, v, cf, cv, lb, o_prev, g, *, reverse):
    b, nw, rows, _ = q.shape
    nb = HG_BATCH if b % HG_BATCH == 0 else 1
    final = o_prev is not None
    wmap = (lambda bi, w: (bi, nw - 1 - w, 0, 0)) if reverse else (lambda bi, w: (bi, w, 0, 0))
    blk = pl.BlockSpec((nb, 1, rows, HG_WIDTH), wmap)
    cblk = pl.BlockSpec((nb, cf.shape[1], HG_WIDTH), lambda bi, w: (bi, 0, 0))
    in_specs = [blk, blk, blk, cblk, cblk, pl.BlockSpec((1, HG_WIDTH), lambda bi, w: (0, 0))]
    args = [q, f, v, cf, cv, lb]
    if final:
        in_specs += [blk, pl.BlockSpec((1, HG_DK), lambda bi, w: (0, 0))]
        args += [o_prev, g]
    return pl.pallas_call(
        functools.partial(_hgrn_kernel, reverse=reverse, final=final, n_ctx_chunks=cf.shape[1] // HG_CHUNK),
        out_shape=jax.ShapeDtypeStruct(q.shape, BF16),
        grid=(b // nb, nw),
        in_specs=in_specs,
        out_specs=blk,
        scratch_shapes=[pltpu.VMEM((nb * HG_HEADS, HG_DK, HG_DK), F32)],
        compiler_params=_params("arbitrary", "arbitrary"),
        name="hgrn_bwd" if reverse else "hgrn_fwd",
    )(*args)


def _s5_weights(lam_re, lam_im, log_dt, b_re, b_im, c_re, c_im):
    hp = lax.Precision.HIGHEST
    g, p, cc, t = S5_GROUPS, S5_STATE, S5_GROUP, S5_T
    lre = jnp.minimum(lam_re.astype(F32), -1e-4)
    lim = lam_im.astype(F32)
    dt = jnp.exp(log_dt.astype(F32))[..., None]
    ks = jnp.arange(t + 1, dtype=F32)[:, None, None, None]
    mag = jnp.exp(ks * (lre * dt)[None])
    pw_re = mag * jnp.cos(ks * (lim * dt)[None])
    pw_im = mag * jnp.sin(ks * (lim * dt)[None])
    nr, ni = pw_re[1] - 1.0, pw_im[1]
    den = lre * lre + lim * lim
    cf_re = (nr * lre + ni * lim) / den
    cf_im = (ni * lre - nr * lim) / den
    bb_re = cf_re[..., None] * b_re - cf_im[..., None] * b_im
    bb_im = cf_re[..., None] * b_im + cf_im[..., None] * b_re
    cre, cim = c_re.astype(F32), c_im.astype(F32)
    sw, ns = S5_WIDTH, 2 * g * p
    grp_of_row = jnp.arange(sw)[:, None] // cc

    cp_re = cre[None, None, :, :, :] * pw_re[:t, :, :, None, :] - cim[None, None] * pw_im[:t, :, :, None, :]
    cp_im = cre[None, None, :, :, :] * pw_im[:t, :, :, None, :] + cim[None, None] * pw_re[:t, :, :, None, :]
    def contract_p(cp, bb):
        return jnp.sum(cp.transpose(4, 1, 0, 2, 3)[..., None] * bb.transpose(2, 0, 1, 3)[:, :, None, :, None, :],
                       axis=0)

    kk = contract_p(cp_re, bb_re) - contract_p(cp_im, bb_im)
    kf, kb = kk[0], kk[1]
    kall = jnp.concatenate([kb[:0:-1], (kf[0] + kb[0])[None], kf[1:]], axis=0)
    kt = kall.transpose(0, 1, 3, 2).reshape(2 * t - 1, sw, cc)

    def spread(x, period, reps):
        sel = (jnp.arange(period)[:, None] == (jnp.arange(period * reps)[None, :] % period)).astype(BF16)
        return jnp.dot(x.astype(BF16), sel, preferred_element_type=BF16)

    same = grp_of_row == (jnp.arange(sw)[None, :] // cc)
    d_lag = jnp.where(same[None], spread(kt, cc, g), 0)

    same_in = jnp.tile(grp_of_row, (t, 1)) == ((jnp.arange(ns)[None, :] % (g * p)) // p)

    def in_to_state(pre, pim, bre, bim):
        xre = pre[..., None] * bre[None] - pim[..., None] * bim[None]
        xim = pre[..., None] * bim[None] + pim[..., None] * bre[None]
        return [spread(xre.transpose(0, 1, 3, 2).reshape(t * sw, p), p, g),
                spread(xim.transpose(0, 1, 3, 2).reshape(t * sw, p), p, g)]

    w_in = jnp.concatenate(in_to_state(pw_re[t - 1::-1, 0], pw_im[t - 1::-1, 0], bb_re[0], bb_im[0])
                           + in_to_state(pw_re[:t, 1], pw_im[:t, 1], bb_re[1], bb_im[1]), axis=1)
    w_in = jnp.where(jnp.tile(same_in, (1, 2)), w_in, 0)

    same_out = ((jnp.arange(ns)[:, None] % (g * p)) // p) == ((jnp.arange(t * sw)[None, :] // cc) % g)
    col = jnp.arange(t * sw)
    pick = (jnp.arange(t * cc)[:, None] == ((col // sw) * cc + col % cc)[None, :]).astype(BF16)

    def state_to_out(pre, pim):
        are = cre[None] * pre[:, :, None, :] - cim[None] * pim[:, :, None, :]
        aim = cre[None] * pim[:, :, None, :] + cim[None] * pre[:, :, None, :]
        a = jnp.concatenate([are.transpose(1, 3, 0, 2), -aim.transpose(1, 3, 0, 2)], axis=0)
        a = jnp.dot(a.reshape(ns, t * cc).astype(BF16), pick, preferred_element_type=BF16)
        return jnp.where(same_out, a, 0)

    w_out_f = state_to_out(pw_re[1:, 0], pw_im[1:, 0])
    w_out_b = state_to_out(pw_re[t:0:-1, 1], pw_im[t:0:-1, 1])

    decay = jnp.stack([pw_re[t].reshape(2, g * p), pw_im[t].reshape(2, g * p)], axis=1)
    return d_lag, w_in, w_out_f, w_out_b, decay


def _s5_in_kernel(u_ref, d_ref, w_ref, o_ref):
    j = pl.program_id(1)

    @pl.when(j < S5_T)
    def _():
        acc = _dot(u_ref[:, 0:S5_WIDTH], d_ref[j + S5_T - 1])
        for s in range(1, S5_T):
            acc = acc + _dot(u_ref[:, s * S5_WIDTH:(s + 1) * S5_WIDTH], d_ref[j - s + S5_T - 1])
        o_ref[...] = acc

    @pl.when(j >= S5_T)
    def _():
        o_ref[...] = _dot(u_ref[...], w_ref[...])


def _s5_in(u, d_lag, w_in, tm):
    m, k = u.shape
    tn = S5_WIDTH
    nj = (k + w_in.shape[1]) // tn
    return pl.pallas_call(
        _s5_in_kernel,
        out_shape=jax.ShapeDtypeStruct((m, nj * tn), F32),
        grid=(m // tm, nj),
        in_specs=[pl.BlockSpec((tm, k), lambda i, j: (i, 0)),
                  pl.BlockSpec(d_lag.shape, lambda i, j: (0, 0, 0)),
                  pl.BlockSpec((k, tn), lambda i, j: (0, jnp.maximum(j - S5_T, 0)))],
        out_specs=pl.BlockSpec((tm, tn), lambda i, j: (i, j)),
        compiler_params=_params("arbitrary", "arbitrary"),
        name="s5_in",
    )(u, d_lag, w_in)


def _s5_scan_kernel(efr_ref, efi_ref, ebr_ref, ebi_ref, a_ref, hfr_ref, hfi_ref, hbr_ref, hbi_ref,
                    *, nb, rows_in, rows_out):
    dirs = ((efr_ref, efi_ref, hfr_ref, hfi_ref, a_ref[0, 0:1, :], a_ref[0, 1:2, :]),
            (ebr_ref, ebi_ref, hbr_ref, hbi_ref, a_ref[1, 0:1, :], a_ref[1, 1:2, :]))
    zero = jnp.zeros_like(dirs[0][4])

    def step(srcs, carry, store):
        new = []
        for di, (er_ref, ei_ref, hr_ref, hi_ref, are, aim) in enumerate(dirs):
            for bi in range(nb):
                hre, him = carry[2 * (di * nb + bi)], carry[2 * (di * nb + bi) + 1]
                if store:
                    hr_ref[pl.ds(bi * rows_out + srcs[di], 1), :] = hre
                    hi_ref[pl.ds(bi * rows_out + srcs[di], 1), :] = him
                ere = er_ref[pl.ds(bi * rows_in + srcs[di], 1), :]
                eim = ei_ref[pl.ds(bi * rows_in + srcs[di], 1), :]
                new += [are * hre - aim * him + ere, are * him + aim * hre + eim]
        return tuple(new)

    n_ctx = rows_in - rows_out
    carry = lax.fori_loop(0, n_ctx, lambda s, c: step((rows_out + s, rows_in - 1 - s), c, False),
                          tuple([zero] * (4 * nb)))
    lax.fori_loop(0, rows_out, lambda s, c: step((s, rows_out - 1 - s), c, True), carry)


def _s5_scan(e, decay, nb, rows_in, rows_out):
    tc = 256
    nsr = S5_GROUPS * S5_STATE
    c0 = (S5_T * S5_WIDTH) // tc
    nt = nsr // tc
    eblk = lambda k: pl.BlockSpec((nb * rows_in, tc), lambda j: (0, c0 + k * nt + j))
    hblk = pl.BlockSpec((nb * rows_out, tc), lambda j: (0, j))
    return pl.pallas_call(
        functools.partial(_s5_scan_kernel, nb=nb, rows_in=rows_in, rows_out=rows_out),
        out_shape=[jax.ShapeDtypeStruct((nb * rows_out, nsr), F32)] * 4,
        grid=(nt,),
        in_specs=[eblk(0), eblk(1), eblk(2), eblk(3), pl.BlockSpec((2, 2, tc), lambda j: (0, 0, j))],
        out_specs=[hblk] * 4,
        compiler_params=_params("arbitrary"),
        name="s5_scan",
    )(e, e, e, e, decay)


def _gelu_tanh(x):
    return 0.5 * x * (1.0 + jnp.tanh(math.sqrt(2.0 / math.pi) * (x + 0.044715 * x * x * x)))


def _s5_out_kernel(hfr_ref, hfi_ref, hbr_ref, hbi_ref, wf_ref, wb_ref, yi_ref, u_ref, d_ref, wg_ref, o_ref):
    nsr = hfr_ref.shape[1]
    y = yi_ref[0] + d_ref[...] * u_ref[...].astype(F32)
    for h_ref, w_ref, r0 in ((hfr_ref, wf_ref, 0), (hfi_ref, wf_ref, nsr), (hbr_ref, wb_ref, 0), (hbi_ref, wb_ref, nsr)):
        y = y + _dot(h_ref[...].astype(BF16), w_ref[r0:r0 + nsr, :])
    y = _gelu_tanh(y)
    gate = _sigmoid(_dot(y.astype(BF16), wg_ref[...]))
    o_ref[...] = (y * gate).astype(o_ref.dtype)


def _s5_out(states, w_out_f, w_out_b, e3, u_rows, d_row, w_glu):
    m, nsr = states[0].shape
    nb = e3.shape[0]
    tm = m // nb
    tn = S5_WIDTH
    st = pl.BlockSpec((tm, nsr), lambda i, j: (i, 0))
    wo = pl.BlockSpec((2 * nsr, tn), lambda i, j: (0, j))
    return pl.pallas_call(
        _s5_out_kernel,
        out_shape=jax.ShapeDtypeStruct((m, S5_T * S5_WIDTH), BF16),
        grid=(nb, S5_T),
        in_specs=[st, st, st, st, wo, wo,
                  pl.BlockSpec((1, tm, tn), lambda i, j: (i, 0, j)),
                  pl.BlockSpec((tm, tn), lambda i, j: (i, j)),
                  pl.BlockSpec((1, tn), lambda i, j: (0, 0)),
                  pl.BlockSpec((tn, tn), lambda i, j: (0, 0))],
        out_specs=pl.BlockSpec((tm, tn), lambda i, j: (i, j)),
        compiler_params=_params("arbitrary", "arbitrary"),
        name="s5_out",
    )(*states, w_out_f, w_out_b, e3, u_rows, d_row, w_glu)


U32 = jnp.uint32
ROW_SUB = 4


def _to_token_rows(ref, val):
    t, d = val.shape
    bits = lax.bitcast_convert_type(val.astype(BF16).astype(F32), U32)
    w = (bits[:, :d // 2] >> 16) | bits[:, d // 2:]
    for s in range(ROW_SUB):
        ref[pl.ds(s, t, stride=ROW_SUB), :] = w[:, s * LANES:(s + 1) * LANES]


def _from_token_rows(ref, t, row0=0):
    w = jnp.concatenate([ref[pl.ds(row0 * ROW_SUB + s, t, stride=ROW_SUB), :] for s in range(ROW_SUB)], axis=-1)
    lo = lax.bitcast_convert_type(w << 16, F32)
    hi = lax.bitcast_convert_type(w & jnp.uint32(0xFFFF0000), F32)
    return jnp.concatenate([lo, hi], axis=-1)


def _route(h2b, wr_ref, br_ref, cnt_ref, ls8_ref, w8_ref, seg_ref):
    tm = h2b.shape[0]
    per_group = N_EXPERTS // ROUTE_GROUPS
    scores = _sigmoid(lax.dot_general(wr_ref[...], h2b, _NT, preferred_element_type=F32))
    biased = scores + br_ref[...]
    neg = -jnp.inf
    sub = lax.broadcasted_iota(jnp.int32, (per_group, tm), 0)
    grp = []
    for gi in range(ROUTE_GROUPS):
        v = biased[gi * per_group:(gi + 1) * per_group, :]
        m1 = jnp.max(v, axis=0, keepdims=True)
        first = jnp.min(jnp.where(v == m1, sub, per_group), axis=0, keepdims=True)
        m2 = jnp.max(jnp.where(sub == first, neg, v), axis=0, keepdims=True)
        grp.append(m1 + m2)
    grp = jnp.concatenate(grp, axis=0)
    gid = lax.broadcasted_iota(jnp.int32, (ROUTE_GROUPS, tm), 0)
    beaten = jnp.zeros((ROUTE_GROUPS, tm), jnp.int32)
    for gj in range(ROUTE_GROUPS):
        r = grp[gj:gj + 1, :]
        beaten = beaten + jnp.where((r > grp) | ((r == grp) & (gj < gid)), 1, 0)
    group_ok = beaten < TOPK_GROUPS
    expert_ok = jnp.concatenate(
        [jnp.broadcast_to(group_ok[gi:gi + 1, :], (per_group, tm)) for gi in range(ROUTE_GROUPS)], axis=0)
    cur = jnp.where(expert_ok, biased, neg)
    eid = lax.broadcasted_iota(jnp.int32, (N_EXPERTS, tm), 0)
    sel = jnp.zeros((N_EXPERTS, tm), F32)
    picks, wts = [], []
    for _ in range(TOP_K):
        m = jnp.max(cur, axis=0, keepdims=True)
        idx = jnp.min(jnp.where(cur == m, eid, N_EXPERTS), axis=0, keepdims=True)
        hit = eid == idx
        picks.append(idx)
        wts.append(jnp.sum(jnp.where(hit, scores, 0.0), axis=0, keepdims=True))
        sel = jnp.where(hit, 1.0, sel)
        cur = jnp.where(hit, neg, cur)
    wsum = wts[0]
    for w in wts[1:]:
        wsum = wsum + w
    selb = sel.astype(BF16)
    ti = lax.broadcasted_iota(jnp.int32, (tm, tm), 0)
    tj = lax.broadcasted_iota(jnp.int32, (tm, tm), 1)
    rank = _dot(selb, jnp.where(ti < tj, 1.0, 0.0).astype(BF16))
    seg_units = jnp.ceil(jnp.sum(sel, axis=1, keepdims=True) * (1.0 / SEG_ALIGN))
    ei = lax.broadcasted_iota(jnp.int32, (N_EXPERTS, N_EXPERTS), 0)
    ej = lax.broadcasted_iota(jnp.int32, (N_EXPERTS, N_EXPERTS), 1)
    units_row = jnp.broadcast_to(seg_units, (N_EXPERTS, LANES)).astype(BF16)
    seg_off = _dot(jnp.where(ej < ei, 1.0, 0.0).astype(BF16), units_row)[:, 0:1] * SEG_ALIGN
    seg_rows = seg_units * SEG_ALIGN
    slot = seg_off + rank
    for k in range(TOP_K):
        w8_ref[k:k + 1, :] = wts[k] / wsum * ROUTED_SCALE
        ls8_ref[k:k + 1, :] = (jnp.sum(jnp.where(eid == picks[k], slot, 0.0), axis=0, keepdims=True)
                               * ROW_SUB).astype(jnp.int32)
    lane = lax.broadcasted_iota(jnp.int32, (N_EXPERTS, LANES), 1)
    seg_ref[0] = jnp.where(lane == 0, cnt_ref[...], jnp.where(lane == 1, seg_rows, seg_off))
    cnt_ref[...] = cnt_ref[...] + seg_rows


def _merge_kernel(x_ref, ya_ref, on_ref, go_ref, ga_ref, gb_ref, g1_ref, sc_ref, sh_ref, n2_ref,
                  pa_ref, pb_ref, wo_ref, wr_ref, br_ref, pt_ref,
                  x1_ref, h2_ref, ls8_ref, w8_ref, seg_ref, cnt_ref, fold_a, fold_b):
    @pl.when(pl.program_id(0) == 0)
    def _():
        cnt_ref[...] = jnp.zeros_like(cnt_ref)

    go = go_ref[...].astype(F32)
    on = _dot(pt_ref[...], on_ref[0].reshape(x_ref.shape[0], HG_WIDTH))
    y_b = (on * (go * _sigmoid(go))).astype(BF16)
    y_a = _unfold_rows(ya_ref[...].astype(F32), fold_a, fold_b).astype(BF16)
    pa = _dot(y_a, pa_ref[...])
    pb = _dot(y_b, pb_ref[...])
    merged = _sigmoid(ga_ref[...].astype(F32)) * pa + _sigmoid(gb_ref[...].astype(F32)) * pb
    x1 = x_ref[...] + g1_ref[0] * _dot(merged.astype(BF16), wo_ref[...])
    x1_ref[...] = x1
    y = x1 * lax.rsqrt(jnp.mean(x1 * x1, axis=-1, keepdims=True) + EPS) * n2_ref[...]
    h2 = y * (1.0 + sc_ref[0]) + sh_ref[0]
    _to_token_rows(h2_ref, h2)
    _route(h2.astype(BF16), wr_ref, br_ref, cnt_ref, ls8_ref, w8_ref, seg_ref)


def _merge(x2d, ya, on, go, ga, gb, g1, sc2, sh2, n2g, pa, pb, wo, wr_t, br, rows_per_batch, tm):
    n, d = x2d.shape
    per = rows_per_batch // tm
    row = lambda wd: pl.BlockSpec((tm, wd), lambda i: (i, 0))
    mod = pl.BlockSpec((1, 1, d), lambda i: (i // per, 0, 0))
    full = lambda a: pl.BlockSpec(a.shape, lambda i: (0, 0))
    tok = pl.BlockSpec((TOP_K, tm), lambda i: (0, i))
    return pl.pallas_call(
        _merge_kernel,
        out_shape=[jax.ShapeDtypeStruct((n, d), F32), jax.ShapeDtypeStruct((n * ROW_SUB, LANES), U32),
                   jax.ShapeDtypeStruct((TOP_K, n), jnp.int32), jax.ShapeDtypeStruct((TOP_K, n), F32),
                   jax.ShapeDtypeStruct((n // tm, N_EXPERTS, LANES), F32),
                   jax.ShapeDtypeStruct((N_EXPERTS, 1), F32)],
        grid=(n // tm,),
        in_specs=[row(d), pl.BlockSpec((tm // S5_T, S5_T * S5_WIDTH), lambda i: (i, 0)),
                  pl.BlockSpec((1, GRID_W, tm // GRID_W, HG_WIDTH), lambda i: (i // per, 0, i % per, 0)),
                  row(HG_WIDTH), row(d), row(d), mod, mod, mod,
                  pl.BlockSpec((1, d), lambda i: (0, 0)), full(pa), full(pb), full(wo), full(wr_t), full(br),
                  pl.BlockSpec((tm, tm), lambda i: (0, 0))],
        out_specs=[row(d), pl.BlockSpec((tm * ROW_SUB, LANES), lambda i: (i, 0)), tok, tok,
                   pl.BlockSpec((1, N_EXPERTS, LANES), lambda i: (i, 0, 0)),
                   pl.BlockSpec((N_EXPERTS, 1), lambda i: (0, 0))],
        scratch_shapes=[pltpu.VMEM((tm, LANES), F32), pltpu.VMEM((tm, LANES), F32)],
        compiler_params=_params("arbitrary"),
        name="merge_out_proj_route",
    )(x2d, ya, on, go, ga, gb, g1, sc2, sh2, n2g.reshape(1, d), pa, pb, wo, wr_t, br,
      _grid_transpose_matrix(tm).T)


MOE_TILE = TOK_TILE
TOKEN_UNROLL = 4
SEG_ALIGN = 8
FILL_ROWS = 512
STAGE_ROWS = MOE_TILE * TOP_K + FILL_ROWS
MOE_BLK = 1024


def _token_row(ref, r):
    return ref.at[pl.ds(pl.multiple_of(r * ROW_SUB, ROW_SUB), ROW_SUB)]


def _wait_rows(any_ref, sem, n_rows):
    view = any_ref.at[pl.ds(0, n_rows * ROW_SUB)]
    pltpu.make_async_copy(view, view, sem).wait()


def _rows(ref, r0, n):
    return ref.at[pl.ds(pl.multiple_of(r0 * ROW_SUB, ROW_SUB), n * ROW_SUB)]


def _pow2_pieces(n, max_piece, fn, min_piece=1):
    done = 0
    piece = max_piece
    while piece >= min_piece:
        hit = (n & piece) != 0
        pl.when(hit)(functools.partial(fn, done, piece))
        done = done + (n & piece)
        piece //= 2


def _copy_rows(src_ref, src0, dst_ref, dst0, n, max_piece, sem, min_piece=1):
    def piece(off, size):
        pltpu.make_async_copy(_rows(src_ref, src0 + off, size), _rows(dst_ref, dst0 + off, size), sem).start()
    _pow2_pieces(n, max_piece, piece, min_piece)


def _wait_copied_rows(src_ref, dst_ref, n, max_piece, sem):
    def piece(off, size):
        pltpu.make_async_copy(_rows(src_ref, 0, size), _rows(dst_ref, 0, size), sem).wait()
    _pow2_pieces(n, max_piece, piece)


def _copy_tile_segments(i, src_ref, src_tab, dst_ref, dst_tab, cnt_ref, off_ref, fill_src0, fill_dst0, sem):
    def per_expert(e, carry):
        _copy_rows(src_ref, src_tab[i, e], dst_ref, dst_tab[i, e], cnt_ref[i, e], MOE_TILE, sem, SEG_ALIGN)
        return carry

    lax.fori_loop(0, N_EXPERTS, per_expert, 0)
    used = off_ref[i, N_EXPERTS - 1] + cnt_ref[i, N_EXPERTS - 1]
    _copy_rows(src_ref, fill_src0(used), dst_ref, fill_dst0(used), STAGE_ROWS - used, FILL_ROWS, sem, SEG_ALIGN)


def _dispatch_kernel(gs_ref, cnt_ref, off_ref, pad_ref, ls_ref, h2_ref, xs_hbm, ls_smem, stage0, stage1, zbuf,
                     sem0, sem1, lsem, zsem, *, tm, n_blocks):
    i = pl.program_id(0)
    last = pl.num_programs(0) - 1
    cp = pltpu.make_async_copy(ls_ref, ls_smem, lsem)
    cp.start()
    trash0 = n_blocks * MOE_BLK

    @pl.when(i == 0)
    def _():
        stage0[...] = jnp.zeros_like(stage0)
        stage1[...] = jnp.zeros_like(stage1)
        zbuf[...] = jnp.zeros_like(zbuf)
        cpz = pltpu.make_async_copy(zbuf, _rows(xs_hbm, trash0, 2 * FILL_ROWS), zsem)
        cpz.start()
        cpz.wait()

    cp.wait()

    def tile(stage, sem, prev_sem, trash):
        def body(tu, carry):
            for u in range(TOKEN_UNROLL):
                t = tu * TOKEN_UNROLL + u
                row = h2_ref[pl.ds(pl.multiple_of(t * ROW_SUB, ROW_SUB), ROW_SUB), :]
                for k in range(TOP_K):
                    stage[pl.ds(pl.multiple_of(ls_smem[t * TOP_K + k], ROW_SUB), ROW_SUB), :] = row
            return carry

        lax.fori_loop(0, tm // TOKEN_UNROLL, body, 0)
        _copy_tile_segments(i, stage, off_ref, xs_hbm, gs_ref, cnt_ref, off_ref,
                            lambda used: used, lambda used: trash, sem)

        @pl.when(i > 0)
        def _():
            _wait_rows(xs_hbm, prev_sem, STAGE_ROWS)

        @pl.when(i == last)
        def _():
            _wait_rows(xs_hbm, sem, STAGE_ROWS)

    pl.when(i % 2 == 0)(functools.partial(tile, stage0, sem0, sem1, trash0))
    pl.when(i % 2 == 1)(functools.partial(tile, stage1, sem1, sem0, trash0 + FILL_ROWS))

    @pl.when(i == 0)
    def _():
        def start(e, carry):
            _copy_rows(zbuf, 0, xs_hbm, pad_ref[0, e], pad_ref[1, e], MOE_BLK // 2, zsem)
            return carry

        def wait(e, carry):
            _wait_copied_rows(zbuf, xs_hbm, pad_ref[1, e], MOE_BLK // 2, zsem)
            return carry

        lax.fori_loop(0, N_EXPERTS, start, 0)
        lax.fori_loop(0, N_EXPERTS, wait, 0)

        def zero_block(j, carry):
            pltpu.make_async_copy(zbuf, _rows(xs_hbm, j * MOE_BLK, MOE_BLK), zsem).start()
            return carry

        def wait_block(j, carry):
            pltpu.make_async_copy(zbuf, _rows(xs_hbm, 0, MOE_BLK), zsem).wait()
            return carry

        lax.fori_loop(pad_ref[2, 0], n_blocks, zero_block, 0)
        lax.fori_loop(pad_ref[2, 0], n_blocks, wait_block, 0)


def _dispatch(gstart, seg_cnt, seg_off, pad, ls8, h2_rows, n_blocks, tm):
    n = ls8.shape[0] // TOP_K
    cap = n_blocks * MOE_BLK + 2 * FILL_ROWS
    return pl.pallas_call(
        functools.partial(_dispatch_kernel, tm=tm, n_blocks=n_blocks),
        out_shape=jax.ShapeDtypeStruct((cap * ROW_SUB, LANES), U32),
        grid_spec=pltpu.PrefetchScalarGridSpec(
            num_scalar_prefetch=4,
            grid=(n // tm,),
            in_specs=[pl.BlockSpec((tm * TOP_K,), lambda i, *_: (i,)),
                      pl.BlockSpec((tm * ROW_SUB, LANES), lambda i, *_: (i, 0))],
            out_specs=pl.BlockSpec(memory_space=pl.ANY),
            scratch_shapes=[pltpu.SMEM((tm * TOP_K,), jnp.int32),
                            pltpu.VMEM((STAGE_ROWS * ROW_SUB, LANES), U32),
                            pltpu.VMEM((STAGE_ROWS * ROW_SUB, LANES), U32),
                            pltpu.VMEM((MOE_BLK * ROW_SUB, LANES), U32),
                            pltpu.SemaphoreType.DMA, pltpu.SemaphoreType.DMA, pltpu.SemaphoreType.DMA,
                            pltpu.SemaphoreType.DMA]),
        compiler_params=pltpu.CompilerParams(dimension_semantics=("arbitrary",), vmem_limit_bytes=VMEM_LIMIT,
                                             has_side_effects=True),
        name="moe_dispatch",
    )(gstart, seg_cnt, seg_off, pad, ls8, h2_rows)


def _expert_kernel(be_ref, nu_ref, x_ref, w1_ref, w3_ref, w2_ref, o_ref, w1b, w3b, w2b):
    j = pl.program_id(0)
    e = be_ref[j]
    prev = be_ref[jnp.maximum(j - 1, 0)]
    used = j < nu_ref[0]

    @pl.when(jnp.logical_and(used, jnp.logical_or(j == 0, e != prev)))
    def _():
        w1b[...] = w1_ref[0].astype(BF16)
        w3b[...] = w3_ref[0].astype(BF16)
        w2b[...] = w2_ref[0].astype(BF16)

    @pl.when(used)
    def _():
        x = _from_token_rows(x_ref, MOE_BLK).astype(BF16)
        a = _dot(x, w1b[...])
        hid = (a * _sigmoid(a)) * _dot(x, w3b[...])
        _to_token_rows(o_ref, _dot(hid.astype(BF16), w2b[...]))

    @pl.when(jnp.logical_not(used))
    def _():
        o_ref[...] = jnp.zeros_like(o_ref)


def _experts(block_e, n_used, xs, w1, w3, w2):
    n_blocks = block_e.shape[0]
    d, f = w1.shape[1], w1.shape[2]
    rows = pl.BlockSpec((MOE_BLK * ROW_SUB, LANES), lambda j, be, nu: (j, 0))
    rows_in = pl.BlockSpec((MOE_BLK * ROW_SUB, LANES), lambda j, be, nu: (jnp.minimum(j, nu[0] - 1), 0))
    return pl.pallas_call(
        _expert_kernel,
        out_shape=jax.ShapeDtypeStruct((n_blocks * MOE_BLK * ROW_SUB, LANES), U32),
        grid_spec=pltpu.PrefetchScalarGridSpec(
            num_scalar_prefetch=2,
            grid=(n_blocks,),
            in_specs=[rows_in,
                      pl.BlockSpec((1, d, f), lambda j, be, nu: (be[j], 0, 0)),
                      pl.BlockSpec((1, d, f), lambda j, be, nu: (be[j], 0, 0)),
                      pl.BlockSpec((1, f, d), lambda j, be, nu: (be[j], 0, 0))],
            out_specs=rows,
            scratch_shapes=[pltpu.VMEM((d, f), BF16), pltpu.VMEM((d, f), BF16), pltpu.VMEM((f, d), BF16)]),
        compiler_params=_params("arbitrary"),
        name="moe_experts",
    )(block_e, n_used, xs, w1, w3, w2)


def _combine_kernel(gs_ref, cnt_ref, off_ref, ls_ref, w8_ref, x1_ref, h2_ref, g2_ref, ws1_ref, ws3_ref, ws2_ref,
                    fg_ref, ys_hbm, o_ref, ls_smem, w_smem, gbuf0, gbuf1, acc_rows, sem0, sem1, lsem, *, tm):
    i = pl.program_id(0)
    last = pl.num_programs(0) - 1
    cp1 = pltpu.make_async_copy(ls_ref, ls_smem, lsem)
    cp2 = pltpu.make_async_copy(w8_ref, w_smem, lsem)
    cp1.start()
    cp2.start()

    def fetch(tile, gbuf, sem):
        _copy_tile_segments(tile, ys_hbm, gs_ref, gbuf, off_ref, cnt_ref, off_ref,
                            lambda used: 0, lambda used: used, sem)

    @pl.when(i == 0)
    def _():
        fetch(0, gbuf0, sem0)

    @pl.when(jnp.logical_and(i < last, i % 2 == 0))
    def _():
        fetch(i + 1, gbuf1, sem1)

    @pl.when(jnp.logical_and(i < last, i % 2 == 1))
    def _():
        fetch(i + 1, gbuf0, sem0)

    h2 = _from_token_rows(h2_ref, tm).astype(BF16)
    a = _dot(h2, ws1_ref[...])
    hid = (a * _sigmoid(a)) * _dot(h2, ws3_ref[...])
    acc = _dot(hid.astype(BF16), ws2_ref[...])
    cp1.wait()
    cp2.wait()

    def reduce_rows(gbuf, sem):
        _wait_rows(gbuf, sem, STAGE_ROWS)

        def body(tu, carry):
            for u in range(TOKEN_UNROLL):
                t = tu * TOKEN_UNROLL + u
                lo = jnp.zeros((ROW_SUB, LANES), F32)
                hi = jnp.zeros((ROW_SUB, LANES), F32)
                for k in range(TOP_K):
                    w = w_smem[t * TOP_K + k]
                    words = gbuf[pl.ds(pl.multiple_of(ls_smem[t * TOP_K + k], ROW_SUB), ROW_SUB), :]
                    lo = lo + w * lax.bitcast_convert_type(words << 16, F32)
                    hi = hi + w * lax.bitcast_convert_type(words & jnp.uint32(0xFFFF0000), F32)
                acc_rows[pl.ds(pl.multiple_of(t * SUBLANES, SUBLANES), ROW_SUB), :] = lo
                acc_rows[pl.ds(pl.multiple_of(t * SUBLANES, SUBLANES) + ROW_SUB, ROW_SUB), :] = hi
            return carry

        lax.fori_loop(0, tm // TOKEN_UNROLL, body, 0)

    pl.when(i % 2 == 0)(functools.partial(reduce_rows, gbuf0, sem0))
    pl.when(i % 2 == 1)(functools.partial(reduce_rows, gbuf1, sem1))
    routed = jnp.concatenate([acc_rows[pl.ds(s, tm, stride=SUBLANES), :] for s in range(SUBLANES)], axis=-1)
    y = x1_ref[...] + g2_ref[0] * (acc + routed)
    o_ref[...] = y * lax.rsqrt(jnp.mean(y * y, axis=-1, keepdims=True) + EPS) * fg_ref[...]


def _combine(gstart, seg_cnt, seg_off, ls8, w8, x1, h2_rows, g2, ws1, ws3, ws2, fg, ys, rows_per_batch, tm):
    n, d = x1.shape
    per = rows_per_batch // tm
    tok = pl.BlockSpec((tm * TOP_K,), lambda i, *_: (i,))
    full = lambda a: pl.BlockSpec(a.shape, lambda i, *_: (0, 0))
    return pl.pallas_call(
        functools.partial(_combine_kernel, tm=tm),
        out_shape=jax.ShapeDtypeStruct((n, d), F32),
        grid_spec=pltpu.PrefetchScalarGridSpec(
            num_scalar_prefetch=3,
            grid=(n // tm,),
            in_specs=[tok, tok, pl.BlockSpec((tm, d), lambda i, *_: (i, 0)),
                      pl.BlockSpec((tm * ROW_SUB, LANES), lambda i, *_: (i, 0)),
                      pl.BlockSpec((1, 1, d), lambda i, *_: (i // per, 0, 0)),
                      full(ws1), full(ws3), full(ws2), pl.BlockSpec((1, d), lambda i, *_: (0, 0)),
                      pl.BlockSpec(memory_space=pl.ANY)],
            out_specs=pl.BlockSpec((tm, d), lambda i, *_: (i, 0)),
            scratch_shapes=[pltpu.SMEM((tm * TOP_K,), jnp.int32), pltpu.SMEM((tm * TOP_K,), F32),
                            pltpu.VMEM((STAGE_ROWS * ROW_SUB, LANES), U32),
                            pltpu.VMEM((STAGE_ROWS * ROW_SUB, LANES), U32),
                            pltpu.VMEM((tm * SUBLANES, LANES), F32), pltpu.SemaphoreType.DMA,
                            pltpu.SemaphoreType.DMA, pltpu.SemaphoreType.DMA]),
        compiler_params=_params("arbitrary"),
        name="moe_combine_final",
    )(gstart, seg_cnt, seg_off, ls8, w8, x1, h2_rows, g2, ws1, ws3, ws2, fg.reshape(1, d), ys)


def _moe_plan(seg, counts, n_assign):
    cnt = counts.reshape(N_EXPERTS).astype(jnp.int32)
    padded = (cnt + MOE_BLK - 1) // MOE_BLK * MOE_BLK
    pends = jnp.cumsum(padded)
    pstarts = pends - padded
    max_rows = n_assign + seg.shape[0] * N_EXPERTS * (SEG_ALIGN - 1)
    n_blocks = (max_rows + N_EXPERTS * (MOE_BLK - 1) + MOE_BLK - 1) // MOE_BLK
    seg = seg[:, :, :3].astype(jnp.int32)
    gstart = pstarts[None, :] + seg[:, :, 0]
    blk_start = jnp.arange(n_blocks, dtype=jnp.int32) * MOE_BLK
    block_e = jnp.minimum(jnp.sum((blk_start[:, None] >= pends[None, :]).astype(jnp.int32), axis=1),
                          N_EXPERTS - 1).astype(jnp.int32)
    n_used = (pends[-1:] // MOE_BLK).astype(jnp.int32)
    pad = jnp.stack([pstarts + cnt, padded - cnt, jnp.broadcast_to(n_used, (N_EXPERTS,))], axis=0).astype(jnp.int32)
    return gstart, seg[:, :, 1], seg[:, :, 2], pad, block_e, n_used, n_blocks


def _mixer(x, c, ctx, c_ctx, w_ada, b_ada, norm1_g, norm2_g, w_in, s5_lam_re, s5_lam_im, s5_log_dt,
           s5_b_re, s5_b_im, s5_c_re, s5_c_im, s5_d, s5_w_glu, lb, hg_norm_g, p_a, p_b, w_out,
           moe_w_router, moe_b_router):
    b, l, d = x.shape
    lc = ctx.shape[1]
    n = b * l
    rows = l // GRID_W

    c8 = jnp.concatenate([c, c_ctx[None], jnp.zeros((8 - b - 1, d), F32)], axis=0)
    mod = _ada(c8, w_ada, b_ada)
    sh1, sc1, g1, sh2, sc2, g2 = [mod[:b, k * d:(k + 1) * d].reshape(b, 1, d) for k in range(6)]
    csh1, csc1 = mod[b:b + 1, 0:d].reshape(1, 1, d), mod[b:b + 1, d:2 * d].reshape(1, 1, d)

    w_in_b = w_in.astype(BF16)
    z = dict(zip([p[0] for p in _IN_PIECES],
                 _inproj(x.reshape(n, d), sc1, sh1, norm1_g, w_in_b, l, TOK_TILE, True)))
    zc = dict(zip([p[0] for p in _IN_PIECES],
                  _inproj(ctx.reshape(b * lc, d), csc1, csh1, norm1_g, w_in_b, lc, lc, False)))

    cx = lambda t: t.reshape(b, lc, HG_WIDTH)
    lb_row = lb.reshape(1, HG_WIDTH)
    o_f = _hgrn_pass(z["q"], z["ff"], z["i"], cx(zc["ff"]), cx(zc["i"]), lb_row, None, None, reverse=False)
    o_n = _hgrn_pass(z["q"], z["fb"], z["i"], cx(zc["fb"]), cx(zc["i"]), lb_row, o_f,
                     hg_norm_g.reshape(1, HG_DK), reverse=True)

    d_lag, w_s5_in, w_out_f, w_out_b, decay = _s5_weights(s5_lam_re, s5_lam_im, s5_log_dt, s5_b_re, s5_b_im,
                                                          s5_c_re, s5_c_im)
    kc, kl = lc // S5_T, l // S5_T
    u_lat = z["u"].reshape(b, kl, S5_T * S5_WIDTH)
    u_ctx = zc["u"].reshape(b, kc, S5_T * S5_WIDTH)
    rows_in = kl + kc
    u_ext = jnp.concatenate([u_lat, u_ctx], axis=1).reshape(b * rows_in, S5_T * S5_WIDTH)
    e = _s5_in(u_ext, d_lag, w_s5_in, (b * rows_in) // 2)
    states = _s5_scan(e, decay, b, rows_in, kl)
    d_row = s5_d.astype(F32).reshape(1, S5_WIDTH)
    y_a = _s5_out(states, w_out_f, w_out_b, e.reshape(b, rows_in, -1), z["u"], d_row, s5_w_glu.astype(BF16))

    return _merge(x.reshape(n, d), y_a, o_n, z["go"], z["ga"], z["gb"], g1, sc2, sh2, norm2_g,
                  p_a.astype(BF16), p_b.astype(BF16), w_out.astype(BF16),
                  moe_w_router.T.astype(BF16), moe_b_router.astype(F32).reshape(N_EXPERTS, 1), l, MOE_TILE) + (g2,)


def kernel(x, c, ctx, c_ctx, w_ada, b_ada, norm1_g, norm2_g, w_in, s5_lam_re, s5_lam_im, s5_log_dt, s5_b_re,
           s5_b_im, s5_c_re, s5_c_im, s5_d, s5_w_glu, hg_lb_logits, hg_norm_g, p_a, p_b, w_out, moe_w_router,
           moe_b_router, moe_w1, moe_w3, moe_w2, moe_ws1, moe_ws3, moe_ws2, final_norm_g):
    b, l, d = x.shape
    n = b * l
    assert w_ada.shape[0] == 1, "single-layer block"
    lb = jnp.cumsum(jax.nn.softmax(hg_lb_logits.astype(F32), axis=0), axis=0)[0]
    x1, h2_rows, ls8, w8, seg, counts, g2 = _mixer(
        x, c, ctx, c_ctx, w_ada[0], b_ada[0], norm1_g[0], norm2_g[0], w_in[0], s5_lam_re[0], s5_lam_im[0],
        s5_log_dt[0], s5_b_re[0], s5_b_im[0], s5_c_re[0], s5_c_im[0], s5_d[0], s5_w_glu[0], lb, hg_norm_g[0],
        p_a[0], p_b[0], w_out[0], moe_w_router[0], moe_b_router[0])
    gstart, seg_cnt, seg_off, pad, block_e, n_used, n_blocks = _moe_plan(seg, counts, n * TOP_K)
    ls_flat, w_flat = ls8.T.reshape(n * TOP_K), w8.T.reshape(n * TOP_K)
    xs = _dispatch(gstart, seg_cnt, seg_off, pad, ls_flat, h2_rows, n_blocks, MOE_TILE)
    ys = _experts(block_e, n_used, xs, moe_w1[0], moe_w3[0], moe_w2[0])
    out = _combine(gstart, seg_cnt, seg_off, ls_flat, w_flat, x1, h2_rows, g2, moe_ws1[0].astype(BF16),
                   moe_ws3[0].astype(BF16), moe_ws2[0].astype(BF16), final_norm_g, ys, l, MOE_TILE)
    return out.reshape(b, l, d)
```

```python
import functools
import math

import jax
import jax.numpy as jnp
from jax import lax
from jax.experimental import pallas as pl
from jax.experimental.pallas import tpu as pltpu

F32 = jnp.float32
BF16 = jnp.bfloat16

GRID_W = 64
S5_WIDTH = 256
S5_GROUP = 16
S5_GROUPS = 16
S5_STATE = 64
HG_HEADS = 6
HG_DK = 128
HG_WIDTH = HG_HEADS * HG_DK
N_EXPERTS = 64
ROUTE_GROUPS = 8
TOPK_GROUPS = 4
TOP_K = 8
ROUTED_SCALE = 2.5
EPS = 1e-6

LANES = 128
SUBLANES = 8

TOK_TILE = 512
S5_T = 16
HG_CHUNK = 64
HG_BATCH = 4
VMEM_LIMIT = 56 * 1024 * 1024

_NT = (((1,), (1,)), ((), ()))
_TN = (((0,), (0,)), ((), ()))


def _params(*sem):
    return pltpu.CompilerParams(dimension_semantics=sem, vmem_limit_bytes=VMEM_LIMIT)


def _dot(a, b):
    return jnp.dot(a, b, preferred_element_type=F32)


def _sigmoid(x):
    return 1.0 / (1.0 + jnp.exp(-x))


def _ada_kernel(c_ref, w_ref, b_ref, o_ref):
    c = c_ref[...]
    s = (c * _sigmoid(c)).astype(BF16)
    o_ref[...] = _dot(s, w_ref[...].astype(BF16)) + b_ref[...]


def _ada(c8, w_ada, b_ada):
    d, n = w_ada.shape
    tn = 1536
    return pl.pallas_call(
        _ada_kernel,
        out_shape=jax.ShapeDtypeStruct((8, n), F32),
        grid=(n // tn,),
        in_specs=[pl.BlockSpec((8, d), lambda j: (0, 0)),
                  pl.BlockSpec((d, tn), lambda j: (0, j)),
                  pl.BlockSpec((1, tn), lambda j: (0, j))],
        out_specs=pl.BlockSpec((8, tn), lambda j: (0, j)),
        compiler_params=_params("arbitrary"),
        name="ada_mod",
    )(c8, w_ada, b_ada.reshape(1, n))


_IN_PIECES = (("u", 0, 256, BF16), ("q", 256, 768, BF16), ("ff", 1024, 768, BF16),
              ("fb", 1792, 768, BF16), ("i", 2560, 768, BF16), ("go", 3328, 768, BF16),
              ("ga", 4096, 1024, BF16), ("gb", 5120, 1024, BF16))


def _fold_rows(val, buf_a, buf_b):
    t = val.shape[0]
    buf_a[...] = val[:, :LANES]
    buf_b[...] = val[:, LANES:]
    pieces = []
    for s in range(S5_T):
        pieces += [buf_a[pl.ds(s, t // S5_T, stride=S5_T), :], buf_b[pl.ds(s, t // S5_T, stride=S5_T), :]]
    return jnp.concatenate(pieces, axis=-1)


def _unfold_rows(val, buf_a, buf_b):
    r = val.shape[0]
    for s in range(S5_T):
        buf_a[pl.ds(s, r, stride=S5_T), :] = val[:, s * S5_WIDTH:s * S5_WIDTH + LANES]
        buf_b[pl.ds(s, r, stride=S5_T), :] = val[:, s * S5_WIDTH + LANES:(s + 1) * S5_WIDTH]
    return jnp.concatenate([buf_a[...], buf_b[...]], axis=-1)


def _grid_transpose_matrix(tm):
    i = jnp.arange(tm)
    src = (i % (tm // GRID_W)) * GRID_W + i // (tm // GRID_W)
    return (src[:, None] == jnp.arange(tm)[None, :]).astype(BF16)


def _inproj_kernel(x_ref, sc_ref, sh_ref, g_ref, w_ref, p_ref, *o_refs):
    o_refs, (fold_a, fold_b) = o_refs[:len(_IN_PIECES)], o_refs[len(_IN_PIECES):]
    x = x_ref[...]
    y = x * lax.rsqrt(jnp.mean(x * x, axis=-1, keepdims=True) + EPS) * g_ref[...]
    h = (y * (1.0 + sc_ref[0]) + sh_ref[0]).astype(BF16)
    h_cm = None
    for (name, a, wd, _), o_ref in zip(_IN_PIECES, o_refs):
        if name == "u":
            o_ref[...] = _fold_rows(_dot(h, w_ref[:, a:a + wd]), fold_a, fold_b).astype(o_ref.dtype)
        elif len(o_ref.shape) == 2:
            o_ref[...] = _dot(h, w_ref[:, a:a + wd]).astype(o_ref.dtype)
        else:
            if h_cm is None:
                h_cm = _dot(p_ref[...], h).astype(BF16)
            o_ref[0] = _dot(h_cm, w_ref[:, a:a + wd]).astype(o_ref.dtype).reshape(o_ref.shape[1:])


_COLMAJOR_PIECES = ("q", "ff", "fb", "i")


def _inproj(x2d, sc, sh, g, w_bf16, rows_per_mod, tm, colmajor):
    n, d = x2d.shape
    per = rows_per_mod // tm
    mod_map = (lambda i: (i // per, 0, 0)) if sc.shape[0] > 1 else (lambda i: (0, 0, 0))
    shapes, specs = [], []
    for name, _, wd, dt in _IN_PIECES:
        if colmajor and name in _COLMAJOR_PIECES:
            shapes.append(jax.ShapeDtypeStruct((n // rows_per_mod, GRID_W, rows_per_mod // GRID_W, wd), dt))
            specs.append(pl.BlockSpec((1, GRID_W, tm // GRID_W, wd), lambda i: (i // per, 0, i % per, 0)))
        elif name == "u":
            shapes.append(jax.ShapeDtypeStruct((n // S5_T, S5_T * wd), dt))
            specs.append(pl.BlockSpec((tm // S5_T, S5_T * wd), lambda i: (i, 0)))
        else:
            shapes.append(jax.ShapeDtypeStruct((n, wd), dt))
            specs.append(pl.BlockSpec((tm, wd), lambda i: (i, 0)))
    return pl.pallas_call(
        _inproj_kernel,
        out_shape=shapes,
        grid=(n // tm,),
        in_specs=[pl.BlockSpec((tm, d), lambda i: (i, 0)),
                  pl.BlockSpec((1, 1, d), mod_map),
                  pl.BlockSpec((1, 1, d), mod_map),
                  pl.BlockSpec((1, d), lambda i: (0, 0)),
                  pl.BlockSpec(w_bf16.shape, lambda i: (0, 0)),
                  pl.BlockSpec((tm, tm), lambda i: (0, 0))],
        out_specs=specs,
        scratch_shapes=[pltpu.VMEM((tm, LANES), F32), pltpu.VMEM((tm, LANES), F32)],
        compiler_params=_params("arbitrary"),
        name="in_proj",
    )(x2d, sc, sh, g.reshape(1, d), w_bf16, _grid_transpose_matrix(tm))


def _hgrn_gates(zf, lb):
    sig = _sigmoid(zf)
    logf = jnp.log(lb + (1.0 - lb) * sig)
    k = (1.0 - lb) * (1.0 - sig)
    return logf, k


def _chunk_cumsum(cs, logf):
    hi = logf.astype(BF16)
    lo = (logf - hi.astype(F32)).astype(BF16)
    return _dot(cs, hi) + _dot(cs, lo)


def _hgrn_state_step(zf, v, lb, st, cs, reverse):
    logf, k = _hgrn_gates(zf, lb)
    cum = _chunk_cumsum(cs, logf)
    t = 0 if reverse else HG_CHUNK - 1
    total = cum[t:t + 1, :]
    kdec = (k * jnp.exp(total - cum)).astype(BF16)
    st_new = st * jnp.exp(total) + lax.dot_general(v.astype(BF16), kdec, _TN, preferred_element_type=F32)
    return cum, k, st_new


def _hgrn_kernel(*refs, reverse, final, n_ctx_chunks):
    if final:
        q_all, f_all, v_all, cf_ref, cv_ref, lb_ref, of_all, g_ref, o_all, st_ref = refs
    else:
        q_all, f_all, v_all, cf_ref, cv_ref, lb_ref, o_all, st_ref = refs
        of_all = None
    n_batch = q_all.shape[0]
    c_len = HG_CHUNK
    n_rows = q_all.shape[2]
    n_chunks = n_rows // c_len
    row = lax.broadcasted_iota(jnp.int32, (n_rows, n_rows), 0)
    col = lax.broadcasted_iota(jnp.int32, (n_rows, n_rows), 1)
    tri = (col >= row) if reverse else (col <= row)
    same_chunk = None
    for c in range(n_chunks):
        lo, hi = c * c_len, (c + 1) * c_len
        blk = (row >= lo) & (row < hi) & (col >= lo) & (col < hi)
        same_chunk = blk if same_chunk is None else (same_chunk | blk)
    mask = tri & same_chunk
    cs = jnp.where(mask, 1.0, 0.0).astype(BF16)

    @pl.when(pl.program_id(1) == 0)
    def _():
        cs1 = cs[:c_len, :c_len]
        order = range(n_ctx_chunks - 1, -1, -1) if reverse else range(n_ctx_chunks)
        for bi in range(n_batch):
            for h in range(HG_HEADS):
                cols = slice(h * HG_DK, (h + 1) * HG_DK)
                st = jnp.zeros((HG_DK, HG_DK), F32)
                for c in order:
                    rows = slice(c * c_len, (c + 1) * c_len)
                    _, _, st = _hgrn_state_step(cf_ref[bi, rows, cols].astype(F32), cv_ref[bi, rows, cols].astype(F32),
                                                lb_ref[:, cols], st, cs1, reverse)
                st_ref[bi * HG_HEADS + h] = st

    def chunk_rows(x, r):
        return [x[c * c_len + r:c * c_len + r + 1, :] for c in range(n_chunks)]

    def over_chunks(rows):
        return jnp.concatenate([jnp.broadcast_to(r, (c_len, r.shape[1])) for r in rows], axis=0)

    lb = lb_ref[...]
    r_ref = c_len // 2 - 1 if reverse else c_len // 2
    r_tot = 0 if reverse else c_len - 1
    order = range(n_chunks - 1, -1, -1) if reverse else range(n_chunks)
    for bi in range(n_batch):
        q = q_all[bi, 0].astype(F32)
        v = v_all[bi, 0]
        logf, k = _hgrn_gates(f_all[bi, 0].astype(F32), lb)
        cum = _chunk_cumsum(cs, logf)
        ref_rows, tot_rows = chunk_rows(cum, r_ref), chunk_rows(cum, r_tot)
        ref = over_chunks(ref_rows)
        qe = q * jnp.exp(cum - ref)
        ke = k * jnp.exp(ref - cum)
        qi, ki = qe.astype(BF16), ke.astype(BF16)
        q_in = (qe * over_chunks([jnp.exp(r) for r in ref_rows])).astype(BF16)
        kdec = (ke * over_chunks([jnp.exp(t - r) for t, r in zip(tot_rows, ref_rows)])).astype(BF16)
        for h in range(HG_HEADS):
            cols = slice(h * HG_DK, (h + 1) * HG_DK)
            s = lax.dot_general(qi[:, cols], ki[:, cols], _NT, preferred_element_type=F32)
            o_intra = _dot(jnp.where(mask, s, 0.0).astype(BF16), v[:, cols])
            st = st_ref[bi * HG_HEADS + h]
            for c in order:
                rows = slice(c * c_len, (c + 1) * c_len)
                o = o_intra[rows] + lax.dot_general(q_in[rows, cols], st.astype(BF16), _NT,
                                                    preferred_element_type=F32)
                total = cum[c * c_len + r_tot:c * c_len + r_tot + 1, cols]
                st = st * jnp.exp(total) + lax.dot_general(v[rows, cols], kdec[rows, cols], _TN,
                                                           preferred_element_type=F32)
                if final:
                    o = o + of_all[bi, 0, rows, cols].astype(F32)
                    o = o * lax.rsqrt(jnp.mean(o * o, axis=-1, keepdims=True) + EPS) * g_ref[...]
                o_all[bi, 0, rows, cols] = o.astype(o_all.dtype)
            st_ref[bi * HG_HEADS + h] = st


def _hgrn_pass(q, f, v, cf, cv, lb, o_prev, g, *, reverse):
    b, nw, rows, _ = q.shape
    nb = HG_BATCH if b % HG_BATCH == 0 else 1
    final = o_prev is not None
    wmap = (lambda bi, w: (bi, nw - 1 - w, 0, 0)) if reverse else (lambda bi, w: (bi, w, 0, 0))
    blk = pl.BlockSpec((nb, 1, rows, HG_WIDTH), wmap)
    cblk = pl.BlockSpec((nb, cf.shape[1], HG_WIDTH), lambda bi, w: (bi, 0, 0))
    in_specs = [blk, blk, blk, cblk, cblk, pl.BlockSpec((1, HG_WIDTH), lambda bi, w: (0, 0))]
    args = [q, f, v, cf, cv, lb]
    if final:
        in_specs += [blk, pl.BlockSpec((1, HG_DK), lambda bi, w: (0, 0))]
        args += [o_prev, g]
    return pl.pallas_call(
        functools.partial(_hgrn_kernel, reverse=reverse, final=final, n_ctx_chunks=cf.shape[1] // HG_CHUNK),
        out_shape=jax.ShapeDtypeStruct(q.shape, BF16),
        grid=(b // nb, nw),
        in_specs=in_specs,
        out_specs=blk,
        scratch_shapes=[pltpu.VMEM((nb * HG_HEADS, HG_DK, HG_DK), F32)],
        compiler_params=_params("arbitrary", "arbitrary"),
        name="hgrn_bwd" if reverse else "hgrn_fwd",
    )(*args)


def _s5_weights(lam_re, lam_im, log_dt, b_re, b_im, c_re, c_im):
    hp = lax.Precision.HIGHEST
    g, p, cc, t = S5_GROUPS, S5_STATE, S5_GROUP, S5_T
    lre = jnp.minimum(lam_re.astype(F32), -1e-4)
    lim = lam_im.astype(F32)
    dt = jnp.exp(log_dt.astype(F32))[..., None]
    ks = jnp.arange(t + 1, dtype=F32)[:, None, None, None]
    mag = jnp.exp(ks * (lre * dt)[None])
    pw_re = mag * jnp.cos(ks * (lim * dt)[None])
    pw_im = mag * jnp.sin(ks * (lim * dt)[None])
    nr, ni = pw_re[1] - 1.0, pw_im[1]
    den = lre * lre + lim * lim
    cf_re = (nr * lre + ni * lim) / den
    cf_im = (ni * lre - nr * lim) / den
    bb_re = cf_re[..., None] * b_re - cf_im[..., None] * b_im
    bb_im = cf_re[..., None] * b_im + cf_im[..., None] * b_re
    cre, cim = c_re.astype(F32), c_im.astype(F32)
    sw, ns = S5_WIDTH, 2 * g * p
    grp_of_row = jnp.arange(sw)[:, None] // cc

    cp_re = cre[None, None, :, :, :] * pw_re[:t, :, :, None, :] - cim[None, None] * pw_im[:t, :, :, None, :]
    cp_im = cre[None, None, :, :, :] * pw_im[:t, :, :, None, :] + cim[None, None] * pw_re[:t, :, :, None, :]
    def contract_p(cp, bb):
        return jnp.sum(cp.transpose(4, 1, 0, 2, 3)[..., None] * bb.transpose(2, 0, 1, 3)[:, :, None, :, None, :],
                       axis=0)

    kk = contract_p(cp_re, bb_re) - contract_p(cp_im, bb_im)
    kf, kb = kk[0], kk[1]
    kall = jnp.concatenate([kb[:0:-1], (kf[0] + kb[0])[None], kf[1:]], axis=0)
    kt = kall.transpose(0, 1, 3, 2).reshape(2 * t - 1, sw, cc)

    def spread(x, period, reps):
        sel = (jnp.arange(period)[:, None] == (jnp.arange(period * reps)[None, :] % period)).astype(BF16)
        return jnp.dot(x.astype(BF16), sel, preferred_element_type=BF16)

    same = grp_of_row == (jnp.arange(sw)[None, :] // cc)
    d_lag = jnp.where(same[None], spread(kt, cc, g), 0)

    same_in = jnp.tile(grp_of_row, (t, 1)) == ((jnp.arange(ns)[None, :] % (g * p)) // p)

    def in_to_state(pre, pim, bre, bim):
        xre = pre[..., None] * bre[None] - pim[..., None] * bim[None]
        xim = pre[..., None] * bim[None] + pim[..., None] * bre[None]
        return [spread(xre.transpose(0, 1, 3, 2).reshape(t * sw, p), p, g),
                spread(xim.transpose(0, 1, 3, 2).reshape(t * sw, p), p, g)]

    w_in = jnp.concatenate(in_to_state(pw_re[t - 1::-1, 0], pw_im[t - 1::-1, 0], bb_re[0], bb_im[0])
                           + in_to_state(pw_re[:t, 1], pw_im[:t, 1], bb_re[1], bb_im[1]), axis=1)
    w_in = jnp.where(jnp.tile(same_in, (1, 2)), w_in, 0)

    same_out = ((jnp.arange(ns)[:, None] % (g * p)) // p) == ((jnp.arange(t * sw)[None, :] // cc) % g)
    col = jnp.arange(t * sw)
    pick = (jnp.arange(t * cc)[:, None] == ((col // sw) * cc + col % cc)[None, :]).astype(BF16)

    def state_to_out(pre, pim):
        are = cre[None] * pre[:, :, None, :] - cim[None] * pim[:, :, None, :]
        aim = cre[None] * pim[:, :, None, :] + cim[None] * pre[:, :, None, :]
        a = jnp.concatenate([are.transpose(1, 3, 0, 2), -aim.transpose(1, 3, 0, 2)], axis=0)
        a = jnp.dot(a.reshape(ns, t * cc).astype(BF16), pick, preferred_element_type=BF16)
        return jnp.where(same_out, a, 0)

    w_out_f = state_to_out(pw_re[1:, 0], pw_im[1:, 0])
    w_out_b = state_to_out(pw_re[t:0:-1, 1], pw_im[t:0:-1, 1])

    decay = jnp.stack([pw_re[t].reshape(2, g * p), pw_im[t].reshape(2, g * p)], axis=1)
    return d_lag, w_in, w_out_f, w_out_b, decay


def _s5_in_kernel(u_ref, d_ref, w_ref, o_ref):
    j = pl.program_id(1)

    @pl.when(j < S5_T)
    def _():
        acc = _dot(u_ref[:, 0:S5_WIDTH], d_ref[j + S5_T - 1])
        for s in range(1, S5_T):
            acc = acc + _dot(u_ref[:, s * S5_WIDTH:(s + 1) * S5_WIDTH], d_ref[j - s + S5_T - 1])
        o_ref[...] = acc

    @pl.when(j >= S5_T)
    def _():
        o_ref[...] = _dot(u_ref[...], w_ref[...])


def _s5_in(u, d_lag, w_in, tm):
    m, k = u.shape
    tn = S5_WIDTH
    nj = (k + w_in.shape[1]) // tn
    return pl.pallas_call(
        _s5_in_kernel,
        out_shape=jax.ShapeDtypeStruct((m, nj * tn), F32),
        grid=(m // tm, nj),
        in_specs=[pl.BlockSpec((tm, k), lambda i, j: (i, 0)),
                  pl.BlockSpec(d_lag.shape, lambda i, j: (0, 0, 0)),
                  pl.BlockSpec((k, tn), lambda i, j: (0, jnp.maximum(j - S5_T, 0)))],
        out_specs=pl.BlockSpec((tm, tn), lambda i, j: (i, j)),
        compiler_params=_params("arbitrary", "arbitrary"),
        name="s5_in",
    )(u, d_lag, w_in)


def _s5_scan_kernel(efr_ref, efi_ref, ebr_ref, ebi_ref, a_ref, hfr_ref, hfi_ref, hbr_ref, hbi_ref,
                    *, nb, rows_in, rows_out):
    dirs = ((efr_ref, efi_ref, hfr_ref, hfi_ref, a_ref[0, 0:1, :], a_ref[0, 1:2, :]),
            (ebr_ref, ebi_ref, hbr_ref, hbi_ref, a_ref[1, 0:1, :], a_ref[1, 1:2, :]))
    zero = jnp.zeros_like(dirs[0][4])

    def step(srcs, carry, store):
        new = []
        for di, (er_ref, ei_ref, hr_ref, hi_ref, are, aim) in enumerate(dirs):
            for bi in range(nb):
                hre, him = carry[2 * (di * nb + bi)], carry[2 * (di * nb + bi) + 1]
                if store:
                    hr_ref[pl.ds(bi * rows_out + srcs[di], 1), :] = hre
                    hi_ref[pl.ds(bi * rows_out + srcs[di], 1), :] = him
                ere = er_ref[pl.ds(bi * rows_in + srcs[di], 1), :]
                eim = ei_ref[pl.ds(bi * rows_in + srcs[di], 1), :]
                new += [are * hre - aim * him + ere, are * him + aim * hre + eim]
        return tuple(new)

    n_ctx = rows_in - rows_out
    carry = lax.fori_loop(0, n_ctx, lambda s, c: step((rows_out + s, rows_in - 1 - s), c, False),
                          tuple([zero] * (4 * nb)))
    lax.fori_loop(0, rows_out, lambda s, c: step((s, rows_out - 1 - s), c, True), carry)


def _s5_scan(e, decay, nb, rows_in, rows_out):
    tc = 256
    nsr = S5_GROUPS * S5_STATE
    c0 = (S5_T * S5_WIDTH) // tc
    nt = nsr // tc
    eblk = lambda k: pl.BlockSpec((nb * rows_in, tc), lambda j: (0, c0 + k * nt + j))
    hblk = pl.BlockSpec((nb * rows_out, tc), lambda j: (0, j))
    return pl.pallas_call(
        functools.partial(_s5_scan_kernel, nb=nb, rows_in=rows_in, rows_out=rows_out),
        out_shape=[jax.ShapeDtypeStruct((nb * rows_out, nsr), F32)] * 4,
        grid=(nt,),
        in_specs=[eblk(0), eblk(1), eblk(2), eblk(3), pl.BlockSpec((2, 2, tc), lambda j: (0, 0, j))],
        out_specs=[hblk] * 4,
        compiler_params=_params("arbitrary"),
        name="s5_scan",
    )(e, e, e, e, decay)


def _gelu_tanh(x):
    return 0.5 * x * (1.0 + jnp.tanh(math.sqrt(2.0 / math.pi) * (x + 0.044715 * x * x * x)))


def _s5_out_kernel(hfr_ref, hfi_ref, hbr_ref, hbi_ref, wf_ref, wb_ref, yi_ref, u_ref, d_ref, wg_ref, o_ref):
    nsr = hfr_ref.shape[1]
    y = yi_ref[0] + d_ref[...] * u_ref[...].astype(F32)
    for h_ref, w_ref, r0 in ((hfr_ref, wf_ref, 0), (hfi_ref, wf_ref, nsr), (hbr_ref, wb_ref, 0), (hbi_ref, wb_ref, nsr)):
        y = y + _dot(h_ref[...].astype(BF16), w_ref[r0:r0 + nsr, :])
    y = _gelu_tanh(y)
    gate = _sigmoid(_dot(y.astype(BF16), wg_ref[...]))
    o_ref[...] = (y * gate).astype(o_ref.dtype)


def _s5_out(states, w_out_f, w_out_b, e3, u_rows, d_row, w_glu):
    m, nsr = states[0].shape
    nb = e3.shape[0]
    tm = m // nb
    tn = S5_WIDTH
    st = pl.BlockSpec((tm, nsr), lambda i, j: (i, 0))
    wo = pl.BlockSpec((2 * nsr, tn), lambda i, j: (0, j))
    return pl.pallas_call(
        _s5_out_kernel,
        out_shape=jax.ShapeDtypeStruct((m, S5_T * S5_WIDTH), BF16),
        grid=(nb, S5_T),
        in_specs=[st, st, st, st, wo, wo,
                  pl.BlockSpec((1, tm, tn), lambda i, j: (i, 0, j)),
                  pl.BlockSpec((tm, tn), lambda i, j: (i, j)),
                  pl.BlockSpec((1, tn), lambda i, j: (0, 0)),
                  pl.BlockSpec((tn, tn), lambda i, j: (0, 0))],
        out_specs=pl.BlockSpec((tm, tn), lambda i, j: (i, j)),
        compiler_params=_params("arbitrary", "arbitrary"),
        name="s5_out",
    )(*states, w_out_f, w_out_b, e3, u_rows, d_row, w_glu)


U32 = jnp.uint32
ROW_SUB = 4


def _to_token_rows(ref, val):
    t, d = val.shape
    bits = lax.bitcast_convert_type(val.astype(BF16).astype(F32), U32)
    w = (bits[:, :d // 2] >> 16) | bits[:, d // 2:]
    for s in range(ROW_SUB):
        ref[pl.ds(s, t, stride=ROW_SUB), :] = w[:, s * LANES:(s + 1) * LANES]


def _from_token_rows(ref, t, row0=0):
    w = jnp.concatenate([ref[pl.ds(row0 * ROW_SUB + s, t, stride=ROW_SUB), :] for s in range(ROW_SUB)], axis=-1)
    lo = lax.bitcast_convert_type(w << 16, F32)
    hi = lax.bitcast_convert_type(w & jnp.uint32(0xFFFF0000), F32)
    return jnp.concatenate([lo, hi], axis=-1)


def _route(h2b, wr_ref, br_ref, cnt_ref, ls8_ref, w8_ref, seg_ref):
    tm = h2b.shape[0]
    per_group = N_EXPERTS // ROUTE_GROUPS
    scores = _sigmoid(lax.dot_general(wr_ref[...], h2b, _NT, preferred_element_type=F32))
    biased = scores + br_ref[...]
    neg = -jnp.inf
    sub = lax.broadcasted_iota(jnp.int32, (per_group, tm), 0)
    grp = []
    for gi in range(ROUTE_GROUPS):
        v = biased[gi * per_group:(gi + 1) * per_group, :]
        m1 = jnp.max(v, axis=0, keepdims=True)
        first = jnp.min(jnp.where(v == m1, sub, per_group), axis=0, keepdims=True)
        m2 = jnp.max(jnp.where(sub == first, neg, v), axis=0, keepdims=True)
        grp.append(m1 + m2)
    grp = jnp.concatenate(grp, axis=0)
    gid = lax.broadcasted_iota(jnp.int32, (ROUTE_GROUPS, tm), 0)
    beaten = jnp.zeros((ROUTE_GROUPS, tm), jnp.int32)
    for gj in range(ROUTE_GROUPS):
        r = grp[gj:gj + 1, :]
        beaten = beaten + jnp.where((r > grp) | ((r == grp) & (gj < gid)), 1, 0)
    group_ok = beaten < TOPK_GROUPS
    expert_ok = jnp.concatenate(
        [jnp.broadcast_to(group_ok[gi:gi + 1, :], (per_group, tm)) for gi in range(ROUTE_GROUPS)], axis=0)
    cur = jnp.where(expert_ok, biased, neg)
    eid = lax.broadcasted_iota(jnp.int32, (N_EXPERTS, tm), 0)
    sel = jnp.zeros((N_EXPERTS, tm), F32)
    picks, wts = [], []
    for _ in range(TOP_K):
        m = jnp.max(cur, axis=0, keepdims=True)
        idx = jnp.min(jnp.where(cur == m, eid, N_EXPERTS), axis=0, keepdims=True)
        hit = eid == idx
        picks.append(idx)
        wts.append(jnp.sum(jnp.where(hit, scores, 0.0), axis=0, keepdims=True))
        sel = jnp.where(hit, 1.0, sel)
        cur = jnp.where(hit, neg, cur)
    wsum = wts[0]
    for w in wts[1:]:
        wsum = wsum + w
    selb = sel.astype(BF16)
    ti = lax.broadcasted_iota(jnp.int32, (tm, tm), 0)
    tj = lax.broadcasted_iota(jnp.int32, (tm, tm), 1)
    rank = _dot(selb, jnp.where(ti < tj, 1.0, 0.0).astype(BF16))
    seg_units = jnp.ceil(jnp.sum(sel, axis=1, keepdims=True) * (1.0 / SEG_ALIGN))
    ei = lax.broadcasted_iota(jnp.int32, (N_EXPERTS, N_EXPERTS), 0)
    ej = lax.broadcasted_iota(jnp.int32, (N_EXPERTS, N_EXPERTS), 1)
    units_row = jnp.broadcast_to(seg_units, (N_EXPERTS, LANES)).astype(BF16)
    seg_off = _dot(jnp.where(ej < ei, 1.0, 0.0).astype(BF16), units_row)[:, 0:1] * SEG_ALIGN
    seg_rows = seg_units * SEG_ALIGN
    slot = seg_off + rank
    for k in range(TOP_K):
        w8_ref[k:k + 1, :] = wts[k] / wsum * ROUTED_SCALE
        ls8_ref[k:k + 1, :] = (jnp.sum(jnp.where(eid == picks[k], slot, 0.0), axis=0, keepdims=True)
                               * ROW_SUB).astype(jnp.int32)
    lane = lax.broadcasted_iota(jnp.int32, (N_EXPERTS, LANES), 1)
    seg_ref[0] = jnp.where(lane == 0, cnt_ref[...], jnp.where(lane == 1, seg_rows, seg_off))
    cnt_ref[...] = cnt_ref[...] + seg_rows


def _merge_kernel(x_ref, ya_ref, on_ref, go_ref, ga_ref, gb_ref, g1_ref, sc_ref, sh_ref, n2_ref,
                  pa_ref, pb_ref, wo_ref, wr_ref, br_ref, pt_ref,
                  x1_ref, h2_ref, ls8_ref, w8_ref, seg_ref, cnt_ref, fold_a, fold_b):
    @pl.when(pl.program_id(0) == 0)
    def _():
        cnt_ref[...] = jnp.zeros_like(cnt_ref)

    go = go_ref[...].astype(F32)
    on = _dot(pt_ref[...], on_ref[0].reshape(x_ref.shape[0], HG_WIDTH))
    y_b = (on * (go * _sigmoid(go))).astype(BF16)
    y_a = _unfold_rows(ya_ref[...].astype(F32), fold_a, fold_b).astype(BF16)
    pa = _dot(y_a, pa_ref[...])
    pb = _dot(y_b, pb_ref[...])
    merged = _sigmoid(ga_ref[...].astype(F32)) * pa + _sigmoid(gb_ref[...].astype(F32)) * pb
    x1 = x_ref[...] + g1_ref[0] * _dot(merged.astype(BF16), wo_ref[...])
    x1_ref[...] = x1
    y = x1 * lax.rsqrt(jnp.mean(x1 * x1, axis=-1, keepdims=True) + EPS) * n2_ref[...]
    h2 = y * (1.0 + sc_ref[0]) + sh_ref[0]
    _to_token_rows(h2_ref, h2)
    _route(h2.astype(BF16), wr_ref, br_ref, cnt_ref, ls8_ref, w8_ref, seg_ref)


def _merge(x2d, ya, on, go, ga, gb, g1, sc2, sh2, n2g, pa, pb, wo, wr_t, br, rows_per_batch, tm):
    n, d = x2d.shape
    per = rows_per_batch // tm
    row = lambda wd: pl.BlockSpec((tm, wd), lambda i: (i, 0))
    mod = pl.BlockSpec((1, 1, d), lambda i: (i // per, 0, 0))
    full = lambda a: pl.BlockSpec(a.shape, lambda i: (0, 0))
    tok = pl.BlockSpec((TOP_K, tm), lambda i: (0, i))
    return pl.pallas_call(
        _merge_kernel,
        out_shape=[jax.ShapeDtypeStruct((n, d), F32), jax.ShapeDtypeStruct((n * ROW_SUB, LANES), U32),
                   jax.ShapeDtypeStruct((TOP_K, n), jnp.int32), jax.ShapeDtypeStruct((TOP_K, n), F32),
                   jax.ShapeDtypeStruct((n // tm, N_EXPERTS, LANES), F32),
                   jax.ShapeDtypeStruct((N_EXPERTS, 1), F32)],
        grid=(n // tm,),
        in_specs=[row(d), pl.BlockSpec((tm // S5_T, S5_T * S5_WIDTH), lambda i: (i, 0)),
                  pl.BlockSpec((1, GRID_W, tm // GRID_W, HG_WIDTH), lambda i: (i // per, 0, i % per, 0)),
                  row(HG_WIDTH), row(d), row(d), mod, mod, mod,
                  pl.BlockSpec((1, d), lambda i: (0, 0)), full(pa), full(pb), full(wo), full(wr_t), full(br),
                  pl.BlockSpec((tm, tm), lambda i: (0, 0))],
        out_specs=[row(d), pl.BlockSpec((tm * ROW_SUB, LANES), lambda i: (i, 0)), tok, tok,
                   pl.BlockSpec((1, N_EXPERTS, LANES), lambda i: (i, 0, 0)),
                   pl.BlockSpec((N_EXPERTS, 1), lambda i: (0, 0))],
        scratch_shapes=[pltpu.VMEM((tm, LANES), F32), pltpu.VMEM((tm, LANES), F32)],
        compiler_params=_params("arbitrary"),
        name="merge_out_proj_route",
    )(x2d, ya, on, go, ga, gb, g1, sc2, sh2, n2g.reshape(1, d), pa, pb, wo, wr_t, br,
      _grid_transpose_matrix(tm).T)


MOE_TILE = TOK_TILE
TOKEN_UNROLL = 4
SEG_ALIGN = 8
FILL_ROWS = 512
STAGE_ROWS = MOE_TILE * TOP_K + FILL_ROWS
MOE_BLK = 1024


def _token_row(ref, r):
    return ref.at[pl.ds(pl.multiple_of(r * ROW_SUB, ROW_SUB), ROW_SUB)]


def _wait_rows(any_ref, sem, n_rows):
    view = any_ref.at[pl.ds(0, n_rows * ROW_SUB)]
    pltpu.make_async_copy(view, view, sem).wait()


def _rows(ref, r0, n):
    return ref.at[pl.ds(pl.multiple_of(r0 * ROW_SUB, ROW_SUB), n * ROW_SUB)]


def _pow2_pieces(n, max_piece, fn, min_piece=1):
    done = 0
    piece = max_piece
    while piece >= min_piece:
        hit = (n & piece) != 0
        pl.when(hit)(functools.partial(fn, done, piece))
        done = done + (n & piece)
        piece //= 2


def _copy_rows(src_ref, src0, dst_ref, dst0, n, max_piece, sem, min_piece=1):
    def piece(off, size):
        pltpu.make_async_copy(_rows(src_ref, src0 + off, size), _rows(dst_ref, dst0 + off, size), sem).start()
    _pow2_pieces(n, max_piece, piece, min_piece)


def _wait_copied_rows(src_ref, dst_ref, n, max_piece, sem):
    def piece(off, size):
        pltpu.make_async_copy(_rows(src_ref, 0, size), _rows(dst_ref, 0, size), sem).wait()
    _pow2_pieces(n, max_piece, piece)


def _copy_tile_segments(i, src_ref, src_tab, dst_ref, dst_tab, cnt_ref, off_ref, fill_src0, fill_dst0, sem):
    def per_expert(e, carry):
        _copy_rows(src_ref, src_tab[i, e], dst_ref, dst_tab[i, e], cnt_ref[i, e], MOE_TILE, sem, SEG_ALIGN)
        return carry

    lax.fori_loop(0, N_EXPERTS, per_expert, 0)
    used = off_ref[i, N_EXPERTS - 1] + cnt_ref[i, N_EXPERTS - 1]
    _copy_rows(src_ref, fill_src0(used), dst_ref, fill_dst0(used), STAGE_ROWS - used, FILL_ROWS, sem, SEG_ALIGN)


def _dispatch_kernel(gs_ref, cnt_ref, off_ref, pad_ref, ls_ref, h2_ref, xs_hbm, ls_smem, stage0, stage1, zbuf,
                     sem0, sem1, lsem, zsem, *, tm, n_blocks):
    i = pl.program_id(0)
    last = pl.num_programs(0) - 1
    cp = pltpu.make_async_copy(ls_ref, ls_smem, lsem)
    cp.start()
    trash0 = n_blocks * MOE_BLK

    @pl.when(i == 0)
    def _():
        stage0[...] = jnp.zeros_like(stage0)
        stage1[...] = jnp.zeros_like(stage1)
        zbuf[...] = jnp.zeros_like(zbuf)
        cpz = pltpu.make_async_copy(zbuf, _rows(xs_hbm, trash0, 2 * FILL_ROWS), zsem)
        cpz.start()
        cpz.wait()

    cp.wait()

    def tile(stage, sem, prev_sem, trash):
        def body(tu, carry):
            for u in range(TOKEN_UNROLL):
                t = tu * TOKEN_UNROLL + u
                row = h2_ref[pl.ds(pl.multiple_of(t * ROW_SUB, ROW_SUB), ROW_SUB), :]
                for k in range(TOP_K):
                    stage[pl.ds(pl.multiple_of(ls_smem[t * TOP_K + k], ROW_SUB), ROW_SUB), :] = row
            return carry

        lax.fori_loop(0, tm // TOKEN_UNROLL, body, 0)
        _copy_tile_segments(i, stage, off_ref, xs_hbm, gs_ref, cnt_ref, off_ref,
                            lambda used: used, lambda used: trash, sem)

        @pl.when(i > 0)
        def _():
            _wait_rows(xs_hbm, prev_sem, STAGE_ROWS)

        @pl.when(i == last)
        def _():
            _wait_rows(xs_hbm, sem, STAGE_ROWS)

    pl.when(i % 2 == 0)(functools.partial(tile, stage0, sem0, sem1, trash0))
    pl.when(i % 2 == 1)(functools.partial(tile, stage1, sem1, sem0, trash0 + FILL_ROWS))

    def zero_pad(e, carry):
        _copy_rows(zbuf, 0, xs_hbm, pad_ref[0, e], pad_ref[1, e], MOE_BLK // 2, zsem)
        return carry

    def wait_pad(e, carry):
        _wait_copied_rows(zbuf, xs_hbm, pad_ref[1, e], MOE_BLK // 2, zsem)
        return carry

    def zero_block(j, carry):
        pltpu.make_async_copy(zbuf, _rows(xs_hbm, j * MOE_BLK, MOE_BLK), zsem).start()
        return carry

    def wait_block(j, carry):
        pltpu.make_async_copy(zbuf, _rows(xs_hbm, 0, MOE_BLK), zsem).wait()
        return carry

    @pl.when(i == 0)
    def _():
        lax.fori_loop(0, N_EXPERTS, zero_pad, 0)
        lax.fori_loop(pad_ref[2, 0], n_blocks, zero_block, 0)

    @pl.when(i == last)
    def _():
        lax.fori_loop(0, N_EXPERTS, wait_pad, 0)
        lax.fori_loop(pad_ref[2, 0], n_blocks, wait_block, 0)


def _dispatch(gstart, seg_cnt, seg_off, pad, ls8, h2_rows, n_blocks, tm):
    n = ls8.shape[0] // TOP_K
    cap = n_blocks * MOE_BLK + 2 * FILL_ROWS
    return pl.pallas_call(
        functools.partial(_dispatch_kernel, tm=tm, n_blocks=n_blocks),
        out_shape=jax.ShapeDtypeStruct((cap * ROW_SUB, LANES), U32),
        grid_spec=pltpu.PrefetchScalarGridSpec(
            num_scalar_prefetch=4,
            grid=(n // tm,),
            in_specs=[pl.BlockSpec((tm * TOP_K,), lambda i, *_: (i,)),
                      pl.BlockSpec((tm * ROW_SUB, LANES), lambda i, *_: (i, 0))],
            out_specs=pl.BlockSpec(memory_space=pl.ANY),
            scratch_shapes=[pltpu.SMEM((tm * TOP_K,), jnp.int32),
                            pltpu.VMEM((STAGE_ROWS * ROW_SUB, LANES), U32),
                            pltpu.VMEM((STAGE_ROWS * ROW_SUB, LANES), U32),
                            pltpu.VMEM((MOE_BLK * ROW_SUB, LANES), U32),
                            pltpu.SemaphoreType.DMA, pltpu.SemaphoreType.DMA, pltpu.SemaphoreType.DMA,
                            pltpu.SemaphoreType.DMA]),
        compiler_params=pltpu.CompilerParams(dimension_semantics=("arbitrary",), vmem_limit_bytes=VMEM_LIMIT,
                                             has_side_effects=True),
        name="moe_dispatch",
    )(gstart, seg_cnt, seg_off, pad, ls8, h2_rows)


def _expert_kernel(be_ref, nu_ref, x_ref, w1_ref, w3_ref, w2_ref, o_ref, w1b, w3b, w2b):
    j = pl.program_id(0)
    e = be_ref[j]
    prev = be_ref[jnp.maximum(j - 1, 0)]
    used = j < nu_ref[0]

    @pl.when(jnp.logical_and(used, jnp.logical_or(j == 0, e != prev)))
    def _():
        w1b[...] = w1_ref[0].astype(BF16)
        w3b[...] = w3_ref[0].astype(BF16)
        w2b[...] = w2_ref[0].astype(BF16)

    @pl.when(used)
    def _():
        x = _from_token_rows(x_ref, MOE_BLK).astype(BF16)
        a = _dot(x, w1b[...])
        hid = (a * _sigmoid(a)) * _dot(x, w3b[...])
        _to_token_rows(o_ref, _dot(hid.astype(BF16), w2b[...]))

    @pl.when(jnp.logical_not(used))
    def _():
        o_ref[...] = jnp.zeros_like(o_ref)


def _experts(block_e, n_used, xs, w1, w3, w2):
    n_blocks = block_e.shape[0]
    d, f = w1.shape[1], w1.shape[2]
    rows = pl.BlockSpec((MOE_BLK * ROW_SUB, LANES), lambda j, be, nu: (j, 0))
    rows_in = pl.BlockSpec((MOE_BLK * ROW_SUB, LANES), lambda j, be, nu: (jnp.minimum(j, nu[0] - 1), 0))
    return pl.pallas_call(
        _expert_kernel,
        out_shape=jax.ShapeDtypeStruct((n_blocks * MOE_BLK * ROW_SUB, LANES), U32),
        grid_spec=pltpu.PrefetchScalarGridSpec(
            num_scalar_prefetch=2,
            grid=(n_blocks,),
            in_specs=[rows_in,
                      pl.BlockSpec((1, d, f), lambda j, be, nu: (be[j], 0, 0)),
                      pl.BlockSpec((1, d, f), lambda j, be, nu: (be[j], 0, 0)),
                      pl.BlockSpec((1, f, d), lambda j, be, nu: (be[j], 0, 0))],
            out_specs=rows,
            scratch_shapes=[pltpu.VMEM((d, f), BF16), pltpu.VMEM((d, f), BF16), pltpu.VMEM((f, d), BF16)]),
        compiler_params=_params("arbitrary"),
        name="moe_experts",
    )(block_e, n_used, xs, w1, w3, w2)


def _combine_kernel(gs_ref, cnt_ref, off_ref, ls_ref, w8_ref, x1_ref, h2_ref, g2_ref, ws1_ref, ws3_ref, ws2_ref,
                    fg_ref, ys_hbm, o_ref, ls_smem, w_smem, gbuf0, gbuf1, acc_rows, sem0, sem1, lsem, *, tm):
    i = pl.program_id(0)
    last = pl.num_programs(0) - 1
    cp1 = pltpu.make_async_copy(ls_ref, ls_smem, lsem)
    cp2 = pltpu.make_async_copy(w8_ref, w_smem, lsem)
    cp1.start()
    cp2.start()

    def fetch(tile, gbuf, sem):
        _copy_tile_segments(tile, ys_hbm, gs_ref, gbuf, off_ref, cnt_ref, off_ref,
                            lambda used: 0, lambda used: used, sem)

    @pl.when(i == 0)
    def _():
        fetch(0, gbuf0, sem0)

    @pl.when(jnp.logical_and(i < last, i % 2 == 0))
    def _():
        fetch(i + 1, gbuf1, sem1)

    @pl.when(jnp.logical_and(i < last, i % 2 == 1))
    def _():
        fetch(i + 1, gbuf0, sem0)

    h2 = _from_token_rows(h2_ref, tm).astype(BF16)
    a = _dot(h2, ws1_ref[...])
    hid = (a * _sigmoid(a)) * _dot(h2, ws3_ref[...])
    acc = _dot(hid.astype(BF16), ws2_ref[...])
    cp1.wait()
    cp2.wait()

    def reduce_rows(gbuf, sem):
        _wait_rows(gbuf, sem, STAGE_ROWS)

        def body(tu, carry):
            for u in range(TOKEN_UNROLL):
                t = tu * TOKEN_UNROLL + u
                lo = jnp.zeros((ROW_SUB, LANES), F32)
                hi = jnp.zeros((ROW_SUB, LANES), F32)
                for k in range(TOP_K):
                    w = w_smem[t * TOP_K + k]
                    words = gbuf[pl.ds(pl.multiple_of(ls_smem[t * TOP_K + k], ROW_SUB), ROW_SUB), :]
                    lo = lo + w * lax.bitcast_convert_type(words << 16, F32)
                    hi = hi + w * lax.bitcast_convert_type(words & jnp.uint32(0xFFFF0000), F32)
                acc_rows[pl.ds(pl.multiple_of(t * SUBLANES, SUBLANES), ROW_SUB), :] = lo
                acc_rows[pl.ds(pl.multiple_of(t * SUBLANES, SUBLANES) + ROW_SUB, ROW_SUB), :] = hi
            return carry

        lax.fori_loop(0, tm // TOKEN_UNROLL, body, 0)

    pl.when(i % 2 == 0)(functools.partial(reduce_rows, gbuf0, sem0))
    pl.when(i % 2 == 1)(functools.partial(reduce_rows, gbuf1, sem1))
    routed = jnp.concatenate([acc_rows[pl.ds(s, tm, stride=SUBLANES), :] for s in range(SUBLANES)], axis=-1)
    y = x1_ref[...] + g2_ref[0] * (acc + routed)
    o_ref[...] = y * lax.rsqrt(jnp.mean(y * y, axis=-1, keepdims=True) + EPS) * fg_ref[...]


def _combine(gstart, seg_cnt, seg_off, ls8, w8, x1, h2_rows, g2, ws1, ws3, ws2, fg, ys, rows_per_batch, tm):
    n, d = x1.shape
    per = rows_per_batch // tm
    tok = pl.BlockSpec((tm * TOP_K,), lambda i, *_: (i,))
    full = lambda a: pl.BlockSpec(a.shape, lambda i, *_: (0, 0))
    return pl.pallas_call(
        functools.partial(_combine_kernel, tm=tm),
        out_shape=jax.ShapeDtypeStruct((n, d), F32),
        grid_spec=pltpu.PrefetchScalarGridSpec(
            num_scalar_prefetch=3,
            grid=(n // tm,),
            in_specs=[tok, tok, pl.BlockSpec((tm, d), lambda i, *_: (i, 0)),
                      pl.BlockSpec((tm * ROW_SUB, LANES), lambda i, *_: (i, 0)),
                      pl.BlockSpec((1, 1, d), lambda i, *_: (i // per, 0, 0)),
                      full(ws1), full(ws3), full(ws2), pl.BlockSpec((1, d), lambda i, *_: (0, 0)),
                      pl.BlockSpec(memory_space=pl.ANY)],
            out_specs=pl.BlockSpec((tm, d), lambda i, *_: (i, 0)),
            scratch_shapes=[pltpu.SMEM((tm * TOP_K,), jnp.int32), pltpu.SMEM((tm * TOP_K,), F32),
                            pltpu.VMEM((STAGE_ROWS * ROW_SUB, LANES), U32),
                            pltpu.VMEM((STAGE_ROWS * ROW_SUB, LANES), U32),
                            pltpu.VMEM((tm * SUBLANES, LANES), F32), pltpu.SemaphoreType.DMA,
                            pltpu.SemaphoreType.DMA, pltpu.SemaphoreType.DMA]),
        compiler_params=_params("arbitrary"),
        name="moe_combine_final",
    )(gstart, seg_cnt, seg_off, ls8, w8, x1, h2_rows, g2, ws1, ws3, ws2, fg.reshape(1, d), ys)


def _moe_plan(seg, counts, n_assign):
    cnt = counts.reshape(N_EXPERTS).astype(jnp.int32)
    padded = (cnt + MOE_BLK - 1) // MOE_BLK * MOE_BLK
    pends = jnp.cumsum(padded)
    pstarts = pends - padded
    max_rows = n_assign + seg.shape[0] * N_EXPERTS * (SEG_ALIGN - 1)
    n_blocks = (max_rows + N_EXPERTS * (MOE_BLK - 1) + MOE_BLK - 1) // MOE_BLK
    seg = seg[:, :, :3].astype(jnp.int32)
    gstart = pstarts[None, :] + seg[:, :, 0]
    blk_start = jnp.arange(n_blocks, dtype=jnp.int32) * MOE_BLK
    block_e = jnp.minimum(jnp.sum((blk_start[:, None] >= pends[None, :]).astype(jnp.int32), axis=1),
                          N_EXPERTS - 1).astype(jnp.int32)
    n_used = (pends[-1:] // MOE_BLK).astype(jnp.int32)
    pad = jnp.stack([pstarts + cnt, padded - cnt, jnp.broadcast_to(n_used, (N_EXPERTS,))], axis=0).astype(jnp.int32)
    return gstart, seg[:, :, 1], seg[:, :, 2], pad, block_e, n_used, n_blocks


def _mixer(x, c, ctx, c_ctx, w_ada, b_ada, norm1_g, norm2_g, w_in, s5_lam_re, s5_lam_im, s5_log_dt,
           s5_b_re, s5_b_im, s5_c_re, s5_c_im, s5_d, s5_w_glu, lb, hg_norm_g, p_a, p_b, w_out,
           moe_w_router, moe_b_router):
    b, l, d = x.shape
    lc = ctx.shape[1]
    n = b * l
    rows = l // GRID_W

    c8 = jnp.concatenate([c, c_ctx[None], jnp.zeros((8 - b - 1, d), F32)], axis=0)
    mod = _ada(c8, w_ada, b_ada)
    sh1, sc1, g1, sh2, sc2, g2 = [mod[:b, k * d:(k + 1) * d].reshape(b, 1, d) for k in range(6)]
    csh1, csc1 = mod[b:b + 1, 0:d].reshape(1, 1, d), mod[b:b + 1, d:2 * d].reshape(1, 1, d)

    w_in_b = w_in.astype(BF16)
    z = dict(zip([p[0] for p in _IN_PIECES],
                 _inproj(x.reshape(n, d), sc1, sh1, norm1_g, w_in_b, l, TOK_TILE, True)))
    zc = dict(zip([p[0] for p in _IN_PIECES],
                  _inproj(ctx.reshape(b * lc, d), csc1, csh1, norm1_g, w_in_b, lc, lc, False)))

    cx = lambda t: t.reshape(b, lc, HG_WIDTH)
    lb_row = lb.reshape(1, HG_WIDTH)
    o_f = _hgrn_pass(z["q"], z["ff"], z["i"], cx(zc["ff"]), cx(zc["i"]), lb_row, None, None, reverse=False)
    o_n = _hgrn_pass(z["q"], z["fb"], z["i"], cx(zc["fb"]), cx(zc["i"]), lb_row, o_f,
                     hg_norm_g.reshape(1, HG_DK), reverse=True)

    d_lag, w_s5_in, w_out_f, w_out_b, decay = _s5_weights(s5_lam_re, s5_lam_im, s5_log_dt, s5_b_re, s5_b_im,
                                                          s5_c_re, s5_c_im)
    kc, kl = lc // S5_T, l // S5_T
    u_lat = z["u"].reshape(b, kl, S5_T * S5_WIDTH)
    u_ctx = zc["u"].reshape(b, kc, S5_T * S5_WIDTH)
    rows_in = kl + kc
    u_ext = jnp.concatenate([u_lat, u_ctx], axis=1).reshape(b * rows_in, S5_T * S5_WIDTH)
    e = _s5_in(u_ext, d_lag, w_s5_in, (b * rows_in) // 2)
    states = _s5_scan(e, decay, b, rows_in, kl)
    d_row = s5_d.astype(F32).reshape(1, S5_WIDTH)
    y_a = _s5_out(states, w_out_f, w_out_b, e.reshape(b, rows_in, -1), z["u"], d_row, s5_w_glu.astype(BF16))

    return _merge(x.reshape(n, d), y_a, o_n, z["go"], z["ga"], z["gb"], g1, sc2, sh2, norm2_g,
                  p_a.astype(BF16), p_b.astype(BF16), w_out.astype(BF16),
                  moe_w_router.T.astype(BF16), moe_b_router.astype(F32).reshape(N_EXPERTS, 1), l, MOE_TILE) + (g2,)


def kernel(x, c, ctx, c_ctx, w_ada, b_ada, norm1_g, norm2_g, w_in, s5_lam_re, s5_lam_im, s5_log_dt, s5_b_re,
           s5_b_im, s5_c_re, s5_c_im, s5_d, s5_w_glu, hg_lb_logits, hg_norm_g, p_a, p_b, w_out, moe_w_router,
           moe_b_router, moe_w1, moe_w3, moe_w2, moe_ws1, moe_ws3, moe_ws2, final_norm_g):
    b, l, d = x.shape
    n = b * l
    assert w_ada.shape[0] == 1, "single-layer block"
    lb = jnp.cumsum(jax.nn.softmax(hg_lb_logits.astype(F32), axis=0), axis=0)[0]
    x1, h2_rows, ls8, w8, seg, counts, g2 = _mixer(
        x, c, ctx, c_ctx, w_ada[0], b_ada[0], norm1_g[0], norm2_g[0], w_in[0], s5_lam_re[0], s5_lam_im[0],
        s5_log_dt[0], s5_b_re[0], s5_b_im[0], s5_c_re[0], s5_c_im[0], s5_d[0], s5_w_glu[0], lb, hg_norm_g[0],
        p_a[0], p_b[0], w_out[0], moe_w_router[0], moe_b_router[0])
    gstart, seg_cnt, seg_off, pad, block_e, n_used, n_blocks = _moe_plan(seg, counts, n * TOP_K)
    ls_flat, w_flat = ls8.T.reshape(n * TOP_K), w8.T.reshape(n * TOP_K)
    xs = _dispatch(gstart, seg_cnt, seg_off, pad, ls_flat, h2_rows, n_blocks, MOE_TILE)
    ys = _experts(block_e, n_used, xs, moe_w1[0], moe_w3[0], moe_w2[0])
    out = _combine(gstart, seg_cnt, seg_off, ls_flat, w_flat, x1, h2_rows, g2, moe_ws1[0].astype(BF16),
                   moe_ws3[0].astype(BF16), moe_ws2[0].astype(BF16), final_norm_g, ys, l, MOE_TILE)
    return out.reshape(b, l, d)
```

```python
import functools
import math

import jax
import jax.numpy as jnp
from jax import lax
from jax.experimental import pallas as pl
from jax.experimental.pallas import tpu as pltpu

F32 = jnp.float32
BF16 = jnp.bfloat16

GRID_W = 64
S5_WIDTH = 256
S5_GROUP = 16
S5_GROUPS = 16
S5_STATE = 64
HG_HEADS = 6
HG_DK = 128
HG_WIDTH = HG_HEADS * HG_DK
N_EXPERTS = 64
ROUTE_GROUPS = 8
TOPK_GROUPS = 4
TOP_K = 8
ROUTED_SCALE = 2.5
EPS = 1e-6

LANES = 128
SUBLANES = 8

TOK_TILE = 512
S5_T = 16
HG_CHUNK = 64
HG_BATCH = 4
VMEM_LIMIT = 56 * 1024 * 1024

_NT = (((1,), (1,)), ((), ()))
_TN = (((0,), (0,)), ((), ()))


def _params(*sem):
    return pltpu.CompilerParams(dimension_semantics=sem, vmem_limit_bytes=VMEM_LIMIT)


def _dot(a, b):
    return jnp.dot(a, b, preferred_element_type=F32)


def _sigmoid(x):
    return 0.5 * jnp.tanh(0.5 * x) + 0.5


def _ada_kernel(c_ref, w_ref, b_ref, o_ref):
    c = c_ref[...]
    s = (c * _sigmoid(c)).astype(BF16)
    o_ref[...] = _dot(s, w_ref[...].astype(BF16)) + b_ref[...]


def _ada(c8, w_ada, b_ada):
    d, n = w_ada.shape
    tn = 1536
    return pl.pallas_call(
        _ada_kernel,
        out_shape=jax.ShapeDtypeStruct((8, n), F32),
        grid=(n // tn,),
        in_specs=[pl.BlockSpec((8, d), lambda j: (0, 0)),
                  pl.BlockSpec((d, tn), lambda j: (0, j)),
                  pl.BlockSpec((1, tn), lambda j: (0, j))],
        out_specs=pl.BlockSpec((8, tn), lambda j: (0, j)),
        compiler_params=_params("arbitrary"),
        name="ada_mod",
    )(c8, w_ada, b_ada.reshape(1, n))


_IN_PIECES = (("u", 0, 256, BF16), ("q", 256, 768, BF16), ("ff", 1024, 768, BF16),
              ("fb", 1792, 768, BF16), ("i", 2560, 768, BF16), ("go", 3328, 768, BF16),
              ("ga", 4096, 1024, BF16), ("gb", 5120, 1024, BF16))


def _fold_rows(val, buf_a, buf_b):
    t = val.shape[0]
    buf_a[...] = val[:, :LANES]
    buf_b[...] = val[:, LANES:]
    pieces = []
    for s in range(S5_T):
        pieces += [buf_a[pl.ds(s, t // S5_T, stride=S5_T), :], buf_b[pl.ds(s, t // S5_T, stride=S5_T), :]]
    return jnp.concatenate(pieces, axis=-1)


def _unfold_rows(val, buf_a, buf_b):
    r = val.shape[0]
    for s in range(S5_T):
        buf_a[pl.ds(s, r, stride=S5_T), :] = val[:, s * S5_WIDTH:s * S5_WIDTH + LANES]
        buf_b[pl.ds(s, r, stride=S5_T), :] = val[:, s * S5_WIDTH + LANES:(s + 1) * S5_WIDTH]
    return jnp.concatenate([buf_a[...], buf_b[...]], axis=-1)


def _grid_transpose_matrix(tm):
    i = jnp.arange(tm)
    src = (i % (tm // GRID_W)) * GRID_W + i // (tm // GRID_W)
    return (src[:, None] == jnp.arange(tm)[None, :]).astype(BF16)


def _inproj_kernel(x_ref, sc_ref, sh_ref, g_ref, w_ref, p_ref, *o_refs):
    o_refs, (fold_a, fold_b) = o_refs[:len(_IN_PIECES)], o_refs[len(_IN_PIECES):]
    x = x_ref[...]
    y = x * lax.rsqrt(jnp.mean(x * x, axis=-1, keepdims=True) + EPS) * g_ref[...]
    h = (y * (1.0 + sc_ref[0]) + sh_ref[0]).astype(BF16)
    h_cm = None
    for (name, a, wd, _), o_ref in zip(_IN_PIECES, o_refs):
        if name == "u":
            o_ref[...] = _fold_rows(_dot(h, w_ref[:, a:a + wd]), fold_a, fold_b).astype(o_ref.dtype)
        elif len(o_ref.shape) == 2:
            o_ref[...] = _dot(h, w_ref[:, a:a + wd]).astype(o_ref.dtype)
        else:
            if h_cm is None:
                h_cm = _dot(p_ref[...], h).astype(BF16)
            o_ref[0] = _dot(h_cm, w_ref[:, a:a + wd]).astype(o_ref.dtype).reshape(o_ref.shape[1:])


_COLMAJOR_PIECES = ("q", "ff", "fb", "i")


def _inproj(x2d, sc, sh, g, w_bf16, rows_per_mod, tm, colmajor):
    n, d = x2d.shape
    per = rows_per_mod // tm
    mod_map = (lambda i: (i // per, 0, 0)) if sc.shape[0] > 1 else (lambda i: (0, 0, 0))
    shapes, specs = [], []
    for name, _, wd, dt in _IN_PIECES:
        if colmajor and name in _COLMAJOR_PIECES:
            shapes.append(jax.ShapeDtypeStruct((n // rows_per_mod, GRID_W, rows_per_mod // GRID_W, wd), dt))
            specs.append(pl.BlockSpec((1, GRID_W, tm // GRID_W, wd), lambda i: (i // per, 0, i % per, 0)))
        elif name == "u":
            shapes.append(jax.ShapeDtypeStruct((n // S5_T, S5_T * wd), dt))
            specs.append(pl.BlockSpec((tm // S5_T, S5_T * wd), lambda i: (i, 0)))
        else:
            shapes.append(jax.ShapeDtypeStruct((n, wd), dt))
            specs.append(pl.BlockSpec((tm, wd), lambda i: (i, 0)))
    return pl.pallas_call(
        _inproj_kernel,
        out_shape=shapes,
        grid=(n // tm,),
        in_specs=[pl.BlockSpec((tm, d), lambda i: (i, 0)),
                  pl.BlockSpec((1, 1, d), mod_map),
                  pl.BlockSpec((1, 1, d), mod_map),
                  pl.BlockSpec((1, d), lambda i: (0, 0)),
                  pl.BlockSpec(w_bf16.shape, lambda i: (0, 0)),
                  pl.BlockSpec((tm, tm), lambda i: (0, 0))],
        out_specs=specs,
        scratch_shapes=[pltpu.VMEM((tm, LANES), F32), pltpu.VMEM((tm, LANES), F32)],
        compiler_params=_params("arbitrary"),
        name="in_proj",
    )(x2d, sc, sh, g.reshape(1, d), w_bf16, _grid_transpose_matrix(tm))


def _hgrn_gates(zf, lb):
    sig = _sigmoid(zf)
    logf = jnp.log(lb + (1.0 - lb) * sig)
    k = (1.0 - lb) * (1.0 - sig)
    return logf, k


def _chunk_cumsum(cs, logf):
    hi = logf.astype(BF16)
    lo = (logf - hi.astype(F32)).astype(BF16)
    return _dot(cs, hi) + _dot(cs, lo)


def _hgrn_state_step(zf, v, lb, st, cs, reverse):
    logf, k = _hgrn_gates(zf, lb)
    cum = _chunk_cumsum(cs, logf)
    t = 0 if reverse else HG_CHUNK - 1
    total = cum[t:t + 1, :]
    kdec = (k * jnp.exp(total - cum)).astype(BF16)
    st_new = st * jnp.exp(total) + lax.dot_general(v.astype(BF16), kdec, _TN, preferred_element_type=F32)
    return cum, k, st_new


def _hgrn_kernel(*refs, reverse, final, n_ctx_chunks):
    if final:
        q_all, f_all, v_all, cf_ref, cv_ref, lb_ref, of_all, g_ref, o_all, st_ref = refs
    else:
        q_all, f_all, v_all, cf_ref, cv_ref, lb_ref, o_all, st_ref = refs
        of_all = None
    n_batch = q_all.shape[0]
    c_len = HG_CHUNK
    n_rows = q_all.shape[2]
    n_chunks = n_rows // c_len
    row = lax.broadcasted_iota(jnp.int32, (n_rows, n_rows), 0)
    col = lax.broadcasted_iota(jnp.int32, (n_rows, n_rows), 1)
    tri = (col >= row) if reverse else (col <= row)
    same_chunk = None
    for c in range(n_chunks):
        lo, hi = c * c_len, (c + 1) * c_len
        blk = (row >= lo) & (row < hi) & (col >= lo) & (col < hi)
        same_chunk = blk if same_chunk is None else (same_chunk | blk)
    mask = tri & same_chunk
    cs = jnp.where(mask, 1.0, 0.0).astype(BF16)

    @pl.when(pl.program_id(1) == 0)
    def _():
        cs1 = cs[:c_len, :c_len]
        order = range(n_ctx_chunks - 1, -1, -1) if reverse else range(n_ctx_chunks)
        for bi in range(n_batch):
            for h in range(HG_HEADS):
                cols = slice(h * HG_DK, (h + 1) * HG_DK)
                st = jnp.zeros((HG_DK, HG_DK), F32)
                for c in order:
                    rows = slice(c * c_len, (c + 1) * c_len)
                    _, _, st = _hgrn_state_step(cf_ref[bi, rows, cols].astype(F32), cv_ref[bi, rows, cols].astype(F32),
                                                lb_ref[:, cols], st, cs1, reverse)
                st_ref[bi * HG_HEADS + h] = st

    def chunk_rows(x, r):
        return [x[c * c_len + r:c * c_len + r + 1, :] for c in range(n_chunks)]

    def over_chunks(rows):
        return jnp.concatenate([jnp.broadcast_to(r, (c_len, r.shape[1])) for r in rows], axis=0)

    lb = lb_ref[...]
    r_ref = c_len // 2 - 1 if reverse else c_len // 2
    r_tot = 0 if reverse else c_len - 1
    order = range(n_chunks - 1, -1, -1) if reverse else range(n_chunks)
    for bi in range(n_batch):
        q = q_all[bi, 0].astype(F32)
        v = v_all[bi, 0]
        logf, k = _hgrn_gates(f_all[bi, 0].astype(F32), lb)
        cum = _chunk_cumsum(cs, logf)
        ref_rows, tot_rows = chunk_rows(cum, r_ref), chunk_rows(cum, r_tot)
        ref = over_chunks(ref_rows)
        qe = q * jnp.exp(cum - ref)
        ke = k * jnp.exp(ref - cum)
        qi, ki = qe.astype(BF16), ke.astype(BF16)
        q_in = (qe * over_chunks([jnp.exp(r) for r in ref_rows])).astype(BF16)
        kdec = (ke * over_chunks([jnp.exp(t - r) for t, r in zip(tot_rows, ref_rows)])).astype(BF16)
        for h in range(HG_HEADS):
            cols = slice(h * HG_DK, (h + 1) * HG_DK)
            s = lax.dot_general(qi[:, cols], ki[:, cols], _NT, preferred_element_type=F32)
            o_intra = _dot(jnp.where(mask, s, 0.0).astype(BF16), v[:, cols])
            st = st_ref[bi * HG_HEADS + h]
            for c in order:
                rows = slice(c * c_len, (c + 1) * c_len)
                o = o_intra[rows] + lax.dot_general(q_in[rows, cols], st.astype(BF16), _NT,
                                                    preferred_element_type=F32)
                total = cum[c * c_len + r_tot:c * c_len + r_tot + 1, cols]
                st = st * jnp.exp(total) + lax.dot_general(v[rows, cols], kdec[rows, cols], _TN,
                                                           preferred_element_type=F32)
                if final:
                    o = o + of_all[bi, 0, rows, cols].astype(F32)
                    o = o * lax.rsqrt(jnp.mean(o * o, axis=-1, keepdims=True) + EPS) * g_ref[...]
                o_all[bi, 0, rows, cols] = o.astype(o_all.dtype)
            st_ref[bi * HG_HEADS + h] = st


def _hgrn_pass(q, f, v, cf, cv, lb, o_prev, g, *, reverse):
    b, nw, rows, _ = q.shape
    nb = HG_BATCH if b % HG_BATCH == 0 else 1
    final = o_prev is not None
    wmap = (lambda bi, w: (bi, nw - 1 - w, 0, 0)) if reverse else (lambda bi, w: (bi, w, 0, 0))
    blk = pl.BlockSpec((nb, 1, rows, HG_WIDTH), wmap)
    cblk = pl.BlockSpec((nb, cf.shape[1], HG_WIDTH), lambda bi, w: (bi, 0, 0))
    in_specs = [blk, blk, blk, cblk, cblk, pl.BlockSpec((1, HG_WIDTH), lambda bi, w: (0, 0))]
    args = [q, f, v, cf, cv, lb]
    if final:
        in_specs += [blk, pl.BlockSpec((1, HG_DK), lambda bi, w: (0, 0))]
        args += [o_prev, g]
    return pl.pallas_call(
        functools.partial(_hgrn_kernel, reverse=reverse, final=final, n_ctx_chunks=cf.shape[1] // HG_CHUNK),
        out_shape=jax.ShapeDtypeStruct(q.shape, BF16),
        grid=(b // nb, nw),
        in_specs=in_specs,
        out_specs=blk,
        scratch_shapes=[pltpu.VMEM((nb * HG_HEADS, HG_DK, HG_DK), F32)],
        compiler_params=_params("arbitrary", "arbitrary"),
        name="hgrn_bwd" if reverse else "hgrn_fwd",
    )(*args)


def _s5_weights(lam_re, lam_im, log_dt, b_re, b_im, c_re, c_im):
    hp = lax.Precision.HIGHEST
    g, p, cc, t = S5_GROUPS, S5_STATE, S5_GROUP, S5_T
    lre = jnp.minimum(lam_re.astype(F32), -1e-4)
    lim = lam_im.astype(F32)
    dt = jnp.exp(log_dt.astype(F32))[..., None]
    ks = jnp.arange(t + 1, dtype=F32)[:, None, None, None]
    mag = jnp.exp(ks * (lre * dt)[None])
    pw_re = mag * jnp.cos(ks * (lim * dt)[None])
    pw_im = mag * jnp.sin(ks * (lim * dt)[None])
    nr, ni = pw_re[1] - 1.0, pw_im[1]
    den = lre * lre + lim * lim
    cf_re = (nr * lre + ni * lim) / den
    cf_im = (ni * lre - nr * lim) / den
    bb_re = cf_re[..., None] * b_re - cf_im[..., None] * b_im
    bb_im = cf_re[..., None] * b_im + cf_im[..., None] * b_re
    cre, cim = c_re.astype(F32), c_im.astype(F32)
    sw, ns = S5_WIDTH, 2 * g * p
    grp_of_row = jnp.arange(sw)[:, None] // cc

    cp_re = cre[None, None, :, :, :] * pw_re[:t, :, :, None, :] - cim[None, None] * pw_im[:t, :, :, None, :]
    cp_im = cre[None, None, :, :, :] * pw_im[:t, :, :, None, :] + cim[None, None] * pw_re[:t, :, :, None, :]
    def contract_p(cp, bb):
        return jnp.sum(cp.transpose(4, 1, 0, 2, 3)[..., None] * bb.transpose(2, 0, 1, 3)[:, :, None, :, None, :],
                       axis=0)

    kk = contract_p(cp_re, bb_re) - contract_p(cp_im, bb_im)
    kf, kb = kk[0], kk[1]
    kall = jnp.concatenate([kb[:0:-1], (kf[0] + kb[0])[None], kf[1:]], axis=0)
    kt = kall.transpose(0, 1, 3, 2).reshape(2 * t - 1, sw, cc)

    def spread(x, period, reps):
        sel = (jnp.arange(period)[:, None] == (jnp.arange(period * reps)[None, :] % period)).astype(BF16)
        return jnp.dot(x.astype(BF16), sel, preferred_element_type=BF16)

    same = grp_of_row == (jnp.arange(sw)[None, :] // cc)
    d_lag = jnp.where(same[None], spread(kt, cc, g), 0)

    same_in = jnp.tile(grp_of_row, (t, 1)) == ((jnp.arange(ns)[None, :] % (g * p)) // p)

    def in_to_state(pre, pim, bre, bim):
        xre = pre[..., None] * bre[None] - pim[..., None] * bim[None]
        xim = pre[..., None] * bim[None] + pim[..., None] * bre[None]
        return [spread(xre.transpose(0, 1, 3, 2).reshape(t * sw, p), p, g),
                spread(xim.transpose(0, 1, 3, 2).reshape(t * sw, p), p, g)]

    w_in = jnp.concatenate(in_to_state(pw_re[t - 1::-1, 0], pw_im[t - 1::-1, 0], bb_re[0], bb_im[0])
                           + in_to_state(pw_re[:t, 1], pw_im[:t, 1], bb_re[1], bb_im[1]), axis=1)
    w_in = jnp.where(jnp.tile(same_in, (1, 2)), w_in, 0)

    same_out = ((jnp.arange(ns)[:, None] % (g * p)) // p) == ((jnp.arange(t * sw)[None, :] // cc) % g)
    col = jnp.arange(t * sw)
    pick = (jnp.arange(t * cc)[:, None] == ((col // sw) * cc + col % cc)[None, :]).astype(BF16)

    def state_to_out(pre, pim):
        are = cre[None] * pre[:, :, None, :] - cim[None] * pim[:, :, None, :]
        aim = cre[None] * pim[:, :, None, :] + cim[None] * pre[:, :, None, :]
        a = jnp.concatenate([are.transpose(1, 3, 0, 2), -aim.transpose(1, 3, 0, 2)], axis=0)
        a = jnp.dot(a.reshape(ns, t * cc).astype(BF16), pick, preferred_element_type=BF16)
        return jnp.where(same_out, a, 0)

    w_out_f = state_to_out(pw_re[1:, 0], pw_im[1:, 0])
    w_out_b = state_to_out(pw_re[t:0:-1, 1], pw_im[t:0:-1, 1])

    decay = jnp.stack([pw_re[t].reshape(2, g * p), pw_im[t].reshape(2, g * p)], axis=1)
    return d_lag, w_in, w_out_f, w_out_b, decay


def _s5_in_kernel(u_ref, d_ref, w_ref, o_ref):
    j = pl.program_id(1)

    @pl.when(j < S5_T)
    def _():
        acc = _dot(u_ref[:, 0:S5_WIDTH], d_ref[j + S5_T - 1])
        for s in range(1, S5_T):
            acc = acc + _dot(u_ref[:, s * S5_WIDTH:(s + 1) * S5_WIDTH], d_ref[j - s + S5_T - 1])
        o_ref[...] = acc

    @pl.when(j >= S5_T)
    def _():
        o_ref[...] = _dot(u_ref[...], w_ref[...])


def _s5_in(u, d_lag, w_in, tm):
    m, k = u.shape
    tn = S5_WIDTH
    nj = (k + w_in.shape[1]) // tn
    return pl.pallas_call(
        _s5_in_kernel,
        out_shape=jax.ShapeDtypeStruct((m, nj * tn), F32),
        grid=(m // tm, nj),
        in_specs=[pl.BlockSpec((tm, k), lambda i, j: (i, 0)),
                  pl.BlockSpec(d_lag.shape, lambda i, j: (0, 0, 0)),
                  pl.BlockSpec((k, tn), lambda i, j: (0, jnp.maximum(j - S5_T, 0)))],
        out_specs=pl.BlockSpec((tm, tn), lambda i, j: (i, j)),
        compiler_params=_params("arbitrary", "arbitrary"),
        name="s5_in",
    )(u, d_lag, w_in)


def _s5_scan_kernel(efr_ref, efi_ref, ebr_ref, ebi_ref, a_ref, hfr_ref, hfi_ref, hbr_ref, hbi_ref,
                    *, nb, rows_in, rows_out):
    dirs = ((efr_ref, efi_ref, hfr_ref, hfi_ref, a_ref[0, 0:1, :], a_ref[0, 1:2, :]),
            (ebr_ref, ebi_ref, hbr_ref, hbi_ref, a_ref[1, 0:1, :], a_ref[1, 1:2, :]))
    zero = jnp.zeros_like(dirs[0][4])

    def step(srcs, carry, store):
        new = []
        for di, (er_ref, ei_ref, hr_ref, hi_ref, are, aim) in enumerate(dirs):
            for bi in range(nb):
                hre, him = carry[2 * (di * nb + bi)], carry[2 * (di * nb + bi) + 1]
                if store:
                    hr_ref[pl.ds(bi * rows_out + srcs[di], 1), :] = hre
                    hi_ref[pl.ds(bi * rows_out + srcs[di], 1), :] = him
                ere = er_ref[pl.ds(bi * rows_in + srcs[di], 1), :]
                eim = ei_ref[pl.ds(bi * rows_in + srcs[di], 1), :]
                new += [are * hre - aim * him + ere, are * him + aim * hre + eim]
        return tuple(new)

    n_ctx = rows_in - rows_out
    carry = lax.fori_loop(0, n_ctx, lambda s, c: step((rows_out + s, rows_in - 1 - s), c, False),
                          tuple([zero] * (4 * nb)))
    lax.fori_loop(0, rows_out, lambda s, c: step((s, rows_out - 1 - s), c, True), carry)


def _s5_scan(e, decay, nb, rows_in, rows_out):
    tc = 256
    nsr = S5_GROUPS * S5_STATE
    c0 = (S5_T * S5_WIDTH) // tc
    nt = nsr // tc
    eblk = lambda k: pl.BlockSpec((nb * rows_in, tc), lambda j: (0, c0 + k * nt + j))
    hblk = pl.BlockSpec((nb * rows_out, tc), lambda j: (0, j))
    return pl.pallas_call(
        functools.partial(_s5_scan_kernel, nb=nb, rows_in=rows_in, rows_out=rows_out),
        out_shape=[jax.ShapeDtypeStruct((nb * rows_out, nsr), F32)] * 4,
        grid=(nt,),
        in_specs=[eblk(0), eblk(1), eblk(2), eblk(3), pl.BlockSpec((2, 2, tc), lambda j: (0, 0, j))],
        out_specs=[hblk] * 4,
        compiler_params=_params("arbitrary"),
        name="s5_scan",
    )(e, e, e, e, decay)


def _gelu_tanh(x):
    return 0.5 * x * (1.0 + jnp.tanh(math.sqrt(2.0 / math.pi) * (x + 0.044715 * x * x * x)))


def _s5_out_kernel(hfr_ref, hfi_ref, hbr_ref, hbi_ref, wf_ref, wb_ref, yi_ref, u_ref, d_ref, wg_ref, o_ref):
    nsr = hfr_ref.shape[1]
    y = yi_ref[0] + d_ref[...] * u_ref[...].astype(F32)
    for h_ref, w_ref, r0 in ((hfr_ref, wf_ref, 0), (hfi_ref, wf_ref, nsr), (hbr_ref, wb_ref, 0), (hbi_ref, wb_ref, nsr)):
        y = y + _dot(h_ref[...].astype(BF16), w_ref[r0:r0 + nsr, :])
    y = _gelu_tanh(y)
    gate = _sigmoid(_dot(y.astype(BF16), wg_ref[...]))
    o_ref[...] = (y * gate).astype(o_ref.dtype)


def _s5_out(states, w_out_f, w_out_b, e3, u_rows, d_row, w_glu):
    m, nsr = states[0].shape
    nb = e3.shape[0]
    tm = m // nb
    tn = S5_WIDTH
    st = pl.BlockSpec((tm, nsr), lambda i, j: (i, 0))
    wo = pl.BlockSpec((2 * nsr, tn), lambda i, j: (0, j))
    return pl.pallas_call(
        _s5_out_kernel,
        out_shape=jax.ShapeDtypeStruct((m, S5_T * S5_WIDTH), BF16),
        grid=(nb, S5_T),
        in_specs=[st, st, st, st, wo, wo,
                  pl.BlockSpec((1, tm, tn), lambda i, j: (i, 0, j)),
                  pl.BlockSpec((tm, tn), lambda i, j: (i, j)),
                  pl.BlockSpec((1, tn), lambda i, j: (0, 0)),
                  pl.BlockSpec((tn, tn), lambda i, j: (0, 0))],
        out_specs=pl.BlockSpec((tm, tn), lambda i, j: (i, j)),
        compiler_params=_params("arbitrary", "arbitrary"),
        name="s5_out",
    )(*states, w_out_f, w_out_b, e3, u_rows, d_row, w_glu)


U32 = jnp.uint32
ROW_SUB = 4


def _to_token_rows(ref, val):
    t, d = val.shape
    bits = lax.bitcast_convert_type(val.astype(BF16).astype(F32), U32)
    w = (bits[:, :d // 2] >> 16) | bits[:, d // 2:]
    for s in range(ROW_SUB):
        ref[pl.ds(s, t, stride=ROW_SUB), :] = w[:, s * LANES:(s + 1) * LANES]


def _from_token_rows(ref, t, row0=0):
    w = jnp.concatenate([ref[pl.ds(row0 * ROW_SUB + s, t, stride=ROW_SUB), :] for s in range(ROW_SUB)], axis=-1)
    lo = lax.bitcast_convert_type(w << 16, F32)
    hi = lax.bitcast_convert_type(w & jnp.uint32(0xFFFF0000), F32)
    return jnp.concatenate([lo, hi], axis=-1)


def _route(h2b, wr_ref, br_ref, cnt_ref, ls8_ref, w8_ref, seg_ref):
    tm = h2b.shape[0]
    per_group = N_EXPERTS // ROUTE_GROUPS
    scores = _sigmoid(lax.dot_general(wr_ref[...], h2b, _NT, preferred_element_type=F32))
    biased = scores + br_ref[...]
    neg = -jnp.inf
    sub = lax.broadcasted_iota(jnp.int32, (per_group, tm), 0)
    grp = []
    for gi in range(ROUTE_GROUPS):
        v = biased[gi * per_group:(gi + 1) * per_group, :]
        m1 = jnp.max(v, axis=0, keepdims=True)
        first = jnp.min(jnp.where(v == m1, sub, per_group), axis=0, keepdims=True)
        m2 = jnp.max(jnp.where(sub == first, neg, v), axis=0, keepdims=True)
        grp.append(m1 + m2)
    grp = jnp.concatenate(grp, axis=0)
    gid = lax.broadcasted_iota(jnp.int32, (ROUTE_GROUPS, tm), 0)
    beaten = jnp.zeros((ROUTE_GROUPS, tm), jnp.int32)
    for gj in range(ROUTE_GROUPS):
        r = grp[gj:gj + 1, :]
        beaten = beaten + jnp.where((r > grp) | ((r == grp) & (gj < gid)), 1, 0)
    group_ok = beaten < TOPK_GROUPS
    expert_ok = jnp.concatenate(
        [jnp.broadcast_to(group_ok[gi:gi + 1, :], (per_group, tm)) for gi in range(ROUTE_GROUPS)], axis=0)
    cur = jnp.where(expert_ok, biased, neg)
    eid = lax.broadcasted_iota(jnp.int32, (N_EXPERTS, tm), 0)
    sel = jnp.zeros((N_EXPERTS, tm), F32)
    picks, wts = [], []
    for _ in range(TOP_K):
        m = jnp.max(cur, axis=0, keepdims=True)
        idx = jnp.min(jnp.where(cur == m, eid, N_EXPERTS), axis=0, keepdims=True)
        hit = eid == idx
        picks.append(idx)
        wts.append(jnp.sum(jnp.where(hit, scores, 0.0), axis=0, keepdims=True))
        sel = jnp.where(hit, 1.0, sel)
        cur = jnp.where(hit, neg, cur)
    wsum = wts[0]
    for w in wts[1:]:
        wsum = wsum + w
    selb = sel.astype(BF16)
    ti = lax.broadcasted_iota(jnp.int32, (tm, tm), 0)
    tj = lax.broadcasted_iota(jnp.int32, (tm, tm), 1)
    rank = _dot(selb, jnp.where(ti < tj, 1.0, 0.0).astype(BF16))
    seg_units = jnp.ceil(jnp.sum(sel, axis=1, keepdims=True) * (1.0 / SEG_ALIGN))
    ei = lax.broadcasted_iota(jnp.int32, (N_EXPERTS, N_EXPERTS), 0)
    ej = lax.broadcasted_iota(jnp.int32, (N_EXPERTS, N_EXPERTS), 1)
    units_row = jnp.broadcast_to(seg_units, (N_EXPERTS, LANES)).astype(BF16)
    seg_off = _dot(jnp.where(ej < ei, 1.0, 0.0).astype(BF16), units_row)[:, 0:1] * SEG_ALIGN
    seg_rows = seg_units * SEG_ALIGN
    slot = seg_off + rank
    for k in range(TOP_K):
        w8_ref[k:k + 1, :] = wts[k] / wsum * ROUTED_SCALE
        ls8_ref[k:k + 1, :] = (jnp.sum(jnp.where(eid == picks[k], slot, 0.0), axis=0, keepdims=True)
                               * ROW_SUB).astype(jnp.int32)
    lane = lax.broadcasted_iota(jnp.int32, (N_EXPERTS, LANES), 1)
    seg_ref[0] = jnp.where(lane == 0, cnt_ref[...], jnp.where(lane == 1, seg_rows, seg_off))
    cnt_ref[...] = cnt_ref[...] + seg_rows


def _merge_kernel(x_ref, ya_ref, on_ref, go_ref, ga_ref, gb_ref, g1_ref, sc_ref, sh_ref, n2_ref,
                  pa_ref, pb_ref, wo_ref, wr_ref, br_ref, pt_ref,
                  x1_ref, h2_ref, ls8_ref, w8_ref, seg_ref, cnt_ref, fold_a, fold_b):
    @pl.when(pl.program_id(0) == 0)
    def _():
        cnt_ref[...] = jnp.zeros_like(cnt_ref)

    go = go_ref[...].astype(F32)
    on = _dot(pt_ref[...], on_ref[0].reshape(x_ref.shape[0], HG_WIDTH))
    y_b = (on * (go * _sigmoid(go))).astype(BF16)
    y_a = _unfold_rows(ya_ref[...].astype(F32), fold_a, fold_b).astype(BF16)
    pa = _dot(y_a, pa_ref[...])
    pb = _dot(y_b, pb_ref[...])
    merged = _sigmoid(ga_ref[...].astype(F32)) * pa + _sigmoid(gb_ref[...].astype(F32)) * pb
    x1 = x_ref[...] + g1_ref[0] * _dot(merged.astype(BF16), wo_ref[...])
    x1_ref[...] = x1
    y = x1 * lax.rsqrt(jnp.mean(x1 * x1, axis=-1, keepdims=True) + EPS) * n2_ref[...]
    h2 = y * (1.0 + sc_ref[0]) + sh_ref[0]
    _to_token_rows(h2_ref, h2)
    _route(h2.astype(BF16), wr_ref, br_ref, cnt_ref, ls8_ref, w8_ref, seg_ref)


def _merge(x2d, ya, on, go, ga, gb, g1, sc2, sh2, n2g, pa, pb, wo, wr_t, br, rows_per_batch, tm):
    n, d = x2d.shape
    per = rows_per_batch // tm
    row = lambda wd: pl.BlockSpec((tm, wd), lambda i: (i, 0))
    mod = pl.BlockSpec((1, 1, d), lambda i: (i // per, 0, 0))
    full = lambda a: pl.BlockSpec(a.shape, lambda i: (0, 0))
    tok = pl.BlockSpec((TOP_K, tm), lambda i: (0, i))
    return pl.pallas_call(
        _merge_kernel,
        out_shape=[jax.ShapeDtypeStruct((n, d), F32), jax.ShapeDtypeStruct((n * ROW_SUB, LANES), U32),
                   jax.ShapeDtypeStruct((TOP_K, n), jnp.int32), jax.ShapeDtypeStruct((TOP_K, n), F32),
                   jax.ShapeDtypeStruct((n // tm, N_EXPERTS, LANES), F32),
                   jax.ShapeDtypeStruct((N_EXPERTS, 1), F32)],
        grid=(n // tm,),
        in_specs=[row(d), pl.BlockSpec((tm // S5_T, S5_T * S5_WIDTH), lambda i: (i, 0)),
                  pl.BlockSpec((1, GRID_W, tm // GRID_W, HG_WIDTH), lambda i: (i // per, 0, i % per, 0)),
                  row(HG_WIDTH), row(d), row(d), mod, mod, mod,
                  pl.BlockSpec((1, d), lambda i: (0, 0)), full(pa), full(pb), full(wo), full(wr_t), full(br),
                  pl.BlockSpec((tm, tm), lambda i: (0, 0))],
        out_specs=[row(d), pl.BlockSpec((tm * ROW_SUB, LANES), lambda i: (i, 0)), tok, tok,
                   pl.BlockSpec((1, N_EXPERTS, LANES), lambda i: (i, 0, 0)),
                   pl.BlockSpec((N_EXPERTS, 1), lambda i: (0, 0))],
        scratch_shapes=[pltpu.VMEM((tm, LANES), F32), pltpu.VMEM((tm, LANES), F32)],
        compiler_params=_params("arbitrary"),
        name="merge_out_proj_route",
    )(x2d, ya, on, go, ga, gb, g1, sc2, sh2, n2g.reshape(1, d), pa, pb, wo, wr_t, br,
      _grid_transpose_matrix(tm).T)


MOE_TILE = TOK_TILE
TOKEN_UNROLL = 4
SEG_ALIGN = 8
FILL_ROWS = 512
STAGE_ROWS = MOE_TILE * TOP_K + FILL_ROWS
MOE_BLK = 1024


def _token_row(ref, r):
    return ref.at[pl.ds(pl.multiple_of(r * ROW_SUB, ROW_SUB), ROW_SUB)]


def _wait_rows(any_ref, sem, n_rows):
    view = any_ref.at[pl.ds(0, n_rows * ROW_SUB)]
    pltpu.make_async_copy(view, view, sem).wait()


def _rows(ref, r0, n):
    return ref.at[pl.ds(pl.multiple_of(r0 * ROW_SUB, ROW_SUB), n * ROW_SUB)]


def _pow2_pieces(n, max_piece, fn, min_piece=1):
    done = 0
    piece = max_piece
    while piece >= min_piece:
        hit = (n & piece) != 0
        pl.when(hit)(functools.partial(fn, done, piece))
        done = done + (n & piece)
        piece //= 2


def _copy_rows(src_ref, src0, dst_ref, dst0, n, max_piece, sem, min_piece=1):
    def piece(off, size):
        pltpu.make_async_copy(_rows(src_ref, src0 + off, size), _rows(dst_ref, dst0 + off, size), sem).start()
    _pow2_pieces(n, max_piece, piece, min_piece)


def _wait_copied_rows(src_ref, dst_ref, n, max_piece, sem):
    def piece(off, size):
        pltpu.make_async_copy(_rows(src_ref, 0, size), _rows(dst_ref, 0, size), sem).wait()
    _pow2_pieces(n, max_piece, piece)


def _copy_tile_segments(i, src_ref, src_tab, dst_ref, dst_tab, cnt_ref, off_ref, fill_src0, fill_dst0, sem):
    def per_expert(e, carry):
        _copy_rows(src_ref, src_tab[i, e], dst_ref, dst_tab[i, e], cnt_ref[i, e], MOE_TILE, sem, SEG_ALIGN)
        return carry

    lax.fori_loop(0, N_EXPERTS, per_expert, 0)
    used = off_ref[i, N_EXPERTS - 1] + cnt_ref[i, N_EXPERTS - 1]
    _copy_rows(src_ref, fill_src0(used), dst_ref, fill_dst0(used), STAGE_ROWS - used, FILL_ROWS, sem, SEG_ALIGN)


def _dispatch_kernel(gs_ref, cnt_ref, off_ref, pad_ref, ls_ref, h2_ref, xs_hbm, ls_smem, stage0, stage1, zbuf,
                     sem0, sem1, lsem, zsem, *, tm, n_blocks):
    i = pl.program_id(0)
    last = pl.num_programs(0) - 1
    cp = pltpu.make_async_copy(ls_ref, ls_smem, lsem)
    cp.start()
    trash0 = n_blocks * MOE_BLK

    @pl.when(i == 0)
    def _():
        stage0[...] = jnp.zeros_like(stage0)
        stage1[...] = jnp.zeros_like(stage1)
        zbuf[...] = jnp.zeros_like(zbuf)
        cpz = pltpu.make_async_copy(zbuf, _rows(xs_hbm, trash0, 2 * FILL_ROWS), zsem)
        cpz.start()
        cpz.wait()

    cp.wait()

    def tile(stage, sem, prev_sem, trash):
        def body(tu, carry):
            for u in range(TOKEN_UNROLL):
                t = tu * TOKEN_UNROLL + u
                row = h2_ref[pl.ds(pl.multiple_of(t * ROW_SUB, ROW_SUB), ROW_SUB), :]
                for k in range(TOP_K):
                    stage[pl.ds(pl.multiple_of(ls_smem[t * TOP_K + k], ROW_SUB), ROW_SUB), :] = row
            return carry

        lax.fori_loop(0, tm // TOKEN_UNROLL, body, 0)
        _copy_tile_segments(i, stage, off_ref, xs_hbm, gs_ref, cnt_ref, off_ref,
                            lambda used: used, lambda used: trash, sem)

        @pl.when(i > 0)
        def _():
            _wait_rows(xs_hbm, prev_sem, STAGE_ROWS)

        @pl.when(i == last)
        def _():
            _wait_rows(xs_hbm, sem, STAGE_ROWS)

    pl.when(i % 2 == 0)(functools.partial(tile, stage0, sem0, sem1, trash0))
    pl.when(i % 2 == 1)(functools.partial(tile, stage1, sem1, sem0, trash0 + FILL_ROWS))

    def zero_pad(e, carry):
        _copy_rows(zbuf, 0, xs_hbm, pad_ref[0, e], pad_ref[1, e], MOE_BLK // 2, zsem)
        return carry

    def wait_pad(e, carry):
        _wait_copied_rows(zbuf, xs_hbm, pad_ref[1, e], MOE_BLK // 2, zsem)
        return carry

    def zero_block(j, carry):
        pltpu.make_async_copy(zbuf, _rows(xs_hbm, j * MOE_BLK, MOE_BLK), zsem).start()
        return carry

    def wait_block(j, carry):
        pltpu.make_async_copy(zbuf, _rows(xs_hbm, 0, MOE_BLK), zsem).wait()
        return carry

    @pl.when(i == 0)
    def _():
        lax.fori_loop(0, N_EXPERTS, zero_pad, 0)
        lax.fori_loop(pad_ref[2, 0], n_blocks, zero_block, 0)

    @pl.when(i == last)
    def _():
        lax.fori_loop(0, N_EXPERTS, wait_pad, 0)
        lax.fori_loop(pad_ref[2, 0], n_blocks, wait_block, 0)


def _dispatch(gstart, seg_cnt, seg_off, pad, ls8, h2_rows, n_blocks, tm):
    n = ls8.shape[0] // TOP_K
    cap = n_blocks * MOE_BLK + 2 * FILL_ROWS
    return pl.pallas_call(
        functools.partial(_dispatch_kernel, tm=tm, n_blocks=n_blocks),
        out_shape=jax.ShapeDtypeStruct((cap * ROW_SUB, LANES), U32),
        grid_spec=pltpu.PrefetchScalarGridSpec(
            num_scalar_prefetch=4,
            grid=(n // tm,),
            in_specs=[pl.BlockSpec((tm * TOP_K,), lambda i, *_: (i,)),
                      pl.BlockSpec((tm * ROW_SUB, LANES), lambda i, *_: (i, 0))],
            out_specs=pl.BlockSpec(memory_space=pl.ANY),
            scratch_shapes=[pltpu.SMEM((tm * TOP_K,), jnp.int32),
                            pltpu.VMEM((STAGE_ROWS * ROW_SUB, LANES), U32),
                            pltpu.VMEM((STAGE_ROWS * ROW_SUB, LANES), U32),
                            pltpu.VMEM((MOE_BLK * ROW_SUB, LANES), U32),
                            pltpu.SemaphoreType.DMA, pltpu.SemaphoreType.DMA, pltpu.SemaphoreType.DMA,
                            pltpu.SemaphoreType.DMA]),
        compiler_params=pltpu.CompilerParams(dimension_semantics=("arbitrary",), vmem_limit_bytes=VMEM_LIMIT,
                                             has_side_effects=True),
        name="moe_dispatch",
    )(gstart, seg_cnt, seg_off, pad, ls8, h2_rows)


def _expert_kernel(be_ref, nu_ref, x_ref, w1_ref, w3_ref, w2_ref, o_ref, w1b, w3b, w2b):
    j = pl.program_id(0)
    e = be_ref[j]
    prev = be_ref[jnp.maximum(j - 1, 0)]
    used = j < nu_ref[0]

    @pl.when(jnp.logical_and(used, jnp.logical_or(j == 0, e != prev)))
    def _():
        w1b[...] = w1_ref[0].astype(BF16)
        w3b[...] = w3_ref[0].astype(BF16)
        w2b[...] = w2_ref[0].astype(BF16)

    @pl.when(used)
    def _():
        x = _from_token_rows(x_ref, MOE_BLK).astype(BF16)
        a = _dot(x, w1b[...])
        hid = (a * _sigmoid(a)) * _dot(x, w3b[...])
        _to_token_rows(o_ref, _dot(hid.astype(BF16), w2b[...]))

    @pl.when(jnp.logical_not(used))
    def _():
        o_ref[...] = jnp.zeros_like(o_ref)


def _experts(block_e, n_used, xs, w1, w3, w2):
    n_blocks = block_e.shape[0]
    d, f = w1.shape[1], w1.shape[2]
    rows = pl.BlockSpec((MOE_BLK * ROW_SUB, LANES), lambda j, be, nu: (j, 0))
    rows_in = pl.BlockSpec((MOE_BLK * ROW_SUB, LANES), lambda j, be, nu: (jnp.minimum(j, nu[0] - 1), 0))
    return pl.pallas_call(
        _expert_kernel,
        out_shape=jax.ShapeDtypeStruct((n_blocks * MOE_BLK * ROW_SUB, LANES), U32),
        grid_spec=pltpu.PrefetchScalarGridSpec(
            num_scalar_prefetch=2,
            grid=(n_blocks,),
            in_specs=[rows_in,
                      pl.BlockSpec((1, d, f), lambda j, be, nu: (be[j], 0, 0)),
                      pl.BlockSpec((1, d, f), lambda j, be, nu: (be[j], 0, 0)),
                      pl.BlockSpec((1, f, d), lambda j, be, nu: (be[j], 0, 0))],
            out_specs=rows,
            scratch_shapes=[pltpu.VMEM((d, f), BF16), pltpu.VMEM((d, f), BF16), pltpu.VMEM((f, d), BF16)]),
        compiler_params=_params("arbitrary"),
        name="moe_experts",
    )(block_e, n_used, xs, w1, w3, w2)


def _combine_kernel(gs_ref, cnt_ref, off_ref, ls_ref, w8_ref, x1_ref, h2_ref, g2_ref, ws1_ref, ws3_ref, ws2_ref,
                    fg_ref, ys_hbm, o_ref, ls_smem, w_smem, gbuf0, gbuf1, acc_rows, sem0, sem1, lsem, *, tm):
    i = pl.program_id(0)
    last = pl.num_programs(0) - 1
    cp1 = pltpu.make_async_copy(ls_ref, ls_smem, lsem)
    cp2 = pltpu.make_async_copy(w8_ref, w_smem, lsem)
    cp1.start()
    cp2.start()

    def fetch(tile, gbuf, sem):
        _copy_tile_segments(tile, ys_hbm, gs_ref, gbuf, off_ref, cnt_ref, off_ref,
                            lambda used: 0, lambda used: used, sem)

    @pl.when(i == 0)
    def _():
        fetch(0, gbuf0, sem0)

    @pl.when(jnp.logical_and(i < last, i % 2 == 0))
    def _():
        fetch(i + 1, gbuf1, sem1)

    @pl.when(jnp.logical_and(i < last, i % 2 == 1))
    def _():
        fetch(i + 1, gbuf0, sem0)

    h2 = _from_token_rows(h2_ref, tm).astype(BF16)
    a = _dot(h2, ws1_ref[...])
    hid = (a * _sigmoid(a)) * _dot(h2, ws3_ref[...])
    acc = _dot(hid.astype(BF16), ws2_ref[...])
    cp1.wait()
    cp2.wait()

    def reduce_rows(gbuf, sem):
        _wait_rows(gbuf, sem, STAGE_ROWS)

        def body(tu, carry):
            for u in range(TOKEN_UNROLL):
                t = tu * TOKEN_UNROLL + u
                lo = jnp.zeros((ROW_SUB, LANES), F32)
                hi = jnp.zeros((ROW_SUB, LANES), F32)
                for k in range(TOP_K):
                    w = w_smem[t * TOP_K + k]
                    words = gbuf[pl.ds(pl.multiple_of(ls_smem[t * TOP_K + k], ROW_SUB), ROW_SUB), :]
                    lo = lo + w * lax.bitcast_convert_type(words << 16, F32)
                    hi = hi + w * lax.bitcast_convert_type(words & jnp.uint32(0xFFFF0000), F32)
                acc_rows[pl.ds(pl.multiple_of(t * SUBLANES, SUBLANES), ROW_SUB), :] = lo
                acc_rows[pl.ds(pl.multiple_of(t * SUBLANES, SUBLANES) + ROW_SUB, ROW_SUB), :] = hi
            return carry

        lax.fori_loop(0, tm // TOKEN_UNROLL, body, 0)

    pl.when(i % 2 == 0)(functools.partial(reduce_rows, gbuf0, sem0))
    pl.when(i % 2 == 1)(functools.partial(reduce_rows, gbuf1, sem1))
    routed = jnp.concatenate([acc_rows[pl.ds(s, tm, stride=SUBLANES), :] for s in range(SUBLANES)], axis=-1)
    y = x1_ref[...] + g2_ref[0] * (acc + routed)
    o_ref[...] = y * lax.rsqrt(jnp.mean(y * y, axis=-1, keepdims=True) + EPS) * fg_ref[...]


def _combine(gstart, seg_cnt, seg_off, ls8, w8, x1, h2_rows, g2, ws1, ws3, ws2, fg, ys, rows_per_batch, tm):
    n, d = x1.shape
    per = rows_per_batch // tm
    tok = pl.BlockSpec((tm * TOP_K,), lambda i, *_: (i,))
    full = lambda a: pl.BlockSpec(a.shape, lambda i, *_: (0, 0))
    return pl.pallas_call(
        functools.partial(_combine_kernel, tm=tm),
        out_shape=jax.ShapeDtypeStruct((n, d), F32),
        grid_spec=pltpu.PrefetchScalarGridSpec(
            num_scalar_prefetch=3,
            grid=(n // tm,),
            in_specs=[tok, tok, pl.BlockSpec((tm, d), lambda i, *_: (i, 0)),
                      pl.BlockSpec((tm * ROW_SUB, LANES), lambda i, *_: (i, 0)),
                      pl.BlockSpec((1, 1, d), lambda i, *_: (i // per, 0, 0)),
                      full(ws1), full(ws3), full(ws2), pl.BlockSpec((1, d), lambda i, *_: (0, 0)),
                      pl.BlockSpec(memory_space=pl.ANY)],
            out_specs=pl.BlockSpec((tm, d), lambda i, *_: (i, 0)),
            scratch_shapes=[pltpu.SMEM((tm * TOP_K,), jnp.int32), pltpu.SMEM((tm * TOP_K,), F32),
                            pltpu.VMEM((STAGE_ROWS * ROW_SUB, LANES), U32),
                            pltpu.VMEM((STAGE_ROWS * ROW_SUB, LANES), U32),
                            pltpu.VMEM((tm * SUBLANES, LANES), F32), pltpu.SemaphoreType.DMA,
                            pltpu.SemaphoreType.DMA, pltpu.SemaphoreType.DMA]),
        compiler_params=_params("arbitrary"),
        name="moe_combine_final",
    )(gstart, seg_cnt, seg_off, ls8, w8, x1, h2_rows, g2, ws1, ws3, ws2, fg.reshape(1, d), ys)


def _moe_plan(seg, counts, n_assign):
    cnt = counts.reshape(N_EXPERTS).astype(jnp.int32)
    padded = (cnt + MOE_BLK - 1) // MOE_BLK * MOE_BLK
    pends = jnp.cumsum(padded)
    pstarts = pends - padded
    max_rows = n_assign + seg.shape[0] * N_EXPERTS * (SEG_ALIGN - 1)
    n_blocks = (max_rows + N_EXPERTS * (MOE_BLK - 1) + MOE_BLK - 1) // MOE_BLK
    seg = seg[:, :, :3].astype(jnp.int32)
    gstart = pstarts[None, :] + seg[:, :, 0]
    blk_start = jnp.arange(n_blocks, dtype=jnp.int32) * MOE_BLK
    block_e = jnp.minimum(jnp.sum((blk_start[:, None] >= pends[None, :]).astype(jnp.int32), axis=1),
                          N_EXPERTS - 1).astype(jnp.int32)
    n_used = (pends[-1:] // MOE_BLK).astype(jnp.int32)
    pad = jnp.stack([pstarts + cnt, padded - cnt, jnp.broadcast_to(n_used, (N_EXPERTS,))], axis=0).astype(jnp.int32)
    return gstart, seg[:, :, 1], seg[:, :, 2], pad, block_e, n_used, n_blocks


def _mixer(x, c, ctx, c_ctx, w_ada, b_ada, norm1_g, norm2_g, w_in, s5_lam_re, s5_lam_im, s5_log_dt,
           s5_b_re, s5_b_im, s5_c_re, s5_c_im, s5_d, s5_w_glu, lb, hg_norm_g, p_a, p_b, w_out,
           moe_w_router, moe_b_router):
    b, l, d = x.shape
    lc = ctx.shape[1]
    n = b * l
    rows = l // GRID_W

    c8 = jnp.concatenate([c, c_ctx[None], jnp.zeros((8 - b - 1, d), F32)], axis=0)
    mod = _ada(c8, w_ada, b_ada)
    sh1, sc1, g1, sh2, sc2, g2 = [mod[:b, k * d:(k + 1) * d].reshape(b, 1, d) for k in range(6)]
    csh1, csc1 = mod[b:b + 1, 0:d].reshape(1, 1, d), mod[b:b + 1, d:2 * d].reshape(1, 1, d)

    w_in_b = w_in.astype(BF16)
    z = dict(zip([p[0] for p in _IN_PIECES],
                 _inproj(x.reshape(n, d), sc1, sh1, norm1_g, w_in_b, l, TOK_TILE, True)))
    zc = dict(zip([p[0] for p in _IN_PIECES],
                  _inproj(ctx.reshape(b * lc, d), csc1, csh1, norm1_g, w_in_b, lc, lc, False)))

    cx = lambda t: t.reshape(b, lc, HG_WIDTH)
    lb_row = lb.reshape(1, HG_WIDTH)
    o_f = _hgrn_pass(z["q"], z["ff"], z["i"], cx(zc["ff"]), cx(zc["i"]), lb_row, None, None, reverse=False)
    o_n = _hgrn_pass(z["q"], z["fb"], z["i"], cx(zc["fb"]), cx(zc["i"]), lb_row, o_f,
                     hg_norm_g.reshape(1, HG_DK), reverse=True)

    d_lag, w_s5_in, w_out_f, w_out_b, decay = _s5_weights(s5_lam_re, s5_lam_im, s5_log_dt, s5_b_re, s5_b_im,
                                                          s5_c_re, s5_c_im)
    kc, kl = lc // S5_T, l // S5_T
    u_lat = z["u"].reshape(b, kl, S5_T * S5_WIDTH)
    u_ctx = zc["u"].reshape(b, kc, S5_T * S5_WIDTH)
    rows_in = kl + kc
    u_ext = jnp.concatenate([u_lat, u_ctx], axis=1).reshape(b * rows_in, S5_T * S5_WIDTH)
    e = _s5_in(u_ext, d_lag, w_s5_in, (b * rows_in) // 2)
    states = _s5_scan(e, decay, b, rows_in, kl)
    d_row = s5_d.astype(F32).reshape(1, S5_WIDTH)
    y_a = _s5_out(states, w_out_f, w_out_b, e.reshape(b, rows_in, -1), z["u"], d_row, s5_w_glu.astype(BF16))

    return _merge(x.reshape(n, d), y_a, o_n, z["go"], z["ga"], z["gb"], g1, sc2, sh2, norm2_g,
                  p_a.astype(BF16), p_b.astype(BF16), w_out.astype(BF16),
                  moe_w_router.T.astype(BF16), moe_b_router.astype(F32).reshape(N_EXPERTS, 1), l, MOE_TILE) + (g2,)


def kernel(x, c, ctx, c_ctx, w_ada, b_ada, norm1_g, norm2_g, w_in, s5_lam_re, s5_lam_im, s5_log_dt, s5_b_re,
           s5_b_im, s5_c_re, s5_c_im, s5_d, s5_w_glu, hg_lb_logits, hg_norm_g, p_a, p_b, w_out, moe_w_router,
           moe_b_router, moe_w1, moe_w3, moe_w2, moe_ws1, moe_ws3, moe_ws2, final_norm_g):
    b, l, d = x.shape
    n = b * l
    assert w_ada.shape[0] == 1, "single-layer block"
    lb = jnp.cumsum(jax.nn.softmax(hg_lb_logits.astype(F32), axis=0), axis=0)[0]
    x1, h2_rows, ls8, w8, seg, counts, g2 = _mixer(
        x, c, ctx, c_ctx, w_ada[0], b_ada[0], norm1_g[0], norm2_g[0], w_in[0], s5_lam_re[0], s5_lam_im[0],
        s5_log_dt[0], s5_b_re[0], s5_b_im[0], s5_c_re[0], s5_c_im[0], s5_d[0], s5_w_glu[0], lb, hg_norm_g[0],
        p_a[0], p_b[0], w_out[0], moe_w_router[0], moe_b_router[0])
    gstart, seg_cnt, seg_off, pad, block_e, n_used, n_blocks = _moe_plan(seg, counts, n * TOP_K)
    ls_flat, w_flat = ls8.T.reshape(n * TOP_K), w8.T.reshape(n * TOP_K)
    xs = _dispatch(gstart, seg_cnt, seg_off, pad, ls_flat, h2_rows, n_blocks, MOE_TILE)
    ys = _experts(block_e, n_used, xs, moe_w1[0], moe_w3[0], moe_w2[0])
    out = _combine(gstart, seg_cnt, seg_off, ls_flat, w_flat, x1, h2_rows, g2, moe_ws1[0].astype(BF16),
                   moe_ws3[0].astype(BF16), moe_ws2[0].astype(BF16), final_norm_g, ys, l, MOE_TILE)
    return out.reshape(b, l, d)
```

```python
import functools
import math

import jax
import jax.numpy as jnp
from jax import lax
from jax.experimental import pallas as pl
from jax.experimental.pallas import tpu as pltpu

F32 = jnp.float32
BF16 = jnp.bfloat16

GRID_W = 64
S5_WIDTH = 256
S5_GROUP = 16
S5_GROUPS = 16
S5_STATE = 64
HG_HEADS = 6
HG_DK = 128
HG_WIDTH = HG_HEADS * HG_DK
N_EXPERTS = 64
ROUTE_GROUPS = 8
TOPK_GROUPS = 4
TOP_K = 8
ROUTED_SCALE = 2.5
EPS = 1e-6

LANES = 128
SUBLANES = 8

TOK_TILE = 512
S5_T = 16
HG_CHUNK = 64
HG_BATCH = 4
VMEM_LIMIT = 56 * 1024 * 1024

_NT = (((1,), (1,)), ((), ()))
_TN = (((0,), (0,)), ((), ()))


def _params(*sem):
    return pltpu.CompilerParams(dimension_semantics=sem, vmem_limit_bytes=VMEM_LIMIT)


def _dot(a, b):
    return jnp.dot(a, b, preferred_element_type=F32)


def _sigmoid(x):
    return 0.5 * jnp.tanh(0.5 * x) + 0.5


def _ada_kernel(c_ref, w_ref, b_ref, o_ref):
    c = c_ref[...]
    s = (c * _sigmoid(c)).astype(BF16)
    o_ref[...] = _dot(s, w_ref[...].astype(BF16)) + b_ref[...]


def _ada(c8, w_ada, b_ada):
    d, n = w_ada.shape
    tn = 1536
    return pl.pallas_call(
        _ada_kernel,
        out_shape=jax.ShapeDtypeStruct((8, n), F32),
        grid=(n // tn,),
        in_specs=[pl.BlockSpec((8, d), lambda j: (0, 0)),
                  pl.BlockSpec((d, tn), lambda j: (0, j)),
                  pl.BlockSpec((1, tn), lambda j: (0, j))],
        out_specs=pl.BlockSpec((8, tn), lambda j: (0, j)),
        compiler_params=_params("arbitrary"),
        name="ada_mod",
    )(c8, w_ada, b_ada.reshape(1, n))


_IN_PIECES = (("u", 0, 256, BF16), ("q", 256, 768, BF16), ("ff", 1024, 768, BF16),
              ("fb", 1792, 768, BF16), ("i", 2560, 768, BF16), ("go", 3328, 768, BF16),
              ("ga", 4096, 1024, BF16), ("gb", 5120, 1024, BF16))


def _fold_rows(val, buf_a, buf_b):
    t = val.shape[0]
    buf_a[...] = val[:, :LANES]
    buf_b[...] = val[:, LANES:]
    pieces = []
    for s in range(S5_T):
        pieces += [buf_a[pl.ds(s, t // S5_T, stride=S5_T), :], buf_b[pl.ds(s, t // S5_T, stride=S5_T), :]]
    return jnp.concatenate(pieces, axis=-1)


def _unfold_rows(val, buf_a, buf_b):
    r = val.shape[0]
    for s in range(S5_T):
        buf_a[pl.ds(s, r, stride=S5_T), :] = val[:, s * S5_WIDTH:s * S5_WIDTH + LANES]
        buf_b[pl.ds(s, r, stride=S5_T), :] = val[:, s * S5_WIDTH + LANES:(s + 1) * S5_WIDTH]
    return jnp.concatenate([buf_a[...], buf_b[...]], axis=-1)


def _grid_transpose_matrix(tm):
    i = jnp.arange(tm)
    src = (i % (tm // GRID_W)) * GRID_W + i // (tm // GRID_W)
    return (src[:, None] == jnp.arange(tm)[None, :]).astype(BF16)


def _inproj_kernel(x_ref, sc_ref, sh_ref, g_ref, w_ref, p_ref, *o_refs):
    o_refs, (fold_a, fold_b) = o_refs[:len(_IN_PIECES)], o_refs[len(_IN_PIECES):]
    x = x_ref[...]
    y = x * lax.rsqrt(jnp.mean(x * x, axis=-1, keepdims=True) + EPS) * g_ref[...]
    h = (y * (1.0 + sc_ref[0]) + sh_ref[0]).astype(BF16)
    h_cm = None
    for (name, a, wd, _), o_ref in zip(_IN_PIECES, o_refs):
        if name == "u":
            o_ref[...] = _fold_rows(_dot(h, w_ref[:, a:a + wd]), fold_a, fold_b).astype(o_ref.dtype)
        elif len(o_ref.shape) == 2:
            o_ref[...] = _dot(h, w_ref[:, a:a + wd]).astype(o_ref.dtype)
        else:
            if h_cm is None:
                h_cm = _dot(p_ref[...], h).astype(BF16)
            o_ref[0] = _dot(h_cm, w_ref[:, a:a + wd]).astype(o_ref.dtype).reshape(o_ref.shape[1:])


_COLMAJOR_PIECES = ("q", "ff", "fb", "i")


def _inproj(x2d, sc, sh, g, w_bf16, rows_per_mod, tm, colmajor):
    n, d = x2d.shape
    per = rows_per_mod // tm
    mod_map = (lambda i: (i // per, 0, 0)) if sc.shape[0] > 1 else (lambda i: (0, 0, 0))
    shapes, specs = [], []
    for name, _, wd, dt in _IN_PIECES:
        if colmajor and name in _COLMAJOR_PIECES:
            shapes.append(jax.ShapeDtypeStruct((n // rows_per_mod, GRID_W, rows_per_mod // GRID_W, wd), dt))
            specs.append(pl.BlockSpec((1, GRID_W, tm // GRID_W, wd), lambda i: (i // per, 0, i % per, 0)))
        elif name == "u":
            shapes.append(jax.ShapeDtypeStruct((n // S5_T, S5_T * wd), dt))
            specs.append(pl.BlockSpec((tm // S5_T, S5_T * wd), lambda i: (i, 0)))
        else:
            shapes.append(jax.ShapeDtypeStruct((n, wd), dt))
            specs.append(pl.BlockSpec((tm, wd), lambda i: (i, 0)))
    return pl.pallas_call(
        _inproj_kernel,
        out_shape=shapes,
        grid=(n // tm,),
        in_specs=[pl.BlockSpec((tm, d), lambda i: (i, 0)),
                  pl.BlockSpec((1, 1, d), mod_map),
                  pl.BlockSpec((1, 1, d), mod_map),
                  pl.BlockSpec((1, d), lambda i: (0, 0)),
                  pl.BlockSpec(w_bf16.shape, lambda i: (0, 0)),
                  pl.BlockSpec((tm, tm), lambda i: (0, 0))],
        out_specs=specs,
        scratch_shapes=[pltpu.VMEM((tm, LANES), F32), pltpu.VMEM((tm, LANES), F32)],
        compiler_params=_params("arbitrary"),
        name="in_proj",
    )(x2d, sc, sh, g.reshape(1, d), w_bf16, _grid_transpose_matrix(tm))


def _hgrn_gates(zf, lb):
    sig = _sigmoid(zf)
    logf = jnp.log(lb + (1.0 - lb) * sig)
    k = (1.0 - lb) * (1.0 - sig)
    return logf, k


def _chunk_cumsum(cs, logf):
    hi = logf.astype(BF16)
    lo = (logf - hi.astype(F32)).astype(BF16)
    return _dot(cs, hi) + _dot(cs, lo)


def _hgrn_state_step(zf, v, lb, st, cs, reverse):
    logf, k = _hgrn_gates(zf, lb)
    cum = _chunk_cumsum(cs, logf)
    t = 0 if reverse else HG_CHUNK - 1
    total = cum[t:t + 1, :]
    kdec = (k * jnp.exp(total - cum)).astype(BF16)
    st_new = st * jnp.exp(total) + lax.dot_general(v.astype(BF16), kdec, _TN, preferred_element_type=F32)
    return cum, k, st_new


def _hgrn_kernel(*refs, reverse, final, n_ctx_chunks):
    if final:
        q_all, f_all, v_all, cf_ref, cv_ref, lb_ref, of_all, g_ref, o_all, st_ref = refs
    else:
        q_all, f_all, v_all, cf_ref, cv_ref, lb_ref, o_all, st_ref = refs
        of_all = None
    n_batch = q_all.shape[0]
    c_len = HG_CHUNK
    n_rows = q_all.shape[2]
    n_chunks = n_rows // c_len
    row = lax.broadcasted_iota(jnp.int32, (n_rows, n_rows), 0)
    col = lax.broadcasted_iota(jnp.int32, (n_rows, n_rows), 1)
    tri = (col >= row) if reverse else (col <= row)
    same_chunk = None
    for c in range(n_chunks):
        lo, hi = c * c_len, (c + 1) * c_len
        blk = (row >= lo) & (row < hi) & (col >= lo) & (col < hi)
        same_chunk = blk if same_chunk is None else (same_chunk | blk)
    mask = tri & same_chunk
    cs = jnp.where(mask, 1.0, 0.0).astype(BF16)

    @pl.when(pl.program_id(1) == 0)
    def _():
        cs1 = cs[:c_len, :c_len]
        order = range(n_ctx_chunks - 1, -1, -1) if reverse else range(n_ctx_chunks)
        for bi in range(n_batch):
            for h in range(HG_HEADS):
                cols = slice(h * HG_DK, (h + 1) * HG_DK)
                st = jnp.zeros((HG_DK, HG_DK), F32)
                for c in order:
                    rows = slice(c * c_len, (c + 1) * c_len)
                    _, _, st = _hgrn_state_step(cf_ref[bi, rows, cols].astype(F32), cv_ref[bi, rows, cols].astype(F32),
                                                lb_ref[:, cols], st, cs1, reverse)
                st_ref[bi * HG_HEADS + h] = st

    def chunk_rows(x, r):
        return [x[c * c_len + r:c * c_len + r + 1, :] for c in range(n_chunks)]

    def over_chunks(rows):
        return jnp.concatenate([jnp.broadcast_to(r, (c_len, r.shape[1])) for r in rows], axis=0)

    lb = lb_ref[...]
    r_ref = c_len // 2 - 1 if reverse else c_len // 2
    r_tot = 0 if reverse else c_len - 1
    order = range(n_chunks - 1, -1, -1) if reverse else range(n_chunks)
    for bi in range(n_batch):
        q = q_all[bi, 0].astype(F32)
        v = v_all[bi, 0]
        logf, k = _hgrn_gates(f_all[bi, 0].astype(F32), lb)
        cum = _chunk_cumsum(cs, logf)
        ref_rows, tot_rows = chunk_rows(cum, r_ref), chunk_rows(cum, r_tot)
        ref = over_chunks(ref_rows)
        qe = q * jnp.exp(cum - ref)
        ke = k * jnp.exp(ref - cum)
        qi, ki = qe.astype(BF16), ke.astype(BF16)
        q_in = (qe * over_chunks([jnp.exp(r) for r in ref_rows])).astype(BF16)
        kdec = (ke * over_chunks([jnp.exp(t - r) for t, r in zip(tot_rows, ref_rows)])).astype(BF16)
        for h in range(HG_HEADS):
            cols = slice(h * HG_DK, (h + 1) * HG_DK)
            s = lax.dot_general(qi[:, cols], ki[:, cols], _NT, preferred_element_type=F32)
            o_intra = _dot(jnp.where(mask, s, 0.0).astype(BF16), v[:, cols])
            st = st_ref[bi * HG_HEADS + h]
            for c in order:
                rows = slice(c * c_len, (c + 1) * c_len)
                o = o_intra[rows] + lax.dot_general(q_in[rows, cols], st.astype(BF16), _NT,
                                                    preferred_element_type=F32)
                total = cum[c * c_len + r_tot:c * c_len + r_tot + 1, cols]
                st = st * jnp.exp(total) + lax.dot_general(v[rows, cols], kdec[rows, cols], _TN,
                                                           preferred_element_type=F32)
                if final:
                    o = o + of_all[bi, 0, rows, cols].astype(F32)
                    o = o * lax.rsqrt(jnp.mean(o * o, axis=-1, keepdims=True) + EPS) * g_ref[...]
                o_all[bi, 0, rows, cols] = o.astype(o_all.dtype)
            st_ref[bi * HG_HEADS + h] = st


def _hgrn_pass(q, f, v, cf, cv, lb, o_prev, g, *, reverse):
    b, nw, rows, _ = q.shape
    nb = HG_BATCH if b % HG_BATCH == 0 else 1
    final = o_prev is not None
    wmap = (lambda bi, w: (bi, nw - 1 - w, 0, 0)) if reverse else (lambda bi, w: (bi, w, 0, 0))
    blk = pl.BlockSpec((nb, 1, rows, HG_WIDTH), wmap)
    cblk = pl.BlockSpec((nb, cf.shape[1], HG_WIDTH), lambda bi, w: (bi, 0, 0))
    in_specs = [blk, blk, blk, cblk, cblk, pl.BlockSpec((1, HG_WIDTH), lambda bi, w: (0, 0))]
    args = [q, f, v, cf, cv, lb]
    if final:
        in_specs += [blk, pl.BlockSpec((1, HG_DK), lambda bi, w: (0, 0))]
        args += [o_prev, g]
    return pl.pallas_call(
        functools.partial(_hgrn_kernel, reverse=reverse, final=final, n_ctx_chunks=cf.shape[1] // HG_CHUNK),
        out_shape=jax.ShapeDtypeStruct(q.shape, BF16),
        grid=(b // nb, nw),
        in_specs=in_specs,
        out_specs=blk,
        scratch_shapes=[pltpu.VMEM((nb * HG_HEADS, HG_DK, HG_DK), F32)],
        compiler_params=_params("arbitrary", "arbitrary"),
        name="hgrn_bwd" if reverse else "hgrn_fwd",
    )(*args)


def _s5_weights(lam_re, lam_im, log_dt, b_re, b_im, c_re, c_im):
    g, p, cc, t = S5_GROUPS, S5_STATE, S5_GROUP, S5_T
    lre = jnp.minimum(lam_re.astype(F32), -1e-4)
    lim = lam_im.astype(F32)
    dt = jnp.exp(log_dt.astype(F32))[..., None]
    ks = jnp.arange(t + 1, dtype=F32)[:, None, None, None]
    mag = jnp.exp(ks * (lre * dt)[None])
    pw_re = mag * jnp.cos(ks * (lim * dt)[None])
    pw_im = mag * jnp.sin(ks * (lim * dt)[None])
    nr, ni = pw_re[1] - 1.0, pw_im[1]
    den = lre * lre + lim * lim
    cf_re = (nr * lre + ni * lim) / den
    cf_im = (ni * lre - nr * lim) / den
    bb_re = cf_re[..., None] * b_re - cf_im[..., None] * b_im
    bb_im = cf_re[..., None] * b_im + cf_im[..., None] * b_re
    cre, cim = c_re.astype(F32), c_im.astype(F32)
    sw, ns = S5_WIDTH, 2 * g * p
    grp_of_row = jnp.arange(sw)[:, None] // cc

    cp_re = cre[None, None, :, :, :] * pw_re[:t, :, :, None, :] - cim[None, None] * pw_im[:t, :, :, None, :]
    cp_im = cre[None, None, :, :, :] * pw_im[:t, :, :, None, :] + cim[None, None] * pw_re[:t, :, :, None, :]
    def contract_p(cp, bb):
        return jnp.sum(cp.transpose(4, 1, 0, 2, 3)[..., None] * bb.transpose(2, 0, 1, 3)[:, :, None, :, None, :],
                       axis=0)

    kk = contract_p(cp_re, bb_re) - contract_p(cp_im, bb_im)
    kf, kb = kk[0], kk[1]
    kall = jnp.concatenate([kb[:0:-1], (kf[0] + kb[0])[None], kf[1:]], axis=0)
    kt = kall.transpose(0, 1, 3, 2).reshape(2 * t - 1, sw, cc)

    def spread(x, period, reps):
        sel = (jnp.arange(period)[:, None] == (jnp.arange(period * reps)[None, :] % period)).astype(BF16)
        return jnp.dot(x.astype(BF16), sel, preferred_element_type=BF16)

    same = grp_of_row == (jnp.arange(sw)[None, :] // cc)
    d_lag = jnp.where(same[None], spread(kt, cc, g), 0)

    same_in = jnp.tile(grp_of_row, (t, 1)) == ((jnp.arange(ns)[None, :] % (g * p)) // p)

    def in_to_state(pre, pim, bre, bim):
        xre = pre[..., None] * bre[None] - pim[..., None] * bim[None]
        xim = pre[..., None] * bim[None] + pim[..., None] * bre[None]
        return [spread(xre.transpose(0, 1, 3, 2).reshape(t * sw, p), p, g),
                spread(xim.transpose(0, 1, 3, 2).reshape(t * sw, p), p, g)]

    w_in = jnp.concatenate(in_to_state(pw_re[t - 1::-1, 0], pw_im[t - 1::-1, 0], bb_re[0], bb_im[0])
                           + in_to_state(pw_re[:t, 1], pw_im[:t, 1], bb_re[1], bb_im[1]), axis=1)
    w_in = jnp.where(jnp.tile(same_in, (1, 2)), w_in, 0)

    same_out = ((jnp.arange(ns)[:, None] % (g * p)) // p) == ((jnp.arange(t * sw)[None, :] // cc) % g)
    col = jnp.arange(t * sw)
    pick = (jnp.arange(t * cc)[:, None] == ((col // sw) * cc + col % cc)[None, :]).astype(BF16)

    def state_to_out(pre, pim):
        are = cre[None] * pre[:, :, None, :] - cim[None] * pim[:, :, None, :]
        aim = cre[None] * pim[:, :, None, :] + cim[None] * pre[:, :, None, :]
        a = jnp.concatenate([are.transpose(1, 3, 0, 2), -aim.transpose(1, 3, 0, 2)], axis=0)
        a = jnp.dot(a.reshape(ns, t * cc).astype(BF16), pick, preferred_element_type=BF16)
        return jnp.where(same_out, a, 0)

    w_out_f = state_to_out(pw_re[1:, 0], pw_im[1:, 0])
    w_out_b = state_to_out(pw_re[t:0:-1, 1], pw_im[t:0:-1, 1])

    decay = jnp.stack([pw_re[t].reshape(2, g * p), pw_im[t].reshape(2, g * p)], axis=1)
    return d_lag, w_in, w_out_f, w_out_b, decay


def _s5_in_kernel(u_ref, d_ref, w_ref, o_ref):
    j = pl.program_id(1)

    @pl.when(j < S5_T)
    def _():
        acc = _dot(u_ref[:, 0:S5_WIDTH], d_ref[j + S5_T - 1])
        for s in range(1, S5_T):
            acc = acc + _dot(u_ref[:, s * S5_WIDTH:(s + 1) * S5_WIDTH], d_ref[j - s + S5_T - 1])
        o_ref[...] = acc

    @pl.when(j >= S5_T)
    def _():
        o_ref[...] = _dot(u_ref[...], w_ref[...])


def _s5_in(u, d_lag, w_in, tm):
    m, k = u.shape
    tn = S5_WIDTH
    nj = (k + w_in.shape[1]) // tn
    return pl.pallas_call(
        _s5_in_kernel,
        out_shape=jax.ShapeDtypeStruct((m, nj * tn), F32),
        grid=(m // tm, nj),
        in_specs=[pl.BlockSpec((tm, k), lambda i, j: (i, 0)),
                  pl.BlockSpec(d_lag.shape, lambda i, j: (0, 0, 0)),
                  pl.BlockSpec((k, tn), lambda i, j: (0, jnp.maximum(j - S5_T, 0)))],
        out_specs=pl.BlockSpec((tm, tn), lambda i, j: (i, j)),
        compiler_params=_params("arbitrary", "arbitrary"),
        name="s5_in",
    )(u, d_lag, w_in)


def _s5_scan_kernel(efr_ref, efi_ref, ebr_ref, ebi_ref, a_ref, hfr_ref, hfi_ref, hbr_ref, hbi_ref,
                    *, nb, rows_in, rows_out):
    dirs = ((efr_ref, efi_ref, hfr_ref, hfi_ref, a_ref[0, 0:1, :], a_ref[0, 1:2, :]),
            (ebr_ref, ebi_ref, hbr_ref, hbi_ref, a_ref[1, 0:1, :], a_ref[1, 1:2, :]))
    zero = jnp.zeros_like(dirs[0][4])

    def step(srcs, carry, store):
        new = []
        for di, (er_ref, ei_ref, hr_ref, hi_ref, are, aim) in enumerate(dirs):
            for bi in range(nb):
                hre, him = carry[2 * (di * nb + bi)], carry[2 * (di * nb + bi) + 1]
                if store:
                    hr_ref[pl.ds(bi * rows_out + srcs[di], 1), :] = hre
                    hi_ref[pl.ds(bi * rows_out + srcs[di], 1), :] = him
                ere = er_ref[pl.ds(bi * rows_in + srcs[di], 1), :]
                eim = ei_ref[pl.ds(bi * rows_in + srcs[di], 1), :]
                new += [are * hre - aim * him + ere, are * him + aim * hre + eim]
        return tuple(new)

    n_ctx = rows_in - rows_out
    carry = lax.fori_loop(0, n_ctx, lambda s, c: step((rows_out + s, rows_in - 1 - s), c, False),
                          tuple([zero] * (4 * nb)))
    lax.fori_loop(0, rows_out, lambda s, c: step((s, rows_out - 1 - s), c, True), carry)


def _s5_scan(e, decay, nb, rows_in, rows_out):
    tc = 256
    nsr = S5_GROUPS * S5_STATE
    c0 = (S5_T * S5_WIDTH) // tc
    nt = nsr // tc
    eblk = lambda k: pl.BlockSpec((nb * rows_in, tc), lambda j: (0, c0 + k * nt + j))
    hblk = pl.BlockSpec((nb * rows_out, tc), lambda j: (0, j))
    return pl.pallas_call(
        functools.partial(_s5_scan_kernel, nb=nb, rows_in=rows_in, rows_out=rows_out),
        out_shape=[jax.ShapeDtypeStruct((nb * rows_out, nsr), F32)] * 4,
        grid=(nt,),
        in_specs=[eblk(0), eblk(1), eblk(2), eblk(3), pl.BlockSpec((2, 2, tc), lambda j: (0, 0, j))],
        out_specs=[hblk] * 4,
        compiler_params=_params("arbitrary"),
        name="s5_scan",
    )(e, e, e, e, decay)


def _gelu_tanh(x):
    return 0.5 * x * (1.0 + jnp.tanh(math.sqrt(2.0 / math.pi) * (x + 0.044715 * x * x * x)))


def _s5_out_kernel(hfr_ref, hfi_ref, hbr_ref, hbi_ref, wf_ref, wb_ref, yi_ref, u_ref, d_ref, wg_ref, o_ref):
    nsr = hfr_ref.shape[1]
    y = yi_ref[0] + d_ref[...] * u_ref[...].astype(F32)
    for h_ref, w_ref, r0 in ((hfr_ref, wf_ref, 0), (hfi_ref, wf_ref, nsr), (hbr_ref, wb_ref, 0), (hbi_ref, wb_ref, nsr)):
        y = y + _dot(h_ref[...].astype(BF16), w_ref[r0:r0 + nsr, :])
    y = _gelu_tanh(y)
    gate = _sigmoid(_dot(y.astype(BF16), wg_ref[...]))
    o_ref[...] = (y * gate).astype(o_ref.dtype)


def _s5_out(states, w_out_f, w_out_b, e3, u_rows, d_row, w_glu):
    m, nsr = states[0].shape
    nb = e3.shape[0]
    tm = m // nb
    tn = S5_WIDTH
    st = pl.BlockSpec((tm, nsr), lambda i, j: (i, 0))
    wo = pl.BlockSpec((2 * nsr, tn), lambda i, j: (0, j))
    return pl.pallas_call(
        _s5_out_kernel,
        out_shape=jax.ShapeDtypeStruct((m, S5_T * S5_WIDTH), BF16),
        grid=(nb, S5_T),
        in_specs=[st, st, st, st, wo, wo,
                  pl.BlockSpec((1, tm, tn), lambda i, j: (i, 0, j)),
                  pl.BlockSpec((tm, tn), lambda i, j: (i, j)),
                  pl.BlockSpec((1, tn), lambda i, j: (0, 0)),
                  pl.BlockSpec((tn, tn), lambda i, j: (0, 0))],
        out_specs=pl.BlockSpec((tm, tn), lambda i, j: (i, j)),
        compiler_params=_params("arbitrary", "arbitrary"),
        name="s5_out",
    )(*states, w_out_f, w_out_b, e3, u_rows, d_row, w_glu)


U32 = jnp.uint32
ROW_SUB = 4


def _to_token_rows(ref, val):
    t, d = val.shape
    bits = lax.bitcast_convert_type(val.astype(BF16).astype(F32), U32)
    w = (bits[:, :d // 2] >> 16) | bits[:, d // 2:]
    for s in range(ROW_SUB):
        ref[pl.ds(s, t, stride=ROW_SUB), :] = w[:, s * LANES:(s + 1) * LANES]


def _from_token_rows(ref, t, row0=0):
    w = jnp.concatenate([ref[pl.ds(row0 * ROW_SUB + s, t, stride=ROW_SUB), :] for s in range(ROW_SUB)], axis=-1)
    lo = lax.bitcast_convert_type(w << 16, F32)
    hi = lax.bitcast_convert_type(w & jnp.uint32(0xFFFF0000), F32)
    return jnp.concatenate([lo, hi], axis=-1)


def _route(h2b, wr_ref, br_ref, cnt_ref, ls8_ref, w8_ref, seg_ref):
    tm = h2b.shape[0]
    per_group = N_EXPERTS // ROUTE_GROUPS
    scores = _sigmoid(lax.dot_general(wr_ref[...], h2b, _NT, preferred_element_type=F32))
    biased = scores + br_ref[...]
    neg = -jnp.inf
    sub = lax.broadcasted_iota(jnp.int32, (per_group, tm), 0)
    grp = []
    for gi in range(ROUTE_GROUPS):
        v = biased[gi * per_group:(gi + 1) * per_group, :]
        m1 = jnp.max(v, axis=0, keepdims=True)
        first = jnp.min(jnp.where(v == m1, sub, per_group), axis=0, keepdims=True)
        m2 = jnp.max(jnp.where(sub == first, neg, v), axis=0, keepdims=True)
        grp.append(m1 + m2)
    grp = jnp.concatenate(grp, axis=0)
    gid = lax.broadcasted_iota(jnp.int32, (ROUTE_GROUPS, tm), 0)
    beaten = jnp.zeros((ROUTE_GROUPS, tm), jnp.int32)
    for gj in range(ROUTE_GROUPS):
        r = grp[gj:gj + 1, :]
        beaten = beaten + jnp.where((r > grp) | ((r == grp) & (gj < gid)), 1, 0)
    group_ok = beaten < TOPK_GROUPS
    expert_ok = jnp.concatenate(
        [jnp.broadcast_to(group_ok[gi:gi + 1, :], (per_group, tm)) for gi in range(ROUTE_GROUPS)], axis=0)
    cur = jnp.where(expert_ok, biased, neg)
    eid = lax.broadcasted_iota(jnp.int32, (N_EXPERTS, tm), 0)
    sel = jnp.zeros((N_EXPERTS, tm), F32)
    picks, wts = [], []
    for _ in range(TOP_K):
        m = jnp.max(cur, axis=0, keepdims=True)
        idx = jnp.min(jnp.where(cur == m, eid, N_EXPERTS), axis=0, keepdims=True)
        hit = eid == idx
        picks.append(idx)
        wts.append(jnp.sum(jnp.where(hit, scores, 0.0), axis=0, keepdims=True))
        sel = jnp.where(hit, 1.0, sel)
        cur = jnp.where(hit, neg, cur)
    wsum = wts[0]
    for w in wts[1:]:
        wsum = wsum + w
    selb = sel.astype(BF16)
    ti = lax.broadcasted_iota(jnp.int32, (tm, tm), 0)
    tj = lax.broadcasted_iota(jnp.int32, (tm, tm), 1)
    rank = _dot(selb, jnp.where(ti < tj, 1.0, 0.0).astype(BF16))
    seg_units = jnp.ceil(jnp.sum(sel, axis=1, keepdims=True) * (1.0 / SEG_ALIGN))
    ei = lax.broadcasted_iota(jnp.int32, (N_EXPERTS, N_EXPERTS), 0)
    ej = lax.broadcasted_iota(jnp.int32, (N_EXPERTS, N_EXPERTS), 1)
    units_row = jnp.broadcast_to(seg_units, (N_EXPERTS, LANES)).astype(BF16)
    seg_off = _dot(jnp.where(ej < ei, 1.0, 0.0).astype(BF16), units_row)[:, 0:1] * SEG_ALIGN
    seg_rows = seg_units * SEG_ALIGN
    slot = seg_off + rank
    for k in range(TOP_K):
        w8_ref[k:k + 1, :] = wts[k] / wsum * ROUTED_SCALE
        ls8_ref[k:k + 1, :] = (jnp.sum(jnp.where(eid == picks[k], slot, 0.0), axis=0, keepdims=True)
                               * ROW_SUB).astype(jnp.int32)
    lane = lax.broadcasted_iota(jnp.int32, (N_EXPERTS, LANES), 1)
    seg_ref[0] = jnp.where(lane == 0, cnt_ref[...], jnp.where(lane == 1, seg_rows, seg_off))
    cnt_ref[...] = cnt_ref[...] + seg_rows


def _merge_kernel(x_ref, ya_ref, on_ref, go_ref, ga_ref, gb_ref, g1_ref, sc_ref, sh_ref, n2_ref,
                  pa_ref, pb_ref, wo_ref, wr_ref, br_ref, pt_ref,
                  x1_ref, h2_ref, ls8_ref, w8_ref, seg_ref, cnt_ref, fold_a, fold_b):
    @pl.when(pl.program_id(0) == 0)
    def _():
        cnt_ref[...] = jnp.zeros_like(cnt_ref)

    go = go_ref[...].astype(F32)
    on = _dot(pt_ref[...], on_ref[0].reshape(x_ref.shape[0], HG_WIDTH))
    y_b = (on * (go * _sigmoid(go))).astype(BF16)
    y_a = _unfold_rows(ya_ref[...].astype(F32), fold_a, fold_b).astype(BF16)
    pa = _dot(y_a, pa_ref[...])
    pb = _dot(y_b, pb_ref[...])
    merged = _sigmoid(ga_ref[...].astype(F32)) * pa + _sigmoid(gb_ref[...].astype(F32)) * pb
    x1 = x_ref[...] + g1_ref[0] * _dot(merged.astype(BF16), wo_ref[...])
    x1_ref[...] = x1
    y = x1 * lax.rsqrt(jnp.mean(x1 * x1, axis=-1, keepdims=True) + EPS) * n2_ref[...]
    h2 = y * (1.0 + sc_ref[0]) + sh_ref[0]
    _to_token_rows(h2_ref, h2)
    _route(h2.astype(BF16), wr_ref, br_ref, cnt_ref, ls8_ref, w8_ref, seg_ref)


def _merge(x2d, ya, on, go, ga, gb, g1, sc2, sh2, n2g, pa, pb, wo, wr_t, br, rows_per_batch, tm):
    n, d = x2d.shape
    per = rows_per_batch // tm
    row = lambda wd: pl.BlockSpec((tm, wd), lambda i: (i, 0))
    mod = pl.BlockSpec((1, 1, d), lambda i: (i // per, 0, 0))
    full = lambda a: pl.BlockSpec(a.shape, lambda i: (0, 0))
    tok = pl.BlockSpec((TOP_K, tm), lambda i: (0, i))
    return pl.pallas_call(
        _merge_kernel,
        out_shape=[jax.ShapeDtypeStruct((n, d), F32), jax.ShapeDtypeStruct((n * ROW_SUB, LANES), U32),
                   jax.ShapeDtypeStruct((TOP_K, n), jnp.int32), jax.ShapeDtypeStruct((TOP_K, n), F32),
                   jax.ShapeDtypeStruct((n // tm, N_EXPERTS, LANES), F32),
                   jax.ShapeDtypeStruct((N_EXPERTS, 1), F32)],
        grid=(n // tm,),
        in_specs=[row(d), pl.BlockSpec((tm // S5_T, S5_T * S5_WIDTH), lambda i: (i, 0)),
                  pl.BlockSpec((1, GRID_W, tm // GRID_W, HG_WIDTH), lambda i: (i // per, 0, i % per, 0)),
                  row(HG_WIDTH), row(d), row(d), mod, mod, mod,
                  pl.BlockSpec((1, d), lambda i: (0, 0)), full(pa), full(pb), full(wo), full(wr_t), full(br),
                  pl.BlockSpec((tm, tm), lambda i: (0, 0))],
        out_specs=[row(d), pl.BlockSpec((tm * ROW_SUB, LANES), lambda i: (i, 0)), tok, tok,
                   pl.BlockSpec((1, N_EXPERTS, LANES), lambda i: (i, 0, 0)),
                   pl.BlockSpec((N_EXPERTS, 1), lambda i: (0, 0))],
        scratch_shapes=[pltpu.VMEM((tm, LANES), F32), pltpu.VMEM((tm, LANES), F32)],
        compiler_params=_params("arbitrary"),
        name="merge_out_proj_route",
    )(x2d, ya, on, go, ga, gb, g1, sc2, sh2, n2g.reshape(1, d), pa, pb, wo, wr_t, br,
      _grid_transpose_matrix(tm).T)


MOE_TILE = TOK_TILE
TOKEN_UNROLL = 4
SEG_ALIGN = 8
FILL_ROWS = 512
STAGE_ROWS = MOE_TILE * TOP_K + FILL_ROWS
MOE_BLK = 1024


def _wait_rows(any_ref, sem, n_rows):
    view = any_ref.at[pl.ds(0, n_rows * ROW_SUB)]
    pltpu.make_async_copy(view, view, sem).wait()


def _rows(ref, r0, n):
    return ref.at[pl.ds(pl.multiple_of(r0 * ROW_SUB, ROW_SUB), n * ROW_SUB)]


def _pow2_pieces(n, max_piece, fn, min_piece=1):
    done = 0
    piece = max_piece
    while piece >= min_piece:
        hit = (n & piece) != 0
        pl.when(hit)(functools.partial(fn, done, piece))
        done = done + (n & piece)
        piece //= 2


def _copy_rows(src_ref, src0, dst_ref, dst0, n, max_piece, sem, min_piece=1):
    def piece(off, size):
        pltpu.make_async_copy(_rows(src_ref, src0 + off, size), _rows(dst_ref, dst0 + off, size), sem).start()
    _pow2_pieces(n, max_piece, piece, min_piece)


def _wait_copied_rows(src_ref, dst_ref, n, max_piece, sem):
    def piece(off, size):
        pltpu.make_async_copy(_rows(src_ref, 0, size), _rows(dst_ref, 0, size), sem).wait()
    _pow2_pieces(n, max_piece, piece)


def _copy_tile_segments(i, src_ref, src_tab, dst_ref, dst_tab, cnt_ref, off_ref, fill_src0, fill_dst0, sem):
    def per_expert(e, carry):
        _copy_rows(src_ref, src_tab[i, e], dst_ref, dst_tab[i, e], cnt_ref[i, e], MOE_TILE, sem, SEG_ALIGN)
        return carry

    lax.fori_loop(0, N_EXPERTS, per_expert, 0)
    used = off_ref[i, N_EXPERTS - 1] + cnt_ref[i, N_EXPERTS - 1]
    _copy_rows(src_ref, fill_src0(used), dst_ref, fill_dst0(used), STAGE_ROWS - used, FILL_ROWS, sem, SEG_ALIGN)


def _dispatch_kernel(gs_ref, cnt_ref, off_ref, pad_ref, ls_ref, h2_ref, xs_hbm, ls_smem, stage0, stage1, zbuf,
                     sem0, sem1, lsem, zsem, *, tm, n_blocks):
    i = pl.program_id(0)
    last = pl.num_programs(0) - 1
    cp = pltpu.make_async_copy(ls_ref, ls_smem, lsem)
    cp.start()
    trash0 = n_blocks * MOE_BLK

    @pl.when(i == 0)
    def _():
        stage0[...] = jnp.zeros_like(stage0)
        stage1[...] = jnp.zeros_like(stage1)
        zbuf[...] = jnp.zeros_like(zbuf)
        cpz = pltpu.make_async_copy(zbuf, _rows(xs_hbm, trash0, 2 * FILL_ROWS), zsem)
        cpz.start()
        cpz.wait()

    cp.wait()

    def tile(stage, sem, prev_sem, trash):
        def body(tu, carry):
            for u in range(TOKEN_UNROLL):
                t = tu * TOKEN_UNROLL + u
                row = h2_ref[pl.ds(pl.multiple_of(t * ROW_SUB, ROW_SUB), ROW_SUB), :]
                for k in range(TOP_K):
                    stage[pl.ds(pl.multiple_of(ls_smem[t * TOP_K + k], ROW_SUB), ROW_SUB), :] = row
            return carry

        lax.fori_loop(0, tm // TOKEN_UNROLL, body, 0)
        _copy_tile_segments(i, stage, off_ref, xs_hbm, gs_ref, cnt_ref, off_ref,
                            lambda used: used, lambda used: trash, sem)

        @pl.when(i > 0)
        def _():
            _wait_rows(xs_hbm, prev_sem, STAGE_ROWS)

        @pl.when(i == last)
        def _():
            _wait_rows(xs_hbm, sem, STAGE_ROWS)

    pl.when(i % 2 == 0)(functools.partial(tile, stage0, sem0, sem1, trash0))
    pl.when(i % 2 == 1)(functools.partial(tile, stage1, sem1, sem0, trash0 + FILL_ROWS))

    def zero_pad(e, carry):
        _copy_rows(zbuf, 0, xs_hbm, pad_ref[0, e], pad_ref[1, e], MOE_BLK // 2, zsem)
        return carry

    def wait_pad(e, carry):
        _wait_copied_rows(zbuf, xs_hbm, pad_ref[1, e], MOE_BLK // 2, zsem)
        return carry

    def zero_block(j, carry):
        pltpu.make_async_copy(zbuf, _rows(xs_hbm, j * MOE_BLK, MOE_BLK), zsem).start()
        return carry

    def wait_block(j, carry):
        pltpu.make_async_copy(zbuf, _rows(xs_hbm, 0, MOE_BLK), zsem).wait()
        return carry

    @pl.when(i == 0)
    def _():
        lax.fori_loop(0, N_EXPERTS, zero_pad, 0)
        lax.fori_loop(pad_ref[2, 0], n_blocks, zero_block, 0)

    @pl.when(i == last)
    def _():
        lax.fori_loop(0, N_EXPERTS, wait_pad, 0)
        lax.fori_loop(pad_ref[2, 0], n_blocks, wait_block, 0)


def _dispatch(gstart, seg_cnt, seg_off, pad, ls8, h2_rows, n_blocks, tm):
    n = ls8.shape[0] // TOP_K
    cap = n_blocks * MOE_BLK + 2 * FILL_ROWS
    return pl.pallas_call(
        functools.partial(_dispatch_kernel, tm=tm, n_blocks=n_blocks),
        out_shape=jax.ShapeDtypeStruct((cap * ROW_SUB, LANES), U32),
        grid_spec=pltpu.PrefetchScalarGridSpec(
            num_scalar_prefetch=4,
            grid=(n // tm,),
            in_specs=[pl.BlockSpec((tm * TOP_K,), lambda i, *_: (i,)),
                      pl.BlockSpec((tm * ROW_SUB, LANES), lambda i, *_: (i, 0))],
            out_specs=pl.BlockSpec(memory_space=pl.ANY),
            scratch_shapes=[pltpu.SMEM((tm * TOP_K,), jnp.int32),
                            pltpu.VMEM((STAGE_ROWS * ROW_SUB, LANES), U32),
                            pltpu.VMEM((STAGE_ROWS * ROW_SUB, LANES), U32),
                            pltpu.VMEM((MOE_BLK * ROW_SUB, LANES), U32),
                            pltpu.SemaphoreType.DMA, pltpu.SemaphoreType.DMA, pltpu.SemaphoreType.DMA,
                            pltpu.SemaphoreType.DMA]),
        compiler_params=pltpu.CompilerParams(dimension_semantics=("arbitrary",), vmem_limit_bytes=VMEM_LIMIT,
                                             has_side_effects=True),
        name="moe_dispatch",
    )(gstart, seg_cnt, seg_off, pad, ls8, h2_rows)


def _expert_kernel(be_ref, nu_ref, x_ref, w1_ref, w3_ref, w2_ref, o_ref, w1b, w3b, w2b):
    j = pl.program_id(0)
    e = be_ref[j]
    prev = be_ref[jnp.maximum(j - 1, 0)]
    used = j < nu_ref[0]

    @pl.when(jnp.logical_and(used, jnp.logical_or(j == 0, e != prev)))
    def _():
        w1b[...] = w1_ref[0].astype(BF16)
        w3b[...] = w3_ref[0].astype(BF16)
        w2b[...] = w2_ref[0].astype(BF16)

    @pl.when(used)
    def _():
        x = _from_token_rows(x_ref, MOE_BLK).astype(BF16)
        a = _dot(x, w1b[...])
        hid = (a * _sigmoid(a)) * _dot(x, w3b[...])
        _to_token_rows(o_ref, _dot(hid.astype(BF16), w2b[...]))

    @pl.when(jnp.logical_not(used))
    def _():
        o_ref[...] = jnp.zeros_like(o_ref)


def _experts(block_e, n_used, xs, w1, w3, w2):
    n_blocks = block_e.shape[0]
    d, f = w1.shape[1], w1.shape[2]
    rows = pl.BlockSpec((MOE_BLK * ROW_SUB, LANES), lambda j, be, nu: (j, 0))
    rows_in = pl.BlockSpec((MOE_BLK * ROW_SUB, LANES), lambda j, be, nu: (jnp.minimum(j, nu[0] - 1), 0))
    return pl.pallas_call(
        _expert_kernel,
        out_shape=jax.ShapeDtypeStruct((n_blocks * MOE_BLK * ROW_SUB, LANES), U32),
        grid_spec=pltpu.PrefetchScalarGridSpec(
            num_scalar_prefetch=2,
            grid=(n_blocks,),
            in_specs=[rows_in,
                      pl.BlockSpec((1, d, f), lambda j, be, nu: (be[j], 0, 0)),
                      pl.BlockSpec((1, d, f), lambda j, be, nu: (be[j], 0, 0)),
                      pl.BlockSpec((1, f, d), lambda j, be, nu: (be[j], 0, 0))],
            out_specs=rows,
            scratch_shapes=[pltpu.VMEM((d, f), BF16), pltpu.VMEM((d, f), BF16), pltpu.VMEM((f, d), BF16)]),
        compiler_params=_params("arbitrary"),
        name="moe_experts",
    )(block_e, n_used, xs, w1, w3, w2)


def _combine_kernel(gs_ref, cnt_ref, off_ref, ls_ref, w8_ref, x1_ref, h2_ref, g2_ref, ws1_ref, ws3_ref, ws2_ref,
                    fg_ref, ys_hbm, o_ref, ls_smem, w_smem, gbuf0, gbuf1, acc_rows, sem0, sem1, lsem, *, tm):
    i = pl.program_id(0)
    last = pl.num_programs(0) - 1
    cp1 = pltpu.make_async_copy(ls_ref, ls_smem, lsem)
    cp2 = pltpu.make_async_copy(w8_ref, w_smem, lsem)
    cp1.start()
    cp2.start()

    def fetch(tile, gbuf, sem):
        _copy_tile_segments(tile, ys_hbm, gs_ref, gbuf, off_ref, cnt_ref, off_ref,
                            lambda used: 0, lambda used: used, sem)

    @pl.when(i == 0)
    def _():
        fetch(0, gbuf0, sem0)

    @pl.when(jnp.logical_and(i < last, i % 2 == 0))
    def _():
        fetch(i + 1, gbuf1, sem1)

    @pl.when(jnp.logical_and(i < last, i % 2 == 1))
    def _():
        fetch(i + 1, gbuf0, sem0)

    h2 = _from_token_rows(h2_ref, tm).astype(BF16)
    a = _dot(h2, ws1_ref[...])
    hid = (a * _sigmoid(a)) * _dot(h2, ws3_ref[...])
    acc = _dot(hid.astype(BF16), ws2_ref[...])
    cp1.wait()
    cp2.wait()

    def reduce_rows(gbuf, sem):
        _wait_rows(gbuf, sem, STAGE_ROWS)

        def body(tu, carry):
            for u in range(TOKEN_UNROLL):
                t = tu * TOKEN_UNROLL + u
                lo = jnp.zeros((ROW_SUB, LANES), F32)
                hi = jnp.zeros((ROW_SUB, LANES), F32)
                for k in range(TOP_K):
                    w = w_smem[t * TOP_K + k]
                    words = gbuf[pl.ds(pl.multiple_of(ls_smem[t * TOP_K + k], ROW_SUB), ROW_SUB), :]
                    lo = lo + w * lax.bitcast_convert_type(words << 16, F32)
                    hi = hi + w * lax.bitcast_convert_type(words & jnp.uint32(0xFFFF0000), F32)
                acc_rows[pl.ds(pl.multiple_of(t * SUBLANES, SUBLANES), ROW_SUB), :] = lo
                acc_rows[pl.ds(pl.multiple_of(t * SUBLANES, SUBLANES) + ROW_SUB, ROW_SUB), :] = hi
            return carry

        lax.fori_loop(0, tm // TOKEN_UNROLL, body, 0)

    pl.when(i % 2 == 0)(functools.partial(reduce_rows, gbuf0, sem0))
    pl.when(i % 2 == 1)(functools.partial(reduce_rows, gbuf1, sem1))
    routed = jnp.concatenate([acc_rows[pl.ds(s, tm, stride=SUBLANES), :] for s in range(SUBLANES)], axis=-1)
    y = x1_ref[...] + g2_ref[0] * (acc + routed)
    o_ref[...] = y * lax.rsqrt(jnp.mean(y * y, axis=-1, keepdims=True) + EPS) * fg_ref[...]


def _combine(gstart, seg_cnt, seg_off, ls8, w8, x1, h2_rows, g2, ws1, ws3, ws2, fg, ys, rows_per_batch, tm):
    n, d = x1.shape
    per = rows_per_batch // tm
    tok = pl.BlockSpec((tm * TOP_K,), lambda i, *_: (i,))
    full = lambda a: pl.BlockSpec(a.shape, lambda i, *_: (0, 0))
    return pl.pallas_call(
        functools.partial(_combine_kernel, tm=tm),
        out_shape=jax.ShapeDtypeStruct((n, d), F32),
        grid_spec=pltpu.PrefetchScalarGridSpec(
            num_scalar_prefetch=3,
            grid=(n // tm,),
            in_specs=[tok, tok, pl.BlockSpec((tm, d), lambda i, *_: (i, 0)),
                      pl.BlockSpec((tm * ROW_SUB, LANES), lambda i, *_: (i, 0)),
                      pl.BlockSpec((1, 1, d), lambda i, *_: (i // per, 0, 0)),
                      full(ws1), full(ws3), full(ws2), pl.BlockSpec((1, d), lambda i, *_: (0, 0)),
                      pl.BlockSpec(memory_space=pl.ANY)],
            out_specs=pl.BlockSpec((tm, d), lambda i, *_: (i, 0)),
            scratch_shapes=[pltpu.SMEM((tm * TOP_K,), jnp.int32), pltpu.SMEM((tm * TOP_K,), F32),
                            pltpu.VMEM((STAGE_ROWS * ROW_SUB, LANES), U32),
                            pltpu.VMEM((STAGE_ROWS * ROW_SUB, LANES), U32),
                            pltpu.VMEM((tm * SUBLANES, LANES), F32), pltpu.SemaphoreType.DMA,
                            pltpu.SemaphoreType.DMA, pltpu.SemaphoreType.DMA]),
        compiler_params=_params("arbitrary"),
        name="moe_combine_final",
    )(gstart, seg_cnt, seg_off, ls8, w8, x1, h2_rows, g2, ws1, ws3, ws2, fg.reshape(1, d), ys)


def _moe_plan(seg, counts, n_assign):
    cnt = counts.reshape(N_EXPERTS).astype(jnp.int32)
    padded = (cnt + MOE_BLK - 1) // MOE_BLK * MOE_BLK
    pends = jnp.cumsum(padded)
    pstarts = pends - padded
    max_rows = n_assign + seg.shape[0] * N_EXPERTS * (SEG_ALIGN - 1)
    n_blocks = (max_rows + N_EXPERTS * (MOE_BLK - 1) + MOE_BLK - 1) // MOE_BLK
    seg = seg[:, :, :3].astype(jnp.int32)
    gstart = pstarts[None, :] + seg[:, :, 0]
    blk_start = jnp.arange(n_blocks, dtype=jnp.int32) * MOE_BLK
    block_e = jnp.minimum(jnp.sum((blk_start[:, None] >= pends[None, :]).astype(jnp.int32), axis=1),
                          N_EXPERTS - 1).astype(jnp.int32)
    n_used = (pends[-1:] // MOE_BLK).astype(jnp.int32)
    pad = jnp.stack([pstarts + cnt, padded - cnt, jnp.broadcast_to(n_used, (N_EXPERTS,))], axis=0).astype(jnp.int32)
    return gstart, seg[:, :, 1], seg[:, :, 2], pad, block_e, n_used, n_blocks


def _mixer(x, c, ctx, c_ctx, w_ada, b_ada, norm1_g, norm2_g, w_in, s5_lam_re, s5_lam_im, s5_log_dt,
           s5_b_re, s5_b_im, s5_c_re, s5_c_im, s5_d, s5_w_glu, lb, hg_norm_g, p_a, p_b, w_out,
           moe_w_router, moe_b_router):
    b, l, d = x.shape
    lc = ctx.shape[1]
    n = b * l

    c8 = jnp.concatenate([c, c_ctx[None], jnp.zeros((8 - b - 1, d), F32)], axis=0)
    mod = _ada(c8, w_ada, b_ada)
    sh1, sc1, g1, sh2, sc2, g2 = [mod[:b, k * d:(k + 1) * d].reshape(b, 1, d) for k in range(6)]
    csh1, csc1 = mod[b:b + 1, 0:d].reshape(1, 1, d), mod[b:b + 1, d:2 * d].reshape(1, 1, d)

    w_in_b = w_in.astype(BF16)
    z = dict(zip([p[0] for p in _IN_PIECES],
                 _inproj(x.reshape(n, d), sc1, sh1, norm1_g, w_in_b, l, TOK_TILE, True)))
    zc = dict(zip([p[0] for p in _IN_PIECES],
                  _inproj(ctx.reshape(b * lc, d), csc1, csh1, norm1_g, w_in_b, lc, lc, False)))

    cx = lambda t: t.reshape(b, lc, HG_WIDTH)
    lb_row = lb.reshape(1, HG_WIDTH)
    o_f = _hgrn_pass(z["q"], z["ff"], z["i"], cx(zc["ff"]), cx(zc["i"]), lb_row, None, None, reverse=False)
    o_n = _hgrn_pass(z["q"], z["fb"], z["i"], cx(zc["fb"]), cx(zc["i"]), lb_row, o_f,
                     hg_norm_g.reshape(1, HG_DK), reverse=True)

    d_lag, w_s5_in, w_out_f, w_out_b, decay = _s5_weights(s5_lam_re, s5_lam_im, s5_log_dt, s5_b_re, s5_b_im,
                                                          s5_c_re, s5_c_im)
    kc, kl = lc // S5_T, l // S5_T
    u_lat = z["u"].reshape(b, kl, S5_T * S5_WIDTH)
    u_ctx = zc["u"].reshape(b, kc, S5_T * S5_WIDTH)
    rows_in = kl + kc
    u_ext = jnp.concatenate([u_lat, u_ctx], axis=1).reshape(b * rows_in, S5_T * S5_WIDTH)
    e = _s5_in(u_ext, d_lag, w_s5_in, (b * rows_in) // 2)
    states = _s5_scan(e, decay, b, rows_in, kl)
    d_row = s5_d.astype(F32).reshape(1, S5_WIDTH)
    y_a = _s5_out(states, w_out_f, w_out_b, e.reshape(b, rows_in, -1), z["u"], d_row, s5_w_glu.astype(BF16))

    return _merge(x.reshape(n, d), y_a, o_n, z["go"], z["ga"], z["gb"], g1, sc2, sh2, norm2_g,
                  p_a.astype(BF16), p_b.astype(BF16), w_out.astype(BF16),
                  moe_w_router.T.astype(BF16), moe_b_router.astype(F32).reshape(N_EXPERTS, 1), l, MOE_TILE) + (g2,)


def kernel(x, c, ctx, c_ctx, w_ada, b_ada, norm1_g, norm2_g, w_in, s5_lam_re, s5_lam_im, s5_log_dt, s5_b_re,
           s5_b_im, s5_c_re, s5_c_im, s5_d, s5_w_glu, hg_lb_logits, hg_norm_g, p_a, p_b, w_out, moe_w_router,
           moe_b_router, moe_w1, moe_w3, moe_w2, moe_ws1, moe_ws3, moe_ws2, final_norm_g):
    b, l, d = x.shape
    n = b * l
    assert w_ada.shape[0] == 1, "single-layer block"
    lb = jnp.cumsum(jax.nn.softmax(hg_lb_logits.astype(F32), axis=0), axis=0)[0]
    x1, h2_rows, ls8, w8, seg, counts, g2 = _mixer(
        x, c, ctx, c_ctx, w_ada[0], b_ada[0], norm1_g[0], norm2_g[0], w_in[0], s5_lam_re[0], s5_lam_im[0],
        s5_log_dt[0], s5_b_re[0], s5_b_im[0], s5_c_re[0], s5_c_im[0], s5_d[0], s5_w_glu[0], lb, hg_norm_g[0],
        p_a[0], p_b[0], w_out[0], moe_w_router[0], moe_b_router[0])
    gstart, seg_cnt, seg_off, pad, block_e, n_used, n_blocks = _moe_plan(seg, counts, n * TOP_K)
    ls_flat, w_flat = ls8.T.reshape(n * TOP_K), w8.T.reshape(n * TOP_K)
    xs = _dispatch(gstart, seg_cnt, seg_off, pad, ls_flat, h2_rows, n_blocks, MOE_TILE)
    ys = _experts(block_e, n_used, xs, moe_w1[0], moe_w3[0], moe_w2[0])
    out = _combine(gstart, seg_cnt, seg_off, ls_flat, w_flat, x1, h2_rows, g2, moe_ws1[0].astype(BF16),
                   moe_ws3[0].astype(BF16), moe_ws2[0].astype(BF16), final_norm_g, ys, l, MOE_TILE)
    return out.reshape(b, l, d)
```

```python
import functools
import math

import jax
import jax.numpy as jnp
from jax import lax
from jax.experimental import pallas as pl
from jax.experimental.pallas import tpu as pltpu

F32 = jnp.float32
BF16 = jnp.bfloat16

GRID_W = 64
S5_WIDTH = 256
S5_GROUP = 16
S5_GROUPS = 16
S5_STATE = 64
HG_HEADS = 6
HG_DK = 128
HG_WIDTH = HG_HEADS * HG_DK
N_EXPERTS = 64
ROUTE_GROUPS = 8
TOPK_GROUPS = 4
TOP_K = 8
ROUTED_SCALE = 2.5
EPS = 1e-6

LANES = 128
SUBLANES = 8

TOK_TILE = 512
S5_T = 16
HG_CHUNK = 64
HG_BATCH = 4
VMEM_LIMIT = 56 * 1024 * 1024

_NT = (((1,), (1,)), ((), ()))
_TN = (((0,), (0,)), ((), ()))


def _params(*sem):
    return pltpu.CompilerParams(dimension_semantics=sem, vmem_limit_bytes=VMEM_LIMIT)


def _dot(a, b):
    return jnp.dot(a, b, preferred_element_type=F32)


def _sigmoid(x):
    return 0.5 * jnp.tanh(0.5 * x) + 0.5


def _ada_kernel(c_ref, w_ref, b_ref, o_ref):
    c = c_ref[...]
    s = (c * _sigmoid(c)).astype(BF16)
    o_ref[...] = _dot(s, w_ref[...].astype(BF16)) + b_ref[...]


def _ada(c8, w_ada, b_ada):
    d, n = w_ada.shape
    tn = 1536
    return pl.pallas_call(
        _ada_kernel,
        out_shape=jax.ShapeDtypeStruct((8, n), F32),
        grid=(n // tn,),
        in_specs=[pl.BlockSpec((8, d), lambda j: (0, 0)),
                  pl.BlockSpec((d, tn), lambda j: (0, j)),
                  pl.BlockSpec((1, tn), lambda j: (0, j))],
        out_specs=pl.BlockSpec((8, tn), lambda j: (0, j)),
        compiler_params=_params("arbitrary"),
        name="ada_mod",
    )(c8, w_ada, b_ada.reshape(1, n))


_IN_PIECES = (("u", 0, 256, BF16), ("q", 256, 768, BF16), ("ff", 1024, 768, BF16),
              ("fb", 1792, 768, BF16), ("i", 2560, 768, BF16), ("go", 3328, 768, BF16),
              ("ga", 4096, 1024, BF16), ("gb", 5120, 1024, BF16))


def _fold_rows(val, buf_a, buf_b):
    t = val.shape[0]
    buf_a[...] = val[:, :LANES]
    buf_b[...] = val[:, LANES:]
    pieces = []
    for s in range(S5_T):
        pieces += [buf_a[pl.ds(s, t // S5_T, stride=S5_T), :], buf_b[pl.ds(s, t // S5_T, stride=S5_T), :]]
    return jnp.concatenate(pieces, axis=-1)


def _unfold_rows(val, buf_a, buf_b):
    r = val.shape[0]
    for s in range(S5_T):
        buf_a[pl.ds(s, r, stride=S5_T), :] = val[:, s * S5_WIDTH:s * S5_WIDTH + LANES]
        buf_b[pl.ds(s, r, stride=S5_T), :] = val[:, s * S5_WIDTH + LANES:(s + 1) * S5_WIDTH]
    return jnp.concatenate([buf_a[...], buf_b[...]], axis=-1)


def _grid_transpose_matrix(tm):
    i = jnp.arange(tm)
    src = (i % (tm // GRID_W)) * GRID_W + i // (tm // GRID_W)
    return (src[:, None] == jnp.arange(tm)[None, :]).astype(BF16)


def _inproj_kernel(x_ref, sc_ref, sh_ref, g_ref, w_ref, p_ref, *o_refs):
    o_refs, (fold_a, fold_b) = o_refs[:len(_IN_PIECES)], o_refs[len(_IN_PIECES):]
    x = x_ref[...]
    y = x * lax.rsqrt(jnp.mean(x * x, axis=-1, keepdims=True) + EPS) * g_ref[...]
    h = (y * (1.0 + sc_ref[0]) + sh_ref[0]).astype(BF16)
    h_cm = None
    for (name, a, wd, _), o_ref in zip(_IN_PIECES, o_refs):
        if name == "u":
            o_ref[...] = _fold_rows(_dot(h, w_ref[:, a:a + wd]), fold_a, fold_b).astype(o_ref.dtype)
        elif len(o_ref.shape) == 2:
            o_ref[...] = _dot(h, w_ref[:, a:a + wd]).astype(o_ref.dtype)
        else:
            if h_cm is None:
                h_cm = _dot(p_ref[...], h).astype(BF16)
            o_ref[0] = _dot(h_cm, w_ref[:, a:a + wd]).astype(o_ref.dtype).reshape(o_ref.shape[1:])


_COLMAJOR_PIECES = ("q", "ff", "fb", "i")


def _inproj(x2d, sc, sh, g, w_bf16, rows_per_mod, tm, colmajor):
    n, d = x2d.shape
    per = rows_per_mod // tm
    mod_map = (lambda i: (i // per, 0, 0)) if sc.shape[0] > 1 else (lambda i: (0, 0, 0))
    shapes, specs = [], []
    for name, _, wd, dt in _IN_PIECES:
        if colmajor and name in _COLMAJOR_PIECES:
            shapes.append(jax.ShapeDtypeStruct((n // rows_per_mod, GRID_W, rows_per_mod // GRID_W, wd), dt))
            specs.append(pl.BlockSpec((1, GRID_W, tm // GRID_W, wd), lambda i: (i // per, 0, i % per, 0)))
        elif name == "u":
            shapes.append(jax.ShapeDtypeStruct((n // S5_T, S5_T * wd), dt))
            specs.append(pl.BlockSpec((tm // S5_T, S5_T * wd), lambda i: (i, 0)))
        else:
            shapes.append(jax.ShapeDtypeStruct((n, wd), dt))
            specs.append(pl.BlockSpec((tm, wd), lambda i: (i, 0)))
    return pl.pallas_call(
        _inproj_kernel,
        out_shape=shapes,
        grid=(n // tm,),
        in_specs=[pl.BlockSpec((tm, d), lambda i: (i, 0)),
                  pl.BlockSpec((1, 1, d), mod_map),
                  pl.BlockSpec((1, 1, d), mod_map),
                  pl.BlockSpec((1, d), lambda i: (0, 0)),
                  pl.BlockSpec(w_bf16.shape, lambda i: (0, 0)),
                  pl.BlockSpec((tm, tm), lambda i: (0, 0))],
        out_specs=specs,
        scratch_shapes=[pltpu.VMEM((tm, LANES), F32), pltpu.VMEM((tm, LANES), F32)],
        compiler_params=_params("arbitrary"),
        name="in_proj",
    )(x2d, sc, sh, g.reshape(1, d), w_bf16, _grid_transpose_matrix(tm))


def _hgrn_gates(zf, lb):
    sig = _sigmoid(zf)
    logf = jnp.log(lb + (1.0 - lb) * sig)
    k = (1.0 - lb) * (1.0 - sig)
    return logf, k


def _chunk_cumsum(cs, logf):
    hi = logf.astype(BF16)
    lo = (logf - hi.astype(F32)).astype(BF16)
    return _dot(cs, hi) + _dot(cs, lo)


def _hgrn_state_step(zf, v, lb, st, cs, reverse):
    logf, k = _hgrn_gates(zf, lb)
    cum = _chunk_cumsum(cs, logf)
    t = 0 if reverse else HG_CHUNK - 1
    total = cum[t:t + 1, :]
    kdec = (k * jnp.exp(total - cum)).astype(BF16)
    st_new = st * jnp.exp(total) + lax.dot_general(v.astype(BF16), kdec, _TN, preferred_element_type=F32)
    return cum, k, st_new


def _hgrn_kernel(*refs, reverse, final, n_ctx_chunks):
    if final:
        q_all, f_all, v_all, cf_ref, cv_ref, lb_ref, of_all, g_ref, o_all, st_ref = refs
    else:
        q_all, f_all, v_all, cf_ref, cv_ref, lb_ref, o_all, st_ref = refs
        of_all = None
    n_batch = q_all.shape[0]
    c_len = HG_CHUNK
    n_rows = q_all.shape[2]
    n_chunks = n_rows // c_len
    row = lax.broadcasted_iota(jnp.int32, (n_rows, n_rows), 0)
    col = lax.broadcasted_iota(jnp.int32, (n_rows, n_rows), 1)
    tri = (col >= row) if reverse else (col <= row)
    same_chunk = None
    for c in range(n_chunks):
        lo, hi = c * c_len, (c + 1) * c_len
        blk = (row >= lo) & (row < hi) & (col >= lo) & (col < hi)
        same_chunk = blk if same_chunk is None else (same_chunk | blk)
    mask = tri & same_chunk
    cs = jnp.where(mask, 1.0, 0.0).astype(BF16)

    @pl.when(pl.program_id(1) == 0)
    def _():
        cs1 = cs[:c_len, :c_len]
        order = range(n_ctx_chunks - 1, -1, -1) if reverse else range(n_ctx_chunks)
        for bi in range(n_batch):
            for h in range(HG_HEADS):
                cols = slice(h * HG_DK, (h + 1) * HG_DK)
                st = jnp.zeros((HG_DK, HG_DK), F32)
                for c in order:
                    rows = slice(c * c_len, (c + 1) * c_len)
                    _, _, st = _hgrn_state_step(cf_ref[bi, rows, cols].astype(F32), cv_ref[bi, rows, cols].astype(F32),
                                                lb_ref[:, cols], st, cs1, reverse)
                st_ref[bi * HG_HEADS + h] = st

    def chunk_rows(x, r):
        return [x[c * c_len + r:c * c_len + r + 1, :] for c in range(n_chunks)]

    def over_chunks(rows):
        return jnp.concatenate([jnp.broadcast_to(r, (c_len, r.shape[1])) for r in rows], axis=0)

    lb = lb_ref[...]
    r_ref = c_len // 2 - 1 if reverse else c_len // 2
    r_tot = 0 if reverse else c_len - 1
    order = range(n_chunks - 1, -1, -1) if reverse else range(n_chunks)
    for bi in range(n_batch):
        q = q_all[bi, 0].astype(F32)
        v = v_all[bi, 0]
        logf, k = _hgrn_gates(f_all[bi, 0].astype(F32), lb)
        cum = _chunk_cumsum(cs, logf)
        ref_rows, tot_rows = chunk_rows(cum, r_ref), chunk_rows(cum, r_tot)
        ref = over_chunks(ref_rows)
        qe = q * jnp.exp(cum - ref)
        ke = k * jnp.exp(ref - cum)
        qi, ki = qe.astype(BF16), ke.astype(BF16)
        q_in = (qe * over_chunks([jnp.exp(r) for r in ref_rows])).astype(BF16)
        kdec = (ke * over_chunks([jnp.exp(t - r) for t, r in zip(tot_rows, ref_rows)])).astype(BF16)
        for h in range(HG_HEADS):
            cols = slice(h * HG_DK, (h + 1) * HG_DK)
            s = lax.dot_general(qi[:, cols], ki[:, cols], _NT, preferred_element_type=F32)
            o_intra = _dot(jnp.where(mask, s, 0.0).astype(BF16), v[:, cols])
            st = st_ref[bi * HG_HEADS + h]
            for c in order:
                rows = slice(c * c_len, (c + 1) * c_len)
                o = o_intra[rows] + lax.dot_general(q_in[rows, cols], st.astype(BF16), _NT,
                                                    preferred_element_type=F32)
                total = cum[c * c_len + r_tot:c * c_len + r_tot + 1, cols]
                st = st * jnp.exp(total) + lax.dot_general(v[rows, cols], kdec[rows, cols], _TN,
                                                           preferred_element_type=F32)
                if final:
                    o = o + of_all[bi, 0, rows, cols].astype(F32)
                    o = o * lax.rsqrt(jnp.mean(o * o, axis=-1, keepdims=True) + EPS) * g_ref[...]
                o_all[bi, 0, rows, cols] = o.astype(o_all.dtype)
            st_ref[bi * HG_HEADS + h] = st


def _hgrn_pass(q, f, v, cf, cv, lb, o_prev, g, *, reverse):
    b, nw, rows, _ = q.shape
    nb = HG_BATCH if b % HG_BATCH == 0 else 1
    final = o_prev is not None
    wmap = (lambda bi, w: (bi, nw - 1 - w, 0, 0)) if reverse else (lambda bi, w: (bi, w, 0, 0))
    blk = pl.BlockSpec((nb, 1, rows, HG_WIDTH), wmap)
    cblk = pl.BlockSpec((nb, cf.shape[1], HG_WIDTH), lambda bi, w: (bi, 0, 0))
    in_specs = [blk, blk, blk, cblk, cblk, pl.BlockSpec((1, HG_WIDTH), lambda bi, w: (0, 0))]
    args = [q, f, v, cf, cv, lb]
    if final:
        in_specs += [blk, pl.BlockSpec((1, HG_DK), lambda bi, w: (0, 0))]
        args += [o_prev, g]
    return pl.pallas_call(
        functools.partial(_hgrn_kernel, reverse=reverse, final=final, n_ctx_chunks=cf.shape[1] // HG_CHUNK),
        out_shape=jax.ShapeDtypeStruct(q.shape, BF16),
        grid=(b // nb, nw),
        in_specs=in_specs,
        out_specs=blk,
        scratch_shapes=[pltpu.VMEM((nb * HG_HEADS, HG_DK, HG_DK), F32)],
        compiler_params=_params("arbitrary", "arbitrary"),
        name="hgrn_bwd" if reverse else "hgrn_fwd",
    )(*args)


def _s5_weights(lam_re, lam_im, log_dt, b_re, b_im, c_re, c_im):
    g, p, cc, t = S5_GROUPS, S5_STATE, S5_GROUP, S5_T
    lre = jnp.minimum(lam_re.astype(F32), -1e-4)
    lim = lam_im.astype(F32)
    dt = jnp.exp(log_dt.astype(F32))[..., None]
    ks = jnp.arange(t + 1, dtype=F32)[:, None, None, None]
    mag = jnp.exp(ks * (lre * dt)[None])
    pw_re = mag * jnp.cos(ks * (lim * dt)[None])
    pw_im = mag * jnp.sin(ks * (lim * dt)[None])
    nr, ni = pw_re[1] - 1.0, pw_im[1]
    den = lre * lre + lim * lim
    cf_re = (nr * lre + ni * lim) / den
    cf_im = (ni * lre - nr * lim) / den
    bb_re = cf_re[..., None] * b_re - cf_im[..., None] * b_im
    bb_im = cf_re[..., None] * b_im + cf_im[..., None] * b_re
    cre, cim = c_re.astype(F32), c_im.astype(F32)
    sw, ns = S5_WIDTH, 2 * g * p
    grp_of_row = jnp.arange(sw)[:, None] // cc

    cp_re = cre[None, None, :, :, :] * pw_re[:t, :, :, None, :] - cim[None, None] * pw_im[:t, :, :, None, :]
    cp_im = cre[None, None, :, :, :] * pw_im[:t, :, :, None, :] + cim[None, None] * pw_re[:t, :, :, None, :]
    def contract_p(cp, bb):
        return jnp.sum(cp.transpose(4, 1, 0, 2, 3)[..., None] * bb.transpose(2, 0, 1, 3)[:, :, None, :, None, :],
                       axis=0)

    kk = contract_p(cp_re, bb_re) - contract_p(cp_im, bb_im)
    kf, kb = kk[0], kk[1]
    kall = jnp.concatenate([kb[:0:-1], (kf[0] + kb[0])[None], kf[1:]], axis=0)
    kt = kall.transpose(0, 1, 3, 2).reshape(2 * t - 1, sw, cc)

    def spread(x, period, reps):
        sel = (jnp.arange(period)[:, None] == (jnp.arange(period * reps)[None, :] % period)).astype(BF16)
        return jnp.dot(x.astype(BF16), sel, preferred_element_type=BF16)

    same = grp_of_row == (jnp.arange(sw)[None, :] // cc)
    d_lag = jnp.where(same[None], spread(kt, cc, g), 0)

    same_in = jnp.tile(grp_of_row, (t, 1)) == ((jnp.arange(ns)[None, :] % (g * p)) // p)

    def in_to_state(pre, pim, bre, bim):
        xre = pre[..., None] * bre[None] - pim[..., None] * bim[None]
        xim = pre[..., None] * bim[None] + pim[..., None] * bre[None]
        return [spread(xre.transpose(0, 1, 3, 2).reshape(t * sw, p), p, g),
                spread(xim.transpose(0, 1, 3, 2).reshape(t * sw, p), p, g)]

    w_in = jnp.concatenate(in_to_state(pw_re[t - 1::-1, 0], pw_im[t - 1::-1, 0], bb_re[0], bb_im[0])
                           + in_to_state(pw_re[:t, 1], pw_im[:t, 1], bb_re[1], bb_im[1]), axis=1)
    w_in = jnp.where(jnp.tile(same_in, (1, 2)), w_in, 0)

    same_out = ((jnp.arange(ns)[:, None] % (g * p)) // p) == ((jnp.arange(t * sw)[None, :] // cc) % g)
    col = jnp.arange(t * sw)
    pick = (jnp.arange(t * cc)[:, None] == ((col // sw) * cc + col % cc)[None, :]).astype(BF16)

    def state_to_out(pre, pim):
        are = cre[None] * pre[:, :, None, :] - cim[None] * pim[:, :, None, :]
        aim = cre[None] * pim[:, :, None, :] + cim[None] * pre[:, :, None, :]
        a = jnp.concatenate([are.transpose(1, 3, 0, 2), -aim.transpose(1, 3, 0, 2)], axis=0)
        a = jnp.dot(a.reshape(ns, t * cc).astype(BF16), pick, preferred_element_type=BF16)
        return jnp.where(same_out, a, 0)

    w_out_f = state_to_out(pw_re[1:, 0], pw_im[1:, 0])
    w_out_b = state_to_out(pw_re[t:0:-1, 1], pw_im[t:0:-1, 1])

    decay = jnp.stack([pw_re[t].reshape(2, g * p), pw_im[t].reshape(2, g * p)], axis=1)
    return d_lag, w_in, w_out_f, w_out_b, decay


def _s5_in_kernel(u_ref, d_ref, w_ref, o_ref):
    j = pl.program_id(1)

    @pl.when(j < S5_T)
    def _():
        acc = _dot(u_ref[:, 0:S5_WIDTH], d_ref[j + S5_T - 1])
        for s in range(1, S5_T):
            acc = acc + _dot(u_ref[:, s * S5_WIDTH:(s + 1) * S5_WIDTH], d_ref[j - s + S5_T - 1])
        o_ref[...] = acc

    @pl.when(j >= S5_T)
    def _():
        o_ref[...] = _dot(u_ref[...], w_ref[...])


def _s5_in(u, d_lag, w_in, tm):
    m, k = u.shape
    tn = S5_WIDTH
    nj = (k + w_in.shape[1]) // tn
    return pl.pallas_call(
        _s5_in_kernel,
        out_shape=jax.ShapeDtypeStruct((m, nj * tn), F32),
        grid=(m // tm, nj),
        in_specs=[pl.BlockSpec((tm, k), lambda i, j: (i, 0)),
                  pl.BlockSpec(d_lag.shape, lambda i, j: (0, 0, 0)),
                  pl.BlockSpec((k, tn), lambda i, j: (0, jnp.maximum(j - S5_T, 0)))],
        out_specs=pl.BlockSpec((tm, tn), lambda i, j: (i, j)),
        compiler_params=_params("arbitrary", "arbitrary"),
        name="s5_in",
    )(u, d_lag, w_in)


def _s5_scan_kernel(efr_ref, efi_ref, ebr_ref, ebi_ref, a_ref, hfr_ref, hfi_ref, hbr_ref, hbi_ref,
                    *, nb, rows_in, rows_out):
    dirs = ((efr_ref, efi_ref, hfr_ref, hfi_ref, a_ref[0, 0:1, :], a_ref[0, 1:2, :]),
            (ebr_ref, ebi_ref, hbr_ref, hbi_ref, a_ref[1, 0:1, :], a_ref[1, 1:2, :]))
    zero = jnp.zeros((nb, dirs[0][4].shape[1]), F32)

    def step(srcs, carry, store):
        new = []
        for di, (er_ref, ei_ref, hr_ref, hi_ref, are, aim) in enumerate(dirs):
            hre, him = carry[2 * di], carry[2 * di + 1]
            if store:
                hr_ref[pl.ds(srcs[di], nb, stride=rows_out), :] = hre
                hi_ref[pl.ds(srcs[di], nb, stride=rows_out), :] = him
            ere = er_ref[pl.ds(srcs[di], nb, stride=rows_in), :]
            eim = ei_ref[pl.ds(srcs[di], nb, stride=rows_in), :]
            new += [are * hre - aim * him + ere, are * him + aim * hre + eim]
        return tuple(new)

    n_ctx = rows_in - rows_out
    carry = lax.fori_loop(0, n_ctx, lambda s, c: step((rows_out + s, rows_in - 1 - s), c, False),
                          tuple([zero] * 4))

    def two_steps(s2, c):
        c = step((2 * s2, rows_out - 1 - 2 * s2), c, True)
        return step((2 * s2 + 1, rows_out - 2 - 2 * s2), c, True)

    lax.fori_loop(0, rows_out // 2, two_steps, carry)


def _s5_scan(e, decay, nb, rows_in, rows_out):
    assert rows_out % 2 == 0, "the latent chunks are scanned two per loop iteration"
    tc = LANES
    nsr = S5_GROUPS * S5_STATE
    c0 = (S5_T * S5_WIDTH) // tc
    nt = nsr // tc
    eblk = lambda k: pl.BlockSpec((nb * rows_in, tc), lambda j: (0, c0 + k * nt + j))
    hblk = pl.BlockSpec((nb * rows_out, tc), lambda j: (0, j))
    return pl.pallas_call(
        functools.partial(_s5_scan_kernel, nb=nb, rows_in=rows_in, rows_out=rows_out),
        out_shape=[jax.ShapeDtypeStruct((nb * rows_out, nsr), F32)] * 4,
        grid=(nt,),
        in_specs=[eblk(0), eblk(1), eblk(2), eblk(3), pl.BlockSpec((2, 2, tc), lambda j: (0, 0, j))],
        out_specs=[hblk] * 4,
        compiler_params=_params("arbitrary"),
        name="s5_scan",
    )(e, e, e, e, decay)


def _gelu_tanh(x):
    return 0.5 * x * (1.0 + jnp.tanh(math.sqrt(2.0 / math.pi) * (x + 0.044715 * x * x * x)))


def _s5_out_kernel(hfr_ref, hfi_ref, hbr_ref, hbi_ref, wf_ref, wb_ref, yi_ref, u_ref, d_ref, wg_ref, o_ref):
    nsr = hfr_ref.shape[1]
    y = yi_ref[0] + d_ref[...] * u_ref[...].astype(F32)
    for h_ref, w_ref, r0 in ((hfr_ref, wf_ref, 0), (hfi_ref, wf_ref, nsr), (hbr_ref, wb_ref, 0), (hbi_ref, wb_ref, nsr)):
        y = y + _dot(h_ref[...].astype(BF16), w_ref[r0:r0 + nsr, :])
    y = _gelu_tanh(y)
    gate = _sigmoid(_dot(y.astype(BF16), wg_ref[...]))
    o_ref[...] = (y * gate).astype(o_ref.dtype)


def _s5_out(states, w_out_f, w_out_b, e3, u_rows, d_row, w_glu):
    m, nsr = states[0].shape
    nb = e3.shape[0]
    tm = m // nb
    tn = S5_WIDTH
    st = pl.BlockSpec((tm, nsr), lambda i, j: (i, 0))
    wo = pl.BlockSpec((2 * nsr, tn), lambda i, j: (0, j))
    return pl.pallas_call(
        _s5_out_kernel,
        out_shape=jax.ShapeDtypeStruct((m, S5_T * S5_WIDTH), BF16),
        grid=(nb, S5_T),
        in_specs=[st, st, st, st, wo, wo,
                  pl.BlockSpec((1, tm, tn), lambda i, j: (i, 0, j)),
                  pl.BlockSpec((tm, tn), lambda i, j: (i, j)),
                  pl.BlockSpec((1, tn), lambda i, j: (0, 0)),
                  pl.BlockSpec((tn, tn), lambda i, j: (0, 0))],
        out_specs=pl.BlockSpec((tm, tn), lambda i, j: (i, j)),
        compiler_params=_params("arbitrary", "arbitrary"),
        name="s5_out",
    )(*states, w_out_f, w_out_b, e3, u_rows, d_row, w_glu)


U32 = jnp.uint32
ROW_SUB = 4


def _to_token_rows(ref, val):
    t, d = val.shape
    bits = lax.bitcast_convert_type(val.astype(BF16).astype(F32), U32)
    w = (bits[:, :d // 2] >> 16) | bits[:, d // 2:]
    for s in range(ROW_SUB):
        ref[pl.ds(s, t, stride=ROW_SUB), :] = w[:, s * LANES:(s + 1) * LANES]


def _from_token_rows(ref, t, row0=0):
    w = jnp.concatenate([ref[pl.ds(row0 * ROW_SUB + s, t, stride=ROW_SUB), :] for s in range(ROW_SUB)], axis=-1)
    lo = lax.bitcast_convert_type(w << 16, F32)
    hi = lax.bitcast_convert_type(w & jnp.uint32(0xFFFF0000), F32)
    return jnp.concatenate([lo, hi], axis=-1)


def _route(h2b, wr_ref, br_ref, cnt_ref, ls8_ref, w8_ref, seg_ref):
    tm = h2b.shape[0]
    per_group = N_EXPERTS // ROUTE_GROUPS
    scores = _sigmoid(lax.dot_general(wr_ref[...], h2b, _NT, preferred_element_type=F32))
    biased = scores + br_ref[...]
    neg = -jnp.inf
    sub = lax.broadcasted_iota(jnp.int32, (per_group, tm), 0)
    grp = []
    for gi in range(ROUTE_GROUPS):
        v = biased[gi * per_group:(gi + 1) * per_group, :]
        m1 = jnp.max(v, axis=0, keepdims=True)
        first = jnp.min(jnp.where(v == m1, sub, per_group), axis=0, keepdims=True)
        m2 = jnp.max(jnp.where(sub == first, neg, v), axis=0, keepdims=True)
        grp.append(m1 + m2)
    grp = jnp.concatenate(grp, axis=0)
    gid = lax.broadcasted_iota(jnp.int32, (ROUTE_GROUPS, tm), 0)
    beaten = jnp.zeros((ROUTE_GROUPS, tm), jnp.int32)
    for gj in range(ROUTE_GROUPS):
        r = grp[gj:gj + 1, :]
        beaten = beaten + jnp.where((r > grp) | ((r == grp) & (gj < gid)), 1, 0)
    group_ok = beaten < TOPK_GROUPS
    expert_ok = jnp.concatenate(
        [jnp.broadcast_to(group_ok[gi:gi + 1, :], (per_group, tm)) for gi in range(ROUTE_GROUPS)], axis=0)
    cur = jnp.where(expert_ok, biased, neg)
    eid = lax.broadcasted_iota(jnp.int32, (N_EXPERTS, tm), 0)
    sel = jnp.zeros((N_EXPERTS, tm), F32)
    picks, wts = [], []
    for _ in range(TOP_K):
        m = jnp.max(cur, axis=0, keepdims=True)
        idx = jnp.min(jnp.where(cur == m, eid, N_EXPERTS), axis=0, keepdims=True)
        hit = eid == idx
        picks.append(idx)
        wts.append(jnp.sum(jnp.where(hit, scores, 0.0), axis=0, keepdims=True))
        sel = jnp.where(hit, 1.0, sel)
        cur = jnp.where(hit, neg, cur)
    wsum = wts[0]
    for w in wts[1:]:
        wsum = wsum + w
    selb = sel.astype(BF16)
    ti = lax.broadcasted_iota(jnp.int32, (tm, tm), 0)
    tj = lax.broadcasted_iota(jnp.int32, (tm, tm), 1)
    rank = _dot(selb, jnp.where(ti < tj, 1.0, 0.0).astype(BF16))
    seg_units = jnp.ceil(jnp.sum(sel, axis=1, keepdims=True) * (1.0 / SEG_ALIGN))
    ei = lax.broadcasted_iota(jnp.int32, (N_EXPERTS, N_EXPERTS), 0)
    ej = lax.broadcasted_iota(jnp.int32, (N_EXPERTS, N_EXPERTS), 1)
    units_row = jnp.broadcast_to(seg_units, (N_EXPERTS, LANES)).astype(BF16)
    seg_off = _dot(jnp.where(ej < ei, 1.0, 0.0).astype(BF16), units_row)[:, 0:1] * SEG_ALIGN
    seg_rows = seg_units * SEG_ALIGN
    slot = seg_off + rank
    for k in range(TOP_K):
        w8_ref[k:k + 1, :] = wts[k] / wsum * ROUTED_SCALE
        ls8_ref[k:k + 1, :] = (jnp.sum(jnp.where(eid == picks[k], slot, 0.0), axis=0, keepdims=True)
                               * ROW_SUB).astype(jnp.int32)
    lane = lax.broadcasted_iota(jnp.int32, (N_EXPERTS, LANES), 1)
    seg_ref[0] = jnp.where(lane == 0, cnt_ref[...], jnp.where(lane == 1, seg_rows, seg_off))
    cnt_ref[...] = cnt_ref[...] + seg_rows


def _merge_kernel(x_ref, ya_ref, on_ref, go_ref, ga_ref, gb_ref, g1_ref, sc_ref, sh_ref, n2_ref,
                  pa_ref, pb_ref, wo_ref, wr_ref, br_ref, pt_ref,
                  x1_ref, h2_ref, ls8_ref, w8_ref, seg_ref, cnt_ref, fold_a, fold_b):
    @pl.when(pl.program_id(0) == 0)
    def _():
        cnt_ref[...] = jnp.zeros_like(cnt_ref)

    go = go_ref[...].astype(F32)
    on = _dot(pt_ref[...], on_ref[0].reshape(x_ref.shape[0], HG_WIDTH))
    y_b = (on * (go * _sigmoid(go))).astype(BF16)
    y_a = _unfold_rows(ya_ref[...].astype(F32), fold_a, fold_b).astype(BF16)
    pa = _dot(y_a, pa_ref[...])
    pb = _dot(y_b, pb_ref[...])
    merged = _sigmoid(ga_ref[...].astype(F32)) * pa + _sigmoid(gb_ref[...].astype(F32)) * pb
    x1 = x_ref[...] + g1_ref[0] * _dot(merged.astype(BF16), wo_ref[...])
    x1_ref[...] = x1
    y = x1 * lax.rsqrt(jnp.mean(x1 * x1, axis=-1, keepdims=True) + EPS) * n2_ref[...]
    h2 = y * (1.0 + sc_ref[0]) + sh_ref[0]
    _to_token_rows(h2_ref, h2)
    _route(h2.astype(BF16), wr_ref, br_ref, cnt_ref, ls8_ref, w8_ref, seg_ref)


def _merge(x2d, ya, on, go, ga, gb, g1, sc2, sh2, n2g, pa, pb, wo, wr_t, br, rows_per_batch, tm):
    n, d = x2d.shape
    per = rows_per_batch // tm
    row = lambda wd: pl.BlockSpec((tm, wd), lambda i: (i, 0))
    mod = pl.BlockSpec((1, 1, d), lambda i: (i // per, 0, 0))
    full = lambda a: pl.BlockSpec(a.shape, lambda i: (0, 0))
    tok = pl.BlockSpec((TOP_K, tm), lambda i: (0, i))
    return pl.pallas_call(
        _merge_kernel,
        out_shape=[jax.ShapeDtypeStruct((n, d), F32), jax.ShapeDtypeStruct((n * ROW_SUB, LANES), U32),
                   jax.ShapeDtypeStruct((TOP_K, n), jnp.int32), jax.ShapeDtypeStruct((TOP_K, n), F32),
                   jax.ShapeDtypeStruct((n // tm, N_EXPERTS, LANES), F32),
                   jax.ShapeDtypeStruct((N_EXPERTS, 1), F32)],
        grid=(n // tm,),
        in_specs=[row(d), pl.BlockSpec((tm // S5_T, S5_T * S5_WIDTH), lambda i: (i, 0)),
                  pl.BlockSpec((1, GRID_W, tm // GRID_W, HG_WIDTH), lambda i: (i // per, 0, i % per, 0)),
                  row(HG_WIDTH), row(d), row(d), mod, mod, mod,
                  pl.BlockSpec((1, d), lambda i: (0, 0)), full(pa), full(pb), full(wo), full(wr_t), full(br),
                  pl.BlockSpec((tm, tm), lambda i: (0, 0))],
        out_specs=[row(d), pl.BlockSpec((tm * ROW_SUB, LANES), lambda i: (i, 0)), tok, tok,
                   pl.BlockSpec((1, N_EXPERTS, LANES), lambda i: (i, 0, 0)),
                   pl.BlockSpec((N_EXPERTS, 1), lambda i: (0, 0))],
        scratch_shapes=[pltpu.VMEM((tm, LANES), F32), pltpu.VMEM((tm, LANES), F32)],
        compiler_params=_params("arbitrary"),
        name="merge_out_proj_route",
    )(x2d, ya, on, go, ga, gb, g1, sc2, sh2, n2g.reshape(1, d), pa, pb, wo, wr_t, br,
      _grid_transpose_matrix(tm).T)


MOE_TILE = TOK_TILE
TOKEN_UNROLL = 4
COMBINE_UNROLL = 8
SEG_ALIGN = 8
FILL_ROWS = 512
STAGE_ROWS = MOE_TILE * TOP_K + FILL_ROWS
MOE_BLK = 1024


def _wait_rows(any_ref, sem, n_rows):
    view = any_ref.at[pl.ds(0, n_rows * ROW_SUB)]
    pltpu.make_async_copy(view, view, sem).wait()


def _rows(ref, r0, n):
    return ref.at[pl.ds(pl.multiple_of(r0 * ROW_SUB, ROW_SUB), n * ROW_SUB)]


def _pow2_pieces(n, max_piece, fn, min_piece=1):
    done = 0
    piece = max_piece
    while piece >= min_piece:
        hit = (n & piece) != 0
        pl.when(hit)(functools.partial(fn, done, piece))
        done = done + (n & piece)
        piece //= 2


def _copy_rows(src_ref, src0, dst_ref, dst0, n, max_piece, sem, min_piece=1):
    def piece(off, size):
        pltpu.make_async_copy(_rows(src_ref, src0 + off, size), _rows(dst_ref, dst0 + off, size), sem).start()
    _pow2_pieces(n, max_piece, piece, min_piece)


def _wait_copied_rows(src_ref, dst_ref, n, max_piece, sem):
    def piece(off, size):
        pltpu.make_async_copy(_rows(src_ref, 0, size), _rows(dst_ref, 0, size), sem).wait()
    _pow2_pieces(n, max_piece, piece)


def _copy_tile_segments(i, src_ref, src_tab, dst_ref, dst_tab, cnt_ref, off_ref, fill_src0, fill_dst0, sem):
    def per_expert(e, carry):
        _copy_rows(src_ref, src_tab[i, e], dst_ref, dst_tab[i, e], cnt_ref[i, e], MOE_TILE, sem, SEG_ALIGN)
        return carry

    lax.fori_loop(0, N_EXPERTS, per_expert, 0)
    used = off_ref[i, N_EXPERTS - 1] + cnt_ref[i, N_EXPERTS - 1]
    _copy_rows(src_ref, fill_src0(used), dst_ref, fill_dst0(used), STAGE_ROWS - used, FILL_ROWS, sem, SEG_ALIGN)


def _dispatch_kernel(gs_ref, cnt_ref, off_ref, pad_ref, ls_ref, h2_ref, xs_hbm, ls_smem, stage0, stage1, zbuf,
                     sem0, sem1, lsem, zsem, *, tm, n_blocks):
    i = pl.program_id(0)
    last = pl.num_programs(0) - 1
    cp = pltpu.make_async_copy(ls_ref, ls_smem, lsem)
    cp.start()
    trash0 = n_blocks * MOE_BLK

    @pl.when(i == 0)
    def _():
        stage0[...] = jnp.zeros_like(stage0)
        stage1[...] = jnp.zeros_like(stage1)
        zbuf[...] = jnp.zeros_like(zbuf)
        cpz = pltpu.make_async_copy(zbuf, _rows(xs_hbm, trash0, 2 * FILL_ROWS), zsem)
        cpz.start()
        cpz.wait()

    cp.wait()

    def tile(stage, sem, prev_sem, trash):
        def body(tu, carry):
            for u in range(TOKEN_UNROLL):
                t = tu * TOKEN_UNROLL + u
                row = h2_ref[pl.ds(pl.multiple_of(t * ROW_SUB, ROW_SUB), ROW_SUB), :]
                for k in range(TOP_K):
                    stage[pl.ds(pl.multiple_of(ls_smem[t * TOP_K + k], ROW_SUB), ROW_SUB), :] = row
            return carry

        lax.fori_loop(0, tm // TOKEN_UNROLL, body, 0)
        _copy_tile_segments(i, stage, off_ref, xs_hbm, gs_ref, cnt_ref, off_ref,
                            lambda used: used, lambda used: trash, sem)

        @pl.when(i > 0)
        def _():
            _wait_rows(xs_hbm, prev_sem, STAGE_ROWS)

        @pl.when(i == last)
        def _():
            _wait_rows(xs_hbm, sem, STAGE_ROWS)

    pl.when(i % 2 == 0)(functools.partial(tile, stage0, sem0, sem1, trash0))
    pl.when(i % 2 == 1)(functools.partial(tile, stage1, sem1, sem0, trash0 + FILL_ROWS))

    def zero_pad(e, carry):
        _copy_rows(zbuf, 0, xs_hbm, pad_ref[0, e], pad_ref[1, e], MOE_BLK // 2, zsem)
        return carry

    def wait_pad(e, carry):
        _wait_copied_rows(zbuf, xs_hbm, pad_ref[1, e], MOE_BLK // 2, zsem)
        return carry

    def zero_block(j, carry):
        pltpu.make_async_copy(zbuf, _rows(xs_hbm, j * MOE_BLK, MOE_BLK), zsem).start()
        return carry

    def wait_block(j, carry):
        pltpu.make_async_copy(zbuf, _rows(xs_hbm, 0, MOE_BLK), zsem).wait()
        return carry

    @pl.when(i == 0)
    def _():
        lax.fori_loop(0, N_EXPERTS, zero_pad, 0)
        lax.fori_loop(pad_ref[2, 0], n_blocks, zero_block, 0)

    @pl.when(i == last)
    def _():
        lax.fori_loop(0, N_EXPERTS, wait_pad, 0)
        lax.fori_loop(pad_ref[2, 0], n_blocks, wait_block, 0)


def _dispatch(gstart, seg_cnt, seg_off, pad, ls8, h2_rows, n_blocks, tm):
    n = ls8.shape[0] // TOP_K
    cap = n_blocks * MOE_BLK + 2 * FILL_ROWS
    return pl.pallas_call(
        functools.partial(_dispatch_kernel, tm=tm, n_blocks=n_blocks),
        out_shape=jax.ShapeDtypeStruct((cap * ROW_SUB, LANES), U32),
        grid_spec=pltpu.PrefetchScalarGridSpec(
            num_scalar_prefetch=4,
            grid=(n // tm,),
            in_specs=[pl.BlockSpec((tm * TOP_K,), lambda i, *_: (i,)),
                      pl.BlockSpec((tm * ROW_SUB, LANES), lambda i, *_: (i, 0))],
            out_specs=pl.BlockSpec(memory_space=pl.ANY),
            scratch_shapes=[pltpu.SMEM((tm * TOP_K,), jnp.int32),
                            pltpu.VMEM((STAGE_ROWS * ROW_SUB, LANES), U32),
                            pltpu.VMEM((STAGE_ROWS * ROW_SUB, LANES), U32),
                            pltpu.VMEM((MOE_BLK * ROW_SUB, LANES), U32),
                            pltpu.SemaphoreType.DMA, pltpu.SemaphoreType.DMA, pltpu.SemaphoreType.DMA,
                            pltpu.SemaphoreType.DMA]),
        compiler_params=pltpu.CompilerParams(dimension_semantics=("arbitrary",), vmem_limit_bytes=VMEM_LIMIT,
                                             has_side_effects=True),
        name="moe_dispatch",
    )(gstart, seg_cnt, seg_off, pad, ls8, h2_rows)


def _expert_kernel(be_ref, nu_ref, x_ref, w1_ref, w3_ref, w2_ref, o_ref, w1b, w3b, w2b):
    j = pl.program_id(0)
    e = be_ref[j]
    prev = be_ref[jnp.maximum(j - 1, 0)]
    used = j < nu_ref[0]

    @pl.when(jnp.logical_and(used, jnp.logical_or(j == 0, e != prev)))
    def _():
        w1b[...] = w1_ref[0].astype(BF16)
        w3b[...] = w3_ref[0].astype(BF16)
        w2b[...] = w2_ref[0].astype(BF16)

    @pl.when(used)
    def _():
        x = _from_token_rows(x_ref, MOE_BLK).astype(BF16)
        a = _dot(x, w1b[...])
        hid = (a * _sigmoid(a)) * _dot(x, w3b[...])
        _to_token_rows(o_ref, _dot(hid.astype(BF16), w2b[...]))

    @pl.when(jnp.logical_not(used))
    def _():
        o_ref[...] = jnp.zeros_like(o_ref)


def _experts(block_e, n_used, xs, w1, w3, w2):
    n_blocks = block_e.shape[0]
    d, f = w1.shape[1], w1.shape[2]
    rows = pl.BlockSpec((MOE_BLK * ROW_SUB, LANES), lambda j, be, nu: (j, 0))
    rows_in = pl.BlockSpec((MOE_BLK * ROW_SUB, LANES), lambda j, be, nu: (jnp.minimum(j, nu[0] - 1), 0))
    return pl.pallas_call(
        _expert_kernel,
        out_shape=jax.ShapeDtypeStruct((n_blocks * MOE_BLK * ROW_SUB, LANES), U32),
        grid_spec=pltpu.PrefetchScalarGridSpec(
            num_scalar_prefetch=2,
            grid=(n_blocks,),
            in_specs=[rows_in,
                      pl.BlockSpec((1, d, f), lambda j, be, nu: (be[j], 0, 0)),
                      pl.BlockSpec((1, d, f), lambda j, be, nu: (be[j], 0, 0)),
                      pl.BlockSpec((1, f, d), lambda j, be, nu: (be[j], 0, 0))],
            out_specs=rows,
            scratch_shapes=[pltpu.VMEM((d, f), BF16), pltpu.VMEM((d, f), BF16), pltpu.VMEM((f, d), BF16)]),
        compiler_params=_params("arbitrary"),
        name="moe_experts",
    )(block_e, n_used, xs, w1, w3, w2)


def _combine_kernel(gs_ref, cnt_ref, off_ref, ls_ref, w8_ref, x1_ref, h2_ref, g2_ref, ws1_ref, ws3_ref, ws2_ref,
                    fg_ref, ys_hbm, o_ref, ls_smem, w_smem, gbuf0, gbuf1, acc_rows, sem0, sem1, lsem, *, tm):
    i = pl.program_id(0)
    last = pl.num_programs(0) - 1
    cp1 = pltpu.make_async_copy(ls_ref, ls_smem, lsem)
    cp2 = pltpu.make_async_copy(w8_ref, w_smem, lsem)
    cp1.start()
    cp2.start()

    def fetch(tile, gbuf, sem):
        _copy_tile_segments(tile, ys_hbm, gs_ref, gbuf, off_ref, cnt_ref, off_ref,
                            lambda used: 0, lambda used: used, sem)

    @pl.when(i == 0)
    def _():
        fetch(0, gbuf0, sem0)

    @pl.when(jnp.logical_and(i < last, i % 2 == 0))
    def _():
        fetch(i + 1, gbuf1, sem1)

    @pl.when(jnp.logical_and(i < last, i % 2 == 1))
    def _():
        fetch(i + 1, gbuf0, sem0)

    h2 = _from_token_rows(h2_ref, tm).astype(BF16)
    a = _dot(h2, ws1_ref[...])
    hid = (a * _sigmoid(a)) * _dot(h2, ws3_ref[...])
    acc = _dot(hid.astype(BF16), ws2_ref[...])
    cp1.wait()
    cp2.wait()

    def reduce_rows(gbuf, sem):
        _wait_rows(gbuf, sem, STAGE_ROWS)

        def body(tu, carry):
            for u in range(COMBINE_UNROLL):
                t = tu * COMBINE_UNROLL + u
                lo = jnp.zeros((ROW_SUB, LANES), F32)
                hi = jnp.zeros((ROW_SUB, LANES), F32)
                for k in range(TOP_K):
                    w = w_smem[t * TOP_K + k]
                    words = gbuf[pl.ds(pl.multiple_of(ls_smem[t * TOP_K + k], ROW_SUB), ROW_SUB), :]
                    lo = lo + w * lax.bitcast_convert_type(words << 16, F32)
                    hi = hi + w * lax.bitcast_convert_type(words & jnp.uint32(0xFFFF0000), F32)
                acc_rows[pl.ds(pl.multiple_of(t * SUBLANES, SUBLANES), ROW_SUB), :] = lo
                acc_rows[pl.ds(pl.multiple_of(t * SUBLANES, SUBLANES) + ROW_SUB, ROW_SUB), :] = hi
            return carry

        lax.fori_loop(0, tm // COMBINE_UNROLL, body, 0)

    pl.when(i % 2 == 0)(functools.partial(reduce_rows, gbuf0, sem0))
    pl.when(i % 2 == 1)(functools.partial(reduce_rows, gbuf1, sem1))
    routed = jnp.concatenate([acc_rows[pl.ds(s, tm, stride=SUBLANES), :] for s in range(SUBLANES)], axis=-1)
    y = x1_ref[...] + g2_ref[0] * (acc + routed)
    o_ref[...] = y * lax.rsqrt(jnp.mean(y * y, axis=-1, keepdims=True) + EPS) * fg_ref[...]


def _combine(gstart, seg_cnt, seg_off, ls8, w8, x1, h2_rows, g2, ws1, ws3, ws2, fg, ys, rows_per_batch, tm):
    n, d = x1.shape
    per = rows_per_batch // tm
    tok = pl.BlockSpec((tm * TOP_K,), lambda i, *_: (i,))
    full = lambda a: pl.BlockSpec(a.shape, lambda i, *_: (0, 0))
    return pl.pallas_call(
        functools.partial(_combine_kernel, tm=tm),
        out_shape=jax.ShapeDtypeStruct((n, d), F32),
        grid_spec=pltpu.PrefetchScalarGridSpec(
            num_scalar_prefetch=3,
            grid=(n // tm,),
            in_specs=[tok, tok, pl.BlockSpec((tm, d), lambda i, *_: (i, 0)),
                      pl.BlockSpec((tm * ROW_SUB, LANES), lambda i, *_: (i, 0)),
                      pl.BlockSpec((1, 1, d), lambda i, *_: (i // per, 0, 0)),
                      full(ws1), full(ws3), full(ws2), pl.BlockSpec((1, d), lambda i, *_: (0, 0)),
                      pl.BlockSpec(memory_space=pl.ANY)],
            out_specs=pl.BlockSpec((tm, d), lambda i, *_: (i, 0)),
            scratch_shapes=[pltpu.SMEM((tm * TOP_K,), jnp.int32), pltpu.SMEM((tm * TOP_K,), F32),
                            pltpu.VMEM((STAGE_ROWS * ROW_SUB, LANES), U32),
                            pltpu.VMEM((STAGE_ROWS * ROW_SUB, LANES), U32),
                            pltpu.VMEM((tm * SUBLANES, LANES), F32), pltpu.SemaphoreType.DMA,
                            pltpu.SemaphoreType.DMA, pltpu.SemaphoreType.DMA]),
        compiler_params=_params("arbitrary"),
        name="moe_combine_final",
    )(gstart, seg_cnt, seg_off, ls8, w8, x1, h2_rows, g2, ws1, ws3, ws2, fg.reshape(1, d), ys)


def _moe_plan(seg, counts, n_assign):
    cnt = counts.reshape(N_EXPERTS).astype(jnp.int32)
    padded = (cnt + MOE_BLK - 1) // MOE_BLK * MOE_BLK
    pends = jnp.cumsum(padded)
    pstarts = pends - padded
    max_rows = n_assign + seg.shape[0] * N_EXPERTS * (SEG_ALIGN - 1)
    n_blocks = (max_rows + N_EXPERTS * (MOE_BLK - 1) + MOE_BLK - 1) // MOE_BLK
    seg = seg[:, :, :3].astype(jnp.int32)
    gstart = pstarts[None, :] + seg[:, :, 0]
    blk_start = jnp.arange(n_blocks, dtype=jnp.int32) * MOE_BLK
    block_e = jnp.minimum(jnp.sum((blk_start[:, None] >= pends[None, :]).astype(jnp.int32), axis=1),
                          N_EXPERTS - 1).astype(jnp.int32)
    n_used = (pends[-1:] // MOE_BLK).astype(jnp.int32)
    pad = jnp.stack([pstarts + cnt, padded - cnt, jnp.broadcast_to(n_used, (N_EXPERTS,))], axis=0).astype(jnp.int32)
    return gstart, seg[:, :, 1], seg[:, :, 2], pad, block_e, n_used, n_blocks


def _mixer(x, c, ctx, c_ctx, w_ada, b_ada, norm1_g, norm2_g, w_in, s5_lam_re, s5_lam_im, s5_log_dt,
           s5_b_re, s5_b_im, s5_c_re, s5_c_im, s5_d, s5_w_glu, lb, hg_norm_g, p_a, p_b, w_out,
           moe_w_router, moe_b_router):
    b, l, d = x.shape
    lc = ctx.shape[1]
    n = b * l

    c8 = jnp.concatenate([c, c_ctx[None], jnp.zeros((8 - b - 1, d), F32)], axis=0)
    mod = _ada(c8, w_ada, b_ada)
    sh1, sc1, g1, sh2, sc2, g2 = [mod[:b, k * d:(k + 1) * d].reshape(b, 1, d) for k in range(6)]
    csh1, csc1 = mod[b:b + 1, 0:d].reshape(1, 1, d), mod[b:b + 1, d:2 * d].reshape(1, 1, d)

    w_in_b = w_in.astype(BF16)
    z = dict(zip([p[0] for p in _IN_PIECES],
                 _inproj(x.reshape(n, d), sc1, sh1, norm1_g, w_in_b, l, TOK_TILE, True)))
    zc = dict(zip([p[0] for p in _IN_PIECES],
                  _inproj(ctx.reshape(b * lc, d), csc1, csh1, norm1_g, w_in_b, lc, lc, False)))

    cx = lambda t: t.reshape(b, lc, HG_WIDTH)
    lb_row = lb.reshape(1, HG_WIDTH)
    o_f = _hgrn_pass(z["q"], z["ff"], z["i"], cx(zc["ff"]), cx(zc["i"]), lb_row, None, None, reverse=False)
    o_n = _hgrn_pass(z["q"], z["fb"], z["i"], cx(zc["fb"]), cx(zc["i"]), lb_row, o_f,
                     hg_norm_g.reshape(1, HG_DK), reverse=True)

    d_lag, w_s5_in, w_out_f, w_out_b, decay = _s5_weights(s5_lam_re, s5_lam_im, s5_log_dt, s5_b_re, s5_b_im,
                                                          s5_c_re, s5_c_im)
    kc, kl = lc // S5_T, l // S5_T
    u_lat = z["u"].reshape(b, kl, S5_T * S5_WIDTH)
    u_ctx = zc["u"].reshape(b, kc, S5_T * S5_WIDTH)
    rows_in = kl + kc
    u_ext = jnp.concatenate([u_lat, u_ctx], axis=1).reshape(b * rows_in, S5_T * S5_WIDTH)
    e = _s5_in(u_ext, d_lag, w_s5_in, (b * rows_in) // 2)
    states = _s5_scan(e, decay, b, rows_in, kl)
    d_row = s5_d.astype(F32).reshape(1, S5_WIDTH)
    y_a = _s5_out(states, w_out_f, w_out_b, e.reshape(b, rows_in, -1), z["u"], d_row, s5_w_glu.astype(BF16))

    return _merge(x.reshape(n, d), y_a, o_n, z["go"], z["ga"], z["gb"], g1, sc2, sh2, norm2_g,
                  p_a.astype(BF16), p_b.astype(BF16), w_out.astype(BF16),
                  moe_w_router.T.astype(BF16), moe_b_router.astype(F32).reshape(N_EXPERTS, 1), l, MOE_TILE) + (g2,)


def kernel(x, c, ctx, c_ctx, w_ada, b_ada, norm1_g, norm2_g, w_in, s5_lam_re, s5_lam_im, s5_log_dt, s5_b_re,
           s5_b_im, s5_c_re, s5_c_im, s5_d, s5_w_glu, hg_lb_logits, hg_norm_g, p_a, p_b, w_out, moe_w_router,
           moe_b_router, moe_w1, moe_w3, moe_w2, moe_ws1, moe_ws3, moe_ws2, final_norm_g):
    b, l, d = x.shape
    n = b * l
    assert w_ada.shape[0] == 1, "single-layer block"
    lb = jnp.cumsum(jax.nn.softmax(hg_lb_logits.astype(F32), axis=0), axis=0)[0]
    x1, h2_rows, ls8, w8, seg, counts, g2 = _mixer(
        x, c, ctx, c_ctx, w_ada[0], b_ada[0], norm1_g[0], norm2_g[0], w_in[0], s5_lam_re[0], s5_lam_im[0],
        s5_log_dt[0], s5_b_re[0], s5_b_im[0], s5_c_re[0], s5_c_im[0], s5_d[0], s5_w_glu[0], lb, hg_norm_g[0],
        p_a[0], p_b[0], w_out[0], moe_w_router[0], moe_b_router[0])
    gstart, seg_cnt, seg_off, pad, block_e, n_used, n_blocks = _moe_plan(seg, counts, n * TOP_K)
    ls_flat, w_flat = ls8.T.reshape(n * TOP_K), w8.T.reshape(n * TOP_K)
    xs = _dispatch(gstart, seg_cnt, seg_off, pad, ls_flat, h2_rows, n_blocks, MOE_TILE)
    ys = _experts(block_e, n_used, xs, moe_w1[0], moe_w3[0], moe_w2[0])
    out = _combine(gstart, seg_cnt, seg_off, ls_flat, w_flat, x1, h2_rows, g2, moe_ws1[0].astype(BF16),
                   moe_ws3[0].astype(BF16), moe_ws2[0].astype(BF16), final_norm_g, ys, l, MOE_TILE)
    return out.reshape(b, l, d)
```

```python
import functools
import math

import jax
import jax.numpy as jnp
from jax import lax
from jax.experimental import pallas as pl
from jax.experimental.pallas import tpu as pltpu

F32 = jnp.float32
BF16 = jnp.bfloat16

GRID_W = 64
S5_WIDTH = 256
S5_GROUP = 16
S5_GROUPS = 16
S5_STATE = 64
HG_HEADS = 6
HG_DK = 128
HG_WIDTH = HG_HEADS * HG_DK
N_EXPERTS = 64
ROUTE_GROUPS = 8
TOPK_GROUPS = 4
TOP_K = 8
ROUTED_SCALE = 2.5
EPS = 1e-6

LANES = 128
SUBLANES = 8

TOK_TILE = 512
S5_T = 16
HG_CHUNK = 64
HG_BATCH = 4
VMEM_LIMIT = 56 * 1024 * 1024

_NT = (((1,), (1,)), ((), ()))
_TN = (((0,), (0,)), ((), ()))


def _params(*sem):
    return pltpu.CompilerParams(dimension_semantics=sem, vmem_limit_bytes=VMEM_LIMIT)


def _dot(a, b):
    return jnp.dot(a, b, preferred_element_type=F32)


def _sigmoid(x):
    return 0.5 * jnp.tanh(0.5 * x) + 0.5


def _ada_kernel(c_ref, w_ref, b_ref, o_ref):
    c = c_ref[...]
    s = (c * _sigmoid(c)).astype(BF16)
    o_ref[...] = _dot(s, w_ref[...].astype(BF16)) + b_ref[...]


def _ada(c8, w_ada, b_ada):
    d, n = w_ada.shape
    tn = 1536
    return pl.pallas_call(
        _ada_kernel,
        out_shape=jax.ShapeDtypeStruct((8, n), F32),
        grid=(n // tn,),
        in_specs=[pl.BlockSpec((8, d), lambda j: (0, 0)),
                  pl.BlockSpec((d, tn), lambda j: (0, j)),
                  pl.BlockSpec((1, tn), lambda j: (0, j))],
        out_specs=pl.BlockSpec((8, tn), lambda j: (0, j)),
        compiler_params=_params("arbitrary"),
        name="ada_mod",
    )(c8, w_ada, b_ada.reshape(1, n))


_IN_PIECES = (("u", 0, 256, BF16), ("q", 256, 768, BF16), ("ff", 1024, 768, BF16),
              ("fb", 1792, 768, BF16), ("i", 2560, 768, BF16), ("go", 3328, 768, BF16),
              ("ga", 4096, 1024, BF16), ("gb", 5120, 1024, BF16))


def _fold_rows(val, buf_a, buf_b):
    t = val.shape[0]
    buf_a[...] = val[:, :LANES]
    buf_b[...] = val[:, LANES:]
    pieces = []
    for s in range(S5_T):
        pieces += [buf_a[pl.ds(s, t // S5_T, stride=S5_T), :], buf_b[pl.ds(s, t // S5_T, stride=S5_T), :]]
    return jnp.concatenate(pieces, axis=-1)


def _unfold_rows(val, buf_a, buf_b):
    r = val.shape[0]
    for s in range(S5_T):
        buf_a[pl.ds(s, r, stride=S5_T), :] = val[:, s * S5_WIDTH:s * S5_WIDTH + LANES]
        buf_b[pl.ds(s, r, stride=S5_T), :] = val[:, s * S5_WIDTH + LANES:(s + 1) * S5_WIDTH]
    return jnp.concatenate([buf_a[...], buf_b[...]], axis=-1)


def _grid_transpose_matrix(tm):
    i = jnp.arange(tm)
    src = (i % (tm // GRID_W)) * GRID_W + i // (tm // GRID_W)
    return (src[:, None] == jnp.arange(tm)[None, :]).astype(BF16)


def _inproj_kernel(x_ref, sc_ref, sh_ref, g_ref, w_ref, p_ref, *o_refs):
    o_refs, (fold_a, fold_b) = o_refs[:len(_IN_PIECES)], o_refs[len(_IN_PIECES):]
    x = x_ref[...]
    y = x * lax.rsqrt(jnp.mean(x * x, axis=-1, keepdims=True) + EPS) * g_ref[...]
    h = (y * (1.0 + sc_ref[0]) + sh_ref[0]).astype(BF16)
    h_cm = None
    for (name, a, wd, _), o_ref in zip(_IN_PIECES, o_refs):
        if name == "u":
            o_ref[...] = _fold_rows(_dot(h, w_ref[:, a:a + wd]), fold_a, fold_b).astype(o_ref.dtype)
        elif len(o_ref.shape) == 2:
            o_ref[...] = _dot(h, w_ref[:, a:a + wd]).astype(o_ref.dtype)
        else:
            if h_cm is None:
                h_cm = _dot(p_ref[...], h).astype(BF16)
            o_ref[0] = _dot(h_cm, w_ref[:, a:a + wd]).astype(o_ref.dtype).reshape(o_ref.shape[1:])


_COLMAJOR_PIECES = ("q", "ff", "fb", "i")


def _inproj(x2d, sc, sh, g, w_bf16, rows_per_mod, tm, colmajor):
    n, d = x2d.shape
    per = rows_per_mod // tm
    mod_map = (lambda i: (i // per, 0, 0)) if sc.shape[0] > 1 else (lambda i: (0, 0, 0))
    shapes, specs = [], []
    for name, _, wd, dt in _IN_PIECES:
        if colmajor and name in _COLMAJOR_PIECES:
            shapes.append(jax.ShapeDtypeStruct((n // rows_per_mod, GRID_W, rows_per_mod // GRID_W, wd), dt))
            specs.append(pl.BlockSpec((1, GRID_W, tm // GRID_W, wd), lambda i: (i // per, 0, i % per, 0)))
        elif name == "u":
            shapes.append(jax.ShapeDtypeStruct((n // S5_T, S5_T * wd), dt))
            specs.append(pl.BlockSpec((tm // S5_T, S5_T * wd), lambda i: (i, 0)))
        else:
            shapes.append(jax.ShapeDtypeStruct((n, wd), dt))
            specs.append(pl.BlockSpec((tm, wd), lambda i: (i, 0)))
    return pl.pallas_call(
        _inproj_kernel,
        out_shape=shapes,
        grid=(n // tm,),
        in_specs=[pl.BlockSpec((tm, d), lambda i: (i, 0)),
                  pl.BlockSpec((1, 1, d), mod_map),
                  pl.BlockSpec((1, 1, d), mod_map),
                  pl.BlockSpec((1, d), lambda i: (0, 0)),
                  pl.BlockSpec(w_bf16.shape, lambda i: (0, 0)),
                  pl.BlockSpec((tm, tm), lambda i: (0, 0))],
        out_specs=specs,
        scratch_shapes=[pltpu.VMEM((tm, LANES), F32), pltpu.VMEM((tm, LANES), F32)],
        compiler_params=_params("arbitrary"),
        name="in_proj",
    )(x2d, sc, sh, g.reshape(1, d), w_bf16, _grid_transpose_matrix(tm))


def _hgrn_gates(zf, lb):
    sig = _sigmoid(zf)
    logf = jnp.log(lb + (1.0 - lb) * sig)
    k = (1.0 - lb) * (1.0 - sig)
    return logf, k


def _chunk_cumsum(cs, logf):
    hi = logf.astype(BF16)
    lo = (logf - hi.astype(F32)).astype(BF16)
    return _dot(cs, hi) + _dot(cs, lo)


def _hgrn_state_step(zf, v, lb, st, cs, reverse):
    logf, k = _hgrn_gates(zf, lb)
    cum = _chunk_cumsum(cs, logf)
    t = 0 if reverse else HG_CHUNK - 1
    total = cum[t:t + 1, :]
    kdec = (k * jnp.exp(total - cum)).astype(BF16)
    st_new = st * jnp.exp(total) + lax.dot_general(v.astype(BF16), kdec, _TN, preferred_element_type=F32)
    return cum, k, st_new


def _hgrn_kernel(*refs, reverse, final, n_ctx_chunks):
    if final:
        q_all, f_all, v_all, cf_ref, cv_ref, lb_ref, of_all, g_ref, o_all, st_ref = refs
    else:
        q_all, f_all, v_all, cf_ref, cv_ref, lb_ref, o_all, st_ref = refs
        of_all = None
    n_batch = q_all.shape[0]
    c_len = HG_CHUNK
    n_rows = q_all.shape[2]
    n_chunks = n_rows // c_len
    row = lax.broadcasted_iota(jnp.int32, (n_rows, n_rows), 0)
    col = lax.broadcasted_iota(jnp.int32, (n_rows, n_rows), 1)
    tri = (col >= row) if reverse else (col <= row)
    same_chunk = None
    for c in range(n_chunks):
        lo, hi = c * c_len, (c + 1) * c_len
        blk = (row >= lo) & (row < hi) & (col >= lo) & (col < hi)
        same_chunk = blk if same_chunk is None else (same_chunk | blk)
    mask = tri & same_chunk
    cs = jnp.where(mask, 1.0, 0.0).astype(BF16)

    @pl.when(pl.program_id(1) == 0)
    def _():
        cs1 = cs[:c_len, :c_len]
        order = range(n_ctx_chunks - 1, -1, -1) if reverse else range(n_ctx_chunks)
        for bi in range(n_batch):
            for h in range(HG_HEADS):
                cols = slice(h * HG_DK, (h + 1) * HG_DK)
                st = jnp.zeros((HG_DK, HG_DK), F32)
                for c in order:
                    rows = slice(c * c_len, (c + 1) * c_len)
                    _, _, st = _hgrn_state_step(cf_ref[bi, rows, cols].astype(F32), cv_ref[bi, rows, cols].astype(F32),
                                                lb_ref[:, cols], st, cs1, reverse)
                st_ref[bi * HG_HEADS + h] = st

    def chunk_rows(x, r):
        return [x[c * c_len + r:c * c_len + r + 1, :] for c in range(n_chunks)]

    def over_chunks(rows):
        return jnp.concatenate([jnp.broadcast_to(r, (c_len, r.shape[1])) for r in rows], axis=0)

    lb = lb_ref[...]
    r_ref = c_len // 2 - 1 if reverse else c_len // 2
    r_tot = 0 if reverse else c_len - 1
    order = range(n_chunks - 1, -1, -1) if reverse else range(n_chunks)
    for bi in range(n_batch):
        q = q_all[bi, 0].astype(F32)
        v = v_all[bi, 0]
        logf, k = _hgrn_gates(f_all[bi, 0].astype(F32), lb)
        cum = _chunk_cumsum(cs, logf)
        ref_rows, tot_rows = chunk_rows(cum, r_ref), chunk_rows(cum, r_tot)
        ref = over_chunks(ref_rows)
        qe = q * jnp.exp(cum - ref)
        ke = k * jnp.exp(ref - cum)
        qi, ki = qe.astype(BF16), ke.astype(BF16)
        q_in = (qe * over_chunks([jnp.exp(r) for r in ref_rows])).astype(BF16)
        kdec = (ke * over_chunks([jnp.exp(t - r) for t, r in zip(tot_rows, ref_rows)])).astype(BF16)
        for h in range(HG_HEADS):
            cols = slice(h * HG_DK, (h + 1) * HG_DK)
            s = lax.dot_general(qi[:, cols], ki[:, cols], _NT, preferred_element_type=F32)
            o_intra = _dot(jnp.where(mask, s, 0.0).astype(BF16), v[:, cols])
            st = st_ref[bi * HG_HEADS + h]
            for c in order:
                rows = slice(c * c_len, (c + 1) * c_len)
                o = o_intra[rows] + lax.dot_general(q_in[rows, cols], st.astype(BF16), _NT,
                                                    preferred_element_type=F32)
                total = cum[c * c_len + r_tot:c * c_len + r_tot + 1, cols]
                st = st * jnp.exp(total) + lax.dot_general(v[rows, cols], kdec[rows, cols], _TN,
                                                           preferred_element_type=F32)
                if final:
                    o = o + of_all[bi, 0, rows, cols].astype(F32)
                    o = o * lax.rsqrt(jnp.mean(o * o, axis=-1, keepdims=True) + EPS) * g_ref[...]
                o_all[bi, 0, rows, cols] = o.astype(o_all.dtype)
            st_ref[bi * HG_HEADS + h] = st


def _hgrn_pass(q, f, v, cf, cv, lb, o_prev, g, *, reverse):
    b, nw, rows, _ = q.shape
    nb = HG_BATCH if b % HG_BATCH == 0 else 1
    final = o_prev is not None
    wmap = (lambda bi, w: (bi, nw - 1 - w, 0, 0)) if reverse else (lambda bi, w: (bi, w, 0, 0))
    blk = pl.BlockSpec((nb, 1, rows, HG_WIDTH), wmap)
    cblk = pl.BlockSpec((nb, cf.shape[1], HG_WIDTH), lambda bi, w: (bi, 0, 0))
    in_specs = [blk, blk, blk, cblk, cblk, pl.BlockSpec((1, HG_WIDTH), lambda bi, w: (0, 0))]
    args = [q, f, v, cf, cv, lb]
    if final:
        in_specs += [blk, pl.BlockSpec((1, HG_DK), lambda bi, w: (0, 0))]
        args += [o_prev, g]
    return pl.pallas_call(
        functools.partial(_hgrn_kernel, reverse=reverse, final=final, n_ctx_chunks=cf.shape[1] // HG_CHUNK),
        out_shape=jax.ShapeDtypeStruct(q.shape, BF16),
        grid=(b // nb, nw),
        in_specs=in_specs,
        out_specs=blk,
        scratch_shapes=[pltpu.VMEM((nb * HG_HEADS, HG_DK, HG_DK), F32)],
        compiler_params=_params("arbitrary", "arbitrary"),
        name="hgrn_bwd" if reverse else "hgrn_fwd",
    )(*args)


def _s5_weights(lam_re, lam_im, log_dt, b_re, b_im, c_re, c_im):
    g, p, cc, t = S5_GROUPS, S5_STATE, S5_GROUP, S5_T
    lre = jnp.minimum(lam_re.astype(F32), -1e-4)
    lim = lam_im.astype(F32)
    dt = jnp.exp(log_dt.astype(F32))[..., None]
    ks = jnp.arange(t + 1, dtype=F32)[:, None, None, None]
    mag = jnp.exp(ks * (lre * dt)[None])
    pw_re = mag * jnp.cos(ks * (lim * dt)[None])
    pw_im = mag * jnp.sin(ks * (lim * dt)[None])
    nr, ni = pw_re[1] - 1.0, pw_im[1]
    den = lre * lre + lim * lim
    cf_re = (nr * lre + ni * lim) / den
    cf_im = (ni * lre - nr * lim) / den
    bb_re = cf_re[..., None] * b_re - cf_im[..., None] * b_im
    bb_im = cf_re[..., None] * b_im + cf_im[..., None] * b_re
    cre, cim = c_re.astype(F32), c_im.astype(F32)
    sw, ns = S5_WIDTH, 2 * g * p
    grp_of_row = jnp.arange(sw)[:, None] // cc

    qr = pw_re[:t].transpose(1, 2, 0, 3)[:, :, :, None, :]
    qi = pw_im[:t].transpose(1, 2, 0, 3)[:, :, :, None, :]
    cp_re = (cre[None, :, None] * qr - cim[None, :, None] * qi).reshape(2, g, t * cc, p)
    cp_im = (cre[None, :, None] * qi + cim[None, :, None] * qr).reshape(2, g, t * cc, p)

    def contract_p(cp, bb):
        return lax.dot_general(cp, bb, (((3,), (2,)), ((0, 1), (0, 1))), precision=lax.Precision.HIGHEST,
                               preferred_element_type=F32)

    kk = (contract_p(cp_re, bb_re) - contract_p(cp_im, bb_im)).reshape(2, g, t, cc, cc).transpose(0, 2, 1, 3, 4)
    kf, kb = kk[0], kk[1]
    kall = jnp.concatenate([kb[:0:-1], (kf[0] + kb[0])[None], kf[1:]], axis=0)
    kt = kall.transpose(0, 1, 3, 2).reshape(2 * t - 1, sw, cc)

    def spread(x, period, reps):
        sel = (jnp.arange(period)[:, None] == (jnp.arange(period * reps)[None, :] % period)).astype(BF16)
        return jnp.dot(x.astype(BF16), sel, preferred_element_type=BF16)

    same = grp_of_row == (jnp.arange(sw)[None, :] // cc)
    d_lag = jnp.where(same[None], spread(kt, cc, g), 0)

    same_in = jnp.tile(grp_of_row, (t, 1)) == ((jnp.arange(ns)[None, :] % (g * p)) // p)

    def in_to_state(pre, pim, bre, bim):
        xre = pre[..., None] * bre[None] - pim[..., None] * bim[None]
        xim = pre[..., None] * bim[None] + pim[..., None] * bre[None]
        return [spread(xre.transpose(0, 1, 3, 2).reshape(t * sw, p), p, g),
                spread(xim.transpose(0, 1, 3, 2).reshape(t * sw, p), p, g)]

    w_in = jnp.concatenate(in_to_state(pw_re[t - 1::-1, 0], pw_im[t - 1::-1, 0], bb_re[0], bb_im[0])
                           + in_to_state(pw_re[:t, 1], pw_im[:t, 1], bb_re[1], bb_im[1]), axis=1)
    w_in = jnp.where(jnp.tile(same_in, (1, 2)), w_in, 0)

    same_out = ((jnp.arange(ns)[:, None] % (g * p)) // p) == ((jnp.arange(t * sw)[None, :] // cc) % g)
    col = jnp.arange(t * sw)
    pick = (jnp.arange(t * cc)[:, None] == ((col // sw) * cc + col % cc)[None, :]).astype(BF16)

    def state_to_out(pre, pim):
        are = cre[None] * pre[:, :, None, :] - cim[None] * pim[:, :, None, :]
        aim = cre[None] * pim[:, :, None, :] + cim[None] * pre[:, :, None, :]
        a = jnp.concatenate([are.transpose(1, 3, 0, 2), -aim.transpose(1, 3, 0, 2)], axis=0)
        a = jnp.dot(a.reshape(ns, t * cc).astype(BF16), pick, preferred_element_type=BF16)
        return jnp.where(same_out, a, 0)

    w_out_f = state_to_out(pw_re[1:, 0], pw_im[1:, 0])
    w_out_b = state_to_out(pw_re[t:0:-1, 1], pw_im[t:0:-1, 1])

    decay = jnp.stack([pw_re[t].reshape(2, g * p), pw_im[t].reshape(2, g * p)], axis=1)
    return d_lag, w_in, w_out_f, w_out_b, decay


def _s5_in_kernel(u_ref, d_ref, w_ref, o_ref):
    j = pl.program_id(1)

    @pl.when(j < S5_T)
    def _():
        acc = _dot(u_ref[:, 0:S5_WIDTH], d_ref[j + S5_T - 1])
        for s in range(1, S5_T):
            acc = acc + _dot(u_ref[:, s * S5_WIDTH:(s + 1) * S5_WIDTH], d_ref[j - s + S5_T - 1])
        o_ref[...] = acc

    @pl.when(j >= S5_T)
    def _():
        o_ref[...] = _dot(u_ref[...], w_ref[...])


def _s5_in(u, d_lag, w_in, tm):
    m, k = u.shape
    tn = S5_WIDTH
    nj = (k + w_in.shape[1]) // tn
    return pl.pallas_call(
        _s5_in_kernel,
        out_shape=jax.ShapeDtypeStruct((m, nj * tn), F32),
        grid=(m // tm, nj),
        in_specs=[pl.BlockSpec((tm, k), lambda i, j: (i, 0)),
                  pl.BlockSpec(d_lag.shape, lambda i, j: (0, 0, 0)),
                  pl.BlockSpec((k, tn), lambda i, j: (0, jnp.maximum(j - S5_T, 0)))],
        out_specs=pl.BlockSpec((tm, tn), lambda i, j: (i, j)),
        compiler_params=_params("arbitrary", "arbitrary"),
        name="s5_in",
    )(u, d_lag, w_in)


def _s5_scan_kernel(efr_ref, efi_ref, ebr_ref, ebi_ref, a_ref, hfr_ref, hfi_ref, hbr_ref, hbi_ref,
                    *, nb, rows_in, rows_out):
    dirs = ((efr_ref, efi_ref, hfr_ref, hfi_ref, a_ref[0, 0:1, :], a_ref[0, 1:2, :]),
            (ebr_ref, ebi_ref, hbr_ref, hbi_ref, a_ref[1, 0:1, :], a_ref[1, 1:2, :]))
    zero = jnp.zeros((nb, dirs[0][4].shape[1]), F32)

    def step(srcs, carry, store):
        new = []
        for di, (er_ref, ei_ref, hr_ref, hi_ref, are, aim) in enumerate(dirs):
            hre, him = carry[2 * di], carry[2 * di + 1]
            if store:
                hr_ref[pl.ds(srcs[di], nb, stride=rows_out), :] = hre
                hi_ref[pl.ds(srcs[di], nb, stride=rows_out), :] = him
            ere = er_ref[pl.ds(srcs[di], nb, stride=rows_in), :]
            eim = ei_ref[pl.ds(srcs[di], nb, stride=rows_in), :]
            new += [are * hre - aim * him + ere, are * him + aim * hre + eim]
        return tuple(new)

    n_ctx = rows_in - rows_out
    carry = lax.fori_loop(0, n_ctx, lambda s, c: step((rows_out + s, rows_in - 1 - s), c, False),
                          tuple([zero] * 4))

    def two_steps(s2, c):
        c = step((2 * s2, rows_out - 1 - 2 * s2), c, True)
        return step((2 * s2 + 1, rows_out - 2 - 2 * s2), c, True)

    lax.fori_loop(0, rows_out // 2, two_steps, carry)


def _s5_scan(e, decay, nb, rows_in, rows_out):
    assert rows_out % 2 == 0, "the latent chunks are scanned two per loop iteration"
    tc = LANES
    nsr = S5_GROUPS * S5_STATE
    c0 = (S5_T * S5_WIDTH) // tc
    nt = nsr // tc
    eblk = lambda k: pl.BlockSpec((nb * rows_in, tc), lambda j: (0, c0 + k * nt + j))
    hblk = pl.BlockSpec((nb * rows_out, tc), lambda j: (0, j))
    return pl.pallas_call(
        functools.partial(_s5_scan_kernel, nb=nb, rows_in=rows_in, rows_out=rows_out),
        out_shape=[jax.ShapeDtypeStruct((nb * rows_out, nsr), F32)] * 4,
        grid=(nt,),
        in_specs=[eblk(0), eblk(1), eblk(2), eblk(3), pl.BlockSpec((2, 2, tc), lambda j: (0, 0, j))],
        out_specs=[hblk] * 4,
        compiler_params=_params("arbitrary"),
        name="s5_scan",
    )(e, e, e, e, decay)


def _gelu_tanh(x):
    return 0.5 * x * (1.0 + jnp.tanh(math.sqrt(2.0 / math.pi) * (x + 0.044715 * x * x * x)))


def _s5_out_kernel(hfr_ref, hfi_ref, hbr_ref, hbi_ref, wf_ref, wb_ref, yi_ref, u_ref, d_ref, wg_ref, o_ref):
    nsr = hfr_ref.shape[1]
    y = yi_ref[0] + d_ref[...] * u_ref[...].astype(F32)
    for h_ref, w_ref, r0 in ((hfr_ref, wf_ref, 0), (hfi_ref, wf_ref, nsr), (hbr_ref, wb_ref, 0), (hbi_ref, wb_ref, nsr)):
        y = y + _dot(h_ref[...].astype(BF16), w_ref[r0:r0 + nsr, :])
    y = _gelu_tanh(y)
    gate = _sigmoid(_dot(y.astype(BF16), wg_ref[...]))
    o_ref[...] = (y * gate).astype(o_ref.dtype)


def _s5_out(states, w_out_f, w_out_b, e3, u_rows, d_row, w_glu):
    m, nsr = states[0].shape
    nb = e3.shape[0]
    tm = m // nb
    tn = S5_WIDTH
    st = pl.BlockSpec((tm, nsr), lambda i, j: (i, 0))
    wo = pl.BlockSpec((2 * nsr, tn), lambda i, j: (0, j))
    return pl.pallas_call(
        _s5_out_kernel,
        out_shape=jax.ShapeDtypeStruct((m, S5_T * S5_WIDTH), BF16),
        grid=(nb, S5_T),
        in_specs=[st, st, st, st, wo, wo,
                  pl.BlockSpec((1, tm, tn), lambda i, j: (i, 0, j)),
                  pl.BlockSpec((tm, tn), lambda i, j: (i, j)),
                  pl.BlockSpec((1, tn), lambda i, j: (0, 0)),
                  pl.BlockSpec((tn, tn), lambda i, j: (0, 0))],
        out_specs=pl.BlockSpec((tm, tn), lambda i, j: (i, j)),
        compiler_params=_params("arbitrary", "arbitrary"),
        name="s5_out",
    )(*states, w_out_f, w_out_b, e3, u_rows, d_row, w_glu)


U32 = jnp.uint32
ROW_SUB = 4


def _to_token_rows(ref, val):
    t, d = val.shape
    bits = lax.bitcast_convert_type(val.astype(BF16).astype(F32), U32)
    w = (bits[:, :d // 2] >> 16) | bits[:, d // 2:]
    for s in range(ROW_SUB):
        ref[pl.ds(s, t, stride=ROW_SUB), :] = w[:, s * LANES:(s + 1) * LANES]


def _from_token_rows(ref, t, row0=0):
    w = jnp.concatenate([ref[pl.ds(row0 * ROW_SUB + s, t, stride=ROW_SUB), :] for s in range(ROW_SUB)], axis=-1)
    lo = lax.bitcast_convert_type(w << 16, F32)
    hi = lax.bitcast_convert_type(w & jnp.uint32(0xFFFF0000), F32)
    return jnp.concatenate([lo, hi], axis=-1)


def _route(h2b, wr_ref, br_ref, cnt_ref, ls8_ref, w8_ref, seg_ref):
    tm = h2b.shape[0]
    per_group = N_EXPERTS // ROUTE_GROUPS
    scores = _sigmoid(lax.dot_general(wr_ref[...], h2b, _NT, preferred_element_type=F32))
    biased = scores + br_ref[...]
    neg = -jnp.inf
    sub = lax.broadcasted_iota(jnp.int32, (per_group, tm), 0)
    grp = []
    for gi in range(ROUTE_GROUPS):
        v = biased[gi * per_group:(gi + 1) * per_group, :]
        m1 = jnp.max(v, axis=0, keepdims=True)
        first = jnp.min(jnp.where(v == m1, sub, per_group), axis=0, keepdims=True)
        m2 = jnp.max(jnp.where(sub == first, neg, v), axis=0, keepdims=True)
        grp.append(m1 + m2)
    grp = jnp.concatenate(grp, axis=0)
    gid = lax.broadcasted_iota(jnp.int32, (ROUTE_GROUPS, tm), 0)
    beaten = jnp.zeros((ROUTE_GROUPS, tm), jnp.int32)
    for gj in range(ROUTE_GROUPS):
        r = grp[gj:gj + 1, :]
        beaten = beaten + jnp.where((r > grp) | ((r == grp) & (gj < gid)), 1, 0)
    group_ok = beaten < TOPK_GROUPS
    expert_ok = jnp.concatenate(
        [jnp.broadcast_to(group_ok[gi:gi + 1, :], (per_group, tm)) for gi in range(ROUTE_GROUPS)], axis=0)
    cur = jnp.where(expert_ok, biased, neg)
    eid = lax.broadcasted_iota(jnp.int32, (N_EXPERTS, tm), 0)
    sel = jnp.zeros((N_EXPERTS, tm), F32)
    picks, wts = [], []
    for _ in range(TOP_K):
        m = jnp.max(cur, axis=0, keepdims=True)
        idx = jnp.min(jnp.where(cur == m, eid, N_EXPERTS), axis=0, keepdims=True)
        hit = eid == idx
        picks.append(idx)
        wts.append(jnp.sum(jnp.where(hit, scores, 0.0), axis=0, keepdims=True))
        sel = jnp.where(hit, 1.0, sel)
        cur = jnp.where(hit, neg, cur)
    wsum = wts[0]
    for w in wts[1:]:
        wsum = wsum + w
    selb = sel.astype(BF16)
    ti = lax.broadcasted_iota(jnp.int32, (tm, tm), 0)
    tj = lax.broadcasted_iota(jnp.int32, (tm, tm), 1)
    rank = _dot(selb, jnp.where(ti < tj, 1.0, 0.0).astype(BF16))
    seg_units = jnp.ceil(jnp.sum(sel, axis=1, keepdims=True) * (1.0 / SEG_ALIGN))
    ei = lax.broadcasted_iota(jnp.int32, (N_EXPERTS, N_EXPERTS), 0)
    ej = lax.broadcasted_iota(jnp.int32, (N_EXPERTS, N_EXPERTS), 1)
    units_row = jnp.broadcast_to(seg_units, (N_EXPERTS, LANES)).astype(BF16)
    seg_off = _dot(jnp.where(ej < ei, 1.0, 0.0).astype(BF16), units_row)[:, 0:1] * SEG_ALIGN
    seg_rows = seg_units * SEG_ALIGN
    slot = seg_off + rank
    for k in range(TOP_K):
        w8_ref[k:k + 1, :] = wts[k] / wsum * ROUTED_SCALE
        ls8_ref[k:k + 1, :] = (jnp.sum(jnp.where(eid == picks[k], slot, 0.0), axis=0, keepdims=True)
                               * ROW_SUB).astype(jnp.int32)
    lane = lax.broadcasted_iota(jnp.int32, (N_EXPERTS, LANES), 1)
    seg_ref[0] = jnp.where(lane == 0, cnt_ref[...], jnp.where(lane == 1, seg_rows, seg_off))
    cnt_ref[...] = cnt_ref[...] + seg_rows


def _merge_kernel(x_ref, ya_ref, on_ref, go_ref, ga_ref, gb_ref, g1_ref, sc_ref, sh_ref, n2_ref,
                  pa_ref, pb_ref, wo_ref, wr_ref, br_ref, pt_ref,
                  x1_ref, h2_ref, ls8_ref, w8_ref, seg_ref, cnt_ref, fold_a, fold_b):
    @pl.when(pl.program_id(0) == 0)
    def _():
        cnt_ref[...] = jnp.zeros_like(cnt_ref)

    go = go_ref[...].astype(F32)
    on = _dot(pt_ref[...], on_ref[0].reshape(x_ref.shape[0], HG_WIDTH))
    y_b = (on * (go * _sigmoid(go))).astype(BF16)
    y_a = _unfold_rows(ya_ref[...].astype(F32), fold_a, fold_b).astype(BF16)
    pa = _dot(y_a, pa_ref[...])
    pb = _dot(y_b, pb_ref[...])
    merged = _sigmoid(ga_ref[...].astype(F32)) * pa + _sigmoid(gb_ref[...].astype(F32)) * pb
    x1 = x_ref[...] + g1_ref[0] * _dot(merged.astype(BF16), wo_ref[...])
    x1_ref[...] = x1
    y = x1 * lax.rsqrt(jnp.mean(x1 * x1, axis=-1, keepdims=True) + EPS) * n2_ref[...]
    h2 = y * (1.0 + sc_ref[0]) + sh_ref[0]
    _to_token_rows(h2_ref, h2)
    _route(h2.astype(BF16), wr_ref, br_ref, cnt_ref, ls8_ref, w8_ref, seg_ref)


def _merge(x2d, ya, on, go, ga, gb, g1, sc2, sh2, n2g, pa, pb, wo, wr_t, br, rows_per_batch, tm):
    n, d = x2d.shape
    per = rows_per_batch // tm
    row = lambda wd: pl.BlockSpec((tm, wd), lambda i: (i, 0))
    mod = pl.BlockSpec((1, 1, d), lambda i: (i // per, 0, 0))
    full = lambda a: pl.BlockSpec(a.shape, lambda i: (0, 0))
    tok = pl.BlockSpec((TOP_K, tm), lambda i: (0, i))
    return pl.pallas_call(
        _merge_kernel,
        out_shape=[jax.ShapeDtypeStruct((n, d), F32), jax.ShapeDtypeStruct((n * ROW_SUB, LANES), U32),
                   jax.ShapeDtypeStruct((TOP_K, n), jnp.int32), jax.ShapeDtypeStruct((TOP_K, n), F32),
                   jax.ShapeDtypeStruct((n // tm, N_EXPERTS, LANES), F32),
                   jax.ShapeDtypeStruct((N_EXPERTS, 1), F32)],
        grid=(n // tm,),
        in_specs=[row(d), pl.BlockSpec((tm // S5_T, S5_T * S5_WIDTH), lambda i: (i, 0)),
                  pl.BlockSpec((1, GRID_W, tm // GRID_W, HG_WIDTH), lambda i: (i // per, 0, i % per, 0)),
                  row(HG_WIDTH), row(d), row(d), mod, mod, mod,
                  pl.BlockSpec((1, d), lambda i: (0, 0)), full(pa), full(pb), full(wo), full(wr_t), full(br),
                  pl.BlockSpec((tm, tm), lambda i: (0, 0))],
        out_specs=[row(d), pl.BlockSpec((tm * ROW_SUB, LANES), lambda i: (i, 0)), tok, tok,
                   pl.BlockSpec((1, N_EXPERTS, LANES), lambda i: (i, 0, 0)),
                   pl.BlockSpec((N_EXPERTS, 1), lambda i: (0, 0))],
        scratch_shapes=[pltpu.VMEM((tm, LANES), F32), pltpu.VMEM((tm, LANES), F32)],
        compiler_params=_params("arbitrary"),
        name="merge_out_proj_route",
    )(x2d, ya, on, go, ga, gb, g1, sc2, sh2, n2g.reshape(1, d), pa, pb, wo, wr_t, br,
      _grid_transpose_matrix(tm).T)


MOE_TILE = TOK_TILE
TOKEN_UNROLL = 4
COMBINE_UNROLL = 8
SEG_ALIGN = 8
FILL_ROWS = 512
STAGE_ROWS = MOE_TILE * TOP_K + FILL_ROWS
MOE_BLK = 1024


def _wait_rows(any_ref, sem, n_rows):
    view = any_ref.at[pl.ds(0, n_rows * ROW_SUB)]
    pltpu.make_async_copy(view, view, sem).wait()


def _rows(ref, r0, n):
    return ref.at[pl.ds(pl.multiple_of(r0 * ROW_SUB, ROW_SUB), n * ROW_SUB)]


def _pow2_pieces(n, max_piece, fn, min_piece=1):
    done = 0
    piece = max_piece
    while piece >= min_piece:
        hit = (n & piece) != 0
        pl.when(hit)(functools.partial(fn, done, piece))
        done = done + (n & piece)
        piece //= 2


def _copy_rows(src_ref, src0, dst_ref, dst0, n, max_piece, sem, min_piece=1):
    def piece(off, size):
        pltpu.make_async_copy(_rows(src_ref, src0 + off, size), _rows(dst_ref, dst0 + off, size), sem).start()
    _pow2_pieces(n, max_piece, piece, min_piece)


def _wait_copied_rows(src_ref, dst_ref, n, max_piece, sem):
    def piece(off, size):
        pltpu.make_async_copy(_rows(src_ref, 0, size), _rows(dst_ref, 0, size), sem).wait()
    _pow2_pieces(n, max_piece, piece)


def _copy_tile_segments(i, src_ref, src_tab, dst_ref, dst_tab, cnt_ref, off_ref, fill_src0, fill_dst0, sem):
    def per_expert(e, carry):
        _copy_rows(src_ref, src_tab[i, e], dst_ref, dst_tab[i, e], cnt_ref[i, e], MOE_TILE, sem, SEG_ALIGN)
        return carry

    lax.fori_loop(0, N_EXPERTS, per_expert, 0)
    used = off_ref[i, N_EXPERTS - 1] + cnt_ref[i, N_EXPERTS - 1]
    _copy_rows(src_ref, fill_src0(used), dst_ref, fill_dst0(used), STAGE_ROWS - used, FILL_ROWS, sem, SEG_ALIGN)


def _dispatch_kernel(gs_ref, cnt_ref, off_ref, pad_ref, ls_ref, h2_ref, xs_hbm, ls_smem, stage0, stage1, zbuf,
                     sem0, sem1, lsem, zsem, *, tm, n_blocks):
    i = pl.program_id(0)
    last = pl.num_programs(0) - 1
    cp = pltpu.make_async_copy(ls_ref, ls_smem, lsem)
    cp.start()
    trash0 = n_blocks * MOE_BLK

    @pl.when(i == 0)
    def _():
        stage0[...] = jnp.zeros_like(stage0)
        stage1[...] = jnp.zeros_like(stage1)
        zbuf[...] = jnp.zeros_like(zbuf)
        cpz = pltpu.make_async_copy(zbuf, _rows(xs_hbm, trash0, 2 * FILL_ROWS), zsem)
        cpz.start()
        cpz.wait()

    cp.wait()

    def tile(stage, sem, prev_sem, trash):
        def body(tu, carry):
            for u in range(TOKEN_UNROLL):
                t = tu * TOKEN_UNROLL + u
                row = h2_ref[pl.ds(pl.multiple_of(t * ROW_SUB, ROW_SUB), ROW_SUB), :]
                for k in range(TOP_K):
                    stage[pl.ds(pl.multiple_of(ls_smem[t * TOP_K + k], ROW_SUB), ROW_SUB), :] = row
            return carry

        lax.fori_loop(0, tm // TOKEN_UNROLL, body, 0)
        _copy_tile_segments(i, stage, off_ref, xs_hbm, gs_ref, cnt_ref, off_ref,
                            lambda used: used, lambda used: trash, sem)

        @pl.when(i > 0)
        def _():
            _wait_rows(xs_hbm, prev_sem, STAGE_ROWS)

        @pl.when(i == last)
        def _():
            _wait_rows(xs_hbm, sem, STAGE_ROWS)

    pl.when(i % 2 == 0)(functools.partial(tile, stage0, sem0, sem1, trash0))
    pl.when(i % 2 == 1)(functools.partial(tile, stage1, sem1, sem0, trash0 + FILL_ROWS))

    def zero_pad(e, carry):
        _copy_rows(zbuf, 0, xs_hbm, pad_ref[0, e], pad_ref[1, e], MOE_BLK // 2, zsem)
        return carry

    def wait_pad(e, carry):
        _wait_copied_rows(zbuf, xs_hbm, pad_ref[1, e], MOE_BLK // 2, zsem)
        return carry

    def zero_block(j, carry):
        pltpu.make_async_copy(zbuf, _rows(xs_hbm, j * MOE_BLK, MOE_BLK), zsem).start()
        return carry

    def wait_block(j, carry):
        pltpu.make_async_copy(zbuf, _rows(xs_hbm, 0, MOE_BLK), zsem).wait()
        return carry

    @pl.when(i == 0)
    def _():
        lax.fori_loop(0, N_EXPERTS, zero_pad, 0)
        lax.fori_loop(pad_ref[2, 0], n_blocks, zero_block, 0)

    @pl.when(i == last)
    def _():
        lax.fori_loop(0, N_EXPERTS, wait_pad, 0)
        lax.fori_loop(pad_ref[2, 0], n_blocks, wait_block, 0)


def _dispatch(gstart, seg_cnt, seg_off, pad, ls8, h2_rows, n_blocks, tm):
    n = ls8.shape[0] // TOP_K
    cap = n_blocks * MOE_BLK + 2 * FILL_ROWS
    return pl.pallas_call(
        functools.partial(_dispatch_kernel, tm=tm, n_blocks=n_blocks),
        out_shape=jax.ShapeDtypeStruct((cap * ROW_SUB, LANES), U32),
        grid_spec=pltpu.PrefetchScalarGridSpec(
            num_scalar_prefetch=4,
            grid=(n // tm,),
            in_specs=[pl.BlockSpec((tm * TOP_K,), lambda i, *_: (i,)),
                      pl.BlockSpec((tm * ROW_SUB, LANES), lambda i, *_: (i, 0))],
            out_specs=pl.BlockSpec(memory_space=pl.ANY),
            scratch_shapes=[pltpu.SMEM((tm * TOP_K,), jnp.int32),
                            pltpu.VMEM((STAGE_ROWS * ROW_SUB, LANES), U32),
                            pltpu.VMEM((STAGE_ROWS * ROW_SUB, LANES), U32),
                            pltpu.VMEM((MOE_BLK * ROW_SUB, LANES), U32),
                            pltpu.SemaphoreType.DMA, pltpu.SemaphoreType.DMA, pltpu.SemaphoreType.DMA,
                            pltpu.SemaphoreType.DMA]),
        compiler_params=pltpu.CompilerParams(dimension_semantics=("arbitrary",), vmem_limit_bytes=VMEM_LIMIT,
                                             has_side_effects=True),
        name="moe_dispatch",
    )(gstart, seg_cnt, seg_off, pad, ls8, h2_rows)


def _expert_kernel(be_ref, nu_ref, x_ref, w1_ref, w3_ref, w2_ref, o_ref, w1b, w3b, w2b):
    j = pl.program_id(0)
    e = be_ref[j]
    prev = be_ref[jnp.maximum(j - 1, 0)]
    used = j < nu_ref[0]

    @pl.when(jnp.logical_and(used, jnp.logical_or(j == 0, e != prev)))
    def _():
        w1b[...] = w1_ref[0].astype(BF16)
        w3b[...] = w3_ref[0].astype(BF16)
        w2b[...] = w2_ref[0].astype(BF16)

    @pl.when(used)
    def _():
        x = _from_token_rows(x_ref, MOE_BLK).astype(BF16)
        a = _dot(x, w1b[...])
        hid = (a * _sigmoid(a)) * _dot(x, w3b[...])
        _to_token_rows(o_ref, _dot(hid.astype(BF16), w2b[...]))

    @pl.when(jnp.logical_not(used))
    def _():
        o_ref[...] = jnp.zeros_like(o_ref)


def _experts(block_e, n_used, xs, w1, w3, w2):
    n_blocks = block_e.shape[0]
    d, f = w1.shape[1], w1.shape[2]
    rows = pl.BlockSpec((MOE_BLK * ROW_SUB, LANES), lambda j, be, nu: (j, 0))
    rows_in = pl.BlockSpec((MOE_BLK * ROW_SUB, LANES), lambda j, be, nu: (jnp.minimum(j, nu[0] - 1), 0))
    return pl.pallas_call(
        _expert_kernel,
        out_shape=jax.ShapeDtypeStruct((n_blocks * MOE_BLK * ROW_SUB, LANES), U32),
        grid_spec=pltpu.PrefetchScalarGridSpec(
            num_scalar_prefetch=2,
            grid=(n_blocks,),
            in_specs=[rows_in,
                      pl.BlockSpec((1, d, f), lambda j, be, nu: (be[j], 0, 0)),
                      pl.BlockSpec((1, d, f), lambda j, be, nu: (be[j], 0, 0)),
                      pl.BlockSpec((1, f, d), lambda j, be, nu: (be[j], 0, 0))],
            out_specs=rows,
            scratch_shapes=[pltpu.VMEM((d, f), BF16), pltpu.VMEM((d, f), BF16), pltpu.VMEM((f, d), BF16)]),
        compiler_params=_params("arbitrary"),
        name="moe_experts",
    )(block_e, n_used, xs, w1, w3, w2)


def _combine_kernel(gs_ref, cnt_ref, off_ref, ls_ref, w8_ref, x1_ref, h2_ref, g2_ref, ws1_ref, ws3_ref, ws2_ref,
                    fg_ref, ys_hbm, o_ref, ls_smem, w_smem, gbuf0, gbuf1, acc_rows, sem0, sem1, lsem, *, tm):
    i = pl.program_id(0)
    last = pl.num_programs(0) - 1
    cp1 = pltpu.make_async_copy(ls_ref, ls_smem, lsem)
    cp2 = pltpu.make_async_copy(w8_ref, w_smem, lsem)
    cp1.start()
    cp2.start()

    def fetch(tile, gbuf, sem):
        _copy_tile_segments(tile, ys_hbm, gs_ref, gbuf, off_ref, cnt_ref, off_ref,
                            lambda used: 0, lambda used: used, sem)

    @pl.when(i == 0)
    def _():
        fetch(0, gbuf0, sem0)

    @pl.when(jnp.logical_and(i < last, i % 2 == 0))
    def _():
        fetch(i + 1, gbuf1, sem1)

    @pl.when(jnp.logical_and(i < last, i % 2 == 1))
    def _():
        fetch(i + 1, gbuf0, sem0)

    h2 = _from_token_rows(h2_ref, tm).astype(BF16)
    a = _dot(h2, ws1_ref[...])
    hid = (a * _sigmoid(a)) * _dot(h2, ws3_ref[...])
    acc = _dot(hid.astype(BF16), ws2_ref[...])
    cp1.wait()
    cp2.wait()

    def reduce_rows(gbuf, sem):
        _wait_rows(gbuf, sem, STAGE_ROWS)

        def body(tu, carry):
            for u in range(COMBINE_UNROLL):
                t = tu * COMBINE_UNROLL + u
                lo = jnp.zeros((ROW_SUB, LANES), F32)
                hi = jnp.zeros((ROW_SUB, LANES), F32)
                for k in range(TOP_K):
                    w = w_smem[t * TOP_K + k]
                    words = gbuf[pl.ds(pl.multiple_of(ls_smem[t * TOP_K + k], ROW_SUB), ROW_SUB), :]
                    lo = lo + w * lax.bitcast_convert_type(words << 16, F32)
                    hi = hi + w * lax.bitcast_convert_type(words & jnp.uint32(0xFFFF0000), F32)
                acc_rows[pl.ds(pl.multiple_of(t * SUBLANES, SUBLANES), ROW_SUB), :] = lo
                acc_rows[pl.ds(pl.multiple_of(t * SUBLANES, SUBLANES) + ROW_SUB, ROW_SUB), :] = hi
            return carry

        lax.fori_loop(0, tm // COMBINE_UNROLL, body, 0)

    pl.when(i % 2 == 0)(functools.partial(reduce_rows, gbuf0, sem0))
    pl.when(i % 2 == 1)(functools.partial(reduce_rows, gbuf1, sem1))
    routed = jnp.concatenate([acc_rows[pl.ds(s, tm, stride=SUBLANES), :] for s in range(SUBLANES)], axis=-1)
    y = x1_ref[...] + g2_ref[0] * (acc + routed)
    o_ref[...] = y * lax.rsqrt(jnp.mean(y * y, axis=-1, keepdims=True) + EPS) * fg_ref[...]


def _combine(gstart, seg_cnt, seg_off, ls8, w8, x1, h2_rows, g2, ws1, ws3, ws2, fg, ys, rows_per_batch, tm):
    n, d = x1.shape
    per = rows_per_batch // tm
    tok = pl.BlockSpec((tm * TOP_K,), lambda i, *_: (i,))
    full = lambda a: pl.BlockSpec(a.shape, lambda i, *_: (0, 0))
    return pl.pallas_call(
        functools.partial(_combine_kernel, tm=tm),
        out_shape=jax.ShapeDtypeStruct((n, d), F32),
        grid_spec=pltpu.PrefetchScalarGridSpec(
            num_scalar_prefetch=3,
            grid=(n // tm,),
            in_specs=[tok, tok, pl.BlockSpec((tm, d), lambda i, *_: (i, 0)),
                      pl.BlockSpec((tm * ROW_SUB, LANES), lambda i, *_: (i, 0)),
                      pl.BlockSpec((1, 1, d), lambda i, *_: (i // per, 0, 0)),
                      full(ws1), full(ws3), full(ws2), pl.BlockSpec((1, d), lambda i, *_: (0, 0)),
                      pl.BlockSpec(memory_space=pl.ANY)],
            out_specs=pl.BlockSpec((tm, d), lambda i, *_: (i, 0)),
            scratch_shapes=[pltpu.SMEM((tm * TOP_K,), jnp.int32), pltpu.SMEM((tm * TOP_K,), F32),
                            pltpu.VMEM((STAGE_ROWS * ROW_SUB, LANES), U32),
                            pltpu.VMEM((STAGE_ROWS * ROW_SUB, LANES), U32),
                            pltpu.VMEM((tm * SUBLANES, LANES), F32), pltpu.SemaphoreType.DMA,
                            pltpu.SemaphoreType.DMA, pltpu.SemaphoreType.DMA]),
        compiler_params=_params("arbitrary"),
        name="moe_combine_final",
    )(gstart, seg_cnt, seg_off, ls8, w8, x1, h2_rows, g2, ws1, ws3, ws2, fg.reshape(1, d), ys)


def _moe_plan(seg, counts, n_assign):
    cnt = counts.reshape(N_EXPERTS).astype(jnp.int32)
    padded = (cnt + MOE_BLK - 1) // MOE_BLK * MOE_BLK
    pends = jnp.cumsum(padded)
    pstarts = pends - padded
    max_rows = n_assign + seg.shape[0] * N_EXPERTS * (SEG_ALIGN - 1)
    n_blocks = (max_rows + N_EXPERTS * (MOE_BLK - 1) + MOE_BLK - 1) // MOE_BLK
    seg = seg[:, :, :3].astype(jnp.int32)
    gstart = pstarts[None, :] + seg[:, :, 0]
    blk_start = jnp.arange(n_blocks, dtype=jnp.int32) * MOE_BLK
    block_e = jnp.minimum(jnp.sum((blk_start[:, None] >= pends[None, :]).astype(jnp.int32), axis=1),
                          N_EXPERTS - 1).astype(jnp.int32)
    n_used = (pends[-1:] // MOE_BLK).astype(jnp.int32)
    pad = jnp.stack([pstarts + cnt, padded - cnt, jnp.broadcast_to(n_used, (N_EXPERTS,))], axis=0).astype(jnp.int32)
    return gstart, seg[:, :, 1], seg[:, :, 2], pad, block_e, n_used, n_blocks


def _mixer(x, c, ctx, c_ctx, w_ada, b_ada, norm1_g, norm2_g, w_in, s5_lam_re, s5_lam_im, s5_log_dt,
           s5_b_re, s5_b_im, s5_c_re, s5_c_im, s5_d, s5_w_glu, lb, hg_norm_g, p_a, p_b, w_out,
           moe_w_router, moe_b_router):
    b, l, d = x.shape
    lc = ctx.shape[1]
    n = b * l

    c8 = jnp.concatenate([c, c_ctx[None], jnp.zeros((8 - b - 1, d), F32)], axis=0)
    mod = _ada(c8, w_ada, b_ada)
    sh1, sc1, g1, sh2, sc2, g2 = [mod[:b, k * d:(k + 1) * d].reshape(b, 1, d) for k in range(6)]
    csh1, csc1 = mod[b:b + 1, 0:d].reshape(1, 1, d), mod[b:b + 1, d:2 * d].reshape(1, 1, d)

    w_in_b = w_in.astype(BF16)
    z = dict(zip([p[0] for p in _IN_PIECES],
                 _inproj(x.reshape(n, d), sc1, sh1, norm1_g, w_in_b, l, TOK_TILE, True)))
    zc = dict(zip([p[0] for p in _IN_PIECES],
                  _inproj(ctx.reshape(b * lc, d), csc1, csh1, norm1_g, w_in_b, lc, lc, False)))

    cx = lambda t: t.reshape(b, lc, HG_WIDTH)
    lb_row = lb.reshape(1, HG_WIDTH)
    o_f = _hgrn_pass(z["q"], z["ff"], z["i"], cx(zc["ff"]), cx(zc["i"]), lb_row, None, None, reverse=False)
    o_n = _hgrn_pass(z["q"], z["fb"], z["i"], cx(zc["fb"]), cx(zc["i"]), lb_row, o_f,
                     hg_norm_g.reshape(1, HG_DK), reverse=True)

    d_lag, w_s5_in, w_out_f, w_out_b, decay = _s5_weights(s5_lam_re, s5_lam_im, s5_log_dt, s5_b_re, s5_b_im,
                                                          s5_c_re, s5_c_im)
    kc, kl = lc // S5_T, l // S5_T
    u_lat = z["u"].reshape(b, kl, S5_T * S5_WIDTH)
    u_ctx = zc["u"].reshape(b, kc, S5_T * S5_WIDTH)
    rows_in = kl + kc
    u_ext = jnp.concatenate([u_lat, u_ctx], axis=1).reshape(b * rows_in, S5_T * S5_WIDTH)
    e = _s5_in(u_ext, d_lag, w_s5_in, (b * rows_in) // 2)
    states = _s5_scan(e, decay, b, rows_in, kl)
    d_row = s5_d.astype(F32).reshape(1, S5_WIDTH)
    y_a = _s5_out(states, w_out_f, w_out_b, e.reshape(b, rows_in, -1), z["u"], d_row, s5_w_glu.astype(BF16))

    return _merge(x.reshape(n, d), y_a, o_n, z["go"], z["ga"], z["gb"], g1, sc2, sh2, norm2_g,
                  p_a.astype(BF16), p_b.astype(BF16), w_out.astype(BF16),
                  moe_w_router.T.astype(BF16), moe_b_router.astype(F32).reshape(N_EXPERTS, 1), l, MOE_TILE) + (g2,)


def kernel(x, c, ctx, c_ctx, w_ada, b_ada, norm1_g, norm2_g, w_in, s5_lam_re, s5_lam_im, s5_log_dt, s5_b_re,
           s5_b_im, s5_c_re, s5_c_im, s5_d, s5_w_glu, hg_lb_logits, hg_norm_g, p_a, p_b, w_out, moe_w_router,
           moe_b_router, moe_w1, moe_w3, moe_w2, moe_ws1, moe_ws3, moe_ws2, final_norm_g):
    b, l, d = x.shape
    n = b * l
    assert w_ada.shape[0] == 1, "single-layer block"
    lb = jnp.cumsum(jax.nn.softmax(hg_lb_logits.astype(F32), axis=0), axis=0)[0]
    x1, h2_rows, ls8, w8, seg, counts, g2 = _mixer(
        x, c, ctx, c_ctx, w_ada[0], b_ada[0], norm1_g[0], norm2_g[0], w_in[0], s5_lam_re[0], s5_lam_im[0],
        s5_log_dt[0], s5_b_re[0], s5_b_im[0], s5_c_re[0], s5_c_im[0], s5_d[0], s5_w_glu[0], lb, hg_norm_g[0],
        p_a[0], p_b[0], w_out[0], moe_w_router[0], moe_b_router[0])
    gstart, seg_cnt, seg_off, pad, block_e, n_used, n_blocks = _moe_plan(seg, counts, n * TOP_K)
    ls_flat, w_flat = ls8.T.reshape(n * TOP_K), w8.T.reshape(n * TOP_K)
    xs = _dispatch(gstart, seg_cnt, seg_off, pad, ls_flat, h2_rows, n_blocks, MOE_TILE)
    ys = _experts(block_e, n_used, xs, moe_w1[0], moe_w3[0], moe_w2[0])
    out = _combine(gstart, seg_cnt, seg_off, ls_flat, w_flat, x1, h2_rows, g2, moe_ws1[0].astype(BF16),
                   moe_ws3[0].astype(BF16), moe_ws2[0].astype(BF16), final_norm_g, ys, l, MOE_TILE)
    return out.reshape(b, l, d)
```

```python
import functools
import math

import jax
import jax.numpy as jnp
from jax import lax
from jax.experimental import pallas as pl
from jax.experimental.pallas import tpu as pltpu

F32 = jnp.float32
BF16 = jnp.bfloat16

GRID_W = 64
S5_WIDTH = 256
S5_GROUP = 16
S5_GROUPS = 16
S5_STATE = 64
HG_HEADS = 6
HG_DK = 128
HG_WIDTH = HG_HEADS * HG_DK
N_EXPERTS = 64
ROUTE_GROUPS = 8
TOPK_GROUPS = 4
TOP_K = 8
ROUTED_SCALE = 2.5
EPS = 1e-6

LANES = 128
SUBLANES = 8

TOK_TILE = 512
S5_T = 16
HG_CHUNK = 64
HG_BATCH = 4
VMEM_LIMIT = 56 * 1024 * 1024

_NT = (((1,), (1,)), ((), ()))
_TN = (((0,), (0,)), ((), ()))


def _params(*sem):
    return pltpu.CompilerParams(dimension_semantics=sem, vmem_limit_bytes=VMEM_LIMIT)


def _dot(a, b):
    return jnp.dot(a, b, preferred_element_type=F32)


def _sigmoid(x):
    return 0.5 * jnp.tanh(0.5 * x) + 0.5


def _ada_kernel(c_ref, w_ref, b_ref, o_ref):
    c = c_ref[...]
    s = (c * _sigmoid(c)).astype(BF16)
    o_ref[...] = _dot(s, w_ref[...].astype(BF16)) + b_ref[...]


def _ada(c8, w_ada, b_ada):
    d, n = w_ada.shape
    tn = 1536
    return pl.pallas_call(
        _ada_kernel,
        out_shape=jax.ShapeDtypeStruct((8, n), F32),
        grid=(n // tn,),
        in_specs=[pl.BlockSpec((8, d), lambda j: (0, 0)),
                  pl.BlockSpec((d, tn), lambda j: (0, j)),
                  pl.BlockSpec((1, tn), lambda j: (0, j))],
        out_specs=pl.BlockSpec((8, tn), lambda j: (0, j)),
        compiler_params=_params("arbitrary"),
        name="ada_mod",
    )(c8, w_ada, b_ada.reshape(1, n))


_IN_PIECES = (("u", 0, 256, BF16), ("q", 256, 768, BF16), ("ff", 1024, 768, BF16),
              ("fb", 1792, 768, BF16), ("i", 2560, 768, BF16), ("go", 3328, 768, BF16),
              ("ga", 4096, 1024, BF16), ("gb", 5120, 1024, BF16))


def _fold_rows(val, buf_a, buf_b):
    t = val.shape[0]
    buf_a[...] = val[:, :LANES]
    buf_b[...] = val[:, LANES:]
    pieces = []
    for s in range(S5_T):
        pieces += [buf_a[pl.ds(s, t // S5_T, stride=S5_T), :], buf_b[pl.ds(s, t // S5_T, stride=S5_T), :]]
    return jnp.concatenate(pieces, axis=-1)


def _unfold_rows(val, buf_a, buf_b):
    r = val.shape[0]
    for s in range(S5_T):
        buf_a[pl.ds(s, r, stride=S5_T), :] = val[:, s * S5_WIDTH:s * S5_WIDTH + LANES]
        buf_b[pl.ds(s, r, stride=S5_T), :] = val[:, s * S5_WIDTH + LANES:(s + 1) * S5_WIDTH]
    return jnp.concatenate([buf_a[...], buf_b[...]], axis=-1)


def _grid_transpose_matrix(tm):
    i = jnp.arange(tm)
    src = (i % (tm // GRID_W)) * GRID_W + i // (tm // GRID_W)
    return (src[:, None] == jnp.arange(tm)[None, :]).astype(BF16)


def _inproj_kernel(x_ref, sc_ref, sh_ref, g_ref, w_ref, p_ref, *o_refs):
    o_refs, (fold_a, fold_b) = o_refs[:len(_IN_PIECES)], o_refs[len(_IN_PIECES):]
    x = x_ref[...]
    y = x * lax.rsqrt(jnp.mean(x * x, axis=-1, keepdims=True) + EPS) * g_ref[...]
    h = (y * (1.0 + sc_ref[0]) + sh_ref[0]).astype(BF16)
    h_cm = None
    for (name, a, wd, _), o_ref in zip(_IN_PIECES, o_refs):
        if name == "u":
            o_ref[...] = _fold_rows(_dot(h, w_ref[:, a:a + wd]), fold_a, fold_b).astype(o_ref.dtype)
        elif len(o_ref.shape) == 2:
            o_ref[...] = _dot(h, w_ref[:, a:a + wd]).astype(o_ref.dtype)
        else:
            if h_cm is None:
                h_cm = _dot(p_ref[...], h).astype(BF16)
            o_ref[0] = _dot(h_cm, w_ref[:, a:a + wd]).astype(o_ref.dtype).reshape(o_ref.shape[1:])


_COLMAJOR_PIECES = ("q", "ff", "fb", "i")


def _inproj(x2d, sc, sh, g, w_bf16, rows_per_mod, tm, colmajor):
    n, d = x2d.shape
    per = rows_per_mod // tm
    mod_map = (lambda i: (i // per, 0, 0)) if sc.shape[0] > 1 else (lambda i: (0, 0, 0))
    shapes, specs = [], []
    for name, _, wd, dt in _IN_PIECES:
        if colmajor and name in _COLMAJOR_PIECES:
            shapes.append(jax.ShapeDtypeStruct((n // rows_per_mod, GRID_W, rows_per_mod // GRID_W, wd), dt))
            specs.append(pl.BlockSpec((1, GRID_W, tm // GRID_W, wd), lambda i: (i // per, 0, i % per, 0)))
        elif name == "u":
            shapes.append(jax.ShapeDtypeStruct((n // S5_T, S5_T * wd), dt))
            specs.append(pl.BlockSpec((tm // S5_T, S5_T * wd), lambda i: (i, 0)))
        else:
            shapes.append(jax.ShapeDtypeStruct((n, wd), dt))
            specs.append(pl.BlockSpec((tm, wd), lambda i: (i, 0)))
    return pl.pallas_call(
        _inproj_kernel,
        out_shape=shapes,
        grid=(n // tm,),
        in_specs=[pl.BlockSpec((tm, d), lambda i: (i, 0)),
                  pl.BlockSpec((1, 1, d), mod_map),
                  pl.BlockSpec((1, 1, d), mod_map),
                  pl.BlockSpec((1, d), lambda i: (0, 0)),
                  pl.BlockSpec(w_bf16.shape, lambda i: (0, 0)),
                  pl.BlockSpec((tm, tm), lambda i: (0, 0))],
        out_specs=specs,
        scratch_shapes=[pltpu.VMEM((tm, LANES), F32), pltpu.VMEM((tm, LANES), F32)],
        compiler_params=_params("arbitrary"),
        name="in_proj",
    )(x2d, sc, sh, g.reshape(1, d), w_bf16, _grid_transpose_matrix(tm))


def _hgrn_gates(zf, lb):
    sig = _sigmoid(zf)
    logf = jnp.log(lb + (1.0 - lb) * sig)
    k = (1.0 - lb) * (1.0 - sig)
    return logf, k


def _chunk_cumsum(cs, logf):
    hi = logf.astype(BF16)
    lo = (logf - hi.astype(F32)).astype(BF16)
    return _dot(cs, hi) + _dot(cs, lo)


def _hgrn_state_step(zf, v, lb, st, cs, reverse):
    logf, k = _hgrn_gates(zf, lb)
    cum = _chunk_cumsum(cs, logf)
    t = 0 if reverse else HG_CHUNK - 1
    total = cum[t:t + 1, :]
    kdec = (k * jnp.exp(total - cum)).astype(BF16)
    st_new = st * jnp.exp(total) + lax.dot_general(v.astype(BF16), kdec, _TN, preferred_element_type=F32)
    return cum, k, st_new


def _hgrn_kernel(*refs, reverse, final, n_ctx_chunks):
    if final:
        q_all, f_all, v_all, cf_ref, cv_ref, lb_ref, of_all, g_ref, o_all, st_ref = refs
    else:
        q_all, f_all, v_all, cf_ref, cv_ref, lb_ref, o_all, st_ref = refs
        of_all = None
    n_batch = q_all.shape[0]
    c_len = HG_CHUNK
    n_rows = q_all.shape[2]
    n_chunks = n_rows // c_len
    row = lax.broadcasted_iota(jnp.int32, (n_rows, n_rows), 0)
    col = lax.broadcasted_iota(jnp.int32, (n_rows, n_rows), 1)
    tri = (col >= row) if reverse else (col <= row)
    same_chunk = None
    for c in range(n_chunks):
        lo, hi = c * c_len, (c + 1) * c_len
        blk = (row >= lo) & (row < hi) & (col >= lo) & (col < hi)
        same_chunk = blk if same_chunk is None else (same_chunk | blk)
    mask = tri & same_chunk
    cs = jnp.where(mask, 1.0, 0.0).astype(BF16)

    @pl.when(pl.program_id(1) == 0)
    def _():
        cs1 = cs[:c_len, :c_len]
        order = range(n_ctx_chunks - 1, -1, -1) if reverse else range(n_ctx_chunks)
        for bi in range(n_batch):
            for h in range(HG_HEADS):
                cols = slice(h * HG_DK, (h + 1) * HG_DK)
                st = jnp.zeros((HG_DK, HG_DK), F32)
                for c in order:
                    rows = slice(c * c_len, (c + 1) * c_len)
                    _, _, st = _hgrn_state_step(cf_ref[bi, rows, cols].astype(F32), cv_ref[bi, rows, cols].astype(F32),
                                                lb_ref[:, cols], st, cs1, reverse)
                st_ref[bi * HG_HEADS + h] = st

    def chunk_rows(x, r):
        return [x[c * c_len + r:c * c_len + r + 1, :] for c in range(n_chunks)]

    def over_chunks(rows):
        return jnp.concatenate([jnp.broadcast_to(r, (c_len, r.shape[1])) for r in rows], axis=0)

    lb = lb_ref[...]
    r_ref = c_len // 2 - 1 if reverse else c_len // 2
    r_tot = 0 if reverse else c_len - 1
    order = range(n_chunks - 1, -1, -1) if reverse else range(n_chunks)
    for bi in range(n_batch):
        q = q_all[bi, 0].astype(F32)
        v = v_all[bi, 0]
        logf, k = _hgrn_gates(f_all[bi, 0].astype(F32), lb)
        cum = _chunk_cumsum(cs, logf)
        ref_rows, tot_rows = chunk_rows(cum, r_ref), chunk_rows(cum, r_tot)
        ref = over_chunks(ref_rows)
        qe = q * jnp.exp(cum - ref)
        ke = k * jnp.exp(ref - cum)
        qi, ki = qe.astype(BF16), ke.astype(BF16)
        q_in = (qe * over_chunks([jnp.exp(r) for r in ref_rows])).astype(BF16)
        kdec = (ke * over_chunks([jnp.exp(t - r) for t, r in zip(tot_rows, ref_rows)])).astype(BF16)
        for h in range(HG_HEADS):
            cols = slice(h * HG_DK, (h + 1) * HG_DK)
            s = lax.dot_general(qi[:, cols], ki[:, cols], _NT, preferred_element_type=F32)
            o_intra = _dot(jnp.where(mask, s, 0.0).astype(BF16), v[:, cols])
            st = st_ref[bi * HG_HEADS + h]
            for c in order:
                rows = slice(c * c_len, (c + 1) * c_len)
                o = o_intra[rows] + lax.dot_general(q_in[rows, cols], st.astype(BF16), _NT,
                                                    preferred_element_type=F32)
                total = cum[c * c_len + r_tot:c * c_len + r_tot + 1, cols]
                st = st * jnp.exp(total) + lax.dot_general(v[rows, cols], kdec[rows, cols], _TN,
                                                           preferred_element_type=F32)
                if final:
                    o = o + of_all[bi, 0, rows, cols].astype(F32)
                    o = o * lax.rsqrt(jnp.mean(o * o, axis=-1, keepdims=True) + EPS) * g_ref[...]
                o_all[bi, 0, rows, cols] = o.astype(o_all.dtype)
            st_ref[bi * HG_HEADS + h] = st


def _hgrn_pass(q, f, v, cf, cv, lb, o_prev, g, *, reverse):
    b, nw, rows, _ = q.shape
    nb = HG_BATCH if b % HG_BATCH == 0 else 1
    final = o_prev is not None
    wmap = (lambda bi, w: (bi, nw - 1 - w, 0, 0)) if reverse else (lambda bi, w: (bi, w, 0, 0))
    blk = pl.BlockSpec((nb, 1, rows, HG_WIDTH), wmap)
    cblk = pl.BlockSpec((nb, cf.shape[1], HG_WIDTH), lambda bi, w: (bi, 0, 0))
    in_specs = [blk, blk, blk, cblk, cblk, pl.BlockSpec((1, HG_WIDTH), lambda bi, w: (0, 0))]
    args = [q, f, v, cf, cv, lb]
    if final:
        in_specs += [blk, pl.BlockSpec((1, HG_DK), lambda bi, w: (0, 0))]
        args += [o_prev, g]
    return pl.pallas_call(
        functools.partial(_hgrn_kernel, reverse=reverse, final=final, n_ctx_chunks=cf.shape[1] // HG_CHUNK),
        out_shape=jax.ShapeDtypeStruct(q.shape, BF16),
        grid=(b // nb, nw),
        in_specs=in_specs,
        out_specs=blk,
        scratch_shapes=[pltpu.VMEM((nb * HG_HEADS, HG_DK, HG_DK), F32)],
        compiler_params=_params("arbitrary", "arbitrary"),
        name="hgrn_bwd" if reverse else "hgrn_fwd",
    )(*args)


def _s5_weights(lam_re, lam_im, log_dt, b_re, b_im, c_re, c_im):
    g, p, cc, t = S5_GROUPS, S5_STATE, S5_GROUP, S5_T
    lre = jnp.minimum(lam_re.astype(F32), -1e-4)
    lim = lam_im.astype(F32)
    dt = jnp.exp(log_dt.astype(F32))[..., None]
    ks = jnp.arange(t + 1, dtype=F32)[:, None, None, None]
    mag = jnp.exp(ks * (lre * dt)[None])
    pw_re = mag * jnp.cos(ks * (lim * dt)[None])
    pw_im = mag * jnp.sin(ks * (lim * dt)[None])
    nr, ni = pw_re[1] - 1.0, pw_im[1]
    den = lre * lre + lim * lim
    cf_re = (nr * lre + ni * lim) / den
    cf_im = (ni * lre - nr * lim) / den
    bb_re = cf_re[..., None] * b_re - cf_im[..., None] * b_im
    bb_im = cf_re[..., None] * b_im + cf_im[..., None] * b_re
    cre, cim = c_re.astype(F32), c_im.astype(F32)
    sw, ns = S5_WIDTH, 2 * g * p
    grp_of_row = jnp.arange(sw)[:, None] // cc

    qr = pw_re[:t].transpose(1, 2, 0, 3)[:, :, :, None, :]
    qi = pw_im[:t].transpose(1, 2, 0, 3)[:, :, :, None, :]
    cp_re = (cre[None, :, None] * qr - cim[None, :, None] * qi).reshape(2, g, t * cc, p)
    cp_im = (cre[None, :, None] * qi + cim[None, :, None] * qr).reshape(2, g, t * cc, p)

    def contract_p(cp, bb):
        return lax.dot_general(cp, bb, (((3,), (2,)), ((0, 1), (0, 1))), precision=lax.Precision.HIGHEST,
                               preferred_element_type=F32)

    kk = (contract_p(cp_re, bb_re) - contract_p(cp_im, bb_im)).reshape(2, g, t, cc, cc).transpose(0, 2, 1, 3, 4)
    kf, kb = kk[0], kk[1]
    kall = jnp.concatenate([kb[:0:-1], (kf[0] + kb[0])[None], kf[1:]], axis=0)
    kt = kall.transpose(0, 1, 3, 2).reshape(2 * t - 1, sw, cc)

    def spread(x, period, reps):
        sel = (jnp.arange(period)[:, None] == (jnp.arange(period * reps)[None, :] % period)).astype(BF16)
        return jnp.dot(x.astype(BF16), sel, preferred_element_type=BF16)

    same = grp_of_row == (jnp.arange(sw)[None, :] // cc)
    d_lag = jnp.where(same[None], spread(kt, cc, g), 0)

    same_in = jnp.tile(grp_of_row, (t, 1)) == ((jnp.arange(ns)[None, :] % (g * p)) // p)

    def in_to_state(pre, pim, bre, bim):
        xre = pre[..., None] * bre[None] - pim[..., None] * bim[None]
        xim = pre[..., None] * bim[None] + pim[..., None] * bre[None]
        return [spread(xre.transpose(0, 1, 3, 2).reshape(t * sw, p), p, g),
                spread(xim.transpose(0, 1, 3, 2).reshape(t * sw, p), p, g)]

    w_in = jnp.concatenate(in_to_state(pw_re[t - 1::-1, 0], pw_im[t - 1::-1, 0], bb_re[0], bb_im[0])
                           + in_to_state(pw_re[:t, 1], pw_im[:t, 1], bb_re[1], bb_im[1]), axis=1)
    w_in = jnp.where(jnp.tile(same_in, (1, 2)), w_in, 0)

    same_out = ((jnp.arange(ns)[:, None] % (g * p)) // p) == ((jnp.arange(t * sw)[None, :] // cc) % g)
    col = jnp.arange(t * sw)
    pick = (jnp.arange(t * cc)[:, None] == ((col // sw) * cc + col % cc)[None, :]).astype(BF16)

    def state_to_out(pre, pim):
        are = cre[None] * pre[:, :, None, :] - cim[None] * pim[:, :, None, :]
        aim = cre[None] * pim[:, :, None, :] + cim[None] * pre[:, :, None, :]
        a = jnp.concatenate([are.transpose(1, 3, 0, 2), -aim.transpose(1, 3, 0, 2)], axis=0)
        a = jnp.dot(a.reshape(ns, t * cc).astype(BF16), pick, preferred_element_type=BF16)
        return jnp.where(same_out, a, 0)

    w_out_f = state_to_out(pw_re[1:, 0], pw_im[1:, 0])
    w_out_b = state_to_out(pw_re[t:0:-1, 1], pw_im[t:0:-1, 1])

    decay = jnp.stack([pw_re[t].reshape(2, g * p), pw_im[t].reshape(2, g * p)], axis=1)
    return d_lag, w_in, w_out_f, w_out_b, decay


def _s5_in_kernel(ul_ref, uc_ref, d_ref, w_ref, o_ref, u_ref):
    j = pl.program_id(1)

    @pl.when(j == 0)
    def _():
        kl, kc = ul_ref.shape[1], uc_ref.shape[1]
        for bi in range(ul_ref.shape[0]):
            u_ref[bi * (kl + kc):bi * (kl + kc) + kl, :] = ul_ref[bi]
            u_ref[bi * (kl + kc) + kl:(bi + 1) * (kl + kc), :] = uc_ref[bi]

    @pl.when(j < S5_T)
    def _():
        acc = _dot(u_ref[:, 0:S5_WIDTH], d_ref[j + S5_T - 1])
        for s in range(1, S5_T):
            acc = acc + _dot(u_ref[:, s * S5_WIDTH:(s + 1) * S5_WIDTH], d_ref[j - s + S5_T - 1])
        o_ref[...] = acc

    @pl.when(j >= S5_T)
    def _():
        o_ref[...] = _dot(u_ref[...], w_ref[...])


def _s5_in(u_lat, u_ctx, d_lag, w_in, nb_tile):
    b, kl, k = u_lat.shape
    kc = u_ctx.shape[1]
    tm = nb_tile * (kl + kc)
    tn = S5_WIDTH
    nj = (k + w_in.shape[1]) // tn
    return pl.pallas_call(
        _s5_in_kernel,
        out_shape=jax.ShapeDtypeStruct((b * (kl + kc), nj * tn), F32),
        grid=(b // nb_tile, nj),
        in_specs=[pl.BlockSpec((nb_tile, kl, k), lambda i, j: (i, 0, 0)),
                  pl.BlockSpec((nb_tile, kc, k), lambda i, j: (i, 0, 0)),
                  pl.BlockSpec(d_lag.shape, lambda i, j: (0, 0, 0)),
                  pl.BlockSpec((k, tn), lambda i, j: (0, jnp.maximum(j - S5_T, 0)))],
        out_specs=pl.BlockSpec((tm, tn), lambda i, j: (i, j)),
        scratch_shapes=[pltpu.VMEM((tm, k), BF16)],
        compiler_params=_params("arbitrary", "arbitrary"),
        name="s5_in",
    )(u_lat, u_ctx, d_lag, w_in)


def _s5_scan_kernel(efr_ref, efi_ref, ebr_ref, ebi_ref, a_ref, hfr_ref, hfi_ref, hbr_ref, hbi_ref,
                    *, nb, rows_in, rows_out):
    dirs = ((efr_ref, efi_ref, hfr_ref, hfi_ref, a_ref[0, 0:1, :], a_ref[0, 1:2, :]),
            (ebr_ref, ebi_ref, hbr_ref, hbi_ref, a_ref[1, 0:1, :], a_ref[1, 1:2, :]))
    zero = jnp.zeros((nb, dirs[0][4].shape[1]), F32)

    def step(srcs, carry, store):
        new = []
        for di, (er_ref, ei_ref, hr_ref, hi_ref, are, aim) in enumerate(dirs):
            hre, him = carry[2 * di], carry[2 * di + 1]
            if store:
                hr_ref[pl.ds(srcs[di], nb, stride=rows_out), :] = hre
                hi_ref[pl.ds(srcs[di], nb, stride=rows_out), :] = him
            ere = er_ref[pl.ds(srcs[di], nb, stride=rows_in), :]
            eim = ei_ref[pl.ds(srcs[di], nb, stride=rows_in), :]
            new += [are * hre - aim * him + ere, are * him + aim * hre + eim]
        return tuple(new)

    n_ctx = rows_in - rows_out
    carry = lax.fori_loop(0, n_ctx, lambda s, c: step((rows_out + s, rows_in - 1 - s), c, False),
                          tuple([zero] * 4))

    def two_steps(s2, c):
        c = step((2 * s2, rows_out - 1 - 2 * s2), c, True)
        return step((2 * s2 + 1, rows_out - 2 - 2 * s2), c, True)

    lax.fori_loop(0, rows_out // 2, two_steps, carry)


def _s5_scan(e, decay, nb, rows_in, rows_out):
    assert rows_out % 2 == 0, "the latent chunks are scanned two per loop iteration"
    tc = LANES
    nsr = S5_GROUPS * S5_STATE
    c0 = (S5_T * S5_WIDTH) // tc
    nt = nsr // tc
    eblk = lambda k: pl.BlockSpec((nb * rows_in, tc), lambda j: (0, c0 + k * nt + j))
    hblk = pl.BlockSpec((nb * rows_out, tc), lambda j: (0, j))
    return pl.pallas_call(
        functools.partial(_s5_scan_kernel, nb=nb, rows_in=rows_in, rows_out=rows_out),
        out_shape=[jax.ShapeDtypeStruct((nb * rows_out, nsr), F32)] * 4,
        grid=(nt,),
        in_specs=[eblk(0), eblk(1), eblk(2), eblk(3), pl.BlockSpec((2, 2, tc), lambda j: (0, 0, j))],
        out_specs=[hblk] * 4,
        compiler_params=_params("arbitrary"),
        name="s5_scan",
    )(e, e, e, e, decay)


def _gelu_tanh(x):
    return 0.5 * x * (1.0 + jnp.tanh(math.sqrt(2.0 / math.pi) * (x + 0.044715 * x * x * x)))


def _s5_out_kernel(hfr_ref, hfi_ref, hbr_ref, hbi_ref, wf_ref, wb_ref, yi_ref, u_ref, d_ref, wg_ref, o_ref):
    nsr = hfr_ref.shape[1]
    y = yi_ref[0] + d_ref[...] * u_ref[...].astype(F32)
    for h_ref, w_ref, r0 in ((hfr_ref, wf_ref, 0), (hfi_ref, wf_ref, nsr), (hbr_ref, wb_ref, 0), (hbi_ref, wb_ref, nsr)):
        y = y + _dot(h_ref[...].astype(BF16), w_ref[r0:r0 + nsr, :])
    y = _gelu_tanh(y)
    gate = _sigmoid(_dot(y.astype(BF16), wg_ref[...]))
    o_ref[...] = (y * gate).astype(o_ref.dtype)


def _s5_out(states, w_out_f, w_out_b, e3, u_rows, d_row, w_glu):
    m, nsr = states[0].shape
    nb = e3.shape[0]
    tm = m // nb
    tn = S5_WIDTH
    st = pl.BlockSpec((tm, nsr), lambda i, j: (i, 0))
    wo = pl.BlockSpec((2 * nsr, tn), lambda i, j: (0, j))
    return pl.pallas_call(
        _s5_out_kernel,
        out_shape=jax.ShapeDtypeStruct((m, S5_T * S5_WIDTH), BF16),
        grid=(nb, S5_T),
        in_specs=[st, st, st, st, wo, wo,
                  pl.BlockSpec((1, tm, tn), lambda i, j: (i, 0, j)),
                  pl.BlockSpec((tm, tn), lambda i, j: (i, j)),
                  pl.BlockSpec((1, tn), lambda i, j: (0, 0)),
                  pl.BlockSpec((tn, tn), lambda i, j: (0, 0))],
        out_specs=pl.BlockSpec((tm, tn), lambda i, j: (i, j)),
        compiler_params=_params("arbitrary", "arbitrary"),
        name="s5_out",
    )(*states, w_out_f, w_out_b, e3, u_rows, d_row, w_glu)


U32 = jnp.uint32
ROW_SUB = 4


def _to_token_rows(ref, val):
    t, d = val.shape
    bits = lax.bitcast_convert_type(val.astype(BF16).astype(F32), U32)
    w = (bits[:, :d // 2] >> 16) | bits[:, d // 2:]
    for s in range(ROW_SUB):
        ref[pl.ds(s, t, stride=ROW_SUB), :] = w[:, s * LANES:(s + 1) * LANES]


def _from_token_rows(ref, t, row0=0):
    w = jnp.concatenate([ref[pl.ds(row0 * ROW_SUB + s, t, stride=ROW_SUB), :] for s in range(ROW_SUB)], axis=-1)
    lo = lax.bitcast_convert_type(w << 16, F32)
    hi = lax.bitcast_convert_type(w & jnp.uint32(0xFFFF0000), F32)
    return jnp.concatenate([lo, hi], axis=-1)


def _route(h2b, wr_ref, br_ref, cnt_ref, ls8_ref, w8_ref, seg_ref):
    tm = h2b.shape[0]
    per_group = N_EXPERTS // ROUTE_GROUPS
    scores = _sigmoid(lax.dot_general(wr_ref[...], h2b, _NT, preferred_element_type=F32))
    biased = scores + br_ref[...]
    neg = -jnp.inf
    sub = lax.broadcasted_iota(jnp.int32, (per_group, tm), 0)
    grp = []
    for gi in range(ROUTE_GROUPS):
        v = biased[gi * per_group:(gi + 1) * per_group, :]
        m1 = jnp.max(v, axis=0, keepdims=True)
        first = jnp.min(jnp.where(v == m1, sub, per_group), axis=0, keepdims=True)
        m2 = jnp.max(jnp.where(sub == first, neg, v), axis=0, keepdims=True)
        grp.append(m1 + m2)
    grp = jnp.concatenate(grp, axis=0)
    gid = lax.broadcasted_iota(jnp.int32, (ROUTE_GROUPS, tm), 0)
    beaten = jnp.zeros((ROUTE_GROUPS, tm), jnp.int32)
    for gj in range(ROUTE_GROUPS):
        r = grp[gj:gj + 1, :]
        beaten = beaten + jnp.where((r > grp) | ((r == grp) & (gj < gid)), 1, 0)
    group_ok = beaten < TOPK_GROUPS
    expert_ok = jnp.concatenate(
        [jnp.broadcast_to(group_ok[gi:gi + 1, :], (per_group, tm)) for gi in range(ROUTE_GROUPS)], axis=0)
    cur = jnp.where(expert_ok, biased, neg)
    eid = lax.broadcasted_iota(jnp.int32, (N_EXPERTS, tm), 0)
    sel = jnp.zeros((N_EXPERTS, tm), F32)
    picks, wts = [], []
    for _ in range(TOP_K):
        m = jnp.max(cur, axis=0, keepdims=True)
        idx = jnp.min(jnp.where(cur == m, eid, N_EXPERTS), axis=0, keepdims=True)
        hit = eid == idx
        picks.append(idx)
        wts.append(jnp.sum(jnp.where(hit, scores, 0.0), axis=0, keepdims=True))
        sel = jnp.where(hit, 1.0, sel)
        cur = jnp.where(hit, neg, cur)
    wsum = wts[0]
    for w in wts[1:]:
        wsum = wsum + w
    selb = sel.astype(BF16)
    ti = lax.broadcasted_iota(jnp.int32, (tm, tm), 0)
    tj = lax.broadcasted_iota(jnp.int32, (tm, tm), 1)
    rank = _dot(selb, jnp.where(ti < tj, 1.0, 0.0).astype(BF16))
    seg_units = jnp.ceil(jnp.sum(sel, axis=1, keepdims=True) * (1.0 / SEG_ALIGN))
    ei = lax.broadcasted_iota(jnp.int32, (N_EXPERTS, N_EXPERTS), 0)
    ej = lax.broadcasted_iota(jnp.int32, (N_EXPERTS, N_EXPERTS), 1)
    units_row = jnp.broadcast_to(seg_units, (N_EXPERTS, LANES)).astype(BF16)
    seg_off = _dot(jnp.where(ej < ei, 1.0, 0.0).astype(BF16), units_row)[:, 0:1] * SEG_ALIGN
    seg_rows = seg_units * SEG_ALIGN
    slot = seg_off + rank
    for k in range(TOP_K):
        w8_ref[k:k + 1, :] = wts[k] / wsum * ROUTED_SCALE
        ls8_ref[k:k + 1, :] = (jnp.sum(jnp.where(eid == picks[k], slot, 0.0), axis=0, keepdims=True)
                               * ROW_SUB).astype(jnp.int32)
    lane = lax.broadcasted_iota(jnp.int32, (N_EXPERTS, LANES), 1)
    seg_ref[0] = jnp.where(lane == 0, cnt_ref[...], jnp.where(lane == 1, seg_rows, seg_off))
    cnt_ref[...] = cnt_ref[...] + seg_rows


def _merge_kernel(x_ref, ya_ref, on_ref, go_ref, ga_ref, gb_ref, g1_ref, sc_ref, sh_ref, n2_ref,
                  pa_ref, pb_ref, wo_ref, wr_ref, br_ref, pt_ref,
                  x1_ref, h2_ref, ls8_ref, w8_ref, seg_ref, cnt_ref, fold_a, fold_b):
    @pl.when(pl.program_id(0) == 0)
    def _():
        cnt_ref[...] = jnp.zeros_like(cnt_ref)

    go = go_ref[...].astype(F32)
    on = _dot(pt_ref[...], on_ref[0].reshape(x_ref.shape[0], HG_WIDTH))
    y_b = (on * (go * _sigmoid(go))).astype(BF16)
    y_a = _unfold_rows(ya_ref[...].astype(F32), fold_a, fold_b).astype(BF16)
    pa = _dot(y_a, pa_ref[...])
    pb = _dot(y_b, pb_ref[...])
    merged = _sigmoid(ga_ref[...].astype(F32)) * pa + _sigmoid(gb_ref[...].astype(F32)) * pb
    x1 = x_ref[...] + g1_ref[0] * _dot(merged.astype(BF16), wo_ref[...])
    x1_ref[...] = x1
    y = x1 * lax.rsqrt(jnp.mean(x1 * x1, axis=-1, keepdims=True) + EPS) * n2_ref[...]
    h2 = y * (1.0 + sc_ref[0]) + sh_ref[0]
    _to_token_rows(h2_ref, h2)
    _route(h2.astype(BF16), wr_ref, br_ref, cnt_ref, ls8_ref, w8_ref, seg_ref)


def _merge(x2d, ya, on, go, ga, gb, g1, sc2, sh2, n2g, pa, pb, wo, wr_t, br, rows_per_batch, tm):
    n, d = x2d.shape
    per = rows_per_batch // tm
    row = lambda wd: pl.BlockSpec((tm, wd), lambda i: (i, 0))
    mod = pl.BlockSpec((1, 1, d), lambda i: (i // per, 0, 0))
    full = lambda a: pl.BlockSpec(a.shape, lambda i: (0, 0))
    tok = pl.BlockSpec((TOP_K, tm), lambda i: (0, i))
    return pl.pallas_call(
        _merge_kernel,
        out_shape=[jax.ShapeDtypeStruct((n, d), F32), jax.ShapeDtypeStruct((n * ROW_SUB, LANES), U32),
                   jax.ShapeDtypeStruct((TOP_K, n), jnp.int32), jax.ShapeDtypeStruct((TOP_K, n), F32),
                   jax.ShapeDtypeStruct((n // tm, N_EXPERTS, LANES), F32),
                   jax.ShapeDtypeStruct((N_EXPERTS, 1), F32)],
        grid=(n // tm,),
        in_specs=[row(d), pl.BlockSpec((tm // S5_T, S5_T * S5_WIDTH), lambda i: (i, 0)),
                  pl.BlockSpec((1, GRID_W, tm // GRID_W, HG_WIDTH), lambda i: (i // per, 0, i % per, 0)),
                  row(HG_WIDTH), row(d), row(d), mod, mod, mod,
                  pl.BlockSpec((1, d), lambda i: (0, 0)), full(pa), full(pb), full(wo), full(wr_t), full(br),
                  pl.BlockSpec((tm, tm), lambda i: (0, 0))],
        out_specs=[row(d), pl.BlockSpec((tm * ROW_SUB, LANES), lambda i: (i, 0)), tok, tok,
                   pl.BlockSpec((1, N_EXPERTS, LANES), lambda i: (i, 0, 0)),
                   pl.BlockSpec((N_EXPERTS, 1), lambda i: (0, 0))],
        scratch_shapes=[pltpu.VMEM((tm, LANES), F32), pltpu.VMEM((tm, LANES), F32)],
        compiler_params=_params("arbitrary"),
        name="merge_out_proj_route",
    )(x2d, ya, on, go, ga, gb, g1, sc2, sh2, n2g.reshape(1, d), pa, pb, wo, wr_t, br,
      _grid_transpose_matrix(tm).T)


MOE_TILE = TOK_TILE
TOKEN_UNROLL = 4
COMBINE_UNROLL = 8
SEG_ALIGN = 8
FILL_ROWS = 512
STAGE_ROWS = MOE_TILE * TOP_K + FILL_ROWS
MOE_BLK = 1024


def _wait_rows(any_ref, sem, n_rows):
    view = any_ref.at[pl.ds(0, n_rows * ROW_SUB)]
    pltpu.make_async_copy(view, view, sem).wait()


def _rows(ref, r0, n):
    return ref.at[pl.ds(pl.multiple_of(r0 * ROW_SUB, ROW_SUB), n * ROW_SUB)]


def _pow2_pieces(n, max_piece, fn, min_piece=1):
    done = 0
    piece = max_piece
    while piece >= min_piece:
        hit = (n & piece) != 0
        pl.when(hit)(functools.partial(fn, done, piece))
        done = done + (n & piece)
        piece //= 2


def _copy_rows(src_ref, src0, dst_ref, dst0, n, max_piece, sem, min_piece=1):
    def piece(off, size):
        pltpu.make_async_copy(_rows(src_ref, src0 + off, size), _rows(dst_ref, dst0 + off, size), sem).start()
    _pow2_pieces(n, max_piece, piece, min_piece)


def _wait_copied_rows(src_ref, dst_ref, n, max_piece, sem):
    def piece(off, size):
        pltpu.make_async_copy(_rows(src_ref, 0, size), _rows(dst_ref, 0, size), sem).wait()
    _pow2_pieces(n, max_piece, piece)


def _copy_tile_segments(i, src_ref, src_tab, dst_ref, dst_tab, cnt_ref, off_ref, fill_src0, fill_dst0, sem):
    def per_expert(e, carry):
        _copy_rows(src_ref, src_tab[i, e], dst_ref, dst_tab[i, e], cnt_ref[i, e], MOE_TILE, sem, SEG_ALIGN)
        return carry

    lax.fori_loop(0, N_EXPERTS, per_expert, 0)
    used = off_ref[i, N_EXPERTS - 1] + cnt_ref[i, N_EXPERTS - 1]
    _copy_rows(src_ref, fill_src0(used), dst_ref, fill_dst0(used), STAGE_ROWS - used, FILL_ROWS, sem, SEG_ALIGN)


def _dispatch_kernel(gs_ref, cnt_ref, off_ref, pad_ref, ls_ref, h2_ref, xs_hbm, ls_smem, stage0, stage1, zbuf,
                     sem0, sem1, lsem, zsem, *, tm, n_blocks):
    i = pl.program_id(0)
    last = pl.num_programs(0) - 1
    cp = pltpu.make_async_copy(ls_ref, ls_smem, lsem)
    cp.start()
    trash0 = n_blocks * MOE_BLK

    @pl.when(i == 0)
    def _():
        stage0[...] = jnp.zeros_like(stage0)
        stage1[...] = jnp.zeros_like(stage1)
        zbuf[...] = jnp.zeros_like(zbuf)
        cpz = pltpu.make_async_copy(zbuf, _rows(xs_hbm, trash0, 2 * FILL_ROWS), zsem)
        cpz.start()
        cpz.wait()

    cp.wait()

    def tile(stage, sem, prev_sem, trash):
        def body(tu, carry):
            for u in range(TOKEN_UNROLL):
                t = tu * TOKEN_UNROLL + u
                row = h2_ref[pl.ds(pl.multiple_of(t * ROW_SUB, ROW_SUB), ROW_SUB), :]
                for k in range(TOP_K):
                    stage[pl.ds(pl.multiple_of(ls_smem[t * TOP_K + k], ROW_SUB), ROW_SUB), :] = row
            return carry

        lax.fori_loop(0, tm // TOKEN_UNROLL, body, 0)
        _copy_tile_segments(i, stage, off_ref, xs_hbm, gs_ref, cnt_ref, off_ref,
                            lambda used: used, lambda used: trash, sem)

        @pl.when(i > 0)
        def _():
            _wait_rows(xs_hbm, prev_sem, STAGE_ROWS)

        @pl.when(i == last)
        def _():
            _wait_rows(xs_hbm, sem, STAGE_ROWS)

    pl.when(i % 2 == 0)(functools.partial(tile, stage0, sem0, sem1, trash0))
    pl.when(i % 2 == 1)(functools.partial(tile, stage1, sem1, sem0, trash0 + FILL_ROWS))

    def zero_pad(e, carry):
        _copy_rows(zbuf, 0, xs_hbm, pad_ref[0, e], pad_ref[1, e], MOE_BLK // 2, zsem)
        return carry

    def wait_pad(e, carry):
        _wait_copied_rows(zbuf, xs_hbm, pad_ref[1, e], MOE_BLK // 2, zsem)
        return carry

    def zero_block(j, carry):
        pltpu.make_async_copy(zbuf, _rows(xs_hbm, j * MOE_BLK, MOE_BLK), zsem).start()
        return carry

    def wait_block(j, carry):
        pltpu.make_async_copy(zbuf, _rows(xs_hbm, 0, MOE_BLK), zsem).wait()
        return carry

    @pl.when(i == 0)
    def _():
        lax.fori_loop(0, N_EXPERTS, zero_pad, 0)
        lax.fori_loop(pad_ref[2, 0], n_blocks, zero_block, 0)

    @pl.when(i == last)
    def _():
        lax.fori_loop(0, N_EXPERTS, wait_pad, 0)
        lax.fori_loop(pad_ref[2, 0], n_blocks, wait_block, 0)


def _dispatch(gstart, seg_cnt, seg_off, pad, ls8, h2_rows, n_blocks, tm):
    n = ls8.shape[0] // TOP_K
    cap = n_blocks * MOE_BLK + 2 * FILL_ROWS
    return pl.pallas_call(
        functools.partial(_dispatch_kernel, tm=tm, n_blocks=n_blocks),
        out_shape=jax.ShapeDtypeStruct((cap * ROW_SUB, LANES), U32),
        grid_spec=pltpu.PrefetchScalarGridSpec(
            num_scalar_prefetch=4,
            grid=(n // tm,),
            in_specs=[pl.BlockSpec((tm * TOP_K,), lambda i, *_: (i,)),
                      pl.BlockSpec((tm * ROW_SUB, LANES), lambda i, *_: (i, 0))],
            out_specs=pl.BlockSpec(memory_space=pl.ANY),
            scratch_shapes=[pltpu.SMEM((tm * TOP_K,), jnp.int32),
                            pltpu.VMEM((STAGE_ROWS * ROW_SUB, LANES), U32),
                            pltpu.VMEM((STAGE_ROWS * ROW_SUB, LANES), U32),
                            pltpu.VMEM((MOE_BLK * ROW_SUB, LANES), U32),
                            pltpu.SemaphoreType.DMA, pltpu.SemaphoreType.DMA, pltpu.SemaphoreType.DMA,
                            pltpu.SemaphoreType.DMA]),
        compiler_params=pltpu.CompilerParams(dimension_semantics=("arbitrary",), vmem_limit_bytes=VMEM_LIMIT,
                                             has_side_effects=True),
        name="moe_dispatch",
    )(gstart, seg_cnt, seg_off, pad, ls8, h2_rows)


def _expert_kernel(be_ref, nu_ref, x_ref, w1_ref, w3_ref, w2_ref, o_ref, w1b, w3b, w2b):
    j = pl.program_id(0)
    e = be_ref[j]
    prev = be_ref[jnp.maximum(j - 1, 0)]
    used = j < nu_ref[0]

    @pl.when(jnp.logical_and(used, jnp.logical_or(j == 0, e != prev)))
    def _():
        w1b[...] = w1_ref[0].astype(BF16)
        w3b[...] = w3_ref[0].astype(BF16)
        w2b[...] = w2_ref[0].astype(BF16)

    @pl.when(used)
    def _():
        x = _from_token_rows(x_ref, MOE_BLK).astype(BF16)
        a = _dot(x, w1b[...])
        hid = (a * _sigmoid(a)) * _dot(x, w3b[...])
        _to_token_rows(o_ref, _dot(hid.astype(BF16), w2b[...]))

    @pl.when(jnp.logical_not(used))
    def _():
        o_ref[...] = jnp.zeros_like(o_ref)


def _experts(block_e, n_used, xs, w1, w3, w2):
    n_blocks = block_e.shape[0]
    d, f = w1.shape[1], w1.shape[2]
    rows = pl.BlockSpec((MOE_BLK * ROW_SUB, LANES), lambda j, be, nu: (j, 0))
    rows_in = pl.BlockSpec((MOE_BLK * ROW_SUB, LANES), lambda j, be, nu: (jnp.minimum(j, nu[0] - 1), 0))
    return pl.pallas_call(
        _expert_kernel,
        out_shape=jax.ShapeDtypeStruct((n_blocks * MOE_BLK * ROW_SUB, LANES), U32),
        grid_spec=pltpu.PrefetchScalarGridSpec(
            num_scalar_prefetch=2,
            grid=(n_blocks,),
            in_specs=[rows_in,
                      pl.BlockSpec((1, d, f), lambda j, be, nu: (be[j], 0, 0)),
                      pl.BlockSpec((1, d, f), lambda j, be, nu: (be[j], 0, 0)),
                      pl.BlockSpec((1, f, d), lambda j, be, nu: (be[j], 0, 0))],
            out_specs=rows,
            scratch_shapes=[pltpu.VMEM((d, f), BF16), pltpu.VMEM((d, f), BF16), pltpu.VMEM((f, d), BF16)]),
        compiler_params=_params("arbitrary"),
        name="moe_experts",
    )(block_e, n_used, xs, w1, w3, w2)


def _combine_kernel(gs_ref, cnt_ref, off_ref, ls_ref, w8_ref, x1_ref, h2_ref, g2_ref, ws1_ref, ws3_ref, ws2_ref,
                    fg_ref, ys_hbm, o_ref, ls_smem, w_smem, gbuf0, gbuf1, acc_rows, sem0, sem1, lsem, *, tm):
    i = pl.program_id(0)
    last = pl.num_programs(0) - 1
    cp1 = pltpu.make_async_copy(ls_ref, ls_smem, lsem)
    cp2 = pltpu.make_async_copy(w8_ref, w_smem, lsem)
    cp1.start()
    cp2.start()

    def fetch(tile, gbuf, sem):
        _copy_tile_segments(tile, ys_hbm, gs_ref, gbuf, off_ref, cnt_ref, off_ref,
                            lambda used: 0, lambda used: used, sem)

    @pl.when(i == 0)
    def _():
        fetch(0, gbuf0, sem0)

    @pl.when(jnp.logical_and(i < last, i % 2 == 0))
    def _():
        fetch(i + 1, gbuf1, sem1)

    @pl.when(jnp.logical_and(i < last, i % 2 == 1))
    def _():
        fetch(i + 1, gbuf0, sem0)

    h2 = _from_token_rows(h2_ref, tm).astype(BF16)
    a = _dot(h2, ws1_ref[...])
    hid = (a * _sigmoid(a)) * _dot(h2, ws3_ref[...])
    acc = _dot(hid.astype(BF16), ws2_ref[...])
    cp1.wait()
    cp2.wait()

    def reduce_rows(gbuf, sem):
        _wait_rows(gbuf, sem, STAGE_ROWS)

        def body(tu, carry):
            for u in range(COMBINE_UNROLL):
                t = tu * COMBINE_UNROLL + u
                lo = jnp.zeros((ROW_SUB, LANES), F32)
                hi = jnp.zeros((ROW_SUB, LANES), F32)
                for k in range(TOP_K):
                    w = w_smem[t * TOP_K + k]
                    words = gbuf[pl.ds(pl.multiple_of(ls_smem[t * TOP_K + k], ROW_SUB), ROW_SUB), :]
                    lo = lo + w * lax.bitcast_convert_type(words << 16, F32)
                    hi = hi + w * lax.bitcast_convert_type(words & jnp.uint32(0xFFFF0000), F32)
                acc_rows[pl.ds(pl.multiple_of(t * SUBLANES, SUBLANES), ROW_SUB), :] = lo
                acc_rows[pl.ds(pl.multiple_of(t * SUBLANES, SUBLANES) + ROW_SUB, ROW_SUB), :] = hi
            return carry

        lax.fori_loop(0, tm // COMBINE_UNROLL, body, 0)

    pl.when(i % 2 == 0)(functools.partial(reduce_rows, gbuf0, sem0))
    pl.when(i % 2 == 1)(functools.partial(reduce_rows, gbuf1, sem1))
    routed = jnp.concatenate([acc_rows[pl.ds(s, tm, stride=SUBLANES), :] for s in range(SUBLANES)], axis=-1)
    y = x1_ref[...] + g2_ref[0] * (acc + routed)
    o_ref[...] = y * lax.rsqrt(jnp.mean(y * y, axis=-1, keepdims=True) + EPS) * fg_ref[...]


def _combine(gstart, seg_cnt, seg_off, ls8, w8, x1, h2_rows, g2, ws1, ws3, ws2, fg, ys, rows_per_batch, tm):
    n, d = x1.shape
    per = rows_per_batch // tm
    tok = pl.BlockSpec((tm * TOP_K,), lambda i, *_: (i,))
    full = lambda a: pl.BlockSpec(a.shape, lambda i, *_: (0, 0))
    return pl.pallas_call(
        functools.partial(_combine_kernel, tm=tm),
        out_shape=jax.ShapeDtypeStruct((n, d), F32),
        grid_spec=pltpu.PrefetchScalarGridSpec(
            num_scalar_prefetch=3,
            grid=(n // tm,),
            in_specs=[tok, tok, pl.BlockSpec((tm, d), lambda i, *_: (i, 0)),
                      pl.BlockSpec((tm * ROW_SUB, LANES), lambda i, *_: (i, 0)),
                      pl.BlockSpec((1, 1, d), lambda i, *_: (i // per, 0, 0)),
                      full(ws1), full(ws3), full(ws2), pl.BlockSpec((1, d), lambda i, *_: (0, 0)),
                      pl.BlockSpec(memory_space=pl.ANY)],
            out_specs=pl.BlockSpec((tm, d), lambda i, *_: (i, 0)),
            scratch_shapes=[pltpu.SMEM((tm * TOP_K,), jnp.int32), pltpu.SMEM((tm * TOP_K,), F32),
                            pltpu.VMEM((STAGE_ROWS * ROW_SUB, LANES), U32),
                            pltpu.VMEM((STAGE_ROWS * ROW_SUB, LANES), U32),
                            pltpu.VMEM((tm * SUBLANES, LANES), F32), pltpu.SemaphoreType.DMA,
                            pltpu.SemaphoreType.DMA, pltpu.SemaphoreType.DMA]),
        compiler_params=_params("arbitrary"),
        name="moe_combine_final",
    )(gstart, seg_cnt, seg_off, ls8, w8, x1, h2_rows, g2, ws1, ws3, ws2, fg.reshape(1, d), ys)


def _moe_plan(seg, counts, n_assign):
    cnt = counts.reshape(N_EXPERTS).astype(jnp.int32)
    padded = (cnt + MOE_BLK - 1) // MOE_BLK * MOE_BLK
    pends = jnp.cumsum(padded)
    pstarts = pends - padded
    max_rows = n_assign + seg.shape[0] * N_EXPERTS * (SEG_ALIGN - 1)
    n_blocks = (max_rows + N_EXPERTS * (MOE_BLK - 1) + MOE_BLK - 1) // MOE_BLK
    seg = seg[:, :, :3].astype(jnp.int32)
    gstart = pstarts[None, :] + seg[:, :, 0]
    blk_start = jnp.arange(n_blocks, dtype=jnp.int32) * MOE_BLK
    block_e = jnp.minimum(jnp.sum((blk_start[:, None] >= pends[None, :]).astype(jnp.int32), axis=1),
                          N_EXPERTS - 1).astype(jnp.int32)
    n_used = (pends[-1:] // MOE_BLK).astype(jnp.int32)
    pad = jnp.stack([pstarts + cnt, padded - cnt, jnp.broadcast_to(n_used, (N_EXPERTS,))], axis=0).astype(jnp.int32)
    return gstart, seg[:, :, 1], seg[:, :, 2], pad, block_e, n_used, n_blocks


def _mixer(x, c, ctx, c_ctx, w_ada, b_ada, norm1_g, norm2_g, w_in, s5_lam_re, s5_lam_im, s5_log_dt,
           s5_b_re, s5_b_im, s5_c_re, s5_c_im, s5_d, s5_w_glu, lb, hg_norm_g, p_a, p_b, w_out,
           moe_w_router, moe_b_router):
    b, l, d = x.shape
    lc = ctx.shape[1]
    n = b * l

    c8 = jnp.concatenate([c, c_ctx[None], jnp.zeros((8 - b - 1, d), F32)], axis=0)
    mod = _ada(c8, w_ada, b_ada)
    sh1, sc1, g1, sh2, sc2, g2 = [mod[:b, k * d:(k + 1) * d].reshape(b, 1, d) for k in range(6)]
    csh1, csc1 = mod[b:b + 1, 0:d].reshape(1, 1, d), mod[b:b + 1, d:2 * d].reshape(1, 1, d)

    w_in_b = w_in.astype(BF16)
    z = dict(zip([p[0] for p in _IN_PIECES],
                 _inproj(x.reshape(n, d), sc1, sh1, norm1_g, w_in_b, l, TOK_TILE, True)))
    zc = dict(zip([p[0] for p in _IN_PIECES],
                  _inproj(ctx.reshape(b * lc, d), csc1, csh1, norm1_g, w_in_b, lc, lc, False)))

    cx = lambda t: t.reshape(b, lc, HG_WIDTH)
    lb_row = lb.reshape(1, HG_WIDTH)
    o_f = _hgrn_pass(z["q"], z["ff"], z["i"], cx(zc["ff"]), cx(zc["i"]), lb_row, None, None, reverse=False)
    o_n = _hgrn_pass(z["q"], z["fb"], z["i"], cx(zc["fb"]), cx(zc["i"]), lb_row, o_f,
                     hg_norm_g.reshape(1, HG_DK), reverse=True)

    d_lag, w_s5_in, w_out_f, w_out_b, decay = _s5_weights(s5_lam_re, s5_lam_im, s5_log_dt, s5_b_re, s5_b_im,
                                                          s5_c_re, s5_c_im)
    kc, kl = lc // S5_T, l // S5_T
    u_lat = z["u"].reshape(b, kl, S5_T * S5_WIDTH)
    u_ctx = zc["u"].reshape(b, kc, S5_T * S5_WIDTH)
    rows_in = kl + kc
    e = _s5_in(u_lat, u_ctx, d_lag, w_s5_in, 2 if b % 2 == 0 else 1)
    states = _s5_scan(e, decay, b, rows_in, kl)
    d_row = s5_d.astype(F32).reshape(1, S5_WIDTH)
    y_a = _s5_out(states, w_out_f, w_out_b, e.reshape(b, rows_in, -1), z["u"], d_row, s5_w_glu.astype(BF16))

    return _merge(x.reshape(n, d), y_a, o_n, z["go"], z["ga"], z["gb"], g1, sc2, sh2, norm2_g,
                  p_a.astype(BF16), p_b.astype(BF16), w_out.astype(BF16),
                  moe_w_router.T.astype(BF16), moe_b_router.astype(F32).reshape(N_EXPERTS, 1), l, MOE_TILE) + (g2,)


def kernel(x, c, ctx, c_ctx, w_ada, b_ada, norm1_g, norm2_g, w_in, s5_lam_re, s5_lam_im, s5_log_dt, s5_b_re,
           s5_b_im, s5_c_re, s5_c_im, s5_d, s5_w_glu, hg_lb_logits, hg_norm_g, p_a, p_b, w_out, moe_w_router,
           moe_b_router, moe_w1, moe_w3, moe_w2, moe_ws1, moe_ws3, moe_ws2, final_norm_g):
    b, l, d = x.shape
    n = b * l
    assert w_ada.shape[0] == 1, "single-layer block"
    lb = jnp.cumsum(jax.nn.softmax(hg_lb_logits.astype(F32), axis=0), axis=0)[0]
    x1, h2_rows, ls8, w8, seg, counts, g2 = _mixer(
        x, c, ctx, c_ctx, w_ada[0], b_ada[0], norm1_g[0], norm2_g[0], w_in[0], s5_lam_re[0], s5_lam_im[0],
        s5_log_dt[0], s5_b_re[0], s5_b_im[0], s5_c_re[0], s5_c_im[0], s5_d[0], s5_w_glu[0], lb, hg_norm_g[0],
        p_a[0], p_b[0], w_out[0], moe_w_router[0], moe_b_router[0])
    gstart, seg_cnt, seg_off, pad, block_e, n_used, n_blocks = _moe_plan(seg, counts, n * TOP_K)
    ls_flat, w_flat = ls8.T.reshape(n * TOP_K), w8.T.reshape(n * TOP_K)
    xs = _dispatch(gstart, seg_cnt, seg_off, pad, ls_flat, h2_rows, n_blocks, MOE_TILE)
    ys = _experts(block_e, n_used, xs, moe_w1[0], moe_w3[0], moe_w2[0])
    out = _combine(gstart, seg_cnt, seg_off, ls_flat, w_flat, x1, h2_rows, g2, moe_ws1[0].astype(BF16),
                   moe_ws3[0].astype(BF16), moe_ws2[0].astype(BF16), final_norm_g, ys, l, MOE_TILE)
    return out.reshape(b, l, d)
```

```python
import functools
import math

import jax
import jax.numpy as jnp
from jax import lax
from jax.experimental import pallas as pl
from jax.experimental.pallas import tpu as pltpu

F32 = jnp.float32
BF16 = jnp.bfloat16

GRID_W = 64
S5_WIDTH = 256
S5_GROUP = 16
S5_GROUPS = 16
S5_STATE = 64
HG_HEADS = 6
HG_DK = 128
HG_WIDTH = HG_HEADS * HG_DK
N_EXPERTS = 64
ROUTE_GROUPS = 8
TOPK_GROUPS = 4
TOP_K = 8
ROUTED_SCALE = 2.5
EPS = 1e-6

LANES = 128
SUBLANES = 8

TOK_TILE = 512
S5_T = 16
HG_CHUNK = 64
HG_BATCH = 4
VMEM_LIMIT = 56 * 1024 * 1024

_NT = (((1,), (1,)), ((), ()))
_TN = (((0,), (0,)), ((), ()))


def _params(*sem):
    return pltpu.CompilerParams(dimension_semantics=sem, vmem_limit_bytes=VMEM_LIMIT)


def _dot(a, b):
    return jnp.dot(a, b, preferred_element_type=F32)


def _sigmoid(x):
    return 0.5 * jnp.tanh(0.5 * x) + 0.5


def _ada_kernel(c_ref, w_ref, b_ref, o_ref):
    c = c_ref[...]
    s = (c * _sigmoid(c)).astype(BF16)
    o_ref[...] = _dot(s, w_ref[...].astype(BF16)) + b_ref[...]


def _ada(c8, w_ada, b_ada):
    d, n = w_ada.shape
    tn = 1536
    return pl.pallas_call(
        _ada_kernel,
        out_shape=jax.ShapeDtypeStruct((8, n), F32),
        grid=(n // tn,),
        in_specs=[pl.BlockSpec((8, d), lambda j: (0, 0)),
                  pl.BlockSpec((d, tn), lambda j: (0, j)),
                  pl.BlockSpec((1, tn), lambda j: (0, j))],
        out_specs=pl.BlockSpec((8, tn), lambda j: (0, j)),
        compiler_params=_params("arbitrary"),
        name="ada_mod",
    )(c8, w_ada, b_ada.reshape(1, n))


_IN_PIECES = (("u", 0, 256, BF16), ("q", 256, 768, BF16), ("ff", 1024, 768, BF16),
              ("fb", 1792, 768, BF16), ("i", 2560, 768, BF16), ("go", 3328, 768, BF16),
              ("ga", 4096, 1024, BF16), ("gb", 5120, 1024, BF16))


def _fold_rows(val, buf_a, buf_b):
    t = val.shape[0]
    buf_a[...] = val[:, :LANES]
    buf_b[...] = val[:, LANES:]
    pieces = []
    for s in range(S5_T):
        pieces += [buf_a[pl.ds(s, t // S5_T, stride=S5_T), :], buf_b[pl.ds(s, t // S5_T, stride=S5_T), :]]
    return jnp.concatenate(pieces, axis=-1)


def _unfold_rows(val, buf_a, buf_b):
    r = val.shape[0]
    for s in range(S5_T):
        buf_a[pl.ds(s, r, stride=S5_T), :] = val[:, s * S5_WIDTH:s * S5_WIDTH + LANES]
        buf_b[pl.ds(s, r, stride=S5_T), :] = val[:, s * S5_WIDTH + LANES:(s + 1) * S5_WIDTH]
    return jnp.concatenate([buf_a[...], buf_b[...]], axis=-1)


def _grid_transpose_matrix(tm):
    i = jnp.arange(tm)
    src = (i % (tm // GRID_W)) * GRID_W + i // (tm // GRID_W)
    return (src[:, None] == jnp.arange(tm)[None, :]).astype(BF16)


def _inproj_kernel(x_ref, sc_ref, sh_ref, g_ref, w_ref, p_ref, *o_refs):
    o_refs, (fold_a, fold_b) = o_refs[:len(_IN_PIECES)], o_refs[len(_IN_PIECES):]
    x = x_ref[...]
    y = x * lax.rsqrt(jnp.mean(x * x, axis=-1, keepdims=True) + EPS) * g_ref[...]
    h = (y * (1.0 + sc_ref[0]) + sh_ref[0]).astype(BF16)
    h_cm = None
    for (name, a, wd, _), o_ref in zip(_IN_PIECES, o_refs):
        if name == "u":
            o_ref[...] = _fold_rows(_dot(h, w_ref[:, a:a + wd]), fold_a, fold_b).astype(o_ref.dtype)
        elif len(o_ref.shape) == 2:
            o_ref[...] = _dot(h, w_ref[:, a:a + wd]).astype(o_ref.dtype)
        else:
            if h_cm is None:
                h_cm = _dot(p_ref[...], h).astype(BF16)
            o_ref[0] = _dot(h_cm, w_ref[:, a:a + wd]).astype(o_ref.dtype).reshape(o_ref.shape[1:])


_COLMAJOR_PIECES = ("q", "ff", "fb", "i")


def _inproj(x2d, sc, sh, g, w_bf16, rows_per_mod, tm, colmajor):
    n, d = x2d.shape
    per = rows_per_mod // tm
    mod_map = (lambda i: (i // per, 0, 0)) if sc.shape[0] > 1 else (lambda i: (0, 0, 0))
    shapes, specs = [], []
    for name, _, wd, dt in _IN_PIECES:
        if colmajor and name in _COLMAJOR_PIECES:
            shapes.append(jax.ShapeDtypeStruct((n // rows_per_mod, GRID_W, rows_per_mod // GRID_W, wd), dt))
            specs.append(pl.BlockSpec((1, GRID_W, tm // GRID_W, wd), lambda i: (i // per, 0, i % per, 0)))
        elif name == "u":
            shapes.append(jax.ShapeDtypeStruct((n // S5_T, S5_T * wd), dt))
            specs.append(pl.BlockSpec((tm // S5_T, S5_T * wd), lambda i: (i, 0)))
        else:
            shapes.append(jax.ShapeDtypeStruct((n, wd), dt))
            specs.append(pl.BlockSpec((tm, wd), lambda i: (i, 0)))
    return pl.pallas_call(
        _inproj_kernel,
        out_shape=shapes,
        grid=(n // tm,),
        in_specs=[pl.BlockSpec((tm, d), lambda i: (i, 0)),
                  pl.BlockSpec((1, 1, d), mod_map),
                  pl.BlockSpec((1, 1, d), mod_map),
                  pl.BlockSpec((1, d), lambda i: (0, 0)),
                  pl.BlockSpec(w_bf16.shape, lambda i: (0, 0)),
                  pl.BlockSpec((tm, tm), lambda i: (0, 0))],
        out_specs=specs,
        scratch_shapes=[pltpu.VMEM((tm, LANES), F32), pltpu.VMEM((tm, LANES), F32)],
        compiler_params=_params("arbitrary"),
        name="in_proj",
    )(x2d, sc, sh, g.reshape(1, d), w_bf16, _grid_transpose_matrix(tm))


def _hgrn_gates(zf, lb):
    sig = _sigmoid(zf)
    logf = jnp.log(lb + (1.0 - lb) * sig)
    k = (1.0 - lb) * (1.0 - sig)
    return logf, k


def _chunk_cumsum(cs, logf):
    hi = logf.astype(BF16)
    lo = (logf - hi.astype(F32)).astype(BF16)
    return _dot(cs, hi) + _dot(cs, lo)


def _hgrn_state_step(zf, v, lb, st, cs, reverse):
    logf, k = _hgrn_gates(zf, lb)
    cum = _chunk_cumsum(cs, logf)
    t = 0 if reverse else HG_CHUNK - 1
    total = cum[t:t + 1, :]
    kdec = (k * jnp.exp(total - cum)).astype(BF16)
    st_new = st * jnp.exp(total) + lax.dot_general(v.astype(BF16), kdec, _TN, preferred_element_type=F32)
    return cum, k, st_new


def _hgrn_kernel(*refs, reverse, final, n_ctx_chunks):
    if final:
        q_all, f_all, v_all, cf_ref, cv_ref, lb_ref, of_all, g_ref, o_all, st_ref = refs
    else:
        q_all, f_all, v_all, cf_ref, cv_ref, lb_ref, o_all, st_ref = refs
        of_all = None
    n_batch = q_all.shape[0]
    c_len = HG_CHUNK
    n_rows = q_all.shape[2]
    n_chunks = n_rows // c_len
    row = lax.broadcasted_iota(jnp.int32, (n_rows, n_rows), 0)
    col = lax.broadcasted_iota(jnp.int32, (n_rows, n_rows), 1)
    tri = (col >= row) if reverse else (col <= row)
    same_chunk = None
    for c in range(n_chunks):
        lo, hi = c * c_len, (c + 1) * c_len
        blk = (row >= lo) & (row < hi) & (col >= lo) & (col < hi)
        same_chunk = blk if same_chunk is None else (same_chunk | blk)
    mask = tri & same_chunk
    cs = jnp.where(mask, 1.0, 0.0).astype(BF16)

    @pl.when(pl.program_id(1) == 0)
    def _():
        cs1 = cs[:c_len, :c_len]
        order = range(n_ctx_chunks - 1, -1, -1) if reverse else range(n_ctx_chunks)
        for bi in range(n_batch):
            for h in range(HG_HEADS):
                cols = slice(h * HG_DK, (h + 1) * HG_DK)
                st = jnp.zeros((HG_DK, HG_DK), F32)
                for c in order:
                    rows = slice(c * c_len, (c + 1) * c_len)
                    _, _, st = _hgrn_state_step(cf_ref[bi, rows, cols].astype(F32), cv_ref[bi, rows, cols].astype(F32),
                                                lb_ref[:, cols], st, cs1, reverse)
                st_ref[bi * HG_HEADS + h] = st

    def chunk_rows(x, r):
        return [x[c * c_len + r:c * c_len + r + 1, :] for c in range(n_chunks)]

    def over_chunks(rows):
        return jnp.concatenate([jnp.broadcast_to(r, (c_len, r.shape[1])) for r in rows], axis=0)

    lb = lb_ref[...]
    r_ref = c_len // 2 - 1 if reverse else c_len // 2
    r_tot = 0 if reverse else c_len - 1
    order = range(n_chunks - 1, -1, -1) if reverse else range(n_chunks)
    for bi in range(n_batch):
        q = q_all[bi, 0].astype(F32)
        v = v_all[bi, 0]
        logf, k = _hgrn_gates(f_all[bi, 0].astype(F32), lb)
        cum = _chunk_cumsum(cs, logf)
        ref_rows, tot_rows = chunk_rows(cum, r_ref), chunk_rows(cum, r_tot)
        ref = over_chunks(ref_rows)
        qe = q * jnp.exp(cum - ref)
        ke = k * jnp.exp(ref - cum)
        qi, ki = qe.astype(BF16), ke.astype(BF16)
        q_in = (qe * over_chunks([jnp.exp(r) for r in ref_rows])).astype(BF16)
        kdec = (ke * over_chunks([jnp.exp(t - r) for t, r in zip(tot_rows, ref_rows)])).astype(BF16)
        for h in range(HG_HEADS):
            cols = slice(h * HG_DK, (h + 1) * HG_DK)
            s = lax.dot_general(qi[:, cols], ki[:, cols], _NT, preferred_element_type=F32)
            o_intra = _dot(jnp.where(mask, s, 0.0).astype(BF16), v[:, cols])
            st = st_ref[bi * HG_HEADS + h]
            for c in order:
                rows = slice(c * c_len, (c + 1) * c_len)
                o = o_intra[rows] + lax.dot_general(q_in[rows, cols], st.astype(BF16), _NT,
                                                    preferred_element_type=F32)
                total = cum[c * c_len + r_tot:c * c_len + r_tot + 1, cols]
                st = st * jnp.exp(total) + lax.dot_general(v[rows, cols], kdec[rows, cols], _TN,
                                                           preferred_element_type=F32)
                if final:
                    o = o + of_all[bi, 0, rows, cols].astype(F32)
                    o = o * lax.rsqrt(jnp.mean(o * o, axis=-1, keepdims=True) + EPS) * g_ref[...]
                o_all[bi, 0, rows, cols] = o.astype(o_all.dtype)
            st_ref[bi * HG_HEADS + h] = st


def _hgrn_pass(q, f, v, cf, cv, lb, o_prev, g, *, reverse):
    b, nw, rows, _ = q.shape
    nb = HG_BATCH if b % HG_BATCH == 0 else 1
    final = o_prev is not None
    wmap = (lambda bi, w: (bi, nw - 1 - w, 0, 0)) if reverse else (lambda bi, w: (bi, w, 0, 0))
    blk = pl.BlockSpec((nb, 1, rows, HG_WIDTH), wmap)
    cblk = pl.BlockSpec((nb, cf.shape[1], HG_WIDTH), lambda bi, w: (bi, 0, 0))
    in_specs = [blk, blk, blk, cblk, cblk, pl.BlockSpec((1, HG_WIDTH), lambda bi, w: (0, 0))]
    args = [q, f, v, cf, cv, lb]
    if final:
        in_specs += [blk, pl.BlockSpec((1, HG_DK), lambda bi, w: (0, 0))]
        args += [o_prev, g]
    return pl.pallas_call(
        functools.partial(_hgrn_kernel, reverse=reverse, final=final, n_ctx_chunks=cf.shape[1] // HG_CHUNK),
        out_shape=jax.ShapeDtypeStruct(q.shape, BF16),
        grid=(b // nb, nw),
        in_specs=in_specs,
        out_specs=blk,
        scratch_shapes=[pltpu.VMEM((nb * HG_HEADS, HG_DK, HG_DK), F32)],
        compiler_params=_params("arbitrary", "arbitrary"),
        name="hgrn_bwd" if reverse else "hgrn_fwd",
    )(*args)


def _s5_weights(lam_re, lam_im, log_dt, b_re, b_im, c_re, c_im):
    g, p, cc, t = S5_GROUPS, S5_STATE, S5_GROUP, S5_T
    lre = jnp.minimum(lam_re.astype(F32), -1e-4)
    lim = lam_im.astype(F32)
    dt = jnp.exp(log_dt.astype(F32))[..., None]
    ks = jnp.arange(t + 1, dtype=F32)[:, None, None, None]
    mag = jnp.exp(ks * (lre * dt)[None])
    pw_re = mag * jnp.cos(ks * (lim * dt)[None])
    pw_im = mag * jnp.sin(ks * (lim * dt)[None])
    nr, ni = pw_re[1] - 1.0, pw_im[1]
    den = lre * lre + lim * lim
    cf_re = (nr * lre + ni * lim) / den
    cf_im = (ni * lre - nr * lim) / den
    bb_re = cf_re[..., None] * b_re - cf_im[..., None] * b_im
    bb_im = cf_re[..., None] * b_im + cf_im[..., None] * b_re
    cre, cim = c_re.astype(F32), c_im.astype(F32)
    sw, ns = S5_WIDTH, 2 * g * p
    grp_of_row = jnp.arange(sw)[:, None] // cc

    qr = pw_re[:t].transpose(1, 2, 0, 3)[:, :, :, None, :]
    qi = pw_im[:t].transpose(1, 2, 0, 3)[:, :, :, None, :]
    cp_re = (cre[None, :, None] * qr - cim[None, :, None] * qi).reshape(2, g, t * cc, p)
    cp_im = (cre[None, :, None] * qi + cim[None, :, None] * qr).reshape(2, g, t * cc, p)

    def contract_p(cp, bb):
        return lax.dot_general(cp, bb, (((3,), (2,)), ((0, 1), (0, 1))), precision=lax.Precision.HIGHEST,
                               preferred_element_type=F32)

    kk = (contract_p(cp_re, bb_re) - contract_p(cp_im, bb_im)).reshape(2, g, t, cc, cc).transpose(0, 2, 1, 3, 4)
    kf, kb = kk[0], kk[1]
    kall = jnp.concatenate([kb[:0:-1], (kf[0] + kb[0])[None], kf[1:]], axis=0)
    kt = kall.transpose(0, 1, 3, 2).reshape(2 * t - 1, sw, cc)

    def spread(x, period, reps):
        sel = (jnp.arange(period)[:, None] == (jnp.arange(period * reps)[None, :] % period)).astype(BF16)
        return jnp.dot(x.astype(BF16), sel, preferred_element_type=BF16)

    same = grp_of_row == (jnp.arange(sw)[None, :] // cc)
    d_lag = jnp.where(same[None], spread(kt, cc, g), 0)

    same_in = ((jnp.arange(t * sw)[:, None] // cc) % g) == ((jnp.arange(2 * ns)[None, :] % (g * p)) // p)

    def in_to_state(pre, pim, bre, bim):
        xre = pre[..., None] * bre[None] - pim[..., None] * bim[None]
        xim = pre[..., None] * bim[None] + pim[..., None] * bre[None]
        return [spread(xre.transpose(0, 1, 3, 2).reshape(t * sw, p), p, g),
                spread(xim.transpose(0, 1, 3, 2).reshape(t * sw, p), p, g)]

    w_in = jnp.concatenate(in_to_state(pw_re[t - 1::-1, 0], pw_im[t - 1::-1, 0], bb_re[0], bb_im[0])
                           + in_to_state(pw_re[:t, 1], pw_im[:t, 1], bb_re[1], bb_im[1]), axis=1)
    w_in = jnp.where(same_in, w_in, 0)

    same_out = ((jnp.arange(ns)[:, None] % (g * p)) // p) == ((jnp.arange(t * sw)[None, :] // cc) % g)
    col = jnp.arange(t * sw)
    pick = (jnp.arange(t * cc)[:, None] == ((col // sw) * cc + col % cc)[None, :]).astype(BF16)

    def state_to_out(pre, pim):
        are = cre[None] * pre[:, :, None, :] - cim[None] * pim[:, :, None, :]
        aim = cre[None] * pim[:, :, None, :] + cim[None] * pre[:, :, None, :]
        a = jnp.concatenate([are.transpose(1, 3, 0, 2), -aim.transpose(1, 3, 0, 2)], axis=0)
        a = jnp.dot(a.reshape(ns, t * cc).astype(BF16), pick, preferred_element_type=BF16)
        return jnp.where(same_out, a, 0)

    w_out_f = state_to_out(pw_re[1:, 0], pw_im[1:, 0])
    w_out_b = state_to_out(pw_re[t:0:-1, 1], pw_im[t:0:-1, 1])

    decay = jnp.stack([pw_re[t].reshape(2, g * p), pw_im[t].reshape(2, g * p)], axis=1)
    return d_lag, w_in, w_out_f, w_out_b, decay


def _s5_in_kernel(ul_ref, uc_ref, d_ref, w_ref, o_ref, u_ref):
    j = pl.program_id(1)

    @pl.when(j == 0)
    def _():
        kl, kc = ul_ref.shape[1], uc_ref.shape[1]
        for bi in range(ul_ref.shape[0]):
            u_ref[bi * (kl + kc):bi * (kl + kc) + kl, :] = ul_ref[bi]
            u_ref[bi * (kl + kc) + kl:(bi + 1) * (kl + kc), :] = uc_ref[bi]

    @pl.when(j < S5_T)
    def _():
        acc = _dot(u_ref[:, 0:S5_WIDTH], d_ref[j + S5_T - 1])
        for s in range(1, S5_T):
            acc = acc + _dot(u_ref[:, s * S5_WIDTH:(s + 1) * S5_WIDTH], d_ref[j - s + S5_T - 1])
        o_ref[...] = acc

    @pl.when(j >= S5_T)
    def _():
        o_ref[...] = _dot(u_ref[...], w_ref[...])


def _s5_in(u_lat, u_ctx, d_lag, w_in, nb_tile):
    b, kl, k = u_lat.shape
    kc = u_ctx.shape[1]
    tm = nb_tile * (kl + kc)
    tn = S5_WIDTH
    nj = (k + w_in.shape[1]) // tn
    return pl.pallas_call(
        _s5_in_kernel,
        out_shape=jax.ShapeDtypeStruct((b * (kl + kc), nj * tn), F32),
        grid=(b // nb_tile, nj),
        in_specs=[pl.BlockSpec((nb_tile, kl, k), lambda i, j: (i, 0, 0)),
                  pl.BlockSpec((nb_tile, kc, k), lambda i, j: (i, 0, 0)),
                  pl.BlockSpec(d_lag.shape, lambda i, j: (0, 0, 0)),
                  pl.BlockSpec((k, tn), lambda i, j: (0, jnp.maximum(j - S5_T, 0)))],
        out_specs=pl.BlockSpec((tm, tn), lambda i, j: (i, j)),
        scratch_shapes=[pltpu.VMEM((tm, k), BF16)],
        compiler_params=_params("arbitrary", "arbitrary"),
        name="s5_in",
    )(u_lat, u_ctx, d_lag, w_in)


def _s5_scan_kernel(efr_ref, efi_ref, ebr_ref, ebi_ref, a_ref, hfr_ref, hfi_ref, hbr_ref, hbi_ref,
                    *, nb, rows_in, rows_out):
    dirs = ((efr_ref, efi_ref, hfr_ref, hfi_ref, a_ref[0, 0:1, :], a_ref[0, 1:2, :]),
            (ebr_ref, ebi_ref, hbr_ref, hbi_ref, a_ref[1, 0:1, :], a_ref[1, 1:2, :]))
    zero = jnp.zeros((nb, dirs[0][4].shape[1]), F32)

    def step(srcs, carry, store):
        new = []
        for di, (er_ref, ei_ref, hr_ref, hi_ref, are, aim) in enumerate(dirs):
            hre, him = carry[2 * di], carry[2 * di + 1]
            if store:
                hr_ref[pl.ds(srcs[di], nb, stride=rows_out), :] = hre
                hi_ref[pl.ds(srcs[di], nb, stride=rows_out), :] = him
            ere = er_ref[pl.ds(srcs[di], nb, stride=rows_in), :]
            eim = ei_ref[pl.ds(srcs[di], nb, stride=rows_in), :]
            new += [are * hre - aim * him + ere, are * him + aim * hre + eim]
        return tuple(new)

    n_ctx = rows_in - rows_out
    carry = lax.fori_loop(0, n_ctx, lambda s, c: step((rows_out + s, rows_in - 1 - s), c, False),
                          tuple([zero] * 4))

    def two_steps(s2, c):
        c = step((2 * s2, rows_out - 1 - 2 * s2), c, True)
        return step((2 * s2 + 1, rows_out - 2 - 2 * s2), c, True)

    lax.fori_loop(0, rows_out // 2, two_steps, carry)


def _s5_scan(e, decay, nb, rows_in, rows_out):
    assert rows_out % 2 == 0, "the latent chunks are scanned two per loop iteration"
    tc = LANES
    nsr = S5_GROUPS * S5_STATE
    c0 = (S5_T * S5_WIDTH) // tc
    nt = nsr // tc
    eblk = lambda k: pl.BlockSpec((nb * rows_in, tc), lambda j: (0, c0 + k * nt + j))
    hblk = pl.BlockSpec((nb * rows_out, tc), lambda j: (0, j))
    return pl.pallas_call(
        functools.partial(_s5_scan_kernel, nb=nb, rows_in=rows_in, rows_out=rows_out),
        out_shape=[jax.ShapeDtypeStruct((nb * rows_out, nsr), F32)] * 4,
        grid=(nt,),
        in_specs=[eblk(0), eblk(1), eblk(2), eblk(3), pl.BlockSpec((2, 2, tc), lambda j: (0, 0, j))],
        out_specs=[hblk] * 4,
        compiler_params=_params("arbitrary"),
        name="s5_scan",
    )(e, e, e, e, decay)


def _gelu_tanh(x):
    return 0.5 * x * (1.0 + jnp.tanh(math.sqrt(2.0 / math.pi) * (x + 0.044715 * x * x * x)))


def _s5_out_kernel(hfr_ref, hfi_ref, hbr_ref, hbi_ref, wf_ref, wb_ref, yi_ref, u_ref, d_ref, wg_ref, o_ref):
    nsr = hfr_ref.shape[1]
    y = yi_ref[0] + d_ref[...] * u_ref[...].astype(F32)
    for h_ref, w_ref, r0 in ((hfr_ref, wf_ref, 0), (hfi_ref, wf_ref, nsr), (hbr_ref, wb_ref, 0), (hbi_ref, wb_ref, nsr)):
        y = y + _dot(h_ref[...].astype(BF16), w_ref[r0:r0 + nsr, :])
    y = _gelu_tanh(y)
    gate = _sigmoid(_dot(y.astype(BF16), wg_ref[...]))
    o_ref[...] = (y * gate).astype(o_ref.dtype)


def _s5_out(states, w_out_f, w_out_b, e3, u_rows, d_row, w_glu):
    m, nsr = states[0].shape
    nb = e3.shape[0]
    tm = m // nb
    tn = S5_WIDTH
    st = pl.BlockSpec((tm, nsr), lambda i, j: (i, 0))
    wo = pl.BlockSpec((2 * nsr, tn), lambda i, j: (0, j))
    return pl.pallas_call(
        _s5_out_kernel,
        out_shape=jax.ShapeDtypeStruct((m, S5_T * S5_WIDTH), BF16),
        grid=(nb, S5_T),
        in_specs=[st, st, st, st, wo, wo,
                  pl.BlockSpec((1, tm, tn), lambda i, j: (i, 0, j)),
                  pl.BlockSpec((tm, tn), lambda i, j: (i, j)),
                  pl.BlockSpec((1, tn), lambda i, j: (0, 0)),
                  pl.BlockSpec((tn, tn), lambda i, j: (0, 0))],
        out_specs=pl.BlockSpec((tm, tn), lambda i, j: (i, j)),
        compiler_params=_params("arbitrary", "arbitrary"),
        name="s5_out",
    )(*states, w_out_f, w_out_b, e3, u_rows, d_row, w_glu)


U32 = jnp.uint32
ROW_SUB = 4


def _to_token_rows(ref, val):
    t, d = val.shape
    bits = lax.bitcast_convert_type(val.astype(BF16).astype(F32), U32)
    w = (bits[:, :d // 2] >> 16) | bits[:, d // 2:]
    for s in range(ROW_SUB):
        ref[pl.ds(s, t, stride=ROW_SUB), :] = w[:, s * LANES:(s + 1) * LANES]


def _from_token_rows(ref, t, row0=0):
    w = jnp.concatenate([ref[pl.ds(row0 * ROW_SUB + s, t, stride=ROW_SUB), :] for s in range(ROW_SUB)], axis=-1)
    lo = lax.bitcast_convert_type(w << 16, F32)
    hi = lax.bitcast_convert_type(w & jnp.uint32(0xFFFF0000), F32)
    return jnp.concatenate([lo, hi], axis=-1)


def _route(h2b, wr_ref, br_ref, cnt_ref, ls8_ref, w8_ref, seg_ref):
    tm = h2b.shape[0]
    per_group = N_EXPERTS // ROUTE_GROUPS
    scores = _sigmoid(lax.dot_general(wr_ref[...], h2b, _NT, preferred_element_type=F32))
    biased = scores + br_ref[...]
    neg = -jnp.inf
    sub = lax.broadcasted_iota(jnp.int32, (per_group, tm), 0)
    grp = []
    for gi in range(ROUTE_GROUPS):
        v = biased[gi * per_group:(gi + 1) * per_group, :]
        m1 = jnp.max(v, axis=0, keepdims=True)
        first = jnp.min(jnp.where(v == m1, sub, per_group), axis=0, keepdims=True)
        m2 = jnp.max(jnp.where(sub == first, neg, v), axis=0, keepdims=True)
        grp.append(m1 + m2)
    grp = jnp.concatenate(grp, axis=0)
    gid = lax.broadcasted_iota(jnp.int32, (ROUTE_GROUPS, tm), 0)
    beaten = jnp.zeros((ROUTE_GROUPS, tm), jnp.int32)
    for gj in range(ROUTE_GROUPS):
        r = grp[gj:gj + 1, :]
        beaten = beaten + jnp.where((r > grp) | ((r == grp) & (gj < gid)), 1, 0)
    group_ok = beaten < TOPK_GROUPS
    expert_ok = jnp.concatenate(
        [jnp.broadcast_to(group_ok[gi:gi + 1, :], (per_group, tm)) for gi in range(ROUTE_GROUPS)], axis=0)
    cur = jnp.where(expert_ok, biased, neg)
    eid = lax.broadcasted_iota(jnp.int32, (N_EXPERTS, tm), 0)
    sel = jnp.zeros((N_EXPERTS, tm), F32)
    picks, wts = [], []
    for _ in range(TOP_K):
        m = jnp.max(cur, axis=0, keepdims=True)
        idx = jnp.min(jnp.where(cur == m, eid, N_EXPERTS), axis=0, keepdims=True)
        hit = eid == idx
        picks.append(idx)
        wts.append(jnp.sum(jnp.where(hit, scores, 0.0), axis=0, keepdims=True))
        sel = jnp.where(hit, 1.0, sel)
        cur = jnp.where(hit, neg, cur)
    wsum = wts[0]
    for w in wts[1:]:
        wsum = wsum + w
    selb = sel.astype(BF16)
    ti = lax.broadcasted_iota(jnp.int32, (tm, tm), 0)
    tj = lax.broadcasted_iota(jnp.int32, (tm, tm), 1)
    rank = _dot(selb, jnp.where(ti < tj, 1.0, 0.0).astype(BF16))
    seg_units = jnp.ceil(jnp.sum(sel, axis=1, keepdims=True) * (1.0 / SEG_ALIGN))
    ei = lax.broadcasted_iota(jnp.int32, (N_EXPERTS, N_EXPERTS), 0)
    ej = lax.broadcasted_iota(jnp.int32, (N_EXPERTS, N_EXPERTS), 1)
    units_row = jnp.broadcast_to(seg_units, (N_EXPERTS, LANES)).astype(BF16)
    seg_off = _dot(jnp.where(ej < ei, 1.0, 0.0).astype(BF16), units_row)[:, 0:1] * SEG_ALIGN
    seg_rows = seg_units * SEG_ALIGN
    slot = seg_off + rank
    for k in range(TOP_K):
        w8_ref[k:k + 1, :] = wts[k] / wsum * ROUTED_SCALE
        ls8_ref[k:k + 1, :] = (jnp.sum(jnp.where(eid == picks[k], slot, 0.0), axis=0, keepdims=True)
                               * ROW_SUB).astype(jnp.int32)
    lane = lax.broadcasted_iota(jnp.int32, (N_EXPERTS, LANES), 1)
    seg_ref[0] = jnp.where(lane == 0, cnt_ref[...], jnp.where(lane == 1, seg_rows, seg_off))
    cnt_ref[...] = cnt_ref[...] + seg_rows


def _merge_kernel(x_ref, ya_ref, on_ref, go_ref, ga_ref, gb_ref, g1_ref, sc_ref, sh_ref, n2_ref,
                  pa_ref, pb_ref, wo_ref, wr_ref, br_ref, pt_ref,
                  x1_ref, h2_ref, ls8_ref, w8_ref, seg_ref, cnt_ref, fold_a, fold_b):
    @pl.when(pl.program_id(0) == 0)
    def _():
        cnt_ref[...] = jnp.zeros_like(cnt_ref)

    go = go_ref[...].astype(F32)
    on = _dot(pt_ref[...], on_ref[0].reshape(x_ref.shape[0], HG_WIDTH))
    y_b = (on * (go * _sigmoid(go))).astype(BF16)
    y_a = _unfold_rows(ya_ref[...].astype(F32), fold_a, fold_b).astype(BF16)
    pa = _dot(y_a, pa_ref[...])
    pb = _dot(y_b, pb_ref[...])
    merged = _sigmoid(ga_ref[...].astype(F32)) * pa + _sigmoid(gb_ref[...].astype(F32)) * pb
    x1 = x_ref[...] + g1_ref[0] * _dot(merged.astype(BF16), wo_ref[...])
    x1_ref[...] = x1
    y = x1 * lax.rsqrt(jnp.mean(x1 * x1, axis=-1, keepdims=True) + EPS) * n2_ref[...]
    h2 = y * (1.0 + sc_ref[0]) + sh_ref[0]
    _to_token_rows(h2_ref, h2)
    _route(h2.astype(BF16), wr_ref, br_ref, cnt_ref, ls8_ref, w8_ref, seg_ref)


def _merge(x2d, ya, on, go, ga, gb, g1, sc2, sh2, n2g, pa, pb, wo, wr_t, br, rows_per_batch, tm):
    n, d = x2d.shape
    per = rows_per_batch // tm
    row = lambda wd: pl.BlockSpec((tm, wd), lambda i: (i, 0))
    mod = pl.BlockSpec((1, 1, d), lambda i: (i // per, 0, 0))
    full = lambda a: pl.BlockSpec(a.shape, lambda i: (0, 0))
    tok = pl.BlockSpec((TOP_K, tm), lambda i: (0, i))
    return pl.pallas_call(
        _merge_kernel,
        out_shape=[jax.ShapeDtypeStruct((n, d), F32), jax.ShapeDtypeStruct((n * ROW_SUB, LANES), U32),
                   jax.ShapeDtypeStruct((TOP_K, n), jnp.int32), jax.ShapeDtypeStruct((TOP_K, n), F32),
                   jax.ShapeDtypeStruct((n // tm, N_EXPERTS, LANES), F32),
                   jax.ShapeDtypeStruct((N_EXPERTS, 1), F32)],
        grid=(n // tm,),
        in_specs=[row(d), pl.BlockSpec((tm // S5_T, S5_T * S5_WIDTH), lambda i: (i, 0)),
                  pl.BlockSpec((1, GRID_W, tm // GRID_W, HG_WIDTH), lambda i: (i // per, 0, i % per, 0)),
                  row(HG_WIDTH), row(d), row(d), mod, mod, mod,
                  pl.BlockSpec((1, d), lambda i: (0, 0)), full(pa), full(pb), full(wo), full(wr_t), full(br),
                  pl.BlockSpec((tm, tm), lambda i: (0, 0))],
        out_specs=[row(d), pl.BlockSpec((tm * ROW_SUB, LANES), lambda i: (i, 0)), tok, tok,
                   pl.BlockSpec((1, N_EXPERTS, LANES), lambda i: (i, 0, 0)),
                   pl.BlockSpec((N_EXPERTS, 1), lambda i: (0, 0))],
        scratch_shapes=[pltpu.VMEM((tm, LANES), F32), pltpu.VMEM((tm, LANES), F32)],
        compiler_params=_params("arbitrary"),
        name="merge_out_proj_route",
    )(x2d, ya, on, go, ga, gb, g1, sc2, sh2, n2g.reshape(1, d), pa, pb, wo, wr_t, br,
      _grid_transpose_matrix(tm).T)


MOE_TILE = TOK_TILE
TOKEN_UNROLL = 4
COMBINE_UNROLL = 8
SEG_ALIGN = 8
FILL_ROWS = 512
STAGE_ROWS = MOE_TILE * TOP_K + FILL_ROWS
MOE_BLK = 1024


def _wait_rows(any_ref, sem, n_rows):
    view = any_ref.at[pl.ds(0, n_rows * ROW_SUB)]
    pltpu.make_async_copy(view, view, sem).wait()


def _rows(ref, r0, n):
    return ref.at[pl.ds(pl.multiple_of(r0 * ROW_SUB, ROW_SUB), n * ROW_SUB)]


def _pow2_pieces(n, max_piece, fn, min_piece=1):
    done = 0
    piece = max_piece
    while piece >= min_piece:
        hit = (n & piece) != 0
        pl.when(hit)(functools.partial(fn, done, piece))
        done = done + (n & piece)
        piece //= 2


def _copy_rows(src_ref, src0, dst_ref, dst0, n, max_piece, sem, min_piece=1):
    def piece(off, size):
        pltpu.make_async_copy(_rows(src_ref, src0 + off, size), _rows(dst_ref, dst0 + off, size), sem).start()
    _pow2_pieces(n, max_piece, piece, min_piece)


def _wait_copied_rows(src_ref, dst_ref, n, max_piece, sem):
    def piece(off, size):
        pltpu.make_async_copy(_rows(src_ref, 0, size), _rows(dst_ref, 0, size), sem).wait()
    _pow2_pieces(n, max_piece, piece)


def _copy_tile_segments(i, src_ref, src_tab, dst_ref, dst_tab, cnt_ref, off_ref, fill_src0, fill_dst0, sem):
    def per_expert(e, carry):
        _copy_rows(src_ref, src_tab[i, e], dst_ref, dst_tab[i, e], cnt_ref[i, e], MOE_TILE, sem, SEG_ALIGN)
        return carry

    lax.fori_loop(0, N_EXPERTS, per_expert, 0)
    used = off_ref[i, N_EXPERTS - 1] + cnt_ref[i, N_EXPERTS - 1]
    _copy_rows(src_ref, fill_src0(used), dst_ref, fill_dst0(used), STAGE_ROWS - used, FILL_ROWS, sem, SEG_ALIGN)


def _dispatch_kernel(gs_ref, cnt_ref, off_ref, pad_ref, ls_ref, h2_ref, xs_hbm, ls_smem, stage0, stage1, zbuf,
                     sem0, sem1, lsem, zsem, *, tm, n_blocks):
    i = pl.program_id(0)
    last = pl.num_programs(0) - 1
    cp = pltpu.make_async_copy(ls_ref, ls_smem, lsem)
    cp.start()
    trash0 = n_blocks * MOE_BLK

    @pl.when(i == 0)
    def _():
        stage0[...] = jnp.zeros_like(stage0)
        stage1[...] = jnp.zeros_like(stage1)
        zbuf[...] = jnp.zeros_like(zbuf)
        cpz = pltpu.make_async_copy(zbuf, _rows(xs_hbm, trash0, 2 * FILL_ROWS), zsem)
        cpz.start()
        cpz.wait()

    cp.wait()

    def tile(stage, sem, prev_sem, trash):
        def body(tu, carry):
            for u in range(TOKEN_UNROLL):
                t = tu * TOKEN_UNROLL + u
                row = h2_ref[pl.ds(pl.multiple_of(t * ROW_SUB, ROW_SUB), ROW_SUB), :]
                for k in range(TOP_K):
                    stage[pl.ds(pl.multiple_of(ls_smem[t * TOP_K + k], ROW_SUB), ROW_SUB), :] = row
            return carry

        lax.fori_loop(0, tm // TOKEN_UNROLL, body, 0)
        _copy_tile_segments(i, stage, off_ref, xs_hbm, gs_ref, cnt_ref, off_ref,
                            lambda used: used, lambda used: trash, sem)

        @pl.when(i > 0)
        def _():
            _wait_rows(xs_hbm, prev_sem, STAGE_ROWS)

        @pl.when(i == last)
        def _():
            _wait_rows(xs_hbm, sem, STAGE_ROWS)

    pl.when(i % 2 == 0)(functools.partial(tile, stage0, sem0, sem1, trash0))
    pl.when(i % 2 == 1)(functools.partial(tile, stage1, sem1, sem0, trash0 + FILL_ROWS))

    def zero_pad(e, carry):
        _copy_rows(zbuf, 0, xs_hbm, pad_ref[0, e], pad_ref[1, e], MOE_BLK // 2, zsem)
        return carry

    def wait_pad(e, carry):
        _wait_copied_rows(zbuf, xs_hbm, pad_ref[1, e], MOE_BLK // 2, zsem)
        return carry

    def zero_block(j, carry):
        pltpu.make_async_copy(zbuf, _rows(xs_hbm, j * MOE_BLK, MOE_BLK), zsem).start()
        return carry

    def wait_block(j, carry):
        pltpu.make_async_copy(zbuf, _rows(xs_hbm, 0, MOE_BLK), zsem).wait()
        return carry

    @pl.when(i == 0)
    def _():
        lax.fori_loop(0, N_EXPERTS, zero_pad, 0)
        lax.fori_loop(pad_ref[2, 0], n_blocks, zero_block, 0)

    @pl.when(i == last)
    def _():
        lax.fori_loop(0, N_EXPERTS, wait_pad, 0)
        lax.fori_loop(pad_ref[2, 0], n_blocks, wait_block, 0)


def _dispatch(gstart, seg_cnt, seg_off, pad, ls8, h2_rows, n_blocks, tm):
    n = ls8.shape[0] // TOP_K
    cap = n_blocks * MOE_BLK + 2 * FILL_ROWS
    return pl.pallas_call(
        functools.partial(_dispatch_kernel, tm=tm, n_blocks=n_blocks),
        out_shape=jax.ShapeDtypeStruct((cap * ROW_SUB, LANES), U32),
        grid_spec=pltpu.PrefetchScalarGridSpec(
            num_scalar_prefetch=4,
            grid=(n // tm,),
            in_specs=[pl.BlockSpec((tm * TOP_K,), lambda i, *_: (i,)),
                      pl.BlockSpec((tm * ROW_SUB, LANES), lambda i, *_: (i, 0))],
            out_specs=pl.BlockSpec(memory_space=pl.ANY),
            scratch_shapes=[pltpu.SMEM((tm * TOP_K,), jnp.int32),
                            pltpu.VMEM((STAGE_ROWS * ROW_SUB, LANES), U32),
                            pltpu.VMEM((STAGE_ROWS * ROW_SUB, LANES), U32),
                            pltpu.VMEM((MOE_BLK * ROW_SUB, LANES), U32),
                            pltpu.SemaphoreType.DMA, pltpu.SemaphoreType.DMA, pltpu.SemaphoreType.DMA,
                            pltpu.SemaphoreType.DMA]),
        compiler_params=pltpu.CompilerParams(dimension_semantics=("arbitrary",), vmem_limit_bytes=VMEM_LIMIT,
                                             has_side_effects=True),
        name="moe_dispatch",
    )(gstart, seg_cnt, seg_off, pad, ls8, h2_rows)


def _expert_kernel(be_ref, nu_ref, x_ref, w1_ref, w3_ref, w2_ref, o_ref, w1b, w3b, w2b):
    j = pl.program_id(0)
    e = be_ref[j]
    prev = be_ref[jnp.maximum(j - 1, 0)]
    used = j < nu_ref[0]

    @pl.when(jnp.logical_and(used, jnp.logical_or(j == 0, e != prev)))
    def _():
        w1b[...] = w1_ref[0].astype(BF16)
        w3b[...] = w3_ref[0].astype(BF16)
        w2b[...] = w2_ref[0].astype(BF16)

    @pl.when(used)
    def _():
        x = _from_token_rows(x_ref, MOE_BLK).astype(BF16)
        a = _dot(x, w1b[...])
        hid = (a * _sigmoid(a)) * _dot(x, w3b[...])
        _to_token_rows(o_ref, _dot(hid.astype(BF16), w2b[...]))

    @pl.when(jnp.logical_not(used))
    def _():
        o_ref[...] = jnp.zeros_like(o_ref)


def _experts(block_e, n_used, xs, w1, w3, w2):
    n_blocks = block_e.shape[0]
    d, f = w1.shape[1], w1.shape[2]
    rows = pl.BlockSpec((MOE_BLK * ROW_SUB, LANES), lambda j, be, nu: (j, 0))
    rows_in = pl.BlockSpec((MOE_BLK * ROW_SUB, LANES), lambda j, be, nu: (jnp.minimum(j, nu[0] - 1), 0))
    return pl.pallas_call(
        _expert_kernel,
        out_shape=jax.ShapeDtypeStruct((n_blocks * MOE_BLK * ROW_SUB, LANES), U32),
        grid_spec=pltpu.PrefetchScalarGridSpec(
            num_scalar_prefetch=2,
            grid=(n_blocks,),
            in_specs=[rows_in,
                      pl.BlockSpec((1, d, f), lambda j, be, nu: (be[j], 0, 0)),
                      pl.BlockSpec((1, d, f), lambda j, be, nu: (be[j], 0, 0)),
                      pl.BlockSpec((1, f, d), lambda j, be, nu: (be[j], 0, 0))],
            out_specs=rows,
            scratch_shapes=[pltpu.VMEM((d, f), BF16), pltpu.VMEM((d, f), BF16), pltpu.VMEM((f, d), BF16)]),
        compiler_params=_params("arbitrary"),
        name="moe_experts",
    )(block_e, n_used, xs, w1, w3, w2)


def _combine_kernel(gs_ref, cnt_ref, off_ref, ls_ref, w8_ref, x1_ref, h2_ref, g2_ref, ws1_ref, ws3_ref, ws2_ref,
                    fg_ref, ys_hbm, o_ref, ls_smem, w_smem, gbuf0, gbuf1, acc_rows, sem0, sem1, lsem, *, tm):
    i = pl.program_id(0)
    last = pl.num_programs(0) - 1
    cp1 = pltpu.make_async_copy(ls_ref, ls_smem, lsem)
    cp2 = pltpu.make_async_copy(w8_ref, w_smem, lsem)
    cp1.start()
    cp2.start()

    def fetch(tile, gbuf, sem):
        _copy_tile_segments(tile, ys_hbm, gs_ref, gbuf, off_ref, cnt_ref, off_ref,
                            lambda used: 0, lambda used: used, sem)

    @pl.when(i == 0)
    def _():
        fetch(0, gbuf0, sem0)

    @pl.when(jnp.logical_and(i < last, i % 2 == 0))
    def _():
        fetch(i + 1, gbuf1, sem1)

    @pl.when(jnp.logical_and(i < last, i % 2 == 1))
    def _():
        fetch(i + 1, gbuf0, sem0)

    h2 = _from_token_rows(h2_ref, tm).astype(BF16)
    a = _dot(h2, ws1_ref[...])
    hid = (a * _sigmoid(a)) * _dot(h2, ws3_ref[...])
    acc = _dot(hid.astype(BF16), ws2_ref[...])
    cp1.wait()
    cp2.wait()

    def reduce_rows(gbuf, sem):
        _wait_rows(gbuf, sem, STAGE_ROWS)

        def body(tu, carry):
            for u in range(COMBINE_UNROLL):
                t = tu * COMBINE_UNROLL + u
                lo = jnp.zeros((ROW_SUB, LANES), F32)
                hi = jnp.zeros((ROW_SUB, LANES), F32)
                for k in range(TOP_K):
                    w = w_smem[t * TOP_K + k]
                    words = gbuf[pl.ds(pl.multiple_of(ls_smem[t * TOP_K + k], ROW_SUB), ROW_SUB), :]
                    lo = lo + w * lax.bitcast_convert_type(words << 16, F32)
                    hi = hi + w * lax.bitcast_convert_type(words & jnp.uint32(0xFFFF0000), F32)
                acc_rows[pl.ds(pl.multiple_of(t * SUBLANES, SUBLANES), ROW_SUB), :] = lo
                acc_rows[pl.ds(pl.multiple_of(t * SUBLANES, SUBLANES) + ROW_SUB, ROW_SUB), :] = hi
            return carry

        lax.fori_loop(0, tm // COMBINE_UNROLL, body, 0)

    pl.when(i % 2 == 0)(functools.partial(reduce_rows, gbuf0, sem0))
    pl.when(i % 2 == 1)(functools.partial(reduce_rows, gbuf1, sem1))
    routed = jnp.concatenate([acc_rows[pl.ds(s, tm, stride=SUBLANES), :] for s in range(SUBLANES)], axis=-1)
    y = x1_ref[...] + g2_ref[0] * (acc + routed)
    o_ref[...] = y * lax.rsqrt(jnp.mean(y * y, axis=-1, keepdims=True) + EPS) * fg_ref[...]


def _combine(gstart, seg_cnt, seg_off, ls8, w8, x1, h2_rows, g2, ws1, ws3, ws2, fg, ys, rows_per_batch, tm):
    n, d = x1.shape
    per = rows_per_batch // tm
    tok = pl.BlockSpec((tm * TOP_K,), lambda i, *_: (i,))
    full = lambda a: pl.BlockSpec(a.shape, lambda i, *_: (0, 0))
    return pl.pallas_call(
        functools.partial(_combine_kernel, tm=tm),
        out_shape=jax.ShapeDtypeStruct((n, d), F32),
        grid_spec=pltpu.PrefetchScalarGridSpec(
            num_scalar_prefetch=3,
            grid=(n // tm,),
            in_specs=[tok, tok, pl.BlockSpec((tm, d), lambda i, *_: (i, 0)),
                      pl.BlockSpec((tm * ROW_SUB, LANES), lambda i, *_: (i, 0)),
                      pl.BlockSpec((1, 1, d), lambda i, *_: (i // per, 0, 0)),
                      full(ws1), full(ws3), full(ws2), pl.BlockSpec((1, d), lambda i, *_: (0, 0)),
                      pl.BlockSpec(memory_space=pl.ANY)],
            out_specs=pl.BlockSpec((tm, d), lambda i, *_: (i, 0)),
            scratch_shapes=[pltpu.SMEM((tm * TOP_K,), jnp.int32), pltpu.SMEM((tm * TOP_K,), F32),
                            pltpu.VMEM((STAGE_ROWS * ROW_SUB, LANES), U32),
                            pltpu.VMEM((STAGE_ROWS * ROW_SUB, LANES), U32),
                            pltpu.VMEM((tm * SUBLANES, LANES), F32), pltpu.SemaphoreType.DMA,
                            pltpu.SemaphoreType.DMA, pltpu.SemaphoreType.DMA]),
        compiler_params=_params("arbitrary"),
        name="moe_combine_final",
    )(gstart, seg_cnt, seg_off, ls8, w8, x1, h2_rows, g2, ws1, ws3, ws2, fg.reshape(1, d), ys)


def _moe_plan(seg, counts, n_assign):
    cnt = counts.reshape(N_EXPERTS).astype(jnp.int32)
    padded = (cnt + MOE_BLK - 1) // MOE_BLK * MOE_BLK
    pends = jnp.cumsum(padded)
    pstarts = pends - padded
    max_rows = n_assign + seg.shape[0] * N_EXPERTS * (SEG_ALIGN - 1)
    n_blocks = (max_rows + N_EXPERTS * (MOE_BLK - 1) + MOE_BLK - 1) // MOE_BLK
    seg = seg[:, :, :3].astype(jnp.int32)
    gstart = pstarts[None, :] + seg[:, :, 0]
    blk_start = jnp.arange(n_blocks, dtype=jnp.int32) * MOE_BLK
    block_e = jnp.minimum(jnp.sum((blk_start[:, None] >= pends[None, :]).astype(jnp.int32), axis=1),
                          N_EXPERTS - 1).astype(jnp.int32)
    n_used = (pends[-1:] // MOE_BLK).astype(jnp.int32)
    pad = jnp.stack([pstarts + cnt, padded - cnt, jnp.broadcast_to(n_used, (N_EXPERTS,))], axis=0).astype(jnp.int32)
    return gstart, seg[:, :, 1], seg[:, :, 2], pad, block_e, n_used, n_blocks


def _mixer(x, c, ctx, c_ctx, w_ada, b_ada, norm1_g, norm2_g, w_in, s5_lam_re, s5_lam_im, s5_log_dt,
           s5_b_re, s5_b_im, s5_c_re, s5_c_im, s5_d, s5_w_glu, lb, hg_norm_g, p_a, p_b, w_out,
           moe_w_router, moe_b_router):
    b, l, d = x.shape
    lc = ctx.shape[1]
    n = b * l

    c8 = jnp.concatenate([c, c_ctx[None], jnp.zeros((8 - b - 1, d), F32)], axis=0)
    mod = _ada(c8, w_ada, b_ada)
    sh1, sc1, g1, sh2, sc2, g2 = [mod[:b, k * d:(k + 1) * d].reshape(b, 1, d) for k in range(6)]
    csh1, csc1 = mod[b:b + 1, 0:d].reshape(1, 1, d), mod[b:b + 1, d:2 * d].reshape(1, 1, d)

    w_in_b = w_in.astype(BF16)
    z = dict(zip([p[0] for p in _IN_PIECES],
                 _inproj(x.reshape(n, d), sc1, sh1, norm1_g, w_in_b, l, TOK_TILE, True)))
    zc = dict(zip([p[0] for p in _IN_PIECES],
                  _inproj(ctx.reshape(b * lc, d), csc1, csh1, norm1_g, w_in_b, lc, lc, False)))

    cx = lambda t: t.reshape(b, lc, HG_WIDTH)
    lb_row = lb.reshape(1, HG_WIDTH)
    o_f = _hgrn_pass(z["q"], z["ff"], z["i"], cx(zc["ff"]), cx(zc["i"]), lb_row, None, None, reverse=False)
    o_n = _hgrn_pass(z["q"], z["fb"], z["i"], cx(zc["fb"]), cx(zc["i"]), lb_row, o_f,
                     hg_norm_g.reshape(1, HG_DK), reverse=True)

    d_lag, w_s5_in, w_out_f, w_out_b, decay = _s5_weights(s5_lam_re, s5_lam_im, s5_log_dt, s5_b_re, s5_b_im,
                                                          s5_c_re, s5_c_im)
    kc, kl = lc // S5_T, l // S5_T
    u_lat = z["u"].reshape(b, kl, S5_T * S5_WIDTH)
    u_ctx = zc["u"].reshape(b, kc, S5_T * S5_WIDTH)
    rows_in = kl + kc
    e = _s5_in(u_lat, u_ctx, d_lag, w_s5_in, 2 if b % 2 == 0 else 1)
    states = _s5_scan(e, decay, b, rows_in, kl)
    d_row = s5_d.astype(F32).reshape(1, S5_WIDTH)
    y_a = _s5_out(states, w_out_f, w_out_b, e.reshape(b, rows_in, -1), z["u"], d_row, s5_w_glu.astype(BF16))

    return _merge(x.reshape(n, d), y_a, o_n, z["go"], z["ga"], z["gb"], g1, sc2, sh2, norm2_g,
                  p_a.astype(BF16), p_b.astype(BF16), w_out.astype(BF16),
                  moe_w_router.T.astype(BF16), moe_b_router.astype(F32).reshape(N_EXPERTS, 1), l, MOE_TILE) + (g2,)


def kernel(x, c, ctx, c_ctx, w_ada, b_ada, norm1_g, norm2_g, w_in, s5_lam_re, s5_lam_im, s5_log_dt, s5_b_re,
           s5_b_im, s5_c_re, s5_c_im, s5_d, s5_w_glu, hg_lb_logits, hg_norm_g, p_a, p_b, w_out, moe_w_router,
           moe_b_router, moe_w1, moe_w3, moe_w2, moe_ws1, moe_ws3, moe_ws2, final_norm_g):
    b, l, d = x.shape
    n = b * l
    assert w_ada.shape[0] == 1, "single-layer block"
    lb = jnp.cumsum(jax.nn.softmax(hg_lb_logits.astype(F32), axis=0), axis=0)[0]
    x1, h2_rows, ls8, w8, seg, counts, g2 = _mixer(
        x, c, ctx, c_ctx, w_ada[0], b_ada[0], norm1_g[0], norm2_g[0], w_in[0], s5_lam_re[0], s5_lam_im[0],
        s5_log_dt[0], s5_b_re[0], s5_b_im[0], s5_c_re[0], s5_c_im[0], s5_d[0], s5_w_glu[0], lb, hg_norm_g[0],
        p_a[0], p_b[0], w_out[0], moe_w_router[0], moe_b_router[0])
    gstart, seg_cnt, seg_off, pad, block_e, n_used, n_blocks = _moe_plan(seg, counts, n * TOP_K)
    ls_flat, w_flat = ls8.T.reshape(n * TOP_K), w8.T.reshape(n * TOP_K)
    xs = _dispatch(gstart, seg_cnt, seg_off, pad, ls_flat, h2_rows, n_blocks, MOE_TILE)
    ys = _experts(block_e, n_used, xs, moe_w1[0], moe_w3[0], moe_w2[0])
    out = _combine(gstart, seg_cnt, seg_off, ls_flat, w_flat, x1, h2_rows, g2, moe_ws1[0].astype(BF16),
                   moe_ws3[0].astype(BF16), moe_ws2[0].astype(BF16), final_norm_g, ys, l, MOE_TILE)
    return out.reshape(b, l, d)
```

```python
import functools
import math

import jax
import jax.numpy as jnp
from jax import lax
from jax.experimental import pallas as pl
from jax.experimental.pallas import tpu as pltpu

F32 = jnp.float32
BF16 = jnp.bfloat16

GRID_W = 64
S5_WIDTH = 256
S5_GROUP = 16
S5_GROUPS = 16
S5_STATE = 64
HG_HEADS = 6
HG_DK = 128
HG_WIDTH = HG_HEADS * HG_DK
N_EXPERTS = 64
ROUTE_GROUPS = 8
TOPK_GROUPS = 4
TOP_K = 8
ROUTED_SCALE = 2.5
EPS = 1e-6

LANES = 128
SUBLANES = 8

TOK_TILE = 512
S5_T = 16
HG_CHUNK = 64
HG_BATCH = 4
VMEM_LIMIT = 56 * 1024 * 1024

_NT = (((1,), (1,)), ((), ()))
_TN = (((0,), (0,)), ((), ()))


def _params(*sem):
    return pltpu.CompilerParams(dimension_semantics=sem, vmem_limit_bytes=VMEM_LIMIT)


def _dot(a, b):
    return jnp.dot(a, b, preferred_element_type=F32)


def _sigmoid(x):
    return 0.5 * jnp.tanh(0.5 * x) + 0.5


def _ada_kernel(c_ref, w_ref, b_ref, o_ref):
    c = c_ref[...]
    s = (c * _sigmoid(c)).astype(BF16)
    o_ref[...] = _dot(s, w_ref[...].astype(BF16)) + b_ref[...]


def _ada(c8, w_ada, b_ada):
    d, n = w_ada.shape
    tn = 1536
    return pl.pallas_call(
        _ada_kernel,
        out_shape=jax.ShapeDtypeStruct((8, n), F32),
        grid=(n // tn,),
        in_specs=[pl.BlockSpec((8, d), lambda j: (0, 0)),
                  pl.BlockSpec((d, tn), lambda j: (0, j)),
                  pl.BlockSpec((1, tn), lambda j: (0, j))],
        out_specs=pl.BlockSpec((8, tn), lambda j: (0, j)),
        compiler_params=_params("arbitrary"),
        name="ada_mod",
    )(c8, w_ada, b_ada.reshape(1, n))


_IN_PIECES = (("u", 0, 256, BF16), ("q", 256, 768, BF16), ("ff", 1024, 768, BF16),
              ("fb", 1792, 768, BF16), ("i", 2560, 768, BF16), ("go", 3328, 768, BF16),
              ("ga", 4096, 1024, BF16), ("gb", 5120, 1024, BF16))


def _fold_rows(val, buf_a, buf_b):
    t = val.shape[0]
    buf_a[...] = val[:, :LANES]
    buf_b[...] = val[:, LANES:]
    pieces = []
    for s in range(S5_T):
        pieces += [buf_a[pl.ds(s, t // S5_T, stride=S5_T), :], buf_b[pl.ds(s, t // S5_T, stride=S5_T), :]]
    return jnp.concatenate(pieces, axis=-1)


def _unfold_rows(val, buf_a, buf_b):
    r = val.shape[0]
    for s in range(S5_T):
        buf_a[pl.ds(s, r, stride=S5_T), :] = val[:, s * S5_WIDTH:s * S5_WIDTH + LANES]
        buf_b[pl.ds(s, r, stride=S5_T), :] = val[:, s * S5_WIDTH + LANES:(s + 1) * S5_WIDTH]
    return jnp.concatenate([buf_a[...], buf_b[...]], axis=-1)


def _grid_transpose_matrix(tm):
    i = jnp.arange(tm)
    src = (i % (tm // GRID_W)) * GRID_W + i // (tm // GRID_W)
    return (src[:, None] == jnp.arange(tm)[None, :]).astype(BF16)


def _inproj_kernel(x_ref, sc_ref, sh_ref, g_ref, w_ref, p_ref, *o_refs):
    o_refs, (fold_a, fold_b) = o_refs[:len(_IN_PIECES)], o_refs[len(_IN_PIECES):]
    x = x_ref[...]
    y = x * lax.rsqrt(jnp.mean(x * x, axis=-1, keepdims=True) + EPS) * g_ref[...]
    h = (y * (1.0 + sc_ref[0]) + sh_ref[0]).astype(BF16)
    h_cm = None
    for (name, a, wd, _), o_ref in zip(_IN_PIECES, o_refs):
        if name == "u":
            o_ref[...] = _fold_rows(_dot(h, w_ref[:, a:a + wd]), fold_a, fold_b).astype(o_ref.dtype)
        elif len(o_ref.shape) == 2:
            o_ref[...] = _dot(h, w_ref[:, a:a + wd]).astype(o_ref.dtype)
        else:
            if h_cm is None:
                h_cm = _dot(p_ref[...], h).astype(BF16)
            o_ref[0] = _dot(h_cm, w_ref[:, a:a + wd]).astype(o_ref.dtype).reshape(o_ref.shape[1:])


_COLMAJOR_PIECES = ("q", "ff", "fb", "i")


def _inproj(x2d, sc, sh, g, w_bf16, rows_per_mod, tm, colmajor):
    n, d = x2d.shape
    per = rows_per_mod // tm
    mod_map = (lambda i: (i // per, 0, 0)) if sc.shape[0] > 1 else (lambda i: (0, 0, 0))
    shapes, specs = [], []
    for name, _, wd, dt in _IN_PIECES:
        if colmajor and name in _COLMAJOR_PIECES:
            shapes.append(jax.ShapeDtypeStruct((n // rows_per_mod, GRID_W, rows_per_mod // GRID_W, wd), dt))
            specs.append(pl.BlockSpec((1, GRID_W, tm // GRID_W, wd), lambda i: (i // per, 0, i % per, 0)))
        elif name == "u":
            shapes.append(jax.ShapeDtypeStruct((n // S5_T, S5_T * wd), dt))
            specs.append(pl.BlockSpec((tm // S5_T, S5_T * wd), lambda i: (i, 0)))
        else:
            shapes.append(jax.ShapeDtypeStruct((n, wd), dt))
            specs.append(pl.BlockSpec((tm, wd), lambda i: (i, 0)))
    return pl.pallas_call(
        _inproj_kernel,
        out_shape=shapes,
        grid=(n // tm,),
        in_specs=[pl.BlockSpec((tm, d), lambda i: (i, 0)),
                  pl.BlockSpec((1, 1, d), mod_map),
                  pl.BlockSpec((1, 1, d), mod_map),
                  pl.BlockSpec((1, d), lambda i: (0, 0)),
                  pl.BlockSpec(w_bf16.shape, lambda i: (0, 0)),
                  pl.BlockSpec((tm, tm), lambda i: (0, 0))],
        out_specs=specs,
        scratch_shapes=[pltpu.VMEM((tm, LANES), F32), pltpu.VMEM((tm, LANES), F32)],
        compiler_params=_params("arbitrary"),
        name="in_proj",
    )(x2d, sc, sh, g.reshape(1, d), w_bf16, _grid_transpose_matrix(tm))


def _hgrn_gates(zf, lb):
    sig = _sigmoid(zf)
    logf = jnp.log(lb + (1.0 - lb) * sig)
    k = (1.0 - lb) * (1.0 - sig)
    return logf, k


def _chunk_cumsum(cs, logf):
    hi = logf.astype(BF16)
    lo = (logf - hi.astype(F32)).astype(BF16)
    return _dot(cs, hi) + _dot(cs, lo)


def _hgrn_state_step(zf, v, lb, st, cs, reverse):
    logf, k = _hgrn_gates(zf, lb)
    cum = _chunk_cumsum(cs, logf)
    t = 0 if reverse else HG_CHUNK - 1
    total = cum[t:t + 1, :]
    kdec = (k * jnp.exp(total - cum)).astype(BF16)
    st_new = st * jnp.exp(total) + lax.dot_general(v.astype(BF16), kdec, _TN, preferred_element_type=F32)
    return cum, k, st_new


def _hgrn_kernel(*refs, reverse, final, n_ctx_chunks):
    if final:
        q_all, f_all, v_all, cf_ref, cv_ref, lb_ref, of_all, g_ref, o_all, st_ref = refs
    else:
        q_all, f_all, v_all, cf_ref, cv_ref, lb_ref, o_all, st_ref = refs
        of_all = None
    n_batch = q_all.shape[0]
    c_len = HG_CHUNK
    n_rows = q_all.shape[2]
    n_chunks = n_rows // c_len
    row = lax.broadcasted_iota(jnp.int32, (n_rows, n_rows), 0)
    col = lax.broadcasted_iota(jnp.int32, (n_rows, n_rows), 1)
    tri = (col >= row) if reverse else (col <= row)
    same_chunk = None
    for c in range(n_chunks):
        lo, hi = c * c_len, (c + 1) * c_len
        blk = (row >= lo) & (row < hi) & (col >= lo) & (col < hi)
        same_chunk = blk if same_chunk is None else (same_chunk | blk)
    mask = tri & same_chunk
    cs = jnp.where(mask, 1.0, 0.0).astype(BF16)

    @pl.when(pl.program_id(1) == 0)
    def _():
        cs1 = cs[:c_len, :c_len]
        order = range(n_ctx_chunks - 1, -1, -1) if reverse else range(n_ctx_chunks)
        for bi in range(n_batch):
            for h in range(HG_HEADS):
                cols = slice(h * HG_DK, (h + 1) * HG_DK)
                st = jnp.zeros((HG_DK, HG_DK), F32)
                for c in order:
                    rows = slice(c * c_len, (c + 1) * c_len)
                    _, _, st = _hgrn_state_step(cf_ref[bi, rows, cols].astype(F32), cv_ref[bi, rows, cols].astype(F32),
                                                lb_ref[:, cols], st, cs1, reverse)
                st_ref[bi * HG_HEADS + h] = st

    def chunk_rows(x, r):
        return [x[c * c_len + r:c * c_len + r + 1, :] for c in range(n_chunks)]

    def over_chunks(rows):
        return jnp.concatenate([jnp.broadcast_to(r, (c_len, r.shape[1])) for r in rows], axis=0)

    lb = lb_ref[...]
    r_ref = c_len // 2 - 1 if reverse else c_len // 2
    r_tot = 0 if reverse else c_len - 1
    order = range(n_chunks - 1, -1, -1) if reverse else range(n_chunks)
    for bi in range(n_batch):
        q = q_all[bi, 0].astype(F32)
        v = v_all[bi, 0]
        logf, k = _hgrn_gates(f_all[bi, 0].astype(F32), lb)
        cum = _chunk_cumsum(cs, logf)
        ref_rows, tot_rows = chunk_rows(cum, r_ref), chunk_rows(cum, r_tot)
        ref = over_chunks(ref_rows)
        qe = q * jnp.exp(cum - ref)
        ke = k * jnp.exp(ref - cum)
        qi, ki = qe.astype(BF16), ke.astype(BF16)
        q_in = (qe * over_chunks([jnp.exp(r) for r in ref_rows])).astype(BF16)
        kdec = (ke * over_chunks([jnp.exp(t - r) for t, r in zip(tot_rows, ref_rows)])).astype(BF16)
        for h in range(HG_HEADS):
            cols = slice(h * HG_DK, (h + 1) * HG_DK)
            s = lax.dot_general(qi[:, cols], ki[:, cols], _NT, preferred_element_type=F32)
            o_intra = _dot(jnp.where(mask, s, 0.0).astype(BF16), v[:, cols])
            st = st_ref[bi * HG_HEADS + h]
            for c in order:
                rows = slice(c * c_len, (c + 1) * c_len)
                o = o_intra[rows] + lax.dot_general(q_in[rows, cols], st.astype(BF16), _NT,
                                                    preferred_element_type=F32)
                total = cum[c * c_len + r_tot:c * c_len + r_tot + 1, cols]
                st = st * jnp.exp(total) + lax.dot_general(v[rows, cols], kdec[rows, cols], _TN,
                                                           preferred_element_type=F32)
                if final:
                    o = o + of_all[bi, 0, rows, cols].astype(F32)
                    o = o * lax.rsqrt(jnp.mean(o * o, axis=-1, keepdims=True) + EPS) * g_ref[...]
                o_all[bi, 0, rows, cols] = o.astype(o_all.dtype)
            st_ref[bi * HG_HEADS + h] = st


def _hgrn_pass(q, f, v, cf, cv, lb, o_prev, g, *, reverse):
    b, nw, rows, _ = q.shape
    nb = HG_BATCH if b % HG_BATCH == 0 else 1
    final = o_prev is not None
    wmap = (lambda bi, w: (bi, nw - 1 - w, 0, 0)) if reverse else (lambda bi, w: (bi, w, 0, 0))
    blk = pl.BlockSpec((nb, 1, rows, HG_WIDTH), wmap)
    cblk = pl.BlockSpec((nb, cf.shape[1], HG_WIDTH), lambda bi, w: (bi, 0, 0))
    in_specs = [blk, blk, blk, cblk, cblk, pl.BlockSpec((1, HG_WIDTH), lambda bi, w: (0, 0))]
    args = [q, f, v, cf, cv, lb]
    if final:
        in_specs += [blk, pl.BlockSpec((1, HG_DK), lambda bi, w: (0, 0))]
        args += [o_prev, g]
    return pl.pallas_call(
        functools.partial(_hgrn_kernel, reverse=reverse, final=final, n_ctx_chunks=cf.shape[1] // HG_CHUNK),
        out_shape=jax.ShapeDtypeStruct(q.shape, BF16),
        grid=(b // nb, nw),
        in_specs=in_specs,
        out_specs=blk,
        scratch_shapes=[pltpu.VMEM((nb * HG_HEADS, HG_DK, HG_DK), F32)],
        compiler_params=_params("arbitrary", "arbitrary"),
        name="hgrn_bwd" if reverse else "hgrn_fwd",
    )(*args)


def _s5_weights(lam_re, lam_im, log_dt, b_re, b_im, c_re, c_im):
    g, p, cc, t = S5_GROUPS, S5_STATE, S5_GROUP, S5_T
    lre = jnp.minimum(lam_re.astype(F32), -1e-4)
    lim = lam_im.astype(F32)
    dt = jnp.exp(log_dt.astype(F32))[..., None]
    ks = jnp.arange(t + 1, dtype=F32)[:, None, None, None]
    mag = jnp.exp(ks * (lre * dt)[None])
    pw_re = mag * jnp.cos(ks * (lim * dt)[None])
    pw_im = mag * jnp.sin(ks * (lim * dt)[None])
    nr, ni = pw_re[1] - 1.0, pw_im[1]
    den = lre * lre + lim * lim
    cf_re = (nr * lre + ni * lim) / den
    cf_im = (ni * lre - nr * lim) / den
    bb_re = cf_re[..., None] * b_re - cf_im[..., None] * b_im
    bb_im = cf_re[..., None] * b_im + cf_im[..., None] * b_re
    cre, cim = c_re.astype(F32), c_im.astype(F32)
    sw, ns = S5_WIDTH, 2 * g * p
    grp_of_row = jnp.arange(sw)[:, None] // cc

    qr = pw_re[:t].transpose(1, 2, 0, 3)[:, :, :, None, :]
    qi = pw_im[:t].transpose(1, 2, 0, 3)[:, :, :, None, :]
    cp_re = (cre[None, :, None] * qr - cim[None, :, None] * qi).reshape(2, g, t * cc, p)
    cp_im = (cre[None, :, None] * qi + cim[None, :, None] * qr).reshape(2, g, t * cc, p)

    def contract_p(cp, bb):
        return lax.dot_general(cp, bb, (((3,), (2,)), ((0, 1), (0, 1))), precision=lax.Precision.HIGHEST,
                               preferred_element_type=F32)

    kk = (contract_p(cp_re, bb_re) - contract_p(cp_im, bb_im)).reshape(2, g, t, cc, cc).transpose(0, 2, 1, 3, 4)
    kf, kb = kk[0], kk[1]
    kall = jnp.concatenate([kb[:0:-1], (kf[0] + kb[0])[None], kf[1:]], axis=0)
    kt = kall.transpose(0, 1, 3, 2).reshape(2 * t - 1, sw, cc)

    def spread(x, period, reps):
        sel = (jnp.arange(period)[:, None] == (jnp.arange(period * reps)[None, :] % period)).astype(BF16)
        return jnp.dot(x.astype(BF16), sel, preferred_element_type=BF16)

    same = grp_of_row == (jnp.arange(sw)[None, :] // cc)
    d_lag = jnp.where(same[None], spread(kt, cc, g), 0)

    same_in = ((jnp.arange(t * sw)[:, None] // cc) % g) == ((jnp.arange(2 * ns)[None, :] % (g * p)) // p)

    def in_to_state(pre, pim, bre, bim):
        xre = pre[..., None] * bre[None] - pim[..., None] * bim[None]
        xim = pre[..., None] * bim[None] + pim[..., None] * bre[None]
        return [spread(xre.transpose(0, 1, 3, 2).reshape(t * sw, p), p, g),
                spread(xim.transpose(0, 1, 3, 2).reshape(t * sw, p), p, g)]

    w_in = jnp.concatenate(in_to_state(pw_re[t - 1::-1, 0], pw_im[t - 1::-1, 0], bb_re[0], bb_im[0])
                           + in_to_state(pw_re[:t, 1], pw_im[:t, 1], bb_re[1], bb_im[1]), axis=1)
    w_in = jnp.where(same_in, w_in, 0)

    same_out = ((jnp.arange(ns)[:, None] % (g * p)) // p) == ((jnp.arange(t * sw)[None, :] // cc) % g)
    col = jnp.arange(t * sw)
    pick = (jnp.arange(t * cc)[:, None] == ((col // sw) * cc + col % cc)[None, :]).astype(BF16)

    def state_to_out(pre, pim):
        are = cre[None] * pre[:, :, None, :] - cim[None] * pim[:, :, None, :]
        aim = cre[None] * pim[:, :, None, :] + cim[None] * pre[:, :, None, :]
        a = jnp.concatenate([are.transpose(1, 3, 0, 2), -aim.transpose(1, 3, 0, 2)], axis=0)
        a = jnp.dot(a.reshape(ns, t * cc).astype(BF16), pick, preferred_element_type=BF16)
        return jnp.where(same_out, a, 0)

    w_out_f = state_to_out(pw_re[1:, 0], pw_im[1:, 0])
    w_out_b = state_to_out(pw_re[t:0:-1, 1], pw_im[t:0:-1, 1])

    decay = jnp.stack([pw_re[t].reshape(2, g * p), pw_im[t].reshape(2, g * p)], axis=1)
    return d_lag, w_in, w_out_f, w_out_b, decay


def _s5_in_kernel(ul_ref, uc_ref, d_ref, w_ref, o_ref, u_ref):
    j = pl.program_id(1)

    @pl.when(j == 0)
    def _():
        kl, kc = ul_ref.shape[1], uc_ref.shape[1]
        for bi in range(ul_ref.shape[0]):
            u_ref[bi * (kl + kc):bi * (kl + kc) + kl, :] = ul_ref[bi]
            u_ref[bi * (kl + kc) + kl:(bi + 1) * (kl + kc), :] = uc_ref[bi]

    @pl.when(j < S5_T)
    def _():
        acc = _dot(u_ref[:, 0:S5_WIDTH], d_ref[j + S5_T - 1])
        for s in range(1, S5_T):
            acc = acc + _dot(u_ref[:, s * S5_WIDTH:(s + 1) * S5_WIDTH], d_ref[j - s + S5_T - 1])
        o_ref[...] = acc

    @pl.when(j >= S5_T)
    def _():
        o_ref[...] = _dot(u_ref[...], w_ref[...])


def _s5_in(u_lat, u_ctx, d_lag, w_in, nb_tile):
    b, kl, k = u_lat.shape
    kc = u_ctx.shape[1]
    tm = nb_tile * (kl + kc)
    tn = S5_WIDTH
    nj = (k + w_in.shape[1]) // tn
    return pl.pallas_call(
        _s5_in_kernel,
        out_shape=jax.ShapeDtypeStruct((b * (kl + kc), nj * tn), F32),
        grid=(b // nb_tile, nj),
        in_specs=[pl.BlockSpec((nb_tile, kl, k), lambda i, j: (i, 0, 0)),
                  pl.BlockSpec((nb_tile, kc, k), lambda i, j: (i, 0, 0)),
                  pl.BlockSpec(d_lag.shape, lambda i, j: (0, 0, 0)),
                  pl.BlockSpec((k, tn), lambda i, j: (0, jnp.maximum(j - S5_T, 0)))],
        out_specs=pl.BlockSpec((tm, tn), lambda i, j: (i, j)),
        scratch_shapes=[pltpu.VMEM((tm, k), BF16)],
        compiler_params=_params("arbitrary", "arbitrary"),
        name="s5_in",
    )(u_lat, u_ctx, d_lag, w_in)


def _s5_scan_kernel(efr_ref, efi_ref, ebr_ref, ebi_ref, a_ref, hfr_ref, hfi_ref, hbr_ref, hbi_ref,
                    *, nb, rows_in, rows_out):
    dirs = ((efr_ref, efi_ref, hfr_ref, hfi_ref, a_ref[0, 0:1, :], a_ref[0, 1:2, :]),
            (ebr_ref, ebi_ref, hbr_ref, hbi_ref, a_ref[1, 0:1, :], a_ref[1, 1:2, :]))
    zero = jnp.zeros((nb, dirs[0][4].shape[1]), F32)

    def step(srcs, carry, store):
        new = []
        for di, (er_ref, ei_ref, hr_ref, hi_ref, are, aim) in enumerate(dirs):
            hre, him = carry[2 * di], carry[2 * di + 1]
            if store:
                hr_ref[pl.ds(srcs[di], nb, stride=rows_out), :] = hre
                hi_ref[pl.ds(srcs[di], nb, stride=rows_out), :] = him
            ere = er_ref[pl.ds(srcs[di], nb, stride=rows_in), :]
            eim = ei_ref[pl.ds(srcs[di], nb, stride=rows_in), :]
            new += [are * hre - aim * him + ere, are * him + aim * hre + eim]
        return tuple(new)

    n_ctx = rows_in - rows_out
    carry = lax.fori_loop(0, n_ctx, lambda s, c: step((rows_out + s, rows_in - 1 - s), c, False),
                          tuple([zero] * 4))

    def two_steps(s2, c):
        c = step((2 * s2, rows_out - 1 - 2 * s2), c, True)
        return step((2 * s2 + 1, rows_out - 2 - 2 * s2), c, True)

    lax.fori_loop(0, rows_out // 2, two_steps, carry)


def _s5_scan(e, decay, nb, rows_in, rows_out):
    assert rows_out % 2 == 0, "the latent chunks are scanned two per loop iteration"
    tc = LANES
    nsr = S5_GROUPS * S5_STATE
    c0 = (S5_T * S5_WIDTH) // tc
    nt = nsr // tc
    eblk = lambda k: pl.BlockSpec((nb * rows_in, tc), lambda j: (0, c0 + k * nt + j))
    hblk = pl.BlockSpec((nb * rows_out, tc), lambda j: (0, j))
    return pl.pallas_call(
        functools.partial(_s5_scan_kernel, nb=nb, rows_in=rows_in, rows_out=rows_out),
        out_shape=[jax.ShapeDtypeStruct((nb * rows_out, nsr), F32)] * 4,
        grid=(nt,),
        in_specs=[eblk(0), eblk(1), eblk(2), eblk(3), pl.BlockSpec((2, 2, tc), lambda j: (0, 0, j))],
        out_specs=[hblk] * 4,
        compiler_params=_params("arbitrary"),
        name="s5_scan",
    )(e, e, e, e, decay)


def _gelu_tanh(x):
    return 0.5 * x * (1.0 + jnp.tanh(math.sqrt(2.0 / math.pi) * (x + 0.044715 * x * x * x)))


def _s5_out_kernel(hfr_ref, hfi_ref, hbr_ref, hbi_ref, wf_ref, wb_ref, yi_ref, u_ref, d_ref, wg_ref, o_ref, hb16):
    nsr = hfr_ref.shape[1]

    @pl.when(pl.program_id(1) == 0)
    def _():
        for k, h_ref in enumerate((hfr_ref, hfi_ref, hbr_ref, hbi_ref)):
            hb16[k] = h_ref[...].astype(BF16)

    y = yi_ref[0] + d_ref[...] * u_ref[...].astype(F32)
    for k, (w_ref, r0) in enumerate(((wf_ref, 0), (wf_ref, nsr), (wb_ref, 0), (wb_ref, nsr))):
        y = y + _dot(hb16[k], w_ref[r0:r0 + nsr, :])
    y = _gelu_tanh(y)
    gate = _sigmoid(_dot(y.astype(BF16), wg_ref[...]))
    o_ref[...] = (y * gate).astype(o_ref.dtype)


def _s5_out(states, w_out_f, w_out_b, e3, u_rows, d_row, w_glu):
    m, nsr = states[0].shape
    nb = e3.shape[0]
    tm = m // nb
    tn = S5_WIDTH
    st = pl.BlockSpec((tm, nsr), lambda i, j: (i, 0))
    wo = pl.BlockSpec((2 * nsr, tn), lambda i, j: (0, j))
    return pl.pallas_call(
        _s5_out_kernel,
        out_shape=jax.ShapeDtypeStruct((m, S5_T * S5_WIDTH), BF16),
        grid=(nb, S5_T),
        in_specs=[st, st, st, st, wo, wo,
                  pl.BlockSpec((1, tm, tn), lambda i, j: (i, 0, j)),
                  pl.BlockSpec((tm, tn), lambda i, j: (i, j)),
                  pl.BlockSpec((1, tn), lambda i, j: (0, 0)),
                  pl.BlockSpec((tn, tn), lambda i, j: (0, 0))],
        out_specs=pl.BlockSpec((tm, tn), lambda i, j: (i, j)),
        scratch_shapes=[pltpu.VMEM((4, tm, nsr), BF16)],
        compiler_params=_params("arbitrary", "arbitrary"),
        name="s5_out",
    )(*states, w_out_f, w_out_b, e3, u_rows, d_row, w_glu)


U32 = jnp.uint32
ROW_SUB = 4


def _to_token_rows(ref, val):
    t, d = val.shape
    bits = lax.bitcast_convert_type(val.astype(BF16).astype(F32), U32)
    w = (bits[:, :d // 2] >> 16) | bits[:, d // 2:]
    for s in range(ROW_SUB):
        ref[pl.ds(s, t, stride=ROW_SUB), :] = w[:, s * LANES:(s + 1) * LANES]


def _from_token_rows(ref, t, row0=0):
    w = jnp.concatenate([ref[pl.ds(row0 * ROW_SUB + s, t, stride=ROW_SUB), :] for s in range(ROW_SUB)], axis=-1)
    lo = lax.bitcast_convert_type(w << 16, F32)
    hi = lax.bitcast_convert_type(w & jnp.uint32(0xFFFF0000), F32)
    return jnp.concatenate([lo, hi], axis=-1)


def _route(h2b, wr_ref, br_ref, cnt_ref, ls8_ref, w8_ref, seg_ref):
    tm = h2b.shape[0]
    per_group = N_EXPERTS // ROUTE_GROUPS
    scores = _sigmoid(lax.dot_general(wr_ref[...], h2b, _NT, preferred_element_type=F32))
    biased = scores + br_ref[...]
    neg = -jnp.inf
    sub = lax.broadcasted_iota(jnp.int32, (per_group, tm), 0)
    grp = []
    for gi in range(ROUTE_GROUPS):
        v = biased[gi * per_group:(gi + 1) * per_group, :]
        m1 = jnp.max(v, axis=0, keepdims=True)
        first = jnp.min(jnp.where(v == m1, sub, per_group), axis=0, keepdims=True)
        m2 = jnp.max(jnp.where(sub == first, neg, v), axis=0, keepdims=True)
        grp.append(m1 + m2)
    grp = jnp.concatenate(grp, axis=0)
    gid = lax.broadcasted_iota(jnp.int32, (ROUTE_GROUPS, tm), 0)
    beaten = jnp.zeros((ROUTE_GROUPS, tm), jnp.int32)
    for gj in range(ROUTE_GROUPS):
        r = grp[gj:gj + 1, :]
        beaten = beaten + jnp.where((r > grp) | ((r == grp) & (gj < gid)), 1, 0)
    group_ok = beaten < TOPK_GROUPS
    expert_ok = jnp.concatenate(
        [jnp.broadcast_to(group_ok[gi:gi + 1, :], (per_group, tm)) for gi in range(ROUTE_GROUPS)], axis=0)
    cur = jnp.where(expert_ok, biased, neg)
    eid = lax.broadcasted_iota(jnp.int32, (N_EXPERTS, tm), 0)
    sel = jnp.zeros((N_EXPERTS, tm), F32)
    picks, wts = [], []
    for _ in range(TOP_K):
        m = jnp.max(cur, axis=0, keepdims=True)
        idx = jnp.min(jnp.where(cur == m, eid, N_EXPERTS), axis=0, keepdims=True)
        hit = eid == idx
        picks.append(idx)
        wts.append(jnp.sum(jnp.where(hit, scores, 0.0), axis=0, keepdims=True))
        sel = jnp.where(hit, 1.0, sel)
        cur = jnp.where(hit, neg, cur)
    wsum = wts[0]
    for w in wts[1:]:
        wsum = wsum + w
    selb = sel.astype(BF16)
    ti = lax.broadcasted_iota(jnp.int32, (tm, tm), 0)
    tj = lax.broadcasted_iota(jnp.int32, (tm, tm), 1)
    rank = _dot(selb, jnp.where(ti < tj, 1.0, 0.0).astype(BF16))
    seg_units = jnp.ceil(jnp.sum(sel, axis=1, keepdims=True) * (1.0 / SEG_ALIGN))
    ei = lax.broadcasted_iota(jnp.int32, (N_EXPERTS, N_EXPERTS), 0)
    ej = lax.broadcasted_iota(jnp.int32, (N_EXPERTS, N_EXPERTS), 1)
    units_row = jnp.broadcast_to(seg_units, (N_EXPERTS, LANES)).astype(BF16)
    seg_off = _dot(jnp.where(ej < ei, 1.0, 0.0).astype(BF16), units_row)[:, 0:1] * SEG_ALIGN
    seg_rows = seg_units * SEG_ALIGN
    slot = seg_off + rank
    for k in range(TOP_K):
        w8_ref[k:k + 1, :] = wts[k] / wsum * ROUTED_SCALE
        ls8_ref[k:k + 1, :] = (jnp.sum(jnp.where(eid == picks[k], slot, 0.0), axis=0, keepdims=True)
                               * ROW_SUB).astype(jnp.int32)
    lane = lax.broadcasted_iota(jnp.int32, (N_EXPERTS, LANES), 1)
    seg_ref[0] = jnp.where(lane == 0, cnt_ref[...], jnp.where(lane == 1, seg_rows, seg_off))
    cnt_ref[...] = cnt_ref[...] + seg_rows


def _merge_kernel(x_ref, ya_ref, on_ref, go_ref, ga_ref, gb_ref, g1_ref, sc_ref, sh_ref, n2_ref,
                  pa_ref, pb_ref, wo_ref, wr_ref, br_ref, pt_ref,
                  x1_ref, h2_ref, ls8_ref, w8_ref, seg_ref, cnt_ref, fold_a, fold_b):
    @pl.when(pl.program_id(0) == 0)
    def _():
        cnt_ref[...] = jnp.zeros_like(cnt_ref)

    go = go_ref[...].astype(F32)
    on = _dot(pt_ref[...], on_ref[0].reshape(x_ref.shape[0], HG_WIDTH))
    y_b = (on * (go * _sigmoid(go))).astype(BF16)
    y_a = _unfold_rows(ya_ref[...].astype(F32), fold_a, fold_b).astype(BF16)
    pa = _dot(y_a, pa_ref[...])
    pb = _dot(y_b, pb_ref[...])
    merged = _sigmoid(ga_ref[...].astype(F32)) * pa + _sigmoid(gb_ref[...].astype(F32)) * pb
    x1 = x_ref[...] + g1_ref[0] * _dot(merged.astype(BF16), wo_ref[...])
    x1_ref[...] = x1
    y = x1 * lax.rsqrt(jnp.mean(x1 * x1, axis=-1, keepdims=True) + EPS) * n2_ref[...]
    h2 = y * (1.0 + sc_ref[0]) + sh_ref[0]
    _to_token_rows(h2_ref, h2)
    _route(h2.astype(BF16), wr_ref, br_ref, cnt_ref, ls8_ref, w8_ref, seg_ref)


def _merge(x2d, ya, on, go, ga, gb, g1, sc2, sh2, n2g, pa, pb, wo, wr_t, br, rows_per_batch, tm):
    n, d = x2d.shape
    per = rows_per_batch // tm
    row = lambda wd: pl.BlockSpec((tm, wd), lambda i: (i, 0))
    mod = pl.BlockSpec((1, 1, d), lambda i: (i // per, 0, 0))
    full = lambda a: pl.BlockSpec(a.shape, lambda i: (0, 0))
    tok = pl.BlockSpec((TOP_K, tm), lambda i: (0, i))
    return pl.pallas_call(
        _merge_kernel,
        out_shape=[jax.ShapeDtypeStruct((n, d), F32), jax.ShapeDtypeStruct((n * ROW_SUB, LANES), U32),
                   jax.ShapeDtypeStruct((TOP_K, n), jnp.int32), jax.ShapeDtypeStruct((TOP_K, n), F32),
                   jax.ShapeDtypeStruct((n // tm, N_EXPERTS, LANES), F32),
                   jax.ShapeDtypeStruct((N_EXPERTS, 1), F32)],
        grid=(n // tm,),
        in_specs=[row(d), pl.BlockSpec((tm // S5_T, S5_T * S5_WIDTH), lambda i: (i, 0)),
                  pl.BlockSpec((1, GRID_W, tm // GRID_W, HG_WIDTH), lambda i: (i // per, 0, i % per, 0)),
                  row(HG_WIDTH), row(d), row(d), mod, mod, mod,
                  pl.BlockSpec((1, d), lambda i: (0, 0)), full(pa), full(pb), full(wo), full(wr_t), full(br),
                  pl.BlockSpec((tm, tm), lambda i: (0, 0))],
        out_specs=[row(d), pl.BlockSpec((tm * ROW_SUB, LANES), lambda i: (i, 0)), tok, tok,
                   pl.BlockSpec((1, N_EXPERTS, LANES), lambda i: (i, 0, 0)),
                   pl.BlockSpec((N_EXPERTS, 1), lambda i: (0, 0))],
        scratch_shapes=[pltpu.VMEM((tm, LANES), F32), pltpu.VMEM((tm, LANES), F32)],
        compiler_params=_params("arbitrary"),
        name="merge_out_proj_route",
    )(x2d, ya, on, go, ga, gb, g1, sc2, sh2, n2g.reshape(1, d), pa, pb, wo, wr_t, br,
      _grid_transpose_matrix(tm).T)


MOE_TILE = TOK_TILE
TOKEN_UNROLL = 4
COMBINE_UNROLL = 8
SEG_ALIGN = 8
FILL_ROWS = 512
STAGE_ROWS = MOE_TILE * TOP_K + FILL_ROWS
MOE_BLK = 1024


def _wait_rows(any_ref, sem, n_rows):
    view = any_ref.at[pl.ds(0, n_rows * ROW_SUB)]
    pltpu.make_async_copy(view, view, sem).wait()


def _rows(ref, r0, n):
    return ref.at[pl.ds(pl.multiple_of(r0 * ROW_SUB, ROW_SUB), n * ROW_SUB)]


def _pow2_pieces(n, max_piece, fn, min_piece=1):
    done = 0
    piece = max_piece
    while piece >= min_piece:
        hit = (n & piece) != 0
        pl.when(hit)(functools.partial(fn, done, piece))
        done = done + (n & piece)
        piece //= 2


def _copy_rows(src_ref, src0, dst_ref, dst0, n, max_piece, sem, min_piece=1):
    def piece(off, size):
        pltpu.make_async_copy(_rows(src_ref, src0 + off, size), _rows(dst_ref, dst0 + off, size), sem).start()
    _pow2_pieces(n, max_piece, piece, min_piece)


def _wait_copied_rows(src_ref, dst_ref, n, max_piece, sem):
    def piece(off, size):
        pltpu.make_async_copy(_rows(src_ref, 0, size), _rows(dst_ref, 0, size), sem).wait()
    _pow2_pieces(n, max_piece, piece)


def _copy_tile_segments(i, src_ref, src_tab, dst_ref, dst_tab, cnt_ref, off_ref, fill_src0, fill_dst0, sem):
    def per_expert(e, carry):
        _copy_rows(src_ref, src_tab[i, e], dst_ref, dst_tab[i, e], cnt_ref[i, e], MOE_TILE, sem, SEG_ALIGN)
        return carry

    lax.fori_loop(0, N_EXPERTS, per_expert, 0)
    used = off_ref[i, N_EXPERTS - 1] + cnt_ref[i, N_EXPERTS - 1]
    _copy_rows(src_ref, fill_src0(used), dst_ref, fill_dst0(used), STAGE_ROWS - used, FILL_ROWS, sem, SEG_ALIGN)


def _dispatch_kernel(gs_ref, cnt_ref, off_ref, pad_ref, ls_ref, h2_ref, xs_hbm, ls_smem, stage0, stage1, zbuf,
                     sem0, sem1, lsem, zsem, *, tm, n_blocks):
    i = pl.program_id(0)
    last = pl.num_programs(0) - 1
    cp = pltpu.make_async_copy(ls_ref, ls_smem, lsem)
    cp.start()
    trash0 = n_blocks * MOE_BLK

    @pl.when(i == 0)
    def _():
        stage0[...] = jnp.zeros_like(stage0)
        stage1[...] = jnp.zeros_like(stage1)
        zbuf[...] = jnp.zeros_like(zbuf)
        cpz = pltpu.make_async_copy(zbuf, _rows(xs_hbm, trash0, 2 * FILL_ROWS), zsem)
        cpz.start()
        cpz.wait()

    cp.wait()

    def tile(stage, sem, prev_sem, trash):
        def body(tu, carry):
            for u in range(TOKEN_UNROLL):
                t = tu * TOKEN_UNROLL + u
                row = h2_ref[pl.ds(pl.multiple_of(t * ROW_SUB, ROW_SUB), ROW_SUB), :]
                for k in range(TOP_K):
                    stage[pl.ds(pl.multiple_of(ls_smem[t * TOP_K + k], ROW_SUB), ROW_SUB), :] = row
            return carry

        lax.fori_loop(0, tm // TOKEN_UNROLL, body, 0)
        _copy_tile_segments(i, stage, off_ref, xs_hbm, gs_ref, cnt_ref, off_ref,
                            lambda used: used, lambda used: trash, sem)

        @pl.when(i > 0)
        def _():
            _wait_rows(xs_hbm, prev_sem, STAGE_ROWS)

        @pl.when(i == last)
        def _():
            _wait_rows(xs_hbm, sem, STAGE_ROWS)

    pl.when(i % 2 == 0)(functools.partial(tile, stage0, sem0, sem1, trash0))
    pl.when(i % 2 == 1)(functools.partial(tile, stage1, sem1, sem0, trash0 + FILL_ROWS))

    def zero_pad(e, carry):
        _copy_rows(zbuf, 0, xs_hbm, pad_ref[0, e], pad_ref[1, e], MOE_BLK // 2, zsem)
        return carry

    def wait_pad(e, carry):
        _wait_copied_rows(zbuf, xs_hbm, pad_ref[1, e], MOE_BLK // 2, zsem)
        return carry

    def zero_block(j, carry):
        pltpu.make_async_copy(zbuf, _rows(xs_hbm, j * MOE_BLK, MOE_BLK), zsem).start()
        return carry

    def wait_block(j, carry):
        pltpu.make_async_copy(zbuf, _rows(xs_hbm, 0, MOE_BLK), zsem).wait()
        return carry

    @pl.when(i == 0)
    def _():
        lax.fori_loop(0, N_EXPERTS, zero_pad, 0)
        lax.fori_loop(pad_ref[2, 0], n_blocks, zero_block, 0)

    @pl.when(i == last)
    def _():
        lax.fori_loop(0, N_EXPERTS, wait_pad, 0)
        lax.fori_loop(pad_ref[2, 0], n_blocks, wait_block, 0)


def _dispatch(gstart, seg_cnt, seg_off, pad, ls8, h2_rows, n_blocks, tm):
    n = ls8.shape[0] // TOP_K
    cap = n_blocks * MOE_BLK + 2 * FILL_ROWS
    return pl.pallas_call(
        functools.partial(_dispatch_kernel, tm=tm, n_blocks=n_blocks),
        out_shape=jax.ShapeDtypeStruct((cap * ROW_SUB, LANES), U32),
        grid_spec=pltpu.PrefetchScalarGridSpec(
            num_scalar_prefetch=4,
            grid=(n // tm,),
            in_specs=[pl.BlockSpec((tm * TOP_K,), lambda i, *_: (i,)),
                      pl.BlockSpec((tm * ROW_SUB, LANES), lambda i, *_: (i, 0))],
            out_specs=pl.BlockSpec(memory_space=pl.ANY),
            scratch_shapes=[pltpu.SMEM((tm * TOP_K,), jnp.int32),
                            pltpu.VMEM((STAGE_ROWS * ROW_SUB, LANES), U32),
                            pltpu.VMEM((STAGE_ROWS * ROW_SUB, LANES), U32),
                            pltpu.VMEM((MOE_BLK * ROW_SUB, LANES), U32),
                            pltpu.SemaphoreType.DMA, pltpu.SemaphoreType.DMA, pltpu.SemaphoreType.DMA,
                            pltpu.SemaphoreType.DMA]),
        compiler_params=pltpu.CompilerParams(dimension_semantics=("arbitrary",), vmem_limit_bytes=VMEM_LIMIT,
                                             has_side_effects=True),
        name="moe_dispatch",
    )(gstart, seg_cnt, seg_off, pad, ls8, h2_rows)


def _expert_kernel(be_ref, nu_ref, x_ref, w1_ref, w3_ref, w2_ref, o_ref, w1b, w3b, w2b):
    j = pl.program_id(0)
    e = be_ref[j]
    prev = be_ref[jnp.maximum(j - 1, 0)]
    used = j < nu_ref[0]

    @pl.when(jnp.logical_and(used, jnp.logical_or(j == 0, e != prev)))
    def _():
        w1b[...] = w1_ref[0].astype(BF16)
        w3b[...] = w3_ref[0].astype(BF16)
        w2b[...] = w2_ref[0].astype(BF16)

    @pl.when(used)
    def _():
        x = _from_token_rows(x_ref, MOE_BLK).astype(BF16)
        a = _dot(x, w1b[...])
        hid = (a * _sigmoid(a)) * _dot(x, w3b[...])
        _to_token_rows(o_ref, _dot(hid.astype(BF16), w2b[...]))

    @pl.when(jnp.logical_not(used))
    def _():
        o_ref[...] = jnp.zeros_like(o_ref)


def _experts(block_e, n_used, xs, w1, w3, w2):
    n_blocks = block_e.shape[0]
    d, f = w1.shape[1], w1.shape[2]
    rows = pl.BlockSpec((MOE_BLK * ROW_SUB, LANES), lambda j, be, nu: (j, 0))
    rows_in = pl.BlockSpec((MOE_BLK * ROW_SUB, LANES), lambda j, be, nu: (jnp.minimum(j, nu[0] - 1), 0))
    return pl.pallas_call(
        _expert_kernel,
        out_shape=jax.ShapeDtypeStruct((n_blocks * MOE_BLK * ROW_SUB, LANES), U32),
        grid_spec=pltpu.PrefetchScalarGridSpec(
            num_scalar_prefetch=2,
            grid=(n_blocks,),
            in_specs=[rows_in,
                      pl.BlockSpec((1, d, f), lambda j, be, nu: (be[j], 0, 0)),
                      pl.BlockSpec((1, d, f), lambda j, be, nu: (be[j], 0, 0)),
                      pl.BlockSpec((1, f, d), lambda j, be, nu: (be[j], 0, 0))],
            out_specs=rows,
            scratch_shapes=[pltpu.VMEM((d, f), BF16), pltpu.VMEM((d, f), BF16), pltpu.VMEM((f, d), BF16)]),
        compiler_params=_params("arbitrary"),
        name="moe_experts",
    )(block_e, n_used, xs, w1, w3, w2)


def _combine_kernel(gs_ref, cnt_ref, off_ref, ls_ref, w8_ref, x1_ref, h2_ref, g2_ref, ws1_ref, ws3_ref, ws2_ref,
                    fg_ref, ys_hbm, o_ref, ls_smem, w_smem, gbuf0, gbuf1, acc_rows, sem0, sem1, lsem, *, tm):
    i = pl.program_id(0)
    last = pl.num_programs(0) - 1
    cp1 = pltpu.make_async_copy(ls_ref, ls_smem, lsem)
    cp2 = pltpu.make_async_copy(w8_ref, w_smem, lsem)
    cp1.start()
    cp2.start()

    def fetch(tile, gbuf, sem):
        _copy_tile_segments(tile, ys_hbm, gs_ref, gbuf, off_ref, cnt_ref, off_ref,
                            lambda used: 0, lambda used: used, sem)

    @pl.when(i == 0)
    def _():
        fetch(0, gbuf0, sem0)

    @pl.when(jnp.logical_and(i < last, i % 2 == 0))
    def _():
        fetch(i + 1, gbuf1, sem1)

    @pl.when(jnp.logical_and(i < last, i % 2 == 1))
    def _():
        fetch(i + 1, gbuf0, sem0)

    h2 = _from_token_rows(h2_ref, tm).astype(BF16)
    a = _dot(h2, ws1_ref[...])
    hid = (a * _sigmoid(a)) * _dot(h2, ws3_ref[...])
    acc = _dot(hid.astype(BF16), ws2_ref[...])
    cp1.wait()
    cp2.wait()

    def reduce_rows(gbuf, sem):
        _wait_rows(gbuf, sem, STAGE_ROWS)

        def body(tu, carry):
            for u in range(COMBINE_UNROLL):
                t = tu * COMBINE_UNROLL + u
                lo = jnp.zeros((ROW_SUB, LANES), F32)
                hi = jnp.zeros((ROW_SUB, LANES), F32)
                for k in range(TOP_K):
                    w = w_smem[t * TOP_K + k]
                    words = gbuf[pl.ds(pl.multiple_of(ls_smem[t * TOP_K + k], ROW_SUB), ROW_SUB), :]
                    lo = lo + w * lax.bitcast_convert_type(words << 16, F32)
                    hi = hi + w * lax.bitcast_convert_type(words & jnp.uint32(0xFFFF0000), F32)
                acc_rows[pl.ds(pl.multiple_of(t * SUBLANES, SUBLANES), ROW_SUB), :] = lo
                acc_rows[pl.ds(pl.multiple_of(t * SUBLANES, SUBLANES) + ROW_SUB, ROW_SUB), :] = hi
            return carry

        lax.fori_loop(0, tm // COMBINE_UNROLL, body, 0)

    pl.when(i % 2 == 0)(functools.partial(reduce_rows, gbuf0, sem0))
    pl.when(i % 2 == 1)(functools.partial(reduce_rows, gbuf1, sem1))
    routed = jnp.concatenate([acc_rows[pl.ds(s, tm, stride=SUBLANES), :] for s in range(SUBLANES)], axis=-1)
    y = x1_ref[...] + g2_ref[0] * (acc + routed)
    o_ref[...] = y * lax.rsqrt(jnp.mean(y * y, axis=-1, keepdims=True) + EPS) * fg_ref[...]


def _combine(gstart, seg_cnt, seg_off, ls8, w8, x1, h2_rows, g2, ws1, ws3, ws2, fg, ys, rows_per_batch, tm):
    n, d = x1.shape
    per = rows_per_batch // tm
    tok = pl.BlockSpec((tm * TOP_K,), lambda i, *_: (i,))
    full = lambda a: pl.BlockSpec(a.shape, lambda i, *_: (0, 0))
    return pl.pallas_call(
        functools.partial(_combine_kernel, tm=tm),
        out_shape=jax.ShapeDtypeStruct((n, d), F32),
        grid_spec=pltpu.PrefetchScalarGridSpec(
            num_scalar_prefetch=3,
            grid=(n // tm,),
            in_specs=[tok, tok, pl.BlockSpec((tm, d), lambda i, *_: (i, 0)),
                      pl.BlockSpec((tm * ROW_SUB, LANES), lambda i, *_: (i, 0)),
                      pl.BlockSpec((1, 1, d), lambda i, *_: (i // per, 0, 0)),
                      full(ws1), full(ws3), full(ws2), pl.BlockSpec((1, d), lambda i, *_: (0, 0)),
                      pl.BlockSpec(memory_space=pl.ANY)],
            out_specs=pl.BlockSpec((tm, d), lambda i, *_: (i, 0)),
            scratch_shapes=[pltpu.SMEM((tm * TOP_K,), jnp.int32), pltpu.SMEM((tm * TOP_K,), F32),
                            pltpu.VMEM((STAGE_ROWS * ROW_SUB, LANES), U32),
                            pltpu.VMEM((STAGE_ROWS * ROW_SUB, LANES), U32),
                            pltpu.VMEM((tm * SUBLANES, LANES), F32), pltpu.SemaphoreType.DMA,
                            pltpu.SemaphoreType.DMA, pltpu.SemaphoreType.DMA]),
        compiler_params=_params("arbitrary"),
        name="moe_combine_final",
    )(gstart, seg_cnt, seg_off, ls8, w8, x1, h2_rows, g2, ws1, ws3, ws2, fg.reshape(1, d), ys)


def _moe_plan(seg, counts, n_assign):
    cnt = counts.reshape(N_EXPERTS).astype(jnp.int32)
    padded = (cnt + MOE_BLK - 1) // MOE_BLK * MOE_BLK
    pends = jnp.cumsum(padded)
    pstarts = pends - padded
    max_rows = n_assign + seg.shape[0] * N_EXPERTS * (SEG_ALIGN - 1)
    n_blocks = (max_rows + N_EXPERTS * (MOE_BLK - 1) + MOE_BLK - 1) // MOE_BLK
    seg = seg[:, :, :3].astype(jnp.int32)
    gstart = pstarts[None, :] + seg[:, :, 0]
    blk_start = jnp.arange(n_blocks, dtype=jnp.int32) * MOE_BLK
    block_e = jnp.minimum(jnp.sum((blk_start[:, None] >= pends[None, :]).astype(jnp.int32), axis=1),
                          N_EXPERTS - 1).astype(jnp.int32)
    n_used = (pends[-1:] // MOE_BLK).astype(jnp.int32)
    pad = jnp.stack([pstarts + cnt, padded - cnt, jnp.broadcast_to(n_used, (N_EXPERTS,))], axis=0).astype(jnp.int32)
    return gstart, seg[:, :, 1], seg[:, :, 2], pad, block_e, n_used, n_blocks


def _mixer(x, c, ctx, c_ctx, w_ada, b_ada, norm1_g, norm2_g, w_in, s5_lam_re, s5_lam_im, s5_log_dt,
           s5_b_re, s5_b_im, s5_c_re, s5_c_im, s5_d, s5_w_glu, lb, hg_norm_g, p_a, p_b, w_out,
           moe_w_router, moe_b_router):
    b, l, d = x.shape
    lc = ctx.shape[1]
    n = b * l

    c8 = jnp.concatenate([c, c_ctx[None], jnp.zeros((8 - b - 1, d), F32)], axis=0)
    mod = _ada(c8, w_ada, b_ada)
    sh1, sc1, g1, sh2, sc2, g2 = [mod[:b, k * d:(k + 1) * d].reshape(b, 1, d) for k in range(6)]
    csh1, csc1 = mod[b:b + 1, 0:d].reshape(1, 1, d), mod[b:b + 1, d:2 * d].reshape(1, 1, d)

    w_in_b = w_in.astype(BF16)
    z = dict(zip([p[0] for p in _IN_PIECES],
                 _inproj(x.reshape(n, d), sc1, sh1, norm1_g, w_in_b, l, TOK_TILE, True)))
    zc = dict(zip([p[0] for p in _IN_PIECES],
                  _inproj(ctx.reshape(b * lc, d), csc1, csh1, norm1_g, w_in_b, lc, lc, False)))

    cx = lambda t: t.reshape(b, lc, HG_WIDTH)
    lb_row = lb.reshape(1, HG_WIDTH)
    o_f = _hgrn_pass(z["q"], z["ff"], z["i"], cx(zc["ff"]), cx(zc["i"]), lb_row, None, None, reverse=False)
    o_n = _hgrn_pass(z["q"], z["fb"], z["i"], cx(zc["fb"]), cx(zc["i"]), lb_row, o_f,
                     hg_norm_g.reshape(1, HG_DK), reverse=True)

    d_lag, w_s5_in, w_out_f, w_out_b, decay = _s5_weights(s5_lam_re, s5_lam_im, s5_log_dt, s5_b_re, s5_b_im,
                                                          s5_c_re, s5_c_im)
    kc, kl = lc // S5_T, l // S5_T
    u_lat = z["u"].reshape(b, kl, S5_T * S5_WIDTH)
    u_ctx = zc["u"].reshape(b, kc, S5_T * S5_WIDTH)
    rows_in = kl + kc
    e = _s5_in(u_lat, u_ctx, d_lag, w_s5_in, 2 if b % 2 == 0 else 1)
    states = _s5_scan(e, decay, b, rows_in, kl)
    d_row = s5_d.astype(F32).reshape(1, S5_WIDTH)
    y_a = _s5_out(states, w_out_f, w_out_b, e.reshape(b, rows_in, -1), z["u"], d_row, s5_w_glu.astype(BF16))

    return _merge(x.reshape(n, d), y_a, o_n, z["go"], z["ga"], z["gb"], g1, sc2, sh2, norm2_g,
                  p_a.astype(BF16), p_b.astype(BF16), w_out.astype(BF16),
                  moe_w_router.T.astype(BF16), moe_b_router.astype(F32).reshape(N_EXPERTS, 1), l, MOE_TILE) + (g2,)


def kernel(x, c, ctx, c_ctx, w_ada, b_ada, norm1_g, norm2_g, w_in, s5_lam_re, s5_lam_im, s5_log_dt, s5_b_re,
           s5_b_im, s5_c_re, s5_c_im, s5_d, s5_w_glu, hg_lb_logits, hg_norm_g, p_a, p_b, w_out, moe_w_router,
           moe_b_router, moe_w1, moe_w3, moe_w2, moe_ws1, moe_ws3, moe_ws2, final_norm_g):
    b, l, d = x.shape
    n = b * l
    assert w_ada.shape[0] == 1, "single-layer block"
    lb = jnp.cumsum(jax.nn.softmax(hg_lb_logits.astype(F32), axis=0), axis=0)[0]
    x1, h2_rows, ls8, w8, seg, counts, g2 = _mixer(
        x, c, ctx, c_ctx, w_ada[0], b_ada[0], norm1_g[0], norm2_g[0], w_in[0], s5_lam_re[0], s5_lam_im[0],
        s5_log_dt[0], s5_b_re[0], s5_b_im[0], s5_c_re[0], s5_c_im[0], s5_d[0], s5_w_glu[0], lb, hg_norm_g[0],
        p_a[0], p_b[0], w_out[0], moe_w_router[0], moe_b_router[0])
    gstart, seg_cnt, seg_off, pad, block_e, n_used, n_blocks = _moe_plan(seg, counts, n * TOP_K)
    ls_flat, w_flat = ls8.T.reshape(n * TOP_K), w8.T.reshape(n * TOP_K)
    xs = _dispatch(gstart, seg_cnt, seg_off, pad, ls_flat, h2_rows, n_blocks, MOE_TILE)
    ys = _experts(block_e, n_used, xs, moe_w1[0], moe_w3[0], moe_w2[0])
    out = _combine(gstart, seg_cnt, seg_off, ls_flat, w_flat, x1, h2_rows, g2, moe_ws1[0].astype(BF16),
                   moe_ws3[0].astype(BF16), moe_ws2[0].astype(BF16), final_norm_g, ys, l, MOE_TILE)
    return out.reshape(b, l, d)
```

```python
import functools
import math

import jax
import jax.numpy as jnp
from jax import lax
from jax.experimental import pallas as pl
from jax.experimental.pallas import tpu as pltpu

F32 = jnp.float32
BF16 = jnp.bfloat16

GRID_W = 64
S5_WIDTH = 256
S5_GROUP = 16
S5_GROUPS = 16
S5_STATE = 64
HG_HEADS = 6
HG_DK = 128
HG_WIDTH = HG_HEADS * HG_DK
N_EXPERTS = 64
ROUTE_GROUPS = 8
TOPK_GROUPS = 4
TOP_K = 8
ROUTED_SCALE = 2.5
EPS = 1e-6

LANES = 128
SUBLANES = 8

TOK_TILE = 512
S5_T = 16
HG_CHUNK = 64
HG_BATCH = 4
VMEM_LIMIT = 56 * 1024 * 1024

_NT = (((1,), (1,)), ((), ()))
_TN = (((0,), (0,)), ((), ()))


def _params(*sem):
    return pltpu.CompilerParams(dimension_semantics=sem, vmem_limit_bytes=VMEM_LIMIT)


def _dot(a, b):
    return jnp.dot(a, b, preferred_element_type=F32)


def _sigmoid(x):
    return 0.5 * jnp.tanh(0.5 * x) + 0.5


def _ada_kernel(c_ref, w_ref, b_ref, o_ref):
    c = c_ref[...]
    s = (c * _sigmoid(c)).astype(BF16)
    o_ref[...] = _dot(s, w_ref[...].astype(BF16)) + b_ref[...]


def _ada(c8, w_ada, b_ada):
    d, n = w_ada.shape
    tn = 1536
    return pl.pallas_call(
        _ada_kernel,
        out_shape=jax.ShapeDtypeStruct((8, n), F32),
        grid=(n // tn,),
        in_specs=[pl.BlockSpec((8, d), lambda j: (0, 0)),
                  pl.BlockSpec((d, tn), lambda j: (0, j)),
                  pl.BlockSpec((1, tn), lambda j: (0, j))],
        out_specs=pl.BlockSpec((8, tn), lambda j: (0, j)),
        compiler_params=_params("arbitrary"),
        name="ada_mod",
    )(c8, w_ada, b_ada.reshape(1, n))


_IN_PIECES = (("u", 0, 256, BF16), ("q", 256, 768, BF16), ("ff", 1024, 768, BF16),
              ("fb", 1792, 768, BF16), ("i", 2560, 768, BF16), ("go", 3328, 768, BF16),
              ("ga", 4096, 1024, BF16), ("gb", 5120, 1024, BF16))


def _fold_rows(val, buf_a, buf_b):
    t = val.shape[0]
    buf_a[...] = val[:, :LANES]
    buf_b[...] = val[:, LANES:]
    pieces = []
    for s in range(S5_T):
        pieces += [buf_a[pl.ds(s, t // S5_T, stride=S5_T), :], buf_b[pl.ds(s, t // S5_T, stride=S5_T), :]]
    return jnp.concatenate(pieces, axis=-1)


def _unfold_rows(val, buf_a, buf_b):
    r = val.shape[0]
    for s in range(S5_T):
        buf_a[pl.ds(s, r, stride=S5_T), :] = val[:, s * S5_WIDTH:s * S5_WIDTH + LANES]
        buf_b[pl.ds(s, r, stride=S5_T), :] = val[:, s * S5_WIDTH + LANES:(s + 1) * S5_WIDTH]
    return jnp.concatenate([buf_a[...], buf_b[...]], axis=-1)


def _grid_transpose_matrix(tm):
    i = jnp.arange(tm)
    src = (i % (tm // GRID_W)) * GRID_W + i // (tm // GRID_W)
    return (src[:, None] == jnp.arange(tm)[None, :]).astype(BF16)


def _inproj_kernel(x_ref, sc_ref, sh_ref, g_ref, w_ref, p_ref, *o_refs):
    o_refs, (fold_a, fold_b) = o_refs[:len(_IN_PIECES)], o_refs[len(_IN_PIECES):]
    x = x_ref[...]
    y = x * lax.rsqrt(jnp.mean(x * x, axis=-1, keepdims=True) + EPS) * g_ref[...]
    h = (y * (1.0 + sc_ref[0]) + sh_ref[0]).astype(BF16)
    h_cm = None
    for (name, a, wd, _), o_ref in zip(_IN_PIECES, o_refs):
        if name == "u":
            o_ref[...] = _fold_rows(_dot(h, w_ref[:, a:a + wd]), fold_a, fold_b).astype(o_ref.dtype)
        elif len(o_ref.shape) == 2:
            o_ref[...] = _dot(h, w_ref[:, a:a + wd]).astype(o_ref.dtype)
        else:
            if h_cm is None:
                h_cm = _dot(p_ref[...], h).astype(BF16)
            o_ref[0] = _dot(h_cm, w_ref[:, a:a + wd]).astype(o_ref.dtype).reshape(o_ref.shape[1:])


_COLMAJOR_PIECES = ("q", "ff", "fb", "i")


def _inproj(x2d, sc, sh, g, w_bf16, rows_per_mod, tm, colmajor):
    n, d = x2d.shape
    per = rows_per_mod // tm
    mod_map = (lambda i: (i // per, 0, 0)) if sc.shape[0] > 1 else (lambda i: (0, 0, 0))
    shapes, specs = [], []
    for name, _, wd, dt in _IN_PIECES:
        if colmajor and name in _COLMAJOR_PIECES:
            shapes.append(jax.ShapeDtypeStruct((n // rows_per_mod, GRID_W, rows_per_mod // GRID_W, wd), dt))
            specs.append(pl.BlockSpec((1, GRID_W, tm // GRID_W, wd), lambda i: (i // per, 0, i % per, 0)))
        elif name == "u":
            shapes.append(jax.ShapeDtypeStruct((n // S5_T, S5_T * wd), dt))
            specs.append(pl.BlockSpec((tm // S5_T, S5_T * wd), lambda i: (i, 0)))
        else:
            shapes.append(jax.ShapeDtypeStruct((n, wd), dt))
            specs.append(pl.BlockSpec((tm, wd), lambda i: (i, 0)))
    return pl.pallas_call(
        _inproj_kernel,
        out_shape=shapes,
        grid=(n // tm,),
        in_specs=[pl.BlockSpec((tm, d), lambda i: (i, 0)),
                  pl.BlockSpec((1, 1, d), mod_map),
                  pl.BlockSpec((1, 1, d), mod_map),
                  pl.BlockSpec((1, d), lambda i: (0, 0)),
                  pl.BlockSpec(w_bf16.shape, lambda i: (0, 0)),
                  pl.BlockSpec((tm, tm), lambda i: (0, 0))],
        out_specs=specs,
        scratch_shapes=[pltpu.VMEM((tm, LANES), F32), pltpu.VMEM((tm, LANES), F32)],
        compiler_params=_params("arbitrary"),
        name="in_proj",
    )(x2d, sc, sh, g.reshape(1, d), w_bf16, _grid_transpose_matrix(tm))


def _hgrn_gates(zf, lb):
    sig = _sigmoid(zf)
    logf = jnp.log(lb + (1.0 - lb) * sig)
    k = (1.0 - lb) * (1.0 - sig)
    return logf, k


def _chunk_cumsum(cs, logf):
    hi = logf.astype(BF16)
    lo = (logf - hi.astype(F32)).astype(BF16)
    return _dot(cs, hi) + _dot(cs, lo)


def _hgrn_state_step(zf, v, lb, st, cs, reverse):
    logf, k = _hgrn_gates(zf, lb)
    cum = _chunk_cumsum(cs, logf)
    t = 0 if reverse else HG_CHUNK - 1
    total = cum[t:t + 1, :]
    kdec = (k * jnp.exp(total - cum)).astype(BF16)
    st_new = st * jnp.exp(total) + lax.dot_general(v.astype(BF16), kdec, _TN, preferred_element_type=F32)
    return cum, k, st_new


def _hgrn_kernel(*refs, reverse, final, n_ctx_chunks):
    if final:
        q_all, f_all, v_all, cf_ref, cv_ref, lb_ref, of_all, g_ref, o_all, st_ref = refs
    else:
        q_all, f_all, v_all, cf_ref, cv_ref, lb_ref, o_all, st_ref = refs
        of_all = None
    n_batch = q_all.shape[0]
    c_len = HG_CHUNK
    n_rows = q_all.shape[2]
    n_chunks = n_rows // c_len
    row = lax.broadcasted_iota(jnp.int32, (n_rows, n_rows), 0)
    col = lax.broadcasted_iota(jnp.int32, (n_rows, n_rows), 1)
    tri = (col >= row) if reverse else (col <= row)
    same_chunk = None
    for c in range(n_chunks):
        lo, hi = c * c_len, (c + 1) * c_len
        blk = (row >= lo) & (row < hi) & (col >= lo) & (col < hi)
        same_chunk = blk if same_chunk is None else (same_chunk | blk)
    mask = tri & same_chunk
    cs = jnp.where(mask, 1.0, 0.0).astype(BF16)

    @pl.when(pl.program_id(1) == 0)
    def _():
        cs1 = cs[:c_len, :c_len]
        order = range(n_ctx_chunks - 1, -1, -1) if reverse else range(n_ctx_chunks)
        for bi in range(n_batch):
            for h in range(HG_HEADS):
                cols = slice(h * HG_DK, (h + 1) * HG_DK)
                st = jnp.zeros((HG_DK, HG_DK), F32)
                for c in order:
                    rows = slice(c * c_len, (c + 1) * c_len)
                    _, _, st = _hgrn_state_step(cf_ref[bi, rows, cols].astype(F32), cv_ref[bi, rows, cols].astype(F32),
                                                lb_ref[:, cols], st, cs1, reverse)
                st_ref[bi * HG_HEADS + h] = st

    def chunk_rows(x, r):
        return [x[c * c_len + r:c * c_len + r + 1, :] for c in range(n_chunks)]

    def over_chunks(rows):
        return jnp.concatenate([jnp.broadcast_to(r, (c_len, r.shape[1])) for r in rows], axis=0)

    lb = lb_ref[...]
    r_ref = c_len // 2 - 1 if reverse else c_len // 2
    r_tot = 0 if reverse else c_len - 1
    order = range(n_chunks - 1, -1, -1) if reverse else range(n_chunks)
    for bi in range(n_batch):
        q = q_all[bi, 0].astype(F32)
        v = v_all[bi, 0]
        logf, k = _hgrn_gates(f_all[bi, 0].astype(F32), lb)
        cum = _chunk_cumsum(cs, logf)
        ref_rows, tot_rows = chunk_rows(cum, r_ref), chunk_rows(cum, r_tot)
        ref = over_chunks(ref_rows)
        qe = q * jnp.exp(cum - ref)
        ke = k * jnp.exp(ref - cum)
        qi, ki = qe.astype(BF16), ke.astype(BF16)
        q_in = (qe * over_chunks([jnp.exp(r) for r in ref_rows])).astype(BF16)
        kdec = (ke * over_chunks([jnp.exp(t - r) for t, r in zip(tot_rows, ref_rows)])).astype(BF16)
        for h in range(HG_HEADS):
            cols = slice(h * HG_DK, (h + 1) * HG_DK)
            s = lax.dot_general(qi[:, cols], ki[:, cols], _NT, preferred_element_type=F32)
            o_intra = _dot(jnp.where(mask, s, 0.0).astype(BF16), v[:, cols])
            st = st_ref[bi * HG_HEADS + h]
            for c in order:
                rows = slice(c * c_len, (c + 1) * c_len)
                o = o_intra[rows] + lax.dot_general(q_in[rows, cols], st.astype(BF16), _NT,
                                                    preferred_element_type=F32)
                total = cum[c * c_len + r_tot:c * c_len + r_tot + 1, cols]
                st = st * jnp.exp(total) + lax.dot_general(v[rows, cols], kdec[rows, cols], _TN,
                                                           preferred_element_type=F32)
                if final:
                    o = o + of_all[bi, 0, rows, cols].astype(F32)
                    o = o * lax.rsqrt(jnp.mean(o * o, axis=-1, keepdims=True) + EPS) * g_ref[...]
                o_all[bi, 0, rows, cols] = o.astype(o_all.dtype)
            st_ref[bi * HG_HEADS + h] = st


def _hgrn_pass(q, f, v, cf, cv, lb, o_prev, g, *, reverse):
    b, nw, rows, _ = q.shape
    nb = HG_BATCH if b % HG_BATCH == 0 else 1
    final = o_prev is not None
    wmap = (lambda bi, w: (bi, nw - 1 - w, 0, 0)) if reverse else (lambda bi, w: (bi, w, 0, 0))
    blk = pl.BlockSpec((nb, 1, rows, HG_WIDTH), wmap)
    cblk = pl.BlockSpec((nb, cf.shape[1], HG_WIDTH), lambda bi, w: (bi, 0, 0))
    in_specs = [blk, blk, blk, cblk, cblk, pl.BlockSpec((1, HG_WIDTH), lambda bi, w: (0, 0))]
    args = [q, f, v, cf, cv, lb]
    if final:
        in_specs += [blk, pl.BlockSpec((1, HG_DK), lambda bi, w: (0, 0))]
        args += [o_prev, g]
    return pl.pallas_call(
        functools.partial(_hgrn_kernel, reverse=reverse, final=final, n_ctx_chunks=cf.shape[1] // HG_CHUNK),
        out_shape=jax.ShapeDtypeStruct(q.shape, BF16),
        grid=(b // nb, nw),
        in_specs=in_specs,
        out_specs=blk,
        scratch_shapes=[pltpu.VMEM((nb * HG_HEADS, HG_DK, HG_DK), F32)],
        compiler_params=_params("arbitrary", "arbitrary"),
        name="hgrn_bwd" if reverse else "hgrn_fwd",
    )(*args)


def _s5_weights(lam_re, lam_im, log_dt, b_re, b_im, c_re, c_im):
    g, p, cc, t = S5_GROUPS, S5_STATE, S5_GROUP, S5_T
    lre = jnp.minimum(lam_re.astype(F32), -1e-4)
    lim = lam_im.astype(F32)
    dt = jnp.exp(log_dt.astype(F32))[..., None]
    ks = jnp.arange(t + 1, dtype=F32)[:, None, None, None]
    mag = jnp.exp(ks * (lre * dt)[None])
    pw_re = mag * jnp.cos(ks * (lim * dt)[None])
    pw_im = mag * jnp.sin(ks * (lim * dt)[None])
    nr, ni = pw_re[1] - 1.0, pw_im[1]
    den = lre * lre + lim * lim
    cf_re = (nr * lre + ni * lim) / den
    cf_im = (ni * lre - nr * lim) / den
    bb_re = cf_re[..., None] * b_re - cf_im[..., None] * b_im
    bb_im = cf_re[..., None] * b_im + cf_im[..., None] * b_re
    cre, cim = c_re.astype(F32), c_im.astype(F32)
    sw, ns = S5_WIDTH, 2 * g * p
    grp_of_row = jnp.arange(sw)[:, None] // cc

    qr = pw_re[:t].transpose(1, 2, 0, 3)[:, :, :, None, :]
    qi = pw_im[:t].transpose(1, 2, 0, 3)[:, :, :, None, :]
    cp_re = (cre[None, :, None] * qr - cim[None, :, None] * qi).reshape(2, g, t * cc, p)
    cp_im = (cre[None, :, None] * qi + cim[None, :, None] * qr).reshape(2, g, t * cc, p)

    def contract_p(cp, bb):
        return lax.dot_general(cp, bb, (((3,), (2,)), ((0, 1), (0, 1))), precision=lax.Precision.HIGHEST,
                               preferred_element_type=F32)

    kk = (contract_p(cp_re, bb_re) - contract_p(cp_im, bb_im)).reshape(2, g, t, cc, cc).transpose(0, 2, 1, 3, 4)
    kf, kb = kk[0], kk[1]
    kall = jnp.concatenate([kb[:0:-1], (kf[0] + kb[0])[None], kf[1:]], axis=0)
    kt = kall.transpose(0, 1, 3, 2).reshape(2 * t - 1, sw, cc)

    def spread(x, period, reps):
        sel = (jnp.arange(period)[:, None] == (jnp.arange(period * reps)[None, :] % period)).astype(BF16)
        return jnp.dot(x.astype(BF16), sel, preferred_element_type=BF16)

    same = grp_of_row == (jnp.arange(sw)[None, :] // cc)
    d_lag = jnp.where(same[None], spread(kt, cc, g), 0)

    same_in = ((jnp.arange(t * sw)[:, None] // cc) % g) == ((jnp.arange(2 * ns)[None, :] % (g * p)) // p)

    def in_to_state(pre, pim, bre, bim):
        xre = pre[..., None] * bre[None] - pim[..., None] * bim[None]
        xim = pre[..., None] * bim[None] + pim[..., None] * bre[None]
        return [spread(xre.transpose(0, 1, 3, 2).reshape(t * sw, p), p, g),
                spread(xim.transpose(0, 1, 3, 2).reshape(t * sw, p), p, g)]

    w_in = jnp.concatenate(in_to_state(pw_re[t - 1::-1, 0], pw_im[t - 1::-1, 0], bb_re[0], bb_im[0])
                           + in_to_state(pw_re[:t, 1], pw_im[:t, 1], bb_re[1], bb_im[1]), axis=1)
    w_in = jnp.where(same_in, w_in, 0)

    same_out = ((jnp.arange(ns)[:, None] % (g * p)) // p) == ((jnp.arange(t * sw)[None, :] // cc) % g)
    col = jnp.arange(t * sw)
    pick = (jnp.arange(t * cc)[:, None] == ((col // sw) * cc + col % cc)[None, :]).astype(BF16)

    def state_to_out(pre, pim):
        are = cre[None] * pre[:, :, None, :] - cim[None] * pim[:, :, None, :]
        aim = cre[None] * pim[:, :, None, :] + cim[None] * pre[:, :, None, :]
        a = jnp.concatenate([are.transpose(1, 3, 0, 2), -aim.transpose(1, 3, 0, 2)], axis=0)
        a = jnp.dot(a.reshape(ns, t * cc).astype(BF16), pick, preferred_element_type=BF16)
        return jnp.where(same_out, a, 0)

    w_out_f = state_to_out(pw_re[1:, 0], pw_im[1:, 0])
    w_out_b = state_to_out(pw_re[t:0:-1, 1], pw_im[t:0:-1, 1])

    decay = jnp.stack([pw_re[t].reshape(2, g * p), pw_im[t].reshape(2, g * p)], axis=1)
    return d_lag, w_in, w_out_f, w_out_b, decay


def _s5_in_kernel(ul_ref, uc_ref, d_ref, w_ref, o_ref, u_ref):
    j = pl.program_id(1)

    @pl.when(j == 0)
    def _():
        kl, kc = ul_ref.shape[1], uc_ref.shape[1]
        for bi in range(ul_ref.shape[0]):
            u_ref[bi * (kl + kc):bi * (kl + kc) + kl, :] = ul_ref[bi]
            u_ref[bi * (kl + kc) + kl:(bi + 1) * (kl + kc), :] = uc_ref[bi]

    @pl.when(j < S5_T)
    def _():
        acc = _dot(u_ref[:, 0:S5_WIDTH], d_ref[j + S5_T - 1])
        for s in range(1, S5_T):
            acc = acc + _dot(u_ref[:, s * S5_WIDTH:(s + 1) * S5_WIDTH], d_ref[j - s + S5_T - 1])
        o_ref[...] = acc

    @pl.when(j >= S5_T)
    def _():
        o_ref[...] = _dot(u_ref[...], w_ref[...])


def _s5_in(u_lat, u_ctx, d_lag, w_in, nb_tile):
    b, kl, k = u_lat.shape
    kc = u_ctx.shape[1]
    tm = nb_tile * (kl + kc)
    tn = S5_WIDTH
    nj = (k + w_in.shape[1]) // tn
    return pl.pallas_call(
        _s5_in_kernel,
        out_shape=jax.ShapeDtypeStruct((b * (kl + kc), nj * tn), F32),
        grid=(b // nb_tile, nj),
        in_specs=[pl.BlockSpec((nb_tile, kl, k), lambda i, j: (i, 0, 0)),
                  pl.BlockSpec((nb_tile, kc, k), lambda i, j: (i, 0, 0)),
                  pl.BlockSpec(d_lag.shape, lambda i, j: (0, 0, 0)),
                  pl.BlockSpec((k, tn), lambda i, j: (0, jnp.maximum(j - S5_T, 0)))],
        out_specs=pl.BlockSpec((tm, tn), lambda i, j: (i, j)),
        scratch_shapes=[pltpu.VMEM((tm, k), BF16)],
        compiler_params=_params("arbitrary", "arbitrary"),
        name="s5_in",
    )(u_lat, u_ctx, d_lag, w_in)


def _s5_scan_kernel(efr_ref, efi_ref, ebr_ref, ebi_ref, a_ref, hfr_ref, hfi_ref, hbr_ref, hbi_ref,
                    *, nb, rows_in, rows_out):
    dirs = ((efr_ref, efi_ref, hfr_ref, hfi_ref, a_ref[0, 0:1, :], a_ref[0, 1:2, :]),
            (ebr_ref, ebi_ref, hbr_ref, hbi_ref, a_ref[1, 0:1, :], a_ref[1, 1:2, :]))
    zero = jnp.zeros((nb, dirs[0][4].shape[1]), F32)

    def step(srcs, carry, store):
        new = []
        for di, (er_ref, ei_ref, hr_ref, hi_ref, are, aim) in enumerate(dirs):
            hre, him = carry[2 * di], carry[2 * di + 1]
            if store:
                hr_ref[pl.ds(srcs[di], nb, stride=rows_out), :] = hre
                hi_ref[pl.ds(srcs[di], nb, stride=rows_out), :] = him
            ere = er_ref[pl.ds(srcs[di], nb, stride=rows_in), :]
            eim = ei_ref[pl.ds(srcs[di], nb, stride=rows_in), :]
            new += [are * hre - aim * him + ere, are * him + aim * hre + eim]
        return tuple(new)

    n_ctx = rows_in - rows_out
    carry = lax.fori_loop(0, n_ctx, lambda s, c: step((rows_out + s, rows_in - 1 - s), c, False),
                          tuple([zero] * 4))

    def two_steps(s2, c):
        c = step((2 * s2, rows_out - 1 - 2 * s2), c, True)
        return step((2 * s2 + 1, rows_out - 2 - 2 * s2), c, True)

    lax.fori_loop(0, rows_out // 2, two_steps, carry)


def _s5_scan(e, decay, nb, rows_in, rows_out):
    assert rows_out % 2 == 0, "the latent chunks are scanned two per loop iteration"
    tc = LANES
    nsr = S5_GROUPS * S5_STATE
    c0 = (S5_T * S5_WIDTH) // tc
    nt = nsr // tc
    eblk = lambda k: pl.BlockSpec((nb * rows_in, tc), lambda j: (0, c0 + k * nt + j))
    hblk = pl.BlockSpec((nb * rows_out, tc), lambda j: (0, j))
    return pl.pallas_call(
        functools.partial(_s5_scan_kernel, nb=nb, rows_in=rows_in, rows_out=rows_out),
        out_shape=[jax.ShapeDtypeStruct((nb * rows_out, nsr), F32)] * 4,
        grid=(nt,),
        in_specs=[eblk(0), eblk(1), eblk(2), eblk(3), pl.BlockSpec((2, 2, tc), lambda j: (0, 0, j))],
        out_specs=[hblk] * 4,
        compiler_params=_params("arbitrary"),
        name="s5_scan",
    )(e, e, e, e, decay)


def _gelu_tanh(x):
    return 0.5 * x * (1.0 + jnp.tanh(math.sqrt(2.0 / math.pi) * (x + 0.044715 * x * x * x)))


def _s5_out_kernel(hfr_ref, hfi_ref, hbr_ref, hbi_ref, wf_ref, wb_ref, yi_ref, u_ref, d_ref, wg_ref, o_ref):
    nsr = hfr_ref.shape[1]
    y = yi_ref[0] + d_ref[...] * u_ref[...].astype(F32)
    for h_ref, w_ref, r0 in ((hfr_ref, wf_ref, 0), (hfi_ref, wf_ref, nsr), (hbr_ref, wb_ref, 0), (hbi_ref, wb_ref, nsr)):
        y = y + _dot(h_ref[...].astype(BF16), w_ref[r0:r0 + nsr, :])
    y = _gelu_tanh(y)
    gate = _sigmoid(_dot(y.astype(BF16), wg_ref[...]))
    o_ref[...] = (y * gate).astype(o_ref.dtype)


def _s5_out(states, w_out_f, w_out_b, e3, u_rows, d_row, w_glu):
    m, nsr = states[0].shape
    nb = e3.shape[0]
    tm = m // nb
    tn = S5_WIDTH
    st = pl.BlockSpec((tm, nsr), lambda i, j: (i, 0))
    wo = pl.BlockSpec((2 * nsr, tn), lambda i, j: (0, j))
    return pl.pallas_call(
        _s5_out_kernel,
        out_shape=jax.ShapeDtypeStruct((m, S5_T * S5_WIDTH), BF16),
        grid=(nb, S5_T),
        in_specs=[st, st, st, st, wo, wo,
                  pl.BlockSpec((1, tm, tn), lambda i, j: (i, 0, j)),
                  pl.BlockSpec((tm, tn), lambda i, j: (i, j)),
                  pl.BlockSpec((1, tn), lambda i, j: (0, 0)),
                  pl.BlockSpec((tn, tn), lambda i, j: (0, 0))],
        out_specs=pl.BlockSpec((tm, tn), lambda i, j: (i, j)),
        compiler_params=_params("arbitrary", "arbitrary"),
        name="s5_out",
    )(*states, w_out_f, w_out_b, e3, u_rows, d_row, w_glu)


U32 = jnp.uint32
ROW_SUB = 4


def _to_token_rows(ref, val):
    t, d = val.shape
    bits = lax.bitcast_convert_type(val.astype(BF16).astype(F32), U32)
    w = (bits[:, :d // 2] >> 16) | bits[:, d // 2:]
    for s in range(ROW_SUB):
        ref[pl.ds(s, t, stride=ROW_SUB), :] = w[:, s * LANES:(s + 1) * LANES]


def _from_token_rows(ref, t, row0=0):
    w = jnp.concatenate([ref[pl.ds(row0 * ROW_SUB + s, t, stride=ROW_SUB), :] for s in range(ROW_SUB)], axis=-1)
    lo = lax.bitcast_convert_type(w << 16, F32)
    hi = lax.bitcast_convert_type(w & jnp.uint32(0xFFFF0000), F32)
    return jnp.concatenate([lo, hi], axis=-1)


def _route(h2b, wr_ref, br_ref, cnt_ref, ls8_ref, w8_ref, seg_ref):
    tm = h2b.shape[0]
    per_group = N_EXPERTS // ROUTE_GROUPS
    scores = _sigmoid(lax.dot_general(wr_ref[...], h2b, _NT, preferred_element_type=F32))
    biased = scores + br_ref[...]
    neg = -jnp.inf
    sub = lax.broadcasted_iota(jnp.int32, (per_group, tm), 0)
    grp = []
    for gi in range(ROUTE_GROUPS):
        v = biased[gi * per_group:(gi + 1) * per_group, :]
        m1 = jnp.max(v, axis=0, keepdims=True)
        first = jnp.min(jnp.where(v == m1, sub, per_group), axis=0, keepdims=True)
        m2 = jnp.max(jnp.where(sub == first, neg, v), axis=0, keepdims=True)
        grp.append(m1 + m2)
    grp = jnp.concatenate(grp, axis=0)
    gid = lax.broadcasted_iota(jnp.int32, (ROUTE_GROUPS, tm), 0)
    beaten = jnp.zeros((ROUTE_GROUPS, tm), jnp.int32)
    for gj in range(ROUTE_GROUPS):
        r = grp[gj:gj + 1, :]
        beaten = beaten + jnp.where((r > grp) | ((r == grp) & (gj < gid)), 1, 0)
    group_ok = beaten < TOPK_GROUPS
    expert_ok = jnp.concatenate(
        [jnp.broadcast_to(group_ok[gi:gi + 1, :], (per_group, tm)) for gi in range(ROUTE_GROUPS)], axis=0)
    cur = jnp.where(expert_ok, biased, neg)
    eid = lax.broadcasted_iota(jnp.int32, (N_EXPERTS, tm), 0)
    sel = jnp.zeros((N_EXPERTS, tm), F32)
    picks, wts = [], []
    for _ in range(TOP_K):
        m = jnp.max(cur, axis=0, keepdims=True)
        idx = jnp.min(jnp.where(cur == m, eid, N_EXPERTS), axis=0, keepdims=True)
        hit = eid == idx
        picks.append(idx)
        wts.append(jnp.sum(jnp.where(hit, scores, 0.0), axis=0, keepdims=True))
        sel = jnp.where(hit, 1.0, sel)
        cur = jnp.where(hit, neg, cur)
    wsum = wts[0]
    for w in wts[1:]:
        wsum = wsum + w
    selb = sel.astype(BF16)
    ti = lax.broadcasted_iota(jnp.int32, (tm, tm), 0)
    tj = lax.broadcasted_iota(jnp.int32, (tm, tm), 1)
    rank = _dot(selb, jnp.where(ti < tj, 1.0, 0.0).astype(BF16))
    seg_units = jnp.ceil(jnp.sum(sel, axis=1, keepdims=True) * (1.0 / SEG_ALIGN))
    ei = lax.broadcasted_iota(jnp.int32, (N_EXPERTS, N_EXPERTS), 0)
    ej = lax.broadcasted_iota(jnp.int32, (N_EXPERTS, N_EXPERTS), 1)
    units_row = jnp.broadcast_to(seg_units, (N_EXPERTS, LANES)).astype(BF16)
    seg_off = _dot(jnp.where(ej < ei, 1.0, 0.0).astype(BF16), units_row)[:, 0:1] * SEG_ALIGN
    seg_rows = seg_units * SEG_ALIGN
    slot = seg_off + rank
    for k in range(TOP_K):
        w8_ref[k:k + 1, :] = wts[k] / wsum * ROUTED_SCALE
        ls8_ref[k:k + 1, :] = (jnp.sum(jnp.where(eid == picks[k], slot, 0.0), axis=0, keepdims=True)
                               * ROW_SUB).astype(jnp.int32)
    lane = lax.broadcasted_iota(jnp.int32, (N_EXPERTS, LANES), 1)
    seg_ref[0] = jnp.where(lane == 0, cnt_ref[...], jnp.where(lane == 1, seg_rows, seg_off))
    cnt_ref[...] = cnt_ref[...] + seg_rows


def _merge_kernel(x_ref, ya_ref, on_ref, go_ref, ga_ref, gb_ref, g1_ref, sc_ref, sh_ref, n2_ref,
                  pa_ref, pb_ref, wo_ref, wr_ref, br_ref, pt_ref,
                  x1_ref, h2_ref, ls8_ref, w8_ref, seg_ref, cnt_ref, fold_a, fold_b):
    @pl.when(pl.program_id(0) == 0)
    def _():
        cnt_ref[...] = jnp.zeros_like(cnt_ref)

    go = go_ref[...].astype(F32)
    on = _dot(pt_ref[...], on_ref[0].reshape(x_ref.shape[0], HG_WIDTH))
    y_b = (on * (go * _sigmoid(go))).astype(BF16)
    y_a = _unfold_rows(ya_ref[...].astype(F32), fold_a, fold_b).astype(BF16)
    pa = _dot(y_a, pa_ref[...])
    pb = _dot(y_b, pb_ref[...])
    merged = _sigmoid(ga_ref[...].astype(F32)) * pa + _sigmoid(gb_ref[...].astype(F32)) * pb
    x1 = x_ref[...] + g1_ref[0] * _dot(merged.astype(BF16), wo_ref[...])
    x1_ref[...] = x1
    y = x1 * lax.rsqrt(jnp.mean(x1 * x1, axis=-1, keepdims=True) + EPS) * n2_ref[...]
    h2 = y * (1.0 + sc_ref[0]) + sh_ref[0]
    _to_token_rows(h2_ref, h2)
    _route(h2.astype(BF16), wr_ref, br_ref, cnt_ref, ls8_ref, w8_ref, seg_ref)


def _merge(x2d, ya, on, go, ga, gb, g1, sc2, sh2, n2g, pa, pb, wo, wr_t, br, rows_per_batch, tm):
    n, d = x2d.shape
    per = rows_per_batch // tm
    row = lambda wd: pl.BlockSpec((tm, wd), lambda i: (i, 0))
    mod = pl.BlockSpec((1, 1, d), lambda i: (i // per, 0, 0))
    full = lambda a: pl.BlockSpec(a.shape, lambda i: (0, 0))
    tok = pl.BlockSpec((TOP_K, tm), lambda i: (0, i))
    return pl.pallas_call(
        _merge_kernel,
        out_shape=[jax.ShapeDtypeStruct((n, d), F32), jax.ShapeDtypeStruct((n * ROW_SUB, LANES), U32),
                   jax.ShapeDtypeStruct((TOP_K, n), jnp.int32), jax.ShapeDtypeStruct((TOP_K, n), F32),
                   jax.ShapeDtypeStruct((n // tm, N_EXPERTS, LANES), F32),
                   jax.ShapeDtypeStruct((N_EXPERTS, 1), F32)],
        grid=(n // tm,),
        in_specs=[row(d), pl.BlockSpec((tm // S5_T, S5_T * S5_WIDTH), lambda i: (i, 0)),
                  pl.BlockSpec((1, GRID_W, tm // GRID_W, HG_WIDTH), lambda i: (i // per, 0, i % per, 0)),
                  row(HG_WIDTH), row(d), row(d), mod, mod, mod,
                  pl.BlockSpec((1, d), lambda i: (0, 0)), full(pa), full(pb), full(wo), full(wr_t), full(br),
                  pl.BlockSpec((tm, tm), lambda i: (0, 0))],
        out_specs=[row(d), pl.BlockSpec((tm * ROW_SUB, LANES), lambda i: (i, 0)), tok, tok,
                   pl.BlockSpec((1, N_EXPERTS, LANES), lambda i: (i, 0, 0)),
                   pl.BlockSpec((N_EXPERTS, 1), lambda i: (0, 0))],
        scratch_shapes=[pltpu.VMEM((tm, LANES), F32), pltpu.VMEM((tm, LANES), F32)],
        compiler_params=_params("arbitrary"),
        name="merge_out_proj_route",
    )(x2d, ya, on, go, ga, gb, g1, sc2, sh2, n2g.reshape(1, d), pa, pb, wo, wr_t, br,
      _grid_transpose_matrix(tm).T)


MOE_TILE = TOK_TILE
TOKEN_UNROLL = 4
COMBINE_UNROLL = 8
SEG_ALIGN = 8
FILL_ROWS = 512
STAGE_ROWS = MOE_TILE * TOP_K + FILL_ROWS
MOE_BLK = 1024


def _wait_rows(any_ref, sem, n_rows):
    view = any_ref.at[pl.ds(0, n_rows * ROW_SUB)]
    pltpu.make_async_copy(view, view, sem).wait()


def _rows(ref, r0, n):
    return ref.at[pl.ds(pl.multiple_of(r0 * ROW_SUB, ROW_SUB), n * ROW_SUB)]


def _pow2_pieces(n, max_piece, fn, min_piece=1):
    done = 0
    piece = max_piece
    while piece >= min_piece:
        hit = (n & piece) != 0
        pl.when(hit)(functools.partial(fn, done, piece))
        done = done + (n & piece)
        piece //= 2


def _copy_rows(src_ref, src0, dst_ref, dst0, n, max_piece, sem, min_piece=1):
    def piece(off, size):
        pltpu.make_async_copy(_rows(src_ref, src0 + off, size), _rows(dst_ref, dst0 + off, size), sem).start()
    _pow2_pieces(n, max_piece, piece, min_piece)


def _wait_copied_rows(src_ref, dst_ref, n, max_piece, sem):
    def piece(off, size):
        pltpu.make_async_copy(_rows(src_ref, 0, size), _rows(dst_ref, 0, size), sem).wait()
    _pow2_pieces(n, max_piece, piece)


def _copy_tile_segments(i, src_ref, src_tab, dst_ref, dst_tab, cnt_ref, off_ref, fill_src0, fill_dst0, sem):
    def per_expert(e, carry):
        _copy_rows(src_ref, src_tab[i, e], dst_ref, dst_tab[i, e], cnt_ref[i, e], MOE_TILE, sem, SEG_ALIGN)
        return carry

    lax.fori_loop(0, N_EXPERTS, per_expert, 0)
    used = off_ref[i, N_EXPERTS - 1] + cnt_ref[i, N_EXPERTS - 1]
    _copy_rows(src_ref, fill_src0(used), dst_ref, fill_dst0(used), STAGE_ROWS - used, FILL_ROWS, sem, SEG_ALIGN)


def _dispatch_kernel(gs_ref, cnt_ref, off_ref, pad_ref, ls_ref, h2_ref, xs_hbm, ls_smem, stage0, stage1, zbuf,
                     sem0, sem1, lsem, zsem, *, tm, n_blocks):
    i = pl.program_id(0)
    last = pl.num_programs(0) - 1
    cp = pltpu.make_async_copy(ls_ref, ls_smem, lsem)
    cp.start()
    trash0 = n_blocks * MOE_BLK

    @pl.when(i == 0)
    def _():
        stage0[...] = jnp.zeros_like(stage0)
        stage1[...] = jnp.zeros_like(stage1)
        zbuf[...] = jnp.zeros_like(zbuf)
        cpz = pltpu.make_async_copy(zbuf, _rows(xs_hbm, trash0, 2 * FILL_ROWS), zsem)
        cpz.start()
        cpz.wait()

    cp.wait()

    def tile(stage, sem, prev_sem, trash):
        def body(tu, carry):
            for u in range(TOKEN_UNROLL):
                t = tu * TOKEN_UNROLL + u
                row = h2_ref[pl.ds(pl.multiple_of(t * ROW_SUB, ROW_SUB), ROW_SUB), :]
                for k in range(TOP_K):
                    stage[pl.ds(pl.multiple_of(ls_smem[t * TOP_K + k], ROW_SUB), ROW_SUB), :] = row
            return carry

        lax.fori_loop(0, tm // TOKEN_UNROLL, body, 0)
        _copy_tile_segments(i, stage, off_ref, xs_hbm, gs_ref, cnt_ref, off_ref,
                            lambda used: used, lambda used: trash, sem)

        @pl.when(i > 0)
        def _():
            _wait_rows(xs_hbm, prev_sem, STAGE_ROWS)

        @pl.when(i == last)
        def _():
            _wait_rows(xs_hbm, sem, STAGE_ROWS)

    pl.when(i % 2 == 0)(functools.partial(tile, stage0, sem0, sem1, trash0))
    pl.when(i % 2 == 1)(functools.partial(tile, stage1, sem1, sem0, trash0 + FILL_ROWS))

    def zero_pad(e, carry):
        _copy_rows(zbuf, 0, xs_hbm, pad_ref[0, e], pad_ref[1, e], MOE_BLK // 2, zsem)
        return carry

    def wait_pad(e, carry):
        _wait_copied_rows(zbuf, xs_hbm, pad_ref[1, e], MOE_BLK // 2, zsem)
        return carry

    def zero_block(j, carry):
        pltpu.make_async_copy(zbuf, _rows(xs_hbm, j * MOE_BLK, MOE_BLK), zsem).start()
        return carry

    def wait_block(j, carry):
        pltpu.make_async_copy(zbuf, _rows(xs_hbm, 0, MOE_BLK), zsem).wait()
        return carry

    @pl.when(i == 0)
    def _():
        lax.fori_loop(0, N_EXPERTS, zero_pad, 0)
        lax.fori_loop(pad_ref[2, 0], n_blocks, zero_block, 0)

    @pl.when(i == last)
    def _():
        lax.fori_loop(0, N_EXPERTS, wait_pad, 0)
        lax.fori_loop(pad_ref[2, 0], n_blocks, wait_block, 0)


def _dispatch(gstart, seg_cnt, seg_off, pad, ls8, h2_rows, n_blocks, tm):
    n = ls8.shape[0] // TOP_K
    cap = n_blocks * MOE_BLK + 2 * FILL_ROWS
    return pl.pallas_call(
        functools.partial(_dispatch_kernel, tm=tm, n_blocks=n_blocks),
        out_shape=jax.ShapeDtypeStruct((cap * ROW_SUB, LANES), U32),
        grid_spec=pltpu.PrefetchScalarGridSpec(
            num_scalar_prefetch=4,
            grid=(n // tm,),
            in_specs=[pl.BlockSpec((tm * TOP_K,), lambda i, *_: (i,)),
                      pl.BlockSpec((tm * ROW_SUB, LANES), lambda i, *_: (i, 0))],
            out_specs=pl.BlockSpec(memory_space=pl.ANY),
            scratch_shapes=[pltpu.SMEM((tm * TOP_K,), jnp.int32),
                            pltpu.VMEM((STAGE_ROWS * ROW_SUB, LANES), U32),
                            pltpu.VMEM((STAGE_ROWS * ROW_SUB, LANES), U32),
                            pltpu.VMEM((MOE_BLK * ROW_SUB, LANES), U32),
                            pltpu.SemaphoreType.DMA, pltpu.SemaphoreType.DMA, pltpu.SemaphoreType.DMA,
                            pltpu.SemaphoreType.DMA]),
        compiler_params=pltpu.CompilerParams(dimension_semantics=("arbitrary",), vmem_limit_bytes=VMEM_LIMIT,
                                             has_side_effects=True),
        name="moe_dispatch",
    )(gstart, seg_cnt, seg_off, pad, ls8, h2_rows)


def _expert_kernel(be_ref, nu_ref, x_ref, w1_ref, w3_ref, w2_ref, o_ref, w1b, w3b, w2b):
    j = pl.program_id(0)
    e = be_ref[j]
    prev = be_ref[jnp.maximum(j - 1, 0)]
    used = j < nu_ref[0]

    @pl.when(jnp.logical_and(used, jnp.logical_or(j == 0, e != prev)))
    def _():
        w1b[...] = w1_ref[0].astype(BF16)
        w3b[...] = w3_ref[0].astype(BF16)
        w2b[...] = w2_ref[0].astype(BF16)

    @pl.when(used)
    def _():
        x = _from_token_rows(x_ref, MOE_BLK).astype(BF16)
        a = _dot(x, w1b[...])
        hid = (a * _sigmoid(a)) * _dot(x, w3b[...])
        _to_token_rows(o_ref, _dot(hid.astype(BF16), w2b[...]))

    @pl.when(jnp.logical_not(used))
    def _():
        o_ref[...] = jnp.zeros_like(o_ref)


def _experts(block_e, n_used, xs, w1, w3, w2):
    n_blocks = block_e.shape[0]
    d, f = w1.shape[1], w1.shape[2]
    rows = pl.BlockSpec((MOE_BLK * ROW_SUB, LANES), lambda j, be, nu: (j, 0))
    rows_in = pl.BlockSpec((MOE_BLK * ROW_SUB, LANES), lambda j, be, nu: (jnp.minimum(j, nu[0] - 1), 0))
    return pl.pallas_call(
        _expert_kernel,
        out_shape=jax.ShapeDtypeStruct((n_blocks * MOE_BLK * ROW_SUB, LANES), U32),
        grid_spec=pltpu.PrefetchScalarGridSpec(
            num_scalar_prefetch=2,
            grid=(n_blocks,),
            in_specs=[rows_in,
                      pl.BlockSpec((1, d, f), lambda j, be, nu: (be[j], 0, 0)),
                      pl.BlockSpec((1, d, f), lambda j, be, nu: (be[j], 0, 0)),
                      pl.BlockSpec((1, f, d), lambda j, be, nu: (be[j], 0, 0))],
            out_specs=rows,
            scratch_shapes=[pltpu.VMEM((d, f), BF16), pltpu.VMEM((d, f), BF16), pltpu.VMEM((f, d), BF16)]),
        compiler_params=_params("arbitrary"),
        name="moe_experts",
    )(block_e, n_used, xs, w1, w3, w2)


def _combine_kernel(gs_ref, cnt_ref, off_ref, ls_ref, w8_ref, x1_ref, h2_ref, g2_ref, ws1_ref, ws3_ref, ws2_ref,
                    fg_ref, ys_hbm, o_ref, ls_smem, w_smem, gbuf0, gbuf1, acc_rows, sem0, sem1, lsem, *, tm):
    i = pl.program_id(0)
    last = pl.num_programs(0) - 1
    cp1 = pltpu.make_async_copy(ls_ref, ls_smem, lsem)
    cp2 = pltpu.make_async_copy(w8_ref, w_smem, lsem)
    cp1.start()
    cp2.start()

    def fetch(tile, gbuf, sem):
        _copy_tile_segments(tile, ys_hbm, gs_ref, gbuf, off_ref, cnt_ref, off_ref,
                            lambda used: 0, lambda used: used, sem)

    @pl.when(i == 0)
    def _():
        fetch(0, gbuf0, sem0)

    h2 = _from_token_rows(h2_ref, tm).astype(BF16)
    a = _dot(h2, ws1_ref[...])
    hid = (a * _sigmoid(a)) * _dot(h2, ws3_ref[...])
    acc = _dot(hid.astype(BF16), ws2_ref[...])

    @pl.when(jnp.logical_and(i < last, i % 2 == 0))
    def _():
        fetch(i + 1, gbuf1, sem1)

    @pl.when(jnp.logical_and(i < last, i % 2 == 1))
    def _():
        fetch(i + 1, gbuf0, sem0)

    cp1.wait()
    cp2.wait()

    def reduce_rows(gbuf, sem):
        _wait_rows(gbuf, sem, STAGE_ROWS)

        def body(tu, carry):
            for u in range(COMBINE_UNROLL):
                t = tu * COMBINE_UNROLL + u
                lo = jnp.zeros((ROW_SUB, LANES), F32)
                hi = jnp.zeros((ROW_SUB, LANES), F32)
                for k in range(TOP_K):
                    w = w_smem[t * TOP_K + k]
                    words = gbuf[pl.ds(pl.multiple_of(ls_smem[t * TOP_K + k], ROW_SUB), ROW_SUB), :]
                    lo = lo + w * lax.bitcast_convert_type(words << 16, F32)
                    hi = hi + w * lax.bitcast_convert_type(words & jnp.uint32(0xFFFF0000), F32)
                acc_rows[pl.ds(pl.multiple_of(t * SUBLANES, SUBLANES), ROW_SUB), :] = lo
                acc_rows[pl.ds(pl.multiple_of(t * SUBLANES, SUBLANES) + ROW_SUB, ROW_SUB), :] = hi
            return carry

        lax.fori_loop(0, tm // COMBINE_UNROLL, body, 0)

    pl.when(i % 2 == 0)(functools.partial(reduce_rows, gbuf0, sem0))
    pl.when(i % 2 == 1)(functools.partial(reduce_rows, gbuf1, sem1))
    routed = jnp.concatenate([acc_rows[pl.ds(s, tm, stride=SUBLANES), :] for s in range(SUBLANES)], axis=-1)
    y = x1_ref[...] + g2_ref[0] * (acc + routed)
    o_ref[...] = y * lax.rsqrt(jnp.mean(y * y, axis=-1, keepdims=True) + EPS) * fg_ref[...]


def _combine(gstart, seg_cnt, seg_off, ls8, w8, x1, h2_rows, g2, ws1, ws3, ws2, fg, ys, rows_per_batch, tm):
    n, d = x1.shape
    per = rows_per_batch // tm
    tok = pl.BlockSpec((tm * TOP_K,), lambda i, *_: (i,))
    full = lambda a: pl.BlockSpec(a.shape, lambda i, *_: (0, 0))
    return pl.pallas_call(
        functools.partial(_combine_kernel, tm=tm),
        out_shape=jax.ShapeDtypeStruct((n, d), F32),
        grid_spec=pltpu.PrefetchScalarGridSpec(
            num_scalar_prefetch=3,
            grid=(n // tm,),
            in_specs=[tok, tok, pl.BlockSpec((tm, d), lambda i, *_: (i, 0)),
                      pl.BlockSpec((tm * ROW_SUB, LANES), lambda i, *_: (i, 0)),
                      pl.BlockSpec((1, 1, d), lambda i, *_: (i // per, 0, 0)),
                      full(ws1), full(ws3), full(ws2), pl.BlockSpec((1, d), lambda i, *_: (0, 0)),
                      pl.BlockSpec(memory_space=pl.ANY)],
            out_specs=pl.BlockSpec((tm, d), lambda i, *_: (i, 0)),
            scratch_shapes=[pltpu.SMEM((tm * TOP_K,), jnp.int32), pltpu.SMEM((tm * TOP_K,), F32),
                            pltpu.VMEM((STAGE_ROWS * ROW_SUB, LANES), U32),
                            pltpu.VMEM((STAGE_ROWS * ROW_SUB, LANES), U32),
                            pltpu.VMEM((tm * SUBLANES, LANES), F32), pltpu.SemaphoreType.DMA,
                            pltpu.SemaphoreType.DMA, pltpu.SemaphoreType.DMA]),
        compiler_params=_params("arbitrary"),
        name="moe_combine_final",
    )(gstart, seg_cnt, seg_off, ls8, w8, x1, h2_rows, g2, ws1, ws3, ws2, fg.reshape(1, d), ys)


def _moe_plan(seg, counts, n_assign):
    cnt = counts.reshape(N_EXPERTS).astype(jnp.int32)
    padded = (cnt + MOE_BLK - 1) // MOE_BLK * MOE_BLK
    pends = jnp.cumsum(padded)
    pstarts = pends - padded
    max_rows = n_assign + seg.shape[0] * N_EXPERTS * (SEG_ALIGN - 1)
    n_blocks = (max_rows + N_EXPERTS * (MOE_BLK - 1) + MOE_BLK - 1) // MOE_BLK
    seg = seg[:, :, :3].astype(jnp.int32)
    gstart = pstarts[None, :] + seg[:, :, 0]
    blk_start = jnp.arange(n_blocks, dtype=jnp.int32) * MOE_BLK
    block_e = jnp.minimum(jnp.sum((blk_start[:, None] >= pends[None, :]).astype(jnp.int32), axis=1),
                          N_EXPERTS - 1).astype(jnp.int32)
    n_used = (pends[-1:] // MOE_BLK).astype(jnp.int32)
    pad = jnp.stack([pstarts + cnt, padded - cnt, jnp.broadcast_to(n_used, (N_EXPERTS,))], axis=0).astype(jnp.int32)
    return gstart, seg[:, :, 1], seg[:, :, 2], pad, block_e, n_used, n_blocks


def _mixer(x, c, ctx, c_ctx, w_ada, b_ada, norm1_g, norm2_g, w_in, s5_lam_re, s5_lam_im, s5_log_dt,
           s5_b_re, s5_b_im, s5_c_re, s5_c_im, s5_d, s5_w_glu, lb, hg_norm_g, p_a, p_b, w_out,
           moe_w_router, moe_b_router):
    b, l, d = x.shape
    lc = ctx.shape[1]
    n = b * l

    c8 = jnp.concatenate([c, c_ctx[None], jnp.zeros((8 - b - 1, d), F32)], axis=0)
    mod = _ada(c8, w_ada, b_ada)
    sh1, sc1, g1, sh2, sc2, g2 = [mod[:b, k * d:(k + 1) * d].reshape(b, 1, d) for k in range(6)]
    csh1, csc1 = mod[b:b + 1, 0:d].reshape(1, 1, d), mod[b:b + 1, d:2 * d].reshape(1, 1, d)

    w_in_b = w_in.astype(BF16)
    z = dict(zip([p[0] for p in _IN_PIECES],
                 _inproj(x.reshape(n, d), sc1, sh1, norm1_g, w_in_b, l, TOK_TILE, True)))
    zc = dict(zip([p[0] for p in _IN_PIECES],
                  _inproj(ctx.reshape(b * lc, d), csc1, csh1, norm1_g, w_in_b, lc, lc, False)))

    cx = lambda t: t.reshape(b, lc, HG_WIDTH)
    lb_row = lb.reshape(1, HG_WIDTH)
    o_f = _hgrn_pass(z["q"], z["ff"], z["i"], cx(zc["ff"]), cx(zc["i"]), lb_row, None, None, reverse=False)
    o_n = _hgrn_pass(z["q"], z["fb"], z["i"], cx(zc["fb"]), cx(zc["i"]), lb_row, o_f,
                     hg_norm_g.reshape(1, HG_DK), reverse=True)

    d_lag, w_s5_in, w_out_f, w_out_b, decay = _s5_weights(s5_lam_re, s5_lam_im, s5_log_dt, s5_b_re, s5_b_im,
                                                          s5_c_re, s5_c_im)
    kc, kl = lc // S5_T, l // S5_T
    u_lat = z["u"].reshape(b, kl, S5_T * S5_WIDTH)
    u_ctx = zc["u"].reshape(b, kc, S5_T * S5_WIDTH)
    rows_in = kl + kc
    e = _s5_in(u_lat, u_ctx, d_lag, w_s5_in, 2 if b % 2 == 0 else 1)
    states = _s5_scan(e, decay, b, rows_in, kl)
    d_row = s5_d.astype(F32).reshape(1, S5_WIDTH)
    y_a = _s5_out(states, w_out_f, w_out_b, e.reshape(b, rows_in, -1), z["u"], d_row, s5_w_glu.astype(BF16))

    return _merge(x.reshape(n, d), y_a, o_n, z["go"], z["ga"], z["gb"], g1, sc2, sh2, norm2_g,
                  p_a.astype(BF16), p_b.astype(BF16), w_out.astype(BF16),
                  moe_w_router.T.astype(BF16), moe_b_router.astype(F32).reshape(N_EXPERTS, 1), l, MOE_TILE) + (g2,)


def kernel(x, c, ctx, c_ctx, w_ada, b_ada, norm1_g, norm2_g, w_in, s5_lam_re, s5_lam_im, s5_log_dt, s5_b_re,
           s5_b_im, s5_c_re, s5_c_im, s5_d, s5_w_glu, hg_lb_logits, hg_norm_g, p_a, p_b, w_out, moe_w_router,
           moe_b_router, moe_w1, moe_w3, moe_w2, moe_ws1, moe_ws3, moe_ws2, final_norm_g):
    b, l, d = x.shape
    n = b * l
    assert w_ada.shape[0] == 1, "single-layer block"
    lb = jnp.cumsum(jax.nn.softmax(hg_lb_logits.astype(F32), axis=0), axis=0)[0]
    x1, h2_rows, ls8, w8, seg, counts, g2 = _mixer(
        x, c, ctx, c_ctx, w_ada[0], b_ada[0], norm1_g[0], norm2_g[0], w_in[0], s5_lam_re[0], s5_lam_im[0],
        s5_log_dt[0], s5_b_re[0], s5_b_im[0], s5_c_re[0], s5_c_im[0], s5_d[0], s5_w_glu[0], lb, hg_norm_g[0],
        p_a[0], p_b[0], w_out[0], moe_w_router[0], moe_b_router[0])
    gstart, seg_cnt, seg_off, pad, block_e, n_used, n_blocks = _moe_plan(seg, counts, n * TOP_K)
    ls_flat, w_flat = ls8.T.reshape(n * TOP_K), w8.T.reshape(n * TOP_K)
    xs = _dispatch(gstart, seg_cnt, seg_off, pad, ls_flat, h2_rows, n_blocks, MOE_TILE)
    ys = _experts(block_e, n_used, xs, moe_w1[0], moe_w3[0], moe_w2[0])
    out = _combine(gstart, seg_cnt, seg_off, ls_flat, w_flat, x1, h2_rows, g2, moe_ws1[0].astype(BF16),
                   moe_ws3[0].astype(BF16), moe_ws2[0].astype(BF16), final_norm_g, ys, l, MOE_TILE)
    return out.reshape(b, l, d)
```
